```python
import math
import jax
import jax.numpy as jnp
from jax import lax
import numpy as np

D_MODEL = 1024
BATCH = 2
SEQ = 8192
DEPTH = 1

N_HEADS = 4
HEAD_DIM = 64
V_DIM = 2 * HEAD_DIM
ATTN_WIDTH = N_HEADS * V_DIM
Q_BLOCK = 128
LAMBDA_STD = 0.1
ALIBI_SLOPES = tuple(2.0 ** (-8.0 * (h + 1) / N_HEADS) for h in range(N_HEADS))

LRU_WIDTH = D_MODEL // 2
LRU_BLOCKS = 8
LRU_BLOCK_W = LRU_WIDTH // LRU_BLOCKS
CONV_W = 4
LRU_C = 8.0
A_MIN = 0.9
A_MAX = 0.999

N_GROUPS = 4
EXPERTS_PER_GROUP = 8
N_EXPERTS = N_GROUPS * EXPERTS_PER_GROUP
TOP_K = 2
D_EXPERT = D_MODEL // 2
MOE_BLOCK = 128

NORM_EPS = 1e-6

QK_COLS = N_HEADS * 2 * HEAD_DIM
SPLITS = (QK_COLS, 2 * QK_COLS, 2 * QK_COLS + ATTN_WIDTH,
          2 * QK_COLS + ATTN_WIDTH + LRU_WIDTH, 2 * QK_COLS + ATTN_WIDTH + 2 * LRU_WIDTH)
IN_COLS = SPLITS[-1] + 2 * D_MODEL

kernel_name = 'hybrid_diffattn_rglru_hmoe'


def rms_norm(x, g):
    xf = x.astype(jnp.float32)
    y = xf * lax.rsqrt(jnp.mean(xf * xf, axis=-1, keepdims=True) + NORM_EPS)
    return (y * g.astype(jnp.float32)).astype(x.dtype)


def diff_attention(q, k, v, lam, subln_g, lam_init):
    B, S = q.shape[0], q.shape[1]
    n_blocks = S // Q_BLOCK
    slopes = jnp.asarray(ALIBI_SLOPES, dtype=jnp.float32)
    scale = HEAD_DIM ** -0.5
    k_pos = jnp.arange(S)

    def one_block(i):
        start = i * Q_BLOCK
        qb = lax.dynamic_slice_in_dim(q, start, Q_BLOCK, axis=1)
        s = jnp.einsum('bqhmd,bkhmd->bhmqk', qb, k).astype(jnp.float32) * scale
        dist = (start + jnp.arange(Q_BLOCK))[:, None] - k_pos[None, :]
        alibi = -slopes[:, None, None] * dist.astype(jnp.float32)[None]
        s = jnp.where((dist >= 0)[None, None, None], s + alibi[None, :, None], -jnp.inf)
        p = jax.nn.softmax(s, axis=-1)
        w = p[:, :, 0] - lam * p[:, :, 1]
        return jnp.einsum('bhqk,bkhv->bqhv', w.astype(v.dtype), v)

    o = lax.map(one_block, jnp.arange(n_blocks))
    o = jnp.moveaxis(o, 0, 1).reshape(B, S, N_HEADS, V_DIM)
    o = rms_norm(o, subln_g) * (1.0 - lam_init)
    return o.reshape(B, S, ATTN_WIDTH)


def causal_conv(x, w, b):
    C = x.shape[-1]
    y = lax.conv_general_dilated(x, w[:, None, :].astype(x.dtype), window_strides=(1,),
                                 padding=[(CONV_W - 1, 0)],
                                 dimension_numbers=('NWC', 'WIO', 'NWC'),
                                 feature_group_count=C)
    return y + b.astype(x.dtype)


def rg_lru(x, w_r, b_r, w_i, b_i, lru_lambda):
    B, S, C = x.shape
    xb = x.reshape(B, S, LRU_BLOCKS, LRU_BLOCK_W)
    r = jax.nn.sigmoid(jnp.einsum('bsnc,ncd->bsnd', xb, w_r).reshape(B, S, C) + b_r)
    i = jax.nn.sigmoid(jnp.einsum('bsnc,ncd->bsnd', xb, w_i).reshape(B, S, C) + b_i)
    log_a = -LRU_C * r.astype(jnp.float32) * jax.nn.softplus(-lru_lambda.astype(jnp.float32))
    a = jnp.exp(log_a)
    mult = jnp.sqrt(-jnp.expm1(2.0 * log_a))
    mult = jnp.where(jnp.arange(S)[None, :, None] == 0, 1.0, mult)
    u = (x * i).astype(jnp.float32) * mult

    def combine(c1, c2):
        a1, b1 = c1
        a2, b2 = c2
        return a1 * a2, a2 * b1 + b2

    _, h = lax.associative_scan(combine, (a, u), axis=1)
    return h.astype(x.dtype)


def hier_moe(h, w_group, w_expert_router, w_gate, w_up, w_down):
    B, S, D = h.shape
    N = B * S
    t = h.reshape(N, D)
    g_logits = (t @ w_group).astype(jnp.float32)
    g_prob = jax.nn.softmax(g_logits, axis=-1)
    g_idx = jnp.argmax(g_logits, axis=-1)
    g_w = jnp.take_along_axis(g_prob, g_idx[:, None], axis=1)[:, 0]
    e_logits_all = jnp.einsum('nd,gde->nge', t, w_expert_router)
    e_logits = jnp.take_along_axis(e_logits_all, g_idx[:, None, None], axis=1)[:, 0].astype(jnp.float32)
    e_prob = jax.nn.softmax(e_logits, axis=-1)
    top_p, top_i = lax.top_k(e_prob, TOP_K)
    top_p = top_p / jnp.sum(top_p, axis=-1, keepdims=True)
    weights = g_w[:, None] * top_p
    expert_id = g_idx[:, None] * EXPERTS_PER_GROUP + top_i

    A = N * TOP_K
    flat_e = expert_id.reshape(A).astype(jnp.int32)
    flat_tok = jnp.repeat(jnp.arange(N, dtype=jnp.int32), TOP_K)
    flat_w = weights.reshape(A)
    order = jnp.argsort(flat_e)
    sorted_e = flat_e[order]
    counts = jnp.bincount(flat_e, length=N_EXPERTS)
    starts = jnp.cumsum(counts) - counts
    padded = (counts + MOE_BLOCK - 1) // MOE_BLOCK * MOE_BLOCK
    pad_ends = jnp.cumsum(padded)
    pad_starts = pad_ends - padded
    dest = pad_starts[sorted_e] + (jnp.arange(A) - starts[sorted_e])
    n_blocks = (A + N_EXPERTS * (MOE_BLOCK - 1) + MOE_BLOCK - 1) // MOE_BLOCK
    P = n_blocks * MOE_BLOCK
    row_tok = jnp.full((P,), N, dtype=jnp.int32).at[dest].set(flat_tok[order])
    row_w = jnp.zeros((P,), jnp.float32).at[dest].set(flat_w[order])
    block_e = jnp.minimum(jnp.searchsorted(pad_ends, jnp.arange(n_blocks) * MOE_BLOCK, side='right'),
                          N_EXPERTS - 1)
    t_pad = jnp.concatenate([t, jnp.zeros((1, D), t.dtype)], axis=0)
    xs = t_pad[row_tok].reshape(n_blocks, MOE_BLOCK, D)

    def expert_block(args):
        xb, e = args
        return (jax.nn.silu(xb @ w_gate[e]) * (xb @ w_up[e])) @ w_down[e]

    ys = lax.map(expert_block, (xs, block_e)).reshape(P, D)
    out = jnp.zeros((N + 1, D), jnp.float32).at[row_tok].add(ys.astype(jnp.float32) * row_w[:, None])[:N]
    return out.reshape(B, S, D).astype(h.dtype)


def setup_inputs(seed: int = 0) -> dict:
    key = jax.random.key(seed)
    ks = jax.random.split(key, 24)
    f32 = jnp.float32
    nrm = lambda k, shape, s: jax.random.normal(k, shape, f32) * s
    a0 = jax.random.uniform(ks[11], (DEPTH, LRU_WIDTH), f32, A_MIN, A_MAX)
    sig = a0 ** (1.0 / LRU_C)
    lru_lambda = jnp.log(sig) - jnp.log1p(-sig)
    return {
        'x': jax.random.normal(ks[0], (BATCH, SEQ, D_MODEL), f32),
        'norm_mix_g': 1.0 + nrm(ks[1], (DEPTH, D_MODEL), 0.02),
        'w_in': nrm(ks[2], (DEPTH, D_MODEL, IN_COLS), D_MODEL ** -0.5),
        'lambda_qk': nrm(ks[3], (DEPTH, 2, 2, HEAD_DIM), LAMBDA_STD),
        'subln_g': 1.0 + nrm(ks[4], (DEPTH, V_DIM), 0.02),
        'conv_w': nrm(ks[5], (DEPTH, CONV_W, LRU_WIDTH), CONV_W ** -0.5),
        'conv_b': nrm(ks[6], (DEPTH, LRU_WIDTH), 0.01),
        'w_r': nrm(ks[7], (DEPTH, LRU_BLOCKS, LRU_BLOCK_W, LRU_BLOCK_W), LRU_BLOCK_W ** -0.5),
        'b_r': nrm(ks[8], (DEPTH, LRU_WIDTH), 0.01),
        'w_i': nrm(ks[9], (DEPTH, LRU_BLOCKS, LRU_BLOCK_W, LRU_BLOCK_W), LRU_BLOCK_W ** -0.5),
        'b_i': nrm(ks[10], (DEPTH, LRU_WIDTH), 0.01),
        'lru_lambda': lru_lambda,
        'w_o_attn': nrm(ks[12], (DEPTH, ATTN_WIDTH, D_MODEL), ATTN_WIDTH ** -0.5),
        'w_o_lru': nrm(ks[13], (DEPTH, LRU_WIDTH, D_MODEL), LRU_WIDTH ** -0.5),
        'w_out': nrm(ks[14], (DEPTH, D_MODEL, D_MODEL), D_MODEL ** -0.5),
        'norm_ffn_g': 1.0 + nrm(ks[15], (DEPTH, D_MODEL), 0.02),
        'w_group': nrm(ks[16], (DEPTH, D_MODEL, N_GROUPS), D_MODEL ** -0.5),
        'w_expert_router': nrm(ks[17], (DEPTH, N_GROUPS, D_MODEL, EXPERTS_PER_GROUP), D_MODEL ** -0.5),
        'w_gate': nrm(ks[18], (DEPTH, N_EXPERTS, D_MODEL, D_EXPERT), D_MODEL ** -0.5),
        'w_up': nrm(ks[19], (DEPTH, N_EXPERTS, D_MODEL, D_EXPERT), D_MODEL ** -0.5),
        'w_down': nrm(ks[20], (DEPTH, N_EXPERTS, D_EXPERT, D_MODEL), D_EXPERT ** -0.5),
        'final_norm_g': 1.0 + nrm(ks[21], (D_MODEL,), 0.02),
    }


def reference(x, norm_mix_g, w_in, lambda_qk, subln_g, conv_w, conv_b, w_r, b_r, w_i, b_i,
              lru_lambda, w_o_attn, w_o_lru, w_out, norm_ffn_g, w_group, w_expert_router,
              w_gate, w_up, w_down, final_norm_g):
    B, S, D = x.shape
    for l in range(DEPTH):
        lam_init = 0.8 - 0.6 * math.exp(-0.3 * l)
        h = rms_norm(x, norm_mix_g[l])
        proj = h @ w_in[l]
        q, k, v, xr, yr, gate_logits = jnp.split(proj, SPLITS + (IN_COLS - 2 * D_MODEL + 0,), axis=-1)[:6] if False else jnp.split(proj, SPLITS, axis=-1)
        q = q.reshape(B, S, N_HEADS, 2, HEAD_DIM)
        k = k.reshape(B, S, N_HEADS, 2, HEAD_DIM)
        v = v.reshape(B, S, N_HEADS, V_DIM)
        lp = lambda_qk[l].astype(jnp.float32)
        lam = jnp.exp(jnp.sum(lp[0, 0] * lp[0, 1])) - jnp.exp(jnp.sum(lp[1, 0] * lp[1, 1])) + lam_init
        attn = diff_attention(q, k, v, lam, subln_g[l], lam_init)
        xr = causal_conv(xr, conv_w[l], conv_b[l])
        lru = rg_lru(xr, w_r[l], b_r[l], w_i[l], b_i[l], lru_lambda[l]) * jax.nn.gelu(yr)
        gates = jax.nn.sigmoid(gate_logits)
        g_attn = gates[..., :D_MODEL]
        g_lru = gates[..., D_MODEL:]
        merged = g_attn * (attn @ w_o_attn[l]) + g_lru * (lru @ w_o_lru[l])
        x = x + merged @ w_out[l]
        x = x + hier_moe(rms_norm(x, norm_ffn_g[l]), w_group[l], w_expert_router[l],
                         w_gate[l], w_up[l], w_down[l])
    return rms_norm(x, final_norm_g)
```

```python
import functools
import math

import jax
import jax.numpy as jnp
from jax import lax
from jax.experimental import pallas as pl
from jax.experimental.pallas import tpu as pltpu

F32 = jnp.float32
BF16 = jnp.bfloat16

D_MODEL = 1024
N_HEADS = 4
HEAD_DIM = 64
V_DIM = 2 * HEAD_DIM
ATTN_WIDTH = N_HEADS * V_DIM
LRU_WIDTH = D_MODEL // 2
LRU_BLOCKS = 8
CONV_W = 4
LRU_C = 8.0
N_GROUPS = 4
EXPERTS_PER_GROUP = 8
N_EXPERTS = N_GROUPS * EXPERTS_PER_GROUP
TOP_K = 2
D_EXPERT = D_MODEL // 2
MOE_BLOCK = 128
NORM_EPS = 1e-6
LAM_INIT = 0.8 - 0.6 * math.exp(-0.3 * 0)

QK_COLS = N_HEADS * 2 * HEAD_DIM
PROJ_COLS = 2 * QK_COLS + ATTN_WIDTH + 2 * LRU_WIDTH
ROUTE_LANES = 128
NEG_BIG = -1e30

SEQ_TILE = 512
LRU_TILE = 512
COMBINE_TILE = 256
VMEM_LIMIT = 48 * 1024 * 1024


def _rms(x, g):
    return x * lax.rsqrt(jnp.mean(x * x, axis=-1, keepdims=True) + NORM_EPS) * g


def _dot(a, b):
    return jnp.dot(a, b, preferred_element_type=F32)


def _inproj_kernel(x_ref, g_ref, w_ref, qT_ref, k_ref, vT_ref, xr_ref, yr_ref):
    hb = _rms(x_ref[...], g_ref[...]).astype(BF16)

    def proj(lo, hi):
        return _dot(hb, w_ref[:, lo:hi])

    q = proj(0, QK_COLS) * (HEAD_DIM ** -0.5)
    for h in range(N_HEADS):
        qT_ref[h] = q[:, h * V_DIM:(h + 1) * V_DIM].T.astype(BF16)
    k_ref[...] = proj(QK_COLS, 2 * QK_COLS).astype(BF16)
    v = proj(2 * QK_COLS, 2 * QK_COLS + ATTN_WIDTH)
    for h in range(N_HEADS):
        vT_ref[h] = v[:, h * V_DIM:(h + 1) * V_DIM].T.astype(BF16)
    c0 = 2 * QK_COLS + ATTN_WIDTH
    xr_ref[...] = proj(c0, c0 + LRU_WIDTH)
    yr_ref[...] = proj(c0 + LRU_WIDTH, c0 + 2 * LRU_WIDTH)


def _inproj(x2, g, w, B, S):
    N = B * S
    tm = SEQ_TILE
    nt = S // tm
    tile5 = pl.BlockSpec((None, N_HEADS, None, V_DIM, tm), lambda i: (i // nt, 0, i % nt, 0, 0))
    rows = lambda c: pl.BlockSpec((tm, c), lambda i: (i, 0))
    return pl.pallas_call(
        _inproj_kernel,
        grid=(N // tm,),
        in_specs=[rows(D_MODEL),
                  pl.BlockSpec((1, D_MODEL), lambda i: (0, 0)),
                  pl.BlockSpec((D_MODEL, PROJ_COLS), lambda i: (0, 0))],
        out_specs=[tile5, rows(QK_COLS), tile5, rows(LRU_WIDTH), rows(LRU_WIDTH)],
        out_shape=[jax.ShapeDtypeStruct((B, N_HEADS, nt, V_DIM, tm), BF16),
                   jax.ShapeDtypeStruct((N, QK_COLS), BF16),
                   jax.ShapeDtypeStruct((B, N_HEADS, nt, V_DIM, tm), BF16),
                   jax.ShapeDtypeStruct((N, LRU_WIDTH), F32),
                   jax.ShapeDtypeStruct((N, LRU_WIDTH), F32)],
        compiler_params=pltpu.CompilerParams(dimension_semantics=("parallel",),
                                             vmem_limit_bytes=VMEM_LIMIT),
        name="inproj",
    )(x2, g, w)


def _attn_kernel(qT_ref, k_ref, vT_ref, lam_ref, g_ref, o_ref, bias_ref, acc_ref):
    t = SEQ_TILE
    h = pl.program_id(1)
    i = pl.program_id(2)
    slope = jnp.where(h == 0, 0.25, jnp.where(h == 1, 0.0625, jnp.where(h == 2, 0.015625, 0.00390625))).astype(F32)

    @pl.when(i == 0)
    def _():
        r = lax.broadcasted_iota(jnp.int32, (t, t), 0)
        c = lax.broadcasted_iota(jnp.int32, (t, t), 1)
        b = r.astype(F32) * slope
        bias_ref[0] = b
        bias_ref[1] = jnp.where(r <= c, b, NEG_BIG)

    qf = qT_ref[...].astype(F32)
    row = lax.broadcasted_iota(jnp.int32, qf.shape, 0)
    qs = (jnp.where(row < HEAD_DIM, qf, 0.0).astype(BF16), jnp.where(row >= HEAD_DIM, qf, 0.0).astype(BF16))
    acc_ref[...] = jnp.zeros_like(acc_ref)

    def body(j, carry):
        kt = k_ref[j]
        vt = vT_ref[j]
        bias = bias_ref[(j == i).astype(jnp.int32)]
        cj = slope * (j * t).astype(F32)
        new = []
        for mi in range(2):
            m, l = carry[2 * mi], carry[2 * mi + 1]
            s = _dot(kt, qs[mi]) + bias
            m_new = jnp.maximum(m, jnp.max(s, axis=0, keepdims=True) + cj)
            alpha = jnp.exp(m - m_new)
            p = jnp.exp(s - (m_new - cj))
            l_new = alpha * l + jnp.sum(p, axis=0, keepdims=True)
            acc_ref[mi] = alpha * acc_ref[mi] + _dot(vt, p.astype(BF16))
            new += [m_new, l_new]
        return tuple(new)

    m_init = jnp.full((1, t), NEG_BIG, F32)
    l_init = jnp.zeros((1, t), F32)
    _, l0, _, l1 = lax.fori_loop(0, i + 1, body, (m_init, l_init, m_init, l_init))

    lp = lam_ref[...]
    s1 = jnp.sum(lp[0:1] * lp[1:2], axis=-1, keepdims=True)
    s2 = jnp.sum(lp[2:3] * lp[3:4], axis=-1, keepdims=True)
    lam = jnp.exp(s1) - jnp.exp(s2) + LAM_INIT
    oT = acc_ref[0] * (1.0 / l0) - lam * (acc_ref[1] * (1.0 / l1))
    o = _rms(oT.T, g_ref[...]) * (1.0 - LAM_INIT)
    o_ref[...] = o.astype(BF16)


def _attn(qT, k4, vT, lam, g, B, S):
    t = SEQ_TILE
    nt = S // t
    return pl.pallas_call(
        _attn_kernel,
        grid=(B, N_HEADS, nt),
        in_specs=[pl.BlockSpec((None, None, None, V_DIM, t), lambda b, h, i: (b, h, i, 0, 0)),
                  pl.BlockSpec((None, nt, t, V_DIM), lambda b, h, i: (b, 0, 0, h)),
                  pl.BlockSpec((None, None, nt, V_DIM, t), lambda b, h, i: (b, h, 0, 0, 0)),
                  pl.BlockSpec((4, HEAD_DIM), lambda b, h, i: (0, 0)),
                  pl.BlockSpec((1, V_DIM), lambda b, h, i: (0, 0))],
        out_specs=pl.BlockSpec((None, t, V_DIM), lambda b, h, i: (b, i, h)),
        out_shape=jax.ShapeDtypeStruct((B, S, ATTN_WIDTH), BF16),
        scratch_shapes=[pltpu.VMEM((2, t, t), F32), pltpu.VMEM((2, V_DIM, t), F32)],
        compiler_params=pltpu.CompilerParams(dimension_semantics=("parallel", "arbitrary", "arbitrary"),
                                             vmem_limit_bytes=VMEM_LIMIT),
        name="diff_attn",
    )(qT, k4, vT, lam, g)


def _lru_kernel(xr_ref, yr_ref, cw_ref, cb_ref, wr_ref, wi_ref, br_ref, bi_ref, lam_ref, o_ref, xbuf, hc):
    T = LRU_TILE
    ti = pl.program_id(1)

    @pl.when(ti == 0)
    def _():
        xbuf[0:8] = jnp.zeros((8, LRU_WIDTH), F32)
        hc[...] = jnp.zeros_like(hc)

    x = xr_ref[...]
    xbuf[8:8 + T] = x
    cw = cw_ref[...]
    xc = cb_ref[...] + cw[3:4] * x
    for j in range(CONV_W - 1):
        xc = xc + cw[j:j + 1] * xbuf[5 + j:5 + j + T]
    xbuf[0:8] = x[T - 8:T]

    xb = xc.astype(BF16)
    r = jax.nn.sigmoid(_dot(xb, wr_ref[...]) + br_ref[...])
    ig = jax.nn.sigmoid(_dot(xb, wi_ref[...]) + bi_ref[...])
    z = -lam_ref[...]
    softplus = jnp.maximum(z, 0.0) + jnp.log1p(jnp.exp(-jnp.abs(z)))
    la = -LRU_C * r * softplus
    a = jnp.exp(la)
    mult = jnp.sqrt(-jnp.tanh(la) * (a * a + 1.0))
    row = lax.broadcasted_iota(jnp.int32, (T, LRU_WIDTH), 0)
    mult = jnp.where((row == 0) & (ti == 0), 1.0, mult)
    u = (xc * ig) * mult

    d = 1
    while d < T:
        valid = row >= d
        u = jnp.where(valid, a * pltpu.roll(u, d, 0) + u, u)
        a = jnp.where(valid, a * pltpu.roll(a, d, 0), a)
        d *= 2
    hfull = u + a * hc[...]
    hc[...] = hfull[T - 1:T]
    y = yr_ref[...]
    gelu = 0.5 * y * (1.0 + jnp.tanh(0.7978845608028654 * (y + 0.044715 * (y * y * y))))
    o_ref[...] = (hfull * gelu).astype(BF16)


def _lru(xr, yr, cw, cb, wr, wi, br, bi, lam, B, S):
    T = LRU_TILE
    seq = pl.BlockSpec((None, T, LRU_WIDTH), lambda b, t: (b, t, 0))
    full = lambda r, c: pl.BlockSpec((r, c), lambda b, t: (0, 0))
    return pl.pallas_call(
        _lru_kernel,
        grid=(B, S // T),
        in_specs=[seq, seq, full(CONV_W, LRU_WIDTH), full(1, LRU_WIDTH), full(LRU_WIDTH, LRU_WIDTH),
                  full(LRU_WIDTH, LRU_WIDTH), full(1, LRU_WIDTH), full(1, LRU_WIDTH), full(1, LRU_WIDTH)],
        out_specs=seq,
        out_shape=jax.ShapeDtypeStruct((B, S, LRU_WIDTH), BF16),
        scratch_shapes=[pltpu.VMEM((T + 8, LRU_WIDTH), F32), pltpu.VMEM((1, LRU_WIDTH), F32)],
        compiler_params=pltpu.CompilerParams(dimension_semantics=("arbitrary", "arbitrary"),
                                             vmem_limit_bytes=VMEM_LIMIT),
        name="rg_lru",
    )(xr, yr, cw, cb, wr, wi, br, bi, lam)


def _merge_kernel(x_ref, attn_ref, lru_ref, g1_ref, wg_ref, woa_ref, wol_ref, wout_ref, g2_ref, wrt_ref,
                  x1_ref, h2_ref, route_ref):
    x = x_ref[...]
    hb = _rms(x, g1_ref[...]).astype(BF16)
    gates = jax.nn.sigmoid(_dot(hb, wg_ref[...]))
    merged = gates[:, :D_MODEL] * _dot(attn_ref[...], woa_ref[...]) + gates[:, D_MODEL:] * _dot(lru_ref[...], wol_ref[...])
    x1 = x + _dot(merged.astype(BF16), wout_ref[...])
    x1_ref[...] = x1
    h2 = _rms(x1, g2_ref[...])
    h2_ref[...] = h2

    logits = jnp.dot(h2, wrt_ref[...], preferred_element_type=F32, precision=lax.Precision.HIGHEST)
    lane = lax.broadcasted_iota(jnp.int32, logits.shape, 1)
    big = jnp.int32(1 << 20)

    def first_argmax(v):
        m = jnp.max(v, axis=-1, keepdims=True)
        return m, jnp.min(jnp.where(v == m, lane, big), axis=-1, keepdims=True)

    gmask = lane < N_GROUPS
    gmax, gidx = first_argmax(jnp.where(gmask, logits, -jnp.inf))
    gsum = jnp.sum(jnp.where(gmask, jnp.exp(logits - gmax), 0.0), axis=-1, keepdims=True)
    g_w = 1.0 / gsum
    lo = N_GROUPS + EXPERTS_PER_GROUP * gidx
    el = jnp.where((lane >= lo) & (lane < lo + EXPERTS_PER_GROUP), logits, -jnp.inf)
    m1, i1 = first_argmax(el)
    m2, i2 = first_argmax(jnp.where(lane == i1, -jnp.inf, el))
    rr = jnp.exp(m2 - m1)
    w1 = g_w / (1.0 + rr)
    w2 = g_w * rr / (1.0 + rr)
    route = jnp.where(lane == 0, (i1 - N_GROUPS).astype(F32),
                      jnp.where(lane == 1, (i2 - N_GROUPS).astype(F32),
                                jnp.where(lane == 2, w1, jnp.where(lane == 3, w2, 0.0))))
    route_ref[...] = route


def _merge(x2, attn, lru, g1, wg, woa, wol, wout, g2, wrt):
    N = x2.shape[0]
    tm = SEQ_TILE
    rows = lambda c: pl.BlockSpec((tm, c), lambda i: (i, 0))
    full = lambda r, c: pl.BlockSpec((r, c), lambda i: (0, 0))
    return pl.pallas_call(
        _merge_kernel,
        grid=(N // tm,),
        in_specs=[rows(D_MODEL), rows(ATTN_WIDTH), rows(LRU_WIDTH), full(1, D_MODEL), full(D_MODEL, 2 * D_MODEL),
                  full(ATTN_WIDTH, D_MODEL), full(LRU_WIDTH, D_MODEL), full(D_MODEL, D_MODEL), full(1, D_MODEL),
                  full(D_MODEL, ROUTE_LANES)],
        out_specs=[rows(D_MODEL), rows(D_MODEL), rows(ROUTE_LANES)],
        out_shape=[jax.ShapeDtypeStruct((N, D_MODEL), F32), jax.ShapeDtypeStruct((N, D_MODEL), F32),
                   jax.ShapeDtypeStruct((N, ROUTE_LANES), F32)],
        compiler_params=pltpu.CompilerParams(dimension_semantics=("parallel",), vmem_limit_bytes=VMEM_LIMIT),
        name="merge_route",
    )(x2, attn, lru, g1, wg, woa, wol, wout, g2, wrt)


def _row_gather(idx_ref, n_rows, src_hbm, dst, sem):
    def start(r, _):
        pltpu.make_async_copy(src_hbm.at[pl.ds(idx_ref[0, 0, r], 1)], dst.at[pl.ds(r, 1)], sem).start()
        return 0
    lax.fori_loop(0, n_rows, start, 0, unroll=8)


def _row_gather_wait(n_rows, src_hbm, dst, sem):
    def wait(r, _):
        pltpu.make_async_copy(src_hbm.at[pl.ds(0, 1)], dst.at[pl.ds(r, 1)], sem).wait()
        return 0
    lax.fori_loop(0, n_rows, wait, 0, unroll=8)


def _moe_kernel(be_ref, tok_ref, tokn_ref, w_ref, h2_hbm, wg_ref, wu_ref, wd_ref, y_ref, xbuf, sem):
    del be_ref
    i = pl.program_id(0)
    n = pl.num_programs(0)
    slot = lax.rem(i, 2)

    @pl.when(i == 0)
    def _():
        _row_gather(tok_ref, MOE_BLOCK, h2_hbm, xbuf.at[0], sem.at[0])

    @pl.when(i + 1 < n)
    def _():
        _row_gather(tokn_ref, MOE_BLOCK, h2_hbm, xbuf.at[1 - slot], sem.at[1 - slot])

    _row_gather_wait(MOE_BLOCK, h2_hbm, xbuf.at[slot], sem.at[slot])
    xb = xbuf[slot].astype(BF16)
    g = _dot(xb, wg_ref[...])
    u = _dot(xb, wu_ref[...])
    hmid = (g * jax.nn.sigmoid(g)) * u
    y_ref[...] = _dot(hmid.astype(BF16), wd_ref[...]) * w_ref[...]


def _moe(block_e, row_tok3, row_w, h2, wg, wu, wd):
    nb = block_e.shape[0]
    idx = lambda f: pl.BlockSpec((1, 1, MOE_BLOCK), f, memory_space=pltpu.SMEM)
    grid_spec = pltpu.PrefetchScalarGridSpec(
        num_scalar_prefetch=1,
        grid=(nb,),
        in_specs=[idx(lambda i, be: (i, 0, 0)),
                  idx(lambda i, be: (jnp.minimum(i + 1, nb - 1), 0, 0)),
                  pl.BlockSpec((MOE_BLOCK, 1), lambda i, be: (i, 0)),
                  pl.BlockSpec(memory_space=pl.ANY),
                  pl.BlockSpec((None, D_MODEL, D_EXPERT), lambda i, be: (be[i], 0, 0)),
                  pl.BlockSpec((None, D_MODEL, D_EXPERT), lambda i, be: (be[i], 0, 0)),
                  pl.BlockSpec((None, D_EXPERT, D_MODEL), lambda i, be: (be[i], 0, 0))],
        out_specs=pl.BlockSpec((MOE_BLOCK, D_MODEL), lambda i, be: (i, 0)),
        scratch_shapes=[pltpu.VMEM((2, MOE_BLOCK, D_MODEL), F32), pltpu.SemaphoreType.DMA((2,))],
    )
    return pl.pallas_call(
        _moe_kernel,
        grid_spec=grid_spec,
        out_shape=jax.ShapeDtypeStruct((nb * MOE_BLOCK, D_MODEL), F32),
        compiler_params=pltpu.CompilerParams(dimension_semantics=("arbitrary",), vmem_limit_bytes=VMEM_LIMIT),
        name="moe_experts",
    )(block_e, row_tok3, row_tok3, row_w, h2, wg, wu, wd)


def _combine_kernel(pos_ref, posn_ref, x1_ref, g_ref, ys_hbm, o_ref, ybuf, sem):
    tc = COMBINE_TILE
    i = pl.program_id(0)
    n = pl.num_programs(0)
    slot = lax.rem(i, 2)

    @pl.when(i == 0)
    def _():
        _row_gather(pos_ref, 2 * tc, ys_hbm, ybuf.at[0], sem.at[0])

    @pl.when(i + 1 < n)
    def _():
        _row_gather(posn_ref, 2 * tc, ys_hbm, ybuf.at[1 - slot], sem.at[1 - slot])

    _row_gather_wait(2 * tc, ys_hbm, ybuf.at[slot], sem.at[slot])
    y = ybuf[slot]
    o_ref[...] = _rms(x1_ref[...] + y[:tc] + y[tc:], g_ref[...])


def _combine(pos3, x1, g, ys):
    N = x1.shape[0]
    tc = COMBINE_TILE
    nt = N // tc
    idx = lambda f: pl.BlockSpec((1, 1, 2 * tc), f, memory_space=pltpu.SMEM)
    return pl.pallas_call(
        _combine_kernel,
        grid=(nt,),
        in_specs=[idx(lambda i: (i, 0, 0)),
                  idx(lambda i: (jnp.minimum(i + 1, nt - 1), 0, 0)),
                  pl.BlockSpec((tc, D_MODEL), lambda i: (i, 0)),
                  pl.BlockSpec((1, D_MODEL), lambda i: (0, 0)),
                  pl.BlockSpec(memory_space=pl.ANY)],
        out_specs=pl.BlockSpec((tc, D_MODEL), lambda i: (i, 0)),
        out_shape=jax.ShapeDtypeStruct((N, D_MODEL), F32),
        scratch_shapes=[pltpu.VMEM((2, 2 * tc, D_MODEL), F32), pltpu.SemaphoreType.DMA((2,))],
        compiler_params=pltpu.CompilerParams(dimension_semantics=("arbitrary",), vmem_limit_bytes=VMEM_LIMIT),
        name="combine_norm",
    )(pos3, pos3, x1, g, ys)


def _block_diag(w):
    nb, c, _ = w.shape
    eye = jnp.eye(nb, dtype=w.dtype)
    return (eye[:, None, :, None] * w[:, :, None, :]).reshape(nb * c, nb * c)


def _dispatch_plan(route, N):
    A = N * TOP_K
    flat_e = route[:, 0:TOP_K].astype(jnp.int32).reshape(A)
    flat_w = route[:, TOP_K:2 * TOP_K].reshape(A)
    flat_tok = jnp.arange(A, dtype=jnp.int32) // TOP_K
    onehot = (flat_e[:, None] == jnp.arange(N_EXPERTS, dtype=jnp.int32)[None, :]).astype(jnp.int32)
    csum = jnp.cumsum(onehot, axis=0)
    rank = jnp.sum(csum * onehot, axis=1) - 1
    counts = csum[-1]
    padded = (counts + MOE_BLOCK - 1) // MOE_BLOCK * MOE_BLOCK
    pad_ends = jnp.cumsum(padded)
    pad_starts = pad_ends - padded
    pos = pad_starts[flat_e] + rank
    n_blocks = (A + N_EXPERTS * (MOE_BLOCK - 1) + MOE_BLOCK - 1) // MOE_BLOCK
    P = n_blocks * MOE_BLOCK
    row_tok = jnp.zeros((P,), jnp.int32).at[pos].set(flat_tok)
    row_w = jnp.zeros((P,), F32).at[pos].set(flat_w)
    block_e = jnp.minimum(jnp.searchsorted(pad_ends, jnp.arange(n_blocks, dtype=jnp.int32) * MOE_BLOCK, side='right'),
                          N_EXPERTS - 1).astype(jnp.int32)
    return pos, row_tok, row_w, block_e, n_blocks


def kernel(x, norm_mix_g, w_in, lambda_qk, subln_g, conv_w, conv_b, w_r, b_r, w_i, b_i, lru_lambda, w_o_attn, w_o_lru, w_out, norm_ffn_g, w_group, w_expert_router, w_gate, w_up, w_down, final_norm_g):
    B, S, D = x.shape
    N = B * S
    nt = S // SEQ_TILE
    depth = norm_mix_g.shape[0]
    assert depth == 1 and D == D_MODEL and S % SEQ_TILE == 0 and S % LRU_TILE == 0 and N % COMBINE_TILE == 0
    l = 0
    row = lambda v: v.reshape(1, -1).astype(F32)

    x2 = x.reshape(N, D)
    w_in_l = w_in[l]
    qT, k, vT, xr, yr = _inproj(x2, row(norm_mix_g[l]), w_in_l[:, :PROJ_COLS].astype(BF16), B, S)

    attn = _attn(qT, k.reshape(B, nt, SEQ_TILE, QK_COLS), vT, lambda_qk[l].reshape(4, HEAD_DIM).astype(F32),
                 row(subln_g[l]), B, S)

    lru = _lru(xr.reshape(B, S, LRU_WIDTH), yr.reshape(B, S, LRU_WIDTH), conv_w[l].astype(F32), row(conv_b[l]),
               _block_diag(w_r[l]).astype(BF16), _block_diag(w_i[l]).astype(BF16), row(b_r[l]), row(b_i[l]),
               row(lru_lambda[l]), B, S)

    w_route = jnp.concatenate(
        [w_group[l], jnp.transpose(w_expert_router[l], (1, 0, 2)).reshape(D, N_EXPERTS),
         jnp.zeros((D, ROUTE_LANES - N_GROUPS - N_EXPERTS), F32)], axis=1).astype(F32)
    x1, h2, route = _merge(x2, attn.reshape(N, ATTN_WIDTH), lru.reshape(N, LRU_WIDTH), row(norm_mix_g[l]),
                           w_in_l[:, PROJ_COLS:].astype(BF16), w_o_attn[l].astype(BF16), w_o_lru[l].astype(BF16),
                           w_out[l].astype(BF16), row(norm_ffn_g[l]), w_route)

    pos, row_tok, row_w, block_e, n_blocks = _dispatch_plan(route, N)
    ys = _moe(block_e, row_tok.reshape(n_blocks, 1, MOE_BLOCK), row_w.reshape(-1, 1), h2,
              w_gate[l].astype(BF16), w_up[l].astype(BF16), w_down[l].astype(BF16))

    tc = COMBINE_TILE
    pos3 = jnp.transpose(pos.reshape(N // tc, tc, TOP_K), (0, 2, 1)).reshape(N // tc, 1, TOP_K * tc)
    out = _combine(pos3, x1, row(final_norm_g), ys)
    return out.reshape(B, S, D)
```

```python
import functools
import math

import jax
import jax.numpy as jnp
from jax import lax
from jax.experimental import pallas as pl
from jax.experimental.pallas import tpu as pltpu

F32 = jnp.float32
BF16 = jnp.bfloat16

D_MODEL = 1024
N_HEADS = 4
HEAD_DIM = 64
V_DIM = 2 * HEAD_DIM
ATTN_WIDTH = N_HEADS * V_DIM
LRU_WIDTH = D_MODEL // 2
LRU_BLOCKS = 8
CONV_W = 4
LRU_C = 8.0
N_GROUPS = 4
EXPERTS_PER_GROUP = 8
N_EXPERTS = N_GROUPS * EXPERTS_PER_GROUP
TOP_K = 2
D_EXPERT = D_MODEL // 2
MOE_BLOCK = 128
NORM_EPS = 1e-6
LAM_INIT = 0.8 - 0.6 * math.exp(-0.3 * 0)

QK_COLS = N_HEADS * 2 * HEAD_DIM
PROJ_COLS = 2 * QK_COLS + ATTN_WIDTH + 2 * LRU_WIDTH
ROUTE_LANES = 128
NEG_BIG = -1e30
LOG2E = math.log2(math.e)
ALIBI_SLOPES = tuple(2.0 ** (-8.0 * (h + 1) / N_HEADS) for h in range(N_HEADS))

SEQ_TILE = 512
LRU_TILE = 512
COMBINE_TILE = 256
VMEM_LIMIT = 48 * 1024 * 1024


def _rms(x, g):
    return x * lax.rsqrt(jnp.mean(x * x, axis=-1, keepdims=True) + NORM_EPS) * g


def _dot(a, b):
    return jnp.dot(a, b, preferred_element_type=F32)


def _inproj_kernel(x_ref, g_ref, w_ref, qT_ref, k_ref, vT_ref, xr_ref, yr_ref):
    hb = _rms(x_ref[...], g_ref[...]).astype(BF16)

    def proj(lo, hi):
        return _dot(hb, w_ref[:, lo:hi])

    q = proj(0, QK_COLS) * (HEAD_DIM ** -0.5 * LOG2E)
    for h in range(N_HEADS):
        qT_ref[h] = q[:, h * V_DIM:(h + 1) * V_DIM].T.astype(BF16)
    k_ref[...] = proj(QK_COLS, 2 * QK_COLS).astype(BF16)
    v = proj(2 * QK_COLS, 2 * QK_COLS + ATTN_WIDTH)
    for h in range(N_HEADS):
        vT_ref[h] = v[:, h * V_DIM:(h + 1) * V_DIM].T.astype(BF16)
    c0 = 2 * QK_COLS + ATTN_WIDTH
    xr_ref[...] = proj(c0, c0 + LRU_WIDTH)
    yr_ref[...] = proj(c0 + LRU_WIDTH, c0 + 2 * LRU_WIDTH)


def _inproj(x2, g, w, B, S):
    N = B * S
    tm = SEQ_TILE
    nt = S // tm
    tile5 = pl.BlockSpec((None, N_HEADS, None, V_DIM, tm), lambda i: (i // nt, 0, i % nt, 0, 0))
    rows = lambda c: pl.BlockSpec((tm, c), lambda i: (i, 0))
    return pl.pallas_call(
        _inproj_kernel,
        grid=(N // tm,),
        in_specs=[rows(D_MODEL),
                  pl.BlockSpec((1, D_MODEL), lambda i: (0, 0)),
                  pl.BlockSpec((D_MODEL, PROJ_COLS), lambda i: (0, 0))],
        out_specs=[tile5, rows(QK_COLS), tile5, rows(LRU_WIDTH), rows(LRU_WIDTH)],
        out_shape=[jax.ShapeDtypeStruct((B, N_HEADS, nt, V_DIM, tm), BF16),
                   jax.ShapeDtypeStruct((N, QK_COLS), BF16),
                   jax.ShapeDtypeStruct((B, N_HEADS, nt, V_DIM, tm), BF16),
                   jax.ShapeDtypeStruct((N, LRU_WIDTH), F32),
                   jax.ShapeDtypeStruct((N, LRU_WIDTH), F32)],
        compiler_params=pltpu.CompilerParams(dimension_semantics=("parallel",),
                                             vmem_limit_bytes=VMEM_LIMIT),
        name="inproj",
    )(x2, g, w)


def _attn_kernel(qT_ref, k_ref, vT_ref, lam_ref, g_ref, o_ref, bias_ref, acc_ref, sa_ref, sb_ref, pa_ref, pb_ref):
    t = SEQ_TILE
    h = pl.program_id(1)
    i = pl.program_id(2)
    slope = LOG2E * jnp.where(h == 0, ALIBI_SLOPES[0], jnp.where(h == 1, ALIBI_SLOPES[1],
                              jnp.where(h == 2, ALIBI_SLOPES[2], ALIBI_SLOPES[3]))).astype(F32)

    @pl.when(i == 0)
    def _():
        r = lax.broadcasted_iota(jnp.int32, (t, t), 0)
        c = lax.broadcasted_iota(jnp.int32, (t, t), 1)
        b = r.astype(F32) * slope
        bias_ref[0] = b
        bias_ref[1] = jnp.where(r <= c, b, NEG_BIG)
        bias_ref[2] = jnp.full((t, t), NEG_BIG, F32)

    qf = qT_ref[...].astype(F32)
    row = lax.broadcasted_iota(jnp.int32, qf.shape, 0)
    qs = (jnp.where(row < HEAD_DIM, qf, 0.0).astype(BF16), jnp.where(row >= HEAD_DIM, qf, 0.0).astype(BF16))
    acc_ref[...] = jnp.zeros_like(acc_ref)
    pb_ref[...] = jnp.zeros_like(pb_ref)

    def stage_q(tau, s_ref):
        kt = k_ref[jnp.minimum(tau, i)]
        bias = bias_ref[jnp.where(tau < i, 0, jnp.where(tau == i, 1, 2))]
        tile_max = []
        for mi in range(2):
            s = _dot(kt, qs[mi]) + bias
            s_ref[mi] = s
            tile_max.append(jnp.max(s, axis=0, keepdims=True))
        return tuple(tile_max)

    def stage_s(tau, s_ref, p_ref, ml, tile_max):
        cj = slope * (jnp.minimum(tau, i) * t).astype(F32)
        out = []
        for mi in range(2):
            m, l = ml[2 * mi], ml[2 * mi + 1]
            m_new = jnp.maximum(m, tile_max[mi] + cj)
            alpha = jnp.exp2(m - m_new)
            p = jnp.exp2(s_ref[mi] - (m_new - cj))
            p_ref[mi] = p.astype(BF16)
            out += [m_new, alpha * l + jnp.sum(p, axis=0, keepdims=True), alpha]
        return (out[0], out[1], out[3], out[4]), (out[2], out[5])

    def stage_v(tau, p_ref, alphas):
        vt = vT_ref[jnp.clip(tau, 0, i)]
        for mi in range(2):
            acc_ref[mi] = alphas[mi] * acc_ref[mi] + _dot(vt, p_ref[mi])

    def body(jj, carry):
        ml, alphas, tmax = carry[:4], carry[4:6], carry[6:]
        tau = 2 * jj
        stage_v(tau - 1, pb_ref, alphas)
        ml, alphas = stage_s(tau, sa_ref, pa_ref, ml, tmax)
        tmax = stage_q(tau + 1, sb_ref)
        stage_v(tau, pa_ref, alphas)
        ml, alphas = stage_s(tau + 1, sb_ref, pb_ref, ml, tmax)
        tmax = stage_q(tau + 2, sa_ref)
        return ml + alphas + tmax

    m_init = jnp.full((1, t), NEG_BIG, F32)
    l_init = jnp.zeros((1, t), F32)
    one = jnp.ones((1, t), F32)
    tmax0 = stage_q(0, sa_ref)
    n_pairs = jnp.right_shift(i + 2, 1)
    fin = lax.fori_loop(0, n_pairs, body, (m_init, l_init, m_init, l_init, one, one) + tmax0)
    stage_v(2 * n_pairs - 1, pb_ref, fin[4:6])
    l0, l1 = fin[1], fin[3]

    lp = lam_ref[...]
    s1 = jnp.sum(lp[0:1] * lp[1:2], axis=-1, keepdims=True)
    s2 = jnp.sum(lp[2:3] * lp[3:4], axis=-1, keepdims=True)
    lam = jnp.exp(s1) - jnp.exp(s2) + LAM_INIT
    oT = acc_ref[0] * (1.0 / l0) - lam * (acc_ref[1] * (1.0 / l1))
    o = _rms(oT.T, g_ref[...]) * (1.0 - LAM_INIT)
    o_ref[...] = o.astype(BF16)


def _attn(qT, k4, vT, lam, g, B, S):
    t = SEQ_TILE
    nt = S // t
    return pl.pallas_call(
        _attn_kernel,
        grid=(B, N_HEADS, nt),
        in_specs=[pl.BlockSpec((None, None, None, V_DIM, t), lambda b, h, i: (b, h, i, 0, 0)),
                  pl.BlockSpec((None, nt, t, V_DIM), lambda b, h, i: (b, 0, 0, h)),
                  pl.BlockSpec((None, None, nt, V_DIM, t), lambda b, h, i: (b, h, 0, 0, 0)),
                  pl.BlockSpec((4, HEAD_DIM), lambda b, h, i: (0, 0)),
                  pl.BlockSpec((1, V_DIM), lambda b, h, i: (0, 0))],
        out_specs=pl.BlockSpec((None, t, V_DIM), lambda b, h, i: (b, i, h)),
        out_shape=jax.ShapeDtypeStruct((B, S, ATTN_WIDTH), BF16),
        scratch_shapes=[pltpu.VMEM((3, t, t), F32), pltpu.VMEM((2, V_DIM, t), F32),
                        pltpu.VMEM((2, t, t), F32), pltpu.VMEM((2, t, t), F32),
                        pltpu.VMEM((2, t, t), BF16), pltpu.VMEM((2, t, t), BF16)],
        compiler_params=pltpu.CompilerParams(dimension_semantics=("parallel", "arbitrary", "arbitrary"),
                                             vmem_limit_bytes=VMEM_LIMIT),
        name="diff_attn",
    )(qT, k4, vT, lam, g)


def _lru_kernel(xr_ref, yr_ref, cw_ref, cb_ref, wr_ref, wi_ref, br_ref, bi_ref, lam_ref, o_ref, xbuf, hc):
    T = LRU_TILE
    ti = pl.program_id(1)

    @pl.when(ti == 0)
    def _():
        xbuf[0:8] = jnp.zeros((8, LRU_WIDTH), F32)
        hc[...] = jnp.zeros_like(hc)

    x = xr_ref[...]
    xbuf[8:8 + T] = x
    cw = cw_ref[...]
    xc = cb_ref[...] + cw[3:4] * x
    for j in range(CONV_W - 1):
        xc = xc + cw[j:j + 1] * xbuf[5 + j:5 + j + T]
    xbuf[0:8] = x[T - 8:T]

    xb = xc.astype(BF16)
    r = jax.nn.sigmoid(_dot(xb, wr_ref[...]) + br_ref[...])
    ig = jax.nn.sigmoid(_dot(xb, wi_ref[...]) + bi_ref[...])
    z = -lam_ref[...]
    softplus = jnp.maximum(z, 0.0) + jnp.log1p(jnp.exp(-jnp.abs(z)))
    la = -LRU_C * r * softplus
    a = jnp.exp(la)
    mult = jnp.sqrt(-jnp.tanh(la) * (a * a + 1.0))
    row = lax.broadcasted_iota(jnp.int32, (T, LRU_WIDTH), 0)
    mult = jnp.where((row == 0) & (ti == 0), 1.0, mult)
    u = (xc * ig) * mult

    d = 1
    while d < T:
        valid = row >= d
        u = jnp.where(valid, a * pltpu.roll(u, d, 0) + u, u)
        a = jnp.where(valid, a * pltpu.roll(a, d, 0), a)
        d *= 2
    hfull = u + a * hc[...]
    hc[...] = hfull[T - 1:T]
    y = yr_ref[...]
    gelu = 0.5 * y * (1.0 + jnp.tanh(0.7978845608028654 * (y + 0.044715 * (y * y * y))))
    o_ref[...] = (hfull * gelu).astype(BF16)


def _lru(xr, yr, cw, cb, wr, wi, br, bi, lam, B, S):
    T = LRU_TILE
    seq = pl.BlockSpec((None, T, LRU_WIDTH), lambda b, t: (b, t, 0))
    full = lambda r, c: pl.BlockSpec((r, c), lambda b, t: (0, 0))
    return pl.pallas_call(
        _lru_kernel,
        grid=(B, S // T),
        in_specs=[seq, seq, full(CONV_W, LRU_WIDTH), full(1, LRU_WIDTH), full(LRU_WIDTH, LRU_WIDTH),
                  full(LRU_WIDTH, LRU_WIDTH), full(1, LRU_WIDTH), full(1, LRU_WIDTH), full(1, LRU_WIDTH)],
        out_specs=seq,
        out_shape=jax.ShapeDtypeStruct((B, S, LRU_WIDTH), BF16),
        scratch_shapes=[pltpu.VMEM((T + 8, LRU_WIDTH), F32), pltpu.VMEM((1, LRU_WIDTH), F32)],
        compiler_params=pltpu.CompilerParams(dimension_semantics=("arbitrary", "arbitrary"),
                                             vmem_limit_bytes=VMEM_LIMIT),
        name="rg_lru",
    )(xr, yr, cw, cb, wr, wi, br, bi, lam)


def _merge_kernel(x_ref, attn_ref, lru_ref, g1_ref, wg_ref, woa_ref, wol_ref, wout_ref, g2_ref, wrt_ref,
                  x1_ref, h2_ref, route_ref):
    x = x_ref[...]
    hb = _rms(x, g1_ref[...]).astype(BF16)
    gates = jax.nn.sigmoid(_dot(hb, wg_ref[...]))
    merged = gates[:, :D_MODEL] * _dot(attn_ref[...], woa_ref[...]) + gates[:, D_MODEL:] * _dot(lru_ref[...], wol_ref[...])
    x1 = x + _dot(merged.astype(BF16), wout_ref[...])
    x1_ref[...] = x1
    h2 = _rms(x1, g2_ref[...])
    h2_ref[...] = h2

    logits = jnp.dot(h2, wrt_ref[...], preferred_element_type=F32, precision=lax.Precision.HIGHEST)
    lane = lax.broadcasted_iota(jnp.int32, logits.shape, 1)
    big = jnp.int32(1 << 20)

    def first_argmax(v):
        m = jnp.max(v, axis=-1, keepdims=True)
        return m, jnp.min(jnp.where(v == m, lane, big), axis=-1, keepdims=True)

    gmask = lane < N_GROUPS
    gmax, gidx = first_argmax(jnp.where(gmask, logits, -jnp.inf))
    gsum = jnp.sum(jnp.where(gmask, jnp.exp(logits - gmax), 0.0), axis=-1, keepdims=True)
    g_w = 1.0 / gsum
    lo = N_GROUPS + EXPERTS_PER_GROUP * gidx
    el = jnp.where((lane >= lo) & (lane < lo + EXPERTS_PER_GROUP), logits, -jnp.inf)
    m1, i1 = first_argmax(el)
    m2, i2 = first_argmax(jnp.where(lane == i1, -jnp.inf, el))
    rr = jnp.exp(m2 - m1)
    w1 = g_w / (1.0 + rr)
    w2 = g_w * rr / (1.0 + rr)
    route = jnp.where(lane == 0, (i1 - N_GROUPS).astype(F32),
                      jnp.where(lane == 1, (i2 - N_GROUPS).astype(F32),
                                jnp.where(lane == 2, w1, jnp.where(lane == 3, w2, 0.0))))
    route_ref[...] = route


def _merge(x2, attn, lru, g1, wg, woa, wol, wout, g2, wrt):
    N = x2.shape[0]
    tm = SEQ_TILE
    rows = lambda c: pl.BlockSpec((tm, c), lambda i: (i, 0))
    full = lambda r, c: pl.BlockSpec((r, c), lambda i: (0, 0))
    return pl.pallas_call(
        _merge_kernel,
        grid=(N // tm,),
        in_specs=[rows(D_MODEL), rows(ATTN_WIDTH), rows(LRU_WIDTH), full(1, D_MODEL), full(D_MODEL, 2 * D_MODEL),
                  full(ATTN_WIDTH, D_MODEL), full(LRU_WIDTH, D_MODEL), full(D_MODEL, D_MODEL), full(1, D_MODEL),
                  full(D_MODEL, ROUTE_LANES)],
        out_specs=[rows(D_MODEL), rows(D_MODEL), rows(ROUTE_LANES)],
        out_shape=[jax.ShapeDtypeStruct((N, D_MODEL), F32), jax.ShapeDtypeStruct((N, D_MODEL), F32),
                   jax.ShapeDtypeStruct((N, ROUTE_LANES), F32)],
        compiler_params=pltpu.CompilerParams(dimension_semantics=("parallel",), vmem_limit_bytes=VMEM_LIMIT),
        name="merge_route",
    )(x2, attn, lru, g1, wg, woa, wol, wout, g2, wrt)


def _row_gather(idx_ref, n_rows, src_hbm, dst, sem):
    def start(r, _):
        pltpu.make_async_copy(src_hbm.at[pl.ds(idx_ref[0, 0, r], 1)], dst.at[pl.ds(r, 1)], sem).start()
        return 0
    lax.fori_loop(0, n_rows, start, 0, unroll=8)


def _row_gather_wait(n_rows, src_hbm, dst, sem):
    def wait(r, _):
        pltpu.make_async_copy(src_hbm.at[pl.ds(0, 1)], dst.at[pl.ds(r, 1)], sem).wait()
        return 0
    lax.fori_loop(0, n_rows, wait, 0, unroll=8)


def _moe_kernel(be_ref, tok_ref, tokn_ref, w_ref, h2_hbm, wg_ref, wu_ref, wd_ref, y_ref, xbuf, sem):
    del be_ref
    i = pl.program_id(0)
    n = pl.num_programs(0)
    slot = lax.rem(i, 2)

    @pl.when(i == 0)
    def _():
        _row_gather(tok_ref, MOE_BLOCK, h2_hbm, xbuf.at[0], sem.at[0])

    @pl.when(i + 1 < n)
    def _():
        _row_gather(tokn_ref, MOE_BLOCK, h2_hbm, xbuf.at[1 - slot], sem.at[1 - slot])

    _row_gather_wait(MOE_BLOCK, h2_hbm, xbuf.at[slot], sem.at[slot])
    xb = xbuf[slot].astype(BF16)
    g = _dot(xb, wg_ref[...])
    u = _dot(xb, wu_ref[...])
    hmid = (g * jax.nn.sigmoid(g)) * u
    y_ref[...] = _dot(hmid.astype(BF16), wd_ref[...]) * w_ref[...]


def _moe(block_e, row_tok3, row_w, h2, wg, wu, wd):
    nb = block_e.shape[0]
    idx = lambda f: pl.BlockSpec((1, 1, MOE_BLOCK), f, memory_space=pltpu.SMEM)
    grid_spec = pltpu.PrefetchScalarGridSpec(
        num_scalar_prefetch=1,
        grid=(nb,),
        in_specs=[idx(lambda i, be: (i, 0, 0)),
                  idx(lambda i, be: (jnp.minimum(i + 1, nb - 1), 0, 0)),
                  pl.BlockSpec((MOE_BLOCK, 1), lambda i, be: (i, 0)),
                  pl.BlockSpec(memory_space=pl.ANY),
                  pl.BlockSpec((None, D_MODEL, D_EXPERT), lambda i, be: (be[i], 0, 0)),
                  pl.BlockSpec((None, D_MODEL, D_EXPERT), lambda i, be: (be[i], 0, 0)),
                  pl.BlockSpec((None, D_EXPERT, D_MODEL), lambda i, be: (be[i], 0, 0))],
        out_specs=pl.BlockSpec((MOE_BLOCK, D_MODEL), lambda i, be: (i, 0)),
        scratch_shapes=[pltpu.VMEM((2, MOE_BLOCK, D_MODEL), F32), pltpu.SemaphoreType.DMA((2,))],
    )
    return pl.pallas_call(
        _moe_kernel,
        grid_spec=grid_spec,
        out_shape=jax.ShapeDtypeStruct((nb * MOE_BLOCK, D_MODEL), F32),
        compiler_params=pltpu.CompilerParams(dimension_semantics=("arbitrary",), vmem_limit_bytes=VMEM_LIMIT),
        name="moe_experts",
    )(block_e, row_tok3, row_tok3, row_w, h2, wg, wu, wd)


def _combine_kernel(pos_ref, posn_ref, x1_ref, g_ref, ys_hbm, o_ref, ybuf, sem):
    tc = COMBINE_TILE
    i = pl.program_id(0)
    n = pl.num_programs(0)
    slot = lax.rem(i, 2)

    @pl.when(i == 0)
    def _():
        _row_gather(pos_ref, 2 * tc, ys_hbm, ybuf.at[0], sem.at[0])

    @pl.when(i + 1 < n)
    def _():
        _row_gather(posn_ref, 2 * tc, ys_hbm, ybuf.at[1 - slot], sem.at[1 - slot])

    _row_gather_wait(2 * tc, ys_hbm, ybuf.at[slot], sem.at[slot])
    y = ybuf[slot]
    o_ref[...] = _rms(x1_ref[...] + y[:tc] + y[tc:], g_ref[...])


def _combine(pos3, x1, g, ys):
    N = x1.shape[0]
    tc = COMBINE_TILE
    nt = N // tc
    idx = lambda f: pl.BlockSpec((1, 1, 2 * tc), f, memory_space=pltpu.SMEM)
    return pl.pallas_call(
        _combine_kernel,
        grid=(nt,),
        in_specs=[idx(lambda i: (i, 0, 0)),
                  idx(lambda i: (jnp.minimum(i + 1, nt - 1), 0, 0)),
                  pl.BlockSpec((tc, D_MODEL), lambda i: (i, 0)),
                  pl.BlockSpec((1, D_MODEL), lambda i: (0, 0)),
                  pl.BlockSpec(memory_space=pl.ANY)],
        out_specs=pl.BlockSpec((tc, D_MODEL), lambda i: (i, 0)),
        out_shape=jax.ShapeDtypeStruct((N, D_MODEL), F32),
        scratch_shapes=[pltpu.VMEM((2, 2 * tc, D_MODEL), F32), pltpu.SemaphoreType.DMA((2,))],
        compiler_params=pltpu.CompilerParams(dimension_semantics=("arbitrary",), vmem_limit_bytes=VMEM_LIMIT),
        name="combine_norm",
    )(pos3, pos3, x1, g, ys)


def _block_diag(w):
    nb, c, _ = w.shape
    eye = jnp.eye(nb, dtype=w.dtype)
    return (eye[:, None, :, None] * w[:, :, None, :]).reshape(nb * c, nb * c)


def _dispatch_plan(route, N):
    A = N * TOP_K
    flat_e = route[:, 0:TOP_K].astype(jnp.int32).reshape(A)
    flat_w = route[:, TOP_K:2 * TOP_K].reshape(A)
    flat_tok = jnp.arange(A, dtype=jnp.int32) // TOP_K
    onehot = (flat_e[:, None] == jnp.arange(N_EXPERTS, dtype=jnp.int32)[None, :]).astype(jnp.int32)
    csum = jnp.cumsum(onehot, axis=0)
    rank = jnp.sum(csum * onehot, axis=1) - 1
    counts = csum[-1]
    padded = (counts + MOE_BLOCK - 1) // MOE_BLOCK * MOE_BLOCK
    pad_ends = jnp.cumsum(padded)
    pad_starts = pad_ends - padded
    pos = pad_starts[flat_e] + rank
    n_blocks = (A + N_EXPERTS * (MOE_BLOCK - 1) + MOE_BLOCK - 1) // MOE_BLOCK
    P = n_blocks * MOE_BLOCK
    row_tok = jnp.zeros((P,), jnp.int32).at[pos].set(flat_tok)
    row_w = jnp.zeros((P,), F32).at[pos].set(flat_w)
    block_e = jnp.minimum(jnp.searchsorted(pad_ends, jnp.arange(n_blocks, dtype=jnp.int32) * MOE_BLOCK, side='right'),
                          N_EXPERTS - 1).astype(jnp.int32)
    return pos, row_tok, row_w, block_e, n_blocks


def kernel(x, norm_mix_g, w_in, lambda_qk, subln_g, conv_w, conv_b, w_r, b_r, w_i, b_i, lru_lambda, w_o_attn, w_o_lru, w_out, norm_ffn_g, w_group, w_expert_router, w_gate, w_up, w_down, final_norm_g):
    B, S, D = x.shape
    N = B * S
    nt = S // SEQ_TILE
    depth = norm_mix_g.shape[0]
    assert depth == 1 and D == D_MODEL and S % SEQ_TILE == 0 and S % LRU_TILE == 0 and N % COMBINE_TILE == 0
    l = 0
    row = lambda v: v.reshape(1, -1).astype(F32)

    x2 = x.reshape(N, D)
    w_in_l = w_in[l]
    qT, k, vT, xr, yr = _inproj(x2, row(norm_mix_g[l]), w_in_l[:, :PROJ_COLS].astype(BF16), B, S)

    attn = _attn(qT, k.reshape(B, nt, SEQ_TILE, QK_COLS), vT, lambda_qk[l].reshape(4, HEAD_DIM).astype(F32),
                 row(subln_g[l]), B, S)

    lru = _lru(xr.reshape(B, S, LRU_WIDTH), yr.reshape(B, S, LRU_WIDTH), conv_w[l].astype(F32), row(conv_b[l]),
               _block_diag(w_r[l]).astype(BF16), _block_diag(w_i[l]).astype(BF16), row(b_r[l]), row(b_i[l]),
               row(lru_lambda[l]), B, S)

    w_route = jnp.concatenate(
        [w_group[l], jnp.transpose(w_expert_router[l], (1, 0, 2)).reshape(D, N_EXPERTS),
         jnp.zeros((D, ROUTE_LANES - N_GROUPS - N_EXPERTS), F32)], axis=1).astype(F32)
    x1, h2, route = _merge(x2, attn.reshape(N, ATTN_WIDTH), lru.reshape(N, LRU_WIDTH), row(norm_mix_g[l]),
                           w_in_l[:, PROJ_COLS:].astype(BF16), w_o_attn[l].astype(BF16), w_o_lru[l].astype(BF16),
                           w_out[l].astype(BF16), row(norm_ffn_g[l]), w_route)

    pos, row_tok, row_w, block_e, n_blocks = _dispatch_plan(route, N)
    ys = _moe(block_e, row_tok.reshape(n_blocks, 1, MOE_BLOCK), row_w.reshape(-1, 1), h2,
              w_gate[l].astype(BF16), w_up[l].astype(BF16), w_down[l].astype(BF16))

    tc = COMBINE_TILE
    pos3 = jnp.transpose(pos.reshape(N // tc, tc, TOP_K), (0, 2, 1)).reshape(N // tc, 1, TOP_K * tc)
    out = _combine(pos3, x1, row(final_norm_g), ys)
    return out.reshape(B, S, D)
```

```python
import functools
import math

import jax
import jax.numpy as jnp
from jax import lax
from jax.experimental import pallas as pl
from jax.experimental.pallas import tpu as pltpu

F32 = jnp.float32
BF16 = jnp.bfloat16

D_MODEL = 1024
N_HEADS = 4
HEAD_DIM = 64
V_DIM = 2 * HEAD_DIM
ATTN_WIDTH = N_HEADS * V_DIM
LRU_WIDTH = D_MODEL // 2
LRU_BLOCKS = 8
CONV_W = 4
LRU_C = 8.0
N_GROUPS = 4
EXPERTS_PER_GROUP = 8
N_EXPERTS = N_GROUPS * EXPERTS_PER_GROUP
TOP_K = 2
D_EXPERT = D_MODEL // 2
NORM_EPS = 1e-6
LAM_INIT = 0.8 - 0.6 * math.exp(-0.3 * 0)

QK_COLS = N_HEADS * 2 * HEAD_DIM
PROJ_COLS = 2 * QK_COLS + ATTN_WIDTH + 2 * LRU_WIDTH
ROUTE_LANES = 128
NEG_BIG = -1e30
LOG2E = math.log2(math.e)
ALIBI_SLOPES = tuple(2.0 ** (-8.0 * (h + 1) / N_HEADS) for h in range(N_HEADS))

SEQ_TILE = 512
LRU_TILE = 512
ROW_TILE = 256
MOE_TILE = 256
DMA_GROUP = 8
LANES = 128
PACK_SUB = D_MODEL // 2 // LANES
F32_SUB = D_MODEL // LANES
EXPERT_LANE0 = N_GROUPS
VMEM_LIMIT = 48 * 1024 * 1024


def _rms(x, g):
    return x * lax.rsqrt(jnp.mean(x * x, axis=-1, keepdims=True) + NORM_EPS) * g


def _dot(a, b):
    return jnp.dot(a, b, preferred_element_type=F32)


def _inproj_kernel(x_ref, g_ref, w_ref, qT_ref, k_ref, vT_ref, xr_ref, yr_ref):
    hb = _rms(x_ref[...], g_ref[...]).astype(BF16)

    def proj(lo, hi):
        return _dot(hb, w_ref[:, lo:hi])

    q = proj(0, QK_COLS) * (HEAD_DIM ** -0.5 * LOG2E)
    for h in range(N_HEADS):
        qT_ref[h] = q[:, h * V_DIM:(h + 1) * V_DIM].T.astype(BF16)
    k_ref[...] = proj(QK_COLS, 2 * QK_COLS).astype(BF16)
    v = proj(2 * QK_COLS, 2 * QK_COLS + ATTN_WIDTH)
    for h in range(N_HEADS):
        vT_ref[h] = v[:, h * V_DIM:(h + 1) * V_DIM].T.astype(BF16)
    c0 = 2 * QK_COLS + ATTN_WIDTH
    xr_ref[...] = proj(c0, c0 + LRU_WIDTH)
    yr_ref[...] = proj(c0 + LRU_WIDTH, c0 + 2 * LRU_WIDTH)


def _inproj(x2, g, w, B, S):
    N = B * S
    tm = SEQ_TILE
    nt = S // tm
    tile5 = pl.BlockSpec((None, N_HEADS, None, V_DIM, tm), lambda i: (i // nt, 0, i % nt, 0, 0))
    rows = lambda c: pl.BlockSpec((tm, c), lambda i: (i, 0))
    return pl.pallas_call(
        _inproj_kernel,
        grid=(N // tm,),
        in_specs=[rows(D_MODEL),
                  pl.BlockSpec((1, D_MODEL), lambda i: (0, 0)),
                  pl.BlockSpec((D_MODEL, PROJ_COLS), lambda i: (0, 0))],
        out_specs=[tile5, rows(QK_COLS), tile5, rows(LRU_WIDTH), rows(LRU_WIDTH)],
        out_shape=[jax.ShapeDtypeStruct((B, N_HEADS, nt, V_DIM, tm), BF16),
                   jax.ShapeDtypeStruct((N, QK_COLS), BF16),
                   jax.ShapeDtypeStruct((B, N_HEADS, nt, V_DIM, tm), BF16),
                   jax.ShapeDtypeStruct((N, LRU_WIDTH), F32),
                   jax.ShapeDtypeStruct((N, LRU_WIDTH), F32)],
        compiler_params=pltpu.CompilerParams(dimension_semantics=("parallel",),
                                             vmem_limit_bytes=VMEM_LIMIT),
        name="inproj",
    )(x2, g, w)


def _attn_kernel(qT_ref, k_ref, vT_ref, lam_ref, g_ref, o_ref, bias_ref, acc_ref, sa_ref, sb_ref, pa_ref, pb_ref):
    t = SEQ_TILE
    h = pl.program_id(1)
    i = pl.program_id(2)
    slope = LOG2E * jnp.where(h == 0, ALIBI_SLOPES[0], jnp.where(h == 1, ALIBI_SLOPES[1],
                              jnp.where(h == 2, ALIBI_SLOPES[2], ALIBI_SLOPES[3]))).astype(F32)

    @pl.when(i == 0)
    def _():
        r = lax.broadcasted_iota(jnp.int32, (t, t), 0)
        c = lax.broadcasted_iota(jnp.int32, (t, t), 1)
        b = r.astype(F32) * slope
        bias_ref[0] = b
        bias_ref[1] = jnp.where(r <= c, b, NEG_BIG)
        bias_ref[2] = jnp.full((t, t), NEG_BIG, F32)

    qf = qT_ref[...].astype(F32)
    row = lax.broadcasted_iota(jnp.int32, qf.shape, 0)
    qs = (jnp.where(row < HEAD_DIM, qf, 0.0).astype(BF16), jnp.where(row >= HEAD_DIM, qf, 0.0).astype(BF16))
    acc_ref[...] = jnp.zeros_like(acc_ref)
    pb_ref[...] = jnp.zeros_like(pb_ref)

    def stage_q(tau, s_ref):
        kt = k_ref[jnp.minimum(tau, i)]
        bias = bias_ref[jnp.where(tau < i, 0, jnp.where(tau == i, 1, 2))]
        tile_max = []
        for mi in range(2):
            s = _dot(kt, qs[mi]) + bias
            s_ref[mi] = s
            tile_max.append(jnp.max(s, axis=0, keepdims=True))
        return tuple(tile_max)

    def stage_s(tau, s_ref, p_ref, ml, tile_max):
        cj = slope * (jnp.minimum(tau, i) * t).astype(F32)
        out = []
        for mi in range(2):
            m, l = ml[2 * mi], ml[2 * mi + 1]
            m_new = jnp.maximum(m, tile_max[mi] + cj)
            alpha = jnp.exp2(m - m_new)
            p = jnp.exp2(s_ref[mi] - (m_new - cj))
            p_ref[mi] = p.astype(BF16)
            out += [m_new, alpha * l + jnp.sum(p, axis=0, keepdims=True), alpha]
        return (out[0], out[1], out[3], out[4]), (out[2], out[5])

    def stage_v(tau, p_ref, alphas):
        vt = vT_ref[jnp.clip(tau, 0, i)]
        for mi in range(2):
            acc_ref[mi] = alphas[mi] * acc_ref[mi] + _dot(vt, p_ref[mi])

    def body(jj, carry):
        ml, alphas, tmax = carry[:4], carry[4:6], carry[6:]
        tau = 2 * jj
        stage_v(tau - 1, pb_ref, alphas)
        ml, alphas = stage_s(tau, sa_ref, pa_ref, ml, tmax)
        tmax = stage_q(tau + 1, sb_ref)
        stage_v(tau, pa_ref, alphas)
        ml, alphas = stage_s(tau + 1, sb_ref, pb_ref, ml, tmax)
        tmax = stage_q(tau + 2, sa_ref)
        return ml + alphas + tmax

    m_init = jnp.full((1, t), NEG_BIG, F32)
    l_init = jnp.zeros((1, t), F32)
    one = jnp.ones((1, t), F32)
    tmax0 = stage_q(0, sa_ref)
    n_pairs = jnp.right_shift(i + 2, 1)
    fin = lax.fori_loop(0, n_pairs, body, (m_init, l_init, m_init, l_init, one, one) + tmax0)
    stage_v(2 * n_pairs - 1, pb_ref, fin[4:6])
    l0, l1 = fin[1], fin[3]

    lp = lam_ref[...]
    s1 = jnp.sum(lp[0:1] * lp[1:2], axis=-1, keepdims=True)
    s2 = jnp.sum(lp[2:3] * lp[3:4], axis=-1, keepdims=True)
    lam = jnp.exp(s1) - jnp.exp(s2) + LAM_INIT
    oT = acc_ref[0] * (1.0 / l0) - lam * (acc_ref[1] * (1.0 / l1))
    o = _rms(oT.T, g_ref[...]) * (1.0 - LAM_INIT)
    o_ref[...] = o.astype(BF16)


def _attn(qT, k4, vT, lam, g, B, S):
    t = SEQ_TILE
    nt = S // t
    return pl.pallas_call(
        _attn_kernel,
        grid=(B, N_HEADS, nt),
        in_specs=[pl.BlockSpec((None, None, None, V_DIM, t), lambda b, h, i: (b, h, i, 0, 0)),
                  pl.BlockSpec((None, nt, t, V_DIM), lambda b, h, i: (b, 0, 0, h)),
                  pl.BlockSpec((None, None, nt, V_DIM, t), lambda b, h, i: (b, h, 0, 0, 0)),
                  pl.BlockSpec((4, HEAD_DIM), lambda b, h, i: (0, 0)),
                  pl.BlockSpec((1, V_DIM), lambda b, h, i: (0, 0))],
        out_specs=pl.BlockSpec((None, t, V_DIM), lambda b, h, i: (b, i, h)),
        out_shape=jax.ShapeDtypeStruct((B, S, ATTN_WIDTH), BF16),
        scratch_shapes=[pltpu.VMEM((3, t, t), F32), pltpu.VMEM((2, V_DIM, t), F32),
                        pltpu.VMEM((2, t, t), F32), pltpu.VMEM((2, t, t), F32),
                        pltpu.VMEM((2, t, t), BF16), pltpu.VMEM((2, t, t), BF16)],
        compiler_params=pltpu.CompilerParams(dimension_semantics=("parallel", "arbitrary", "arbitrary"),
                                             vmem_limit_bytes=VMEM_LIMIT),
        name="diff_attn",
    )(qT, k4, vT, lam, g)


def _lru_kernel(xr_ref, yr_ref, cw_ref, cb_ref, wr_ref, wi_ref, br_ref, bi_ref, lam_ref, o_ref, xbuf, hc):
    T = LRU_TILE
    ti = pl.program_id(1)

    @pl.when(ti == 0)
    def _():
        xbuf[0:8] = jnp.zeros((8, LRU_WIDTH), F32)
        hc[...] = jnp.zeros_like(hc)

    x = xr_ref[...]
    xbuf[8:8 + T] = x
    cw = cw_ref[...]
    xc = cb_ref[...] + cw[3:4] * x
    for j in range(CONV_W - 1):
        xc = xc + cw[j:j + 1] * xbuf[5 + j:5 + j + T]
    xbuf[0:8] = x[T - 8:T]

    xb = xc.astype(BF16)
    r = jax.nn.sigmoid(_dot(xb, wr_ref[...]) + br_ref[...])
    ig = jax.nn.sigmoid(_dot(xb, wi_ref[...]) + bi_ref[...])
    z = -lam_ref[...]
    softplus = jnp.maximum(z, 0.0) + jnp.log1p(jnp.exp(-jnp.abs(z)))
    la = -LRU_C * r * softplus
    a = jnp.exp(la)
    mult = jnp.sqrt(-jnp.tanh(la) * (a * a + 1.0))
    row = lax.broadcasted_iota(jnp.int32, (T, LRU_WIDTH), 0)
    mult = jnp.where((row == 0) & (ti == 0), 1.0, mult)
    u = (xc * ig) * mult

    d = 1
    while d < T:
        valid = row >= d
        u = jnp.where(valid, a * pltpu.roll(u, d, 0) + u, u)
        a = jnp.where(valid, a * pltpu.roll(a, d, 0), a)
        d *= 2
    hfull = u + a * hc[...]
    hc[...] = hfull[T - 1:T]
    y = yr_ref[...]
    gelu = 0.5 * y * (1.0 + jnp.tanh(0.7978845608028654 * (y + 0.044715 * (y * y * y))))
    o_ref[...] = (hfull * gelu).astype(BF16)


def _lru(xr, yr, cw, cb, wr, wi, br, bi, lam, B, S):
    T = LRU_TILE
    seq = pl.BlockSpec((None, T, LRU_WIDTH), lambda b, t: (b, t, 0))
    full = lambda r, c: pl.BlockSpec((r, c), lambda b, t: (0, 0))
    return pl.pallas_call(
        _lru_kernel,
        grid=(B, S // T),
        in_specs=[seq, seq, full(CONV_W, LRU_WIDTH), full(1, LRU_WIDTH), full(LRU_WIDTH, LRU_WIDTH),
                  full(LRU_WIDTH, LRU_WIDTH), full(1, LRU_WIDTH), full(1, LRU_WIDTH), full(1, LRU_WIDTH)],
        out_specs=seq,
        out_shape=jax.ShapeDtypeStruct((B, S, LRU_WIDTH), BF16),
        scratch_shapes=[pltpu.VMEM((T + 8, LRU_WIDTH), F32), pltpu.VMEM((1, LRU_WIDTH), F32)],
        compiler_params=pltpu.CompilerParams(dimension_semantics=("arbitrary", "arbitrary"),
                                             vmem_limit_bytes=VMEM_LIMIT),
        name="rg_lru",
    )(xr, yr, cw, cb, wr, wi, br, bi, lam)


def _pack_rows(v, out_ref):
    bits = pltpu.bitcast(v.astype(BF16).astype(F32), jnp.uint32)
    half = D_MODEL // 2
    packed = (bits[:, :half] >> 16) | (bits[:, half:] & jnp.uint32(0xFFFF0000))
    for c in range(PACK_SUB):
        out_ref[pl.ds(c, v.shape[0], stride=PACK_SUB), :] = packed[:, c * LANES:(c + 1) * LANES]


def _unpack_rows(in_ref, rows):
    lo, hi = [], []
    for c in range(PACK_SUB):
        w = in_ref[pl.ds(c, rows, stride=PACK_SUB), :]
        lo.append(pltpu.bitcast(w << 16, F32))
        hi.append(pltpu.bitcast(w & jnp.uint32(0xFFFF0000), F32))
    return jnp.concatenate(lo + hi, axis=1).astype(BF16)


def _merge_kernel(x_ref, attn_ref, lru_ref, g1_ref, wg_ref, woa_ref, wol_ref, wout_ref, g2_ref, wrt_ref,
                  x1_ref, h2p_ref, route_ref, counts_ref, cnt, tri):
    tm = SEQ_TILE

    @pl.when(pl.program_id(0) == 0)
    def _():
        cnt[...] = jnp.zeros_like(cnt)
        r = lax.broadcasted_iota(jnp.int32, (tm, tm), 0)
        c = lax.broadcasted_iota(jnp.int32, (tm, tm), 1)
        tri[...] = jnp.where(c < r, 1.0, 0.0).astype(BF16)

    x = x_ref[...]
    hb = _rms(x, g1_ref[...]).astype(BF16)
    gates = jax.nn.sigmoid(_dot(hb, wg_ref[...]))
    merged = gates[:, :D_MODEL] * _dot(attn_ref[...], woa_ref[...]) + gates[:, D_MODEL:] * _dot(lru_ref[...], wol_ref[...])
    x1 = x + _dot(merged.astype(BF16), wout_ref[...])
    x1_ref[...] = x1
    h2 = _rms(x1, g2_ref[...])
    _pack_rows(h2, h2p_ref)

    logits = jnp.dot(h2, wrt_ref[...], preferred_element_type=F32, precision=lax.Precision.HIGHEST)
    lane = lax.broadcasted_iota(jnp.int32, logits.shape, 1)
    big = jnp.int32(1 << 20)

    def first_argmax(v):
        m = jnp.max(v, axis=-1, keepdims=True)
        return m, jnp.min(jnp.where(v == m, lane, big), axis=-1, keepdims=True)

    gmask = lane < N_GROUPS
    gmax, gidx = first_argmax(jnp.where(gmask, logits, -jnp.inf))
    gsum = jnp.sum(jnp.where(gmask, jnp.exp(logits - gmax), 0.0), axis=-1, keepdims=True)
    g_w = 1.0 / gsum
    lo = N_GROUPS + EXPERTS_PER_GROUP * gidx
    el = jnp.where((lane >= lo) & (lane < lo + EXPERTS_PER_GROUP), logits, -jnp.inf)
    m1, i1 = first_argmax(el)
    m2, i2 = first_argmax(jnp.where(lane == i1, -jnp.inf, el))
    rr = jnp.exp(m2 - m1)
    w1 = g_w / (1.0 + rr)
    w2 = g_w * rr / (1.0 + rr)
    oh1 = lane == i1
    oh2 = lane == i2
    oh = jnp.where(oh1 | oh2, 1.0, 0.0)
    before = _dot(tri[...], oh.astype(BF16)) + cnt[...]
    r1 = jnp.sum(jnp.where(oh1, before, 0.0), axis=-1, keepdims=True)
    r2 = jnp.sum(jnp.where(oh2, before, 0.0), axis=-1, keepdims=True)
    cnt[...] = cnt[...] + jnp.sum(oh, axis=0, keepdims=True)
    counts_ref[...] = cnt[...]
    vals = (i1.astype(F32), i2.astype(F32), w1, w2, r1, r2)
    route = jnp.zeros_like(logits)
    for k, v in enumerate(vals):
        route = jnp.where(lane == k, v, route)
    route_ref[...] = route


def _merge(x2, attn, lru, g1, wg, woa, wol, wout, g2, wrt):
    N = x2.shape[0]
    tm = SEQ_TILE
    rows = lambda c: pl.BlockSpec((tm, c), lambda i: (i, 0))
    full = lambda r, c: pl.BlockSpec((r, c), lambda i: (0, 0))
    return pl.pallas_call(
        _merge_kernel,
        grid=(N // tm,),
        in_specs=[rows(D_MODEL), rows(ATTN_WIDTH), rows(LRU_WIDTH), full(1, D_MODEL), full(D_MODEL, 2 * D_MODEL),
                  full(ATTN_WIDTH, D_MODEL), full(LRU_WIDTH, D_MODEL), full(D_MODEL, D_MODEL), full(1, D_MODEL),
                  full(D_MODEL, ROUTE_LANES)],
        out_specs=[rows(D_MODEL), pl.BlockSpec((tm * PACK_SUB, LANES), lambda i: (i, 0)), rows(ROUTE_LANES),
                   full(1, ROUTE_LANES)],
        out_shape=[jax.ShapeDtypeStruct((N, D_MODEL), F32), jax.ShapeDtypeStruct((N * PACK_SUB, LANES), jnp.uint32),
                   jax.ShapeDtypeStruct((N, ROUTE_LANES), F32), jax.ShapeDtypeStruct((1, ROUTE_LANES), F32)],
        scratch_shapes=[pltpu.VMEM((1, ROUTE_LANES), F32), pltpu.VMEM((tm, tm), BF16)],
        compiler_params=pltpu.CompilerParams(dimension_semantics=("arbitrary",), vmem_limit_bytes=VMEM_LIMIT),
        name="merge_route",
    )(x2, attn, lru, g1, wg, woa, wol, wout, g2, wrt)


def _for_each_assignment(starts_ref, lane_ref, rank_ref, fn):
    def group(gi, _):
        toks = [gi * DMA_GROUP + j for j in range(DMA_GROUP)]
        pos = [[starts_ref[lane_ref[0, 0, tk * TOP_K + k]] + rank_ref[0, 0, tk * TOP_K + k] for k in range(TOP_K)]
               for tk in toks]
        for tk, p in zip(toks, pos):
            for k in range(TOP_K):
                fn(tk, k, p[k])
        return 0
    lax.fori_loop(0, ROW_TILE // DMA_GROUP, group, 0)


def _dispatch_kernel(starts_ref, lane_ref, rank_ref, h2p_hbm, xs_hbm, sem):
    n_asg = TOP_K * ROW_TILE
    i = pl.program_id(0)
    slot = lax.rem(i, 2)

    def drain(s):
        pltpu.make_async_copy(h2p_hbm.at[pl.ds(0, n_asg * PACK_SUB)], xs_hbm.at[pl.ds(0, n_asg * PACK_SUB)],
                              sem.at[s]).wait()

    def start(tk, k, pos):
        pltpu.make_async_copy(h2p_hbm.at[pl.ds(pl.multiple_of((i * ROW_TILE + tk) * PACK_SUB, PACK_SUB), PACK_SUB)],
                              xs_hbm.at[pl.ds(pl.multiple_of(pos * PACK_SUB, PACK_SUB), PACK_SUB)],
                              sem.at[slot]).start()

    _for_each_assignment(starts_ref, lane_ref, rank_ref, start)

    @pl.when(i > 0)
    def _():
        drain(1 - slot)

    @pl.when(i == pl.num_programs(0) - 1)
    def _():
        drain(slot)


def _dispatch(starts, lane3, rank3, h2p):
    nt = lane3.shape[0]
    idx = pl.BlockSpec((1, 1, TOP_K * ROW_TILE), lambda i, st: (i, 0, 0), memory_space=pltpu.SMEM)
    grid_spec = pltpu.PrefetchScalarGridSpec(
        num_scalar_prefetch=1,
        grid=(nt,),
        in_specs=[idx, idx, pl.BlockSpec(memory_space=pl.ANY)],
        out_specs=pl.BlockSpec(memory_space=pl.ANY),
        scratch_shapes=[pltpu.SemaphoreType.DMA((2,))],
    )
    return pl.pallas_call(
        _dispatch_kernel,
        grid_spec=grid_spec,
        out_shape=jax.ShapeDtypeStruct((TOP_K * h2p.shape[0], LANES), jnp.uint32),
        compiler_params=pltpu.CompilerParams(dimension_semantics=("arbitrary",), has_side_effects=True),
        name="moe_dispatch",
    )(starts, lane3, rank3, h2p)


def _moe_kernel(vt_ref, ve_ref, vlo_ref, vhi_ref, xs_ref, wg_ref, wu_ref, wd_ref, ys_ref, wgb, wub, wdb, acc):
    tm = MOE_TILE
    v = pl.program_id(0)
    nv = pl.num_programs(0)
    t = vt_ref[v]
    e = ve_ref[v]
    prev = jnp.maximum(v - 1, 0)
    nxt = jnp.minimum(v + 1, nv - 1)

    @pl.when((v == 0) | (ve_ref[prev] != e))
    def _():
        wgb[...] = wg_ref[...].astype(BF16)
        wub[...] = wu_ref[...].astype(BF16)
        wdb[...] = wd_ref[...].astype(BF16)

    xb = _unpack_rows(xs_ref, tm)
    g = _dot(xb, wgb[...])
    u = _dot(xb, wub[...])
    hmid = (g * jax.nn.sigmoid(g)) * u
    y = _dot(hmid.astype(BF16), wdb[...])
    rows = t * tm + lax.broadcasted_iota(jnp.int32, (tm, 1), 0)
    y = jnp.where((rows >= vlo_ref[v]) & (rows < vhi_ref[v]), y, 0.0)

    first = (v == 0) | (vt_ref[prev] != t)

    @pl.when(first)
    def _():
        acc[...] = y

    @pl.when(jnp.logical_not(first))
    def _():
        acc[...] = acc[...] + y

    @pl.when((v == nv - 1) | (vt_ref[nxt] != t))
    def _():
        for c in range(F32_SUB):
            ys_ref[pl.ds(c, tm, stride=F32_SUB), :] = acc[:, c * LANES:(c + 1) * LANES]


def _moe(vt, ve, vlo, vhi, xs, wg, wu, wd):
    tm = MOE_TILE
    n_rows = xs.shape[0] // PACK_SUB
    grid_spec = pltpu.PrefetchScalarGridSpec(
        num_scalar_prefetch=4,
        grid=(vt.shape[0],),
        in_specs=[pl.BlockSpec((tm * PACK_SUB, LANES), lambda v, vt, ve, lo, hi: (vt[v], 0)),
                  pl.BlockSpec((None, D_MODEL, D_EXPERT), lambda v, vt, ve, lo, hi: (ve[v], 0, 0)),
                  pl.BlockSpec((None, D_MODEL, D_EXPERT), lambda v, vt, ve, lo, hi: (ve[v], 0, 0)),
                  pl.BlockSpec((None, D_EXPERT, D_MODEL), lambda v, vt, ve, lo, hi: (ve[v], 0, 0))],
        out_specs=pl.BlockSpec((tm * F32_SUB, LANES), lambda v, vt, ve, lo, hi: (vt[v], 0)),
        scratch_shapes=[pltpu.VMEM((D_MODEL, D_EXPERT), BF16), pltpu.VMEM((D_MODEL, D_EXPERT), BF16),
                        pltpu.VMEM((D_EXPERT, D_MODEL), BF16), pltpu.VMEM((tm, D_MODEL), F32)],
    )
    return pl.pallas_call(
        _moe_kernel,
        grid_spec=grid_spec,
        out_shape=jax.ShapeDtypeStruct((n_rows * F32_SUB, LANES), F32),
        compiler_params=pltpu.CompilerParams(dimension_semantics=("arbitrary",), vmem_limit_bytes=VMEM_LIMIT),
        name="moe_experts",
    )(vt, ve, vlo, vhi, xs, wg, wu, wd)


def _combine_kernel(starts_ref, lane_ref, rank_ref, lanen_ref, rankn_ref, x1_ref, route_ref, g_ref, ys_hbm, o_ref,
                    ybuf, sem):
    tc = ROW_TILE
    i = pl.program_id(0)
    n = pl.num_programs(0)
    slot = lax.rem(i, 2)

    def gather(l_ref, r_ref, s):
        def start(tk, k, pos):
            pltpu.make_async_copy(ys_hbm.at[pl.ds(pl.multiple_of(pos * F32_SUB, F32_SUB), F32_SUB)],
                                  ybuf.at[s, pl.ds(pl.multiple_of((k * tc + tk) * F32_SUB, F32_SUB), F32_SUB)],
                                  sem.at[s]).start()
        _for_each_assignment(starts_ref, l_ref, r_ref, start)

    @pl.when(i == 0)
    def _():
        gather(lane_ref, rank_ref, 0)

    @pl.when(i + 1 < n)
    def _():
        gather(lanen_ref, rankn_ref, 1 - slot)

    pltpu.make_async_copy(ys_hbm.at[pl.ds(0, TOP_K * tc * F32_SUB)], ybuf.at[slot], sem.at[slot]).wait()

    route = route_ref[...]
    w = [route[:, TOP_K + k:TOP_K + k + 1] for k in range(TOP_K)]
    z = []
    for c in range(F32_SUB):
        zc = x1_ref[:, c * LANES:(c + 1) * LANES]
        for k in range(TOP_K):
            zc = zc + w[k] * ybuf.at[slot][pl.ds(k * tc * F32_SUB + c, tc, stride=F32_SUB), :]
        z.append(zc)
    ss = sum(jnp.sum(zc * zc, axis=-1, keepdims=True) for zc in z)
    inv = lax.rsqrt(ss * (1.0 / D_MODEL) + NORM_EPS)
    for c in range(F32_SUB):
        o_ref[:, c * LANES:(c + 1) * LANES] = z[c] * inv * g_ref[:, c * LANES:(c + 1) * LANES]


def _combine(starts, lane3, rank3, x1, route, g, ys):
    N = x1.shape[0]
    tc = ROW_TILE
    nt = N // tc
    idx = lambda f: pl.BlockSpec((1, 1, TOP_K * tc), f, memory_space=pltpu.SMEM)
    cur = idx(lambda i, st: (i, 0, 0))
    nxt = idx(lambda i, st: (jnp.minimum(i + 1, nt - 1), 0, 0))
    grid_spec = pltpu.PrefetchScalarGridSpec(
        num_scalar_prefetch=1,
        grid=(nt,),
        in_specs=[cur, cur, nxt, nxt,
                  pl.BlockSpec((tc, D_MODEL), lambda i, st: (i, 0)),
                  pl.BlockSpec((tc, ROUTE_LANES), lambda i, st: (i, 0)),
                  pl.BlockSpec((1, D_MODEL), lambda i, st: (0, 0)),
                  pl.BlockSpec(memory_space=pl.ANY)],
        out_specs=pl.BlockSpec((tc, D_MODEL), lambda i, st: (i, 0)),
        scratch_shapes=[pltpu.VMEM((2, TOP_K * tc * F32_SUB, LANES), F32), pltpu.SemaphoreType.DMA((2,))],
    )
    return pl.pallas_call(
        _combine_kernel,
        grid_spec=grid_spec,
        out_shape=jax.ShapeDtypeStruct((N, D_MODEL), F32),
        compiler_params=pltpu.CompilerParams(dimension_semantics=("arbitrary",), vmem_limit_bytes=VMEM_LIMIT),
        name="combine_norm",
    )(starts, lane3, rank3, lane3, rank3, x1, route, g, ys)


def _block_diag(w):
    nb, c, _ = w.shape
    eye = jnp.eye(nb, dtype=w.dtype)
    return (eye[:, None, :, None] * w[:, :, None, :]).reshape(nb * c, nb * c)


def _visit_plan(counts, n_rows):
    tm = MOE_TILE
    n_tiles = n_rows // tm
    n_visits = n_tiles + N_EXPERTS - 1
    cnt = counts.astype(jnp.int32)
    ends = jnp.cumsum(cnt)
    starts = ends - cnt
    first_tile = starts // tm
    n_vis = jnp.where(cnt > 0, (ends - 1) // tm - first_tile + 1, 0)
    v_end = jnp.cumsum(n_vis)
    v_start = v_end - n_vis
    total = v_end[-1]
    v = jnp.arange(n_visits, dtype=jnp.int32)
    owner = lambda q: jnp.minimum(jnp.searchsorted(v_end, q, side='right'), ROUTE_LANES - 1).astype(jnp.int32)
    lane = owner(v)
    valid = v < total
    tile = first_tile[lane] + (v - v_start[lane])
    lo = jnp.maximum(starts[lane], tile * tm)
    hi = jnp.minimum(ends[lane], (tile + 1) * tm)
    vt = jnp.where(valid, tile, n_tiles - 1)
    ve = jnp.where(valid, lane, owner(total - 1)) - EXPERT_LANE0
    zero = jnp.zeros_like(lo)
    return starts, vt, ve, jnp.where(valid, lo, zero), jnp.where(valid, hi, zero)


def kernel(x, norm_mix_g, w_in, lambda_qk, subln_g, conv_w, conv_b, w_r, b_r, w_i, b_i, lru_lambda, w_o_attn, w_o_lru, w_out, norm_ffn_g, w_group, w_expert_router, w_gate, w_up, w_down, final_norm_g):
    B, S, D = x.shape
    N = B * S
    nt = S // SEQ_TILE
    depth = norm_mix_g.shape[0]
    assert depth == 1 and D == D_MODEL and S % SEQ_TILE == 0 and S % LRU_TILE == 0
    assert N % ROW_TILE == 0 and (N * TOP_K) % MOE_TILE == 0
    l = 0
    row = lambda v: v.reshape(1, -1).astype(F32)

    x2 = x.reshape(N, D)
    w_in_l = w_in[l]
    qT, k, vT, xr, yr = _inproj(x2, row(norm_mix_g[l]), w_in_l[:, :PROJ_COLS].astype(BF16), B, S)

    attn = _attn(qT, k.reshape(B, nt, SEQ_TILE, QK_COLS), vT, lambda_qk[l].reshape(4, HEAD_DIM).astype(F32),
                 row(subln_g[l]), B, S)

    lru = _lru(xr.reshape(B, S, LRU_WIDTH), yr.reshape(B, S, LRU_WIDTH), conv_w[l].astype(F32), row(conv_b[l]),
               _block_diag(w_r[l]).astype(BF16), _block_diag(w_i[l]).astype(BF16), row(b_r[l]), row(b_i[l]),
               row(lru_lambda[l]), B, S)

    w_route = jnp.concatenate(
        [w_group[l], jnp.transpose(w_expert_router[l], (1, 0, 2)).reshape(D, N_EXPERTS),
         jnp.zeros((D, ROUTE_LANES - N_GROUPS - N_EXPERTS), F32)], axis=1).astype(F32)
    x1, h2p, route, counts = _merge(x2, attn.reshape(N, ATTN_WIDTH), lru.reshape(N, LRU_WIDTH), row(norm_mix_g[l]),
                                    w_in_l[:, PROJ_COLS:].astype(BF16), w_o_attn[l].astype(BF16),
                                    w_o_lru[l].astype(BF16), w_out[l].astype(BF16), row(norm_ffn_g[l]), w_route)

    starts, vt, ve, vlo, vhi = _visit_plan(counts[0], N * TOP_K)
    as_idx = lambda cols: cols.astype(jnp.int32).reshape(N // ROW_TILE, 1, TOP_K * ROW_TILE)
    lane3 = as_idx(route[:, 0:TOP_K])
    rank3 = as_idx(route[:, 2 * TOP_K:3 * TOP_K])

    xs = _dispatch(starts, lane3, rank3, h2p)
    ys = _moe(vt, ve, vlo, vhi, xs, w_gate[l], w_up[l], w_down[l])
    out = _combine(starts, lane3, rank3, x1, route, row(final_norm_g), ys)
    return out.reshape(B, S, D)
```

```python
import functools
import math

import jax
import jax.numpy as jnp
from jax import lax
from jax.experimental import pallas as pl
from jax.experimental.pallas import tpu as pltpu

F32 = jnp.float32
BF16 = jnp.bfloat16

D_MODEL = 1024
N_HEADS = 4
HEAD_DIM = 64
V_DIM = 2 * HEAD_DIM
ATTN_WIDTH = N_HEADS * V_DIM
LRU_WIDTH = D_MODEL // 2
LRU_BLOCKS = 8
CONV_W = 4
LRU_C = 8.0
N_GROUPS = 4
EXPERTS_PER_GROUP = 8
N_EXPERTS = N_GROUPS * EXPERTS_PER_GROUP
TOP_K = 2
D_EXPERT = D_MODEL // 2
NORM_EPS = 1e-6
LAM_INIT = 0.8 - 0.6 * math.exp(-0.3 * 0)

QK_COLS = N_HEADS * 2 * HEAD_DIM
PROJ_COLS = 2 * QK_COLS + ATTN_WIDTH + 2 * LRU_WIDTH
ROUTE_LANES = 128
NEG_BIG = -1e30
LOG2E = math.log2(math.e)
ALIBI_SLOPES = tuple(2.0 ** (-8.0 * (h + 1) / N_HEADS) for h in range(N_HEADS))

SEQ_TILE = 512
LRU_TILE = 512
ROW_TILE = 256
MOE_TILE = 256
DMA_GROUP = 8
DISPATCH_SLOTS = 3
LANES = 128
PACK_SUB = D_MODEL // 2 // LANES
F32_SUB = D_MODEL // LANES
EXPERT_LANE0 = N_GROUPS
VMEM_LIMIT = 48 * 1024 * 1024


def _rms(x, g):
    return x * lax.rsqrt(jnp.mean(x * x, axis=-1, keepdims=True) + NORM_EPS) * g


def _dot(a, b):
    return jnp.dot(a, b, preferred_element_type=F32)


def _inproj_kernel(x_ref, g_ref, w_ref, qT_ref, k_ref, vT_ref, xr_ref, yr_ref):
    hb = _rms(x_ref[...], g_ref[...]).astype(BF16)

    def proj(lo, hi):
        return _dot(hb, w_ref[:, lo:hi])

    q = proj(0, QK_COLS) * (HEAD_DIM ** -0.5 * LOG2E)
    for h in range(N_HEADS):
        qT_ref[h] = q[:, h * V_DIM:(h + 1) * V_DIM].T.astype(BF16)
    k_ref[...] = proj(QK_COLS, 2 * QK_COLS).astype(BF16)
    v = proj(2 * QK_COLS, 2 * QK_COLS + ATTN_WIDTH)
    for h in range(N_HEADS):
        vT_ref[h] = v[:, h * V_DIM:(h + 1) * V_DIM].T.astype(BF16)
    c0 = 2 * QK_COLS + ATTN_WIDTH
    xr_ref[...] = proj(c0, c0 + LRU_WIDTH)
    yr_ref[...] = proj(c0 + LRU_WIDTH, c0 + 2 * LRU_WIDTH)


def _inproj(x2, g, w, B, S):
    N = B * S
    tm = SEQ_TILE
    nt = S // tm
    tile5 = pl.BlockSpec((None, N_HEADS, None, V_DIM, tm), lambda i: (i // nt, 0, i % nt, 0, 0))
    rows = lambda c: pl.BlockSpec((tm, c), lambda i: (i, 0))
    return pl.pallas_call(
        _inproj_kernel,
        grid=(N // tm,),
        in_specs=[rows(D_MODEL),
                  pl.BlockSpec((1, D_MODEL), lambda i: (0, 0)),
                  pl.BlockSpec((D_MODEL, PROJ_COLS), lambda i: (0, 0))],
        out_specs=[tile5, rows(QK_COLS), tile5, rows(LRU_WIDTH), rows(LRU_WIDTH)],
        out_shape=[jax.ShapeDtypeStruct((B, N_HEADS, nt, V_DIM, tm), BF16),
                   jax.ShapeDtypeStruct((N, QK_COLS), BF16),
                   jax.ShapeDtypeStruct((B, N_HEADS, nt, V_DIM, tm), BF16),
                   jax.ShapeDtypeStruct((N, LRU_WIDTH), F32),
                   jax.ShapeDtypeStruct((N, LRU_WIDTH), F32)],
        compiler_params=pltpu.CompilerParams(dimension_semantics=("parallel",),
                                             vmem_limit_bytes=VMEM_LIMIT),
        name="inproj",
    )(x2, g, w)


def _attn_kernel(qT_ref, k_ref, vT_ref, lam_ref, g_ref, o_ref, bias_ref, acc_ref, sa_ref, sb_ref, pa_ref, pb_ref):
    t = SEQ_TILE
    h = pl.program_id(1)
    i = pl.program_id(2)
    slope = LOG2E * jnp.where(h == 0, ALIBI_SLOPES[0], jnp.where(h == 1, ALIBI_SLOPES[1],
                              jnp.where(h == 2, ALIBI_SLOPES[2], ALIBI_SLOPES[3]))).astype(F32)

    @pl.when(i == 0)
    def _():
        r = lax.broadcasted_iota(jnp.int32, (t, t), 0)
        c = lax.broadcasted_iota(jnp.int32, (t, t), 1)
        b = r.astype(F32) * slope
        bias_ref[0] = b
        bias_ref[1] = jnp.where(r <= c, b, NEG_BIG)
        bias_ref[2] = jnp.full((t, t), NEG_BIG, F32)

    qf = qT_ref[...].astype(F32)
    row = lax.broadcasted_iota(jnp.int32, qf.shape, 0)
    qs = (jnp.where(row < HEAD_DIM, qf, 0.0).astype(BF16), jnp.where(row >= HEAD_DIM, qf, 0.0).astype(BF16))
    acc_ref[...] = jnp.zeros_like(acc_ref)
    pb_ref[...] = jnp.zeros_like(pb_ref)

    def stage_q(tau, s_ref):
        kt = k_ref[jnp.minimum(tau, i)]
        bias = bias_ref[jnp.where(tau < i, 0, jnp.where(tau == i, 1, 2))]
        tile_max = []
        for mi in range(2):
            s = _dot(kt, qs[mi]) + bias
            s_ref[mi] = s
            tile_max.append(jnp.max(s, axis=0, keepdims=True))
        return tuple(tile_max)

    def stage_s(tau, s_ref, p_ref, ml, tile_max):
        cj = slope * (jnp.minimum(tau, i) * t).astype(F32)
        out = []
        for mi in range(2):
            m, l = ml[2 * mi], ml[2 * mi + 1]
            m_new = jnp.maximum(m, tile_max[mi] + cj)
            alpha = jnp.exp2(m - m_new)
            p = jnp.exp2(s_ref[mi] - (m_new - cj))
            p_ref[mi] = p.astype(BF16)
            out += [m_new, alpha * l + jnp.sum(p, axis=0, keepdims=True), alpha]
        return (out[0], out[1], out[3], out[4]), (out[2], out[5])

    def stage_v(tau, p_ref, alphas):
        vt = vT_ref[jnp.clip(tau, 0, i)]
        for mi in range(2):
            acc_ref[mi] = alphas[mi] * acc_ref[mi] + _dot(vt, p_ref[mi])

    def body(jj, carry):
        ml, alphas, tmax = carry[:4], carry[4:6], carry[6:]
        tau = 2 * jj
        stage_v(tau - 1, pb_ref, alphas)
        ml, alphas = stage_s(tau, sa_ref, pa_ref, ml, tmax)
        tmax = stage_q(tau + 1, sb_ref)
        stage_v(tau, pa_ref, alphas)
        ml, alphas = stage_s(tau + 1, sb_ref, pb_ref, ml, tmax)
        tmax = stage_q(tau + 2, sa_ref)
        return ml + alphas + tmax

    m_init = jnp.full((1, t), NEG_BIG, F32)
    l_init = jnp.zeros((1, t), F32)
    one = jnp.ones((1, t), F32)
    tmax0 = stage_q(0, sa_ref)
    n_pairs = jnp.right_shift(i + 2, 1)
    fin = lax.fori_loop(0, n_pairs, body, (m_init, l_init, m_init, l_init, one, one) + tmax0)
    stage_v(2 * n_pairs - 1, pb_ref, fin[4:6])
    l0, l1 = fin[1], fin[3]

    lp = lam_ref[...]
    s1 = jnp.sum(lp[0:1] * lp[1:2], axis=-1, keepdims=True)
    s2 = jnp.sum(lp[2:3] * lp[3:4], axis=-1, keepdims=True)
    lam = jnp.exp(s1) - jnp.exp(s2) + LAM_INIT
    oT = acc_ref[0] * (1.0 / l0) - lam * (acc_ref[1] * (1.0 / l1))
    o = _rms(oT.T, g_ref[...]) * (1.0 - LAM_INIT)
    o_ref[...] = o.astype(BF16)


def _attn(qT, k4, vT, lam, g, B, S):
    t = SEQ_TILE
    nt = S // t
    return pl.pallas_call(
        _attn_kernel,
        grid=(B, N_HEADS, nt),
        in_specs=[pl.BlockSpec((None, None, None, V_DIM, t), lambda b, h, i: (b, h, i, 0, 0)),
                  pl.BlockSpec((None, nt, t, V_DIM), lambda b, h, i: (b, 0, 0, h)),
                  pl.BlockSpec((None, None, nt, V_DIM, t), lambda b, h, i: (b, h, 0, 0, 0)),
                  pl.BlockSpec((4, HEAD_DIM), lambda b, h, i: (0, 0)),
                  pl.BlockSpec((1, V_DIM), lambda b, h, i: (0, 0))],
        out_specs=pl.BlockSpec((None, t, V_DIM), lambda b, h, i: (b, i, h)),
        out_shape=jax.ShapeDtypeStruct((B, S, ATTN_WIDTH), BF16),
        scratch_shapes=[pltpu.VMEM((3, t, t), F32), pltpu.VMEM((2, V_DIM, t), F32),
                        pltpu.VMEM((2, t, t), F32), pltpu.VMEM((2, t, t), F32),
                        pltpu.VMEM((2, t, t), BF16), pltpu.VMEM((2, t, t), BF16)],
        compiler_params=pltpu.CompilerParams(dimension_semantics=("parallel", "arbitrary", "arbitrary"),
                                             vmem_limit_bytes=VMEM_LIMIT),
        name="diff_attn",
    )(qT, k4, vT, lam, g)


def _lru_kernel(xr_ref, yr_ref, cw_ref, cb_ref, wr_ref, wi_ref, br_ref, bi_ref, lam_ref, o_ref, xbuf, hc):
    T = LRU_TILE
    ti = pl.program_id(1)

    @pl.when(ti == 0)
    def _():
        xbuf[0:8] = jnp.zeros((8, LRU_WIDTH), F32)
        hc[...] = jnp.zeros_like(hc)

    x = xr_ref[...]
    xbuf[8:8 + T] = x
    cw = cw_ref[...]
    xc = cb_ref[...] + cw[3:4] * x
    for j in range(CONV_W - 1):
        xc = xc + cw[j:j + 1] * xbuf[5 + j:5 + j + T]
    xbuf[0:8] = x[T - 8:T]

    xb = xc.astype(BF16)
    r = jax.nn.sigmoid(_dot(xb, wr_ref[...]) + br_ref[...])
    ig = jax.nn.sigmoid(_dot(xb, wi_ref[...]) + bi_ref[...])
    z = -lam_ref[...]
    softplus = jnp.maximum(z, 0.0) + jnp.log1p(jnp.exp(-jnp.abs(z)))
    la = -LRU_C * r * softplus
    a = jnp.exp(la)
    mult = jnp.sqrt(-jnp.tanh(la) * (a * a + 1.0))
    row = lax.broadcasted_iota(jnp.int32, (T, LRU_WIDTH), 0)
    mult = jnp.where((row == 0) & (ti == 0), 1.0, mult)
    u = (xc * ig) * mult

    d = 1
    while d < T:
        valid = row >= d
        u = jnp.where(valid, a * pltpu.roll(u, d, 0) + u, u)
        a = jnp.where(valid, a * pltpu.roll(a, d, 0), a)
        d *= 2
    hfull = u + a * hc[...]
    hc[...] = hfull[T - 1:T]
    y = yr_ref[...]
    gelu = 0.5 * y * (1.0 + jnp.tanh(0.7978845608028654 * (y + 0.044715 * (y * y * y))))
    o_ref[...] = (hfull * gelu).astype(BF16)


def _lru(xr, yr, cw, cb, wr, wi, br, bi, lam, B, S):
    T = LRU_TILE
    seq = pl.BlockSpec((None, T, LRU_WIDTH), lambda b, t: (b, t, 0))
    full = lambda r, c: pl.BlockSpec((r, c), lambda b, t: (0, 0))
    return pl.pallas_call(
        _lru_kernel,
        grid=(B, S // T),
        in_specs=[seq, seq, full(CONV_W, LRU_WIDTH), full(1, LRU_WIDTH), full(LRU_WIDTH, LRU_WIDTH),
                  full(LRU_WIDTH, LRU_WIDTH), full(1, LRU_WIDTH), full(1, LRU_WIDTH), full(1, LRU_WIDTH)],
        out_specs=seq,
        out_shape=jax.ShapeDtypeStruct((B, S, LRU_WIDTH), BF16),
        scratch_shapes=[pltpu.VMEM((T + 8, LRU_WIDTH), F32), pltpu.VMEM((1, LRU_WIDTH), F32)],
        compiler_params=pltpu.CompilerParams(dimension_semantics=("arbitrary", "arbitrary"),
                                             vmem_limit_bytes=VMEM_LIMIT),
        name="rg_lru",
    )(xr, yr, cw, cb, wr, wi, br, bi, lam)


def _pack_rows(v, out_ref):
    bits = pltpu.bitcast(v.astype(BF16).astype(F32), jnp.uint32)
    half = D_MODEL // 2
    packed = (bits[:, :half] >> 16) | (bits[:, half:] & jnp.uint32(0xFFFF0000))
    for c in range(PACK_SUB):
        out_ref[pl.ds(c, v.shape[0], stride=PACK_SUB), :] = packed[:, c * LANES:(c + 1) * LANES]


def _unpack_rows(in_ref, rows):
    lo, hi = [], []
    for c in range(PACK_SUB):
        w = in_ref[pl.ds(c, rows, stride=PACK_SUB), :]
        lo.append(pltpu.bitcast(w << 16, F32))
        hi.append(pltpu.bitcast(w & jnp.uint32(0xFFFF0000), F32))
    return jnp.concatenate(lo + hi, axis=1).astype(BF16)


def _merge_kernel(x_ref, attn_ref, lru_ref, g1_ref, wg_ref, woa_ref, wol_ref, wout_ref, g2_ref, wrt_ref,
                  x1_ref, h2p_ref, route_ref, counts_ref, cnt, tri):
    tm = SEQ_TILE

    @pl.when(pl.program_id(0) == 0)
    def _():
        cnt[...] = jnp.zeros_like(cnt)
        r = lax.broadcasted_iota(jnp.int32, (tm, tm), 0)
        c = lax.broadcasted_iota(jnp.int32, (tm, tm), 1)
        tri[...] = jnp.where(c < r, 1.0, 0.0).astype(BF16)

    x = x_ref[...]
    hb = _rms(x, g1_ref[...]).astype(BF16)
    gates = jax.nn.sigmoid(_dot(hb, wg_ref[...]))
    merged = gates[:, :D_MODEL] * _dot(attn_ref[...], woa_ref[...]) + gates[:, D_MODEL:] * _dot(lru_ref[...], wol_ref[...])
    x1 = x + _dot(merged.astype(BF16), wout_ref[...])
    x1_ref[...] = x1
    h2 = _rms(x1, g2_ref[...])
    _pack_rows(h2, h2p_ref)

    logits = jnp.dot(h2, wrt_ref[...], preferred_element_type=F32, precision=lax.Precision.HIGHEST)
    lane = lax.broadcasted_iota(jnp.int32, logits.shape, 1)
    big = jnp.int32(1 << 20)

    def first_argmax(v):
        m = jnp.max(v, axis=-1, keepdims=True)
        return m, jnp.min(jnp.where(v == m, lane, big), axis=-1, keepdims=True)

    gmask = lane < N_GROUPS
    gmax, gidx = first_argmax(jnp.where(gmask, logits, -jnp.inf))
    gsum = jnp.sum(jnp.where(gmask, jnp.exp(logits - gmax), 0.0), axis=-1, keepdims=True)
    g_w = 1.0 / gsum
    lo = N_GROUPS + EXPERTS_PER_GROUP * gidx
    el = jnp.where((lane >= lo) & (lane < lo + EXPERTS_PER_GROUP), logits, -jnp.inf)
    m1, i1 = first_argmax(el)
    m2, i2 = first_argmax(jnp.where(lane == i1, -jnp.inf, el))
    rr = jnp.exp(m2 - m1)
    w1 = g_w / (1.0 + rr)
    w2 = g_w * rr / (1.0 + rr)
    oh1 = lane == i1
    oh2 = lane == i2
    oh = jnp.where(oh1 | oh2, 1.0, 0.0)
    before = _dot(tri[...], oh.astype(BF16)) + cnt[...]
    r1 = jnp.sum(jnp.where(oh1, before, 0.0), axis=-1, keepdims=True)
    r2 = jnp.sum(jnp.where(oh2, before, 0.0), axis=-1, keepdims=True)
    cnt[...] = cnt[...] + jnp.sum(oh, axis=0, keepdims=True)
    counts_ref[...] = cnt[...]
    vals = (i1.astype(F32), i2.astype(F32), w1, w2, r1, r2)
    route = jnp.zeros_like(logits)
    for k, v in enumerate(vals):
        route = jnp.where(lane == k, v, route)
    route_ref[...] = route


def _merge(x2, attn, lru, g1, wg, woa, wol, wout, g2, wrt):
    N = x2.shape[0]
    tm = SEQ_TILE
    rows = lambda c: pl.BlockSpec((tm, c), lambda i: (i, 0))
    full = lambda r, c: pl.BlockSpec((r, c), lambda i: (0, 0))
    return pl.pallas_call(
        _merge_kernel,
        grid=(N // tm,),
        in_specs=[rows(D_MODEL), rows(ATTN_WIDTH), rows(LRU_WIDTH), full(1, D_MODEL), full(D_MODEL, 2 * D_MODEL),
                  full(ATTN_WIDTH, D_MODEL), full(LRU_WIDTH, D_MODEL), full(D_MODEL, D_MODEL), full(1, D_MODEL),
                  full(D_MODEL, ROUTE_LANES)],
        out_specs=[rows(D_MODEL), pl.BlockSpec((tm * PACK_SUB, LANES), lambda i: (i, 0)), rows(ROUTE_LANES),
                   full(1, ROUTE_LANES)],
        out_shape=[jax.ShapeDtypeStruct((N, D_MODEL), F32), jax.ShapeDtypeStruct((N * PACK_SUB, LANES), jnp.uint32),
                   jax.ShapeDtypeStruct((N, ROUTE_LANES), F32), jax.ShapeDtypeStruct((1, ROUTE_LANES), F32)],
        scratch_shapes=[pltpu.VMEM((1, ROUTE_LANES), F32), pltpu.VMEM((tm, tm), BF16)],
        compiler_params=pltpu.CompilerParams(dimension_semantics=("arbitrary",), vmem_limit_bytes=VMEM_LIMIT),
        name="merge_route",
    )(x2, attn, lru, g1, wg, woa, wol, wout, g2, wrt)


def _for_each_assignment(starts_ref, lane_ref, rank_ref, fn):
    def group(gi, _):
        toks = [gi * DMA_GROUP + j for j in range(DMA_GROUP)]
        pos = [[starts_ref[lane_ref[0, 0, tk * TOP_K + k]] + rank_ref[0, 0, tk * TOP_K + k] for k in range(TOP_K)]
               for tk in toks]
        for tk, p in zip(toks, pos):
            for k in range(TOP_K):
                fn(tk, k, p[k])
        return 0
    lax.fori_loop(0, ROW_TILE // DMA_GROUP, group, 0)


def _dispatch_kernel(starts_ref, lane_ref, rank_ref, h2p_hbm, xs_hbm, xin, in_sem, out_sem):
    tile_rows = ROW_TILE * PACK_SUB
    i = pl.program_id(0)
    n = pl.num_programs(0)
    slot = lax.rem(i, DISPATCH_SLOTS)

    def fetch(tile, s):
        return pltpu.make_async_copy(h2p_hbm.at[pl.ds(pl.multiple_of(tile * tile_rows, tile_rows), tile_rows)],
                                     xin.at[s], in_sem.at[s])

    def drain(s):
        for _ in range(TOP_K):
            pltpu.make_async_copy(xin.at[s], xs_hbm.at[pl.ds(0, tile_rows)], out_sem.at[s]).wait()

    @pl.when(i == 0)
    def _():
        fetch(0, 0).start()

    @pl.when(i + 1 < n)
    def _():
        fetch(i + 1, lax.rem(i + 1, DISPATCH_SLOTS)).start()

    fetch(i, slot).wait()

    def start(tk, k, pos):
        pltpu.make_async_copy(xin.at[slot, pl.ds(pl.multiple_of(tk * PACK_SUB, PACK_SUB), PACK_SUB)],
                              xs_hbm.at[pl.ds(pl.multiple_of(pos * PACK_SUB, PACK_SUB), PACK_SUB)],
                              out_sem.at[slot]).start()

    _for_each_assignment(starts_ref, lane_ref, rank_ref, start)

    @pl.when(i > 0)
    def _():
        drain(lax.rem(i + DISPATCH_SLOTS - 1, DISPATCH_SLOTS))

    @pl.when(i == n - 1)
    def _():
        drain(slot)


def _dispatch(starts, lane3, rank3, h2p):
    nt = lane3.shape[0]
    idx = pl.BlockSpec((1, 1, TOP_K * ROW_TILE), lambda i, st: (i, 0, 0), memory_space=pltpu.SMEM)
    grid_spec = pltpu.PrefetchScalarGridSpec(
        num_scalar_prefetch=1,
        grid=(nt,),
        in_specs=[idx, idx, pl.BlockSpec(memory_space=pl.ANY)],
        out_specs=pl.BlockSpec(memory_space=pl.ANY),
        scratch_shapes=[pltpu.VMEM((DISPATCH_SLOTS, ROW_TILE * PACK_SUB, LANES), jnp.uint32),
                        pltpu.SemaphoreType.DMA((DISPATCH_SLOTS,)), pltpu.SemaphoreType.DMA((DISPATCH_SLOTS,))],
    )
    return pl.pallas_call(
        _dispatch_kernel,
        grid_spec=grid_spec,
        out_shape=jax.ShapeDtypeStruct((TOP_K * h2p.shape[0], LANES), jnp.uint32),
        compiler_params=pltpu.CompilerParams(dimension_semantics=("arbitrary",), has_side_effects=True),
        name="moe_dispatch",
    )(starts, lane3, rank3, h2p)


def _moe_kernel(vt_ref, ve_ref, vlo_ref, vhi_ref, xs_ref, wg_ref, wu_ref, wd_ref, ys_ref, wgb, wub, wdb, acc):
    tm = MOE_TILE
    v = pl.program_id(0)
    nv = pl.num_programs(0)
    t = vt_ref[v]
    e = ve_ref[v]
    prev = jnp.maximum(v - 1, 0)
    nxt = jnp.minimum(v + 1, nv - 1)

    @pl.when((v == 0) | (ve_ref[prev] != e))
    def _():
        wgb[...] = wg_ref[...].astype(BF16)
        wub[...] = wu_ref[...].astype(BF16)
        wdb[...] = wd_ref[...].astype(BF16)

    xb = _unpack_rows(xs_ref, tm)
    g = _dot(xb, wgb[...])
    u = _dot(xb, wub[...])
    hmid = (g * jax.nn.sigmoid(g)) * u
    y = _dot(hmid.astype(BF16), wdb[...])
    rows = t * tm + lax.broadcasted_iota(jnp.int32, (tm, 1), 0)
    y = jnp.where((rows >= vlo_ref[v]) & (rows < vhi_ref[v]), y, 0.0)

    first = (v == 0) | (vt_ref[prev] != t)

    @pl.when(first)
    def _():
        acc[...] = y

    @pl.when(jnp.logical_not(first))
    def _():
        acc[...] = acc[...] + y

    @pl.when((v == nv - 1) | (vt_ref[nxt] != t))
    def _():
        for c in range(F32_SUB):
            ys_ref[pl.ds(c, tm, stride=F32_SUB), :] = acc[:, c * LANES:(c + 1) * LANES]


def _moe(vt, ve, vlo, vhi, xs, wg, wu, wd):
    tm = MOE_TILE
    n_rows = xs.shape[0] // PACK_SUB
    grid_spec = pltpu.PrefetchScalarGridSpec(
        num_scalar_prefetch=4,
        grid=(vt.shape[0],),
        in_specs=[pl.BlockSpec((tm * PACK_SUB, LANES), lambda v, vt, ve, lo, hi: (vt[v], 0)),
                  pl.BlockSpec((None, D_MODEL, D_EXPERT), lambda v, vt, ve, lo, hi: (ve[v], 0, 0)),
                  pl.BlockSpec((None, D_MODEL, D_EXPERT), lambda v, vt, ve, lo, hi: (ve[v], 0, 0)),
                  pl.BlockSpec((None, D_EXPERT, D_MODEL), lambda v, vt, ve, lo, hi: (ve[v], 0, 0))],
        out_specs=pl.BlockSpec((tm * F32_SUB, LANES), lambda v, vt, ve, lo, hi: (vt[v], 0)),
        scratch_shapes=[pltpu.VMEM((D_MODEL, D_EXPERT), BF16), pltpu.VMEM((D_MODEL, D_EXPERT), BF16),
                        pltpu.VMEM((D_EXPERT, D_MODEL), BF16), pltpu.VMEM((tm, D_MODEL), F32)],
    )
    return pl.pallas_call(
        _moe_kernel,
        grid_spec=grid_spec,
        out_shape=jax.ShapeDtypeStruct((n_rows * F32_SUB, LANES), F32),
        compiler_params=pltpu.CompilerParams(dimension_semantics=("arbitrary",), vmem_limit_bytes=VMEM_LIMIT),
        name="moe_experts",
    )(vt, ve, vlo, vhi, xs, wg, wu, wd)


def _combine_kernel(starts_ref, lane_ref, rank_ref, lanen_ref, rankn_ref, x1_ref, route_ref, g_ref, ys_hbm, o_ref,
                    ybuf, sem):
    tc = ROW_TILE
    i = pl.program_id(0)
    n = pl.num_programs(0)
    slot = lax.rem(i, 2)

    def gather(l_ref, r_ref, s):
        def start(tk, k, pos):
            pltpu.make_async_copy(ys_hbm.at[pl.ds(pl.multiple_of(pos * F32_SUB, F32_SUB), F32_SUB)],
                                  ybuf.at[s, pl.ds(pl.multiple_of((k * tc + tk) * F32_SUB, F32_SUB), F32_SUB)],
                                  sem.at[s]).start()
        _for_each_assignment(starts_ref, l_ref, r_ref, start)

    @pl.when(i == 0)
    def _():
        gather(lane_ref, rank_ref, 0)

    @pl.when(i + 1 < n)
    def _():
        gather(lanen_ref, rankn_ref, 1 - slot)

    pltpu.make_async_copy(ys_hbm.at[pl.ds(0, TOP_K * tc * F32_SUB)], ybuf.at[slot], sem.at[slot]).wait()

    route = route_ref[...]
    w = [route[:, TOP_K + k:TOP_K + k + 1] for k in range(TOP_K)]
    z = []
    for c in range(F32_SUB):
        zc = x1_ref[:, c * LANES:(c + 1) * LANES]
        for k in range(TOP_K):
            zc = zc + w[k] * ybuf.at[slot][pl.ds(k * tc * F32_SUB + c, tc, stride=F32_SUB), :]
        z.append(zc)
    ss = sum(jnp.sum(zc * zc, axis=-1, keepdims=True) for zc in z)
    inv = lax.rsqrt(ss * (1.0 / D_MODEL) + NORM_EPS)
    for c in range(F32_SUB):
        o_ref[:, c * LANES:(c + 1) * LANES] = z[c] * inv * g_ref[:, c * LANES:(c + 1) * LANES]


def _combine(starts, lane3, rank3, x1, route, g, ys):
    N = x1.shape[0]
    tc = ROW_TILE
    nt = N // tc
    idx = lambda f: pl.BlockSpec((1, 1, TOP_K * tc), f, memory_space=pltpu.SMEM)
    cur = idx(lambda i, st: (i, 0, 0))
    nxt = idx(lambda i, st: (jnp.minimum(i + 1, nt - 1), 0, 0))
    grid_spec = pltpu.PrefetchScalarGridSpec(
        num_scalar_prefetch=1,
        grid=(nt,),
        in_specs=[cur, cur, nxt, nxt,
                  pl.BlockSpec((tc, D_MODEL), lambda i, st: (i, 0)),
                  pl.BlockSpec((tc, ROUTE_LANES), lambda i, st: (i, 0)),
                  pl.BlockSpec((1, D_MODEL), lambda i, st: (0, 0)),
                  pl.BlockSpec(memory_space=pl.ANY)],
        out_specs=pl.BlockSpec((tc, D_MODEL), lambda i, st: (i, 0)),
        scratch_shapes=[pltpu.VMEM((2, TOP_K * tc * F32_SUB, LANES), F32), pltpu.SemaphoreType.DMA((2,))],
    )
    return pl.pallas_call(
        _combine_kernel,
        grid_spec=grid_spec,
        out_shape=jax.ShapeDtypeStruct((N, D_MODEL), F32),
        compiler_params=pltpu.CompilerParams(dimension_semantics=("arbitrary",), vmem_limit_bytes=VMEM_LIMIT),
        name="combine_norm",
    )(starts, lane3, rank3, lane3, rank3, x1, route, g, ys)


def _block_diag(w):
    nb, c, _ = w.shape
    eye = jnp.eye(nb, dtype=w.dtype)
    return (eye[:, None, :, None] * w[:, :, None, :]).reshape(nb * c, nb * c)


def _visit_plan(counts, n_rows):
    tm = MOE_TILE
    n_tiles = n_rows // tm
    n_visits = n_tiles + N_EXPERTS - 1
    cnt = counts.astype(jnp.int32)
    ends = jnp.cumsum(cnt)
    starts = ends - cnt
    first_tile = starts // tm
    n_vis = jnp.where(cnt > 0, (ends - 1) // tm - first_tile + 1, 0)
    v_end = jnp.cumsum(n_vis)
    v_start = v_end - n_vis
    total = v_end[-1]
    v = jnp.arange(n_visits, dtype=jnp.int32)
    owner = lambda q: jnp.minimum(jnp.searchsorted(v_end, q, side='right'), ROUTE_LANES - 1).astype(jnp.int32)
    lane = owner(v)
    valid = v < total
    tile = first_tile[lane] + (v - v_start[lane])
    lo = jnp.maximum(starts[lane], tile * tm)
    hi = jnp.minimum(ends[lane], (tile + 1) * tm)
    vt = jnp.where(valid, tile, n_tiles - 1)
    ve = jnp.where(valid, lane, owner(total - 1)) - EXPERT_LANE0
    zero = jnp.zeros_like(lo)
    return starts, vt, ve, jnp.where(valid, lo, zero), jnp.where(valid, hi, zero)


def kernel(x, norm_mix_g, w_in, lambda_qk, subln_g, conv_w, conv_b, w_r, b_r, w_i, b_i, lru_lambda, w_o_attn, w_o_lru, w_out, norm_ffn_g, w_group, w_expert_router, w_gate, w_up, w_down, final_norm_g):
    B, S, D = x.shape
    N = B * S
    nt = S // SEQ_TILE
    depth = norm_mix_g.shape[0]
    assert depth == 1 and D == D_MODEL and S % SEQ_TILE == 0 and S % LRU_TILE == 0
    assert N % ROW_TILE == 0 and (N * TOP_K) % MOE_TILE == 0
    l = 0
    row = lambda v: v.reshape(1, -1).astype(F32)

    x2 = x.reshape(N, D)
    w_in_l = w_in[l]
    qT, k, vT, xr, yr = _inproj(x2, row(norm_mix_g[l]), w_in_l[:, :PROJ_COLS].astype(BF16), B, S)

    attn = _attn(qT, k.reshape(B, nt, SEQ_TILE, QK_COLS), vT, lambda_qk[l].reshape(4, HEAD_DIM).astype(F32),
                 row(subln_g[l]), B, S)

    lru = _lru(xr.reshape(B, S, LRU_WIDTH), yr.reshape(B, S, LRU_WIDTH), conv_w[l].astype(F32), row(conv_b[l]),
               _block_diag(w_r[l]).astype(BF16), _block_diag(w_i[l]).astype(BF16), row(b_r[l]), row(b_i[l]),
               row(lru_lambda[l]), B, S)

    w_route = jnp.concatenate(
        [w_group[l], jnp.transpose(w_expert_router[l], (1, 0, 2)).reshape(D, N_EXPERTS),
         jnp.zeros((D, ROUTE_LANES - N_GROUPS - N_EXPERTS), F32)], axis=1).astype(F32)
    x1, h2p, route, counts = _merge(x2, attn.reshape(N, ATTN_WIDTH), lru.reshape(N, LRU_WIDTH), row(norm_mix_g[l]),
                                    w_in_l[:, PROJ_COLS:].astype(BF16), w_o_attn[l].astype(BF16),
                                    w_o_lru[l].astype(BF16), w_out[l].astype(BF16), row(norm_ffn_g[l]), w_route)

    starts, vt, ve, vlo, vhi = _visit_plan(counts[0], N * TOP_K)
    as_idx = lambda cols: cols.astype(jnp.int32).reshape(N // ROW_TILE, 1, TOP_K * ROW_TILE)
    lane3 = as_idx(route[:, 0:TOP_K])
    rank3 = as_idx(route[:, 2 * TOP_K:3 * TOP_K])

    xs = _dispatch(starts, lane3, rank3, h2p)
    ys = _moe(vt, ve, vlo, vhi, xs, w_gate[l], w_up[l], w_down[l])
    out = _combine(starts, lane3, rank3, x1, route, row(final_norm_g), ys)
    return out.reshape(B, S, D)
```

```python
import functools
import math

import jax
import jax.numpy as jnp
from jax import lax
from jax.experimental import pallas as pl
from jax.experimental.pallas import tpu as pltpu

F32 = jnp.float32
BF16 = jnp.bfloat16

D_MODEL = 1024
N_HEADS = 4
HEAD_DIM = 64
V_DIM = 2 * HEAD_DIM
ATTN_WIDTH = N_HEADS * V_DIM
V_ROWS = V_DIM + 16
K_COLS = 2 * V_DIM
POS_SPLIT = 3
POS_RADIX = 256
LRU_WIDTH = D_MODEL // 2
LRU_BLOCKS = 8
CONV_W = 4
LRU_C = 8.0
N_GROUPS = 4
EXPERTS_PER_GROUP = 8
N_EXPERTS = N_GROUPS * EXPERTS_PER_GROUP
TOP_K = 2
D_EXPERT = D_MODEL // 2
NORM_EPS = 1e-6
LAM_INIT = 0.8 - 0.6 * math.exp(-0.3 * 0)

QK_COLS = N_HEADS * 2 * HEAD_DIM
PROJ_COLS = 2 * QK_COLS + ATTN_WIDTH + 2 * LRU_WIDTH
ROUTE_LANES = 128
NEG_BIG = -1e30
LOG2E = math.log2(math.e)
ALIBI_SLOPES = tuple(2.0 ** (-8.0 * (h + 1) / N_HEADS) for h in range(N_HEADS))

SEQ_TILE = 512
LRU_TILE = 512
ROW_TILE = 256
MOE_TILE = 256
DMA_GROUP = 8
DISPATCH_SLOTS = 3
LANES = 128
PACK_SUB = D_MODEL // 2 // LANES
F32_SUB = D_MODEL // LANES
EXPERT_LANE0 = N_GROUPS
VMEM_LIMIT = 48 * 1024 * 1024


def _rms(x, g):
    return x * lax.rsqrt(jnp.mean(x * x, axis=-1, keepdims=True) + NORM_EPS) * g


def _dot(a, b):
    return jnp.dot(a, b, preferred_element_type=F32)


def _inproj_kernel(x_ref, g_ref, w_ref, qT_ref, k_ref, vT_ref, xr_ref, yr_ref):
    hb = _rms(x_ref[...], g_ref[...]).astype(BF16)

    def proj(lo, hi):
        return _dot(hb, w_ref[:, lo:hi])

    q = proj(0, QK_COLS) * (HEAD_DIM ** -0.5 * LOG2E)
    for h in range(N_HEADS):
        qT_ref[h] = q[:, h * V_DIM:(h + 1) * V_DIM].T.astype(BF16)
    k = proj(QK_COLS, 2 * QK_COLS).astype(BF16)
    r = lax.broadcasted_iota(jnp.int32, (k.shape[0], K_COLS - V_DIM), 0)
    lane = lax.broadcasted_iota(jnp.int32, r.shape, 1)
    a = r // POS_RADIX * POS_RADIX
    feat = jnp.where(lane < POS_SPLIT, a, jnp.where(lane < 2 * POS_SPLIT, r - a, 0)).astype(F32).astype(BF16)
    for h in range(N_HEADS):
        k_ref[:, h * K_COLS:h * K_COLS + V_DIM] = k[:, h * V_DIM:(h + 1) * V_DIM]
        k_ref[:, h * K_COLS + V_DIM:(h + 1) * K_COLS] = feat
    v = proj(2 * QK_COLS, 2 * QK_COLS + ATTN_WIDTH)
    for h in range(N_HEADS):
        vT_ref[h, :V_DIM, :] = v[:, h * V_DIM:(h + 1) * V_DIM].T.astype(BF16)
        pad_row = lax.broadcasted_iota(jnp.int32, (V_ROWS - V_DIM, v.shape[0]), 0)
        vT_ref[h, V_DIM:, :] = jnp.where(pad_row == 0, 1.0, 0.0).astype(BF16)
    c0 = 2 * QK_COLS + ATTN_WIDTH
    xr_ref[...] = proj(c0, c0 + LRU_WIDTH)
    yr_ref[...] = proj(c0 + LRU_WIDTH, c0 + 2 * LRU_WIDTH)


def _inproj(x2, g, w, B, S):
    N = B * S
    tm = SEQ_TILE
    nt = S // tm
    tile5 = pl.BlockSpec((None, N_HEADS, None, V_DIM, tm), lambda i: (i // nt, 0, i % nt, 0, 0))
    rows = lambda c: pl.BlockSpec((tm, c), lambda i: (i, 0))
    return pl.pallas_call(
        _inproj_kernel,
        grid=(N // tm,),
        in_specs=[rows(D_MODEL),
                  pl.BlockSpec((1, D_MODEL), lambda i: (0, 0)),
                  pl.BlockSpec((D_MODEL, PROJ_COLS), lambda i: (0, 0))],
        out_specs=[tile5, rows(N_HEADS * K_COLS),
                   pl.BlockSpec((None, N_HEADS, None, V_ROWS, tm), lambda i: (i // nt, 0, i % nt, 0, 0)),
                   rows(LRU_WIDTH), rows(LRU_WIDTH)],
        out_shape=[jax.ShapeDtypeStruct((B, N_HEADS, nt, V_DIM, tm), BF16),
                   jax.ShapeDtypeStruct((N, N_HEADS * K_COLS), BF16),
                   jax.ShapeDtypeStruct((B, N_HEADS, nt, V_ROWS, tm), BF16),
                   jax.ShapeDtypeStruct((N, LRU_WIDTH), F32),
                   jax.ShapeDtypeStruct((N, LRU_WIDTH), F32)],
        compiler_params=pltpu.CompilerParams(dimension_semantics=("parallel",),
                                             vmem_limit_bytes=VMEM_LIMIT),
        name="inproj",
    )(x2, g, w)


def _attn_kernel(qT_ref, k_ref, vT_ref, lam_ref, g_ref, o_ref, mask_ref, acc_ref, sa_ref, sb_ref, pa_ref, pb_ref):
    t = SEQ_TILE
    h = pl.program_id(1)
    i = pl.program_id(2)
    slope = LOG2E * jnp.where(h == 0, ALIBI_SLOPES[0], jnp.where(h == 1, ALIBI_SLOPES[1],
                              jnp.where(h == 2, ALIBI_SLOPES[2], ALIBI_SLOPES[3]))).astype(F32)
    last_off = jnp.maximum(i - 1, 0)

    @pl.when(i == 0)
    def _():
        r = lax.broadcasted_iota(jnp.int32, (t, t), 0)
        c = lax.broadcasted_iota(jnp.int32, (t, t), 1)
        mask_ref[...] = jnp.where(r <= c, 0.0, NEG_BIG)

    qf = qT_ref[...].astype(F32)
    row = lax.broadcasted_iota(jnp.int32, qf.shape, 0)
    sl = jnp.full(qf.shape, slope, F32)
    hi = sl.astype(BF16).astype(F32)
    mid = (sl - hi).astype(BF16).astype(F32)
    lo = (sl - hi - mid).astype(BF16).astype(F32)
    piece = jnp.where(row % 3 == 0, hi, jnp.where(row % 3 == 1, mid, lo))
    srows = jnp.where(row < 2 * POS_SPLIT, piece, 0.0).astype(BF16)
    qs = tuple(jnp.concatenate([jnp.where(sel, qf, 0.0).astype(BF16), srows], axis=0)
               for sel in (row < HEAD_DIM, row >= HEAD_DIM))
    acc_ref[...] = jnp.zeros_like(acc_ref)
    pb_ref[...] = jnp.zeros_like(pb_ref)

    def key_tile(tau):
        return jnp.where(tau <= 0, i, jnp.minimum(tau - 1, last_off))

    def stage_q(tau, s_ref, diagonal=False):
        kt = k_ref[key_tile(tau)]
        tile_max = []
        for mi in range(2):
            s = _dot(kt, qs[mi])
            if diagonal:
                s = s + mask_ref[...]
            s_ref[mi] = s
            tile_max.append(jnp.where(tau <= i, jnp.max(s, axis=0, keepdims=True), NEG_BIG))
        return tuple(tile_max)

    def stage_s(tau, s_ref, p_ref, ms, tile_max):
        cj = jnp.where(tau <= i, slope * (key_tile(tau) * t).astype(F32), NEG_BIG)
        m_out, alphas = [], []
        for mi in range(2):
            m_new = jnp.maximum(ms[mi], tile_max[mi] + cj)
            alphas.append(jnp.exp2(ms[mi] - m_new))
            p_ref[mi] = jnp.exp2(s_ref[mi] - (m_new - cj)).astype(BF16)
            m_out.append(m_new)
        return tuple(m_out), tuple(alphas)

    def stage_v(tau, p_ref, alphas):
        vt = vT_ref[key_tile(tau)]
        for mi in range(2):
            acc_ref[mi] = alphas[mi] * acc_ref[mi] + _dot(vt, p_ref[mi])

    def body(jj, carry):
        ms, alphas, tmax = carry[:2], carry[2:4], carry[4:]
        tau = 2 * jj
        tmax_b = stage_q(tau + 1, sb_ref)
        stage_v(tau - 1, pb_ref, alphas)
        ms, alphas = stage_s(tau, sa_ref, pa_ref, ms, tmax)
        tmax_a = stage_q(tau + 2, sa_ref)
        stage_v(tau, pa_ref, alphas)
        ms, alphas = stage_s(tau + 1, sb_ref, pb_ref, ms, tmax_b)
        return ms + alphas + tmax_a

    m_init = jnp.full((1, t), NEG_BIG, F32)
    one = jnp.ones((1, t), F32)
    tmax0 = stage_q(0, sa_ref, diagonal=True)
    n_pairs = jnp.right_shift(i + 2, 1)
    fin = lax.fori_loop(0, n_pairs, body, (m_init, m_init, one, one) + tmax0)
    stage_v(2 * n_pairs - 1, pb_ref, fin[2:4])

    lp = lam_ref[...]
    s1 = jnp.sum(lp[0:1] * lp[1:2], axis=-1, keepdims=True)
    s2 = jnp.sum(lp[2:3] * lp[3:4], axis=-1, keepdims=True)
    lam = jnp.exp(s1) - jnp.exp(s2) + LAM_INIT
    norm = [acc_ref[mi, :V_DIM, :] * (1.0 / acc_ref[mi, V_DIM:V_DIM + 1, :]) for mi in range(2)]
    oT = norm[0] - lam * norm[1]
    o = _rms(oT.T, g_ref[...]) * (1.0 - LAM_INIT)
    o_ref[...] = o.astype(BF16)


def _attn(qT, k4, vT, lam, g, B, S):
    t = SEQ_TILE
    nt = S // t
    return pl.pallas_call(
        _attn_kernel,
        grid=(B, N_HEADS, nt),
        in_specs=[pl.BlockSpec((None, None, None, V_DIM, t), lambda b, h, i: (b, h, i, 0, 0)),
                  pl.BlockSpec((None, nt, t, K_COLS), lambda b, h, i: (b, 0, 0, h)),
                  pl.BlockSpec((None, None, nt, V_ROWS, t), lambda b, h, i: (b, h, 0, 0, 0)),
                  pl.BlockSpec((4, HEAD_DIM), lambda b, h, i: (0, 0)),
                  pl.BlockSpec((1, V_DIM), lambda b, h, i: (0, 0))],
        out_specs=pl.BlockSpec((None, t, V_DIM), lambda b, h, i: (b, i, h)),
        out_shape=jax.ShapeDtypeStruct((B, S, ATTN_WIDTH), BF16),
        scratch_shapes=[pltpu.VMEM((t, t), F32), pltpu.VMEM((2, V_ROWS, t), F32),
                        pltpu.VMEM((2, t, t), F32), pltpu.VMEM((2, t, t), F32),
                        pltpu.VMEM((2, t, t), BF16), pltpu.VMEM((2, t, t), BF16)],
        compiler_params=pltpu.CompilerParams(dimension_semantics=("parallel", "arbitrary", "arbitrary"),
                                             vmem_limit_bytes=VMEM_LIMIT),
        name="diff_attn",
    )(qT, k4, vT, lam, g)


def _lru_kernel(xr_ref, yr_ref, cw_ref, cb_ref, wr_ref, wi_ref, br_ref, bi_ref, lam_ref, o_ref, xbuf, hc):
    T = LRU_TILE
    ti = pl.program_id(1)

    @pl.when(ti == 0)
    def _():
        xbuf[0:8] = jnp.zeros((8, LRU_WIDTH), F32)
        hc[...] = jnp.zeros_like(hc)

    x = xr_ref[...]
    xbuf[8:8 + T] = x
    cw = cw_ref[...]
    xc = cb_ref[...] + cw[3:4] * x
    for j in range(CONV_W - 1):
        xc = xc + cw[j:j + 1] * xbuf[5 + j:5 + j + T]
    xbuf[0:8] = x[T - 8:T]

    xb = xc.astype(BF16)
    r = jax.nn.sigmoid(_dot(xb, wr_ref[...]) + br_ref[...])
    ig = jax.nn.sigmoid(_dot(xb, wi_ref[...]) + bi_ref[...])
    z = -lam_ref[...]
    softplus = jnp.maximum(z, 0.0) + jnp.log1p(jnp.exp(-jnp.abs(z)))
    la = -LRU_C * r * softplus
    a = jnp.exp(la)
    mult = jnp.sqrt(-jnp.tanh(la) * (a * a + 1.0))
    row = lax.broadcasted_iota(jnp.int32, (T, LRU_WIDTH), 0)
    mult = jnp.where((row == 0) & (ti == 0), 1.0, mult)
    u = (xc * ig) * mult

    d = 1
    while d < T:
        valid = row >= d
        u = jnp.where(valid, a * pltpu.roll(u, d, 0) + u, u)
        a = jnp.where(valid, a * pltpu.roll(a, d, 0), a)
        d *= 2
    hfull = u + a * hc[...]
    hc[...] = hfull[T - 1:T]
    y = yr_ref[...]
    gelu = 0.5 * y * (1.0 + jnp.tanh(0.7978845608028654 * (y + 0.044715 * (y * y * y))))
    o_ref[...] = (hfull * gelu).astype(BF16)


def _lru(xr, yr, cw, cb, wr, wi, br, bi, lam, B, S):
    T = LRU_TILE
    seq = pl.BlockSpec((None, T, LRU_WIDTH), lambda b, t: (b, t, 0))
    full = lambda r, c: pl.BlockSpec((r, c), lambda b, t: (0, 0))
    return pl.pallas_call(
        _lru_kernel,
        grid=(B, S // T),
        in_specs=[seq, seq, full(CONV_W, LRU_WIDTH), full(1, LRU_WIDTH), full(LRU_WIDTH, LRU_WIDTH),
                  full(LRU_WIDTH, LRU_WIDTH), full(1, LRU_WIDTH), full(1, LRU_WIDTH), full(1, LRU_WIDTH)],
        out_specs=seq,
        out_shape=jax.ShapeDtypeStruct((B, S, LRU_WIDTH), BF16),
        scratch_shapes=[pltpu.VMEM((T + 8, LRU_WIDTH), F32), pltpu.VMEM((1, LRU_WIDTH), F32)],
        compiler_params=pltpu.CompilerParams(dimension_semantics=("arbitrary", "arbitrary"),
                                             vmem_limit_bytes=VMEM_LIMIT),
        name="rg_lru",
    )(xr, yr, cw, cb, wr, wi, br, bi, lam)


def _pack_rows(v, out_ref):
    bits = pltpu.bitcast(v.astype(BF16).astype(F32), jnp.uint32)
    half = D_MODEL // 2
    packed = (bits[:, :half] >> 16) | (bits[:, half:] & jnp.uint32(0xFFFF0000))
    for c in range(PACK_SUB):
        out_ref[pl.ds(c, v.shape[0], stride=PACK_SUB), :] = packed[:, c * LANES:(c + 1) * LANES]


def _unpack_rows(in_ref, rows):
    lo, hi = [], []
    for c in range(PACK_SUB):
        w = in_ref[pl.ds(c, rows, stride=PACK_SUB), :]
        lo.append(pltpu.bitcast(w << 16, F32))
        hi.append(pltpu.bitcast(w & jnp.uint32(0xFFFF0000), F32))
    return jnp.concatenate(lo + hi, axis=1).astype(BF16)


def _merge_kernel(x_ref, attn_ref, lru_ref, g1_ref, wg_ref, woa_ref, wol_ref, wout_ref, g2_ref, wrt_ref,
                  x1_ref, h2p_ref, route_ref, counts_ref, cnt, tri):
    tm = SEQ_TILE

    @pl.when(pl.program_id(0) == 0)
    def _():
        cnt[...] = jnp.zeros_like(cnt)
        r = lax.broadcasted_iota(jnp.int32, (tm, tm), 0)
        c = lax.broadcasted_iota(jnp.int32, (tm, tm), 1)
        tri[...] = jnp.where(c < r, 1.0, 0.0).astype(BF16)

    x = x_ref[...]
    hb = _rms(x, g1_ref[...]).astype(BF16)
    gates = jax.nn.sigmoid(_dot(hb, wg_ref[...]))
    merged = gates[:, :D_MODEL] * _dot(attn_ref[...], woa_ref[...]) + gates[:, D_MODEL:] * _dot(lru_ref[...], wol_ref[...])
    x1 = x + _dot(merged.astype(BF16), wout_ref[...])
    x1_ref[...] = x1
    h2 = _rms(x1, g2_ref[...])
    _pack_rows(h2, h2p_ref)

    logits = jnp.dot(h2, wrt_ref[...], preferred_element_type=F32, precision=lax.Precision.HIGHEST)
    lane = lax.broadcasted_iota(jnp.int32, logits.shape, 1)
    big = jnp.int32(1 << 20)

    def first_argmax(v):
        m = jnp.max(v, axis=-1, keepdims=True)
        return m, jnp.min(jnp.where(v == m, lane, big), axis=-1, keepdims=True)

    gmask = lane < N_GROUPS
    gmax, gidx = first_argmax(jnp.where(gmask, logits, -jnp.inf))
    gsum = jnp.sum(jnp.where(gmask, jnp.exp(logits - gmax), 0.0), axis=-1, keepdims=True)
    g_w = 1.0 / gsum
    lo = N_GROUPS + EXPERTS_PER_GROUP * gidx
    el = jnp.where((lane >= lo) & (lane < lo + EXPERTS_PER_GROUP), logits, -jnp.inf)
    m1, i1 = first_argmax(el)
    m2, i2 = first_argmax(jnp.where(lane == i1, -jnp.inf, el))
    rr = jnp.exp(m2 - m1)
    w1 = g_w / (1.0 + rr)
    w2 = g_w * rr / (1.0 + rr)
    oh1 = lane == i1
    oh2 = lane == i2
    oh = jnp.where(oh1 | oh2, 1.0, 0.0)
    before = _dot(tri[...], oh.astype(BF16)) + cnt[...]
    r1 = jnp.sum(jnp.where(oh1, before, 0.0), axis=-1, keepdims=True)
    r2 = jnp.sum(jnp.where(oh2, before, 0.0), axis=-1, keepdims=True)
    cnt[...] = cnt[...] + jnp.sum(oh, axis=0, keepdims=True)
    counts_ref[...] = cnt[...]
    vals = (i1.astype(F32), i2.astype(F32), w1, w2, r1, r2)
    route = jnp.zeros_like(logits)
    for k, v in enumerate(vals):
        route = jnp.where(lane == k, v, route)
    route_ref[...] = route


def _merge(x2, attn, lru, g1, wg, woa, wol, wout, g2, wrt):
    N = x2.shape[0]
    tm = SEQ_TILE
    rows = lambda c: pl.BlockSpec((tm, c), lambda i: (i, 0))
    full = lambda r, c: pl.BlockSpec((r, c), lambda i: (0, 0))
    return pl.pallas_call(
        _merge_kernel,
        grid=(N // tm,),
        in_specs=[rows(D_MODEL), rows(ATTN_WIDTH), rows(LRU_WIDTH), full(1, D_MODEL), full(D_MODEL, 2 * D_MODEL),
                  full(ATTN_WIDTH, D_MODEL), full(LRU_WIDTH, D_MODEL), full(D_MODEL, D_MODEL), full(1, D_MODEL),
                  full(D_MODEL, ROUTE_LANES)],
        out_specs=[rows(D_MODEL), pl.BlockSpec((tm * PACK_SUB, LANES), lambda i: (i, 0)), rows(ROUTE_LANES),
                   full(1, ROUTE_LANES)],
        out_shape=[jax.ShapeDtypeStruct((N, D_MODEL), F32), jax.ShapeDtypeStruct((N * PACK_SUB, LANES), jnp.uint32),
                   jax.ShapeDtypeStruct((N, ROUTE_LANES), F32), jax.ShapeDtypeStruct((1, ROUTE_LANES), F32)],
        scratch_shapes=[pltpu.VMEM((1, ROUTE_LANES), F32), pltpu.VMEM((tm, tm), BF16)],
        compiler_params=pltpu.CompilerParams(dimension_semantics=("arbitrary",), vmem_limit_bytes=VMEM_LIMIT),
        name="merge_route",
    )(x2, attn, lru, g1, wg, woa, wol, wout, g2, wrt)


def _for_each_assignment(starts_ref, lane_ref, rank_ref, fn):
    def group(gi, _):
        toks = [gi * DMA_GROUP + j for j in range(DMA_GROUP)]
        pos = [[starts_ref[lane_ref[0, 0, tk * TOP_K + k]] + rank_ref[0, 0, tk * TOP_K + k] for k in range(TOP_K)]
               for tk in toks]
        for tk, p in zip(toks, pos):
            for k in range(TOP_K):
                fn(tk, k, p[k])
        return 0
    lax.fori_loop(0, ROW_TILE // DMA_GROUP, group, 0)


def _dispatch_kernel(starts_ref, lane_ref, rank_ref, h2p_hbm, xs_hbm, xin, in_sem, out_sem):
    tile_rows = ROW_TILE * PACK_SUB
    i = pl.program_id(0)
    n = pl.num_programs(0)
    slot = lax.rem(i, DISPATCH_SLOTS)

    def fetch(tile, s):
        return pltpu.make_async_copy(h2p_hbm.at[pl.ds(pl.multiple_of(tile * tile_rows, tile_rows), tile_rows)],
                                     xin.at[s], in_sem.at[s])

    def drain(s):
        for _ in range(TOP_K):
            pltpu.make_async_copy(xin.at[s], xs_hbm.at[pl.ds(0, tile_rows)], out_sem.at[s]).wait()

    @pl.when(i == 0)
    def _():
        fetch(0, 0).start()

    @pl.when(i + 1 < n)
    def _():
        fetch(i + 1, lax.rem(i + 1, DISPATCH_SLOTS)).start()

    fetch(i, slot).wait()

    def start(tk, k, pos):
        pltpu.make_async_copy(xin.at[slot, pl.ds(pl.multiple_of(tk * PACK_SUB, PACK_SUB), PACK_SUB)],
                              xs_hbm.at[pl.ds(pl.multiple_of(pos * PACK_SUB, PACK_SUB), PACK_SUB)],
                              out_sem.at[slot]).start()

    _for_each_assignment(starts_ref, lane_ref, rank_ref, start)

    @pl.when(i > 0)
    def _():
        drain(lax.rem(i + DISPATCH_SLOTS - 1, DISPATCH_SLOTS))

    @pl.when(i == n - 1)
    def _():
        drain(slot)


def _dispatch(starts, lane3, rank3, h2p):
    nt = lane3.shape[0]
    idx = pl.BlockSpec((1, 1, TOP_K * ROW_TILE), lambda i, st: (i, 0, 0), memory_space=pltpu.SMEM)
    grid_spec = pltpu.PrefetchScalarGridSpec(
        num_scalar_prefetch=1,
        grid=(nt,),
        in_specs=[idx, idx, pl.BlockSpec(memory_space=pl.ANY)],
        out_specs=pl.BlockSpec(memory_space=pl.ANY),
        scratch_shapes=[pltpu.VMEM((DISPATCH_SLOTS, ROW_TILE * PACK_SUB, LANES), jnp.uint32),
                        pltpu.SemaphoreType.DMA((DISPATCH_SLOTS,)), pltpu.SemaphoreType.DMA((DISPATCH_SLOTS,))],
    )
    return pl.pallas_call(
        _dispatch_kernel,
        grid_spec=grid_spec,
        out_shape=jax.ShapeDtypeStruct((TOP_K * h2p.shape[0], LANES), jnp.uint32),
        compiler_params=pltpu.CompilerParams(dimension_semantics=("arbitrary",), has_side_effects=True),
        name="moe_dispatch",
    )(starts, lane3, rank3, h2p)


def _moe_kernel(vt_ref, ve_ref, vlo_ref, vhi_ref, xs_ref, wg_ref, wu_ref, wd_ref, ys_ref, wgb, wub, wdb, acc):
    tm = MOE_TILE
    v = pl.program_id(0)
    nv = pl.num_programs(0)
    t = vt_ref[v]
    e = ve_ref[v]
    prev = jnp.maximum(v - 1, 0)
    nxt = jnp.minimum(v + 1, nv - 1)

    @pl.when((v == 0) | (ve_ref[prev] != e))
    def _():
        wgb[...] = wg_ref[...].astype(BF16)
        wub[...] = wu_ref[...].astype(BF16)
        wdb[...] = wd_ref[...].astype(BF16)

    xb = _unpack_rows(xs_ref, tm)
    g = _dot(xb, wgb[...])
    u = _dot(xb, wub[...])
    hmid = (g * jax.nn.sigmoid(g)) * u
    y = _dot(hmid.astype(BF16), wdb[...])
    rows = t * tm + lax.broadcasted_iota(jnp.int32, (tm, 1), 0)
    y = jnp.where((rows >= vlo_ref[v]) & (rows < vhi_ref[v]), y, 0.0)

    first = (v == 0) | (vt_ref[prev] != t)

    @pl.when(first)
    def _():
        acc[...] = y

    @pl.when(jnp.logical_not(first))
    def _():
        acc[...] = acc[...] + y

    @pl.when((v == nv - 1) | (vt_ref[nxt] != t))
    def _():
        for c in range(F32_SUB):
            ys_ref[pl.ds(c, tm, stride=F32_SUB), :] = acc[:, c * LANES:(c + 1) * LANES]


def _moe(vt, ve, vlo, vhi, xs, wg, wu, wd):
    tm = MOE_TILE
    n_rows = xs.shape[0] // PACK_SUB
    grid_spec = pltpu.PrefetchScalarGridSpec(
        num_scalar_prefetch=4,
        grid=(vt.shape[0],),
        in_specs=[pl.BlockSpec((tm * PACK_SUB, LANES), lambda v, vt, ve, lo, hi: (vt[v], 0)),
                  pl.BlockSpec((None, D_MODEL, D_EXPERT), lambda v, vt, ve, lo, hi: (ve[v], 0, 0)),
                  pl.BlockSpec((None, D_MODEL, D_EXPERT), lambda v, vt, ve, lo, hi: (ve[v], 0, 0)),
                  pl.BlockSpec((None, D_EXPERT, D_MODEL), lambda v, vt, ve, lo, hi: (ve[v], 0, 0))],
        out_specs=pl.BlockSpec((tm * F32_SUB, LANES), lambda v, vt, ve, lo, hi: (vt[v], 0)),
        scratch_shapes=[pltpu.VMEM((D_MODEL, D_EXPERT), BF16), pltpu.VMEM((D_MODEL, D_EXPERT), BF16),
                        pltpu.VMEM((D_EXPERT, D_MODEL), BF16), pltpu.VMEM((tm, D_MODEL), F32)],
    )
    return pl.pallas_call(
        _moe_kernel,
        grid_spec=grid_spec,
        out_shape=jax.ShapeDtypeStruct((n_rows * F32_SUB, LANES), F32),
        compiler_params=pltpu.CompilerParams(dimension_semantics=("arbitrary",), vmem_limit_bytes=VMEM_LIMIT),
        name="moe_experts",
    )(vt, ve, vlo, vhi, xs, wg, wu, wd)


def _combine_kernel(starts_ref, lane_ref, rank_ref, lanen_ref, rankn_ref, x1_ref, route_ref, g_ref, ys_hbm, o_ref,
                    ybuf, sem):
    tc = ROW_TILE
    i = pl.program_id(0)
    n = pl.num_programs(0)
    slot = lax.rem(i, 2)

    def gather(l_ref, r_ref, s):
        def start(tk, k, pos):
            pltpu.make_async_copy(ys_hbm.at[pl.ds(pl.multiple_of(pos * F32_SUB, F32_SUB), F32_SUB)],
                                  ybuf.at[s, pl.ds(pl.multiple_of((k * tc + tk) * F32_SUB, F32_SUB), F32_SUB)],
                                  sem.at[s]).start()
        _for_each_assignment(starts_ref, l_ref, r_ref, start)

    @pl.when(i == 0)
    def _():
        gather(lane_ref, rank_ref, 0)

    @pl.when(i + 1 < n)
    def _():
        gather(lanen_ref, rankn_ref, 1 - slot)

    pltpu.make_async_copy(ys_hbm.at[pl.ds(0, TOP_K * tc * F32_SUB)], ybuf.at[slot], sem.at[slot]).wait()

    route = route_ref[...]
    w = [route[:, TOP_K + k:TOP_K + k + 1] for k in range(TOP_K)]
    z = []
    for c in range(F32_SUB):
        zc = x1_ref[:, c * LANES:(c + 1) * LANES]
        for k in range(TOP_K):
            zc = zc + w[k] * ybuf.at[slot][pl.ds(k * tc * F32_SUB + c, tc, stride=F32_SUB), :]
        z.append(zc)
    ss = sum(jnp.sum(zc * zc, axis=-1, keepdims=True) for zc in z)
    inv = lax.rsqrt(ss * (1.0 / D_MODEL) + NORM_EPS)
    for c in range(F32_SUB):
        o_ref[:, c * LANES:(c + 1) * LANES] = z[c] * inv * g_ref[:, c * LANES:(c + 1) * LANES]


def _combine(starts, lane3, rank3, x1, route, g, ys):
    N = x1.shape[0]
    tc = ROW_TILE
    nt = N // tc
    idx = lambda f: pl.BlockSpec((1, 1, TOP_K * tc), f, memory_space=pltpu.SMEM)
    cur = idx(lambda i, st: (i, 0, 0))
    nxt = idx(lambda i, st: (jnp.minimum(i + 1, nt - 1), 0, 0))
    grid_spec = pltpu.PrefetchScalarGridSpec(
        num_scalar_prefetch=1,
        grid=(nt,),
        in_specs=[cur, cur, nxt, nxt,
                  pl.BlockSpec((tc, D_MODEL), lambda i, st: (i, 0)),
                  pl.BlockSpec((tc, ROUTE_LANES), lambda i, st: (i, 0)),
                  pl.BlockSpec((1, D_MODEL), lambda i, st: (0, 0)),
                  pl.BlockSpec(memory_space=pl.ANY)],
        out_specs=pl.BlockSpec((tc, D_MODEL), lambda i, st: (i, 0)),
        scratch_shapes=[pltpu.VMEM((2, TOP_K * tc * F32_SUB, LANES), F32), pltpu.SemaphoreType.DMA((2,))],
    )
    return pl.pallas_call(
        _combine_kernel,
        grid_spec=grid_spec,
        out_shape=jax.ShapeDtypeStruct((N, D_MODEL), F32),
        compiler_params=pltpu.CompilerParams(dimension_semantics=("arbitrary",), vmem_limit_bytes=VMEM_LIMIT),
        name="combine_norm",
    )(starts, lane3, rank3, lane3, rank3, x1, route, g, ys)


def _block_diag(w):
    nb, c, _ = w.shape
    eye = jnp.eye(nb, dtype=w.dtype)
    return (eye[:, None, :, None] * w[:, :, None, :]).reshape(nb * c, nb * c)


def _visit_plan(counts, n_rows):
    tm = MOE_TILE
    n_tiles = n_rows // tm
    n_visits = n_tiles + N_EXPERTS - 1
    cnt = counts.astype(jnp.int32)
    ends = jnp.cumsum(cnt)
    starts = ends - cnt
    first_tile = starts // tm
    n_vis = jnp.where(cnt > 0, (ends - 1) // tm - first_tile + 1, 0)
    v_end = jnp.cumsum(n_vis)
    v_start = v_end - n_vis
    total = v_end[-1]
    v = jnp.arange(n_visits, dtype=jnp.int32)
    owner = lambda q: jnp.minimum(jnp.searchsorted(v_end, q, side='right'), ROUTE_LANES - 1).astype(jnp.int32)
    lane = owner(v)
    valid = v < total
    tile = first_tile[lane] + (v - v_start[lane])
    lo = jnp.maximum(starts[lane], tile * tm)
    hi = jnp.minimum(ends[lane], (tile + 1) * tm)
    vt = jnp.where(valid, tile, n_tiles - 1)
    ve = jnp.where(valid, lane, owner(total - 1)) - EXPERT_LANE0
    zero = jnp.zeros_like(lo)
    return starts, vt, ve, jnp.where(valid, lo, zero), jnp.where(valid, hi, zero)


def kernel(x, norm_mix_g, w_in, lambda_qk, subln_g, conv_w, conv_b, w_r, b_r, w_i, b_i, lru_lambda, w_o_attn, w_o_lru, w_out, norm_ffn_g, w_group, w_expert_router, w_gate, w_up, w_down, final_norm_g):
    B, S, D = x.shape
    N = B * S
    nt = S // SEQ_TILE
    depth = norm_mix_g.shape[0]
    assert depth == 1 and D == D_MODEL and S % SEQ_TILE == 0 and S % LRU_TILE == 0
    assert N % ROW_TILE == 0 and (N * TOP_K) % MOE_TILE == 0
    l = 0
    row = lambda v: v.reshape(1, -1).astype(F32)

    x2 = x.reshape(N, D)
    w_in_l = w_in[l]
    qT, k, vT, xr, yr = _inproj(x2, row(norm_mix_g[l]), w_in_l[:, :PROJ_COLS].astype(BF16), B, S)

    attn = _attn(qT, k.reshape(B, nt, SEQ_TILE, N_HEADS * K_COLS), vT, lambda_qk[l].reshape(4, HEAD_DIM).astype(F32),
                 row(subln_g[l]), B, S)

    lru = _lru(xr.reshape(B, S, LRU_WIDTH), yr.reshape(B, S, LRU_WIDTH), conv_w[l].astype(F32), row(conv_b[l]),
               _block_diag(w_r[l]).astype(BF16), _block_diag(w_i[l]).astype(BF16), row(b_r[l]), row(b_i[l]),
               row(lru_lambda[l]), B, S)

    w_route = jnp.concatenate(
        [w_group[l], jnp.transpose(w_expert_router[l], (1, 0, 2)).reshape(D, N_EXPERTS),
         jnp.zeros((D, ROUTE_LANES - N_GROUPS - N_EXPERTS), F32)], axis=1).astype(F32)
    x1, h2p, route, counts = _merge(x2, attn.reshape(N, ATTN_WIDTH), lru.reshape(N, LRU_WIDTH), row(norm_mix_g[l]),
                                    w_in_l[:, PROJ_COLS:].astype(BF16), w_o_attn[l].astype(BF16),
                                    w_o_lru[l].astype(BF16), w_out[l].astype(BF16), row(norm_ffn_g[l]), w_route)

    starts, vt, ve, vlo, vhi = _visit_plan(counts[0], N * TOP_K)
    as_idx = lambda cols: cols.astype(jnp.int32).reshape(N // ROW_TILE, 1, TOP_K * ROW_TILE)
    lane3 = as_idx(route[:, 0:TOP_K])
    rank3 = as_idx(route[:, 2 * TOP_K:3 * TOP_K])

    xs = _dispatch(starts, lane3, rank3, h2p)
    ys = _moe(vt, ve, vlo, vhi, xs, w_gate[l], w_up[l], w_down[l])
    out = _combine(starts, lane3, rank3, x1, route, row(final_norm_g), ys)
    return out.reshape(B, S, D)
```

```python
import functools
import math

import jax
import jax.numpy as jnp
from jax import lax
from jax.experimental import pallas as pl
from jax.experimental.pallas import tpu as pltpu

F32 = jnp.float32
BF16 = jnp.bfloat16

D_MODEL = 1024
N_HEADS = 4
HEAD_DIM = 64
V_DIM = 2 * HEAD_DIM
ATTN_WIDTH = N_HEADS * V_DIM
V_ROWS = V_DIM + 16
K_COLS = 2 * V_DIM
POS_SPLIT = 3
POS_RADIX = 256
LRU_WIDTH = D_MODEL // 2
LRU_BLOCKS = 8
CONV_W = 4
LRU_C = 8.0
N_GROUPS = 4
EXPERTS_PER_GROUP = 8
N_EXPERTS = N_GROUPS * EXPERTS_PER_GROUP
TOP_K = 2
D_EXPERT = D_MODEL // 2
NORM_EPS = 1e-6
LAM_INIT = 0.8 - 0.6 * math.exp(-0.3 * 0)

QK_COLS = N_HEADS * 2 * HEAD_DIM
PROJ_COLS = 2 * QK_COLS + ATTN_WIDTH + 2 * LRU_WIDTH
ROUTE_LANES = 128
NEG_BIG = -1e30
LOG2E = math.log2(math.e)
ALIBI_SLOPES = tuple(2.0 ** (-8.0 * (h + 1) / N_HEADS) for h in range(N_HEADS))

SEQ_TILE = 512
LRU_TILE = 512
ROW_TILE = 256
MOE_TILE = 256
DMA_GROUP = 8
DISPATCH_SLOTS = 3
LANES = 128
PACK_SUB = D_MODEL // 2 // LANES
EXPERT_LANE0 = N_GROUPS
VMEM_LIMIT = 48 * 1024 * 1024


def _rms(x, g):
    return x * lax.rsqrt(jnp.mean(x * x, axis=-1, keepdims=True) + NORM_EPS) * g


def _dot(a, b):
    return jnp.dot(a, b, preferred_element_type=F32)


def _inproj_kernel(x_ref, g_ref, w_ref, qT_ref, k_ref, vT_ref, xr_ref, yr_ref):
    hb = _rms(x_ref[...], g_ref[...]).astype(BF16)

    def proj(lo, hi):
        return _dot(hb, w_ref[:, lo:hi])

    q = proj(0, QK_COLS) * (HEAD_DIM ** -0.5 * LOG2E)
    for h in range(N_HEADS):
        qT_ref[h] = q[:, h * V_DIM:(h + 1) * V_DIM].T.astype(BF16)
    k = proj(QK_COLS, 2 * QK_COLS).astype(BF16)
    r = lax.broadcasted_iota(jnp.int32, (k.shape[0], K_COLS - V_DIM), 0)
    lane = lax.broadcasted_iota(jnp.int32, r.shape, 1)
    a = r // POS_RADIX * POS_RADIX
    feat = jnp.where(lane < POS_SPLIT, a, jnp.where(lane < 2 * POS_SPLIT, r - a, 0)).astype(F32).astype(BF16)
    for h in range(N_HEADS):
        k_ref[:, h * K_COLS:h * K_COLS + V_DIM] = k[:, h * V_DIM:(h + 1) * V_DIM]
        k_ref[:, h * K_COLS + V_DIM:(h + 1) * K_COLS] = feat
    v = proj(2 * QK_COLS, 2 * QK_COLS + ATTN_WIDTH)
    for h in range(N_HEADS):
        vT_ref[h, :V_DIM, :] = v[:, h * V_DIM:(h + 1) * V_DIM].T.astype(BF16)
        pad_row = lax.broadcasted_iota(jnp.int32, (V_ROWS - V_DIM, v.shape[0]), 0)
        vT_ref[h, V_DIM:, :] = jnp.where(pad_row == 0, 1.0, 0.0).astype(BF16)
    c0 = 2 * QK_COLS + ATTN_WIDTH
    xr_ref[...] = proj(c0, c0 + LRU_WIDTH)
    yr_ref[...] = proj(c0 + LRU_WIDTH, c0 + 2 * LRU_WIDTH)


def _inproj(x2, g, w, B, S):
    N = B * S
    tm = SEQ_TILE
    nt = S // tm
    tile5 = pl.BlockSpec((None, N_HEADS, None, V_DIM, tm), lambda i: (i // nt, 0, i % nt, 0, 0))
    rows = lambda c: pl.BlockSpec((tm, c), lambda i: (i, 0))
    return pl.pallas_call(
        _inproj_kernel,
        grid=(N // tm,),
        in_specs=[rows(D_MODEL),
                  pl.BlockSpec((1, D_MODEL), lambda i: (0, 0)),
                  pl.BlockSpec((D_MODEL, PROJ_COLS), lambda i: (0, 0))],
        out_specs=[tile5, rows(N_HEADS * K_COLS),
                   pl.BlockSpec((None, N_HEADS, None, V_ROWS, tm), lambda i: (i // nt, 0, i % nt, 0, 0)),
                   rows(LRU_WIDTH), rows(LRU_WIDTH)],
        out_shape=[jax.ShapeDtypeStruct((B, N_HEADS, nt, V_DIM, tm), BF16),
                   jax.ShapeDtypeStruct((N, N_HEADS * K_COLS), BF16),
                   jax.ShapeDtypeStruct((B, N_HEADS, nt, V_ROWS, tm), BF16),
                   jax.ShapeDtypeStruct((N, LRU_WIDTH), F32),
                   jax.ShapeDtypeStruct((N, LRU_WIDTH), F32)],
        compiler_params=pltpu.CompilerParams(dimension_semantics=("parallel",),
                                             vmem_limit_bytes=VMEM_LIMIT),
        name="inproj",
    )(x2, g, w)


def _attn_kernel(qT_ref, k_ref, vT_ref, lam_ref, g_ref, o_ref, mask_ref, acc_ref, sa_ref, sb_ref, pa_ref, pb_ref):
    t = SEQ_TILE
    h = pl.program_id(1)
    i = pl.program_id(2)
    slope = LOG2E * jnp.where(h == 0, ALIBI_SLOPES[0], jnp.where(h == 1, ALIBI_SLOPES[1],
                              jnp.where(h == 2, ALIBI_SLOPES[2], ALIBI_SLOPES[3]))).astype(F32)
    last_off = jnp.maximum(i - 1, 0)

    @pl.when(i == 0)
    def _():
        r = lax.broadcasted_iota(jnp.int32, (t, t), 0)
        c = lax.broadcasted_iota(jnp.int32, (t, t), 1)
        mask_ref[...] = jnp.where(r <= c, 0.0, NEG_BIG)

    qf = qT_ref[...].astype(F32)
    row = lax.broadcasted_iota(jnp.int32, qf.shape, 0)
    sl = jnp.full(qf.shape, slope, F32)
    hi = sl.astype(BF16).astype(F32)
    mid = (sl - hi).astype(BF16).astype(F32)
    lo = (sl - hi - mid).astype(BF16).astype(F32)
    piece = jnp.where(row % 3 == 0, hi, jnp.where(row % 3 == 1, mid, lo))
    srows = jnp.where(row < 2 * POS_SPLIT, piece, 0.0).astype(BF16)
    qs = tuple(jnp.concatenate([jnp.where(sel, qf, 0.0).astype(BF16), srows], axis=0)
               for sel in (row < HEAD_DIM, row >= HEAD_DIM))
    acc_ref[...] = jnp.zeros_like(acc_ref)
    pb_ref[...] = jnp.zeros_like(pb_ref)

    def key_tile(tau):
        return jnp.where(tau <= 0, i, jnp.minimum(tau - 1, last_off))

    def stage_q(tau, s_ref, diagonal=False):
        kt = k_ref[key_tile(tau)]
        tile_max = []
        for mi in range(2):
            s = _dot(kt, qs[mi])
            if diagonal:
                s = s + mask_ref[...]
            s_ref[mi] = s
            tile_max.append(jnp.where(tau <= i, jnp.max(s, axis=0, keepdims=True), NEG_BIG))
        return tuple(tile_max)

    def stage_s(tau, s_ref, p_ref, ms, tile_max):
        cj = jnp.where(tau <= i, slope * (key_tile(tau) * t).astype(F32), NEG_BIG)
        m_out, alphas = [], []
        for mi in range(2):
            m_new = jnp.maximum(ms[mi], tile_max[mi] + cj)
            alphas.append(jnp.exp2(ms[mi] - m_new))
            p_ref[mi] = jnp.exp2(s_ref[mi] - (m_new - cj)).astype(BF16)
            m_out.append(m_new)
        return tuple(m_out), tuple(alphas)

    def stage_v(tau, p_ref, alphas):
        vt = vT_ref[key_tile(tau)]
        for mi in range(2):
            acc_ref[mi] = alphas[mi] * acc_ref[mi] + _dot(vt, p_ref[mi])

    def body(jj, carry):
        ms, alphas, tmax = carry[:2], carry[2:4], carry[4:]
        tau = 2 * jj
        tmax_b = stage_q(tau + 1, sb_ref)
        stage_v(tau - 1, pb_ref, alphas)
        ms, alphas = stage_s(tau, sa_ref, pa_ref, ms, tmax)
        tmax_a = stage_q(tau + 2, sa_ref)
        stage_v(tau, pa_ref, alphas)
        ms, alphas = stage_s(tau + 1, sb_ref, pb_ref, ms, tmax_b)
        return ms + alphas + tmax_a

    m_init = jnp.full((1, t), NEG_BIG, F32)
    one = jnp.ones((1, t), F32)
    tmax0 = stage_q(0, sa_ref, diagonal=True)
    n_pairs = jnp.right_shift(i + 2, 1)
    fin = lax.fori_loop(0, n_pairs, body, (m_init, m_init, one, one) + tmax0)
    stage_v(2 * n_pairs - 1, pb_ref, fin[2:4])

    lp = lam_ref[...]
    s1 = jnp.sum(lp[0:1] * lp[1:2], axis=-1, keepdims=True)
    s2 = jnp.sum(lp[2:3] * lp[3:4], axis=-1, keepdims=True)
    lam = jnp.exp(s1) - jnp.exp(s2) + LAM_INIT
    norm = [acc_ref[mi, :V_DIM, :] * (1.0 / acc_ref[mi, V_DIM:V_DIM + 1, :]) for mi in range(2)]
    oT = norm[0] - lam * norm[1]
    o = _rms(oT.T, g_ref[...]) * (1.0 - LAM_INIT)
    o_ref[...] = o.astype(BF16)


def _attn(qT, k4, vT, lam, g, B, S):
    t = SEQ_TILE
    nt = S // t
    return pl.pallas_call(
        _attn_kernel,
        grid=(B, N_HEADS, nt),
        in_specs=[pl.BlockSpec((None, None, None, V_DIM, t), lambda b, h, i: (b, h, i, 0, 0)),
                  pl.BlockSpec((None, nt, t, K_COLS), lambda b, h, i: (b, 0, 0, h)),
                  pl.BlockSpec((None, None, nt, V_ROWS, t), lambda b, h, i: (b, h, 0, 0, 0)),
                  pl.BlockSpec((4, HEAD_DIM), lambda b, h, i: (0, 0)),
                  pl.BlockSpec((1, V_DIM), lambda b, h, i: (0, 0))],
        out_specs=pl.BlockSpec((None, t, V_DIM), lambda b, h, i: (b, i, h)),
        out_shape=jax.ShapeDtypeStruct((B, S, ATTN_WIDTH), BF16),
        scratch_shapes=[pltpu.VMEM((t, t), F32), pltpu.VMEM((2, V_ROWS, t), F32),
                        pltpu.VMEM((2, t, t), F32), pltpu.VMEM((2, t, t), F32),
                        pltpu.VMEM((2, t, t), BF16), pltpu.VMEM((2, t, t), BF16)],
        compiler_params=pltpu.CompilerParams(dimension_semantics=("parallel", "arbitrary", "arbitrary"),
                                             vmem_limit_bytes=VMEM_LIMIT),
        name="diff_attn",
    )(qT, k4, vT, lam, g)


def _lru_kernel(xr_ref, yr_ref, cw_ref, cb_ref, wr_ref, wi_ref, br_ref, bi_ref, lam_ref, o_ref, xbuf, hc):
    T = LRU_TILE
    ti = pl.program_id(1)

    @pl.when(ti == 0)
    def _():
        xbuf[0:8] = jnp.zeros((8, LRU_WIDTH), F32)
        hc[...] = jnp.zeros_like(hc)

    x = xr_ref[...]
    xbuf[8:8 + T] = x
    cw = cw_ref[...]
    xc = cb_ref[...] + cw[3:4] * x
    for j in range(CONV_W - 1):
        xc = xc + cw[j:j + 1] * xbuf[5 + j:5 + j + T]
    xbuf[0:8] = x[T - 8:T]

    xb = xc.astype(BF16)
    r = jax.nn.sigmoid(_dot(xb, wr_ref[...]) + br_ref[...])
    ig = jax.nn.sigmoid(_dot(xb, wi_ref[...]) + bi_ref[...])
    z = -lam_ref[...]
    softplus = jnp.maximum(z, 0.0) + jnp.log1p(jnp.exp(-jnp.abs(z)))
    la = -LRU_C * r * softplus
    a = jnp.exp(la)
    mult = jnp.sqrt(-jnp.tanh(la) * (a * a + 1.0))
    row = lax.broadcasted_iota(jnp.int32, (T, LRU_WIDTH), 0)
    mult = jnp.where((row == 0) & (ti == 0), 1.0, mult)
    u = (xc * ig) * mult

    d = 1
    while d < T:
        valid = row >= d
        u = jnp.where(valid, a * pltpu.roll(u, d, 0) + u, u)
        a = jnp.where(valid, a * pltpu.roll(a, d, 0), a)
        d *= 2
    hfull = u + a * hc[...]
    hc[...] = hfull[T - 1:T]
    y = yr_ref[...]
    gelu = 0.5 * y * (1.0 + jnp.tanh(0.7978845608028654 * (y + 0.044715 * (y * y * y))))
    o_ref[...] = (hfull * gelu).astype(BF16)


def _lru(xr, yr, cw, cb, wr, wi, br, bi, lam, B, S):
    T = LRU_TILE
    seq = pl.BlockSpec((None, T, LRU_WIDTH), lambda b, t: (b, t, 0))
    full = lambda r, c: pl.BlockSpec((r, c), lambda b, t: (0, 0))
    return pl.pallas_call(
        _lru_kernel,
        grid=(B, S // T),
        in_specs=[seq, seq, full(CONV_W, LRU_WIDTH), full(1, LRU_WIDTH), full(LRU_WIDTH, LRU_WIDTH),
                  full(LRU_WIDTH, LRU_WIDTH), full(1, LRU_WIDTH), full(1, LRU_WIDTH), full(1, LRU_WIDTH)],
        out_specs=seq,
        out_shape=jax.ShapeDtypeStruct((B, S, LRU_WIDTH), BF16),
        scratch_shapes=[pltpu.VMEM((T + 8, LRU_WIDTH), F32), pltpu.VMEM((1, LRU_WIDTH), F32)],
        compiler_params=pltpu.CompilerParams(dimension_semantics=("arbitrary", "arbitrary"),
                                             vmem_limit_bytes=VMEM_LIMIT),
        name="rg_lru",
    )(xr, yr, cw, cb, wr, wi, br, bi, lam)


def _pack_words(v):
    bits = pltpu.bitcast(v.astype(BF16).astype(F32), jnp.uint32)
    half = D_MODEL // 2
    packed = (bits[:, :half] >> 16) | (bits[:, half:] & jnp.uint32(0xFFFF0000))
    return [packed[:, c * LANES:(c + 1) * LANES] for c in range(PACK_SUB)]


def _packed_chunk(c, rows, first_row=0):
    return (pl.ds(first_row * PACK_SUB + c, rows, stride=PACK_SUB), slice(None))


def _pack_rows(v, out_ref):
    for c, words in enumerate(_pack_words(v)):
        out_ref[_packed_chunk(c, v.shape[0])] = words


def _unpack_chunks(in_ref, rows, first_row=0):
    lo, hi = [], []
    for c in range(PACK_SUB):
        w = in_ref[_packed_chunk(c, rows, first_row)]
        lo.append(pltpu.bitcast(w << 16, F32))
        hi.append(pltpu.bitcast(w & jnp.uint32(0xFFFF0000), F32))
    return lo + hi


def _unpack_rows(in_ref, rows):
    return jnp.concatenate(_unpack_chunks(in_ref, rows), axis=1).astype(BF16)


def _merge_kernel(x_ref, attn_ref, lru_ref, g1_ref, wg_ref, woa_ref, wol_ref, wout_ref, g2_ref, wrt_ref,
                  x1_ref, h2p_ref, route_ref, counts_ref, cnt, tri):
    tm = SEQ_TILE

    @pl.when(pl.program_id(0) == 0)
    def _():
        cnt[...] = jnp.zeros_like(cnt)
        r = lax.broadcasted_iota(jnp.int32, (tm, tm), 0)
        c = lax.broadcasted_iota(jnp.int32, (tm, tm), 1)
        tri[...] = jnp.where(c < r, 1.0, 0.0).astype(BF16)

    x = x_ref[...]
    hb = _rms(x, g1_ref[...]).astype(BF16)
    gates = 0.5 * jnp.tanh(0.5 * _dot(hb, wg_ref[...])) + 0.5
    merged = gates[:, :D_MODEL] * _dot(attn_ref[...], woa_ref[...]) + gates[:, D_MODEL:] * _dot(lru_ref[...], wol_ref[...])
    x1 = x + _dot(merged.astype(BF16), wout_ref[...])
    x1_ref[...] = x1
    h2 = _rms(x1, g2_ref[...])
    _pack_rows(h2, h2p_ref)

    h_hi = h2.astype(BF16)
    h_lo = (h2 - h_hi.astype(F32)).astype(BF16)
    wrt = wrt_ref[...]
    hh = _dot(h_hi, wrt)
    logits = hh[:, :ROUTE_LANES] + hh[:, ROUTE_LANES:] + _dot(h_lo, wrt[:, :ROUTE_LANES])
    lane = lax.broadcasted_iota(jnp.int32, logits.shape, 1)
    big = jnp.int32(1 << 20)

    def first_argmax(v):
        m = jnp.max(v, axis=-1, keepdims=True)
        return m, jnp.min(jnp.where(v == m, lane, big), axis=-1, keepdims=True)

    gmask = lane < N_GROUPS
    gmax, gidx = first_argmax(jnp.where(gmask, logits, -jnp.inf))
    gsum = jnp.sum(jnp.where(gmask, jnp.exp(logits - gmax), 0.0), axis=-1, keepdims=True)
    g_w = 1.0 / gsum
    lo = N_GROUPS + EXPERTS_PER_GROUP * gidx
    el = jnp.where((lane >= lo) & (lane < lo + EXPERTS_PER_GROUP), logits, -jnp.inf)
    m1, i1 = first_argmax(el)
    m2, i2 = first_argmax(jnp.where(lane == i1, -jnp.inf, el))
    rr = jnp.exp(m2 - m1)
    w1 = g_w / (1.0 + rr)
    w2 = g_w * rr / (1.0 + rr)
    oh1 = lane == i1
    oh2 = lane == i2
    oh = jnp.where(oh1 | oh2, 1.0, 0.0)
    before = _dot(tri[...], oh.astype(BF16)) + cnt[...]
    r1 = jnp.sum(jnp.where(oh1, before, 0.0), axis=-1, keepdims=True)
    r2 = jnp.sum(jnp.where(oh2, before, 0.0), axis=-1, keepdims=True)
    cnt[...] = cnt[...] + jnp.sum(oh, axis=0, keepdims=True)
    counts_ref[...] = cnt[...]
    vals = (i1.astype(F32), i2.astype(F32), w1, w2, r1, r2)
    route = jnp.zeros_like(logits)
    for k, v in enumerate(vals):
        route = jnp.where(lane == k, v, route)
    route_ref[...] = route


def _merge(x2, attn, lru, g1, wg, woa, wol, wout, g2, wrt):
    N = x2.shape[0]
    tm = SEQ_TILE
    rows = lambda c: pl.BlockSpec((tm, c), lambda i: (i, 0))
    full = lambda r, c: pl.BlockSpec((r, c), lambda i: (0, 0))
    return pl.pallas_call(
        _merge_kernel,
        grid=(N // tm,),
        in_specs=[rows(D_MODEL), rows(ATTN_WIDTH), rows(LRU_WIDTH), full(1, D_MODEL), full(D_MODEL, 2 * D_MODEL),
                  full(ATTN_WIDTH, D_MODEL), full(LRU_WIDTH, D_MODEL), full(D_MODEL, D_MODEL), full(1, D_MODEL),
                  full(D_MODEL, 2 * ROUTE_LANES)],
        out_specs=[rows(D_MODEL), pl.BlockSpec((tm * PACK_SUB, LANES), lambda i: (i, 0)), rows(ROUTE_LANES),
                   full(1, ROUTE_LANES)],
        out_shape=[jax.ShapeDtypeStruct((N, D_MODEL), F32), jax.ShapeDtypeStruct((N * PACK_SUB, LANES), jnp.uint32),
                   jax.ShapeDtypeStruct((N, ROUTE_LANES), F32), jax.ShapeDtypeStruct((1, ROUTE_LANES), F32)],
        scratch_shapes=[pltpu.VMEM((1, ROUTE_LANES), F32), pltpu.VMEM((tm, tm), BF16)],
        compiler_params=pltpu.CompilerParams(dimension_semantics=("arbitrary",), vmem_limit_bytes=VMEM_LIMIT),
        name="merge_route",
    )(x2, attn, lru, g1, wg, woa, wol, wout, g2, wrt)


def _for_each_assignment(starts_ref, lane_ref, rank_ref, fn):
    def group(gi, _):
        toks = [gi * DMA_GROUP + j for j in range(DMA_GROUP)]
        pos = [[starts_ref[lane_ref[0, 0, tk * TOP_K + k]] + rank_ref[0, 0, tk * TOP_K + k] for k in range(TOP_K)]
               for tk in toks]
        for tk, p in zip(toks, pos):
            for k in range(TOP_K):
                fn(tk, k, p[k])
        return 0
    lax.fori_loop(0, ROW_TILE // DMA_GROUP, group, 0)


def _dispatch_kernel(starts_ref, lane_ref, rank_ref, h2p_hbm, xs_hbm, xin, in_sem, out_sem):
    tile_rows = ROW_TILE * PACK_SUB
    i = pl.program_id(0)
    n = pl.num_programs(0)
    slot = lax.rem(i, DISPATCH_SLOTS)

    def fetch(tile, s):
        return pltpu.make_async_copy(h2p_hbm.at[pl.ds(pl.multiple_of(tile * tile_rows, tile_rows), tile_rows)],
                                     xin.at[s], in_sem.at[s])

    def drain(s):
        for _ in range(TOP_K):
            pltpu.make_async_copy(xin.at[s], xs_hbm.at[pl.ds(0, tile_rows)], out_sem.at[s]).wait()

    @pl.when(i == 0)
    def _():
        fetch(0, 0).start()

    @pl.when(i + 1 < n)
    def _():
        fetch(i + 1, lax.rem(i + 1, DISPATCH_SLOTS)).start()

    fetch(i, slot).wait()

    def start(tk, k, pos):
        pltpu.make_async_copy(xin.at[slot, pl.ds(pl.multiple_of(tk * PACK_SUB, PACK_SUB), PACK_SUB)],
                              xs_hbm.at[pl.ds(pl.multiple_of(pos * PACK_SUB, PACK_SUB), PACK_SUB)],
                              out_sem.at[slot]).start()

    _for_each_assignment(starts_ref, lane_ref, rank_ref, start)

    @pl.when(i > 0)
    def _():
        drain(lax.rem(i + DISPATCH_SLOTS - 1, DISPATCH_SLOTS))

    @pl.when(i == n - 1)
    def _():
        drain(slot)


def _dispatch(starts, lane3, rank3, h2p):
    nt = lane3.shape[0]
    idx = pl.BlockSpec((1, 1, TOP_K * ROW_TILE), lambda i, st: (i, 0, 0), memory_space=pltpu.SMEM)
    grid_spec = pltpu.PrefetchScalarGridSpec(
        num_scalar_prefetch=1,
        grid=(nt,),
        in_specs=[idx, idx, pl.BlockSpec(memory_space=pl.ANY)],
        out_specs=pl.BlockSpec(memory_space=pl.ANY),
        scratch_shapes=[pltpu.VMEM((DISPATCH_SLOTS, ROW_TILE * PACK_SUB, LANES), jnp.uint32),
                        pltpu.SemaphoreType.DMA((DISPATCH_SLOTS,)), pltpu.SemaphoreType.DMA((DISPATCH_SLOTS,))],
    )
    return pl.pallas_call(
        _dispatch_kernel,
        grid_spec=grid_spec,
        out_shape=jax.ShapeDtypeStruct((TOP_K * h2p.shape[0], LANES), jnp.uint32),
        compiler_params=pltpu.CompilerParams(dimension_semantics=("arbitrary",), has_side_effects=True),
        name="moe_dispatch",
    )(starts, lane3, rank3, h2p)


def _moe_kernel(vt_ref, ve_ref, vlo_ref, vhi_ref, xs_ref, wg_ref, wu_ref, wd_ref, ys_ref, wgb, wub, wdb):
    tm = MOE_TILE
    v = pl.program_id(0)
    nv = pl.num_programs(0)
    t = vt_ref[v]
    e = ve_ref[v]
    prev = jnp.maximum(v - 1, 0)

    @pl.when((v == 0) | (ve_ref[prev] != e))
    def _():
        wgb[...] = wg_ref[...].astype(BF16)
        wub[...] = wu_ref[...].astype(BF16)
        wdb[...] = wd_ref[...].astype(BF16)

    xb = _unpack_rows(xs_ref, tm)
    g = _dot(xb, wgb[...])
    u = _dot(xb, wub[...])
    hmid = (g * jax.nn.sigmoid(g)) * u
    words = _pack_words(_dot(hmid.astype(BF16), wdb[...]))
    first = (v == 0) | (vt_ref[prev] != t)

    @pl.when(first)
    def _():
        for c in range(PACK_SUB):
            ys_ref[_packed_chunk(c, tm)] = words[c]

    @pl.when(jnp.logical_not(first))
    def _():
        rows = t * tm + lax.broadcasted_iota(jnp.int32, (tm, LANES), 0)
        mine = (rows >= vlo_ref[v]) & (rows < vhi_ref[v])
        for c in range(PACK_SUB):
            ys_ref[_packed_chunk(c, tm)] = jnp.where(mine, words[c], ys_ref[_packed_chunk(c, tm)])


def _moe(vt, ve, vlo, vhi, xs, wg, wu, wd):
    tm = MOE_TILE
    n_rows = xs.shape[0] // PACK_SUB
    grid_spec = pltpu.PrefetchScalarGridSpec(
        num_scalar_prefetch=4,
        grid=(vt.shape[0],),
        in_specs=[pl.BlockSpec((tm * PACK_SUB, LANES), lambda v, vt, ve, lo, hi: (vt[v], 0)),
                  pl.BlockSpec((None, D_MODEL, D_EXPERT), lambda v, vt, ve, lo, hi: (ve[v], 0, 0)),
                  pl.BlockSpec((None, D_MODEL, D_EXPERT), lambda v, vt, ve, lo, hi: (ve[v], 0, 0)),
                  pl.BlockSpec((None, D_EXPERT, D_MODEL), lambda v, vt, ve, lo, hi: (ve[v], 0, 0))],
        out_specs=pl.BlockSpec((tm * PACK_SUB, LANES), lambda v, vt, ve, lo, hi: (vt[v], 0)),
        scratch_shapes=[pltpu.VMEM((D_MODEL, D_EXPERT), BF16), pltpu.VMEM((D_MODEL, D_EXPERT), BF16),
                        pltpu.VMEM((D_EXPERT, D_MODEL), BF16)],
    )
    return pl.pallas_call(
        _moe_kernel,
        grid_spec=grid_spec,
        out_shape=jax.ShapeDtypeStruct((n_rows * PACK_SUB, LANES), jnp.uint32),
        compiler_params=pltpu.CompilerParams(dimension_semantics=("arbitrary",), vmem_limit_bytes=VMEM_LIMIT),
        name="moe_experts",
    )(vt, ve, vlo, vhi, xs, wg, wu, wd)


def _combine_kernel(starts_ref, lane_ref, rank_ref, lanen_ref, rankn_ref, x1_ref, route_ref, g_ref, ys_hbm, o_ref,
                    ybuf, sem):
    tc = ROW_TILE
    i = pl.program_id(0)
    n = pl.num_programs(0)
    slot = lax.rem(i, 2)

    def gather(l_ref, r_ref, s):
        def start(tk, k, pos):
            pltpu.make_async_copy(ys_hbm.at[pl.ds(pl.multiple_of(pos * PACK_SUB, PACK_SUB), PACK_SUB)],
                                  ybuf.at[s, pl.ds(pl.multiple_of((k * tc + tk) * PACK_SUB, PACK_SUB), PACK_SUB)],
                                  sem.at[s]).start()
        _for_each_assignment(starts_ref, l_ref, r_ref, start)

    @pl.when(i == 0)
    def _():
        gather(lane_ref, rank_ref, 0)

    @pl.when(i + 1 < n)
    def _():
        gather(lanen_ref, rankn_ref, 1 - slot)

    pltpu.make_async_copy(ys_hbm.at[pl.ds(0, TOP_K * tc * PACK_SUB)], ybuf.at[slot], sem.at[slot]).wait()

    route = route_ref[...]
    n_chunks = D_MODEL // LANES
    z = [x1_ref[:, c * LANES:(c + 1) * LANES] for c in range(n_chunks)]
    for k in range(TOP_K):
        wk = route[:, TOP_K + k:TOP_K + k + 1]
        yk = _unpack_chunks(ybuf.at[slot], tc, first_row=k * tc)
        z = [zc + wk * yc for zc, yc in zip(z, yk)]
    ss = sum(jnp.sum(zc * zc, axis=-1, keepdims=True) for zc in z)
    inv = lax.rsqrt(ss * (1.0 / D_MODEL) + NORM_EPS)
    for c in range(n_chunks):
        o_ref[:, c * LANES:(c + 1) * LANES] = z[c] * inv * g_ref[:, c * LANES:(c + 1) * LANES]


def _combine(starts, lane3, rank3, x1, route, g, ys):
    N = x1.shape[0]
    tc = ROW_TILE
    nt = N // tc
    idx = lambda f: pl.BlockSpec((1, 1, TOP_K * tc), f, memory_space=pltpu.SMEM)
    cur = idx(lambda i, st: (i, 0, 0))
    nxt = idx(lambda i, st: (jnp.minimum(i + 1, nt - 1), 0, 0))
    grid_spec = pltpu.PrefetchScalarGridSpec(
        num_scalar_prefetch=1,
        grid=(nt,),
        in_specs=[cur, cur, nxt, nxt,
                  pl.BlockSpec((tc, D_MODEL), lambda i, st: (i, 0)),
                  pl.BlockSpec((tc, ROUTE_LANES), lambda i, st: (i, 0)),
                  pl.BlockSpec((1, D_MODEL), lambda i, st: (0, 0)),
                  pl.BlockSpec(memory_space=pl.ANY)],
        out_specs=pl.BlockSpec((tc, D_MODEL), lambda i, st: (i, 0)),
        scratch_shapes=[pltpu.VMEM((2, TOP_K * tc * PACK_SUB, LANES), jnp.uint32), pltpu.SemaphoreType.DMA((2,))],
    )
    return pl.pallas_call(
        _combine_kernel,
        grid_spec=grid_spec,
        out_shape=jax.ShapeDtypeStruct((N, D_MODEL), F32),
        compiler_params=pltpu.CompilerParams(dimension_semantics=("arbitrary",), vmem_limit_bytes=VMEM_LIMIT),
        name="combine_norm",
    )(starts, lane3, rank3, lane3, rank3, x1, route, g, ys)


def _block_diag(w):
    nb, c, _ = w.shape
    eye = jnp.eye(nb, dtype=w.dtype)
    return (eye[:, None, :, None] * w[:, :, None, :]).reshape(nb * c, nb * c)


def _visit_plan(counts, n_rows):
    tm = MOE_TILE
    n_tiles = n_rows // tm
    n_visits = n_tiles + N_EXPERTS - 1
    cnt = counts.astype(jnp.int32)
    ends = jnp.cumsum(cnt)
    starts = ends - cnt
    first_tile = starts // tm
    n_vis = jnp.where(cnt > 0, (ends - 1) // tm - first_tile + 1, 0)
    v_end = jnp.cumsum(n_vis)
    v_start = v_end - n_vis
    total = v_end[-1]
    v = jnp.arange(n_visits, dtype=jnp.int32)
    owner = lambda q: jnp.minimum(jnp.searchsorted(v_end, q, side='right'), ROUTE_LANES - 1).astype(jnp.int32)
    lane = owner(v)
    valid = v < total
    tile = first_tile[lane] + (v - v_start[lane])
    lo = jnp.maximum(starts[lane], tile * tm)
    hi = jnp.minimum(ends[lane], (tile + 1) * tm)
    vt = jnp.where(valid, tile, n_tiles - 1)
    ve = jnp.maximum(jnp.where(valid, lane, owner(total - 1)) - EXPERT_LANE0, 0)
    zero = jnp.zeros_like(lo)
    return starts, vt, ve, jnp.where(valid, lo, zero), jnp.where(valid, hi, zero)


def kernel(x, norm_mix_g, w_in, lambda_qk, subln_g, conv_w, conv_b, w_r, b_r, w_i, b_i, lru_lambda, w_o_attn, w_o_lru, w_out, norm_ffn_g, w_group, w_expert_router, w_gate, w_up, w_down, final_norm_g):
    B, S, D = x.shape
    N = B * S
    nt = S // SEQ_TILE
    depth = norm_mix_g.shape[0]
    assert depth == 1 and D == D_MODEL and S % SEQ_TILE == 0 and S % LRU_TILE == 0
    assert N % ROW_TILE == 0 and (N * TOP_K) % MOE_TILE == 0
    l = 0
    row = lambda v: v.reshape(1, -1).astype(F32)

    x2 = x.reshape(N, D)
    w_in_l = w_in[l]
    qT, k, vT, xr, yr = _inproj(x2, row(norm_mix_g[l]), w_in_l[:, :PROJ_COLS].astype(BF16), B, S)

    attn = _attn(qT, k.reshape(B, nt, SEQ_TILE, N_HEADS * K_COLS), vT, lambda_qk[l].reshape(4, HEAD_DIM).astype(F32),
                 row(subln_g[l]), B, S)

    lru = _lru(xr.reshape(B, S, LRU_WIDTH), yr.reshape(B, S, LRU_WIDTH), conv_w[l].astype(F32), row(conv_b[l]),
               _block_diag(w_r[l]).astype(BF16), _block_diag(w_i[l]).astype(BF16), row(b_r[l]), row(b_i[l]),
               row(lru_lambda[l]), B, S)

    w_route = jnp.concatenate(
        [w_group[l], jnp.transpose(w_expert_router[l], (1, 0, 2)).reshape(D, N_EXPERTS),
         jnp.zeros((D, ROUTE_LANES - N_GROUPS - N_EXPERTS), F32)], axis=1).astype(F32)
    w_route_hi = w_route.astype(BF16)
    w_route = jnp.concatenate([w_route_hi, (w_route - w_route_hi.astype(F32)).astype(BF16)], axis=1)
    x1, h2p, route, counts = _merge(x2, attn.reshape(N, ATTN_WIDTH), lru.reshape(N, LRU_WIDTH), row(norm_mix_g[l]),
                                    w_in_l[:, PROJ_COLS:].astype(BF16), w_o_attn[l].astype(BF16),
                                    w_o_lru[l].astype(BF16), w_out[l].astype(BF16), row(norm_ffn_g[l]), w_route)

    starts, vt, ve, vlo, vhi = _visit_plan(counts[0], N * TOP_K)
    as_idx = lambda cols: cols.astype(jnp.int32).reshape(N // ROW_TILE, 1, TOP_K * ROW_TILE)
    lane3 = as_idx(route[:, 0:TOP_K])
    rank3 = as_idx(route[:, 2 * TOP_K:3 * TOP_K])

    xs = _dispatch(starts, lane3, rank3, h2p)
    ys = _moe(vt, ve, vlo, vhi, xs, w_gate[l], w_up[l], w_down[l])
    out = _combine(starts, lane3, rank3, x1, route, row(final_norm_g), ys)
    return out.reshape(B, S, D)
```

```python
import functools
import math

import jax
import jax.numpy as jnp
from jax import lax
from jax.experimental import pallas as pl
from jax.experimental.pallas import tpu as pltpu

F32 = jnp.float32
BF16 = jnp.bfloat16

D_MODEL = 1024
N_HEADS = 4
HEAD_DIM = 64
V_DIM = 2 * HEAD_DIM
ATTN_WIDTH = N_HEADS * V_DIM
V_ROWS = V_DIM + 16
K_COLS = 2 * V_DIM
POS_SPLIT = 3
POS_RADIX = 256
LRU_WIDTH = D_MODEL // 2
LRU_BLOCKS = 8
CONV_W = 4
LRU_C = 8.0
N_GROUPS = 4
EXPERTS_PER_GROUP = 8
N_EXPERTS = N_GROUPS * EXPERTS_PER_GROUP
TOP_K = 2
D_EXPERT = D_MODEL // 2
NORM_EPS = 1e-6
LAM_INIT = 0.8 - 0.6 * math.exp(-0.3 * 0)

QK_COLS = N_HEADS * 2 * HEAD_DIM
PROJ_COLS = 2 * QK_COLS + ATTN_WIDTH + 2 * LRU_WIDTH
ROUTE_LANES = 128
NEG_BIG = -1e30
LOG2E = math.log2(math.e)
ALIBI_SLOPES = tuple(2.0 ** (-8.0 * (h + 1) / N_HEADS) for h in range(N_HEADS))

SEQ_TILE = 512
LRU_TILE = 512
ROW_TILE = 256
MOE_TILE = 256
DMA_GROUP = 8
DISPATCH_SLOTS = 3
LANES = 128
SUBLANES = 8
PACK_SUB = D_MODEL // 2 // LANES
EXPERT_LANE0 = N_GROUPS
VMEM_LIMIT = 48 * 1024 * 1024


def _rms(x, g):
    return x * lax.rsqrt(jnp.mean(x * x, axis=-1, keepdims=True) + NORM_EPS) * g


def _dot(a, b):
    return jnp.dot(a, b, preferred_element_type=F32)


def _inproj_kernel(x_ref, g_ref, w_ref, qT_ref, k_ref, vT_ref, xr_ref, yr_ref):
    hb = _rms(x_ref[...], g_ref[...]).astype(BF16)

    def proj(lo, hi):
        return _dot(hb, w_ref[:, lo:hi])

    q = proj(0, QK_COLS) * (HEAD_DIM ** -0.5 * LOG2E)
    for h in range(N_HEADS):
        qT_ref[h] = q[:, h * V_DIM:(h + 1) * V_DIM].T.astype(BF16)
    k = proj(QK_COLS, 2 * QK_COLS).astype(BF16)
    r = lax.broadcasted_iota(jnp.int32, (k.shape[0], K_COLS - V_DIM), 0)
    lane = lax.broadcasted_iota(jnp.int32, r.shape, 1)
    a = r // POS_RADIX * POS_RADIX
    feat = jnp.where(lane < POS_SPLIT, a, jnp.where(lane < 2 * POS_SPLIT, r - a, 0)).astype(F32).astype(BF16)
    for h in range(N_HEADS):
        k_ref[:, h * K_COLS:h * K_COLS + V_DIM] = k[:, h * V_DIM:(h + 1) * V_DIM]
        k_ref[:, h * K_COLS + V_DIM:(h + 1) * K_COLS] = feat
    v = proj(2 * QK_COLS, 2 * QK_COLS + ATTN_WIDTH)
    for h in range(N_HEADS):
        vT_ref[h, :V_DIM, :] = v[:, h * V_DIM:(h + 1) * V_DIM].T.astype(BF16)
        pad_row = lax.broadcasted_iota(jnp.int32, (V_ROWS - V_DIM, v.shape[0]), 0)
        vT_ref[h, V_DIM:, :] = jnp.where(pad_row == 0, 1.0, 0.0).astype(BF16)
    c0 = 2 * QK_COLS + ATTN_WIDTH
    xr_ref[...] = proj(c0, c0 + LRU_WIDTH)
    yr_ref[...] = proj(c0 + LRU_WIDTH, c0 + 2 * LRU_WIDTH)


def _inproj(x2, g, w, B, S):
    N = B * S
    tm = SEQ_TILE
    nt = S // tm
    tile5 = pl.BlockSpec((None, N_HEADS, None, V_DIM, tm), lambda i: (i // nt, 0, i % nt, 0, 0))
    rows = lambda c: pl.BlockSpec((tm, c), lambda i: (i, 0))
    return pl.pallas_call(
        _inproj_kernel,
        grid=(N // tm,),
        in_specs=[rows(D_MODEL),
                  pl.BlockSpec((1, D_MODEL), lambda i: (0, 0)),
                  pl.BlockSpec((D_MODEL, PROJ_COLS), lambda i: (0, 0))],
        out_specs=[tile5, rows(N_HEADS * K_COLS),
                   pl.BlockSpec((None, N_HEADS, None, V_ROWS, tm), lambda i: (i // nt, 0, i % nt, 0, 0)),
                   rows(LRU_WIDTH), rows(LRU_WIDTH)],
        out_shape=[jax.ShapeDtypeStruct((B, N_HEADS, nt, V_DIM, tm), BF16),
                   jax.ShapeDtypeStruct((N, N_HEADS * K_COLS), BF16),
                   jax.ShapeDtypeStruct((B, N_HEADS, nt, V_ROWS, tm), BF16),
                   jax.ShapeDtypeStruct((N, LRU_WIDTH), F32),
                   jax.ShapeDtypeStruct((N, LRU_WIDTH), F32)],
        compiler_params=pltpu.CompilerParams(dimension_semantics=("parallel",),
                                             vmem_limit_bytes=VMEM_LIMIT),
        name="inproj",
    )(x2, g, w)


def _attn_kernel(qT_ref, k_ref, vT_ref, lam_ref, g_ref, o_ref, mask_ref, acc_ref, sa_ref, sb_ref, pa_ref, pb_ref):
    t = SEQ_TILE
    h = pl.program_id(1)
    i = pl.program_id(2)
    slope = LOG2E * jnp.where(h == 0, ALIBI_SLOPES[0], jnp.where(h == 1, ALIBI_SLOPES[1],
                              jnp.where(h == 2, ALIBI_SLOPES[2], ALIBI_SLOPES[3]))).astype(F32)
    last_off = jnp.maximum(i - 1, 0)

    @pl.when(i == 0)
    def _():
        r = lax.broadcasted_iota(jnp.int32, (t, t), 0)
        c = lax.broadcasted_iota(jnp.int32, (t, t), 1)
        mask_ref[...] = jnp.where(r <= c, 0.0, NEG_BIG)

    qf = qT_ref[...].astype(F32)
    row = lax.broadcasted_iota(jnp.int32, qf.shape, 0)
    sl = jnp.full(qf.shape, slope, F32)
    hi = sl.astype(BF16).astype(F32)
    mid = (sl - hi).astype(BF16).astype(F32)
    lo = (sl - hi - mid).astype(BF16).astype(F32)
    piece = jnp.where(row % 3 == 0, hi, jnp.where(row % 3 == 1, mid, lo))
    srows = jnp.where(row < 2 * POS_SPLIT, piece, 0.0).astype(BF16)
    qs = tuple(jnp.concatenate([jnp.where(sel, qf, 0.0).astype(BF16), srows], axis=0)
               for sel in (row < HEAD_DIM, row >= HEAD_DIM))
    acc_ref[...] = jnp.zeros_like(acc_ref)
    pb_ref[...] = jnp.zeros_like(pb_ref)

    def key_tile(tau):
        return jnp.where(tau <= 0, i, jnp.minimum(tau - 1, last_off))

    def stage_q(tau, s_ref, diagonal=False):
        kt = k_ref[key_tile(tau)]
        tile_max = []
        for mi in range(2):
            s = _dot(kt, qs[mi])
            if diagonal:
                s = s + mask_ref[...]
            s_ref[mi] = s
            tile_max.append(jnp.where(tau <= i, jnp.max(s, axis=0, keepdims=True), NEG_BIG))
        return tuple(tile_max)

    def stage_s(tau, s_ref, p_ref, ms, tile_max):
        cj = jnp.where(tau <= i, slope * (key_tile(tau) * t).astype(F32), NEG_BIG)
        m_out, alphas = [], []
        for mi in range(2):
            m_new = jnp.maximum(ms[mi], tile_max[mi] + cj)
            alphas.append(jnp.exp2(ms[mi] - m_new))
            p_ref[mi] = jnp.exp2(s_ref[mi] - (m_new - cj)).astype(BF16)
            m_out.append(m_new)
        return tuple(m_out), tuple(alphas)

    def stage_v(tau, p_ref, alphas):
        vt = vT_ref[key_tile(tau)]
        for mi in range(2):
            acc_ref[mi] = alphas[mi] * acc_ref[mi] + _dot(vt, p_ref[mi])

    def body(jj, carry):
        ms, alphas, tmax = carry[:2], carry[2:4], carry[4:]
        tau = 2 * jj
        tmax_b = stage_q(tau + 1, sb_ref)
        stage_v(tau - 1, pb_ref, alphas)
        ms, alphas = stage_s(tau, sa_ref, pa_ref, ms, tmax)
        tmax_a = stage_q(tau + 2, sa_ref)
        stage_v(tau, pa_ref, alphas)
        ms, alphas = stage_s(tau + 1, sb_ref, pb_ref, ms, tmax_b)
        return ms + alphas + tmax_a

    m_init = jnp.full((1, t), NEG_BIG, F32)
    one = jnp.ones((1, t), F32)
    tmax0 = stage_q(0, sa_ref, diagonal=True)
    n_pairs = jnp.right_shift(i + 2, 1)
    fin = lax.fori_loop(0, n_pairs, body, (m_init, m_init, one, one) + tmax0)
    stage_v(2 * n_pairs - 1, pb_ref, fin[2:4])

    lp = lam_ref[...]
    s1 = jnp.sum(lp[0:1] * lp[1:2], axis=-1, keepdims=True)
    s2 = jnp.sum(lp[2:3] * lp[3:4], axis=-1, keepdims=True)
    lam = jnp.exp(s1) - jnp.exp(s2) + LAM_INIT
    norm = [acc_ref[mi, :V_DIM, :] * (1.0 / acc_ref[mi, V_DIM:V_DIM + 1, :]) for mi in range(2)]
    oT = norm[0] - lam * norm[1]
    o = _rms(oT.T, g_ref[...]) * (1.0 - LAM_INIT)
    o_ref[...] = o.astype(BF16)


def _attn(qT, k4, vT, lam, g, B, S):
    t = SEQ_TILE
    nt = S // t
    return pl.pallas_call(
        _attn_kernel,
        grid=(B, N_HEADS, nt),
        in_specs=[pl.BlockSpec((None, None, None, V_DIM, t), lambda b, h, i: (b, h, i, 0, 0)),
                  pl.BlockSpec((None, nt, t, K_COLS), lambda b, h, i: (b, 0, 0, h)),
                  pl.BlockSpec((None, None, nt, V_ROWS, t), lambda b, h, i: (b, h, 0, 0, 0)),
                  pl.BlockSpec((4, HEAD_DIM), lambda b, h, i: (0, 0)),
                  pl.BlockSpec((1, V_DIM), lambda b, h, i: (0, 0))],
        out_specs=pl.BlockSpec((None, t, V_DIM), lambda b, h, i: (b, i, h)),
        out_shape=jax.ShapeDtypeStruct((B, S, ATTN_WIDTH), BF16),
        scratch_shapes=[pltpu.VMEM((t, t), F32), pltpu.VMEM((2, V_ROWS, t), F32),
                        pltpu.VMEM((2, t, t), F32), pltpu.VMEM((2, t, t), F32),
                        pltpu.VMEM((2, t, t), BF16), pltpu.VMEM((2, t, t), BF16)],
        compiler_params=pltpu.CompilerParams(dimension_semantics=("parallel", "arbitrary", "arbitrary"),
                                             vmem_limit_bytes=VMEM_LIMIT),
        name="diff_attn",
    )(qT, k4, vT, lam, g)


def _lru_kernel(xr_ref, yr_ref, cw_ref, cb_ref, wr_ref, wi_ref, br_ref, bi_ref, lam_ref, o_ref, xbuf, hc):
    T = LRU_TILE
    ti = pl.program_id(1)

    @pl.when(ti == 0)
    def _():
        xbuf[0:8] = jnp.zeros((8, LRU_WIDTH), F32)
        hc[...] = jnp.zeros_like(hc)

    x = xr_ref[...]
    xbuf[8:8 + T] = x
    cw = cw_ref[...]
    xc = cb_ref[...] + cw[3:4] * x
    for j in range(CONV_W - 1):
        xc = xc + cw[j:j + 1] * xbuf[5 + j:5 + j + T]
    xbuf[0:8] = x[T - 8:T]

    xb = xc.astype(BF16)
    r = jax.nn.sigmoid(_dot(xb, wr_ref[...]) + br_ref[...])
    ig = jax.nn.sigmoid(_dot(xb, wi_ref[...]) + bi_ref[...])
    z = -lam_ref[...]
    softplus = jnp.maximum(z, 0.0) + jnp.log1p(jnp.exp(-jnp.abs(z)))
    la = -LRU_C * r * softplus
    a = jnp.exp(la)
    m2 = -jnp.tanh(la) * (a * a + 1.0)
    mult = jnp.where(m2 > 0.0, m2 * lax.rsqrt(m2), 0.0)
    row = lax.broadcasted_iota(jnp.int32, (T, LRU_WIDTH), 0)
    mult = jnp.where((row == 0) & (ti == 0), 1.0, mult)
    u = (xc * ig) * mult

    nb = T // SUBLANES
    a3 = a.reshape(nb, SUBLANES, LRU_WIDTH)
    u3 = u.reshape(nb, SUBLANES, LRU_WIDTH)
    sub = lax.broadcasted_iota(jnp.int32, a3.shape, 1)
    d = 1
    while d < SUBLANES:
        valid = sub >= d
        u3 = jnp.where(valid, a3 * pltpu.roll(u3, d, 1) + u3, u3)
        a3 = jnp.where(valid, a3 * pltpu.roll(a3, d, 1), a3)
        d *= 2
    h = hc[...]
    blocks = []
    for b in range(nb):
        hb = u3[b] + a3[b] * h
        blocks.append(hb)
        h = hb[SUBLANES - 1:SUBLANES]
    hfull = jnp.concatenate(blocks, axis=0)
    hc[...] = h
    y = yr_ref[...]
    gelu = 0.5 * y * (1.0 + jnp.tanh(0.7978845608028654 * (y + 0.044715 * (y * y * y))))
    o_ref[...] = (hfull * gelu).astype(BF16)


def _lru(xr, yr, cw, cb, wr, wi, br, bi, lam, B, S):
    T = LRU_TILE
    seq = pl.BlockSpec((None, T, LRU_WIDTH), lambda b, t: (b, t, 0))
    full = lambda r, c: pl.BlockSpec((r, c), lambda b, t: (0, 0))
    return pl.pallas_call(
        _lru_kernel,
        grid=(B, S // T),
        in_specs=[seq, seq, full(CONV_W, LRU_WIDTH), full(1, LRU_WIDTH), full(LRU_WIDTH, LRU_WIDTH),
                  full(LRU_WIDTH, LRU_WIDTH), full(1, LRU_WIDTH), full(1, LRU_WIDTH), full(1, LRU_WIDTH)],
        out_specs=seq,
        out_shape=jax.ShapeDtypeStruct((B, S, LRU_WIDTH), BF16),
        scratch_shapes=[pltpu.VMEM((T + 8, LRU_WIDTH), F32), pltpu.VMEM((1, LRU_WIDTH), F32)],
        compiler_params=pltpu.CompilerParams(dimension_semantics=("arbitrary", "arbitrary"),
                                             vmem_limit_bytes=VMEM_LIMIT),
        name="rg_lru",
    )(xr, yr, cw, cb, wr, wi, br, bi, lam)


def _pack_words(v):
    bits = pltpu.bitcast(v.astype(BF16).astype(F32), jnp.uint32)
    half = D_MODEL // 2
    packed = (bits[:, :half] >> 16) | (bits[:, half:] & jnp.uint32(0xFFFF0000))
    return [packed[:, c * LANES:(c + 1) * LANES] for c in range(PACK_SUB)]


def _packed_chunk(c, rows, first_row=0):
    return (pl.ds(first_row * PACK_SUB + c, rows, stride=PACK_SUB), slice(None))


def _pack_rows(v, out_ref):
    for c, words in enumerate(_pack_words(v)):
        out_ref[_packed_chunk(c, v.shape[0])] = words


def _unpack_chunks(in_ref, rows, first_row=0):
    lo, hi = [], []
    for c in range(PACK_SUB):
        w = in_ref[_packed_chunk(c, rows, first_row)]
        lo.append(pltpu.bitcast(w << 16, F32))
        hi.append(pltpu.bitcast(w & jnp.uint32(0xFFFF0000), F32))
    return lo + hi


def _unpack_rows(in_ref, rows):
    return jnp.concatenate(_unpack_chunks(in_ref, rows), axis=1).astype(BF16)


def _merge_kernel(x_ref, attn_ref, lru_ref, g1_ref, wg_ref, woa_ref, wol_ref, wout_ref, g2_ref, wrt_ref,
                  x1_ref, h2p_ref, route_ref, counts_ref, cnt, tri):
    tm = SEQ_TILE

    @pl.when(pl.program_id(0) == 0)
    def _():
        cnt[...] = jnp.zeros_like(cnt)
        r = lax.broadcasted_iota(jnp.int32, (tm, tm), 0)
        c = lax.broadcasted_iota(jnp.int32, (tm, tm), 1)
        tri[...] = jnp.where(c < r, 1.0, 0.0).astype(BF16)

    x = x_ref[...]
    hb = _rms(x, g1_ref[...]).astype(BF16)
    gates = 0.5 * jnp.tanh(0.5 * _dot(hb, wg_ref[...])) + 0.5
    merged = gates[:, :D_MODEL] * _dot(attn_ref[...], woa_ref[...]) + gates[:, D_MODEL:] * _dot(lru_ref[...], wol_ref[...])
    x1 = x + _dot(merged.astype(BF16), wout_ref[...])
    x1_ref[...] = x1
    h2 = _rms(x1, g2_ref[...])
    _pack_rows(h2, h2p_ref)

    h_hi = h2.astype(BF16)
    h_lo = (h2 - h_hi.astype(F32)).astype(BF16)
    wrt = wrt_ref[...]
    hh = _dot(h_hi, wrt)
    logits = hh[:, :ROUTE_LANES] + hh[:, ROUTE_LANES:] + _dot(h_lo, wrt[:, :ROUTE_LANES])
    lane = lax.broadcasted_iota(jnp.int32, logits.shape, 1)
    big = jnp.int32(1 << 20)

    def first_argmax(v):
        m = jnp.max(v, axis=-1, keepdims=True)
        return m, jnp.min(jnp.where(v == m, lane, big), axis=-1, keepdims=True)

    gmask = lane < N_GROUPS
    gmax, gidx = first_argmax(jnp.where(gmask, logits, -jnp.inf))
    gsum = jnp.sum(jnp.where(gmask, jnp.exp(logits - gmax), 0.0), axis=-1, keepdims=True)
    g_w = 1.0 / gsum
    lo = N_GROUPS + EXPERTS_PER_GROUP * gidx
    el = jnp.where((lane >= lo) & (lane < lo + EXPERTS_PER_GROUP), logits, -jnp.inf)
    m1, i1 = first_argmax(el)
    m2, i2 = first_argmax(jnp.where(lane == i1, -jnp.inf, el))
    rr = jnp.exp(m2 - m1)
    w1 = g_w / (1.0 + rr)
    w2 = g_w * rr / (1.0 + rr)
    oh1 = lane == i1
    oh2 = lane == i2
    oh = jnp.where(oh1 | oh2, 1.0, 0.0)
    before = _dot(tri[...], oh.astype(BF16)) + cnt[...]
    r1 = jnp.sum(jnp.where(oh1, before, 0.0), axis=-1, keepdims=True)
    r2 = jnp.sum(jnp.where(oh2, before, 0.0), axis=-1, keepdims=True)
    cnt[...] = cnt[...] + jnp.sum(oh, axis=0, keepdims=True)
    counts_ref[...] = cnt[...]
    vals = (i1.astype(F32), i2.astype(F32), w1, w2, r1, r2)
    route = jnp.zeros_like(logits)
    for k, v in enumerate(vals):
        route = jnp.where(lane == k, v, route)
    route_ref[...] = route


def _merge(x2, attn, lru, g1, wg, woa, wol, wout, g2, wrt):
    N = x2.shape[0]
    tm = SEQ_TILE
    rows = lambda c: pl.BlockSpec((tm, c), lambda i: (i, 0))
    full = lambda r, c: pl.BlockSpec((r, c), lambda i: (0, 0))
    return pl.pallas_call(
        _merge_kernel,
        grid=(N // tm,),
        in_specs=[rows(D_MODEL), rows(ATTN_WIDTH), rows(LRU_WIDTH), full(1, D_MODEL), full(D_MODEL, 2 * D_MODEL),
                  full(ATTN_WIDTH, D_MODEL), full(LRU_WIDTH, D_MODEL), full(D_MODEL, D_MODEL), full(1, D_MODEL),
                  full(D_MODEL, 2 * ROUTE_LANES)],
        out_specs=[rows(D_MODEL), pl.BlockSpec((tm * PACK_SUB, LANES), lambda i: (i, 0)), rows(ROUTE_LANES),
                   full(1, ROUTE_LANES)],
        out_shape=[jax.ShapeDtypeStruct((N, D_MODEL), F32), jax.ShapeDtypeStruct((N * PACK_SUB, LANES), jnp.uint32),
                   jax.ShapeDtypeStruct((N, ROUTE_LANES), F32), jax.ShapeDtypeStruct((1, ROUTE_LANES), F32)],
        scratch_shapes=[pltpu.VMEM((1, ROUTE_LANES), F32), pltpu.VMEM((tm, tm), BF16)],
        compiler_params=pltpu.CompilerParams(dimension_semantics=("arbitrary",), vmem_limit_bytes=VMEM_LIMIT),
        name="merge_route",
    )(x2, attn, lru, g1, wg, woa, wol, wout, g2, wrt)


def _for_each_assignment(off_ref, fn):
    def group(gi, _):
        toks = [gi * DMA_GROUP + j for j in range(DMA_GROUP)]
        offs = [[off_ref[0, 0, tk * TOP_K + k] for k in range(TOP_K)] for tk in toks]
        for tk, o in zip(toks, offs):
            for k in range(TOP_K):
                fn(tk, k, pl.multiple_of(o[k], PACK_SUB))
        return 0
    lax.fori_loop(0, ROW_TILE // DMA_GROUP, group, 0)


def _dispatch_kernel(off_ref, h2p_hbm, xs_hbm, xin, in_sem, out_sem):
    tile_rows = ROW_TILE * PACK_SUB
    i = pl.program_id(0)
    n = pl.num_programs(0)
    slot = lax.rem(i, DISPATCH_SLOTS)

    def fetch(tile, s):
        return pltpu.make_async_copy(h2p_hbm.at[pl.ds(pl.multiple_of(tile * tile_rows, tile_rows), tile_rows)],
                                     xin.at[s], in_sem.at[s])

    def drain(s):
        for _ in range(TOP_K):
            pltpu.make_async_copy(xin.at[s], xs_hbm.at[pl.ds(0, tile_rows)], out_sem.at[s]).wait()

    @pl.when(i == 0)
    def _():
        fetch(0, 0).start()

    @pl.when(i + 1 < n)
    def _():
        fetch(i + 1, lax.rem(i + 1, DISPATCH_SLOTS)).start()

    fetch(i, slot).wait()

    def start(tk, k, off):
        pltpu.make_async_copy(xin.at[slot, pl.ds(pl.multiple_of(tk * PACK_SUB, PACK_SUB), PACK_SUB)],
                              xs_hbm.at[pl.ds(off, PACK_SUB)], out_sem.at[slot]).start()

    _for_each_assignment(off_ref, start)

    @pl.when(i > 0)
    def _():
        drain(lax.rem(i + DISPATCH_SLOTS - 1, DISPATCH_SLOTS))

    @pl.when(i == n - 1)
    def _():
        drain(slot)


def _dispatch(off3, h2p):
    return pl.pallas_call(
        _dispatch_kernel,
        grid=(off3.shape[0],),
        in_specs=[pl.BlockSpec((1, 1, TOP_K * ROW_TILE), lambda i: (i, 0, 0), memory_space=pltpu.SMEM),
                  pl.BlockSpec(memory_space=pl.ANY)],
        out_specs=pl.BlockSpec(memory_space=pl.ANY),
        out_shape=jax.ShapeDtypeStruct((TOP_K * h2p.shape[0], LANES), jnp.uint32),
        scratch_shapes=[pltpu.VMEM((DISPATCH_SLOTS, ROW_TILE * PACK_SUB, LANES), jnp.uint32),
                        pltpu.SemaphoreType.DMA((DISPATCH_SLOTS,)), pltpu.SemaphoreType.DMA((DISPATCH_SLOTS,))],
        compiler_params=pltpu.CompilerParams(dimension_semantics=("arbitrary",), has_side_effects=True),
        name="moe_dispatch",
    )(off3, h2p)


def _moe_kernel(vt_ref, ve_ref, vlo_ref, vhi_ref, xs_ref, wg_ref, wu_ref, wd_ref, ys_ref, wgb, wub, wdb):
    tm = MOE_TILE
    v = pl.program_id(0)
    nv = pl.num_programs(0)
    t = vt_ref[v]
    e = ve_ref[v]
    prev = jnp.maximum(v - 1, 0)

    @pl.when((v == 0) | (ve_ref[prev] != e))
    def _():
        wgb[...] = wg_ref[...].astype(BF16)
        wub[...] = wu_ref[...].astype(BF16)
        wdb[...] = wd_ref[...].astype(BF16)

    xb = _unpack_rows(xs_ref, tm)
    g = _dot(xb, wgb[...])
    u = _dot(xb, wub[...])
    hmid = (g * jax.nn.sigmoid(g)) * u
    words = _pack_words(_dot(hmid.astype(BF16), wdb[...]))
    first = (v == 0) | (vt_ref[prev] != t)

    @pl.when(first)
    def _():
        for c in range(PACK_SUB):
            ys_ref[_packed_chunk(c, tm)] = words[c]

    @pl.when(jnp.logical_not(first))
    def _():
        rows = t * tm + lax.broadcasted_iota(jnp.int32, (tm, LANES), 0)
        mine = (rows >= vlo_ref[v]) & (rows < vhi_ref[v])
        for c in range(PACK_SUB):
            ys_ref[_packed_chunk(c, tm)] = jnp.where(mine, words[c], ys_ref[_packed_chunk(c, tm)])


def _moe(vt, ve, vlo, vhi, xs, wg, wu, wd):
    tm = MOE_TILE
    n_rows = xs.shape[0] // PACK_SUB
    grid_spec = pltpu.PrefetchScalarGridSpec(
        num_scalar_prefetch=4,
        grid=(vt.shape[0],),
        in_specs=[pl.BlockSpec((tm * PACK_SUB, LANES), lambda v, vt, ve, lo, hi: (vt[v], 0)),
                  pl.BlockSpec((None, D_MODEL, D_EXPERT), lambda v, vt, ve, lo, hi: (ve[v], 0, 0)),
                  pl.BlockSpec((None, D_MODEL, D_EXPERT), lambda v, vt, ve, lo, hi: (ve[v], 0, 0)),
                  pl.BlockSpec((None, D_EXPERT, D_MODEL), lambda v, vt, ve, lo, hi: (ve[v], 0, 0))],
        out_specs=pl.BlockSpec((tm * PACK_SUB, LANES), lambda v, vt, ve, lo, hi: (vt[v], 0)),
        scratch_shapes=[pltpu.VMEM((D_MODEL, D_EXPERT), BF16), pltpu.VMEM((D_MODEL, D_EXPERT), BF16),
                        pltpu.VMEM((D_EXPERT, D_MODEL), BF16)],
    )
    return pl.pallas_call(
        _moe_kernel,
        grid_spec=grid_spec,
        out_shape=jax.ShapeDtypeStruct((n_rows * PACK_SUB, LANES), jnp.uint32),
        compiler_params=pltpu.CompilerParams(dimension_semantics=("arbitrary",), vmem_limit_bytes=VMEM_LIMIT),
        name="moe_experts",
    )(vt, ve, vlo, vhi, xs, wg, wu, wd)


def _combine_kernel(off_ref, offn_ref, x1_ref, route_ref, g_ref, ys_hbm, o_ref, ybuf, sem):
    tc = ROW_TILE
    i = pl.program_id(0)
    n = pl.num_programs(0)
    slot = lax.rem(i, 2)

    def gather(o_ref_, s):
        def start(tk, k, off):
            pltpu.make_async_copy(ys_hbm.at[pl.ds(off, PACK_SUB)],
                                  ybuf.at[s, pl.ds(pl.multiple_of((k * tc + tk) * PACK_SUB, PACK_SUB), PACK_SUB)],
                                  sem.at[s]).start()
        _for_each_assignment(o_ref_, start)

    @pl.when(i == 0)
    def _():
        gather(off_ref, 0)

    @pl.when(i + 1 < n)
    def _():
        gather(offn_ref, 1 - slot)

    pltpu.make_async_copy(ys_hbm.at[pl.ds(0, TOP_K * tc * PACK_SUB)], ybuf.at[slot], sem.at[slot]).wait()

    route = route_ref[...]
    n_chunks = D_MODEL // LANES
    z = [x1_ref[:, c * LANES:(c + 1) * LANES] for c in range(n_chunks)]
    for k in range(TOP_K):
        wk = route[:, TOP_K + k:TOP_K + k + 1]
        yk = _unpack_chunks(ybuf.at[slot], tc, first_row=k * tc)
        z = [zc + wk * yc for zc, yc in zip(z, yk)]
    ss = sum(jnp.sum(zc * zc, axis=-1, keepdims=True) for zc in z)
    inv = lax.rsqrt(ss * (1.0 / D_MODEL) + NORM_EPS)
    for c in range(n_chunks):
        o_ref[:, c * LANES:(c + 1) * LANES] = z[c] * inv * g_ref[:, c * LANES:(c + 1) * LANES]


def _combine(off3, x1, route, g, ys):
    N = x1.shape[0]
    tc = ROW_TILE
    nt = N // tc
    idx = lambda f: pl.BlockSpec((1, 1, TOP_K * tc), f, memory_space=pltpu.SMEM)
    return pl.pallas_call(
        _combine_kernel,
        grid=(nt,),
        in_specs=[idx(lambda i: (i, 0, 0)),
                  idx(lambda i: (jnp.minimum(i + 1, nt - 1), 0, 0)),
                  pl.BlockSpec((tc, D_MODEL), lambda i: (i, 0)),
                  pl.BlockSpec((tc, ROUTE_LANES), lambda i: (i, 0)),
                  pl.BlockSpec((1, D_MODEL), lambda i: (0, 0)),
                  pl.BlockSpec(memory_space=pl.ANY)],
        out_specs=pl.BlockSpec((tc, D_MODEL), lambda i: (i, 0)),
        out_shape=jax.ShapeDtypeStruct((N, D_MODEL), F32),
        scratch_shapes=[pltpu.VMEM((2, TOP_K * tc * PACK_SUB, LANES), jnp.uint32), pltpu.SemaphoreType.DMA((2,))],
        compiler_params=pltpu.CompilerParams(dimension_semantics=("arbitrary",), vmem_limit_bytes=VMEM_LIMIT),
        name="combine_norm",
    )(off3, off3, x1, route, g, ys)


def _block_diag(w):
    nb, c, _ = w.shape
    eye = jnp.eye(nb, dtype=w.dtype)
    return (eye[:, None, :, None] * w[:, :, None, :]).reshape(nb * c, nb * c)


def _visit_plan(counts, n_rows):
    tm = MOE_TILE
    n_tiles = n_rows // tm
    n_visits = n_tiles + N_EXPERTS - 1
    cnt = counts.astype(jnp.int32)
    ends = jnp.cumsum(cnt)
    starts = ends - cnt
    first_tile = starts // tm
    n_vis = jnp.where(cnt > 0, (ends - 1) // tm - first_tile + 1, 0)
    v_end = jnp.cumsum(n_vis)
    v_start = v_end - n_vis
    total = v_end[-1]
    v = jnp.arange(n_visits, dtype=jnp.int32)
    owner = lambda q: jnp.minimum(jnp.searchsorted(v_end, q, side='right'), ROUTE_LANES - 1).astype(jnp.int32)
    lane = owner(v)
    valid = v < total
    tile = first_tile[lane] + (v - v_start[lane])
    lo = jnp.maximum(starts[lane], tile * tm)
    hi = jnp.minimum(ends[lane], (tile + 1) * tm)
    vt = jnp.where(valid, tile, n_tiles - 1)
    ve = jnp.maximum(jnp.where(valid, lane, owner(total - 1)) - EXPERT_LANE0, 0)
    zero = jnp.zeros_like(lo)
    return starts, vt, ve, jnp.where(valid, lo, zero), jnp.where(valid, hi, zero)


def kernel(x, norm_mix_g, w_in, lambda_qk, subln_g, conv_w, conv_b, w_r, b_r, w_i, b_i, lru_lambda, w_o_attn, w_o_lru, w_out, norm_ffn_g, w_group, w_expert_router, w_gate, w_up, w_down, final_norm_g):
    B, S, D = x.shape
    N = B * S
    nt = S // SEQ_TILE
    depth = norm_mix_g.shape[0]
    assert depth == 1 and D == D_MODEL and S % SEQ_TILE == 0 and S % LRU_TILE == 0
    assert N % ROW_TILE == 0 and (N * TOP_K) % MOE_TILE == 0
    l = 0
    row = lambda v: v.reshape(1, -1).astype(F32)

    x2 = x.reshape(N, D)
    w_in_l = w_in[l]
    qT, k, vT, xr, yr = _inproj(x2, row(norm_mix_g[l]), w_in_l[:, :PROJ_COLS].astype(BF16), B, S)

    attn = _attn(qT, k.reshape(B, nt, SEQ_TILE, N_HEADS * K_COLS), vT, lambda_qk[l].reshape(4, HEAD_DIM).astype(F32),
                 row(subln_g[l]), B, S)

    lru = _lru(xr.reshape(B, S, LRU_WIDTH), yr.reshape(B, S, LRU_WIDTH), conv_w[l].astype(F32), row(conv_b[l]),
               _block_diag(w_r[l]).astype(BF16), _block_diag(w_i[l]).astype(BF16), row(b_r[l]), row(b_i[l]),
               row(lru_lambda[l]), B, S)

    w_route = jnp.concatenate(
        [w_group[l], jnp.transpose(w_expert_router[l], (1, 0, 2)).reshape(D, N_EXPERTS),
         jnp.zeros((D, ROUTE_LANES - N_GROUPS - N_EXPERTS), F32)], axis=1).astype(F32)
    w_route_hi = w_route.astype(BF16)
    w_route = jnp.concatenate([w_route_hi, (w_route - w_route_hi.astype(F32)).astype(BF16)], axis=1)
    x1, h2p, route, counts = _merge(x2, attn.reshape(N, ATTN_WIDTH), lru.reshape(N, LRU_WIDTH), row(norm_mix_g[l]),
                                    w_in_l[:, PROJ_COLS:].astype(BF16), w_o_attn[l].astype(BF16),
                                    w_o_lru[l].astype(BF16), w_out[l].astype(BF16), row(norm_ffn_g[l]), w_route)

    starts, vt, ve, vlo, vhi = _visit_plan(counts[0], N * TOP_K)
    lane = route[:, 0:TOP_K].astype(jnp.int32)
    rank = route[:, 2 * TOP_K:3 * TOP_K].astype(jnp.int32)
    off3 = ((jnp.take(starts, lane) + rank) * PACK_SUB).reshape(N // ROW_TILE, 1, TOP_K * ROW_TILE)

    xs = _dispatch(off3, h2p)
    ys = _moe(vt, ve, vlo, vhi, xs, w_gate[l], w_up[l], w_down[l])
    out = _combine(off3, x1, route, row(final_norm_g), ys)
    return out.reshape(B, S, D)
```

```python
import functools
import math

import jax
import jax.numpy as jnp
from jax import lax
from jax.experimental import pallas as pl
from jax.experimental.pallas import tpu as pltpu

F32 = jnp.float32
BF16 = jnp.bfloat16

D_MODEL = 1024
N_HEADS = 4
HEAD_DIM = 64
V_DIM = 2 * HEAD_DIM
ATTN_WIDTH = N_HEADS * V_DIM
V_ROWS = V_DIM + 16
K_COLS = 2 * V_DIM
POS_SPLIT = 3
POS_RADIX = 256
LRU_WIDTH = D_MODEL // 2
LRU_BLOCKS = 8
CONV_W = 4
LRU_C = 8.0
N_GROUPS = 4
EXPERTS_PER_GROUP = 8
N_EXPERTS = N_GROUPS * EXPERTS_PER_GROUP
TOP_K = 2
D_EXPERT = D_MODEL // 2
NORM_EPS = 1e-6
LAM_INIT = 0.8 - 0.6 * math.exp(-0.3 * 0)

QK_COLS = N_HEADS * 2 * HEAD_DIM
PROJ_COLS = 2 * QK_COLS + ATTN_WIDTH + 2 * LRU_WIDTH
ROUTE_LANES = 128
NEG_BIG = -1e30
LOG2E = math.log2(math.e)
ALIBI_SLOPES = tuple(2.0 ** (-8.0 * (h + 1) / N_HEADS) for h in range(N_HEADS))

SEQ_TILE = 512
LRU_TILE = 512
ROW_TILE = 256
MOE_TILE = 256
DMA_GROUP = 8
DISPATCH_SLOTS = 3
LANES = 128
SUBLANES = 8
PACK_SUB = D_MODEL // 2 // LANES
EXPERT_LANE0 = N_GROUPS
VMEM_LIMIT = 48 * 1024 * 1024


def _rms(x, g):
    return x * lax.rsqrt(jnp.mean(x * x, axis=-1, keepdims=True) + NORM_EPS) * g


def _dot(a, b):
    return jnp.dot(a, b, preferred_element_type=F32)


def _inproj_kernel(x_ref, g_ref, w_ref, qT_ref, k_ref, vT_ref, xr_ref, yr_ref):
    hb = _rms(x_ref[...], g_ref[...]).astype(BF16)

    def proj(lo, hi):
        return _dot(hb, w_ref[:, lo:hi])

    q = proj(0, QK_COLS) * (HEAD_DIM ** -0.5 * LOG2E)
    for h in range(N_HEADS):
        qT_ref[h] = q[:, h * V_DIM:(h + 1) * V_DIM].T.astype(BF16)
    k = proj(QK_COLS, 2 * QK_COLS).astype(BF16)
    r = lax.broadcasted_iota(jnp.int32, (k.shape[0], K_COLS - V_DIM), 0)
    lane = lax.broadcasted_iota(jnp.int32, r.shape, 1)
    a = r // POS_RADIX * POS_RADIX
    feat = jnp.where(lane < POS_SPLIT, a, jnp.where(lane < 2 * POS_SPLIT, r - a, 0)).astype(F32).astype(BF16)
    for h in range(N_HEADS):
        k_ref[:, h * K_COLS:h * K_COLS + V_DIM] = k[:, h * V_DIM:(h + 1) * V_DIM]
        k_ref[:, h * K_COLS + V_DIM:(h + 1) * K_COLS] = feat
    v = proj(2 * QK_COLS, 2 * QK_COLS + ATTN_WIDTH)
    for h in range(N_HEADS):
        vT_ref[h, :V_DIM, :] = v[:, h * V_DIM:(h + 1) * V_DIM].T.astype(BF16)
        pad_row = lax.broadcasted_iota(jnp.int32, (V_ROWS - V_DIM, v.shape[0]), 0)
        vT_ref[h, V_DIM:, :] = jnp.where(pad_row == 0, 1.0, 0.0).astype(BF16)
    c0 = 2 * QK_COLS + ATTN_WIDTH
    xr_ref[...] = proj(c0, c0 + LRU_WIDTH)
    yr_ref[...] = proj(c0 + LRU_WIDTH, c0 + 2 * LRU_WIDTH)


def _inproj(x2, g, w, B, S):
    N = B * S
    tm = SEQ_TILE
    nt = S // tm
    tile5 = pl.BlockSpec((None, N_HEADS, None, V_DIM, tm), lambda i: (i // nt, 0, i % nt, 0, 0))
    rows = lambda c: pl.BlockSpec((tm, c), lambda i: (i, 0))
    return pl.pallas_call(
        _inproj_kernel,
        grid=(N // tm,),
        in_specs=[rows(D_MODEL),
                  pl.BlockSpec((1, D_MODEL), lambda i: (0, 0)),
                  pl.BlockSpec((D_MODEL, PROJ_COLS), lambda i: (0, 0))],
        out_specs=[tile5, rows(N_HEADS * K_COLS),
                   pl.BlockSpec((None, N_HEADS, None, V_ROWS, tm), lambda i: (i // nt, 0, i % nt, 0, 0)),
                   rows(LRU_WIDTH), rows(LRU_WIDTH)],
        out_shape=[jax.ShapeDtypeStruct((B, N_HEADS, nt, V_DIM, tm), BF16),
                   jax.ShapeDtypeStruct((N, N_HEADS * K_COLS), BF16),
                   jax.ShapeDtypeStruct((B, N_HEADS, nt, V_ROWS, tm), BF16),
                   jax.ShapeDtypeStruct((N, LRU_WIDTH), F32),
                   jax.ShapeDtypeStruct((N, LRU_WIDTH), F32)],
        compiler_params=pltpu.CompilerParams(dimension_semantics=("parallel",),
                                             vmem_limit_bytes=VMEM_LIMIT),
        name="inproj",
    )(x2, g, w)


def _attn_kernel(qT_ref, k_ref, vT_ref, lam_ref, g_ref, o_ref, mask_ref, acc_ref, sa_ref, sb_ref, pa_ref, pb_ref):
    t = SEQ_TILE
    h = pl.program_id(1)
    i = pl.program_id(2)
    slope = LOG2E * jnp.where(h == 0, ALIBI_SLOPES[0], jnp.where(h == 1, ALIBI_SLOPES[1],
                              jnp.where(h == 2, ALIBI_SLOPES[2], ALIBI_SLOPES[3]))).astype(F32)
    last_off = jnp.maximum(i - 1, 0)

    @pl.when(i == 0)
    def _():
        r = lax.broadcasted_iota(jnp.int32, (t, t), 0)
        c = lax.broadcasted_iota(jnp.int32, (t, t), 1)
        mask_ref[...] = jnp.where(r <= c, 0.0, NEG_BIG)

    qf = qT_ref[...].astype(F32)
    row = lax.broadcasted_iota(jnp.int32, qf.shape, 0)
    sl = jnp.full(qf.shape, slope, F32)
    hi = sl.astype(BF16).astype(F32)
    mid = (sl - hi).astype(BF16).astype(F32)
    lo = (sl - hi - mid).astype(BF16).astype(F32)
    piece = jnp.where(row % 3 == 0, hi, jnp.where(row % 3 == 1, mid, lo))
    srows = jnp.where(row < 2 * POS_SPLIT, piece, 0.0).astype(BF16)
    qs = tuple(jnp.concatenate([jnp.where(sel, qf, 0.0).astype(BF16), srows], axis=0)
               for sel in (row < HEAD_DIM, row >= HEAD_DIM))
    acc_ref[...] = jnp.zeros_like(acc_ref)
    pb_ref[...] = jnp.zeros_like(pb_ref)

    def key_tile(tau):
        return jnp.where(tau <= 0, i, jnp.minimum(tau - 1, last_off))

    def stage_q(tau, s_ref, diagonal=False):
        kt = k_ref[key_tile(tau)]
        tile_max = []
        for mi in range(2):
            s = _dot(kt, qs[mi])
            if diagonal:
                s = s + mask_ref[...]
            s_ref[mi] = s
            tile_max.append(jnp.where(tau <= i, jnp.max(s, axis=0, keepdims=True), NEG_BIG))
        return tuple(tile_max)

    def stage_s(tau, s_ref, p_ref, ms, tile_max):
        cj = jnp.where(tau <= i, slope * (key_tile(tau) * t).astype(F32), NEG_BIG)
        m_out, alphas = [], []
        for mi in range(2):
            m_new = jnp.maximum(ms[mi], tile_max[mi] + cj)
            alphas.append(jnp.exp2(ms[mi] - m_new))
            p_ref[mi] = jnp.exp2(s_ref[mi] - (m_new - cj)).astype(BF16)
            m_out.append(m_new)
        return tuple(m_out), tuple(alphas)

    def stage_v(tau, p_ref, alphas):
        vt = vT_ref[key_tile(tau)]
        for mi in range(2):
            acc_ref[mi] = alphas[mi] * acc_ref[mi] + _dot(vt, p_ref[mi])

    def body(jj, carry):
        ms, alphas, tmax = carry[:2], carry[2:4], carry[4:]
        tau = 2 * jj
        tmax_b = stage_q(tau + 1, sb_ref)
        stage_v(tau - 1, pb_ref, alphas)
        ms, alphas = stage_s(tau, sa_ref, pa_ref, ms, tmax)
        tmax_a = stage_q(tau + 2, sa_ref)
        stage_v(tau, pa_ref, alphas)
        ms, alphas = stage_s(tau + 1, sb_ref, pb_ref, ms, tmax_b)
        return ms + alphas + tmax_a

    m_init = jnp.full((1, t), NEG_BIG, F32)
    one = jnp.ones((1, t), F32)
    tmax0 = stage_q(0, sa_ref, diagonal=True)
    n_pairs = jnp.right_shift(i + 2, 1)
    fin = lax.fori_loop(0, n_pairs, body, (m_init, m_init, one, one) + tmax0)
    stage_v(2 * n_pairs - 1, pb_ref, fin[2:4])

    lp = lam_ref[...]
    s1 = jnp.sum(lp[0:1] * lp[1:2], axis=-1, keepdims=True)
    s2 = jnp.sum(lp[2:3] * lp[3:4], axis=-1, keepdims=True)
    lam = jnp.exp(s1) - jnp.exp(s2) + LAM_INIT
    norm = [acc_ref[mi, :V_DIM, :] * (1.0 / acc_ref[mi, V_DIM:V_DIM + 1, :]) for mi in range(2)]
    oT = norm[0] - lam * norm[1]
    o = _rms(oT.T, g_ref[...]) * (1.0 - LAM_INIT)
    o_ref[...] = o.astype(BF16)


def _attn(qT, k4, vT, lam, g, B, S):
    t = SEQ_TILE
    nt = S // t
    return pl.pallas_call(
        _attn_kernel,
        grid=(B, N_HEADS, nt),
        in_specs=[pl.BlockSpec((None, None, None, V_DIM, t), lambda b, h, i: (b, h, i, 0, 0)),
                  pl.BlockSpec((None, nt, t, K_COLS), lambda b, h, i: (b, 0, 0, h)),
                  pl.BlockSpec((None, None, nt, V_ROWS, t), lambda b, h, i: (b, h, 0, 0, 0)),
                  pl.BlockSpec((4, HEAD_DIM), lambda b, h, i: (0, 0)),
                  pl.BlockSpec((1, V_DIM), lambda b, h, i: (0, 0))],
        out_specs=pl.BlockSpec((None, t, V_DIM), lambda b, h, i: (b, i, h)),
        out_shape=jax.ShapeDtypeStruct((B, S, ATTN_WIDTH), BF16),
        scratch_shapes=[pltpu.VMEM((t, t), F32), pltpu.VMEM((2, V_ROWS, t), F32),
                        pltpu.VMEM((2, t, t), F32), pltpu.VMEM((2, t, t), F32),
                        pltpu.VMEM((2, t, t), BF16), pltpu.VMEM((2, t, t), BF16)],
        compiler_params=pltpu.CompilerParams(dimension_semantics=("parallel", "arbitrary", "arbitrary"),
                                             vmem_limit_bytes=VMEM_LIMIT),
        name="diff_attn",
    )(qT, k4, vT, lam, g)


def _lru_kernel(xr_ref, yr_ref, cw_ref, cb_ref, wr_ref, wi_ref, br_ref, bi_ref, lam_ref, o_ref, xbuf, hc):
    T = LRU_TILE
    ti = pl.program_id(1)

    @pl.when(ti == 0)
    def _():
        xbuf[0:8] = jnp.zeros((8, LRU_WIDTH), F32)
        hc[...] = jnp.zeros_like(hc)

    x = xr_ref[...]
    xbuf[8:8 + T] = x
    cw = cw_ref[...]
    xc = cb_ref[...] + cw[3:4] * x
    for j in range(CONV_W - 1):
        xc = xc + cw[j:j + 1] * xbuf[5 + j:5 + j + T]
    xbuf[0:8] = x[T - 8:T]

    xb = xc.astype(BF16)
    r = jax.nn.sigmoid(_dot(xb, wr_ref[...]) + br_ref[...])
    ig = jax.nn.sigmoid(_dot(xb, wi_ref[...]) + bi_ref[...])
    z = -lam_ref[...]
    softplus = jnp.maximum(z, 0.0) + jnp.log1p(jnp.exp(-jnp.abs(z)))
    la = -LRU_C * r * softplus
    a = jnp.exp(la)
    m2 = -jnp.tanh(la) * (a * a + 1.0)
    mult = jnp.where(m2 > 0.0, m2 * lax.rsqrt(m2), 0.0)
    row = lax.broadcasted_iota(jnp.int32, (T, LRU_WIDTH), 0)
    mult = jnp.where((row == 0) & (ti == 0), 1.0, mult)
    u = (xc * ig) * mult

    nb = T // SUBLANES
    a3 = a.reshape(nb, SUBLANES, LRU_WIDTH)
    u3 = u.reshape(nb, SUBLANES, LRU_WIDTH)
    sub = lax.broadcasted_iota(jnp.int32, a3.shape, 1)
    d = 1
    while d < SUBLANES:
        valid = sub >= d
        u3 = jnp.where(valid, a3 * pltpu.roll(u3, d, 1) + u3, u3)
        a3 = jnp.where(valid, a3 * pltpu.roll(a3, d, 1), a3)
        d *= 2
    h = hc[...]
    blocks = []
    for b in range(nb):
        hb = u3[b] + a3[b] * h
        blocks.append(hb)
        h = hb[SUBLANES - 1:SUBLANES]
    hfull = jnp.concatenate(blocks, axis=0)
    hc[...] = h
    y = yr_ref[...]
    gelu = 0.5 * y * (1.0 + jnp.tanh(0.7978845608028654 * (y + 0.044715 * (y * y * y))))
    o_ref[...] = (hfull * gelu).astype(BF16)


def _lru(xr, yr, cw, cb, wr, wi, br, bi, lam, B, S):
    T = LRU_TILE
    seq = pl.BlockSpec((None, T, LRU_WIDTH), lambda b, t: (b, t, 0))
    full = lambda r, c: pl.BlockSpec((r, c), lambda b, t: (0, 0))
    return pl.pallas_call(
        _lru_kernel,
        grid=(B, S // T),
        in_specs=[seq, seq, full(CONV_W, LRU_WIDTH), full(1, LRU_WIDTH), full(LRU_WIDTH, LRU_WIDTH),
                  full(LRU_WIDTH, LRU_WIDTH), full(1, LRU_WIDTH), full(1, LRU_WIDTH), full(1, LRU_WIDTH)],
        out_specs=seq,
        out_shape=jax.ShapeDtypeStruct((B, S, LRU_WIDTH), BF16),
        scratch_shapes=[pltpu.VMEM((T + 8, LRU_WIDTH), F32), pltpu.VMEM((1, LRU_WIDTH), F32)],
        compiler_params=pltpu.CompilerParams(dimension_semantics=("arbitrary", "arbitrary"),
                                             vmem_limit_bytes=VMEM_LIMIT),
        name="rg_lru",
    )(xr, yr, cw, cb, wr, wi, br, bi, lam)


def _pack_words(v):
    bits = pltpu.bitcast(v.astype(BF16).astype(F32), jnp.uint32)
    half = D_MODEL // 2
    packed = (bits[:, :half] >> 16) | (bits[:, half:] & jnp.uint32(0xFFFF0000))
    return [packed[:, c * LANES:(c + 1) * LANES] for c in range(PACK_SUB)]


def _packed_chunk(c, rows, first_row=0):
    return (pl.ds(first_row * PACK_SUB + c, rows, stride=PACK_SUB), slice(None))


def _pack_rows(v, out_ref):
    for c, words in enumerate(_pack_words(v)):
        out_ref[_packed_chunk(c, v.shape[0])] = words


def _unpack_chunks(in_ref, rows, first_row=0):
    lo, hi = [], []
    for c in range(PACK_SUB):
        w = in_ref[_packed_chunk(c, rows, first_row)]
        lo.append(pltpu.bitcast(w << 16, F32))
        hi.append(pltpu.bitcast(w & jnp.uint32(0xFFFF0000), F32))
    return lo + hi


def _unpack_rows(in_ref, rows):
    return jnp.concatenate(_unpack_chunks(in_ref, rows), axis=1).astype(BF16)


def _merge_kernel(x_ref, attn_ref, lru_ref, g1_ref, wg_ref, woa_ref, wol_ref, wout_ref, g2_ref, wrt_ref,
                  x1_ref, h2p_ref, route_ref, counts_ref, cnt, tri):
    tm = SEQ_TILE

    @pl.when(pl.program_id(0) == 0)
    def _():
        cnt[...] = jnp.zeros_like(cnt)
        r = lax.broadcasted_iota(jnp.int32, (tm, tm), 0)
        c = lax.broadcasted_iota(jnp.int32, (tm, tm), 1)
        tri[...] = jnp.where(c < r, 1.0, 0.0).astype(BF16)

    x = x_ref[...]
    hb = _rms(x, g1_ref[...]).astype(BF16)
    gates = 0.5 * jnp.tanh(0.5 * _dot(hb, wg_ref[...])) + 0.5
    merged = gates[:, :D_MODEL] * _dot(attn_ref[...], woa_ref[...]) + gates[:, D_MODEL:] * _dot(lru_ref[...], wol_ref[...])
    x1 = x + _dot(merged.astype(BF16), wout_ref[...])
    x1_ref[...] = x1
    h2 = _rms(x1, g2_ref[...])
    _pack_rows(h2, h2p_ref)

    h_hi = h2.astype(BF16)
    h_lo = (h2 - h_hi.astype(F32)).astype(BF16)
    wrt = wrt_ref[...]
    hh = _dot(h_hi, wrt)
    logits = hh[:, :ROUTE_LANES] + hh[:, ROUTE_LANES:] + _dot(h_lo, wrt[:, :ROUTE_LANES])
    lane = lax.broadcasted_iota(jnp.int32, logits.shape, 1)
    big = jnp.int32(1 << 20)

    def first_argmax(v):
        m = jnp.max(v, axis=-1, keepdims=True)
        return m, jnp.min(jnp.where(v == m, lane, big), axis=-1, keepdims=True)

    gmask = lane < N_GROUPS
    gmax, gidx = first_argmax(jnp.where(gmask, logits, -jnp.inf))
    gsum = jnp.sum(jnp.where(gmask, jnp.exp(logits - gmax), 0.0), axis=-1, keepdims=True)
    g_w = 1.0 / gsum
    lo = N_GROUPS + EXPERTS_PER_GROUP * gidx
    el = jnp.where((lane >= lo) & (lane < lo + EXPERTS_PER_GROUP), logits, -jnp.inf)
    m1, i1 = first_argmax(el)
    m2, i2 = first_argmax(jnp.where(lane == i1, -jnp.inf, el))
    rr = jnp.exp(m2 - m1)
    w1 = g_w / (1.0 + rr)
    w2 = g_w * rr / (1.0 + rr)
    oh1 = lane == i1
    oh2 = lane == i2
    oh = jnp.where(oh1 | oh2, 1.0, 0.0)
    before = _dot(tri[...], oh.astype(BF16)) + cnt[...]
    r1 = jnp.sum(jnp.where(oh1, before, 0.0), axis=-1, keepdims=True)
    r2 = jnp.sum(jnp.where(oh2, before, 0.0), axis=-1, keepdims=True)
    cnt[...] = cnt[...] + jnp.sum(oh, axis=0, keepdims=True)
    counts_ref[...] = cnt[...]
    vals = (i1.astype(F32), i2.astype(F32), w1, w2, r1, r2)
    route = jnp.zeros_like(logits)
    for k, v in enumerate(vals):
        route = jnp.where(lane == k, v, route)
    route_ref[...] = route


def _merge(x2, attn, lru, g1, wg, woa, wol, wout, g2, wrt):
    N = x2.shape[0]
    tm = SEQ_TILE
    rows = lambda c: pl.BlockSpec((tm, c), lambda i: (i, 0))
    full = lambda r, c: pl.BlockSpec((r, c), lambda i: (0, 0))
    return pl.pallas_call(
        _merge_kernel,
        grid=(N // tm,),
        in_specs=[rows(D_MODEL), rows(ATTN_WIDTH), rows(LRU_WIDTH), full(1, D_MODEL), full(D_MODEL, 2 * D_MODEL),
                  full(ATTN_WIDTH, D_MODEL), full(LRU_WIDTH, D_MODEL), full(D_MODEL, D_MODEL), full(1, D_MODEL),
                  full(D_MODEL, 2 * ROUTE_LANES)],
        out_specs=[rows(D_MODEL), pl.BlockSpec((tm * PACK_SUB, LANES), lambda i: (i, 0)), rows(ROUTE_LANES),
                   full(1, ROUTE_LANES)],
        out_shape=[jax.ShapeDtypeStruct((N, D_MODEL), F32), jax.ShapeDtypeStruct((N * PACK_SUB, LANES), jnp.uint32),
                   jax.ShapeDtypeStruct((N, ROUTE_LANES), F32), jax.ShapeDtypeStruct((1, ROUTE_LANES), F32)],
        scratch_shapes=[pltpu.VMEM((1, ROUTE_LANES), F32), pltpu.VMEM((tm, tm), BF16)],
        compiler_params=pltpu.CompilerParams(dimension_semantics=("arbitrary",), vmem_limit_bytes=VMEM_LIMIT),
        name="merge_route",
    )(x2, attn, lru, g1, wg, woa, wol, wout, g2, wrt)


def _for_each_assignment(off_ref, fn):
    def group(gi, _):
        toks = [gi * DMA_GROUP + j for j in range(DMA_GROUP)]
        offs = [[off_ref[0, 0, tk * TOP_K + k] for k in range(TOP_K)] for tk in toks]
        for tk, o in zip(toks, offs):
            for k in range(TOP_K):
                fn(tk, k, pl.multiple_of(o[k], PACK_SUB))
        return 0
    lax.fori_loop(0, ROW_TILE // DMA_GROUP, group, 0)


def _dispatch_kernel(off_ref, h2p_hbm, xs_hbm, xin, in_sem, out_sem):
    tile_rows = ROW_TILE * PACK_SUB
    i = pl.program_id(0)
    n = pl.num_programs(0)
    slot = lax.rem(i, DISPATCH_SLOTS)

    def fetch(tile, s):
        return pltpu.make_async_copy(h2p_hbm.at[pl.ds(pl.multiple_of(tile * tile_rows, tile_rows), tile_rows)],
                                     xin.at[s], in_sem.at[s])

    def drain(s):
        for _ in range(TOP_K):
            pltpu.make_async_copy(xin.at[s], xs_hbm.at[pl.ds(0, tile_rows)], out_sem.at[s]).wait()

    @pl.when(i == 0)
    def _():
        fetch(0, 0).start()

    @pl.when(i + 1 < n)
    def _():
        fetch(i + 1, lax.rem(i + 1, DISPATCH_SLOTS)).start()

    fetch(i, slot).wait()

    def start(tk, k, off):
        pltpu.make_async_copy(xin.at[slot, pl.ds(pl.multiple_of(tk * PACK_SUB, PACK_SUB), PACK_SUB)],
                              xs_hbm.at[pl.ds(off, PACK_SUB)], out_sem.at[slot]).start(priority=k)

    _for_each_assignment(off_ref, start)

    @pl.when(i > 0)
    def _():
        drain(lax.rem(i + DISPATCH_SLOTS - 1, DISPATCH_SLOTS))

    @pl.when(i == n - 1)
    def _():
        drain(slot)


def _dispatch(off3, h2p):
    return pl.pallas_call(
        _dispatch_kernel,
        grid=(off3.shape[0],),
        in_specs=[pl.BlockSpec((1, 1, TOP_K * ROW_TILE), lambda i: (i, 0, 0), memory_space=pltpu.SMEM),
                  pl.BlockSpec(memory_space=pl.ANY)],
        out_specs=pl.BlockSpec(memory_space=pl.ANY),
        out_shape=jax.ShapeDtypeStruct((TOP_K * h2p.shape[0], LANES), jnp.uint32),
        scratch_shapes=[pltpu.VMEM((DISPATCH_SLOTS, ROW_TILE * PACK_SUB, LANES), jnp.uint32),
                        pltpu.SemaphoreType.DMA((DISPATCH_SLOTS,)), pltpu.SemaphoreType.DMA((DISPATCH_SLOTS,))],
        compiler_params=pltpu.CompilerParams(dimension_semantics=("arbitrary",), has_side_effects=True),
        name="moe_dispatch",
    )(off3, h2p)


def _moe_kernel(vt_ref, ve_ref, vlo_ref, vhi_ref, xs_ref, wg_ref, wu_ref, wd_ref, ys_ref, wgb, wub, wdb):
    tm = MOE_TILE
    v = pl.program_id(0)
    nv = pl.num_programs(0)
    t = vt_ref[v]
    e = ve_ref[v]
    prev = jnp.maximum(v - 1, 0)

    @pl.when((v == 0) | (ve_ref[prev] != e))
    def _():
        wgb[...] = wg_ref[...].astype(BF16)
        wub[...] = wu_ref[...].astype(BF16)
        wdb[...] = wd_ref[...].astype(BF16)

    xb = _unpack_rows(xs_ref, tm)
    g = _dot(xb, wgb[...])
    u = _dot(xb, wub[...])
    hmid = (g * jax.nn.sigmoid(g)) * u
    words = _pack_words(_dot(hmid.astype(BF16), wdb[...]))
    first = (v == 0) | (vt_ref[prev] != t)

    @pl.when(first)
    def _():
        for c in range(PACK_SUB):
            ys_ref[_packed_chunk(c, tm)] = words[c]

    @pl.when(jnp.logical_not(first))
    def _():
        rows = t * tm + lax.broadcasted_iota(jnp.int32, (tm, LANES), 0)
        mine = (rows >= vlo_ref[v]) & (rows < vhi_ref[v])
        for c in range(PACK_SUB):
            ys_ref[_packed_chunk(c, tm)] = jnp.where(mine, words[c], ys_ref[_packed_chunk(c, tm)])


def _moe(vt, ve, vlo, vhi, xs, wg, wu, wd):
    tm = MOE_TILE
    n_rows = xs.shape[0] // PACK_SUB
    grid_spec = pltpu.PrefetchScalarGridSpec(
        num_scalar_prefetch=4,
        grid=(vt.shape[0],),
        in_specs=[pl.BlockSpec((tm * PACK_SUB, LANES), lambda v, vt, ve, lo, hi: (vt[v], 0)),
                  pl.BlockSpec((None, D_MODEL, D_EXPERT), lambda v, vt, ve, lo, hi: (ve[v], 0, 0)),
                  pl.BlockSpec((None, D_MODEL, D_EXPERT), lambda v, vt, ve, lo, hi: (ve[v], 0, 0)),
                  pl.BlockSpec((None, D_EXPERT, D_MODEL), lambda v, vt, ve, lo, hi: (ve[v], 0, 0))],
        out_specs=pl.BlockSpec((tm * PACK_SUB, LANES), lambda v, vt, ve, lo, hi: (vt[v], 0)),
        scratch_shapes=[pltpu.VMEM((D_MODEL, D_EXPERT), BF16), pltpu.VMEM((D_MODEL, D_EXPERT), BF16),
                        pltpu.VMEM((D_EXPERT, D_MODEL), BF16)],
    )
    return pl.pallas_call(
        _moe_kernel,
        grid_spec=grid_spec,
        out_shape=jax.ShapeDtypeStruct((n_rows * PACK_SUB, LANES), jnp.uint32),
        compiler_params=pltpu.CompilerParams(dimension_semantics=("arbitrary",), vmem_limit_bytes=VMEM_LIMIT),
        name="moe_experts",
    )(vt, ve, vlo, vhi, xs, wg, wu, wd)


def _combine_kernel(off_ref, offn_ref, x1_ref, route_ref, g_ref, ys_hbm, o_ref, ybuf, sem):
    tc = ROW_TILE
    i = pl.program_id(0)
    n = pl.num_programs(0)
    slot = lax.rem(i, 2)

    def gather(o_ref_, s):
        def start(tk, k, off):
            pltpu.make_async_copy(ys_hbm.at[pl.ds(off, PACK_SUB)],
                                  ybuf.at[s, pl.ds(pl.multiple_of((k * tc + tk) * PACK_SUB, PACK_SUB), PACK_SUB)],
                                  sem.at[s]).start(priority=k)
        _for_each_assignment(o_ref_, start)

    @pl.when(i == 0)
    def _():
        gather(off_ref, 0)

    @pl.when(i + 1 < n)
    def _():
        gather(offn_ref, 1 - slot)

    pltpu.make_async_copy(ys_hbm.at[pl.ds(0, TOP_K * tc * PACK_SUB)], ybuf.at[slot], sem.at[slot]).wait()

    route = route_ref[...]
    n_chunks = D_MODEL // LANES
    z = [x1_ref[:, c * LANES:(c + 1) * LANES] for c in range(n_chunks)]
    for k in range(TOP_K):
        wk = route[:, TOP_K + k:TOP_K + k + 1]
        yk = _unpack_chunks(ybuf.at[slot], tc, first_row=k * tc)
        z = [zc + wk * yc for zc, yc in zip(z, yk)]
    ss = sum(jnp.sum(zc * zc, axis=-1, keepdims=True) for zc in z)
    inv = lax.rsqrt(ss * (1.0 / D_MODEL) + NORM_EPS)
    for c in range(n_chunks):
        o_ref[:, c * LANES:(c + 1) * LANES] = z[c] * inv * g_ref[:, c * LANES:(c + 1) * LANES]


def _combine(off3, x1, route, g, ys):
    N = x1.shape[0]
    tc = ROW_TILE
    nt = N // tc
    idx = lambda f: pl.BlockSpec((1, 1, TOP_K * tc), f, memory_space=pltpu.SMEM)
    return pl.pallas_call(
        _combine_kernel,
        grid=(nt,),
        in_specs=[idx(lambda i: (i, 0, 0)),
                  idx(lambda i: (jnp.minimum(i + 1, nt - 1), 0, 0)),
                  pl.BlockSpec((tc, D_MODEL), lambda i: (i, 0)),
                  pl.BlockSpec((tc, ROUTE_LANES), lambda i: (i, 0)),
                  pl.BlockSpec((1, D_MODEL), lambda i: (0, 0)),
                  pl.BlockSpec(memory_space=pl.ANY)],
        out_specs=pl.BlockSpec((tc, D_MODEL), lambda i: (i, 0)),
        out_shape=jax.ShapeDtypeStruct((N, D_MODEL), F32),
        scratch_shapes=[pltpu.VMEM((2, TOP_K * tc * PACK_SUB, LANES), jnp.uint32), pltpu.SemaphoreType.DMA((2,))],
        compiler_params=pltpu.CompilerParams(dimension_semantics=("arbitrary",), vmem_limit_bytes=VMEM_LIMIT),
        name="combine_norm",
    )(off3, off3, x1, route, g, ys)


def _block_diag(w):
    nb, c, _ = w.shape
    eye = jnp.eye(nb, dtype=w.dtype)
    return (eye[:, None, :, None] * w[:, :, None, :]).reshape(nb * c, nb * c)


def _visit_plan(counts, n_rows):
    tm = MOE_TILE
    n_tiles = n_rows // tm
    n_visits = n_tiles + N_EXPERTS - 1
    cnt = counts.astype(jnp.int32)
    ends = jnp.cumsum(cnt)
    starts = ends - cnt
    first_tile = starts // tm
    n_vis = jnp.where(cnt > 0, (ends - 1) // tm - first_tile + 1, 0)
    v_end = jnp.cumsum(n_vis)
    v_start = v_end - n_vis
    total = v_end[-1]
    v = jnp.arange(n_visits, dtype=jnp.int32)
    owner = lambda q: jnp.minimum(jnp.searchsorted(v_end, q, side='right'), ROUTE_LANES - 1).astype(jnp.int32)
    lane = owner(v)
    valid = v < total
    tile = first_tile[lane] + (v - v_start[lane])
    lo = jnp.maximum(starts[lane], tile * tm)
    hi = jnp.minimum(ends[lane], (tile + 1) * tm)
    vt = jnp.where(valid, tile, n_tiles - 1)
    ve = jnp.maximum(jnp.where(valid, lane, owner(total - 1)) - EXPERT_LANE0, 0)
    zero = jnp.zeros_like(lo)
    return starts, vt, ve, jnp.where(valid, lo, zero), jnp.where(valid, hi, zero)


def kernel(x, norm_mix_g, w_in, lambda_qk, subln_g, conv_w, conv_b, w_r, b_r, w_i, b_i, lru_lambda, w_o_attn, w_o_lru, w_out, norm_ffn_g, w_group, w_expert_router, w_gate, w_up, w_down, final_norm_g):
    B, S, D = x.shape
    N = B * S
    nt = S // SEQ_TILE
    depth = norm_mix_g.shape[0]
    assert depth == 1 and D == D_MODEL and S % SEQ_TILE == 0 and S % LRU_TILE == 0
    assert N % ROW_TILE == 0 and (N * TOP_K) % MOE_TILE == 0
    l = 0
    row = lambda v: v.reshape(1, -1).astype(F32)

    x2 = x.reshape(N, D)
    w_in_l = w_in[l]
    qT, k, vT, xr, yr = _inproj(x2, row(norm_mix_g[l]), w_in_l[:, :PROJ_COLS].astype(BF16), B, S)

    attn = _attn(qT, k.reshape(B, nt, SEQ_TILE, N_HEADS * K_COLS), vT, lambda_qk[l].reshape(4, HEAD_DIM).astype(F32),
                 row(subln_g[l]), B, S)

    lru = _lru(xr.reshape(B, S, LRU_WIDTH), yr.reshape(B, S, LRU_WIDTH), conv_w[l].astype(F32), row(conv_b[l]),
               _block_diag(w_r[l]).astype(BF16), _block_diag(w_i[l]).astype(BF16), row(b_r[l]), row(b_i[l]),
               row(lru_lambda[l]), B, S)

    w_route = jnp.concatenate(
        [w_group[l], jnp.transpose(w_expert_router[l], (1, 0, 2)).reshape(D, N_EXPERTS),
         jnp.zeros((D, ROUTE_LANES - N_GROUPS - N_EXPERTS), F32)], axis=1).astype(F32)
    w_route_hi = w_route.astype(BF16)
    w_route = jnp.concatenate([w_route_hi, (w_route - w_route_hi.astype(F32)).astype(BF16)], axis=1)
    x1, h2p, route, counts = _merge(x2, attn.reshape(N, ATTN_WIDTH), lru.reshape(N, LRU_WIDTH), row(norm_mix_g[l]),
                                    w_in_l[:, PROJ_COLS:].astype(BF16), w_o_attn[l].astype(BF16),
                                    w_o_lru[l].astype(BF16), w_out[l].astype(BF16), row(norm_ffn_g[l]), w_route)

    starts, vt, ve, vlo, vhi = _visit_plan(counts[0], N * TOP_K)
    lane = route[:, 0:TOP_K].astype(jnp.int32)
    rank = route[:, 2 * TOP_K:3 * TOP_K].astype(jnp.int32)
    lane_ids = jnp.arange(ROUTE_LANES, dtype=jnp.int32)
    start_of = jnp.sum(jnp.where(lane[..., None] == lane_ids, starts, 0), axis=-1)
    off3 = ((start_of + rank) * PACK_SUB).reshape(N // ROW_TILE, 1, TOP_K * ROW_TILE)

    xs = _dispatch(off3, h2p)
    ys = _moe(vt, ve, vlo, vhi, xs, w_gate[l], w_up[l], w_down[l])
    out = _combine(off3, x1, route, row(final_norm_g), ys)
    return out.reshape(B, S, D)
```

```python
import functools
import math

import jax
import jax.numpy as jnp
from jax import lax
from jax.experimental import pallas as pl
from jax.experimental.pallas import tpu as pltpu

F32 = jnp.float32
BF16 = jnp.bfloat16

D_MODEL = 1024
N_HEADS = 4
HEAD_DIM = 64
V_DIM = 2 * HEAD_DIM
ATTN_WIDTH = N_HEADS * V_DIM
V_ROWS = V_DIM + 16
K_COLS = 2 * V_DIM
POS_SPLIT = 3
POS_RADIX = 256
LRU_WIDTH = D_MODEL // 2
LRU_BLOCKS = 8
CONV_W = 4
LRU_C = 8.0
N_GROUPS = 4
EXPERTS_PER_GROUP = 8
N_EXPERTS = N_GROUPS * EXPERTS_PER_GROUP
TOP_K = 2
D_EXPERT = D_MODEL // 2
NORM_EPS = 1e-6
LAM_INIT = 0.8 - 0.6 * math.exp(-0.3 * 0)

QK_COLS = N_HEADS * 2 * HEAD_DIM
PROJ_COLS = 2 * QK_COLS + ATTN_WIDTH + 2 * LRU_WIDTH
ROUTE_LANES = 128
NEG_BIG = -1e30
LOG2E = math.log2(math.e)
ALIBI_SLOPES = tuple(2.0 ** (-8.0 * (h + 1) / N_HEADS) for h in range(N_HEADS))

SEQ_TILE = 512
LRU_TILE = 512
ROW_TILE = 256
MOE_TILE = 256
DMA_GROUP = 8
DISPATCH_SLOTS = 3
LANES = 128
SUBLANES = 8
PACK_SUB = D_MODEL // 2 // LANES
EXPERT_LANE0 = N_GROUPS
VMEM_LIMIT = 48 * 1024 * 1024


def _rms(x, g):
    return x * lax.rsqrt(jnp.mean(x * x, axis=-1, keepdims=True) + NORM_EPS) * g


def _dot(a, b):
    return jnp.dot(a, b, preferred_element_type=F32)


def _inproj_kernel(x_ref, g_ref, w_ref, qT_ref, k_ref, vT_ref, xr_ref, yr_ref):
    hb = _rms(x_ref[...], g_ref[...]).astype(BF16)

    def proj(lo, hi):
        return _dot(hb, w_ref[:, lo:hi])

    q = proj(0, QK_COLS) * (HEAD_DIM ** -0.5 * LOG2E)
    for h in range(N_HEADS):
        qT_ref[h] = q[:, h * V_DIM:(h + 1) * V_DIM].T.astype(BF16)
    k = proj(QK_COLS, 2 * QK_COLS).astype(BF16)
    r = lax.broadcasted_iota(jnp.int32, (k.shape[0], K_COLS - V_DIM), 0)
    lane = lax.broadcasted_iota(jnp.int32, r.shape, 1)
    a = r // POS_RADIX * POS_RADIX
    feat = jnp.where(lane < POS_SPLIT, a, jnp.where(lane < 2 * POS_SPLIT, r - a, 0)).astype(F32).astype(BF16)
    for h in range(N_HEADS):
        k_ref[:, h * K_COLS:h * K_COLS + V_DIM] = k[:, h * V_DIM:(h + 1) * V_DIM]
        k_ref[:, h * K_COLS + V_DIM:(h + 1) * K_COLS] = feat
    v = proj(2 * QK_COLS, 2 * QK_COLS + ATTN_WIDTH)
    for h in range(N_HEADS):
        vT_ref[h, :V_DIM, :] = v[:, h * V_DIM:(h + 1) * V_DIM].T.astype(BF16)
        pad_row = lax.broadcasted_iota(jnp.int32, (V_ROWS - V_DIM, v.shape[0]), 0)
        vT_ref[h, V_DIM:, :] = jnp.where(pad_row == 0, 1.0, 0.0).astype(BF16)
    c0 = 2 * QK_COLS + ATTN_WIDTH
    xr_ref[...] = proj(c0, c0 + LRU_WIDTH)
    yr_ref[...] = proj(c0 + LRU_WIDTH, c0 + 2 * LRU_WIDTH)


def _inproj(x2, g, w, B, S):
    N = B * S
    tm = SEQ_TILE
    nt = S // tm
    tile5 = pl.BlockSpec((None, N_HEADS, None, V_DIM, tm), lambda i: (i // nt, 0, i % nt, 0, 0))
    rows = lambda c: pl.BlockSpec((tm, c), lambda i: (i, 0))
    return pl.pallas_call(
        _inproj_kernel,
        grid=(N // tm,),
        in_specs=[rows(D_MODEL),
                  pl.BlockSpec((1, D_MODEL), lambda i: (0, 0)),
                  pl.BlockSpec((D_MODEL, PROJ_COLS), lambda i: (0, 0))],
        out_specs=[tile5, rows(N_HEADS * K_COLS),
                   pl.BlockSpec((None, N_HEADS, None, V_ROWS, tm), lambda i: (i // nt, 0, i % nt, 0, 0)),
                   rows(LRU_WIDTH), rows(LRU_WIDTH)],
        out_shape=[jax.ShapeDtypeStruct((B, N_HEADS, nt, V_DIM, tm), BF16),
                   jax.ShapeDtypeStruct((N, N_HEADS * K_COLS), BF16),
                   jax.ShapeDtypeStruct((B, N_HEADS, nt, V_ROWS, tm), BF16),
                   jax.ShapeDtypeStruct((N, LRU_WIDTH), F32),
                   jax.ShapeDtypeStruct((N, LRU_WIDTH), F32)],
        compiler_params=pltpu.CompilerParams(dimension_semantics=("parallel",),
                                             vmem_limit_bytes=VMEM_LIMIT),
        name="inproj",
    )(x2, g, w)


def _attn_kernel(qT_ref, k_ref, vT_ref, lam_ref, g_ref, o_ref, mask_ref, acc_ref, sa_ref, sb_ref, pa_ref, pb_ref):
    t = SEQ_TILE
    h = pl.program_id(1)
    i = pl.program_id(2)
    slope = LOG2E * jnp.where(h == 0, ALIBI_SLOPES[0], jnp.where(h == 1, ALIBI_SLOPES[1],
                              jnp.where(h == 2, ALIBI_SLOPES[2], ALIBI_SLOPES[3]))).astype(F32)

    @pl.when(i == 0)
    def _():
        r = lax.broadcasted_iota(jnp.int32, (t, t), 0)
        c = lax.broadcasted_iota(jnp.int32, (t, t), 1)
        mask_ref[...] = jnp.where(r <= c, 0.0, NEG_BIG)

    qf = qT_ref[...].astype(F32)
    row = lax.broadcasted_iota(jnp.int32, qf.shape, 0)
    sl = jnp.full(qf.shape, slope, F32)
    hi = sl.astype(BF16).astype(F32)
    mid = (sl - hi).astype(BF16).astype(F32)
    lo = (sl - hi - mid).astype(BF16).astype(F32)
    piece = jnp.where(row % 3 == 0, hi, jnp.where(row % 3 == 1, mid, lo))
    srows = jnp.where(row < 2 * POS_SPLIT, piece, 0.0).astype(BF16)
    qs = tuple(jnp.concatenate([jnp.where(sel, qf, 0.0).astype(BF16), srows], axis=0)
               for sel in (row < HEAD_DIM, row >= HEAD_DIM))
    acc_ref[...] = jnp.zeros_like(acc_ref)
    pb_ref[...] = jnp.zeros_like(pb_ref)

    def key_tile(tau):
        return jnp.where(tau <= 0, i, tau - 1)

    def stage_q(tau, s_ref, diagonal=False):
        kt = k_ref[key_tile(tau)]
        tile_max = []
        for mi in range(2):
            s = _dot(kt, qs[mi])
            if diagonal:
                s = s + mask_ref[...]
            s_ref[mi] = s
            tile_max.append(jnp.max(s, axis=0, keepdims=True))
        return tuple(tile_max)

    def stage_s(tau, s_ref, p_ref, ms, tile_max):
        cj = slope * (key_tile(tau) * t).astype(F32)
        m_out, alphas = [], []
        for mi in range(2):
            m_new = jnp.maximum(ms[mi], tile_max[mi] + cj)
            alphas.append(jnp.exp2(ms[mi] - m_new))
            p_ref[mi] = jnp.exp2(s_ref[mi] - (m_new - cj)).astype(BF16)
            m_out.append(m_new)
        return tuple(m_out), tuple(alphas)

    def stage_v(tau, p_ref, alphas):
        vt = vT_ref[key_tile(tau)]
        for mi in range(2):
            acc_ref[mi] = alphas[mi] * acc_ref[mi] + _dot(vt, p_ref[mi])

    def body(jj, carry):
        ms, alphas, tmax = carry[:2], carry[2:4], carry[4:]
        tau = 2 * jj
        tmax_b = stage_q(tau + 1, sb_ref)
        ms, alphas_a = stage_s(tau, sa_ref, pa_ref, ms, tmax)
        stage_v(tau - 1, pb_ref, alphas)
        tmax_a = stage_q(tau + 2, sa_ref)
        ms, alphas_b = stage_s(tau + 1, sb_ref, pb_ref, ms, tmax_b)
        stage_v(tau, pa_ref, alphas_a)
        return ms + alphas_b + tmax_a

    def finalize():
        lp = lam_ref[...]
        s1 = jnp.sum(lp[0:1] * lp[1:2], axis=-1, keepdims=True)
        s2 = jnp.sum(lp[2:3] * lp[3:4], axis=-1, keepdims=True)
        lam = jnp.exp(s1) - jnp.exp(s2) + LAM_INIT
        norm = [acc_ref[mi, :V_DIM, :] * (1.0 / acc_ref[mi, V_DIM:V_DIM + 1, :]) for mi in range(2)]
        oT = norm[0] - lam * norm[1]
        o = _rms(oT.T, g_ref[...]) * (1.0 - LAM_INIT)
        o_ref[...] = o.astype(BF16)

    m_init = jnp.full((1, t), NEG_BIG, F32)
    one = jnp.ones((1, t), F32)
    tmax0 = stage_q(0, sa_ref, diagonal=True)
    n_main = jnp.right_shift(i, 1)
    fin = lax.fori_loop(0, n_main, body, (m_init, m_init, one, one) + tmax0)
    ms, alphas, tmax = fin[:2], fin[2:4], fin[4:]
    tau = 2 * n_main
    odd_tiles = tau == i

    @pl.when(odd_tiles)
    def _():
        _, alphas_a = stage_s(tau, sa_ref, pa_ref, ms, tmax)
        stage_v(tau - 1, pb_ref, alphas)
        stage_v(tau, pa_ref, alphas_a)
        finalize()

    @pl.when(jnp.logical_not(odd_tiles))
    def _():
        tmax_b = stage_q(tau + 1, sb_ref)
        ms_a, alphas_a = stage_s(tau, sa_ref, pa_ref, ms, tmax)
        stage_v(tau - 1, pb_ref, alphas)
        _, alphas_b = stage_s(tau + 1, sb_ref, pb_ref, ms_a, tmax_b)
        stage_v(tau, pa_ref, alphas_a)
        stage_v(tau + 1, pb_ref, alphas_b)
        finalize()


def _attn(qT, k4, vT, lam, g, B, S):
    t = SEQ_TILE
    nt = S // t
    return pl.pallas_call(
        _attn_kernel,
        grid=(B, N_HEADS, nt),
        in_specs=[pl.BlockSpec((None, None, None, V_DIM, t), lambda b, h, i: (b, h, i, 0, 0)),
                  pl.BlockSpec((None, nt, t, K_COLS), lambda b, h, i: (b, 0, 0, h)),
                  pl.BlockSpec((None, None, nt, V_ROWS, t), lambda b, h, i: (b, h, 0, 0, 0)),
                  pl.BlockSpec((4, HEAD_DIM), lambda b, h, i: (0, 0)),
                  pl.BlockSpec((1, V_DIM), lambda b, h, i: (0, 0))],
        out_specs=pl.BlockSpec((None, t, V_DIM), lambda b, h, i: (b, i, h)),
        out_shape=jax.ShapeDtypeStruct((B, S, ATTN_WIDTH), BF16),
        scratch_shapes=[pltpu.VMEM((t, t), F32), pltpu.VMEM((2, V_ROWS, t), F32),
                        pltpu.VMEM((2, t, t), F32), pltpu.VMEM((2, t, t), F32),
                        pltpu.VMEM((2, t, t), BF16), pltpu.VMEM((2, t, t), BF16)],
        compiler_params=pltpu.CompilerParams(dimension_semantics=("parallel", "arbitrary", "arbitrary"),
                                             vmem_limit_bytes=VMEM_LIMIT),
        name="diff_attn",
    )(qT, k4, vT, lam, g)


def _lru_kernel(xr_ref, yr_ref, cw_ref, cb_ref, wr_ref, wi_ref, br_ref, bi_ref, lam_ref, o_ref, xbuf, hc):
    T = LRU_TILE
    ti = pl.program_id(1)

    @pl.when(ti == 0)
    def _():
        xbuf[0:8] = jnp.zeros((8, LRU_WIDTH), F32)
        hc[...] = jnp.zeros_like(hc)

    x = xr_ref[...]
    xbuf[8:8 + T] = x
    cw = cw_ref[...]
    xc = cb_ref[...] + cw[3:4] * x
    for j in range(CONV_W - 1):
        xc = xc + cw[j:j + 1] * xbuf[5 + j:5 + j + T]
    xbuf[0:8] = x[T - 8:T]

    xb = xc.astype(BF16)
    r = jax.nn.sigmoid(_dot(xb, wr_ref[...]) + br_ref[...])
    ig = jax.nn.sigmoid(_dot(xb, wi_ref[...]) + bi_ref[...])
    z = -lam_ref[...]
    softplus = jnp.maximum(z, 0.0) + jnp.log1p(jnp.exp(-jnp.abs(z)))
    la = -LRU_C * r * softplus
    a = jnp.exp(la)
    m2 = -jnp.tanh(la) * (a * a + 1.0)
    mult = jnp.where(m2 > 0.0, m2 * lax.rsqrt(m2), 0.0)
    row = lax.broadcasted_iota(jnp.int32, (T, LRU_WIDTH), 0)
    mult = jnp.where((row == 0) & (ti == 0), 1.0, mult)
    u = (xc * ig) * mult

    nb = T // SUBLANES
    a3 = a.reshape(nb, SUBLANES, LRU_WIDTH)
    u3 = u.reshape(nb, SUBLANES, LRU_WIDTH)
    sub = lax.broadcasted_iota(jnp.int32, a3.shape, 1)
    d = 1
    while d < SUBLANES:
        valid = sub >= d
        u3 = jnp.where(valid, a3 * pltpu.roll(u3, d, 1) + u3, u3)
        a3 = jnp.where(valid, a3 * pltpu.roll(a3, d, 1), a3)
        d *= 2
    h = hc[...]
    blocks = []
    for b in range(nb):
        hb = u3[b] + a3[b] * h
        blocks.append(hb)
        h = hb[SUBLANES - 1:SUBLANES]
    hfull = jnp.concatenate(blocks, axis=0)
    hc[...] = h
    y = yr_ref[...]
    gelu = 0.5 * y * (1.0 + jnp.tanh(0.7978845608028654 * (y + 0.044715 * (y * y * y))))
    o_ref[...] = (hfull * gelu).astype(BF16)


def _lru(xr, yr, cw, cb, wr, wi, br, bi, lam, B, S):
    T = LRU_TILE
    seq = pl.BlockSpec((None, T, LRU_WIDTH), lambda b, t: (b, t, 0))
    full = lambda r, c: pl.BlockSpec((r, c), lambda b, t: (0, 0))
    return pl.pallas_call(
        _lru_kernel,
        grid=(B, S // T),
        in_specs=[seq, seq, full(CONV_W, LRU_WIDTH), full(1, LRU_WIDTH), full(LRU_WIDTH, LRU_WIDTH),
                  full(LRU_WIDTH, LRU_WIDTH), full(1, LRU_WIDTH), full(1, LRU_WIDTH), full(1, LRU_WIDTH)],
        out_specs=seq,
        out_shape=jax.ShapeDtypeStruct((B, S, LRU_WIDTH), BF16),
        scratch_shapes=[pltpu.VMEM((T + 8, LRU_WIDTH), F32), pltpu.VMEM((1, LRU_WIDTH), F32)],
        compiler_params=pltpu.CompilerParams(dimension_semantics=("arbitrary", "arbitrary"),
                                             vmem_limit_bytes=VMEM_LIMIT),
        name="rg_lru",
    )(xr, yr, cw, cb, wr, wi, br, bi, lam)


def _pack_words(v):
    bits = pltpu.bitcast(v.astype(BF16).astype(F32), jnp.uint32)
    half = D_MODEL // 2
    packed = (bits[:, :half] >> 16) | (bits[:, half:] & jnp.uint32(0xFFFF0000))
    return [packed[:, c * LANES:(c + 1) * LANES] for c in range(PACK_SUB)]


def _packed_chunk(c, rows, first_row=0):
    return (pl.ds(first_row * PACK_SUB + c, rows, stride=PACK_SUB), slice(None))


def _pack_rows(v, out_ref):
    for c, words in enumerate(_pack_words(v)):
        out_ref[_packed_chunk(c, v.shape[0])] = words


def _unpack_chunks(in_ref, rows, first_row=0):
    lo, hi = [], []
    for c in range(PACK_SUB):
        w = in_ref[_packed_chunk(c, rows, first_row)]
        lo.append(pltpu.bitcast(w << 16, F32))
        hi.append(pltpu.bitcast(w & jnp.uint32(0xFFFF0000), F32))
    return lo + hi


def _unpack_rows(in_ref, rows):
    return jnp.concatenate(_unpack_chunks(in_ref, rows), axis=1).astype(BF16)


def _merge_kernel(x_ref, attn_ref, lru_ref, g1_ref, wg_ref, woa_ref, wol_ref, wout_ref, g2_ref, wrt_ref,
                  x1_ref, h2p_ref, route_ref, counts_ref, cnt, tri):
    tm = SEQ_TILE

    @pl.when(pl.program_id(0) == 0)
    def _():
        cnt[...] = jnp.zeros_like(cnt)
        r = lax.broadcasted_iota(jnp.int32, (tm, tm), 0)
        c = lax.broadcasted_iota(jnp.int32, (tm, tm), 1)
        tri[...] = jnp.where(c < r, 1.0, 0.0).astype(BF16)

    x = x_ref[...]
    hb = _rms(x, g1_ref[...]).astype(BF16)
    gates = 0.5 * jnp.tanh(0.5 * _dot(hb, wg_ref[...])) + 0.5
    merged = gates[:, :D_MODEL] * _dot(attn_ref[...], woa_ref[...]) + gates[:, D_MODEL:] * _dot(lru_ref[...], wol_ref[...])
    x1 = x + _dot(merged.astype(BF16), wout_ref[...])
    x1_ref[...] = x1
    h2 = _rms(x1, g2_ref[...])
    _pack_rows(h2, h2p_ref)

    h_hi = h2.astype(BF16)
    h_lo = (h2 - h_hi.astype(F32)).astype(BF16)
    wrt = wrt_ref[...]
    hh = _dot(h_hi, wrt)
    logits = hh[:, :ROUTE_LANES] + hh[:, ROUTE_LANES:] + _dot(h_lo, wrt[:, :ROUTE_LANES])
    lane = lax.broadcasted_iota(jnp.int32, logits.shape, 1)
    big = jnp.int32(1 << 20)

    def first_argmax(v):
        m = jnp.max(v, axis=-1, keepdims=True)
        return m, jnp.min(jnp.where(v == m, lane, big), axis=-1, keepdims=True)

    gmask = lane < N_GROUPS
    gmax, gidx = first_argmax(jnp.where(gmask, logits, -jnp.inf))
    gsum = jnp.sum(jnp.where(gmask, jnp.exp(logits - gmax), 0.0), axis=-1, keepdims=True)
    g_w = 1.0 / gsum
    lo = N_GROUPS + EXPERTS_PER_GROUP * gidx
    el = jnp.where((lane >= lo) & (lane < lo + EXPERTS_PER_GROUP), logits, -jnp.inf)
    m1, i1 = first_argmax(el)
    m2, i2 = first_argmax(jnp.where(lane == i1, -jnp.inf, el))
    rr = jnp.exp(m2 - m1)
    w1 = g_w / (1.0 + rr)
    w2 = g_w * rr / (1.0 + rr)
    oh1 = lane == i1
    oh2 = lane == i2
    oh = jnp.where(oh1 | oh2, 1.0, 0.0)
    before = _dot(tri[...], oh.astype(BF16)) + cnt[...]
    r1 = jnp.sum(jnp.where(oh1, before, 0.0), axis=-1, keepdims=True)
    r2 = jnp.sum(jnp.where(oh2, before, 0.0), axis=-1, keepdims=True)
    cnt[...] = cnt[...] + jnp.sum(oh, axis=0, keepdims=True)
    counts_ref[...] = cnt[...]
    vals = (i1.astype(F32), i2.astype(F32), w1, w2, r1, r2)
    route = jnp.zeros_like(logits)
    for k, v in enumerate(vals):
        route = jnp.where(lane == k, v, route)
    route_ref[...] = route


def _merge(x2, attn, lru, g1, wg, woa, wol, wout, g2, wrt):
    N = x2.shape[0]
    tm = SEQ_TILE
    rows = lambda c: pl.BlockSpec((tm, c), lambda i: (i, 0))
    full = lambda r, c: pl.BlockSpec((r, c), lambda i: (0, 0))
    return pl.pallas_call(
        _merge_kernel,
        grid=(N // tm,),
        in_specs=[rows(D_MODEL), rows(ATTN_WIDTH), rows(LRU_WIDTH), full(1, D_MODEL), full(D_MODEL, 2 * D_MODEL),
                  full(ATTN_WIDTH, D_MODEL), full(LRU_WIDTH, D_MODEL), full(D_MODEL, D_MODEL), full(1, D_MODEL),
                  full(D_MODEL, 2 * ROUTE_LANES)],
        out_specs=[rows(D_MODEL), pl.BlockSpec((tm * PACK_SUB, LANES), lambda i: (i, 0)), rows(ROUTE_LANES),
                   full(1, ROUTE_LANES)],
        out_shape=[jax.ShapeDtypeStruct((N, D_MODEL), F32), jax.ShapeDtypeStruct((N * PACK_SUB, LANES), jnp.uint32),
                   jax.ShapeDtypeStruct((N, ROUTE_LANES), F32), jax.ShapeDtypeStruct((1, ROUTE_LANES), F32)],
        scratch_shapes=[pltpu.VMEM((1, ROUTE_LANES), F32), pltpu.VMEM((tm, tm), BF16)],
        compiler_params=pltpu.CompilerParams(dimension_semantics=("arbitrary",), vmem_limit_bytes=VMEM_LIMIT),
        name="merge_route",
    )(x2, attn, lru, g1, wg, woa, wol, wout, g2, wrt)


def _for_each_assignment(off_ref, fn):
    def group(gi, _):
        toks = [gi * DMA_GROUP + j for j in range(DMA_GROUP)]
        offs = [[off_ref[0, 0, tk * TOP_K + k] for k in range(TOP_K)] for tk in toks]
        for tk, o in zip(toks, offs):
            for k in range(TOP_K):
                fn(tk, k, pl.multiple_of(o[k], PACK_SUB))
        return 0
    lax.fori_loop(0, ROW_TILE // DMA_GROUP, group, 0)


def _dispatch_kernel(off_ref, h2p_hbm, xs_hbm, xin, in_sem, out_sem):
    tile_rows = ROW_TILE * PACK_SUB
    i = pl.program_id(0)
    n = pl.num_programs(0)
    slot = lax.rem(i, DISPATCH_SLOTS)

    def fetch(tile, s):
        return pltpu.make_async_copy(h2p_hbm.at[pl.ds(pl.multiple_of(tile * tile_rows, tile_rows), tile_rows)],
                                     xin.at[s], in_sem.at[s])

    def drain(s):
        for _ in range(TOP_K):
            pltpu.make_async_copy(xin.at[s], xs_hbm.at[pl.ds(0, tile_rows)], out_sem.at[s]).wait()

    @pl.when(i == 0)
    def _():
        fetch(0, 0).start()

    @pl.when(i + 1 < n)
    def _():
        fetch(i + 1, lax.rem(i + 1, DISPATCH_SLOTS)).start()

    fetch(i, slot).wait()

    def start(tk, k, off):
        pltpu.make_async_copy(xin.at[slot, pl.ds(pl.multiple_of(tk * PACK_SUB, PACK_SUB), PACK_SUB)],
                              xs_hbm.at[pl.ds(off, PACK_SUB)], out_sem.at[slot]).start(priority=k)

    _for_each_assignment(off_ref, start)

    @pl.when(i > 0)
    def _():
        drain(lax.rem(i + DISPATCH_SLOTS - 1, DISPATCH_SLOTS))

    @pl.when(i == n - 1)
    def _():
        drain(slot)


def _dispatch(off3, h2p):
    return pl.pallas_call(
        _dispatch_kernel,
        grid=(off3.shape[0],),
        in_specs=[pl.BlockSpec((1, 1, TOP_K * ROW_TILE), lambda i: (i, 0, 0), memory_space=pltpu.SMEM),
                  pl.BlockSpec(memory_space=pl.ANY)],
        out_specs=pl.BlockSpec(memory_space=pl.ANY),
        out_shape=jax.ShapeDtypeStruct((TOP_K * h2p.shape[0], LANES), jnp.uint32),
        scratch_shapes=[pltpu.VMEM((DISPATCH_SLOTS, ROW_TILE * PACK_SUB, LANES), jnp.uint32),
                        pltpu.SemaphoreType.DMA((DISPATCH_SLOTS,)), pltpu.SemaphoreType.DMA((DISPATCH_SLOTS,))],
        compiler_params=pltpu.CompilerParams(dimension_semantics=("arbitrary",), has_side_effects=True),
        name="moe_dispatch",
    )(off3, h2p)


def _moe_kernel(vt_ref, ve_ref, vlo_ref, vhi_ref, xs_ref, wg_ref, wu_ref, wd_ref, ys_ref, wgb, wub, wdb):
    tm = MOE_TILE
    v = pl.program_id(0)
    nv = pl.num_programs(0)
    t = vt_ref[v]
    e = ve_ref[v]
    prev = jnp.maximum(v - 1, 0)

    @pl.when((v == 0) | (ve_ref[prev] != e))
    def _():
        wgb[...] = wg_ref[...].astype(BF16)
        wub[...] = wu_ref[...].astype(BF16)
        wdb[...] = wd_ref[...].astype(BF16)

    xb = _unpack_rows(xs_ref, tm)
    g = _dot(xb, wgb[...])
    u = _dot(xb, wub[...])
    hmid = (g * jax.nn.sigmoid(g)) * u
    words = _pack_words(_dot(hmid.astype(BF16), wdb[...]))
    first = (v == 0) | (vt_ref[prev] != t)

    @pl.when(first)
    def _():
        for c in range(PACK_SUB):
            ys_ref[_packed_chunk(c, tm)] = words[c]

    @pl.when(jnp.logical_not(first))
    def _():
        rows = t * tm + lax.broadcasted_iota(jnp.int32, (tm, LANES), 0)
        mine = (rows >= vlo_ref[v]) & (rows < vhi_ref[v])
        for c in range(PACK_SUB):
            ys_ref[_packed_chunk(c, tm)] = jnp.where(mine, words[c], ys_ref[_packed_chunk(c, tm)])


def _moe(vt, ve, vlo, vhi, xs, wg, wu, wd):
    tm = MOE_TILE
    n_rows = xs.shape[0] // PACK_SUB
    grid_spec = pltpu.PrefetchScalarGridSpec(
        num_scalar_prefetch=4,
        grid=(vt.shape[0],),
        in_specs=[pl.BlockSpec((tm * PACK_SUB, LANES), lambda v, vt, ve, lo, hi: (vt[v], 0)),
                  pl.BlockSpec((None, D_MODEL, D_EXPERT), lambda v, vt, ve, lo, hi: (ve[v], 0, 0)),
                  pl.BlockSpec((None, D_MODEL, D_EXPERT), lambda v, vt, ve, lo, hi: (ve[v], 0, 0)),
                  pl.BlockSpec((None, D_EXPERT, D_MODEL), lambda v, vt, ve, lo, hi: (ve[v], 0, 0))],
        out_specs=pl.BlockSpec((tm * PACK_SUB, LANES), lambda v, vt, ve, lo, hi: (vt[v], 0)),
        scratch_shapes=[pltpu.VMEM((D_MODEL, D_EXPERT), BF16), pltpu.VMEM((D_MODEL, D_EXPERT), BF16),
                        pltpu.VMEM((D_EXPERT, D_MODEL), BF16)],
    )
    return pl.pallas_call(
        _moe_kernel,
        grid_spec=grid_spec,
        out_shape=jax.ShapeDtypeStruct((n_rows * PACK_SUB, LANES), jnp.uint32),
        compiler_params=pltpu.CompilerParams(dimension_semantics=("arbitrary",), vmem_limit_bytes=VMEM_LIMIT),
        name="moe_experts",
    )(vt, ve, vlo, vhi, xs, wg, wu, wd)


def _combine_kernel(off_ref, offn_ref, x1_ref, route_ref, g_ref, ys_hbm, o_ref, ybuf, sem):
    tc = ROW_TILE
    i = pl.program_id(0)
    n = pl.num_programs(0)
    slot = lax.rem(i, 2)

    def gather(o_ref_, s):
        def start(tk, k, off):
            pltpu.make_async_copy(ys_hbm.at[pl.ds(off, PACK_SUB)],
                                  ybuf.at[s, pl.ds(pl.multiple_of((k * tc + tk) * PACK_SUB, PACK_SUB), PACK_SUB)],
                                  sem.at[s]).start(priority=k)
        _for_each_assignment(o_ref_, start)

    @pl.when(i == 0)
    def _():
        gather(off_ref, 0)

    @pl.when(i + 1 < n)
    def _():
        gather(offn_ref, 1 - slot)

    pltpu.make_async_copy(ys_hbm.at[pl.ds(0, TOP_K * tc * PACK_SUB)], ybuf.at[slot], sem.at[slot]).wait()

    route = route_ref[...]
    n_chunks = D_MODEL // LANES
    z = [x1_ref[:, c * LANES:(c + 1) * LANES] for c in range(n_chunks)]
    for k in range(TOP_K):
        wk = route[:, TOP_K + k:TOP_K + k + 1]
        yk = _unpack_chunks(ybuf.at[slot], tc, first_row=k * tc)
        z = [zc + wk * yc for zc, yc in zip(z, yk)]
    ss = sum(jnp.sum(zc * zc, axis=-1, keepdims=True) for zc in z)
    inv = lax.rsqrt(ss * (1.0 / D_MODEL) + NORM_EPS)
    for c in range(n_chunks):
        o_ref[:, c * LANES:(c + 1) * LANES] = z[c] * inv * g_ref[:, c * LANES:(c + 1) * LANES]


def _combine(off3, x1, route, g, ys):
    N = x1.shape[0]
    tc = ROW_TILE
    nt = N // tc
    idx = lambda f: pl.BlockSpec((1, 1, TOP_K * tc), f, memory_space=pltpu.SMEM)
    return pl.pallas_call(
        _combine_kernel,
        grid=(nt,),
        in_specs=[idx(lambda i: (i, 0, 0)),
                  idx(lambda i: (jnp.minimum(i + 1, nt - 1), 0, 0)),
                  pl.BlockSpec((tc, D_MODEL), lambda i: (i, 0)),
                  pl.BlockSpec((tc, ROUTE_LANES), lambda i: (i, 0)),
                  pl.BlockSpec((1, D_MODEL), lambda i: (0, 0)),
                  pl.BlockSpec(memory_space=pl.ANY)],
        out_specs=pl.BlockSpec((tc, D_MODEL), lambda i: (i, 0)),
        out_shape=jax.ShapeDtypeStruct((N, D_MODEL), F32),
        scratch_shapes=[pltpu.VMEM((2, TOP_K * tc * PACK_SUB, LANES), jnp.uint32), pltpu.SemaphoreType.DMA((2,))],
        compiler_params=pltpu.CompilerParams(dimension_semantics=("arbitrary",), vmem_limit_bytes=VMEM_LIMIT),
        name="combine_norm",
    )(off3, off3, x1, route, g, ys)


def _block_diag(w):
    nb, c, _ = w.shape
    eye = jnp.eye(nb, dtype=w.dtype)
    return (eye[:, None, :, None] * w[:, :, None, :]).reshape(nb * c, nb * c)


def _visit_plan(counts, n_rows):
    tm = MOE_TILE
    n_tiles = n_rows // tm
    n_visits = n_tiles + N_EXPERTS - 1
    cnt = counts.astype(jnp.int32)
    ends = jnp.cumsum(cnt)
    starts = ends - cnt
    first_tile = starts // tm
    n_vis = jnp.where(cnt > 0, (ends - 1) // tm - first_tile + 1, 0)
    v_end = jnp.cumsum(n_vis)
    v_start = v_end - n_vis
    total = v_end[-1]
    v = jnp.arange(n_visits, dtype=jnp.int32)
    owner = lambda q: jnp.minimum(jnp.searchsorted(v_end, q, side='right'), ROUTE_LANES - 1).astype(jnp.int32)
    lane = owner(v)
    valid = v < total
    tile = first_tile[lane] + (v - v_start[lane])
    lo = jnp.maximum(starts[lane], tile * tm)
    hi = jnp.minimum(ends[lane], (tile + 1) * tm)
    vt = jnp.where(valid, tile, n_tiles - 1)
    ve = jnp.maximum(jnp.where(valid, lane, owner(total - 1)) - EXPERT_LANE0, 0)
    zero = jnp.zeros_like(lo)
    return starts, vt, ve, jnp.where(valid, lo, zero), jnp.where(valid, hi, zero)


def kernel(x, norm_mix_g, w_in, lambda_qk, subln_g, conv_w, conv_b, w_r, b_r, w_i, b_i, lru_lambda, w_o_attn, w_o_lru, w_out, norm_ffn_g, w_group, w_expert_router, w_gate, w_up, w_down, final_norm_g):
    B, S, D = x.shape
    N = B * S
    nt = S // SEQ_TILE
    depth = norm_mix_g.shape[0]
    assert depth == 1 and D == D_MODEL and S % SEQ_TILE == 0 and S % LRU_TILE == 0
    assert N % ROW_TILE == 0 and (N * TOP_K) % MOE_TILE == 0
    l = 0
    row = lambda v: v.reshape(1, -1).astype(F32)

    x2 = x.reshape(N, D)
    w_in_l = w_in[l]
    qT, k, vT, xr, yr = _inproj(x2, row(norm_mix_g[l]), w_in_l[:, :PROJ_COLS].astype(BF16), B, S)

    attn = _attn(qT, k.reshape(B, nt, SEQ_TILE, N_HEADS * K_COLS), vT, lambda_qk[l].reshape(4, HEAD_DIM).astype(F32),
                 row(subln_g[l]), B, S)

    lru = _lru(xr.reshape(B, S, LRU_WIDTH), yr.reshape(B, S, LRU_WIDTH), conv_w[l].astype(F32), row(conv_b[l]),
               _block_diag(w_r[l]).astype(BF16), _block_diag(w_i[l]).astype(BF16), row(b_r[l]), row(b_i[l]),
               row(lru_lambda[l]), B, S)

    w_route = jnp.concatenate(
        [w_group[l], jnp.transpose(w_expert_router[l], (1, 0, 2)).reshape(D, N_EXPERTS),
         jnp.zeros((D, ROUTE_LANES - N_GROUPS - N_EXPERTS), F32)], axis=1).astype(F32)
    w_route_hi = w_route.astype(BF16)
    w_route = jnp.concatenate([w_route_hi, (w_route - w_route_hi.astype(F32)).astype(BF16)], axis=1)
    x1, h2p, route, counts = _merge(x2, attn.reshape(N, ATTN_WIDTH), lru.reshape(N, LRU_WIDTH), row(norm_mix_g[l]),
                                    w_in_l[:, PROJ_COLS:].astype(BF16), w_o_attn[l].astype(BF16),
                                    w_o_lru[l].astype(BF16), w_out[l].astype(BF16), row(norm_ffn_g[l]), w_route)

    starts, vt, ve, vlo, vhi = _visit_plan(counts[0], N * TOP_K)
    lane = route[:, 0:TOP_K].astype(jnp.int32)
    rank = route[:, 2 * TOP_K:3 * TOP_K].astype(jnp.int32)
    lane_ids = jnp.arange(ROUTE_LANES, dtype=jnp.int32)
    start_of = jnp.sum(jnp.where(lane[..., None] == lane_ids, starts, 0), axis=-1)
    off3 = ((start_of + rank) * PACK_SUB).reshape(N // ROW_TILE, 1, TOP_K * ROW_TILE)

    xs = _dispatch(off3, h2p)
    ys = _moe(vt, ve, vlo, vhi, xs, w_gate[l], w_up[l], w_down[l])
    out = _combine(off3, x1, route, row(final_norm_g), ys)
    return out.reshape(B, S, D)
```

```python
import functools
import math

import jax
import jax.numpy as jnp
from jax import lax
from jax.experimental import pallas as pl
from jax.experimental.pallas import tpu as pltpu

F32 = jnp.float32
BF16 = jnp.bfloat16

D_MODEL = 1024
N_HEADS = 4
HEAD_DIM = 64
V_DIM = 2 * HEAD_DIM
ATTN_WIDTH = N_HEADS * V_DIM
V_ROWS = V_DIM + 16
K_COLS = 2 * V_DIM
POS_SPLIT = 3
POS_RADIX = 256
LRU_WIDTH = D_MODEL // 2
LRU_BLOCKS = 8
CONV_W = 4
LRU_C = 8.0
N_GROUPS = 4
EXPERTS_PER_GROUP = 8
N_EXPERTS = N_GROUPS * EXPERTS_PER_GROUP
TOP_K = 2
D_EXPERT = D_MODEL // 2
NORM_EPS = 1e-6
LAM_INIT = 0.8 - 0.6 * math.exp(-0.3 * 0)

QK_COLS = N_HEADS * 2 * HEAD_DIM
PROJ_COLS = 2 * QK_COLS + ATTN_WIDTH + 2 * LRU_WIDTH
ROUTE_LANES = 128
NEG_BIG = -1e30
LOG2E = math.log2(math.e)
ALIBI_SLOPES = tuple(2.0 ** (-8.0 * (h + 1) / N_HEADS) for h in range(N_HEADS))

SEQ_TILE = 512
LRU_TILE = 512
ROW_TILE = 256
MERGE_CHAIN = 512
MOE_TILE = 256
DMA_GROUP = 8
DISPATCH_SLOTS = 3
LANES = 128
SUBLANES = 8
PACK_SUB = D_MODEL // 2 // LANES
EXPERT_LANE0 = N_GROUPS
VMEM_LIMIT = 48 * 1024 * 1024


def _rms(x, g):
    return x * lax.rsqrt(jnp.mean(x * x, axis=-1, keepdims=True) + NORM_EPS) * g


def _dot(a, b):
    return jnp.dot(a, b, preferred_element_type=F32)


def _inproj_kernel(x_ref, g_ref, w_ref, qT_ref, k_ref, vT_ref, xr_ref, yr_ref):
    hb = _rms(x_ref[...], g_ref[...]).astype(BF16)

    def proj(lo, hi):
        return _dot(hb, w_ref[:, lo:hi])

    q = proj(0, QK_COLS) * (HEAD_DIM ** -0.5 * LOG2E)
    for h in range(N_HEADS):
        qT_ref[h] = q[:, h * V_DIM:(h + 1) * V_DIM].T.astype(BF16)
    k = proj(QK_COLS, 2 * QK_COLS).astype(BF16)
    r = lax.broadcasted_iota(jnp.int32, (k.shape[0], K_COLS - V_DIM), 0)
    lane = lax.broadcasted_iota(jnp.int32, r.shape, 1)
    a = r // POS_RADIX * POS_RADIX
    feat = jnp.where(lane < POS_SPLIT, a, jnp.where(lane < 2 * POS_SPLIT, r - a, 0)).astype(F32).astype(BF16)
    for h in range(N_HEADS):
        k_ref[:, h * K_COLS:h * K_COLS + V_DIM] = k[:, h * V_DIM:(h + 1) * V_DIM]
        k_ref[:, h * K_COLS + V_DIM:(h + 1) * K_COLS] = feat
    v = proj(2 * QK_COLS, 2 * QK_COLS + ATTN_WIDTH)
    for h in range(N_HEADS):
        vT_ref[h, :V_DIM, :] = v[:, h * V_DIM:(h + 1) * V_DIM].T.astype(BF16)
        pad_row = lax.broadcasted_iota(jnp.int32, (V_ROWS - V_DIM, v.shape[0]), 0)
        vT_ref[h, V_DIM:, :] = jnp.where(pad_row == 0, 1.0, 0.0).astype(BF16)
    c0 = 2 * QK_COLS + ATTN_WIDTH
    xr_ref[...] = proj(c0, c0 + LRU_WIDTH)
    yr_ref[...] = proj(c0 + LRU_WIDTH, c0 + 2 * LRU_WIDTH)


def _inproj(x2, g, w, B, S):
    N = B * S
    tm = SEQ_TILE
    nt = S // tm
    tile5 = pl.BlockSpec((None, N_HEADS, None, V_DIM, tm), lambda i: (i // nt, 0, i % nt, 0, 0))
    rows = lambda c: pl.BlockSpec((tm, c), lambda i: (i, 0))
    return pl.pallas_call(
        _inproj_kernel,
        grid=(N // tm,),
        in_specs=[rows(D_MODEL),
                  pl.BlockSpec((1, D_MODEL), lambda i: (0, 0)),
                  pl.BlockSpec((D_MODEL, PROJ_COLS), lambda i: (0, 0))],
        out_specs=[tile5, rows(N_HEADS * K_COLS),
                   pl.BlockSpec((None, N_HEADS, None, V_ROWS, tm), lambda i: (i // nt, 0, i % nt, 0, 0)),
                   rows(LRU_WIDTH), rows(LRU_WIDTH)],
        out_shape=[jax.ShapeDtypeStruct((B, N_HEADS, nt, V_DIM, tm), BF16),
                   jax.ShapeDtypeStruct((N, N_HEADS * K_COLS), BF16),
                   jax.ShapeDtypeStruct((B, N_HEADS, nt, V_ROWS, tm), BF16),
                   jax.ShapeDtypeStruct((N, LRU_WIDTH), F32),
                   jax.ShapeDtypeStruct((N, LRU_WIDTH), F32)],
        compiler_params=pltpu.CompilerParams(dimension_semantics=("parallel",),
                                             vmem_limit_bytes=VMEM_LIMIT),
        name="inproj",
    )(x2, g, w)


def _attn_kernel(qT_ref, k_ref, vT_ref, lam_ref, g_ref, o_ref, mask_ref, acc_ref, sa_ref, sb_ref, pa_ref, pb_ref):
    t = SEQ_TILE
    h = pl.program_id(1)
    i = pl.program_id(2)
    slope = LOG2E * jnp.where(h == 0, ALIBI_SLOPES[0], jnp.where(h == 1, ALIBI_SLOPES[1],
                              jnp.where(h == 2, ALIBI_SLOPES[2], ALIBI_SLOPES[3]))).astype(F32)

    @pl.when(i == 0)
    def _():
        r = lax.broadcasted_iota(jnp.int32, (t, t), 0)
        c = lax.broadcasted_iota(jnp.int32, (t, t), 1)
        mask_ref[...] = jnp.where(r <= c, 0.0, NEG_BIG)

    qf = qT_ref[...].astype(F32)
    row = lax.broadcasted_iota(jnp.int32, qf.shape, 0)
    sl = jnp.full(qf.shape, slope, F32)
    hi = sl.astype(BF16).astype(F32)
    mid = (sl - hi).astype(BF16).astype(F32)
    lo = (sl - hi - mid).astype(BF16).astype(F32)
    piece = jnp.where(row % 3 == 0, hi, jnp.where(row % 3 == 1, mid, lo))
    srows = jnp.where(row < 2 * POS_SPLIT, piece, 0.0).astype(BF16)
    qs = tuple(jnp.concatenate([jnp.where(sel, qf, 0.0).astype(BF16), srows], axis=0)
               for sel in (row < HEAD_DIM, row >= HEAD_DIM))
    acc_ref[...] = jnp.zeros_like(acc_ref)
    pb_ref[...] = jnp.zeros_like(pb_ref)

    def key_tile(tau):
        return jnp.where(tau <= 0, i, tau - 1)

    def stage_q(tau, s_ref, diagonal=False):
        kt = k_ref[key_tile(tau)]
        tile_max = []
        for mi in range(2):
            s = _dot(kt, qs[mi])
            if diagonal:
                s = s + mask_ref[...]
            s_ref[mi] = s
            tile_max.append(jnp.max(s, axis=0, keepdims=True))
        return tuple(tile_max)

    def stage_s(tau, s_ref, p_ref, ms, tile_max):
        cj = slope * (key_tile(tau) * t).astype(F32)
        m_out, alphas = [], []
        for mi in range(2):
            m_new = jnp.maximum(ms[mi], tile_max[mi] + cj)
            alphas.append(jnp.exp2(ms[mi] - m_new))
            p_ref[mi] = jnp.exp2(s_ref[mi] - (m_new - cj)).astype(BF16)
            m_out.append(m_new)
        return tuple(m_out), tuple(alphas)

    def stage_v(tau, p_ref, alphas):
        vt = vT_ref[key_tile(tau)]
        for mi in range(2):
            acc_ref[mi] = alphas[mi] * acc_ref[mi] + _dot(vt, p_ref[mi])

    def body(jj, carry):
        ms, alphas, tmax = carry[:2], carry[2:4], carry[4:]
        tau = 2 * jj
        tmax_b = stage_q(tau + 1, sb_ref)
        ms, alphas_a = stage_s(tau, sa_ref, pa_ref, ms, tmax)
        stage_v(tau - 1, pb_ref, alphas)
        tmax_a = stage_q(tau + 2, sa_ref)
        ms, alphas_b = stage_s(tau + 1, sb_ref, pb_ref, ms, tmax_b)
        stage_v(tau, pa_ref, alphas_a)
        return ms + alphas_b + tmax_a

    def finalize():
        lp = lam_ref[...]
        s1 = jnp.sum(lp[0:1] * lp[1:2], axis=-1, keepdims=True)
        s2 = jnp.sum(lp[2:3] * lp[3:4], axis=-1, keepdims=True)
        lam = jnp.exp(s1) - jnp.exp(s2) + LAM_INIT
        norm = [acc_ref[mi, :V_DIM, :] * (1.0 / acc_ref[mi, V_DIM:V_DIM + 1, :]) for mi in range(2)]
        oT = norm[0] - lam * norm[1]
        o = _rms(oT.T, g_ref[...]) * (1.0 - LAM_INIT)
        o_ref[...] = o.astype(BF16)

    m_init = jnp.full((1, t), NEG_BIG, F32)
    one = jnp.ones((1, t), F32)
    tmax0 = stage_q(0, sa_ref, diagonal=True)
    n_main = jnp.right_shift(i, 1)
    fin = lax.fori_loop(0, n_main, body, (m_init, m_init, one, one) + tmax0)
    ms, alphas, tmax = fin[:2], fin[2:4], fin[4:]
    tau = 2 * n_main
    odd_tiles = tau == i

    @pl.when(odd_tiles)
    def _():
        _, alphas_a = stage_s(tau, sa_ref, pa_ref, ms, tmax)
        stage_v(tau - 1, pb_ref, alphas)
        stage_v(tau, pa_ref, alphas_a)
        finalize()

    @pl.when(jnp.logical_not(odd_tiles))
    def _():
        tmax_b = stage_q(tau + 1, sb_ref)
        ms_a, alphas_a = stage_s(tau, sa_ref, pa_ref, ms, tmax)
        stage_v(tau - 1, pb_ref, alphas)
        _, alphas_b = stage_s(tau + 1, sb_ref, pb_ref, ms_a, tmax_b)
        stage_v(tau, pa_ref, alphas_a)
        stage_v(tau + 1, pb_ref, alphas_b)
        finalize()


def _attn(qT, k4, vT, lam, g, B, S):
    t = SEQ_TILE
    nt = S // t
    return pl.pallas_call(
        _attn_kernel,
        grid=(B, N_HEADS, nt),
        in_specs=[pl.BlockSpec((None, None, None, V_DIM, t), lambda b, h, i: (b, h, i, 0, 0)),
                  pl.BlockSpec((None, nt, t, K_COLS), lambda b, h, i: (b, 0, 0, h)),
                  pl.BlockSpec((None, None, nt, V_ROWS, t), lambda b, h, i: (b, h, 0, 0, 0)),
                  pl.BlockSpec((4, HEAD_DIM), lambda b, h, i: (0, 0)),
                  pl.BlockSpec((1, V_DIM), lambda b, h, i: (0, 0))],
        out_specs=pl.BlockSpec((None, t, V_DIM), lambda b, h, i: (b, i, h)),
        out_shape=jax.ShapeDtypeStruct((B, S, ATTN_WIDTH), BF16),
        scratch_shapes=[pltpu.VMEM((t, t), F32), pltpu.VMEM((2, V_ROWS, t), F32),
                        pltpu.VMEM((2, t, t), F32), pltpu.VMEM((2, t, t), F32),
                        pltpu.VMEM((2, t, t), BF16), pltpu.VMEM((2, t, t), BF16)],
        compiler_params=pltpu.CompilerParams(dimension_semantics=("parallel", "arbitrary", "arbitrary"),
                                             vmem_limit_bytes=VMEM_LIMIT),
        name="diff_attn",
    )(qT, k4, vT, lam, g)


def _lru_kernel(xr_ref, yr_ref, cw_ref, cb_ref, wr_ref, wi_ref, br_ref, bi_ref, lam_ref, o_ref, xbuf, hc):
    T = LRU_TILE
    ti = pl.program_id(1)

    @pl.when(ti == 0)
    def _():
        xbuf[0:8] = jnp.zeros((8, LRU_WIDTH), F32)
        hc[...] = jnp.zeros_like(hc)

    x = xr_ref[...]
    xbuf[8:8 + T] = x
    cw = cw_ref[...]
    xc = cb_ref[...] + cw[3:4] * x
    for j in range(CONV_W - 1):
        xc = xc + cw[j:j + 1] * xbuf[5 + j:5 + j + T]
    xbuf[0:8] = x[T - 8:T]

    xb = xc.astype(BF16)
    r = jax.nn.sigmoid(_dot(xb, wr_ref[...]) + br_ref[...])
    ig = jax.nn.sigmoid(_dot(xb, wi_ref[...]) + bi_ref[...])
    z = -lam_ref[...]
    softplus = jnp.maximum(z, 0.0) + jnp.log1p(jnp.exp(-jnp.abs(z)))
    la = -LRU_C * r * softplus
    a = jnp.exp(la)
    m2 = -jnp.tanh(la) * (a * a + 1.0)
    mult = jnp.where(m2 > 0.0, m2 * lax.rsqrt(m2), 0.0)
    row = lax.broadcasted_iota(jnp.int32, (T, LRU_WIDTH), 0)
    mult = jnp.where((row == 0) & (ti == 0), 1.0, mult)
    u = (xc * ig) * mult

    nb = T // SUBLANES
    a3 = a.reshape(nb, SUBLANES, LRU_WIDTH)
    u3 = u.reshape(nb, SUBLANES, LRU_WIDTH)
    sub = lax.broadcasted_iota(jnp.int32, a3.shape, 1)
    d = 1
    while d < SUBLANES:
        valid = sub >= d
        u3 = jnp.where(valid, a3 * pltpu.roll(u3, d, 1) + u3, u3)
        a3 = jnp.where(valid, a3 * pltpu.roll(a3, d, 1), a3)
        d *= 2
    h = hc[...]
    blocks = []
    for b in range(nb):
        hb = u3[b] + a3[b] * h
        blocks.append(hb)
        h = hb[SUBLANES - 1:SUBLANES]
    hfull = jnp.concatenate(blocks, axis=0)
    hc[...] = h
    y = yr_ref[...]
    gelu = 0.5 * y * (1.0 + jnp.tanh(0.7978845608028654 * (y + 0.044715 * (y * y * y))))
    o_ref[...] = (hfull * gelu).astype(BF16)


def _lru(xr, yr, cw, cb, wr, wi, br, bi, lam, B, S):
    T = LRU_TILE
    seq = pl.BlockSpec((None, T, LRU_WIDTH), lambda b, t: (b, t, 0))
    full = lambda r, c: pl.BlockSpec((r, c), lambda b, t: (0, 0))
    return pl.pallas_call(
        _lru_kernel,
        grid=(B, S // T),
        in_specs=[seq, seq, full(CONV_W, LRU_WIDTH), full(1, LRU_WIDTH), full(LRU_WIDTH, LRU_WIDTH),
                  full(LRU_WIDTH, LRU_WIDTH), full(1, LRU_WIDTH), full(1, LRU_WIDTH), full(1, LRU_WIDTH)],
        out_specs=seq,
        out_shape=jax.ShapeDtypeStruct((B, S, LRU_WIDTH), BF16),
        scratch_shapes=[pltpu.VMEM((T + 8, LRU_WIDTH), F32), pltpu.VMEM((1, LRU_WIDTH), F32)],
        compiler_params=pltpu.CompilerParams(dimension_semantics=("arbitrary", "arbitrary"),
                                             vmem_limit_bytes=VMEM_LIMIT),
        name="rg_lru",
    )(xr, yr, cw, cb, wr, wi, br, bi, lam)


def _pack_words(v):
    bits = pltpu.bitcast(v.astype(BF16).astype(F32), jnp.uint32)
    half = D_MODEL // 2
    packed = (bits[:, :half] >> 16) | (bits[:, half:] & jnp.uint32(0xFFFF0000))
    return [packed[:, c * LANES:(c + 1) * LANES] for c in range(PACK_SUB)]


def _packed_chunk(c, rows, first_row=0):
    return (pl.ds(first_row * PACK_SUB + c, rows, stride=PACK_SUB), slice(None))


def _pack_rows(v, out_ref, first_row=0):
    for c, words in enumerate(_pack_words(v)):
        out_ref[_packed_chunk(c, v.shape[0], first_row)] = words


def _unpack_chunks(in_ref, rows, first_row=0):
    lo, hi = [], []
    for c in range(PACK_SUB):
        w = in_ref[_packed_chunk(c, rows, first_row)]
        lo.append(pltpu.bitcast(w << 16, F32))
        hi.append(pltpu.bitcast(w & jnp.uint32(0xFFFF0000), F32))
    return lo + hi


def _unpack_rows(in_ref, rows):
    return jnp.concatenate(_unpack_chunks(in_ref, rows), axis=1).astype(BF16)


def _merge_kernel(x_ref, attn_ref, lru_ref, g1_ref, wg_ref, woa_ref, wol_ref, wout_ref, g2_ref, wrt_ref,
                  x1_ref, h2p_ref, route_ref, counts_ref, cnt, tri):
    tm = MERGE_CHAIN

    @pl.when(pl.program_id(0) == 0)
    def _():
        cnt[...] = jnp.zeros_like(cnt)
        r = lax.broadcasted_iota(jnp.int32, (tm, tm), 0)
        c = lax.broadcasted_iota(jnp.int32, (tm, tm), 1)
        tri[...] = jnp.where(c < r, 1.0, 0.0).astype(BF16)

    counts = cnt[...]
    for r0 in range(0, SEQ_TILE, tm):
        counts = _merge_chain(r0, tm, counts, x_ref, attn_ref, lru_ref, g1_ref, wg_ref, woa_ref, wol_ref, wout_ref,
                              g2_ref, wrt_ref, x1_ref, h2p_ref, route_ref, tri)
    cnt[...] = counts
    counts_ref[...] = counts


def _merge_chain(r0, tm, counts, x_ref, attn_ref, lru_ref, g1_ref, wg_ref, woa_ref, wol_ref, wout_ref, g2_ref, wrt_ref,
                 x1_ref, h2p_ref, route_ref, tri):
    rs = slice(r0, r0 + tm)
    x = x_ref[rs, :]
    hb = _rms(x, g1_ref[...]).astype(BF16)
    gates = 0.5 * jnp.tanh(0.5 * _dot(hb, wg_ref[...])) + 0.5
    merged = gates[:, :D_MODEL] * _dot(attn_ref[rs, :], woa_ref[...]) + gates[:, D_MODEL:] * _dot(lru_ref[rs, :], wol_ref[...])
    x1 = x + _dot(merged.astype(BF16), wout_ref[...])
    x1_ref[rs, :] = x1
    h2 = _rms(x1, g2_ref[...])
    _pack_rows(h2, h2p_ref, first_row=r0)

    h_hi = h2.astype(BF16)
    h_lo = (h2 - h_hi.astype(F32)).astype(BF16)
    wrt = wrt_ref[...]
    hh = _dot(h_hi, wrt)
    logits = hh[:, :ROUTE_LANES] + hh[:, ROUTE_LANES:] + _dot(h_lo, wrt[:, :ROUTE_LANES])
    lane = lax.broadcasted_iota(jnp.int32, logits.shape, 1)
    big = jnp.int32(1 << 20)

    def first_argmax(v):
        m = jnp.max(v, axis=-1, keepdims=True)
        return m, jnp.min(jnp.where(v == m, lane, big), axis=-1, keepdims=True)

    gmask = lane < N_GROUPS
    gmax, gidx = first_argmax(jnp.where(gmask, logits, -jnp.inf))
    gsum = jnp.sum(jnp.where(gmask, jnp.exp(logits - gmax), 0.0), axis=-1, keepdims=True)
    g_w = 1.0 / gsum
    lo = N_GROUPS + EXPERTS_PER_GROUP * gidx
    el = jnp.where((lane >= lo) & (lane < lo + EXPERTS_PER_GROUP), logits, -jnp.inf)
    m1, i1 = first_argmax(el)
    m2, i2 = first_argmax(jnp.where(lane == i1, -jnp.inf, el))
    rr = jnp.exp(m2 - m1)
    w1 = g_w / (1.0 + rr)
    w2 = g_w * rr / (1.0 + rr)
    oh1 = lane == i1
    oh2 = lane == i2
    oh = jnp.where(oh1 | oh2, 1.0, 0.0)
    before = _dot(tri[...], oh.astype(BF16)) + counts
    r1 = jnp.sum(jnp.where(oh1, before, 0.0), axis=-1, keepdims=True)
    r2 = jnp.sum(jnp.where(oh2, before, 0.0), axis=-1, keepdims=True)
    vals = (i1.astype(F32), i2.astype(F32), w1, w2, r1, r2)
    route = jnp.zeros_like(logits)
    for k, v in enumerate(vals):
        route = jnp.where(lane == k, v, route)
    route_ref[rs, :] = route
    return counts + jnp.sum(oh, axis=0, keepdims=True)


def _merge(x2, attn, lru, g1, wg, woa, wol, wout, g2, wrt):
    N = x2.shape[0]
    tm = SEQ_TILE
    rows = lambda c: pl.BlockSpec((tm, c), lambda i: (i, 0))
    full = lambda r, c: pl.BlockSpec((r, c), lambda i: (0, 0))
    return pl.pallas_call(
        _merge_kernel,
        grid=(N // tm,),
        in_specs=[rows(D_MODEL), rows(ATTN_WIDTH), rows(LRU_WIDTH), full(1, D_MODEL), full(D_MODEL, 2 * D_MODEL),
                  full(ATTN_WIDTH, D_MODEL), full(LRU_WIDTH, D_MODEL), full(D_MODEL, D_MODEL), full(1, D_MODEL),
                  full(D_MODEL, 2 * ROUTE_LANES)],
        out_specs=[rows(D_MODEL), pl.BlockSpec((tm * PACK_SUB, LANES), lambda i: (i, 0)), rows(ROUTE_LANES),
                   full(1, ROUTE_LANES)],
        out_shape=[jax.ShapeDtypeStruct((N, D_MODEL), F32), jax.ShapeDtypeStruct((N * PACK_SUB, LANES), jnp.uint32),
                   jax.ShapeDtypeStruct((N, ROUTE_LANES), F32), jax.ShapeDtypeStruct((1, ROUTE_LANES), F32)],
        scratch_shapes=[pltpu.VMEM((1, ROUTE_LANES), F32), pltpu.VMEM((MERGE_CHAIN, MERGE_CHAIN), BF16)],
        compiler_params=pltpu.CompilerParams(dimension_semantics=("arbitrary",), vmem_limit_bytes=VMEM_LIMIT),
        name="merge_route",
    )(x2, attn, lru, g1, wg, woa, wol, wout, g2, wrt)


def _for_each_assignment(off_ref, fn, unrolled=False):
    def group(gi, _):
        toks = [gi * DMA_GROUP + j for j in range(DMA_GROUP)]
        offs = [[off_ref[0, 0, tk * TOP_K + k] for k in range(TOP_K)] for tk in toks]
        for tk, o in zip(toks, offs):
            for k in range(TOP_K):
                fn(tk, k, pl.multiple_of(o[k], PACK_SUB))
        return 0
    if unrolled:
        for gi in range(ROW_TILE // DMA_GROUP):
            group(gi, 0)
    else:
        lax.fori_loop(0, ROW_TILE // DMA_GROUP, group, 0)


def _dispatch_kernel(off_ref, h2p_hbm, xs_hbm, xin, in_sem, out_sem):
    tile_rows = ROW_TILE * PACK_SUB
    i = pl.program_id(0)
    n = pl.num_programs(0)
    slot = lax.rem(i, DISPATCH_SLOTS)

    def fetch(tile, s):
        return pltpu.make_async_copy(h2p_hbm.at[pl.ds(pl.multiple_of(tile * tile_rows, tile_rows), tile_rows)],
                                     xin.at[s], in_sem.at[s])

    def drain(s):
        for _ in range(TOP_K):
            pltpu.make_async_copy(xin.at[s], xs_hbm.at[pl.ds(0, tile_rows)], out_sem.at[s]).wait()

    @pl.when(i == 0)
    def _():
        fetch(0, 0).start()

    @pl.when(i + 1 < n)
    def _():
        fetch(i + 1, lax.rem(i + 1, DISPATCH_SLOTS)).start()

    fetch(i, slot).wait()

    def start(tk, k, off):
        pltpu.make_async_copy(xin.at[slot, pl.ds(pl.multiple_of(tk * PACK_SUB, PACK_SUB), PACK_SUB)],
                              xs_hbm.at[pl.ds(off, PACK_SUB)], out_sem.at[slot]).start(priority=k)

    _for_each_assignment(off_ref, start)

    @pl.when(i > 0)
    def _():
        drain(lax.rem(i + DISPATCH_SLOTS - 1, DISPATCH_SLOTS))

    @pl.when(i == n - 1)
    def _():
        drain(slot)


def _dispatch(off3, h2p):
    return pl.pallas_call(
        _dispatch_kernel,
        grid=(off3.shape[0],),
        in_specs=[pl.BlockSpec((1, 1, TOP_K * ROW_TILE), lambda i: (i, 0, 0), memory_space=pltpu.SMEM),
                  pl.BlockSpec(memory_space=pl.ANY)],
        out_specs=pl.BlockSpec(memory_space=pl.ANY),
        out_shape=jax.ShapeDtypeStruct((TOP_K * h2p.shape[0], LANES), jnp.uint32),
        scratch_shapes=[pltpu.VMEM((DISPATCH_SLOTS, ROW_TILE * PACK_SUB, LANES), jnp.uint32),
                        pltpu.SemaphoreType.DMA((DISPATCH_SLOTS,)), pltpu.SemaphoreType.DMA((DISPATCH_SLOTS,))],
        compiler_params=pltpu.CompilerParams(dimension_semantics=("arbitrary",), has_side_effects=True),
        name="moe_dispatch",
    )(off3, h2p)


def _moe_kernel(vt_ref, ve_ref, vlo_ref, vhi_ref, xs_ref, wg_ref, wu_ref, wd_ref, ys_ref, wgb, wub, wdb):
    tm = MOE_TILE
    v = pl.program_id(0)
    nv = pl.num_programs(0)
    t = vt_ref[v]
    e = ve_ref[v]
    prev = jnp.maximum(v - 1, 0)

    @pl.when((v == 0) | (ve_ref[prev] != e))
    def _():
        wgb[...] = wg_ref[...].astype(BF16)
        wub[...] = wu_ref[...].astype(BF16)
        wdb[...] = wd_ref[...].astype(BF16)

    xb = _unpack_rows(xs_ref, tm)
    g = _dot(xb, wgb[...])
    u = _dot(xb, wub[...])
    hmid = (g * jax.nn.sigmoid(g)) * u
    words = _pack_words(_dot(hmid.astype(BF16), wdb[...]))
    first = (v == 0) | (vt_ref[prev] != t)

    @pl.when(first)
    def _():
        for c in range(PACK_SUB):
            ys_ref[_packed_chunk(c, tm)] = words[c]

    @pl.when(jnp.logical_not(first))
    def _():
        rows = t * tm + lax.broadcasted_iota(jnp.int32, (tm, LANES), 0)
        mine = (rows >= vlo_ref[v]) & (rows < vhi_ref[v])
        for c in range(PACK_SUB):
            ys_ref[_packed_chunk(c, tm)] = jnp.where(mine, words[c], ys_ref[_packed_chunk(c, tm)])


def _moe(vt, ve, vlo, vhi, xs, wg, wu, wd):
    tm = MOE_TILE
    n_rows = xs.shape[0] // PACK_SUB
    grid_spec = pltpu.PrefetchScalarGridSpec(
        num_scalar_prefetch=4,
        grid=(vt.shape[0],),
        in_specs=[pl.BlockSpec((tm * PACK_SUB, LANES), lambda v, vt, ve, lo, hi: (vt[v], 0)),
                  pl.BlockSpec((None, D_MODEL, D_EXPERT), lambda v, vt, ve, lo, hi: (ve[v], 0, 0)),
                  pl.BlockSpec((None, D_MODEL, D_EXPERT), lambda v, vt, ve, lo, hi: (ve[v], 0, 0)),
                  pl.BlockSpec((None, D_EXPERT, D_MODEL), lambda v, vt, ve, lo, hi: (ve[v], 0, 0))],
        out_specs=pl.BlockSpec((tm * PACK_SUB, LANES), lambda v, vt, ve, lo, hi: (vt[v], 0)),
        scratch_shapes=[pltpu.VMEM((D_MODEL, D_EXPERT), BF16), pltpu.VMEM((D_MODEL, D_EXPERT), BF16),
                        pltpu.VMEM((D_EXPERT, D_MODEL), BF16)],
    )
    return pl.pallas_call(
        _moe_kernel,
        grid_spec=grid_spec,
        out_shape=jax.ShapeDtypeStruct((n_rows * PACK_SUB, LANES), jnp.uint32),
        compiler_params=pltpu.CompilerParams(dimension_semantics=("arbitrary",), vmem_limit_bytes=VMEM_LIMIT),
        name="moe_experts",
    )(vt, ve, vlo, vhi, xs, wg, wu, wd)


def _combine_kernel(off_ref, offn_ref, x1_ref, route_ref, g_ref, ys_hbm, o_ref, ybuf, sem):
    tc = ROW_TILE
    i = pl.program_id(0)
    n = pl.num_programs(0)
    slot = lax.rem(i, 2)

    def gather(o_ref_, s, unrolled):
        def start(tk, k, off):
            dst = (k * tc + tk) * PACK_SUB
            dst = dst if isinstance(dst, int) else pl.multiple_of(dst, PACK_SUB)
            pltpu.make_async_copy(ys_hbm.at[pl.ds(off, PACK_SUB)], ybuf.at[s, pl.ds(dst, PACK_SUB)],
                                  sem.at[s]).start(priority=k)
        _for_each_assignment(o_ref_, start, unrolled)

    def drain(s):
        pltpu.make_async_copy(ys_hbm.at[pl.ds(0, TOP_K * tc * PACK_SUB)], ybuf.at[s], sem.at[s]).wait()

    @pl.when(i == 0)
    def _():
        gather(off_ref, 0, unrolled=False)

    drain(slot)

    route = route_ref[...]
    n_chunks = D_MODEL // LANES
    z = [x1_ref[:, c * LANES:(c + 1) * LANES] for c in range(n_chunks)]
    for k in range(TOP_K):
        wk = route[:, TOP_K + k:TOP_K + k + 1]
        yk = _unpack_chunks(ybuf.at[slot], tc, first_row=k * tc)
        z = [zc + wk * yc for zc, yc in zip(z, yk)]
    ss = sum(jnp.sum(zc * zc, axis=-1, keepdims=True) for zc in z)
    inv = lax.rsqrt(ss * (1.0 / D_MODEL) + NORM_EPS)
    for c in range(n_chunks):
        o_ref[:, c * LANES:(c + 1) * LANES] = z[c] * inv * g_ref[:, c * LANES:(c + 1) * LANES]

    gather(offn_ref, 1 - slot, unrolled=True)

    @pl.when(i == n - 1)
    def _():
        drain(1 - slot)


def _combine(off3, x1, route, g, ys):
    N = x1.shape[0]
    tc = ROW_TILE
    nt = N // tc
    idx = lambda f: pl.BlockSpec((1, 1, TOP_K * tc), f, memory_space=pltpu.SMEM)
    return pl.pallas_call(
        _combine_kernel,
        grid=(nt,),
        in_specs=[idx(lambda i: (i, 0, 0)),
                  idx(lambda i: (jnp.minimum(i + 1, nt - 1), 0, 0)),
                  pl.BlockSpec((tc, D_MODEL), lambda i: (i, 0)),
                  pl.BlockSpec((tc, ROUTE_LANES), lambda i: (i, 0)),
                  pl.BlockSpec((1, D_MODEL), lambda i: (0, 0)),
                  pl.BlockSpec(memory_space=pl.ANY)],
        out_specs=pl.BlockSpec((tc, D_MODEL), lambda i: (i, 0)),
        out_shape=jax.ShapeDtypeStruct((N, D_MODEL), F32),
        scratch_shapes=[pltpu.VMEM((2, TOP_K * tc * PACK_SUB, LANES), jnp.uint32), pltpu.SemaphoreType.DMA((2,))],
        compiler_params=pltpu.CompilerParams(dimension_semantics=("arbitrary",), vmem_limit_bytes=VMEM_LIMIT),
        name="combine_norm",
    )(off3, off3, x1, route, g, ys)


def _block_diag(w):
    nb, c, _ = w.shape
    eye = jnp.eye(nb, dtype=w.dtype)
    return (eye[:, None, :, None] * w[:, :, None, :]).reshape(nb * c, nb * c)


def _visit_plan(counts, n_rows):
    tm = MOE_TILE
    n_tiles = n_rows // tm
    n_visits = n_tiles + N_EXPERTS - 1
    cnt = counts.astype(jnp.int32)
    lanes = jnp.arange(ROUTE_LANES, dtype=jnp.int32)
    upto = (lanes[None, :] <= lanes[:, None]).astype(jnp.int32)
    ends = upto @ cnt
    starts = ends - cnt
    first_tile = starts // tm
    n_vis = jnp.where(cnt > 0, (ends - 1) // tm - first_tile + 1, 0)
    v_end = upto @ n_vis
    v_start = v_end - n_vis
    total = v_end[-1]
    v = jnp.arange(n_visits, dtype=jnp.int32)
    vc = jnp.minimum(v, total - 1)
    own = ((vc[:, None] >= v_start[None, :]) & (vc[:, None] < v_end[None, :])).astype(jnp.int32)
    pick = lambda per_lane: own @ per_lane
    tile = jnp.maximum(pick(first_tile - v_start) + vc, 0)
    lo = jnp.maximum(pick(starts), tile * tm)
    hi = jnp.minimum(pick(ends), (tile + 1) * tm)
    valid = v < total
    ve = jnp.maximum(pick(lanes) - EXPERT_LANE0, 0)
    return starts, tile, ve, jnp.where(valid, lo, 0), jnp.where(valid, hi, 0)


def kernel(x, norm_mix_g, w_in, lambda_qk, subln_g, conv_w, conv_b, w_r, b_r, w_i, b_i, lru_lambda, w_o_attn, w_o_lru, w_out, norm_ffn_g, w_group, w_expert_router, w_gate, w_up, w_down, final_norm_g):
    B, S, D = x.shape
    N = B * S
    nt = S // SEQ_TILE
    depth = norm_mix_g.shape[0]
    assert depth == 1 and D == D_MODEL and S % SEQ_TILE == 0 and S % LRU_TILE == 0
    assert N % ROW_TILE == 0 and (N * TOP_K) % MOE_TILE == 0
    l = 0
    row = lambda v: v.reshape(1, -1).astype(F32)

    x2 = x.reshape(N, D)
    w_in_l = w_in[l]
    qT, k, vT, xr, yr = _inproj(x2, row(norm_mix_g[l]), w_in_l[:, :PROJ_COLS].astype(BF16), B, S)

    attn = _attn(qT, k.reshape(B, nt, SEQ_TILE, N_HEADS * K_COLS), vT, lambda_qk[l].reshape(4, HEAD_DIM).astype(F32),
                 row(subln_g[l]), B, S)

    lru = _lru(xr.reshape(B, S, LRU_WIDTH), yr.reshape(B, S, LRU_WIDTH), conv_w[l].astype(F32), row(conv_b[l]),
               _block_diag(w_r[l]).astype(BF16), _block_diag(w_i[l]).astype(BF16), row(b_r[l]), row(b_i[l]),
               row(lru_lambda[l]), B, S)

    w_route = jnp.concatenate(
        [w_group[l], jnp.transpose(w_expert_router[l], (1, 0, 2)).reshape(D, N_EXPERTS),
         jnp.zeros((D, ROUTE_LANES - N_GROUPS - N_EXPERTS), F32)], axis=1).astype(F32)
    w_route_hi = w_route.astype(BF16)
    w_route = jnp.concatenate([w_route_hi, (w_route - w_route_hi.astype(F32)).astype(BF16)], axis=1)
    x1, h2p, route, counts = _merge(x2, attn.reshape(N, ATTN_WIDTH), lru.reshape(N, LRU_WIDTH), row(norm_mix_g[l]),
                                    w_in_l[:, PROJ_COLS:].astype(BF16), w_o_attn[l].astype(BF16),
                                    w_o_lru[l].astype(BF16), w_out[l].astype(BF16), row(norm_ffn_g[l]), w_route)

    starts, vt, ve, vlo, vhi = _visit_plan(counts[0], N * TOP_K)
    lane = route[:, 0:TOP_K].astype(jnp.int32)
    rank = route[:, 2 * TOP_K:3 * TOP_K].astype(jnp.int32)
    lane_ids = jnp.arange(ROUTE_LANES, dtype=jnp.int32)
    start_of = jnp.sum(jnp.where(lane[..., None] == lane_ids, starts, 0), axis=-1)
    off3 = ((start_of + rank) * PACK_SUB).reshape(N // ROW_TILE, 1, TOP_K * ROW_TILE)

    xs = _dispatch(off3, h2p)
    ys = _moe(vt, ve, vlo, vhi, xs, w_gate[l], w_up[l], w_down[l])
    out = _combine(off3, x1, route, row(final_norm_g), ys)
    return out.reshape(B, S, D)
```

```python
import functools
import math

import jax
import jax.numpy as jnp
from jax import lax
from jax.experimental import pallas as pl
from jax.experimental.pallas import tpu as pltpu

F32 = jnp.float32
BF16 = jnp.bfloat16

D_MODEL = 1024
N_HEADS = 4
HEAD_DIM = 64
V_DIM = 2 * HEAD_DIM
ATTN_WIDTH = N_HEADS * V_DIM
V_ROWS = V_DIM + 16
K_COLS = 2 * V_DIM
POS_SPLIT = 3
POS_RADIX = 256
SKIP_MARGIN = 171.0
NORM_SLACK = 1.0201
LRU_WIDTH = D_MODEL // 2
LRU_BLOCKS = 8
CONV_W = 4
LRU_C = 8.0
N_GROUPS = 4
EXPERTS_PER_GROUP = 8
N_EXPERTS = N_GROUPS * EXPERTS_PER_GROUP
TOP_K = 2
D_EXPERT = D_MODEL // 2
NORM_EPS = 1e-6
LAM_INIT = 0.8 - 0.6 * math.exp(-0.3 * 0)

QK_COLS = N_HEADS * 2 * HEAD_DIM
PROJ_COLS = 2 * QK_COLS + ATTN_WIDTH + 2 * LRU_WIDTH
ROUTE_LANES = 128
NEG_BIG = -1e30
LOG2E = math.log2(math.e)
ALIBI_SLOPES = tuple(2.0 ** (-8.0 * (h + 1) / N_HEADS) for h in range(N_HEADS))

SEQ_TILE = 512
LRU_TILE = 512
ROW_TILE = 256
MERGE_CHAIN = 512
MOE_TILE = 256
DMA_GROUP = 8
DISPATCH_SLOTS = 3
LANES = 128
SUBLANES = 8
PACK_SUB = D_MODEL // 2 // LANES
EXPERT_LANE0 = N_GROUPS
VMEM_LIMIT = 48 * 1024 * 1024


def _rms(x, g):
    return x * lax.rsqrt(jnp.mean(x * x, axis=-1, keepdims=True) + NORM_EPS) * g


def _dot(a, b):
    return jnp.dot(a, b, preferred_element_type=F32)


def _inproj_kernel(x_ref, g_ref, w_ref, qT_ref, k_ref, vT_ref, xr_ref, yr_ref, kn_ref):
    hb = _rms(x_ref[...], g_ref[...]).astype(BF16)

    def proj(lo, hi):
        return _dot(hb, w_ref[:, lo:hi])

    q = proj(0, QK_COLS) * (HEAD_DIM ** -0.5 * LOG2E)
    for h in range(N_HEADS):
        qT_ref[h] = q[:, h * V_DIM:(h + 1) * V_DIM].T.astype(BF16)
    k = proj(QK_COLS, 2 * QK_COLS).astype(BF16)
    r = lax.broadcasted_iota(jnp.int32, (k.shape[0], K_COLS - V_DIM), 0)
    lane = lax.broadcasted_iota(jnp.int32, r.shape, 1)
    a = r // POS_RADIX * POS_RADIX
    feat = jnp.where(lane < POS_SPLIT, a, jnp.where(lane < 2 * POS_SPLIT, r - a, 0)).astype(F32).astype(BF16)
    for h in range(N_HEADS):
        k_ref[:, h * K_COLS:h * K_COLS + V_DIM] = k[:, h * V_DIM:(h + 1) * V_DIM]
        k_ref[:, h * K_COLS + V_DIM:(h + 1) * K_COLS] = feat
    k2 = jnp.square(k.astype(F32))
    for hm in range(2 * N_HEADS):
        row_n2 = jnp.sum(k2[:, hm * HEAD_DIM:(hm + 1) * HEAD_DIM], axis=-1, keepdims=True)
        kn_ref[hm:hm + 1, :] = jnp.broadcast_to(jnp.max(row_n2, axis=0, keepdims=True), (1, LANES))
    v = proj(2 * QK_COLS, 2 * QK_COLS + ATTN_WIDTH)
    for h in range(N_HEADS):
        vT_ref[h, :V_DIM, :] = v[:, h * V_DIM:(h + 1) * V_DIM].T.astype(BF16)
        pad_row = lax.broadcasted_iota(jnp.int32, (V_ROWS - V_DIM, v.shape[0]), 0)
        vT_ref[h, V_DIM:, :] = jnp.where(pad_row == 0, 1.0, 0.0).astype(BF16)
    c0 = 2 * QK_COLS + ATTN_WIDTH
    xr_ref[...] = proj(c0, c0 + LRU_WIDTH)
    yr_ref[...] = proj(c0 + LRU_WIDTH, c0 + 2 * LRU_WIDTH)


def _inproj(x2, g, w, B, S):
    N = B * S
    tm = SEQ_TILE
    nt = S // tm
    tile5 = pl.BlockSpec((None, N_HEADS, None, V_DIM, tm), lambda i: (i // nt, 0, i % nt, 0, 0))
    rows = lambda c: pl.BlockSpec((tm, c), lambda i: (i, 0))
    return pl.pallas_call(
        _inproj_kernel,
        grid=(N // tm,),
        in_specs=[rows(D_MODEL),
                  pl.BlockSpec((1, D_MODEL), lambda i: (0, 0)),
                  pl.BlockSpec((D_MODEL, PROJ_COLS), lambda i: (0, 0))],
        out_specs=[tile5, rows(N_HEADS * K_COLS),
                   pl.BlockSpec((None, N_HEADS, None, V_ROWS, tm), lambda i: (i // nt, 0, i % nt, 0, 0)),
                   rows(LRU_WIDTH), rows(LRU_WIDTH),
                   pl.BlockSpec((None, 2 * N_HEADS, LANES), lambda i: (i, 0, 0))],
        out_shape=[jax.ShapeDtypeStruct((B, N_HEADS, nt, V_DIM, tm), BF16),
                   jax.ShapeDtypeStruct((N, N_HEADS * K_COLS), BF16),
                   jax.ShapeDtypeStruct((B, N_HEADS, nt, V_ROWS, tm), BF16),
                   jax.ShapeDtypeStruct((N, LRU_WIDTH), F32),
                   jax.ShapeDtypeStruct((N, LRU_WIDTH), F32),
                   jax.ShapeDtypeStruct((N // tm, 2 * N_HEADS, LANES), F32)],
        compiler_params=pltpu.CompilerParams(dimension_semantics=("parallel",),
                                             vmem_limit_bytes=VMEM_LIMIT),
        name="inproj",
    )(x2, g, w)


def _attn_kernel(kn_ref, qT_ref, k_ref, vT_ref, lam_ref, g_ref, o_ref, mask_ref, acc_ref, sa_ref, sb_ref, pa_ref,
                 pb_ref):
    t = SEQ_TILE
    b = pl.program_id(0)
    h = pl.program_id(1)
    i = pl.program_id(2)
    slope = LOG2E * jnp.where(h == 0, ALIBI_SLOPES[0], jnp.where(h == 1, ALIBI_SLOPES[1],
                              jnp.where(h == 2, ALIBI_SLOPES[2], ALIBI_SLOPES[3]))).astype(F32)

    @pl.when(i == 0)
    def _():
        r = lax.broadcasted_iota(jnp.int32, (t, t), 0)
        c = lax.broadcasted_iota(jnp.int32, (t, t), 1)
        mask_ref[...] = jnp.where(r <= c, 0.0, NEG_BIG)

    qf = qT_ref[...].astype(F32)
    row = lax.broadcasted_iota(jnp.int32, qf.shape, 0)
    sl = jnp.full(qf.shape, slope, F32)
    hi = sl.astype(BF16).astype(F32)
    mid = (sl - hi).astype(BF16).astype(F32)
    lo = (sl - hi - mid).astype(BF16).astype(F32)
    piece = jnp.where(row % 3 == 0, hi, jnp.where(row % 3 == 1, mid, lo))
    srows = jnp.where(row < 2 * POS_SPLIT, piece, 0.0).astype(BF16)
    qs = tuple(jnp.concatenate([jnp.where(sel, qf, 0.0).astype(BF16), srows], axis=0)
               for sel in (row < HEAD_DIM, row >= HEAD_DIM))
    acc_ref[...] = jnp.zeros_like(acc_ref)
    pb_ref[...] = jnp.zeros_like(pb_ref)

    first_off = [0]

    def key_tile(tau):
        return jnp.where(tau <= 0, i, first_off[0] + tau - 1)

    def stage_q(tau, s_ref, diagonal=False):
        kt = k_ref[key_tile(tau)]
        tile_max = []
        for mi in range(2):
            s = _dot(kt, qs[mi])
            if diagonal:
                s = s + mask_ref[...]
            s_ref[mi] = s
            tile_max.append(jnp.max(s, axis=0, keepdims=True))
        return tuple(tile_max)

    def stage_s(tau, s_ref, p_ref, ms, tile_max):
        cj = slope * (key_tile(tau) * t).astype(F32)
        m_out, alphas = [], []
        for mi in range(2):
            m_new = jnp.maximum(ms[mi], tile_max[mi] + cj)
            alphas.append(jnp.exp2(ms[mi] - m_new))
            p_ref[mi] = jnp.exp2(s_ref[mi] - (m_new - cj)).astype(BF16)
            m_out.append(m_new)
        return tuple(m_out), tuple(alphas)

    def stage_v(tau, p_ref, alphas):
        vt = vT_ref[key_tile(tau)]
        for mi in range(2):
            acc_ref[mi] = alphas[mi] * acc_ref[mi] + _dot(vt, p_ref[mi])

    def body(jj, carry):
        ms, alphas, tmax = carry[:2], carry[2:4], carry[4:]
        tau = 2 * jj
        tmax_b = stage_q(tau + 1, sb_ref)
        ms, alphas_a = stage_s(tau, sa_ref, pa_ref, ms, tmax)
        stage_v(tau - 1, pb_ref, alphas)
        tmax_a = stage_q(tau + 2, sa_ref)
        ms, alphas_b = stage_s(tau + 1, sb_ref, pb_ref, ms, tmax_b)
        stage_v(tau, pa_ref, alphas_a)
        return ms + alphas_b + tmax_a

    def finalize():
        lp = lam_ref[...]
        s1 = jnp.sum(lp[0:1] * lp[1:2], axis=-1, keepdims=True)
        s2 = jnp.sum(lp[2:3] * lp[3:4], axis=-1, keepdims=True)
        lam = jnp.exp(s1) - jnp.exp(s2) + LAM_INIT
        norm = [acc_ref[mi, :V_DIM, :] * (1.0 / acc_ref[mi, V_DIM:V_DIM + 1, :]) for mi in range(2)]
        oT = norm[0] - lam * norm[1]
        o = _rms(oT.T, g_ref[...]) * (1.0 - LAM_INIT)
        o_ref[...] = o.astype(BF16)

    m_init = jnp.full((1, t), NEG_BIG, F32)
    one = jnp.ones((1, t), F32)
    tmax0 = stage_q(0, sa_ref, diagonal=True)

    cj0 = slope * (i * t).astype(F32)
    qn2, m_low = [], []
    for mi in range(2):
        qsq = jnp.square(qf[mi * HEAD_DIM:(mi + 1) * HEAD_DIM, :])
        qn2.append(jnp.max(jnp.sum(qsq, axis=0, keepdims=True), axis=1, keepdims=True)[0, 0])
        m_low.append(jnp.min(tmax0[mi], axis=1, keepdims=True)[0, 0] + cj0)

    def skippable(j):
        ok = None
        for mi in range(2):
            kn2 = kn_ref[((b * pl.num_programs(2) + j) * N_HEADS + h) * 2 + mi]
            room = m_low[mi] - SKIP_MARGIN - slope * (t - 1) - slope * (j * t).astype(F32)
            fits = (room > 0.0) & (qn2[mi] * kn2 * NORM_SLACK < room * room)
            ok = fits if ok is None else ok & fits
        return ok

    first_off[0] = lax.fori_loop(0, i, lambda j, j0: jnp.where((j0 == j) & skippable(j), j + 1, j0), 0)
    n_off = i - first_off[0]

    n_main = jnp.right_shift(n_off, 1)
    fin = lax.fori_loop(0, n_main, body, (m_init, m_init, one, one) + tmax0)
    ms, alphas, tmax = fin[:2], fin[2:4], fin[4:]
    tau = 2 * n_main
    odd_tiles = tau == n_off

    @pl.when(odd_tiles)
    def _():
        _, alphas_a = stage_s(tau, sa_ref, pa_ref, ms, tmax)
        stage_v(tau - 1, pb_ref, alphas)
        stage_v(tau, pa_ref, alphas_a)
        finalize()

    @pl.when(jnp.logical_not(odd_tiles))
    def _():
        tmax_b = stage_q(tau + 1, sb_ref)
        ms_a, alphas_a = stage_s(tau, sa_ref, pa_ref, ms, tmax)
        stage_v(tau - 1, pb_ref, alphas)
        _, alphas_b = stage_s(tau + 1, sb_ref, pb_ref, ms_a, tmax_b)
        stage_v(tau, pa_ref, alphas_a)
        stage_v(tau + 1, pb_ref, alphas_b)
        finalize()


def _attn(kn, qT, k4, vT, lam, g, B, S):
    t = SEQ_TILE
    nt = S // t
    grid_spec = pltpu.PrefetchScalarGridSpec(
        num_scalar_prefetch=1,
        grid=(B, N_HEADS, nt),
        in_specs=[pl.BlockSpec((None, None, None, V_DIM, t), lambda b, h, i, kn: (b, h, i, 0, 0)),
                  pl.BlockSpec((None, nt, t, K_COLS), lambda b, h, i, kn: (b, 0, 0, h)),
                  pl.BlockSpec((None, None, nt, V_ROWS, t), lambda b, h, i, kn: (b, h, 0, 0, 0)),
                  pl.BlockSpec((4, HEAD_DIM), lambda b, h, i, kn: (0, 0)),
                  pl.BlockSpec((1, V_DIM), lambda b, h, i, kn: (0, 0))],
        out_specs=pl.BlockSpec((None, t, V_DIM), lambda b, h, i, kn: (b, i, h)),
        scratch_shapes=[pltpu.VMEM((t, t), F32), pltpu.VMEM((2, V_ROWS, t), F32),
                        pltpu.VMEM((2, t, t), F32), pltpu.VMEM((2, t, t), F32),
                        pltpu.VMEM((2, t, t), BF16), pltpu.VMEM((2, t, t), BF16)],
    )
    return pl.pallas_call(
        _attn_kernel,
        grid_spec=grid_spec,
        out_shape=jax.ShapeDtypeStruct((B, S, ATTN_WIDTH), BF16),
        compiler_params=pltpu.CompilerParams(dimension_semantics=("parallel", "arbitrary", "arbitrary"),
                                             vmem_limit_bytes=VMEM_LIMIT),
        name="diff_attn",
    )(kn, qT, k4, vT, lam, g)


def _lru_kernel(xr_ref, yr_ref, cw_ref, cb_ref, wr_ref, wi_ref, br_ref, bi_ref, lam_ref, o_ref, xbuf, hc):
    T = LRU_TILE
    ti = pl.program_id(1)

    @pl.when(ti == 0)
    def _():
        xbuf[0:8] = jnp.zeros((8, LRU_WIDTH), F32)
        hc[...] = jnp.zeros_like(hc)

    x = xr_ref[...]
    xbuf[8:8 + T] = x
    cw = cw_ref[...]
    xc = cb_ref[...] + cw[3:4] * x
    for j in range(CONV_W - 1):
        xc = xc + cw[j:j + 1] * xbuf[5 + j:5 + j + T]
    xbuf[0:8] = x[T - 8:T]

    xb = xc.astype(BF16)
    r = jax.nn.sigmoid(_dot(xb, wr_ref[...]) + br_ref[...])
    ig = jax.nn.sigmoid(_dot(xb, wi_ref[...]) + bi_ref[...])
    z = -lam_ref[...]
    softplus = jnp.maximum(z, 0.0) + jnp.log1p(jnp.exp(-jnp.abs(z)))
    la = -LRU_C * r * softplus
    a = jnp.exp(la)
    m2 = -jnp.tanh(la) * (a * a + 1.0)
    mult = jnp.where(m2 > 0.0, m2 * lax.rsqrt(m2), 0.0)
    row = lax.broadcasted_iota(jnp.int32, (T, LRU_WIDTH), 0)
    mult = jnp.where((row == 0) & (ti == 0), 1.0, mult)
    u = (xc * ig) * mult

    nb = T // SUBLANES
    a3 = a.reshape(nb, SUBLANES, LRU_WIDTH)
    u3 = u.reshape(nb, SUBLANES, LRU_WIDTH)
    sub = lax.broadcasted_iota(jnp.int32, a3.shape, 1)
    d = 1
    while d < SUBLANES:
        valid = sub >= d
        u3 = jnp.where(valid, a3 * pltpu.roll(u3, d, 1) + u3, u3)
        a3 = jnp.where(valid, a3 * pltpu.roll(a3, d, 1), a3)
        d *= 2
    h = hc[...]
    blocks = []
    for b in range(nb):
        hb = u3[b] + a3[b] * h
        blocks.append(hb)
        h = hb[SUBLANES - 1:SUBLANES]
    hfull = jnp.concatenate(blocks, axis=0)
    hc[...] = h
    y = yr_ref[...]
    gelu = 0.5 * y * (1.0 + jnp.tanh(0.7978845608028654 * (y + 0.044715 * (y * y * y))))
    o_ref[...] = (hfull * gelu).astype(BF16)


def _lru(xr, yr, cw, cb, wr, wi, br, bi, lam, B, S):
    T = LRU_TILE
    seq = pl.BlockSpec((None, T, LRU_WIDTH), lambda b, t: (b, t, 0))
    full = lambda r, c: pl.BlockSpec((r, c), lambda b, t: (0, 0))
    return pl.pallas_call(
        _lru_kernel,
        grid=(B, S // T),
        in_specs=[seq, seq, full(CONV_W, LRU_WIDTH), full(1, LRU_WIDTH), full(LRU_WIDTH, LRU_WIDTH),
                  full(LRU_WIDTH, LRU_WIDTH), full(1, LRU_WIDTH), full(1, LRU_WIDTH), full(1, LRU_WIDTH)],
        out_specs=seq,
        out_shape=jax.ShapeDtypeStruct((B, S, LRU_WIDTH), BF16),
        scratch_shapes=[pltpu.VMEM((T + 8, LRU_WIDTH), F32), pltpu.VMEM((1, LRU_WIDTH), F32)],
        compiler_params=pltpu.CompilerParams(dimension_semantics=("arbitrary", "arbitrary"),
                                             vmem_limit_bytes=VMEM_LIMIT),
        name="rg_lru",
    )(xr, yr, cw, cb, wr, wi, br, bi, lam)


def _pack_words(v):
    bits = pltpu.bitcast(v.astype(BF16).astype(F32), jnp.uint32)
    half = D_MODEL // 2
    packed = (bits[:, :half] >> 16) | (bits[:, half:] & jnp.uint32(0xFFFF0000))
    return [packed[:, c * LANES:(c + 1) * LANES] for c in range(PACK_SUB)]


def _packed_chunk(c, rows, first_row=0):
    return (pl.ds(first_row * PACK_SUB + c, rows, stride=PACK_SUB), slice(None))


def _pack_rows(v, out_ref, first_row=0):
    for c, words in enumerate(_pack_words(v)):
        out_ref[_packed_chunk(c, v.shape[0], first_row)] = words


def _unpack_chunks(in_ref, rows, first_row=0):
    lo, hi = [], []
    for c in range(PACK_SUB):
        w = in_ref[_packed_chunk(c, rows, first_row)]
        lo.append(pltpu.bitcast(w << 16, F32))
        hi.append(pltpu.bitcast(w & jnp.uint32(0xFFFF0000), F32))
    return lo + hi


def _unpack_rows(in_ref, rows):
    return jnp.concatenate(_unpack_chunks(in_ref, rows), axis=1).astype(BF16)


def _merge_kernel(x_ref, attn_ref, lru_ref, g1_ref, wg_ref, woa_ref, wol_ref, wout_ref, g2_ref, wrt_ref,
                  x1_ref, h2p_ref, route_ref, counts_ref, cnt, tri):
    tm = MERGE_CHAIN

    @pl.when(pl.program_id(0) == 0)
    def _():
        cnt[...] = jnp.zeros_like(cnt)
        r = lax.broadcasted_iota(jnp.int32, (tm, tm), 0)
        c = lax.broadcasted_iota(jnp.int32, (tm, tm), 1)
        tri[...] = jnp.where(c < r, 1.0, 0.0).astype(BF16)

    counts = cnt[...]
    for r0 in range(0, SEQ_TILE, tm):
        counts = _merge_chain(r0, tm, counts, x_ref, attn_ref, lru_ref, g1_ref, wg_ref, woa_ref, wol_ref, wout_ref,
                              g2_ref, wrt_ref, x1_ref, h2p_ref, route_ref, tri)
    cnt[...] = counts
    counts_ref[...] = counts


def _merge_chain(r0, tm, counts, x_ref, attn_ref, lru_ref, g1_ref, wg_ref, woa_ref, wol_ref, wout_ref, g2_ref, wrt_ref,
                 x1_ref, h2p_ref, route_ref, tri):
    rs = slice(r0, r0 + tm)
    x = x_ref[rs, :]
    hb = _rms(x, g1_ref[...]).astype(BF16)
    gates = 0.5 * jnp.tanh(0.5 * _dot(hb, wg_ref[...])) + 0.5
    merged = gates[:, :D_MODEL] * _dot(attn_ref[rs, :], woa_ref[...]) + gates[:, D_MODEL:] * _dot(lru_ref[rs, :], wol_ref[...])
    x1 = x + _dot(merged.astype(BF16), wout_ref[...])
    x1_ref[rs, :] = x1
    h2 = _rms(x1, g2_ref[...])
    _pack_rows(h2, h2p_ref, first_row=r0)

    h_hi = h2.astype(BF16)
    h_lo = (h2 - h_hi.astype(F32)).astype(BF16)
    wrt = wrt_ref[...]
    hh = _dot(h_hi, wrt)
    logits = hh[:, :ROUTE_LANES] + hh[:, ROUTE_LANES:] + _dot(h_lo, wrt[:, :ROUTE_LANES])
    lane = lax.broadcasted_iota(jnp.int32, logits.shape, 1)
    big = jnp.int32(1 << 20)

    def first_argmax(v):
        m = jnp.max(v, axis=-1, keepdims=True)
        return m, jnp.min(jnp.where(v == m, lane, big), axis=-1, keepdims=True)

    gmask = lane < N_GROUPS
    gmax, gidx = first_argmax(jnp.where(gmask, logits, -jnp.inf))
    gsum = jnp.sum(jnp.where(gmask, jnp.exp(logits - gmax), 0.0), axis=-1, keepdims=True)
    g_w = 1.0 / gsum
    lo = N_GROUPS + EXPERTS_PER_GROUP * gidx
    el = jnp.where((lane >= lo) & (lane < lo + EXPERTS_PER_GROUP), logits, -jnp.inf)
    m1, i1 = first_argmax(el)
    m2, i2 = first_argmax(jnp.where(lane == i1, -jnp.inf, el))
    rr = jnp.exp(m2 - m1)
    w1 = g_w / (1.0 + rr)
    w2 = g_w * rr / (1.0 + rr)
    oh1 = lane == i1
    oh2 = lane == i2
    oh = jnp.where(oh1 | oh2, 1.0, 0.0)
    before = _dot(tri[...], oh.astype(BF16)) + counts
    r1 = jnp.sum(jnp.where(oh1, before, 0.0), axis=-1, keepdims=True)
    r2 = jnp.sum(jnp.where(oh2, before, 0.0), axis=-1, keepdims=True)
    vals = (i1.astype(F32), i2.astype(F32), w1, w2, r1, r2)
    route = jnp.zeros_like(logits)
    for k, v in enumerate(vals):
        route = jnp.where(lane == k, v, route)
    route_ref[rs, :] = route
    return counts + jnp.sum(oh, axis=0, keepdims=True)


def _merge(x2, attn, lru, g1, wg, woa, wol, wout, g2, wrt):
    N = x2.shape[0]
    tm = SEQ_TILE
    rows = lambda c: pl.BlockSpec((tm, c), lambda i: (i, 0))
    full = lambda r, c: pl.BlockSpec((r, c), lambda i: (0, 0))
    return pl.pallas_call(
        _merge_kernel,
        grid=(N // tm,),
        in_specs=[rows(D_MODEL), rows(ATTN_WIDTH), rows(LRU_WIDTH), full(1, D_MODEL), full(D_MODEL, 2 * D_MODEL),
                  full(ATTN_WIDTH, D_MODEL), full(LRU_WIDTH, D_MODEL), full(D_MODEL, D_MODEL), full(1, D_MODEL),
                  full(D_MODEL, 2 * ROUTE_LANES)],
        out_specs=[rows(D_MODEL), pl.BlockSpec((tm * PACK_SUB, LANES), lambda i: (i, 0)), rows(ROUTE_LANES),
                   full(1, ROUTE_LANES)],
        out_shape=[jax.ShapeDtypeStruct((N, D_MODEL), F32), jax.ShapeDtypeStruct((N * PACK_SUB, LANES), jnp.uint32),
                   jax.ShapeDtypeStruct((N, ROUTE_LANES), F32), jax.ShapeDtypeStruct((1, ROUTE_LANES), F32)],
        scratch_shapes=[pltpu.VMEM((1, ROUTE_LANES), F32), pltpu.VMEM((MERGE_CHAIN, MERGE_CHAIN), BF16)],
        compiler_params=pltpu.CompilerParams(dimension_semantics=("arbitrary",), vmem_limit_bytes=VMEM_LIMIT),
        name="merge_route",
    )(x2, attn, lru, g1, wg, woa, wol, wout, g2, wrt)


def _for_each_assignment(off_ref, fn):
    def group(gi, _):
        toks = [gi * DMA_GROUP + j for j in range(DMA_GROUP)]
        offs = [[off_ref[0, 0, tk * TOP_K + k] for k in range(TOP_K)] for tk in toks]
        for tk, o in zip(toks, offs):
            for k in range(TOP_K):
                fn(tk, k, pl.multiple_of(o[k], PACK_SUB))
        return 0
    lax.fori_loop(0, ROW_TILE // DMA_GROUP, group, 0)


def _dispatch_kernel(off_ref, h2p_hbm, xs_hbm, xin, in_sem, out_sem):
    tile_rows = ROW_TILE * PACK_SUB
    i = pl.program_id(0)
    n = pl.num_programs(0)
    slot = lax.rem(i, DISPATCH_SLOTS)

    def fetch(tile, s):
        return pltpu.make_async_copy(h2p_hbm.at[pl.ds(pl.multiple_of(tile * tile_rows, tile_rows), tile_rows)],
                                     xin.at[s], in_sem.at[s])

    def drain(s):
        for _ in range(TOP_K):
            pltpu.make_async_copy(xin.at[s], xs_hbm.at[pl.ds(0, tile_rows)], out_sem.at[s]).wait()

    @pl.when(i == 0)
    def _():
        fetch(0, 0).start()

    @pl.when(i + 1 < n)
    def _():
        fetch(i + 1, lax.rem(i + 1, DISPATCH_SLOTS)).start()

    fetch(i, slot).wait()

    def start(tk, k, off):
        pltpu.make_async_copy(xin.at[slot, pl.ds(pl.multiple_of(tk * PACK_SUB, PACK_SUB), PACK_SUB)],
                              xs_hbm.at[pl.ds(off, PACK_SUB)], out_sem.at[slot]).start(priority=k)

    _for_each_assignment(off_ref, start)

    @pl.when(i > 0)
    def _():
        drain(lax.rem(i + DISPATCH_SLOTS - 1, DISPATCH_SLOTS))

    @pl.when(i == n - 1)
    def _():
        drain(slot)


def _dispatch(off3, h2p):
    return pl.pallas_call(
        _dispatch_kernel,
        grid=(off3.shape[0],),
        in_specs=[pl.BlockSpec((1, 1, TOP_K * ROW_TILE), lambda i: (i, 0, 0), memory_space=pltpu.SMEM),
                  pl.BlockSpec(memory_space=pl.ANY)],
        out_specs=pl.BlockSpec(memory_space=pl.ANY),
        out_shape=jax.ShapeDtypeStruct((TOP_K * h2p.shape[0], LANES), jnp.uint32),
        scratch_shapes=[pltpu.VMEM((DISPATCH_SLOTS, ROW_TILE * PACK_SUB, LANES), jnp.uint32),
                        pltpu.SemaphoreType.DMA((DISPATCH_SLOTS,)), pltpu.SemaphoreType.DMA((DISPATCH_SLOTS,))],
        compiler_params=pltpu.CompilerParams(dimension_semantics=("arbitrary",), has_side_effects=True),
        name="moe_dispatch",
    )(off3, h2p)


def _moe_kernel(vt_ref, ve_ref, vlo_ref, vhi_ref, xs_ref, wg_ref, wu_ref, wd_ref, ys_ref, wgb, wub, wdb):
    tm = MOE_TILE
    v = pl.program_id(0)
    nv = pl.num_programs(0)
    t = vt_ref[v]
    e = ve_ref[v]
    prev = jnp.maximum(v - 1, 0)

    @pl.when((v == 0) | (ve_ref[prev] != e))
    def _():
        wgb[...] = wg_ref[...].astype(BF16)
        wub[...] = wu_ref[...].astype(BF16)
        wdb[...] = wd_ref[...].astype(BF16)

    xb = _unpack_rows(xs_ref, tm)
    g = _dot(xb, wgb[...])
    u = _dot(xb, wub[...])
    hmid = (g * jax.nn.sigmoid(g)) * u
    words = _pack_words(_dot(hmid.astype(BF16), wdb[...]))
    first = (v == 0) | (vt_ref[prev] != t)

    @pl.when(first)
    def _():
        for c in range(PACK_SUB):
            ys_ref[_packed_chunk(c, tm)] = words[c]

    @pl.when(jnp.logical_not(first))
    def _():
        rows = t * tm + lax.broadcasted_iota(jnp.int32, (tm, LANES), 0)
        mine = (rows >= vlo_ref[v]) & (rows < vhi_ref[v])
        for c in range(PACK_SUB):
            ys_ref[_packed_chunk(c, tm)] = jnp.where(mine, words[c], ys_ref[_packed_chunk(c, tm)])


def _moe(vt, ve, vlo, vhi, xs, wg, wu, wd):
    tm = MOE_TILE
    n_rows = xs.shape[0] // PACK_SUB
    grid_spec = pltpu.PrefetchScalarGridSpec(
        num_scalar_prefetch=4,
        grid=(vt.shape[0],),
        in_specs=[pl.BlockSpec((tm * PACK_SUB, LANES), lambda v, vt, ve, lo, hi: (vt[v], 0)),
                  pl.BlockSpec((None, D_MODEL, D_EXPERT), lambda v, vt, ve, lo, hi: (ve[v], 0, 0)),
                  pl.BlockSpec((None, D_MODEL, D_EXPERT), lambda v, vt, ve, lo, hi: (ve[v], 0, 0)),
                  pl.BlockSpec((None, D_EXPERT, D_MODEL), lambda v, vt, ve, lo, hi: (ve[v], 0, 0))],
        out_specs=pl.BlockSpec((tm * PACK_SUB, LANES), lambda v, vt, ve, lo, hi: (vt[v], 0)),
        scratch_shapes=[pltpu.VMEM((D_MODEL, D_EXPERT), BF16), pltpu.VMEM((D_MODEL, D_EXPERT), BF16),
                        pltpu.VMEM((D_EXPERT, D_MODEL), BF16)],
    )
    return pl.pallas_call(
        _moe_kernel,
        grid_spec=grid_spec,
        out_shape=jax.ShapeDtypeStruct((n_rows * PACK_SUB, LANES), jnp.uint32),
        compiler_params=pltpu.CompilerParams(dimension_semantics=("arbitrary",), vmem_limit_bytes=VMEM_LIMIT),
        name="moe_experts",
    )(vt, ve, vlo, vhi, xs, wg, wu, wd)


def _combine_kernel(off_ref, offn_ref, x1_ref, route_ref, g_ref, ys_hbm, o_ref, ybuf, sem):
    tc = ROW_TILE
    i = pl.program_id(0)
    n = pl.num_programs(0)
    slot = lax.rem(i, 2)

    def gather(o_ref_, s):
        def start(tk, k, off):
            pltpu.make_async_copy(ys_hbm.at[pl.ds(off, PACK_SUB)],
                                  ybuf.at[s, pl.ds(pl.multiple_of((k * tc + tk) * PACK_SUB, PACK_SUB), PACK_SUB)],
                                  sem.at[s]).start(priority=k)
        _for_each_assignment(o_ref_, start)

    def drain(s):
        pltpu.make_async_copy(ys_hbm.at[pl.ds(0, TOP_K * tc * PACK_SUB)], ybuf.at[s], sem.at[s]).wait()

    @pl.when(i == 0)
    def _():
        gather(off_ref, 0)

    @pl.when(i + 1 < n)
    def _():
        gather(offn_ref, 1 - slot)

    drain(slot)

    route = route_ref[...]
    n_chunks = D_MODEL // LANES
    z = [x1_ref[:, c * LANES:(c + 1) * LANES] for c in range(n_chunks)]
    for k in range(TOP_K):
        wk = route[:, TOP_K + k:TOP_K + k + 1]
        yk = _unpack_chunks(ybuf.at[slot], tc, first_row=k * tc)
        z = [zc + wk * yc for zc, yc in zip(z, yk)]
    ss = sum(jnp.sum(zc * zc, axis=-1, keepdims=True) for zc in z)
    inv = lax.rsqrt(ss * (1.0 / D_MODEL) + NORM_EPS)
    for c in range(n_chunks):
        o_ref[:, c * LANES:(c + 1) * LANES] = z[c] * inv * g_ref[:, c * LANES:(c + 1) * LANES]


def _combine(off3, x1, route, g, ys):
    N = x1.shape[0]
    tc = ROW_TILE
    nt = N // tc
    idx = lambda f: pl.BlockSpec((1, 1, TOP_K * tc), f, memory_space=pltpu.SMEM)
    return pl.pallas_call(
        _combine_kernel,
        grid=(nt,),
        in_specs=[idx(lambda i: (i, 0, 0)),
                  idx(lambda i: (jnp.minimum(i + 1, nt - 1), 0, 0)),
                  pl.BlockSpec((tc, D_MODEL), lambda i: (i, 0)),
                  pl.BlockSpec((tc, ROUTE_LANES), lambda i: (i, 0)),
                  pl.BlockSpec((1, D_MODEL), lambda i: (0, 0)),
                  pl.BlockSpec(memory_space=pl.ANY)],
        out_specs=pl.BlockSpec((tc, D_MODEL), lambda i: (i, 0)),
        out_shape=jax.ShapeDtypeStruct((N, D_MODEL), F32),
        scratch_shapes=[pltpu.VMEM((2, TOP_K * tc * PACK_SUB, LANES), jnp.uint32), pltpu.SemaphoreType.DMA((2,))],
        compiler_params=pltpu.CompilerParams(dimension_semantics=("arbitrary",), vmem_limit_bytes=VMEM_LIMIT),
        name="combine_norm",
    )(off3, off3, x1, route, g, ys)


def _block_diag(w):
    nb, c, _ = w.shape
    eye = jnp.eye(nb, dtype=w.dtype)
    return (eye[:, None, :, None] * w[:, :, None, :]).reshape(nb * c, nb * c)


def _visit_plan(counts, n_rows):
    tm = MOE_TILE
    n_tiles = n_rows // tm
    n_visits = n_tiles + N_EXPERTS - 1
    cnt = counts.astype(jnp.int32)
    lanes = jnp.arange(ROUTE_LANES, dtype=jnp.int32)
    upto = (lanes[None, :] <= lanes[:, None]).astype(jnp.int32)
    ends = upto @ cnt
    starts = ends - cnt
    first_tile = starts // tm
    n_vis = jnp.where(cnt > 0, (ends - 1) // tm - first_tile + 1, 0)
    v_end = upto @ n_vis
    v_start = v_end - n_vis
    total = v_end[-1]
    v = jnp.arange(n_visits, dtype=jnp.int32)
    vc = jnp.minimum(v, total - 1)
    own = ((vc[:, None] >= v_start[None, :]) & (vc[:, None] < v_end[None, :])).astype(jnp.int32)
    pick = lambda per_lane: own @ per_lane
    tile = jnp.maximum(pick(first_tile - v_start) + vc, 0)
    lo = jnp.maximum(pick(starts), tile * tm)
    hi = jnp.minimum(pick(ends), (tile + 1) * tm)
    valid = v < total
    ve = jnp.maximum(pick(lanes) - EXPERT_LANE0, 0)
    return starts, tile, ve, jnp.where(valid, lo, 0), jnp.where(valid, hi, 0)


def kernel(x, norm_mix_g, w_in, lambda_qk, subln_g, conv_w, conv_b, w_r, b_r, w_i, b_i, lru_lambda, w_o_attn, w_o_lru, w_out, norm_ffn_g, w_group, w_expert_router, w_gate, w_up, w_down, final_norm_g):
    B, S, D = x.shape
    N = B * S
    nt = S // SEQ_TILE
    depth = norm_mix_g.shape[0]
    assert depth == 1 and D == D_MODEL and S % SEQ_TILE == 0 and S % LRU_TILE == 0
    assert N % ROW_TILE == 0 and (N * TOP_K) % MOE_TILE == 0
    l = 0
    row = lambda v: v.reshape(1, -1).astype(F32)

    x2 = x.reshape(N, D)
    w_in_l = w_in[l]
    qT, k, vT, xr, yr, kn = _inproj(x2, row(norm_mix_g[l]), w_in_l[:, :PROJ_COLS].astype(BF16), B, S)

    attn = _attn(kn[:, :, 0].reshape(-1), qT, k.reshape(B, nt, SEQ_TILE, N_HEADS * K_COLS), vT,
                 lambda_qk[l].reshape(4, HEAD_DIM).astype(F32), row(subln_g[l]), B, S)

    lru = _lru(xr.reshape(B, S, LRU_WIDTH), yr.reshape(B, S, LRU_WIDTH), conv_w[l].astype(F32), row(conv_b[l]),
               _block_diag(w_r[l]).astype(BF16), _block_diag(w_i[l]).astype(BF16), row(b_r[l]), row(b_i[l]),
               row(lru_lambda[l]), B, S)

    w_route = jnp.concatenate(
        [w_group[l], jnp.transpose(w_expert_router[l], (1, 0, 2)).reshape(D, N_EXPERTS),
         jnp.zeros((D, ROUTE_LANES - N_GROUPS - N_EXPERTS), F32)], axis=1).astype(F32)
    w_route_hi = w_route.astype(BF16)
    w_route = jnp.concatenate([w_route_hi, (w_route - w_route_hi.astype(F32)).astype(BF16)], axis=1)
    x1, h2p, route, counts = _merge(x2, attn.reshape(N, ATTN_WIDTH), lru.reshape(N, LRU_WIDTH), row(norm_mix_g[l]),
                                    w_in_l[:, PROJ_COLS:].astype(BF16), w_o_attn[l].astype(BF16),
                                    w_o_lru[l].astype(BF16), w_out[l].astype(BF16), row(norm_ffn_g[l]), w_route)

    starts, vt, ve, vlo, vhi = _visit_plan(counts[0], N * TOP_K)
    lane = route[:, 0:TOP_K].astype(jnp.int32)
    rank = route[:, 2 * TOP_K:3 * TOP_K].astype(jnp.int32)
    lane_ids = jnp.arange(ROUTE_LANES, dtype=jnp.int32)
    start_of = jnp.sum(jnp.where(lane[..., None] == lane_ids, starts, 0), axis=-1)
    off3 = ((start_of + rank) * PACK_SUB).reshape(N // ROW_TILE, 1, TOP_K * ROW_TILE)

    xs = _dispatch(off3, h2p)
    ys = _moe(vt, ve, vlo, vhi, xs, w_gate[l], w_up[l], w_down[l])
    out = _combine(off3, x1, route, row(final_norm_g), ys)
    return out.reshape(B, S, D)
```

```python
import functools
import math

import jax
import jax.numpy as jnp
from jax import lax
from jax.experimental import pallas as pl
from jax.experimental.pallas import tpu as pltpu

F32 = jnp.float32
BF16 = jnp.bfloat16

D_MODEL = 1024
N_HEADS = 4
HEAD_DIM = 64
V_DIM = 2 * HEAD_DIM
ATTN_WIDTH = N_HEADS * V_DIM
V_ROWS = V_DIM + 16
K_COLS = 2 * V_DIM
POS_SPLIT = 3
POS_RADIX = 256
SKIP_MARGIN = 171.0
NORM_SLACK = 1.0201
LRU_WIDTH = D_MODEL // 2
LRU_BLOCKS = 8
CONV_W = 4
LRU_C = 8.0
N_GROUPS = 4
EXPERTS_PER_GROUP = 8
N_EXPERTS = N_GROUPS * EXPERTS_PER_GROUP
TOP_K = 2
D_EXPERT = D_MODEL // 2
NORM_EPS = 1e-6
LAM_INIT = 0.8 - 0.6 * math.exp(-0.3 * 0)

QK_COLS = N_HEADS * 2 * HEAD_DIM
PROJ_COLS = 2 * QK_COLS + ATTN_WIDTH + 2 * LRU_WIDTH
ROUTE_LANES = 128
NEG_BIG = -1e30
LOG2E = math.log2(math.e)
ALIBI_SLOPES = tuple(2.0 ** (-8.0 * (h + 1) / N_HEADS) for h in range(N_HEADS))

SEQ_TILE = 512
LRU_TILE = 512
ROW_TILE = 256
MERGE_CHAIN = 512
MOE_TILE = 256
DMA_GROUP = 8
DISPATCH_SLOTS = 3
LANES = 128
SUBLANES = 8
PACK_SUB = D_MODEL // 2 // LANES
EXPERT_LANE0 = N_GROUPS
VMEM_LIMIT = 48 * 1024 * 1024


def _rms(x, g):
    return x * lax.rsqrt(jnp.mean(x * x, axis=-1, keepdims=True) + NORM_EPS) * g


def _dot(a, b):
    return jnp.dot(a, b, preferred_element_type=F32)


def _inproj_kernel(x_ref, g_ref, w_ref, qT_ref, k_ref, vT_ref, xr_ref, yr_ref, kn_ref):
    hb = _rms(x_ref[...], g_ref[...]).astype(BF16)

    def proj(lo, hi):
        return _dot(hb, w_ref[:, lo:hi])

    q = proj(0, QK_COLS) * (HEAD_DIM ** -0.5 * LOG2E)
    for h in range(N_HEADS):
        qT_ref[h] = q[:, h * V_DIM:(h + 1) * V_DIM].T.astype(BF16)
    k = proj(QK_COLS, 2 * QK_COLS).astype(BF16)
    r = lax.broadcasted_iota(jnp.int32, (k.shape[0], K_COLS - V_DIM), 0)
    lane = lax.broadcasted_iota(jnp.int32, r.shape, 1)
    a = r // POS_RADIX * POS_RADIX
    feat = jnp.where(lane < POS_SPLIT, a, jnp.where(lane < 2 * POS_SPLIT, r - a, 0)).astype(F32).astype(BF16)
    for h in range(N_HEADS):
        k_ref[:, h * K_COLS:h * K_COLS + V_DIM] = k[:, h * V_DIM:(h + 1) * V_DIM]
        k_ref[:, h * K_COLS + V_DIM:(h + 1) * K_COLS] = feat
    seg = lax.broadcasted_iota(jnp.int32, (QK_COLS, LANES), 0) // HEAD_DIM
    pick = jnp.where(seg == lax.broadcasted_iota(jnp.int32, (QK_COLS, LANES), 1), 1.0, 0.0).astype(BF16)
    row_n2 = _dot(jnp.square(k.astype(F32)).astype(BF16), pick)
    kn_ref[...] = jnp.max(row_n2, axis=0, keepdims=True)
    v = proj(2 * QK_COLS, 2 * QK_COLS + ATTN_WIDTH)
    for h in range(N_HEADS):
        vT_ref[h, :V_DIM, :] = v[:, h * V_DIM:(h + 1) * V_DIM].T.astype(BF16)
        pad_row = lax.broadcasted_iota(jnp.int32, (V_ROWS - V_DIM, v.shape[0]), 0)
        vT_ref[h, V_DIM:, :] = jnp.where(pad_row == 0, 1.0, 0.0).astype(BF16)
    c0 = 2 * QK_COLS + ATTN_WIDTH
    xr_ref[...] = proj(c0, c0 + LRU_WIDTH)
    yr_ref[...] = proj(c0 + LRU_WIDTH, c0 + 2 * LRU_WIDTH)


def _inproj(x2, g, w, B, S):
    N = B * S
    tm = SEQ_TILE
    nt = S // tm
    tile5 = pl.BlockSpec((None, N_HEADS, None, V_DIM, tm), lambda i: (i // nt, 0, i % nt, 0, 0))
    rows = lambda c: pl.BlockSpec((tm, c), lambda i: (i, 0))
    return pl.pallas_call(
        _inproj_kernel,
        grid=(N // tm,),
        in_specs=[rows(D_MODEL),
                  pl.BlockSpec((1, D_MODEL), lambda i: (0, 0)),
                  pl.BlockSpec((D_MODEL, PROJ_COLS), lambda i: (0, 0))],
        out_specs=[tile5, rows(N_HEADS * K_COLS),
                   pl.BlockSpec((None, N_HEADS, None, V_ROWS, tm), lambda i: (i // nt, 0, i % nt, 0, 0)),
                   rows(LRU_WIDTH), rows(LRU_WIDTH),
                   pl.BlockSpec((None, 1, LANES), lambda i: (i, 0, 0))],
        out_shape=[jax.ShapeDtypeStruct((B, N_HEADS, nt, V_DIM, tm), BF16),
                   jax.ShapeDtypeStruct((N, N_HEADS * K_COLS), BF16),
                   jax.ShapeDtypeStruct((B, N_HEADS, nt, V_ROWS, tm), BF16),
                   jax.ShapeDtypeStruct((N, LRU_WIDTH), F32),
                   jax.ShapeDtypeStruct((N, LRU_WIDTH), F32),
                   jax.ShapeDtypeStruct((N // tm, 1, LANES), F32)],
        compiler_params=pltpu.CompilerParams(dimension_semantics=("parallel",),
                                             vmem_limit_bytes=VMEM_LIMIT),
        name="inproj",
    )(x2, g, w)


def _attn_kernel(kn_ref, qT_ref, k_ref, vT_ref, lam_ref, g_ref, o_ref, mask_ref, acc_ref, sa_ref, sb_ref, pa_ref,
                 pb_ref):
    t = SEQ_TILE
    b = pl.program_id(0)
    h = pl.program_id(1)
    i = pl.program_id(2)
    slope = LOG2E * jnp.where(h == 0, ALIBI_SLOPES[0], jnp.where(h == 1, ALIBI_SLOPES[1],
                              jnp.where(h == 2, ALIBI_SLOPES[2], ALIBI_SLOPES[3]))).astype(F32)

    @pl.when(i == 0)
    def _():
        r = lax.broadcasted_iota(jnp.int32, (t, t), 0)
        c = lax.broadcasted_iota(jnp.int32, (t, t), 1)
        mask_ref[...] = jnp.where(r <= c, 0.0, NEG_BIG)

    qf = qT_ref[...].astype(F32)
    row = lax.broadcasted_iota(jnp.int32, qf.shape, 0)
    sl = jnp.full(qf.shape, slope, F32)
    hi = sl.astype(BF16).astype(F32)
    mid = (sl - hi).astype(BF16).astype(F32)
    lo = (sl - hi - mid).astype(BF16).astype(F32)
    piece = jnp.where(row % 3 == 0, hi, jnp.where(row % 3 == 1, mid, lo))
    srows = jnp.where(row < 2 * POS_SPLIT, piece, 0.0).astype(BF16)
    qs = tuple(jnp.concatenate([jnp.where(sel, qf, 0.0).astype(BF16), srows], axis=0)
               for sel in (row < HEAD_DIM, row >= HEAD_DIM))
    acc_ref[...] = jnp.zeros_like(acc_ref)
    pb_ref[...] = jnp.zeros_like(pb_ref)

    first_off = [0]

    def key_tile(tau):
        return jnp.where(tau <= 0, i, first_off[0] + tau - 1)

    def stage_q(tau, s_ref, diagonal=False):
        kt = k_ref[key_tile(tau)]
        tile_max = []
        for mi in range(2):
            s = _dot(kt, qs[mi])
            if diagonal:
                s = s + mask_ref[...]
            s_ref[mi] = s
            tile_max.append(jnp.max(s, axis=0, keepdims=True))
        return tuple(tile_max)

    def stage_s(tau, s_ref, p_ref, ms, tile_max):
        cj = slope * (key_tile(tau) * t).astype(F32)
        m_out, alphas = [], []
        for mi in range(2):
            m_new = jnp.maximum(ms[mi], tile_max[mi] + cj)
            alphas.append(jnp.exp2(ms[mi] - m_new))
            p_ref[mi] = jnp.exp2(s_ref[mi] - (m_new - cj)).astype(BF16)
            m_out.append(m_new)
        return tuple(m_out), tuple(alphas)

    def stage_v(tau, p_ref, alphas):
        vt = vT_ref[key_tile(tau)]
        for mi in range(2):
            acc_ref[mi] = alphas[mi] * acc_ref[mi] + _dot(vt, p_ref[mi])

    def body(jj, carry):
        ms, alphas, tmax = carry[:2], carry[2:4], carry[4:]
        tau = 2 * jj
        tmax_b = stage_q(tau + 1, sb_ref)
        ms, alphas_a = stage_s(tau, sa_ref, pa_ref, ms, tmax)
        stage_v(tau - 1, pb_ref, alphas)
        tmax_a = stage_q(tau + 2, sa_ref)
        ms, alphas_b = stage_s(tau + 1, sb_ref, pb_ref, ms, tmax_b)
        stage_v(tau, pa_ref, alphas_a)
        return ms + alphas_b + tmax_a

    def finalize():
        lp = lam_ref[...]
        s1 = jnp.sum(lp[0:1] * lp[1:2], axis=-1, keepdims=True)
        s2 = jnp.sum(lp[2:3] * lp[3:4], axis=-1, keepdims=True)
        lam = jnp.exp(s1) - jnp.exp(s2) + LAM_INIT
        norm = [acc_ref[mi, :V_DIM, :] * (1.0 / acc_ref[mi, V_DIM:V_DIM + 1, :]) for mi in range(2)]
        oT = norm[0] - lam * norm[1]
        o = _rms(oT.T, g_ref[...]) * (1.0 - LAM_INIT)
        o_ref[...] = o.astype(BF16)

    m_init = jnp.full((1, t), NEG_BIG, F32)
    one = jnp.ones((1, t), F32)
    tmax0 = stage_q(0, sa_ref, diagonal=True)

    cj0 = slope * (i * t).astype(F32)
    qn2, m_low = [], []
    for mi in range(2):
        qsq = jnp.square(qf[mi * HEAD_DIM:(mi + 1) * HEAD_DIM, :])
        qn2.append(jnp.max(jnp.sum(qsq, axis=0, keepdims=True), axis=1, keepdims=True)[0, 0])
        m_low.append(jnp.min(tmax0[mi], axis=1, keepdims=True)[0, 0] + cj0)

    def skippable(j):
        ok = None
        for mi in range(2):
            kn2 = kn_ref[((b * pl.num_programs(2) + j) * N_HEADS + h) * 2 + mi]
            room = m_low[mi] - SKIP_MARGIN - slope * (t - 1) - slope * (j * t).astype(F32)
            fits = (room > 0.0) & (qn2[mi] * kn2 * NORM_SLACK < room * room)
            ok = fits if ok is None else ok & fits
        return ok

    first_off[0] = lax.fori_loop(0, i, lambda j, j0: jnp.where((j0 == j) & skippable(j), j + 1, j0), 0)
    n_off = i - first_off[0]

    n_main = jnp.right_shift(n_off, 1)
    fin = lax.fori_loop(0, n_main, body, (m_init, m_init, one, one) + tmax0)
    ms, alphas, tmax = fin[:2], fin[2:4], fin[4:]
    tau = 2 * n_main
    odd_tiles = tau == n_off

    @pl.when(odd_tiles)
    def _():
        _, alphas_a = stage_s(tau, sa_ref, pa_ref, ms, tmax)
        stage_v(tau - 1, pb_ref, alphas)
        stage_v(tau, pa_ref, alphas_a)
        finalize()

    @pl.when(jnp.logical_not(odd_tiles))
    def _():
        tmax_b = stage_q(tau + 1, sb_ref)
        ms_a, alphas_a = stage_s(tau, sa_ref, pa_ref, ms, tmax)
        stage_v(tau - 1, pb_ref, alphas)
        _, alphas_b = stage_s(tau + 1, sb_ref, pb_ref, ms_a, tmax_b)
        stage_v(tau, pa_ref, alphas_a)
        stage_v(tau + 1, pb_ref, alphas_b)
        finalize()


def _attn(kn, qT, k4, vT, lam, g, B, S):
    t = SEQ_TILE
    nt = S // t
    grid_spec = pltpu.PrefetchScalarGridSpec(
        num_scalar_prefetch=1,
        grid=(B, N_HEADS, nt),
        in_specs=[pl.BlockSpec((None, None, None, V_DIM, t), lambda b, h, i, kn: (b, h, i, 0, 0)),
                  pl.BlockSpec((None, nt, t, K_COLS), lambda b, h, i, kn: (b, 0, 0, h)),
                  pl.BlockSpec((None, None, nt, V_ROWS, t), lambda b, h, i, kn: (b, h, 0, 0, 0)),
                  pl.BlockSpec((4, HEAD_DIM), lambda b, h, i, kn: (0, 0)),
                  pl.BlockSpec((1, V_DIM), lambda b, h, i, kn: (0, 0))],
        out_specs=pl.BlockSpec((None, t, V_DIM), lambda b, h, i, kn: (b, i, h)),
        scratch_shapes=[pltpu.VMEM((t, t), F32), pltpu.VMEM((2, V_ROWS, t), F32),
                        pltpu.VMEM((2, t, t), F32), pltpu.VMEM((2, t, t), F32),
                        pltpu.VMEM((2, t, t), BF16), pltpu.VMEM((2, t, t), BF16)],
    )
    return pl.pallas_call(
        _attn_kernel,
        grid_spec=grid_spec,
        out_shape=jax.ShapeDtypeStruct((B, S, ATTN_WIDTH), BF16),
        compiler_params=pltpu.CompilerParams(dimension_semantics=("parallel", "arbitrary", "arbitrary"),
                                             vmem_limit_bytes=VMEM_LIMIT),
        name="diff_attn",
    )(kn, qT, k4, vT, lam, g)


def _lru_kernel(xr_ref, yr_ref, cw_ref, cb_ref, wr_ref, wi_ref, br_ref, bi_ref, lam_ref, o_ref, xbuf, hc):
    T = LRU_TILE
    ti = pl.program_id(1)

    @pl.when(ti == 0)
    def _():
        xbuf[0:8] = jnp.zeros((8, LRU_WIDTH), F32)
        hc[...] = jnp.zeros_like(hc)

    x = xr_ref[...]
    xbuf[8:8 + T] = x
    cw = cw_ref[...]
    xc = cb_ref[...] + cw[3:4] * x
    for j in range(CONV_W - 1):
        xc = xc + cw[j:j + 1] * xbuf[5 + j:5 + j + T]
    xbuf[0:8] = x[T - 8:T]

    xb = xc.astype(BF16)
    r = jax.nn.sigmoid(_dot(xb, wr_ref[...]) + br_ref[...])
    ig = jax.nn.sigmoid(_dot(xb, wi_ref[...]) + bi_ref[...])
    z = -lam_ref[...]
    softplus = jnp.maximum(z, 0.0) + jnp.log1p(jnp.exp(-jnp.abs(z)))
    la = -LRU_C * r * softplus
    a = jnp.exp(la)
    m2 = -jnp.tanh(la) * (a * a + 1.0)
    mult = jnp.where(m2 > 0.0, m2 * lax.rsqrt(m2), 0.0)
    row = lax.broadcasted_iota(jnp.int32, (T, LRU_WIDTH), 0)
    mult = jnp.where((row == 0) & (ti == 0), 1.0, mult)
    u = (xc * ig) * mult

    nb = T // SUBLANES
    a3 = a.reshape(nb, SUBLANES, LRU_WIDTH)
    u3 = u.reshape(nb, SUBLANES, LRU_WIDTH)
    sub = lax.broadcasted_iota(jnp.int32, a3.shape, 1)
    d = 1
    while d < SUBLANES:
        valid = sub >= d
        u3 = jnp.where(valid, a3 * pltpu.roll(u3, d, 1) + u3, u3)
        a3 = jnp.where(valid, a3 * pltpu.roll(a3, d, 1), a3)
        d *= 2
    h = hc[...]
    blocks = []
    for b in range(nb):
        hb = u3[b] + a3[b] * h
        blocks.append(hb)
        h = hb[SUBLANES - 1:SUBLANES]
    hfull = jnp.concatenate(blocks, axis=0)
    hc[...] = h
    y = yr_ref[...]
    gelu = 0.5 * y * (1.0 + jnp.tanh(0.7978845608028654 * (y + 0.044715 * (y * y * y))))
    o_ref[...] = (hfull * gelu).astype(BF16)


def _lru(xr, yr, cw, cb, wr, wi, br, bi, lam, B, S):
    T = LRU_TILE
    seq = pl.BlockSpec((None, T, LRU_WIDTH), lambda b, t: (b, t, 0))
    full = lambda r, c: pl.BlockSpec((r, c), lambda b, t: (0, 0))
    return pl.pallas_call(
        _lru_kernel,
        grid=(B, S // T),
        in_specs=[seq, seq, full(CONV_W, LRU_WIDTH), full(1, LRU_WIDTH), full(LRU_WIDTH, LRU_WIDTH),
                  full(LRU_WIDTH, LRU_WIDTH), full(1, LRU_WIDTH), full(1, LRU_WIDTH), full(1, LRU_WIDTH)],
        out_specs=seq,
        out_shape=jax.ShapeDtypeStruct((B, S, LRU_WIDTH), BF16),
        scratch_shapes=[pltpu.VMEM((T + 8, LRU_WIDTH), F32), pltpu.VMEM((1, LRU_WIDTH), F32)],
        compiler_params=pltpu.CompilerParams(dimension_semantics=("arbitrary", "arbitrary"),
                                             vmem_limit_bytes=VMEM_LIMIT),
        name="rg_lru",
    )(xr, yr, cw, cb, wr, wi, br, bi, lam)


def _pack_words(v):
    bits = pltpu.bitcast(v.astype(BF16).astype(F32), jnp.uint32)
    half = D_MODEL // 2
    packed = (bits[:, :half] >> 16) | (bits[:, half:] & jnp.uint32(0xFFFF0000))
    return [packed[:, c * LANES:(c + 1) * LANES] for c in range(PACK_SUB)]


def _packed_chunk(c, rows, first_row=0):
    return (pl.ds(first_row * PACK_SUB + c, rows, stride=PACK_SUB), slice(None))


def _pack_rows(v, out_ref, first_row=0):
    for c, words in enumerate(_pack_words(v)):
        out_ref[_packed_chunk(c, v.shape[0], first_row)] = words


def _unpack_chunks(in_ref, rows, first_row=0):
    lo, hi = [], []
    for c in range(PACK_SUB):
        w = in_ref[_packed_chunk(c, rows, first_row)]
        lo.append(pltpu.bitcast(w << 16, F32))
        hi.append(pltpu.bitcast(w & jnp.uint32(0xFFFF0000), F32))
    return lo + hi


def _unpack_rows(in_ref, rows):
    return jnp.concatenate(_unpack_chunks(in_ref, rows), axis=1).astype(BF16)


def _merge_kernel(x_ref, attn_ref, lru_ref, g1_ref, wg_ref, woa_ref, wol_ref, wout_ref, g2_ref, wrt_ref,
                  x1_ref, h2p_ref, route_ref, counts_ref, cnt, tri):
    tm = MERGE_CHAIN

    @pl.when(pl.program_id(0) == 0)
    def _():
        cnt[...] = jnp.zeros_like(cnt)
        r = lax.broadcasted_iota(jnp.int32, (tm, tm), 0)
        c = lax.broadcasted_iota(jnp.int32, (tm, tm), 1)
        tri[...] = jnp.where(c < r, 1.0, 0.0).astype(BF16)

    counts = cnt[...]
    for r0 in range(0, SEQ_TILE, tm):
        counts = _merge_chain(r0, tm, counts, x_ref, attn_ref, lru_ref, g1_ref, wg_ref, woa_ref, wol_ref, wout_ref,
                              g2_ref, wrt_ref, x1_ref, h2p_ref, route_ref, tri)
    cnt[...] = counts
    counts_ref[...] = counts


def _merge_chain(r0, tm, counts, x_ref, attn_ref, lru_ref, g1_ref, wg_ref, woa_ref, wol_ref, wout_ref, g2_ref, wrt_ref,
                 x1_ref, h2p_ref, route_ref, tri):
    rs = slice(r0, r0 + tm)
    x = x_ref[rs, :]
    hb = _rms(x, g1_ref[...]).astype(BF16)
    gates = 0.5 * jnp.tanh(0.5 * _dot(hb, wg_ref[...])) + 0.5
    merged = gates[:, :D_MODEL] * _dot(attn_ref[rs, :], woa_ref[...]) + gates[:, D_MODEL:] * _dot(lru_ref[rs, :], wol_ref[...])
    x1 = x + _dot(merged.astype(BF16), wout_ref[...])
    x1_ref[rs, :] = x1
    h2 = _rms(x1, g2_ref[...])
    _pack_rows(h2, h2p_ref, first_row=r0)

    h_hi = h2.astype(BF16)
    h_lo = (h2 - h_hi.astype(F32)).astype(BF16)
    wrt = wrt_ref[...]
    hh = _dot(h_hi, wrt)
    logits = hh[:, :ROUTE_LANES] + hh[:, ROUTE_LANES:] + _dot(h_lo, wrt[:, :ROUTE_LANES])
    lane = lax.broadcasted_iota(jnp.int32, logits.shape, 1)
    big = jnp.int32(1 << 20)

    def first_argmax(v):
        m = jnp.max(v, axis=-1, keepdims=True)
        return m, jnp.min(jnp.where(v == m, lane, big), axis=-1, keepdims=True)

    gmask = lane < N_GROUPS
    gmax, gidx = first_argmax(jnp.where(gmask, logits, -jnp.inf))
    gsum = jnp.sum(jnp.where(gmask, jnp.exp(logits - gmax), 0.0), axis=-1, keepdims=True)
    g_w = 1.0 / gsum
    lo = N_GROUPS + EXPERTS_PER_GROUP * gidx
    el = jnp.where((lane >= lo) & (lane < lo + EXPERTS_PER_GROUP), logits, -jnp.inf)
    m1, i1 = first_argmax(el)
    m2, i2 = first_argmax(jnp.where(lane == i1, -jnp.inf, el))
    rr = jnp.exp(m2 - m1)
    w1 = g_w / (1.0 + rr)
    w2 = g_w * rr / (1.0 + rr)
    oh1 = lane == i1
    oh2 = lane == i2
    oh = jnp.where(oh1 | oh2, 1.0, 0.0)
    before = _dot(tri[...], oh.astype(BF16)) + counts
    r1 = jnp.sum(jnp.where(oh1, before, 0.0), axis=-1, keepdims=True)
    r2 = jnp.sum(jnp.where(oh2, before, 0.0), axis=-1, keepdims=True)
    vals = (i1.astype(F32), i2.astype(F32), w1, w2, r1, r2)
    route = jnp.zeros_like(logits)
    for k, v in enumerate(vals):
        route = jnp.where(lane == k, v, route)
    route_ref[rs, :] = route
    return counts + jnp.sum(oh, axis=0, keepdims=True)


def _merge(x2, attn, lru, g1, wg, woa, wol, wout, g2, wrt):
    N = x2.shape[0]
    tm = SEQ_TILE
    rows = lambda c: pl.BlockSpec((tm, c), lambda i: (i, 0))
    full = lambda r, c: pl.BlockSpec((r, c), lambda i: (0, 0))
    return pl.pallas_call(
        _merge_kernel,
        grid=(N // tm,),
        in_specs=[rows(D_MODEL), rows(ATTN_WIDTH), rows(LRU_WIDTH), full(1, D_MODEL), full(D_MODEL, 2 * D_MODEL),
                  full(ATTN_WIDTH, D_MODEL), full(LRU_WIDTH, D_MODEL), full(D_MODEL, D_MODEL), full(1, D_MODEL),
                  full(D_MODEL, 2 * ROUTE_LANES)],
        out_specs=[rows(D_MODEL), pl.BlockSpec((tm * PACK_SUB, LANES), lambda i: (i, 0)), rows(ROUTE_LANES),
                   full(1, ROUTE_LANES)],
        out_shape=[jax.ShapeDtypeStruct((N, D_MODEL), F32), jax.ShapeDtypeStruct((N * PACK_SUB, LANES), jnp.uint32),
                   jax.ShapeDtypeStruct((N, ROUTE_LANES), F32), jax.ShapeDtypeStruct((1, ROUTE_LANES), F32)],
        scratch_shapes=[pltpu.VMEM((1, ROUTE_LANES), F32), pltpu.VMEM((MERGE_CHAIN, MERGE_CHAIN), BF16)],
        compiler_params=pltpu.CompilerParams(dimension_semantics=("arbitrary",), vmem_limit_bytes=VMEM_LIMIT),
        name="merge_route",
    )(x2, attn, lru, g1, wg, woa, wol, wout, g2, wrt)


def _for_each_assignment(off_ref, fn):
    def group(gi, _):
        toks = [gi * DMA_GROUP + j for j in range(DMA_GROUP)]
        offs = [[off_ref[0, 0, tk * TOP_K + k] for k in range(TOP_K)] for tk in toks]
        for tk, o in zip(toks, offs):
            for k in range(TOP_K):
                fn(tk, k, pl.multiple_of(o[k], PACK_SUB))
        return 0
    lax.fori_loop(0, ROW_TILE // DMA_GROUP, group, 0)


def _dispatch_kernel(off_ref, h2p_hbm, xs_hbm, xin, in_sem, out_sem):
    tile_rows = ROW_TILE * PACK_SUB
    i = pl.program_id(0)
    n = pl.num_programs(0)
    slot = lax.rem(i, DISPATCH_SLOTS)

    def fetch(tile, s):
        return pltpu.make_async_copy(h2p_hbm.at[pl.ds(pl.multiple_of(tile * tile_rows, tile_rows), tile_rows)],
                                     xin.at[s], in_sem.at[s])

    def drain(s):
        for _ in range(TOP_K):
            pltpu.make_async_copy(xin.at[s], xs_hbm.at[pl.ds(0, tile_rows)], out_sem.at[s]).wait()

    @pl.when(i == 0)
    def _():
        fetch(0, 0).start()

    @pl.when(i + 1 < n)
    def _():
        fetch(i + 1, lax.rem(i + 1, DISPATCH_SLOTS)).start()

    fetch(i, slot).wait()

    def start(tk, k, off):
        pltpu.make_async_copy(xin.at[slot, pl.ds(pl.multiple_of(tk * PACK_SUB, PACK_SUB), PACK_SUB)],
                              xs_hbm.at[pl.ds(off, PACK_SUB)], out_sem.at[slot]).start(priority=k)

    _for_each_assignment(off_ref, start)

    @pl.when(i > 0)
    def _():
        drain(lax.rem(i + DISPATCH_SLOTS - 1, DISPATCH_SLOTS))

    @pl.when(i == n - 1)
    def _():
        drain(slot)


def _dispatch(off3, h2p):
    return pl.pallas_call(
        _dispatch_kernel,
        grid=(off3.shape[0],),
        in_specs=[pl.BlockSpec((1, 1, TOP_K * ROW_TILE), lambda i: (i, 0, 0), memory_space=pltpu.SMEM),
                  pl.BlockSpec(memory_space=pl.ANY)],
        out_specs=pl.BlockSpec(memory_space=pl.ANY),
        out_shape=jax.ShapeDtypeStruct((TOP_K * h2p.shape[0], LANES), jnp.uint32),
        scratch_shapes=[pltpu.VMEM((DISPATCH_SLOTS, ROW_TILE * PACK_SUB, LANES), jnp.uint32),
                        pltpu.SemaphoreType.DMA((DISPATCH_SLOTS,)), pltpu.SemaphoreType.DMA((DISPATCH_SLOTS,))],
        compiler_params=pltpu.CompilerParams(dimension_semantics=("arbitrary",), has_side_effects=True),
        name="moe_dispatch",
    )(off3, h2p)


def _moe_kernel(vt_ref, ve_ref, vlo_ref, vhi_ref, xs_ref, wg_ref, wu_ref, wd_ref, ys_ref, wgb, wub, wdb):
    tm = MOE_TILE
    v = pl.program_id(0)
    nv = pl.num_programs(0)
    t = vt_ref[v]
    e = ve_ref[v]
    prev = jnp.maximum(v - 1, 0)

    @pl.when((v == 0) | (ve_ref[prev] != e))
    def _():
        wgb[...] = wg_ref[...].astype(BF16)
        wub[...] = wu_ref[...].astype(BF16)
        wdb[...] = wd_ref[...].astype(BF16)

    xb = _unpack_rows(xs_ref, tm)
    g = _dot(xb, wgb[...])
    u = _dot(xb, wub[...])
    hmid = (g * jax.nn.sigmoid(g)) * u
    words = _pack_words(_dot(hmid.astype(BF16), wdb[...]))
    first = (v == 0) | (vt_ref[prev] != t)

    @pl.when(first)
    def _():
        for c in range(PACK_SUB):
            ys_ref[_packed_chunk(c, tm)] = words[c]

    @pl.when(jnp.logical_not(first))
    def _():
        rows = t * tm + lax.broadcasted_iota(jnp.int32, (tm, LANES), 0)
        mine = (rows >= vlo_ref[v]) & (rows < vhi_ref[v])
        for c in range(PACK_SUB):
            ys_ref[_packed_chunk(c, tm)] = jnp.where(mine, words[c], ys_ref[_packed_chunk(c, tm)])


def _moe(vt, ve, vlo, vhi, xs, wg, wu, wd):
    tm = MOE_TILE
    n_rows = xs.shape[0] // PACK_SUB
    grid_spec = pltpu.PrefetchScalarGridSpec(
        num_scalar_prefetch=4,
        grid=(vt.shape[0],),
        in_specs=[pl.BlockSpec((tm * PACK_SUB, LANES), lambda v, vt, ve, lo, hi: (vt[v], 0)),
                  pl.BlockSpec((None, D_MODEL, D_EXPERT), lambda v, vt, ve, lo, hi: (ve[v], 0, 0)),
                  pl.BlockSpec((None, D_MODEL, D_EXPERT), lambda v, vt, ve, lo, hi: (ve[v], 0, 0)),
                  pl.BlockSpec((None, D_EXPERT, D_MODEL), lambda v, vt, ve, lo, hi: (ve[v], 0, 0))],
        out_specs=pl.BlockSpec((tm * PACK_SUB, LANES), lambda v, vt, ve, lo, hi: (vt[v], 0)),
        scratch_shapes=[pltpu.VMEM((D_MODEL, D_EXPERT), BF16), pltpu.VMEM((D_MODEL, D_EXPERT), BF16),
                        pltpu.VMEM((D_EXPERT, D_MODEL), BF16)],
    )
    return pl.pallas_call(
        _moe_kernel,
        grid_spec=grid_spec,
        out_shape=jax.ShapeDtypeStruct((n_rows * PACK_SUB, LANES), jnp.uint32),
        compiler_params=pltpu.CompilerParams(dimension_semantics=("arbitrary",), vmem_limit_bytes=VMEM_LIMIT),
        name="moe_experts",
    )(vt, ve, vlo, vhi, xs, wg, wu, wd)


def _combine_kernel(off_ref, offn_ref, x1_ref, route_ref, g_ref, ys_hbm, o_ref, ybuf, sem):
    tc = ROW_TILE
    i = pl.program_id(0)
    n = pl.num_programs(0)
    slot = lax.rem(i, 2)

    def gather(o_ref_, s):
        def start(tk, k, off):
            pltpu.make_async_copy(ys_hbm.at[pl.ds(off, PACK_SUB)],
                                  ybuf.at[s, pl.ds(pl.multiple_of((k * tc + tk) * PACK_SUB, PACK_SUB), PACK_SUB)],
                                  sem.at[s]).start(priority=k)
        _for_each_assignment(o_ref_, start)

    def drain(s):
        pltpu.make_async_copy(ys_hbm.at[pl.ds(0, TOP_K * tc * PACK_SUB)], ybuf.at[s], sem.at[s]).wait()

    @pl.when(i == 0)
    def _():
        gather(off_ref, 0)

    @pl.when(i + 1 < n)
    def _():
        gather(offn_ref, 1 - slot)

    drain(slot)

    route = route_ref[...]
    n_chunks = D_MODEL // LANES
    z = [x1_ref[:, c * LANES:(c + 1) * LANES] for c in range(n_chunks)]
    for k in range(TOP_K):
        wk = route[:, TOP_K + k:TOP_K + k + 1]
        yk = _unpack_chunks(ybuf.at[slot], tc, first_row=k * tc)
        z = [zc + wk * yc for zc, yc in zip(z, yk)]
    ss = sum(jnp.sum(zc * zc, axis=-1, keepdims=True) for zc in z)
    inv = lax.rsqrt(ss * (1.0 / D_MODEL) + NORM_EPS)
    for c in range(n_chunks):
        o_ref[:, c * LANES:(c + 1) * LANES] = z[c] * inv * g_ref[:, c * LANES:(c + 1) * LANES]


def _combine(off3, x1, route, g, ys):
    N = x1.shape[0]
    tc = ROW_TILE
    nt = N // tc
    idx = lambda f: pl.BlockSpec((1, 1, TOP_K * tc), f, memory_space=pltpu.SMEM)
    return pl.pallas_call(
        _combine_kernel,
        grid=(nt,),
        in_specs=[idx(lambda i: (i, 0, 0)),
                  idx(lambda i: (jnp.minimum(i + 1, nt - 1), 0, 0)),
                  pl.BlockSpec((tc, D_MODEL), lambda i: (i, 0)),
                  pl.BlockSpec((tc, ROUTE_LANES), lambda i: (i, 0)),
                  pl.BlockSpec((1, D_MODEL), lambda i: (0, 0)),
                  pl.BlockSpec(memory_space=pl.ANY)],
        out_specs=pl.BlockSpec((tc, D_MODEL), lambda i: (i, 0)),
        out_shape=jax.ShapeDtypeStruct((N, D_MODEL), F32),
        scratch_shapes=[pltpu.VMEM((2, TOP_K * tc * PACK_SUB, LANES), jnp.uint32), pltpu.SemaphoreType.DMA((2,))],
        compiler_params=pltpu.CompilerParams(dimension_semantics=("arbitrary",), vmem_limit_bytes=VMEM_LIMIT),
        name="combine_norm",
    )(off3, off3, x1, route, g, ys)


def _block_diag(w):
    nb, c, _ = w.shape
    eye = jnp.eye(nb, dtype=w.dtype)
    return (eye[:, None, :, None] * w[:, :, None, :]).reshape(nb * c, nb * c)


def _visit_plan(counts, n_rows):
    tm = MOE_TILE
    n_tiles = n_rows // tm
    n_visits = n_tiles + N_EXPERTS - 1
    cnt = counts.astype(jnp.int32)
    lanes = jnp.arange(ROUTE_LANES, dtype=jnp.int32)
    upto = (lanes[None, :] <= lanes[:, None]).astype(jnp.int32)
    ends = upto @ cnt
    starts = ends - cnt
    first_tile = starts // tm
    n_vis = jnp.where(cnt > 0, (ends - 1) // tm - first_tile + 1, 0)
    v_end = upto @ n_vis
    v_start = v_end - n_vis
    total = v_end[-1]
    v = jnp.arange(n_visits, dtype=jnp.int32)
    vc = jnp.minimum(v, total - 1)
    own = ((vc[:, None] >= v_start[None, :]) & (vc[:, None] < v_end[None, :])).astype(jnp.int32)
    pick = lambda per_lane: own @ per_lane
    tile = jnp.maximum(pick(first_tile - v_start) + vc, 0)
    lo = jnp.maximum(pick(starts), tile * tm)
    hi = jnp.minimum(pick(ends), (tile + 1) * tm)
    valid = v < total
    ve = jnp.maximum(pick(lanes) - EXPERT_LANE0, 0)
    return starts, tile, ve, jnp.where(valid, lo, 0), jnp.where(valid, hi, 0)


def kernel(x, norm_mix_g, w_in, lambda_qk, subln_g, conv_w, conv_b, w_r, b_r, w_i, b_i, lru_lambda, w_o_attn, w_o_lru, w_out, norm_ffn_g, w_group, w_expert_router, w_gate, w_up, w_down, final_norm_g):
    B, S, D = x.shape
    N = B * S
    nt = S // SEQ_TILE
    depth = norm_mix_g.shape[0]
    assert depth == 1 and D == D_MODEL and S % SEQ_TILE == 0 and S % LRU_TILE == 0
    assert N % ROW_TILE == 0 and (N * TOP_K) % MOE_TILE == 0
    l = 0
    row = lambda v: v.reshape(1, -1).astype(F32)

    x2 = x.reshape(N, D)
    w_in_l = w_in[l]
    qT, k, vT, xr, yr, kn = _inproj(x2, row(norm_mix_g[l]), w_in_l[:, :PROJ_COLS].astype(BF16), B, S)

    attn = _attn(kn[:, 0, :2 * N_HEADS].reshape(-1), qT, k.reshape(B, nt, SEQ_TILE, N_HEADS * K_COLS), vT,
                 lambda_qk[l].reshape(4, HEAD_DIM).astype(F32), row(subln_g[l]), B, S)

    lru = _lru(xr.reshape(B, S, LRU_WIDTH), yr.reshape(B, S, LRU_WIDTH), conv_w[l].astype(F32), row(conv_b[l]),
               _block_diag(w_r[l]).astype(BF16), _block_diag(w_i[l]).astype(BF16), row(b_r[l]), row(b_i[l]),
               row(lru_lambda[l]), B, S)

    w_route = jnp.concatenate(
        [w_group[l], jnp.transpose(w_expert_router[l], (1, 0, 2)).reshape(D, N_EXPERTS),
         jnp.zeros((D, ROUTE_LANES - N_GROUPS - N_EXPERTS), F32)], axis=1).astype(F32)
    w_route_hi = w_route.astype(BF16)
    w_route = jnp.concatenate([w_route_hi, (w_route - w_route_hi.astype(F32)).astype(BF16)], axis=1)
    x1, h2p, route, counts = _merge(x2, attn.reshape(N, ATTN_WIDTH), lru.reshape(N, LRU_WIDTH), row(norm_mix_g[l]),
                                    w_in_l[:, PROJ_COLS:].astype(BF16), w_o_attn[l].astype(BF16),
                                    w_o_lru[l].astype(BF16), w_out[l].astype(BF16), row(norm_ffn_g[l]), w_route)

    starts, vt, ve, vlo, vhi = _visit_plan(counts[0], N * TOP_K)
    lane = route[:, 0:TOP_K].astype(jnp.int32)
    rank = route[:, 2 * TOP_K:3 * TOP_K].astype(jnp.int32)
    lane_ids = jnp.arange(ROUTE_LANES, dtype=jnp.int32)
    start_of = jnp.sum(jnp.where(lane[..., None] == lane_ids, starts, 0), axis=-1)
    off3 = ((start_of + rank) * PACK_SUB).reshape(N // ROW_TILE, 1, TOP_K * ROW_TILE)

    xs = _dispatch(off3, h2p)
    ys = _moe(vt, ve, vlo, vhi, xs, w_gate[l], w_up[l], w_down[l])
    out = _combine(off3, x1, route, row(final_norm_g), ys)
    return out.reshape(B, S, D)
```

```python
import functools
import math

import jax
import jax.numpy as jnp
from jax import lax
from jax.experimental import pallas as pl
from jax.experimental.pallas import tpu as pltpu

F32 = jnp.float32
BF16 = jnp.bfloat16

D_MODEL = 1024
N_HEADS = 4
HEAD_DIM = 64
V_DIM = 2 * HEAD_DIM
ATTN_WIDTH = N_HEADS * V_DIM
V_ROWS = V_DIM + 16
K_COLS = 2 * V_DIM
POS_SPLIT = 3
POS_RADIX = 256
SKIP_MARGIN = 138.0
NORM_SLACK = 1.0201
LRU_WIDTH = D_MODEL // 2
LRU_BLOCKS = 8
CONV_W = 4
LRU_C = 8.0
N_GROUPS = 4
EXPERTS_PER_GROUP = 8
N_EXPERTS = N_GROUPS * EXPERTS_PER_GROUP
TOP_K = 2
D_EXPERT = D_MODEL // 2
NORM_EPS = 1e-6
LAM_INIT = 0.8 - 0.6 * math.exp(-0.3 * 0)

QK_COLS = N_HEADS * 2 * HEAD_DIM
PROJ_COLS = 2 * QK_COLS + ATTN_WIDTH + 2 * LRU_WIDTH
ROUTE_LANES = 128
NEG_BIG = -1e30
LOG2E = math.log2(math.e)
ALIBI_SLOPES = tuple(2.0 ** (-8.0 * (h + 1) / N_HEADS) for h in range(N_HEADS))

SEQ_TILE = 512
LRU_TILE = 512
ROW_TILE = 256
MERGE_CHAIN = 512
MOE_TILE = 256
DMA_GROUP = 8
DISPATCH_SLOTS = 3
LANES = 128
SUBLANES = 8
PACK_SUB = D_MODEL // 2 // LANES
EXPERT_LANE0 = N_GROUPS
VMEM_LIMIT = 48 * 1024 * 1024


def _rms(x, g):
    return x * lax.rsqrt(jnp.mean(x * x, axis=-1, keepdims=True) + NORM_EPS) * g


def _dot(a, b):
    return jnp.dot(a, b, preferred_element_type=F32)


def _inproj_kernel(x_ref, g_ref, w_ref, qT_ref, k_ref, vT_ref, xr_ref, yr_ref, kn_ref):
    hb = _rms(x_ref[...], g_ref[...]).astype(BF16)

    def proj(lo, hi):
        return _dot(hb, w_ref[:, lo:hi])

    q = proj(0, QK_COLS) * (HEAD_DIM ** -0.5 * LOG2E)
    for h in range(N_HEADS):
        qT_ref[h] = q[:, h * V_DIM:(h + 1) * V_DIM].T.astype(BF16)
    k = proj(QK_COLS, 2 * QK_COLS).astype(BF16)
    r = lax.broadcasted_iota(jnp.int32, (k.shape[0], K_COLS - V_DIM), 0)
    lane = lax.broadcasted_iota(jnp.int32, r.shape, 1)
    a = r // POS_RADIX * POS_RADIX
    feat = jnp.where(lane < POS_SPLIT, a, jnp.where(lane < 2 * POS_SPLIT, r - a, 0)).astype(F32).astype(BF16)
    for h in range(N_HEADS):
        k_ref[:, h * K_COLS:h * K_COLS + V_DIM] = k[:, h * V_DIM:(h + 1) * V_DIM]
        k_ref[:, h * K_COLS + V_DIM:(h + 1) * K_COLS] = feat
    seg = lax.broadcasted_iota(jnp.int32, (QK_COLS, LANES), 0) // HEAD_DIM
    pick = jnp.where(seg == lax.broadcasted_iota(jnp.int32, (QK_COLS, LANES), 1), 1.0, 0.0).astype(BF16)
    row_n2 = _dot(jnp.square(k.astype(F32)).astype(BF16), pick)
    kn_ref[...] = jnp.max(row_n2, axis=0, keepdims=True)
    v = proj(2 * QK_COLS, 2 * QK_COLS + ATTN_WIDTH)
    for h in range(N_HEADS):
        vT_ref[h, :V_DIM, :] = v[:, h * V_DIM:(h + 1) * V_DIM].T.astype(BF16)
        pad_row = lax.broadcasted_iota(jnp.int32, (V_ROWS - V_DIM, v.shape[0]), 0)
        vT_ref[h, V_DIM:, :] = jnp.where(pad_row == 0, 1.0, 0.0).astype(BF16)
    c0 = 2 * QK_COLS + ATTN_WIDTH
    xr_ref[...] = proj(c0, c0 + LRU_WIDTH)
    yr_ref[...] = proj(c0 + LRU_WIDTH, c0 + 2 * LRU_WIDTH)


def _inproj(x2, g, w, B, S):
    N = B * S
    tm = SEQ_TILE
    nt = S // tm
    tile5 = pl.BlockSpec((None, N_HEADS, None, V_DIM, tm), lambda i: (i // nt, 0, i % nt, 0, 0))
    rows = lambda c: pl.BlockSpec((tm, c), lambda i: (i, 0))
    return pl.pallas_call(
        _inproj_kernel,
        grid=(N // tm,),
        in_specs=[rows(D_MODEL),
                  pl.BlockSpec((1, D_MODEL), lambda i: (0, 0)),
                  pl.BlockSpec((D_MODEL, PROJ_COLS), lambda i: (0, 0))],
        out_specs=[tile5, rows(N_HEADS * K_COLS),
                   pl.BlockSpec((None, N_HEADS, None, V_ROWS, tm), lambda i: (i // nt, 0, i % nt, 0, 0)),
                   rows(LRU_WIDTH), rows(LRU_WIDTH),
                   pl.BlockSpec((None, 1, LANES), lambda i: (i, 0, 0))],
        out_shape=[jax.ShapeDtypeStruct((B, N_HEADS, nt, V_DIM, tm), BF16),
                   jax.ShapeDtypeStruct((N, N_HEADS * K_COLS), BF16),
                   jax.ShapeDtypeStruct((B, N_HEADS, nt, V_ROWS, tm), BF16),
                   jax.ShapeDtypeStruct((N, LRU_WIDTH), F32),
                   jax.ShapeDtypeStruct((N, LRU_WIDTH), F32),
                   jax.ShapeDtypeStruct((N // tm, 1, LANES), F32)],
        compiler_params=pltpu.CompilerParams(dimension_semantics=("parallel",),
                                             vmem_limit_bytes=VMEM_LIMIT),
        name="inproj",
    )(x2, g, w)


def _attn_kernel(kn_ref, qT_ref, k_ref, vT_ref, lam_ref, g_ref, o_ref, mask_ref, acc_ref, sa_ref, sb_ref, pa_ref,
                 pb_ref):
    t = SEQ_TILE
    b = pl.program_id(0)
    h = pl.program_id(1)
    i = pl.program_id(2)
    slope = LOG2E * jnp.where(h == 0, ALIBI_SLOPES[0], jnp.where(h == 1, ALIBI_SLOPES[1],
                              jnp.where(h == 2, ALIBI_SLOPES[2], ALIBI_SLOPES[3]))).astype(F32)

    @pl.when(i == 0)
    def _():
        r = lax.broadcasted_iota(jnp.int32, (t, t), 0)
        c = lax.broadcasted_iota(jnp.int32, (t, t), 1)
        mask_ref[...] = jnp.where(r <= c, 0.0, NEG_BIG)

    qf = qT_ref[...].astype(F32)
    row = lax.broadcasted_iota(jnp.int32, qf.shape, 0)
    sl = jnp.full(qf.shape, slope, F32)
    hi = sl.astype(BF16).astype(F32)
    mid = (sl - hi).astype(BF16).astype(F32)
    lo = (sl - hi - mid).astype(BF16).astype(F32)
    piece = jnp.where(row % 3 == 0, hi, jnp.where(row % 3 == 1, mid, lo))
    srows = jnp.where(row < 2 * POS_SPLIT, piece, 0.0).astype(BF16)
    qs = tuple(jnp.concatenate([jnp.where(sel, qf, 0.0).astype(BF16), srows], axis=0)
               for sel in (row < HEAD_DIM, row >= HEAD_DIM))
    acc_ref[...] = jnp.zeros_like(acc_ref)
    pb_ref[...] = jnp.zeros_like(pb_ref)

    first_off = [0]

    def key_tile(tau):
        return jnp.where(tau <= 0, i, first_off[0] + tau - 1)

    def stage_q(tau, s_ref, diagonal=False):
        kt = k_ref[key_tile(tau)]
        tile_max = []
        for mi in range(2):
            s = _dot(kt, qs[mi])
            if diagonal:
                s = s + mask_ref[...]
            s_ref[mi] = s
            tile_max.append(jnp.max(s, axis=0, keepdims=True))
        return tuple(tile_max)

    def stage_s(tau, s_ref, p_ref, ms, tile_max):
        cj = slope * (key_tile(tau) * t).astype(F32)
        m_out, alphas = [], []
        for mi in range(2):
            m_new = jnp.maximum(ms[mi], tile_max[mi] + cj)
            alphas.append(jnp.exp2(ms[mi] - m_new))
            p_ref[mi] = jnp.exp2(s_ref[mi] - (m_new - cj)).astype(BF16)
            m_out.append(m_new)
        return tuple(m_out), tuple(alphas)

    def stage_v(tau, p_ref, alphas):
        vt = vT_ref[key_tile(tau)]
        for mi in range(2):
            acc_ref[mi] = alphas[mi] * acc_ref[mi] + _dot(vt, p_ref[mi])

    def body(jj, carry):
        ms, alphas, tmax = carry[:2], carry[2:4], carry[4:]
        tau = 2 * jj
        tmax_b = stage_q(tau + 1, sb_ref)
        ms, alphas_a = stage_s(tau, sa_ref, pa_ref, ms, tmax)
        stage_v(tau - 1, pb_ref, alphas)
        tmax_a = stage_q(tau + 2, sa_ref)
        ms, alphas_b = stage_s(tau + 1, sb_ref, pb_ref, ms, tmax_b)
        stage_v(tau, pa_ref, alphas_a)
        return ms + alphas_b + tmax_a

    def finalize():
        lp = lam_ref[...]
        s1 = jnp.sum(lp[0:1] * lp[1:2], axis=-1, keepdims=True)
        s2 = jnp.sum(lp[2:3] * lp[3:4], axis=-1, keepdims=True)
        lam = jnp.exp(s1) - jnp.exp(s2) + LAM_INIT
        norm = [acc_ref[mi, :V_DIM, :] * (1.0 / acc_ref[mi, V_DIM:V_DIM + 1, :]) for mi in range(2)]
        oT = norm[0] - lam * norm[1]
        o = _rms(oT.T, g_ref[...]) * (1.0 - LAM_INIT)
        o_ref[...] = o.astype(BF16)

    m_init = jnp.full((1, t), NEG_BIG, F32)
    one = jnp.ones((1, t), F32)
    tmax0 = stage_q(0, sa_ref, diagonal=True)

    cj0 = slope * (i * t).astype(F32)
    qn2, m_low = [], []
    for mi in range(2):
        qsq = jnp.square(qf[mi * HEAD_DIM:(mi + 1) * HEAD_DIM, :])
        qn2.append(jnp.max(jnp.sum(qsq, axis=0, keepdims=True), axis=1, keepdims=True)[0, 0])
        m_low.append(jnp.min(tmax0[mi], axis=1, keepdims=True)[0, 0] + cj0)

    def skippable(j):
        ok = None
        for mi in range(2):
            kn2 = kn_ref[((b * pl.num_programs(2) + j) * N_HEADS + h) * 2 + mi]
            room = m_low[mi] - SKIP_MARGIN - slope * (t - 1) - slope * (j * t).astype(F32)
            fits = (room > 0.0) & (qn2[mi] * kn2 * NORM_SLACK < room * room)
            ok = fits if ok is None else ok & fits
        return ok

    first_off[0] = lax.fori_loop(0, i, lambda j, j0: jnp.where((j0 == j) & skippable(j), j + 1, j0), 0)
    n_off = i - first_off[0]

    n_main = jnp.right_shift(n_off, 1)
    fin = lax.fori_loop(0, n_main, body, (m_init, m_init, one, one) + tmax0)
    ms, alphas, tmax = fin[:2], fin[2:4], fin[4:]
    tau = 2 * n_main
    odd_tiles = tau == n_off

    @pl.when(odd_tiles)
    def _():
        _, alphas_a = stage_s(tau, sa_ref, pa_ref, ms, tmax)
        stage_v(tau - 1, pb_ref, alphas)
        stage_v(tau, pa_ref, alphas_a)
        finalize()

    @pl.when(jnp.logical_not(odd_tiles))
    def _():
        tmax_b = stage_q(tau + 1, sb_ref)
        ms_a, alphas_a = stage_s(tau, sa_ref, pa_ref, ms, tmax)
        stage_v(tau - 1, pb_ref, alphas)
        _, alphas_b = stage_s(tau + 1, sb_ref, pb_ref, ms_a, tmax_b)
        stage_v(tau, pa_ref, alphas_a)
        stage_v(tau + 1, pb_ref, alphas_b)
        finalize()


def _attn(kn, qT, k4, vT, lam, g, B, S):
    t = SEQ_TILE
    nt = S // t
    grid_spec = pltpu.PrefetchScalarGridSpec(
        num_scalar_prefetch=1,
        grid=(B, N_HEADS, nt),
        in_specs=[pl.BlockSpec((None, None, None, V_DIM, t), lambda b, h, i, kn: (b, h, i, 0, 0)),
                  pl.BlockSpec((None, nt, t, K_COLS), lambda b, h, i, kn: (b, 0, 0, h)),
                  pl.BlockSpec((None, None, nt, V_ROWS, t), lambda b, h, i, kn: (b, h, 0, 0, 0)),
                  pl.BlockSpec((4, HEAD_DIM), lambda b, h, i, kn: (0, 0)),
                  pl.BlockSpec((1, V_DIM), lambda b, h, i, kn: (0, 0))],
        out_specs=pl.BlockSpec((None, t, V_DIM), lambda b, h, i, kn: (b, i, h)),
        scratch_shapes=[pltpu.VMEM((t, t), F32), pltpu.VMEM((2, V_ROWS, t), F32),
                        pltpu.VMEM((2, t, t), F32), pltpu.VMEM((2, t, t), F32),
                        pltpu.VMEM((2, t, t), BF16), pltpu.VMEM((2, t, t), BF16)],
    )
    return pl.pallas_call(
        _attn_kernel,
        grid_spec=grid_spec,
        out_shape=jax.ShapeDtypeStruct((B, S, ATTN_WIDTH), BF16),
        compiler_params=pltpu.CompilerParams(dimension_semantics=("parallel", "arbitrary", "arbitrary"),
                                             vmem_limit_bytes=VMEM_LIMIT),
        name="diff_attn",
    )(kn, qT, k4, vT, lam, g)


def _lru_kernel(xr_ref, yr_ref, cw_ref, cb_ref, wr_ref, wi_ref, br_ref, bi_ref, lam_ref, o_ref, xbuf, hc):
    T = LRU_TILE
    ti = pl.program_id(1)

    @pl.when(ti == 0)
    def _():
        xbuf[0:8] = jnp.zeros((8, LRU_WIDTH), F32)
        hc[...] = jnp.zeros_like(hc)

    x = xr_ref[...]
    xbuf[8:8 + T] = x
    cw = cw_ref[...]
    xc = cb_ref[...] + cw[3:4] * x
    for j in range(CONV_W - 1):
        xc = xc + cw[j:j + 1] * xbuf[5 + j:5 + j + T]
    xbuf[0:8] = x[T - 8:T]

    xb = xc.astype(BF16)
    r = jax.nn.sigmoid(_dot(xb, wr_ref[...]) + br_ref[...])
    ig = jax.nn.sigmoid(_dot(xb, wi_ref[...]) + bi_ref[...])
    z = -lam_ref[...]
    softplus = jnp.maximum(z, 0.0) + jnp.log1p(jnp.exp(-jnp.abs(z)))
    la = -LRU_C * r * softplus
    a = jnp.exp(la)
    m2 = -jnp.tanh(la) * (a * a + 1.0)
    mult = jnp.where(m2 > 0.0, m2 * lax.rsqrt(m2), 0.0)
    row = lax.broadcasted_iota(jnp.int32, (T, LRU_WIDTH), 0)
    mult = jnp.where((row == 0) & (ti == 0), 1.0, mult)
    u = (xc * ig) * mult

    nb = T // SUBLANES
    a3 = a.reshape(nb, SUBLANES, LRU_WIDTH)
    u3 = u.reshape(nb, SUBLANES, LRU_WIDTH)
    sub = lax.broadcasted_iota(jnp.int32, a3.shape, 1)
    d = 1
    while d < SUBLANES:
        valid = sub >= d
        u3 = jnp.where(valid, a3 * pltpu.roll(u3, d, 1) + u3, u3)
        a3 = jnp.where(valid, a3 * pltpu.roll(a3, d, 1), a3)
        d *= 2
    h = hc[...]
    blocks = []
    for b in range(nb):
        hb = u3[b] + a3[b] * h
        blocks.append(hb)
        h = hb[SUBLANES - 1:SUBLANES]
    hfull = jnp.concatenate(blocks, axis=0)
    hc[...] = h
    y = yr_ref[...]
    gelu = 0.5 * y * (1.0 + jnp.tanh(0.7978845608028654 * (y + 0.044715 * (y * y * y))))
    o_ref[...] = (hfull * gelu).astype(BF16)


def _lru(xr, yr, cw, cb, wr, wi, br, bi, lam, B, S):
    T = LRU_TILE
    seq = pl.BlockSpec((None, T, LRU_WIDTH), lambda b, t: (b, t, 0))
    full = lambda r, c: pl.BlockSpec((r, c), lambda b, t: (0, 0))
    return pl.pallas_call(
        _lru_kernel,
        grid=(B, S // T),
        in_specs=[seq, seq, full(CONV_W, LRU_WIDTH), full(1, LRU_WIDTH), full(LRU_WIDTH, LRU_WIDTH),
                  full(LRU_WIDTH, LRU_WIDTH), full(1, LRU_WIDTH), full(1, LRU_WIDTH), full(1, LRU_WIDTH)],
        out_specs=seq,
        out_shape=jax.ShapeDtypeStruct((B, S, LRU_WIDTH), BF16),
        scratch_shapes=[pltpu.VMEM((T + 8, LRU_WIDTH), F32), pltpu.VMEM((1, LRU_WIDTH), F32)],
        compiler_params=pltpu.CompilerParams(dimension_semantics=("arbitrary", "arbitrary"),
                                             vmem_limit_bytes=VMEM_LIMIT),
        name="rg_lru",
    )(xr, yr, cw, cb, wr, wi, br, bi, lam)


def _pack_words(v):
    bits = pltpu.bitcast(v.astype(BF16).astype(F32), jnp.uint32)
    half = D_MODEL // 2
    packed = (bits[:, :half] >> 16) | (bits[:, half:] & jnp.uint32(0xFFFF0000))
    return [packed[:, c * LANES:(c + 1) * LANES] for c in range(PACK_SUB)]


def _packed_chunk(c, rows, first_row=0):
    return (pl.ds(first_row * PACK_SUB + c, rows, stride=PACK_SUB), slice(None))


def _pack_rows(v, out_ref, first_row=0):
    for c, words in enumerate(_pack_words(v)):
        out_ref[_packed_chunk(c, v.shape[0], first_row)] = words


def _unpack_chunks(in_ref, rows, first_row=0):
    lo, hi = [], []
    for c in range(PACK_SUB):
        w = in_ref[_packed_chunk(c, rows, first_row)]
        lo.append(pltpu.bitcast(w << 16, F32))
        hi.append(pltpu.bitcast(w & jnp.uint32(0xFFFF0000), F32))
    return lo + hi


def _unpack_rows(in_ref, rows):
    return jnp.concatenate(_unpack_chunks(in_ref, rows), axis=1).astype(BF16)


def _merge_kernel(x_ref, attn_ref, lru_ref, g1_ref, wg_ref, woa_ref, wol_ref, wout_ref, g2_ref, wrt_ref,
                  x1_ref, h2p_ref, route_ref, counts_ref, cnt, tri):
    tm = MERGE_CHAIN

    @pl.when(pl.program_id(0) == 0)
    def _():
        cnt[...] = jnp.zeros_like(cnt)
        r = lax.broadcasted_iota(jnp.int32, (tm, tm), 0)
        c = lax.broadcasted_iota(jnp.int32, (tm, tm), 1)
        tri[...] = jnp.where(c < r, 1.0, 0.0).astype(BF16)

    counts = cnt[...]
    for r0 in range(0, SEQ_TILE, tm):
        counts = _merge_chain(r0, tm, counts, x_ref, attn_ref, lru_ref, g1_ref, wg_ref, woa_ref, wol_ref, wout_ref,
                              g2_ref, wrt_ref, x1_ref, h2p_ref, route_ref, tri)
    cnt[...] = counts
    counts_ref[...] = counts


def _merge_chain(r0, tm, counts, x_ref, attn_ref, lru_ref, g1_ref, wg_ref, woa_ref, wol_ref, wout_ref, g2_ref, wrt_ref,
                 x1_ref, h2p_ref, route_ref, tri):
    rs = slice(r0, r0 + tm)
    x = x_ref[rs, :]
    hb = _rms(x, g1_ref[...]).astype(BF16)
    gates = 0.5 * jnp.tanh(0.5 * _dot(hb, wg_ref[...])) + 0.5
    merged = gates[:, :D_MODEL] * _dot(attn_ref[rs, :], woa_ref[...]) + gates[:, D_MODEL:] * _dot(lru_ref[rs, :], wol_ref[...])
    x1 = x + _dot(merged.astype(BF16), wout_ref[...])
    x1_ref[rs, :] = x1
    h2 = _rms(x1, g2_ref[...])
    _pack_rows(h2, h2p_ref, first_row=r0)

    h_hi = h2.astype(BF16)
    h_lo = (h2 - h_hi.astype(F32)).astype(BF16)
    wrt = wrt_ref[...]
    hh = _dot(h_hi, wrt)
    logits = hh[:, :ROUTE_LANES] + hh[:, ROUTE_LANES:] + _dot(h_lo, wrt[:, :ROUTE_LANES])
    lane = lax.broadcasted_iota(jnp.int32, logits.shape, 1)
    big = jnp.int32(1 << 20)

    def first_argmax(v):
        m = jnp.max(v, axis=-1, keepdims=True)
        return m, jnp.min(jnp.where(v == m, lane, big), axis=-1, keepdims=True)

    gmask = lane < N_GROUPS
    gmax, gidx = first_argmax(jnp.where(gmask, logits, -jnp.inf))
    gsum = jnp.sum(jnp.where(gmask, jnp.exp(logits - gmax), 0.0), axis=-1, keepdims=True)
    g_w = 1.0 / gsum
    lo = N_GROUPS + EXPERTS_PER_GROUP * gidx
    el = jnp.where((lane >= lo) & (lane < lo + EXPERTS_PER_GROUP), logits, -jnp.inf)
    m1, i1 = first_argmax(el)
    m2, i2 = first_argmax(jnp.where(lane == i1, -jnp.inf, el))
    rr = jnp.exp(m2 - m1)
    w1 = g_w / (1.0 + rr)
    w2 = g_w * rr / (1.0 + rr)
    oh1 = lane == i1
    oh2 = lane == i2
    oh = jnp.where(oh1 | oh2, 1.0, 0.0)
    before = _dot(tri[...], oh.astype(BF16)) + counts
    r1 = jnp.sum(jnp.where(oh1, before, 0.0), axis=-1, keepdims=True)
    r2 = jnp.sum(jnp.where(oh2, before, 0.0), axis=-1, keepdims=True)
    vals = (i1.astype(F32), i2.astype(F32), w1, w2, r1, r2)
    route = jnp.zeros_like(logits)
    for k, v in enumerate(vals):
        route = jnp.where(lane == k, v, route)
    route_ref[rs, :] = route
    return counts + jnp.sum(oh, axis=0, keepdims=True)


def _merge(x2, attn, lru, g1, wg, woa, wol, wout, g2, wrt):
    N = x2.shape[0]
    tm = SEQ_TILE
    rows = lambda c: pl.BlockSpec((tm, c), lambda i: (i, 0))
    full = lambda r, c: pl.BlockSpec((r, c), lambda i: (0, 0))
    return pl.pallas_call(
        _merge_kernel,
        grid=(N // tm,),
        in_specs=[rows(D_MODEL), rows(ATTN_WIDTH), rows(LRU_WIDTH), full(1, D_MODEL), full(D_MODEL, 2 * D_MODEL),
                  full(ATTN_WIDTH, D_MODEL), full(LRU_WIDTH, D_MODEL), full(D_MODEL, D_MODEL), full(1, D_MODEL),
                  full(D_MODEL, 2 * ROUTE_LANES)],
        out_specs=[rows(D_MODEL), pl.BlockSpec((tm * PACK_SUB, LANES), lambda i: (i, 0)), rows(ROUTE_LANES),
                   full(1, ROUTE_LANES)],
        out_shape=[jax.ShapeDtypeStruct((N, D_MODEL), F32), jax.ShapeDtypeStruct((N * PACK_SUB, LANES), jnp.uint32),
                   jax.ShapeDtypeStruct((N, ROUTE_LANES), F32), jax.ShapeDtypeStruct((1, ROUTE_LANES), F32)],
        scratch_shapes=[pltpu.VMEM((1, ROUTE_LANES), F32), pltpu.VMEM((MERGE_CHAIN, MERGE_CHAIN), BF16)],
        compiler_params=pltpu.CompilerParams(dimension_semantics=("arbitrary",), vmem_limit_bytes=VMEM_LIMIT),
        name="merge_route",
    )(x2, attn, lru, g1, wg, woa, wol, wout, g2, wrt)


def _for_each_assignment(off_ref, fn):
    def group(gi, _):
        toks = [gi * DMA_GROUP + j for j in range(DMA_GROUP)]
        offs = [[off_ref[0, 0, tk * TOP_K + k] for k in range(TOP_K)] for tk in toks]
        for tk, o in zip(toks, offs):
            for k in range(TOP_K):
                fn(tk, k, pl.multiple_of(o[k], PACK_SUB))
        return 0
    lax.fori_loop(0, ROW_TILE // DMA_GROUP, group, 0)


def _dispatch_kernel(off_ref, h2p_hbm, xs_hbm, xin, in_sem, out_sem):
    tile_rows = ROW_TILE * PACK_SUB
    i = pl.program_id(0)
    n = pl.num_programs(0)
    slot = lax.rem(i, DISPATCH_SLOTS)

    def fetch(tile, s):
        return pltpu.make_async_copy(h2p_hbm.at[pl.ds(pl.multiple_of(tile * tile_rows, tile_rows), tile_rows)],
                                     xin.at[s], in_sem.at[s])

    def drain(s):
        for _ in range(TOP_K):
            pltpu.make_async_copy(xin.at[s], xs_hbm.at[pl.ds(0, tile_rows)], out_sem.at[s]).wait()

    @pl.when(i == 0)
    def _():
        fetch(0, 0).start()

    @pl.when(i + 1 < n)
    def _():
        fetch(i + 1, lax.rem(i + 1, DISPATCH_SLOTS)).start()

    fetch(i, slot).wait()

    def start(tk, k, off):
        pltpu.make_async_copy(xin.at[slot, pl.ds(pl.multiple_of(tk * PACK_SUB, PACK_SUB), PACK_SUB)],
                              xs_hbm.at[pl.ds(off, PACK_SUB)], out_sem.at[slot]).start(priority=k)

    _for_each_assignment(off_ref, start)

    @pl.when(i > 0)
    def _():
        drain(lax.rem(i + DISPATCH_SLOTS - 1, DISPATCH_SLOTS))

    @pl.when(i == n - 1)
    def _():
        drain(slot)


def _dispatch(off3, h2p):
    return pl.pallas_call(
        _dispatch_kernel,
        grid=(off3.shape[0],),
        in_specs=[pl.BlockSpec((1, 1, TOP_K * ROW_TILE), lambda i: (i, 0, 0), memory_space=pltpu.SMEM),
                  pl.BlockSpec(memory_space=pl.ANY)],
        out_specs=pl.BlockSpec(memory_space=pl.ANY),
        out_shape=jax.ShapeDtypeStruct((TOP_K * h2p.shape[0], LANES), jnp.uint32),
        scratch_shapes=[pltpu.VMEM((DISPATCH_SLOTS, ROW_TILE * PACK_SUB, LANES), jnp.uint32),
                        pltpu.SemaphoreType.DMA((DISPATCH_SLOTS,)), pltpu.SemaphoreType.DMA((DISPATCH_SLOTS,))],
        compiler_params=pltpu.CompilerParams(dimension_semantics=("arbitrary",), has_side_effects=True),
        name="moe_dispatch",
    )(off3, h2p)


def _moe_kernel(vt_ref, ve_ref, vlo_ref, vhi_ref, vnext_ref, vpar_ref, xs_ref, wg_hbm, wu_hbm, wd_hbm, ys_ref,
                wgb, wub, wdb, wgf, wuf, wdf, sem):
    tm = MOE_TILE
    v = pl.program_id(0)
    t = vt_ref[v]
    e = ve_ref[v]
    slot = vpar_ref[v]
    prev = jnp.maximum(v - 1, 0)

    def fetch(expert, s):
        return [pltpu.make_async_copy(w_hbm.at[expert], buf.at[s], sem.at[s])
                for w_hbm, buf in ((wg_hbm, wgf), (wu_hbm, wuf), (wd_hbm, wdf))]

    @pl.when(v == 0)
    def _():
        for copy in fetch(e, slot):
            copy.start()

    @pl.when((v == 0) | (ve_ref[prev] != e))
    def _():
        for copy in fetch(e, slot):
            copy.wait()
        nxt = vnext_ref[v]

        @pl.when(nxt != e)
        def _():
            for copy in fetch(nxt, 1 - slot):
                copy.start()

        wgb[...] = wgf[slot].astype(BF16)
        wub[...] = wuf[slot].astype(BF16)
        wdb[...] = wdf[slot].astype(BF16)

    xb = _unpack_rows(xs_ref, tm)
    g = _dot(xb, wgb[...])
    u = _dot(xb, wub[...])
    hmid = (g * jax.nn.sigmoid(g)) * u
    words = _pack_words(_dot(hmid.astype(BF16), wdb[...]))
    first = (v == 0) | (vt_ref[prev] != t)

    @pl.when(first)
    def _():
        for c in range(PACK_SUB):
            ys_ref[_packed_chunk(c, tm)] = words[c]

    @pl.when(jnp.logical_not(first))
    def _():
        rows = t * tm + lax.broadcasted_iota(jnp.int32, (tm, LANES), 0)
        mine = (rows >= vlo_ref[v]) & (rows < vhi_ref[v])
        for c in range(PACK_SUB):
            ys_ref[_packed_chunk(c, tm)] = jnp.where(mine, words[c], ys_ref[_packed_chunk(c, tm)])


def _moe(plan, xs, wg, wu, wd):
    tm = MOE_TILE
    n_rows = xs.shape[0] // PACK_SUB
    tile = lambda v, vt, *_: (vt[v], 0)
    hbm = pl.BlockSpec(memory_space=pl.ANY)
    grid_spec = pltpu.PrefetchScalarGridSpec(
        num_scalar_prefetch=len(plan),
        grid=(plan[0].shape[0],),
        in_specs=[pl.BlockSpec((tm * PACK_SUB, LANES), tile), hbm, hbm, hbm],
        out_specs=pl.BlockSpec((tm * PACK_SUB, LANES), tile),
        scratch_shapes=[pltpu.VMEM((D_MODEL, D_EXPERT), BF16), pltpu.VMEM((D_MODEL, D_EXPERT), BF16),
                        pltpu.VMEM((D_EXPERT, D_MODEL), BF16),
                        pltpu.VMEM((2, D_MODEL, D_EXPERT), F32), pltpu.VMEM((2, D_MODEL, D_EXPERT), F32),
                        pltpu.VMEM((2, D_EXPERT, D_MODEL), F32), pltpu.SemaphoreType.DMA((2,))],
    )
    return pl.pallas_call(
        _moe_kernel,
        grid_spec=grid_spec,
        out_shape=jax.ShapeDtypeStruct((n_rows * PACK_SUB, LANES), jnp.uint32),
        compiler_params=pltpu.CompilerParams(dimension_semantics=("arbitrary",), vmem_limit_bytes=VMEM_LIMIT),
        name="moe_experts",
    )(*plan, xs, wg, wu, wd)


def _combine_kernel(off_ref, offn_ref, x1_ref, route_ref, g_ref, ys_hbm, o_ref, ybuf, sem):
    tc = ROW_TILE
    i = pl.program_id(0)
    n = pl.num_programs(0)
    slot = lax.rem(i, 2)

    def gather(o_ref_, s):
        def start(tk, k, off):
            pltpu.make_async_copy(ys_hbm.at[pl.ds(off, PACK_SUB)],
                                  ybuf.at[s, pl.ds(pl.multiple_of((k * tc + tk) * PACK_SUB, PACK_SUB), PACK_SUB)],
                                  sem.at[s]).start(priority=k)
        _for_each_assignment(o_ref_, start)

    def drain(s):
        pltpu.make_async_copy(ys_hbm.at[pl.ds(0, TOP_K * tc * PACK_SUB)], ybuf.at[s], sem.at[s]).wait()

    @pl.when(i == 0)
    def _():
        gather(off_ref, 0)

    @pl.when(i + 1 < n)
    def _():
        gather(offn_ref, 1 - slot)

    drain(slot)

    route = route_ref[...]
    n_chunks = D_MODEL // LANES
    z = [x1_ref[:, c * LANES:(c + 1) * LANES] for c in range(n_chunks)]
    for k in range(TOP_K):
        wk = route[:, TOP_K + k:TOP_K + k + 1]
        yk = _unpack_chunks(ybuf.at[slot], tc, first_row=k * tc)
        z = [zc + wk * yc for zc, yc in zip(z, yk)]
    ss = sum(jnp.sum(zc * zc, axis=-1, keepdims=True) for zc in z)
    inv = lax.rsqrt(ss * (1.0 / D_MODEL) + NORM_EPS)
    for c in range(n_chunks):
        o_ref[:, c * LANES:(c + 1) * LANES] = z[c] * inv * g_ref[:, c * LANES:(c + 1) * LANES]


def _combine(off3, x1, route, g, ys):
    N = x1.shape[0]
    tc = ROW_TILE
    nt = N // tc
    idx = lambda f: pl.BlockSpec((1, 1, TOP_K * tc), f, memory_space=pltpu.SMEM)
    return pl.pallas_call(
        _combine_kernel,
        grid=(nt,),
        in_specs=[idx(lambda i: (i, 0, 0)),
                  idx(lambda i: (jnp.minimum(i + 1, nt - 1), 0, 0)),
                  pl.BlockSpec((tc, D_MODEL), lambda i: (i, 0)),
                  pl.BlockSpec((tc, ROUTE_LANES), lambda i: (i, 0)),
                  pl.BlockSpec((1, D_MODEL), lambda i: (0, 0)),
                  pl.BlockSpec(memory_space=pl.ANY)],
        out_specs=pl.BlockSpec((tc, D_MODEL), lambda i: (i, 0)),
        out_shape=jax.ShapeDtypeStruct((N, D_MODEL), F32),
        scratch_shapes=[pltpu.VMEM((2, TOP_K * tc * PACK_SUB, LANES), jnp.uint32), pltpu.SemaphoreType.DMA((2,))],
        compiler_params=pltpu.CompilerParams(dimension_semantics=("arbitrary",), vmem_limit_bytes=VMEM_LIMIT),
        name="combine_norm",
    )(off3, off3, x1, route, g, ys)


def _block_diag(w):
    nb, c, _ = w.shape
    eye = jnp.eye(nb, dtype=w.dtype)
    return (eye[:, None, :, None] * w[:, :, None, :]).reshape(nb * c, nb * c)


def _visit_plan(counts, n_rows):
    tm = MOE_TILE
    n_tiles = n_rows // tm
    n_visits = n_tiles + N_EXPERTS - 1
    cnt = counts.astype(jnp.int32)
    lanes = jnp.arange(ROUTE_LANES, dtype=jnp.int32)
    upto = (lanes[None, :] <= lanes[:, None]).astype(jnp.int32)
    ends = upto @ cnt
    starts = ends - cnt
    first_tile = starts // tm
    n_vis = jnp.where(cnt > 0, (ends - 1) // tm - first_tile + 1, 0)
    v_end = upto @ n_vis
    v_start = v_end - n_vis
    total = v_end[-1]
    v = jnp.arange(n_visits, dtype=jnp.int32)
    vc = jnp.minimum(v, total - 1)
    own = ((vc[:, None] >= v_start[None, :]) & (vc[:, None] < v_end[None, :])).astype(jnp.int32)
    pick = lambda per_lane: own @ per_lane
    tile = jnp.maximum(pick(first_tile - v_start) + vc, 0)
    lo = jnp.maximum(pick(starts), tile * tm)
    hi = jnp.minimum(pick(ends), (tile + 1) * tm)
    valid = v < total
    to_expert = lambda lane: jnp.maximum(lane - EXPERT_LANE0, 0)
    used = cnt > 0
    later = jnp.where((lanes[None, :] > lanes[:, None]) & used[None, :], lanes[None, :], ROUTE_LANES)
    nxt = jnp.min(later, axis=1)
    nxt = jnp.where(nxt == ROUTE_LANES, lanes, nxt)
    parity = (upto @ used.astype(jnp.int32) - 1) % 2
    plan = (tile, to_expert(pick(lanes)), jnp.where(valid, lo, 0), jnp.where(valid, hi, 0),
            to_expert(pick(nxt)), pick(parity))
    return starts, plan


def kernel(x, norm_mix_g, w_in, lambda_qk, subln_g, conv_w, conv_b, w_r, b_r, w_i, b_i, lru_lambda, w_o_attn, w_o_lru, w_out, norm_ffn_g, w_group, w_expert_router, w_gate, w_up, w_down, final_norm_g):
    B, S, D = x.shape
    N = B * S
    nt = S // SEQ_TILE
    depth = norm_mix_g.shape[0]
    assert depth == 1 and D == D_MODEL and S % SEQ_TILE == 0 and S % LRU_TILE == 0
    assert N % ROW_TILE == 0 and (N * TOP_K) % MOE_TILE == 0
    l = 0
    row = lambda v: v.reshape(1, -1).astype(F32)

    x2 = x.reshape(N, D)
    w_in_l = w_in[l]
    qT, k, vT, xr, yr, kn = _inproj(x2, row(norm_mix_g[l]), w_in_l[:, :PROJ_COLS].astype(BF16), B, S)

    attn = _attn(kn[:, 0, :2 * N_HEADS].reshape(-1), qT, k.reshape(B, nt, SEQ_TILE, N_HEADS * K_COLS), vT,
                 lambda_qk[l].reshape(4, HEAD_DIM).astype(F32), row(subln_g[l]), B, S)

    lru = _lru(xr.reshape(B, S, LRU_WIDTH), yr.reshape(B, S, LRU_WIDTH), conv_w[l].astype(F32), row(conv_b[l]),
               _block_diag(w_r[l]).astype(BF16), _block_diag(w_i[l]).astype(BF16), row(b_r[l]), row(b_i[l]),
               row(lru_lambda[l]), B, S)

    w_route = jnp.concatenate(
        [w_group[l], jnp.transpose(w_expert_router[l], (1, 0, 2)).reshape(D, N_EXPERTS),
         jnp.zeros((D, ROUTE_LANES - N_GROUPS - N_EXPERTS), F32)], axis=1).astype(F32)
    w_route_hi = w_route.astype(BF16)
    w_route = jnp.concatenate([w_route_hi, (w_route - w_route_hi.astype(F32)).astype(BF16)], axis=1)
    x1, h2p, route, counts = _merge(x2, attn.reshape(N, ATTN_WIDTH), lru.reshape(N, LRU_WIDTH), row(norm_mix_g[l]),
                                    w_in_l[:, PROJ_COLS:].astype(BF16), w_o_attn[l].astype(BF16),
                                    w_o_lru[l].astype(BF16), w_out[l].astype(BF16), row(norm_ffn_g[l]), w_route)

    starts, plan = _visit_plan(counts[0], N * TOP_K)
    lane = route[:, 0:TOP_K].astype(jnp.int32)
    rank = route[:, 2 * TOP_K:3 * TOP_K].astype(jnp.int32)
    lane_ids = jnp.arange(ROUTE_LANES, dtype=jnp.int32)
    start_of = jnp.sum(jnp.where(lane[..., None] == lane_ids, starts, 0), axis=-1)
    off3 = ((start_of + rank) * PACK_SUB).reshape(N // ROW_TILE, 1, TOP_K * ROW_TILE)

    xs = _dispatch(off3, h2p)
    ys = _moe(plan, xs, w_gate[l], w_up[l], w_down[l])
    out = _combine(off3, x1, route, row(final_norm_g), ys)
    return out.reshape(B, S, D)
```

```python
import functools
import math

import jax
import jax.numpy as jnp
from jax import lax
from jax.experimental import pallas as pl
from jax.experimental.pallas import tpu as pltpu

F32 = jnp.float32
BF16 = jnp.bfloat16

D_MODEL = 1024
N_HEADS = 4
HEAD_DIM = 64
V_DIM = 2 * HEAD_DIM
ATTN_WIDTH = N_HEADS * V_DIM
V_ROWS = V_DIM + 16
K_COLS = 2 * V_DIM
POS_SPLIT = 3
POS_RADIX = 256
SKIP_MARGIN = 138.0
NORM_SLACK = 1.0201
LRU_WIDTH = D_MODEL // 2
LRU_BLOCKS = 8
CONV_W = 4
LRU_C = 8.0
N_GROUPS = 4
EXPERTS_PER_GROUP = 8
N_EXPERTS = N_GROUPS * EXPERTS_PER_GROUP
TOP_K = 2
D_EXPERT = D_MODEL // 2
NORM_EPS = 1e-6
LAM_INIT = 0.8 - 0.6 * math.exp(-0.3 * 0)

QK_COLS = N_HEADS * 2 * HEAD_DIM
PROJ_COLS = 2 * QK_COLS + ATTN_WIDTH + 2 * LRU_WIDTH
ROUTE_LANES = 128
ROUTE_ROWS = 64
ROUTE_FIELDS = 8
NEG_BIG = -1e30
LOG2E = math.log2(math.e)
ALIBI_SLOPES = tuple(2.0 ** (-8.0 * (h + 1) / N_HEADS) for h in range(N_HEADS))

SEQ_TILE = 512
LRU_TILE = 512
ROW_TILE = 256
MERGE_CHAIN = 512
MOE_TILE = 256
DMA_GROUP = 8
DISPATCH_SLOTS = 3
LANES = 128
SUBLANES = 8
PACK_SUB = D_MODEL // 2 // LANES
EXPERT_LANE0 = N_GROUPS
VMEM_LIMIT = 48 * 1024 * 1024


def _rms(x, g):
    return x * lax.rsqrt(jnp.mean(x * x, axis=-1, keepdims=True) + NORM_EPS) * g


def _dot(a, b):
    return jnp.dot(a, b, preferred_element_type=F32)


def _inproj_kernel(x_ref, g_ref, w_ref, qT_ref, k_ref, vT_ref, xr_ref, yr_ref, kn_ref):
    hb = _rms(x_ref[...], g_ref[...]).astype(BF16)

    def proj(lo, hi):
        return _dot(hb, w_ref[:, lo:hi])

    q = proj(0, QK_COLS) * (HEAD_DIM ** -0.5 * LOG2E)
    for h in range(N_HEADS):
        qT_ref[h] = q[:, h * V_DIM:(h + 1) * V_DIM].T.astype(BF16)
    k = proj(QK_COLS, 2 * QK_COLS).astype(BF16)
    r = lax.broadcasted_iota(jnp.int32, (k.shape[0], K_COLS - V_DIM), 0)
    lane = lax.broadcasted_iota(jnp.int32, r.shape, 1)
    a = r // POS_RADIX * POS_RADIX
    feat = jnp.where(lane < POS_SPLIT, a, jnp.where(lane < 2 * POS_SPLIT, r - a, 0)).astype(F32).astype(BF16)
    for h in range(N_HEADS):
        k_ref[:, h * K_COLS:h * K_COLS + V_DIM] = k[:, h * V_DIM:(h + 1) * V_DIM]
        k_ref[:, h * K_COLS + V_DIM:(h + 1) * K_COLS] = feat
    seg = lax.broadcasted_iota(jnp.int32, (QK_COLS, LANES), 0) // HEAD_DIM
    pick = jnp.where(seg == lax.broadcasted_iota(jnp.int32, (QK_COLS, LANES), 1), 1.0, 0.0).astype(BF16)
    row_n2 = _dot(jnp.square(k.astype(F32)).astype(BF16), pick)
    kn_ref[...] = jnp.max(row_n2, axis=0, keepdims=True)
    v = proj(2 * QK_COLS, 2 * QK_COLS + ATTN_WIDTH)
    for h in range(N_HEADS):
        vT_ref[h, :V_DIM, :] = v[:, h * V_DIM:(h + 1) * V_DIM].T.astype(BF16)
        pad_row = lax.broadcasted_iota(jnp.int32, (V_ROWS - V_DIM, v.shape[0]), 0)
        vT_ref[h, V_DIM:, :] = jnp.where(pad_row == 0, 1.0, 0.0).astype(BF16)
    c0 = 2 * QK_COLS + ATTN_WIDTH
    xr_ref[...] = proj(c0, c0 + LRU_WIDTH)
    yr_ref[...] = proj(c0 + LRU_WIDTH, c0 + 2 * LRU_WIDTH)


def _inproj(x2, g, w, B, S):
    N = B * S
    tm = SEQ_TILE
    nt = S // tm
    tile5 = pl.BlockSpec((None, N_HEADS, None, V_DIM, tm), lambda i: (i // nt, 0, i % nt, 0, 0))
    rows = lambda c: pl.BlockSpec((tm, c), lambda i: (i, 0))
    return pl.pallas_call(
        _inproj_kernel,
        grid=(N // tm,),
        in_specs=[rows(D_MODEL),
                  pl.BlockSpec((1, D_MODEL), lambda i: (0, 0)),
                  pl.BlockSpec((D_MODEL, PROJ_COLS), lambda i: (0, 0))],
        out_specs=[tile5, rows(N_HEADS * K_COLS),
                   pl.BlockSpec((None, N_HEADS, None, V_ROWS, tm), lambda i: (i // nt, 0, i % nt, 0, 0)),
                   rows(LRU_WIDTH), rows(LRU_WIDTH),
                   pl.BlockSpec((None, 1, LANES), lambda i: (i, 0, 0))],
        out_shape=[jax.ShapeDtypeStruct((B, N_HEADS, nt, V_DIM, tm), BF16),
                   jax.ShapeDtypeStruct((N, N_HEADS * K_COLS), BF16),
                   jax.ShapeDtypeStruct((B, N_HEADS, nt, V_ROWS, tm), BF16),
                   jax.ShapeDtypeStruct((N, LRU_WIDTH), F32),
                   jax.ShapeDtypeStruct((N, LRU_WIDTH), F32),
                   jax.ShapeDtypeStruct((N // tm, 1, LANES), F32)],
        compiler_params=pltpu.CompilerParams(dimension_semantics=("parallel",),
                                             vmem_limit_bytes=VMEM_LIMIT),
        name="inproj",
    )(x2, g, w)


def _attn_kernel(kn_ref, qT_ref, k_ref, vT_ref, lam_ref, g_ref, o_ref, mask_ref, acc_ref, sa_ref, sb_ref, pa_ref,
                 pb_ref):
    t = SEQ_TILE
    b = pl.program_id(0)
    h = pl.program_id(1)
    i = pl.program_id(2)
    slope = LOG2E * jnp.where(h == 0, ALIBI_SLOPES[0], jnp.where(h == 1, ALIBI_SLOPES[1],
                              jnp.where(h == 2, ALIBI_SLOPES[2], ALIBI_SLOPES[3]))).astype(F32)

    @pl.when(i == 0)
    def _():
        r = lax.broadcasted_iota(jnp.int32, (t, t), 0)
        c = lax.broadcasted_iota(jnp.int32, (t, t), 1)
        mask_ref[...] = jnp.where(r <= c, 0.0, NEG_BIG)

    qf = qT_ref[...].astype(F32)
    row = lax.broadcasted_iota(jnp.int32, qf.shape, 0)
    sl = jnp.full(qf.shape, slope, F32)
    hi = sl.astype(BF16).astype(F32)
    mid = (sl - hi).astype(BF16).astype(F32)
    lo = (sl - hi - mid).astype(BF16).astype(F32)
    piece = jnp.where(row % 3 == 0, hi, jnp.where(row % 3 == 1, mid, lo))
    srows = jnp.where(row < 2 * POS_SPLIT, piece, 0.0).astype(BF16)
    qs = tuple(jnp.concatenate([jnp.where(sel, qf, 0.0).astype(BF16), srows], axis=0)
               for sel in (row < HEAD_DIM, row >= HEAD_DIM))
    acc_ref[...] = jnp.zeros_like(acc_ref)
    pb_ref[...] = jnp.zeros_like(pb_ref)

    first_off = [0]

    def key_tile(tau):
        return jnp.where(tau <= 0, i, first_off[0] + tau - 1)

    def stage_q(tau, s_ref, diagonal=False):
        kt = k_ref[key_tile(tau)]
        tile_max = []
        for mi in range(2):
            s = _dot(kt, qs[mi])
            if diagonal:
                s = s + mask_ref[...]
            s_ref[mi] = s
            tile_max.append(jnp.max(s, axis=0, keepdims=True))
        return tuple(tile_max)

    def stage_s(tau, s_ref, p_ref, ms, tile_max):
        cj = slope * (key_tile(tau) * t).astype(F32)
        m_out, alphas = [], []
        for mi in range(2):
            m_new = jnp.maximum(ms[mi], tile_max[mi] + cj)
            alphas.append(jnp.exp2(ms[mi] - m_new))
            p_ref[mi] = jnp.exp2(s_ref[mi] - (m_new - cj)).astype(BF16)
            m_out.append(m_new)
        return tuple(m_out), tuple(alphas)

    def stage_v(tau, p_ref, alphas):
        vt = vT_ref[key_tile(tau)]
        for mi in range(2):
            acc_ref[mi] = alphas[mi] * acc_ref[mi] + _dot(vt, p_ref[mi])

    def body(jj, carry):
        ms, alphas, tmax = carry[:2], carry[2:4], carry[4:]
        tau = 2 * jj
        tmax_b = stage_q(tau + 1, sb_ref)
        ms, alphas_a = stage_s(tau, sa_ref, pa_ref, ms, tmax)
        stage_v(tau - 1, pb_ref, alphas)
        tmax_a = stage_q(tau + 2, sa_ref)
        ms, alphas_b = stage_s(tau + 1, sb_ref, pb_ref, ms, tmax_b)
        stage_v(tau, pa_ref, alphas_a)
        return ms + alphas_b + tmax_a

    def finalize():
        lp = lam_ref[...]
        s1 = jnp.sum(lp[0:1] * lp[1:2], axis=-1, keepdims=True)
        s2 = jnp.sum(lp[2:3] * lp[3:4], axis=-1, keepdims=True)
        lam = jnp.exp(s1) - jnp.exp(s2) + LAM_INIT
        norm = [acc_ref[mi, :V_DIM, :] * (1.0 / acc_ref[mi, V_DIM:V_DIM + 1, :]) for mi in range(2)]
        oT = norm[0] - lam * norm[1]
        o = _rms(oT.T, g_ref[...]) * (1.0 - LAM_INIT)
        o_ref[...] = o.astype(BF16)

    m_init = jnp.full((1, t), NEG_BIG, F32)
    one = jnp.ones((1, t), F32)
    tmax0 = stage_q(0, sa_ref, diagonal=True)

    cj0 = slope * (i * t).astype(F32)
    n_skip = None
    for mi in range(2):
        qsq = jnp.square(qf[mi * HEAD_DIM:(mi + 1) * HEAD_DIM, :])
        qn2 = jnp.max(jnp.sum(qsq, axis=0, keepdims=True), axis=1, keepdims=True)
        m_low = jnp.min(tmax0[mi], axis=1, keepdims=True) + cj0
        qk = jnp.sqrt(qn2 * (kn_ref[(b * N_HEADS + h) * 2 + mi] * NORM_SLACK))
        count = jnp.ceil((m_low - SKIP_MARGIN - slope * (t - 1) - qk) / (slope * t))
        n_skip = count if n_skip is None else jnp.minimum(n_skip, count)
    first_off[0] = jnp.clip(n_skip, 0.0, i.astype(F32)).astype(jnp.int32)[0, 0]
    n_off = i - first_off[0]

    n_main = jnp.right_shift(n_off, 1)
    fin = lax.fori_loop(0, n_main, body, (m_init, m_init, one, one) + tmax0)
    ms, alphas, tmax = fin[:2], fin[2:4], fin[4:]
    tau = 2 * n_main
    odd_tiles = tau == n_off

    @pl.when(odd_tiles)
    def _():
        _, alphas_a = stage_s(tau, sa_ref, pa_ref, ms, tmax)
        stage_v(tau - 1, pb_ref, alphas)
        stage_v(tau, pa_ref, alphas_a)
        finalize()

    @pl.when(jnp.logical_not(odd_tiles))
    def _():
        tmax_b = stage_q(tau + 1, sb_ref)
        ms_a, alphas_a = stage_s(tau, sa_ref, pa_ref, ms, tmax)
        stage_v(tau - 1, pb_ref, alphas)
        _, alphas_b = stage_s(tau + 1, sb_ref, pb_ref, ms_a, tmax_b)
        stage_v(tau, pa_ref, alphas_a)
        stage_v(tau + 1, pb_ref, alphas_b)
        finalize()


def _attn(kn, qT, k4, vT, lam, g, B, S):
    t = SEQ_TILE
    nt = S // t
    grid_spec = pltpu.PrefetchScalarGridSpec(
        num_scalar_prefetch=1,
        grid=(B, N_HEADS, nt),
        in_specs=[pl.BlockSpec((None, None, None, V_DIM, t), lambda b, h, i, kn: (b, h, i, 0, 0)),
                  pl.BlockSpec((None, nt, t, K_COLS), lambda b, h, i, kn: (b, 0, 0, h)),
                  pl.BlockSpec((None, None, nt, V_ROWS, t), lambda b, h, i, kn: (b, h, 0, 0, 0)),
                  pl.BlockSpec((4, HEAD_DIM), lambda b, h, i, kn: (0, 0)),
                  pl.BlockSpec((1, V_DIM), lambda b, h, i, kn: (0, 0))],
        out_specs=pl.BlockSpec((None, t, V_DIM), lambda b, h, i, kn: (b, i, h)),
        scratch_shapes=[pltpu.VMEM((t, t), F32), pltpu.VMEM((2, V_ROWS, t), F32),
                        pltpu.VMEM((2, t, t), F32), pltpu.VMEM((2, t, t), F32),
                        pltpu.VMEM((2, t, t), BF16), pltpu.VMEM((2, t, t), BF16)],
    )
    return pl.pallas_call(
        _attn_kernel,
        grid_spec=grid_spec,
        out_shape=jax.ShapeDtypeStruct((B, S, ATTN_WIDTH), BF16),
        compiler_params=pltpu.CompilerParams(dimension_semantics=("parallel", "arbitrary", "arbitrary"),
                                             vmem_limit_bytes=VMEM_LIMIT),
        name="diff_attn",
    )(kn, qT, k4, vT, lam, g)


def _lru_kernel(xr_ref, yr_ref, cw_ref, cb_ref, wr_ref, wi_ref, br_ref, bi_ref, lam_ref, o_ref, xbuf, hc):
    T = LRU_TILE
    ti = pl.program_id(1)

    @pl.when(ti == 0)
    def _():
        xbuf[0:8] = jnp.zeros((8, LRU_WIDTH), F32)
        hc[...] = jnp.zeros_like(hc)

    x = xr_ref[...]
    xbuf[8:8 + T] = x
    cw = cw_ref[...]
    xc = cb_ref[...] + cw[3:4] * x
    for j in range(CONV_W - 1):
        xc = xc + cw[j:j + 1] * xbuf[5 + j:5 + j + T]
    xbuf[0:8] = x[T - 8:T]

    xb = xc.astype(BF16)
    r = jax.nn.sigmoid(_dot(xb, wr_ref[...]) + br_ref[...])
    ig = jax.nn.sigmoid(_dot(xb, wi_ref[...]) + bi_ref[...])
    z = -lam_ref[...]
    softplus = jnp.maximum(z, 0.0) + jnp.log1p(jnp.exp(-jnp.abs(z)))
    la = -LRU_C * r * softplus
    a = jnp.exp(la)
    m2 = -jnp.tanh(la) * (a * a + 1.0)
    mult = jnp.where(m2 > 0.0, m2 * lax.rsqrt(m2), 0.0)
    row = lax.broadcasted_iota(jnp.int32, (T, LRU_WIDTH), 0)
    mult = jnp.where((row == 0) & (ti == 0), 1.0, mult)
    u = (xc * ig) * mult

    nb = T // SUBLANES
    a3 = a.reshape(nb, SUBLANES, LRU_WIDTH)
    u3 = u.reshape(nb, SUBLANES, LRU_WIDTH)
    sub = lax.broadcasted_iota(jnp.int32, a3.shape, 1)
    d = 1
    while d < SUBLANES:
        valid = sub >= d
        u3 = jnp.where(valid, a3 * pltpu.roll(u3, d, 1) + u3, u3)
        a3 = jnp.where(valid, a3 * pltpu.roll(a3, d, 1), a3)
        d *= 2
    h = hc[...]
    blocks = []
    for b in range(nb):
        hb = u3[b] + a3[b] * h
        blocks.append(hb)
        h = hb[SUBLANES - 1:SUBLANES]
    hfull = jnp.concatenate(blocks, axis=0)
    hc[...] = h
    y = yr_ref[...]
    gelu = 0.5 * y * (1.0 + jnp.tanh(0.7978845608028654 * (y + 0.044715 * (y * y * y))))
    o_ref[...] = (hfull * gelu).astype(BF16)


def _lru(xr, yr, cw, cb, wr, wi, br, bi, lam, B, S):
    T = LRU_TILE
    seq = pl.BlockSpec((None, T, LRU_WIDTH), lambda b, t: (b, t, 0))
    full = lambda r, c: pl.BlockSpec((r, c), lambda b, t: (0, 0))
    return pl.pallas_call(
        _lru_kernel,
        grid=(B, S // T),
        in_specs=[seq, seq, full(CONV_W, LRU_WIDTH), full(1, LRU_WIDTH), full(LRU_WIDTH, LRU_WIDTH),
                  full(LRU_WIDTH, LRU_WIDTH), full(1, LRU_WIDTH), full(1, LRU_WIDTH), full(1, LRU_WIDTH)],
        out_specs=seq,
        out_shape=jax.ShapeDtypeStruct((B, S, LRU_WIDTH), BF16),
        scratch_shapes=[pltpu.VMEM((T + 8, LRU_WIDTH), F32), pltpu.VMEM((1, LRU_WIDTH), F32)],
        compiler_params=pltpu.CompilerParams(dimension_semantics=("arbitrary", "arbitrary"),
                                             vmem_limit_bytes=VMEM_LIMIT),
        name="rg_lru",
    )(xr, yr, cw, cb, wr, wi, br, bi, lam)


def _pack_words(v):
    bits = pltpu.bitcast(v.astype(BF16).astype(F32), jnp.uint32)
    half = D_MODEL // 2
    packed = (bits[:, :half] >> 16) | (bits[:, half:] & jnp.uint32(0xFFFF0000))
    return [packed[:, c * LANES:(c + 1) * LANES] for c in range(PACK_SUB)]


def _packed_chunk(c, rows, first_row=0):
    return (pl.ds(first_row * PACK_SUB + c, rows, stride=PACK_SUB), slice(None))


def _pack_rows(v, out_ref, first_row=0):
    for c, words in enumerate(_pack_words(v)):
        out_ref[_packed_chunk(c, v.shape[0], first_row)] = words


def _unpack_chunks(in_ref, rows, first_row=0):
    lo, hi = [], []
    for c in range(PACK_SUB):
        w = in_ref[_packed_chunk(c, rows, first_row)]
        lo.append(pltpu.bitcast(w << 16, F32))
        hi.append(pltpu.bitcast(w & jnp.uint32(0xFFFF0000), F32))
    return lo + hi


def _unpack_rows(in_ref, rows):
    return jnp.concatenate(_unpack_chunks(in_ref, rows), axis=1).astype(BF16)


def _merge_kernel(x_ref, attn_ref, lru_ref, g1_ref, wg_ref, woa_ref, wol_ref, wout_ref, g2_ref, wrt_ref,
                  x1_ref, h2p_ref, route_ref, counts_ref, cnt, tri):
    tm = MERGE_CHAIN

    @pl.when(pl.program_id(0) == 0)
    def _():
        cnt[...] = jnp.zeros_like(cnt)
        r = lax.broadcasted_iota(jnp.int32, (tm, tm), 0)
        c = lax.broadcasted_iota(jnp.int32, (tm, tm), 1)
        tri[...] = jnp.where(r < c, 1.0, 0.0).astype(BF16)

    counts = cnt[...]
    for r0 in range(0, SEQ_TILE, tm):
        counts = _merge_chain(r0, tm, counts, x_ref, attn_ref, lru_ref, g1_ref, wg_ref, woa_ref, wol_ref, wout_ref,
                              g2_ref, wrt_ref, x1_ref, h2p_ref, route_ref, tri)
    cnt[...] = counts
    counts_ref[...] = counts


def _merge_chain(r0, tm, counts, x_ref, attn_ref, lru_ref, g1_ref, wg_ref, woa_ref, wol_ref, wout_ref, g2_ref, wrt_ref,
                 x1_ref, h2p_ref, route_ref, tri):
    rs = slice(r0, r0 + tm)
    x = x_ref[rs, :]
    hb = _rms(x, g1_ref[...]).astype(BF16)
    gates = 0.5 * jnp.tanh(0.5 * _dot(hb, wg_ref[...])) + 0.5
    merged = gates[:, :D_MODEL] * _dot(attn_ref[rs, :], woa_ref[...]) + gates[:, D_MODEL:] * _dot(lru_ref[rs, :], wol_ref[...])
    x1 = x + _dot(merged.astype(BF16), wout_ref[...])
    x1_ref[rs, :] = x1
    h2 = _rms(x1, g2_ref[...])
    _pack_rows(h2, h2p_ref, first_row=r0)

    h_hi = h2.astype(BF16)
    h_lo = (h2 - h_hi.astype(F32)).astype(BF16)
    wrt = wrt_ref[...]
    nt_dims = (((1,), (1,)), ((), ()))
    hh = lax.dot_general(wrt, h_hi, nt_dims, preferred_element_type=F32)
    lo_pass = lax.dot_general(wrt[:ROUTE_LANES], h_lo, nt_dims, preferred_element_type=F32)
    logits = (hh[:ROUTE_LANES] + hh[ROUTE_LANES:] + lo_pass)[:ROUTE_ROWS]
    row = lax.broadcasted_iota(jnp.int32, logits.shape, 0)
    big = jnp.int32(1 << 20)

    def first_argmax(v):
        m = jnp.max(v, axis=0, keepdims=True)
        return m, jnp.min(jnp.where(v == m, row, big), axis=0, keepdims=True)

    gmask = row < N_GROUPS
    gmax, gidx = first_argmax(jnp.where(gmask, logits, -jnp.inf))
    gsum = jnp.sum(jnp.where(gmask, jnp.exp(logits - gmax), 0.0), axis=0, keepdims=True)
    g_w = 1.0 / gsum
    lo = N_GROUPS + EXPERTS_PER_GROUP * gidx
    el = jnp.where((row >= lo) & (row < lo + EXPERTS_PER_GROUP), logits, -jnp.inf)
    m1, i1 = first_argmax(el)
    m2, i2 = first_argmax(jnp.where(row == i1, -jnp.inf, el))
    rr = jnp.exp(m2 - m1)
    w1 = g_w / (1.0 + rr)
    w2 = g_w * rr / (1.0 + rr)
    oh1 = row == i1
    oh2 = row == i2
    oh = jnp.where(oh1 | oh2, 1.0, 0.0)
    before = _dot(oh.astype(BF16), tri[...]) + counts
    r1 = jnp.sum(jnp.where(oh1, before, 0.0), axis=0, keepdims=True)
    r2 = jnp.sum(jnp.where(oh2, before, 0.0), axis=0, keepdims=True)
    vals = (i1.astype(F32), i2.astype(F32), w1, w2, r1, r2, jnp.zeros_like(w1), jnp.zeros_like(w1))
    for k, v in enumerate(vals):
        route_ref[k:k + 1, rs] = v
    return counts + jnp.sum(oh, axis=1, keepdims=True)


def _merge(x2, attn, lru, g1, wg, woa, wol, wout, g2, wrt):
    N = x2.shape[0]
    tm = SEQ_TILE
    rows = lambda c: pl.BlockSpec((tm, c), lambda i: (i, 0))
    full = lambda r, c: pl.BlockSpec((r, c), lambda i: (0, 0))
    return pl.pallas_call(
        _merge_kernel,
        grid=(N // tm,),
        in_specs=[rows(D_MODEL), rows(ATTN_WIDTH), rows(LRU_WIDTH), full(1, D_MODEL), full(D_MODEL, 2 * D_MODEL),
                  full(ATTN_WIDTH, D_MODEL), full(LRU_WIDTH, D_MODEL), full(D_MODEL, D_MODEL), full(1, D_MODEL),
                  full(2 * ROUTE_LANES, D_MODEL)],
        out_specs=[rows(D_MODEL), pl.BlockSpec((tm * PACK_SUB, LANES), lambda i: (i, 0)),
                   pl.BlockSpec((ROUTE_FIELDS, tm), lambda i: (0, i)), full(ROUTE_ROWS, 1)],
        out_shape=[jax.ShapeDtypeStruct((N, D_MODEL), F32), jax.ShapeDtypeStruct((N * PACK_SUB, LANES), jnp.uint32),
                   jax.ShapeDtypeStruct((ROUTE_FIELDS, N), F32), jax.ShapeDtypeStruct((ROUTE_ROWS, 1), F32)],
        scratch_shapes=[pltpu.VMEM((ROUTE_ROWS, 1), F32), pltpu.VMEM((MERGE_CHAIN, MERGE_CHAIN), BF16)],
        compiler_params=pltpu.CompilerParams(dimension_semantics=("arbitrary",), vmem_limit_bytes=VMEM_LIMIT),
        name="merge_route",
    )(x2, attn, lru, g1, wg, woa, wol, wout, g2, wrt)


def _for_each_assignment(off_ref, fn):
    def group(gi, _):
        toks = [gi * DMA_GROUP + j for j in range(DMA_GROUP)]
        offs = [[off_ref[0, 0, tk * TOP_K + k] for k in range(TOP_K)] for tk in toks]
        for tk, o in zip(toks, offs):
            for k in range(TOP_K):
                fn(tk, k, pl.multiple_of(o[k], PACK_SUB))
        return 0
    lax.fori_loop(0, ROW_TILE // DMA_GROUP, group, 0)


def _dispatch_kernel(off_ref, h2p_hbm, xs_hbm, xin, in_sem, out_sem):
    tile_rows = ROW_TILE * PACK_SUB
    i = pl.program_id(0)
    n = pl.num_programs(0)
    slot = lax.rem(i, DISPATCH_SLOTS)

    def fetch(tile, s):
        return pltpu.make_async_copy(h2p_hbm.at[pl.ds(pl.multiple_of(tile * tile_rows, tile_rows), tile_rows)],
                                     xin.at[s], in_sem.at[s])

    def drain(s):
        for _ in range(TOP_K):
            pltpu.make_async_copy(xin.at[s], xs_hbm.at[pl.ds(0, tile_rows)], out_sem.at[s]).wait()

    @pl.when(i == 0)
    def _():
        fetch(0, 0).start()

    @pl.when(i + 1 < n)
    def _():
        fetch(i + 1, lax.rem(i + 1, DISPATCH_SLOTS)).start()

    fetch(i, slot).wait()

    def start(tk, k, off):
        pltpu.make_async_copy(xin.at[slot, pl.ds(pl.multiple_of(tk * PACK_SUB, PACK_SUB), PACK_SUB)],
                              xs_hbm.at[pl.ds(off, PACK_SUB)], out_sem.at[slot]).start(priority=k)

    _for_each_assignment(off_ref, start)

    @pl.when(i > 0)
    def _():
        drain(lax.rem(i + DISPATCH_SLOTS - 1, DISPATCH_SLOTS))

    @pl.when(i == n - 1)
    def _():
        drain(slot)


def _dispatch(off3, h2p):
    return pl.pallas_call(
        _dispatch_kernel,
        grid=(off3.shape[0],),
        in_specs=[pl.BlockSpec((1, 1, TOP_K * ROW_TILE), lambda i: (i, 0, 0), memory_space=pltpu.SMEM),
                  pl.BlockSpec(memory_space=pl.ANY)],
        out_specs=pl.BlockSpec(memory_space=pl.ANY),
        out_shape=jax.ShapeDtypeStruct((TOP_K * h2p.shape[0], LANES), jnp.uint32),
        scratch_shapes=[pltpu.VMEM((DISPATCH_SLOTS, ROW_TILE * PACK_SUB, LANES), jnp.uint32),
                        pltpu.SemaphoreType.DMA((DISPATCH_SLOTS,)), pltpu.SemaphoreType.DMA((DISPATCH_SLOTS,))],
        compiler_params=pltpu.CompilerParams(dimension_semantics=("arbitrary",), has_side_effects=True),
        name="moe_dispatch",
    )(off3, h2p)


def _moe_kernel(vt_ref, ve_ref, vlo_ref, vhi_ref, vnext_ref, vpar_ref, xs_ref, wg_hbm, wu_hbm, wd_hbm, ys_ref,
                wgb, wub, wdb, wgf, wuf, wdf, sem):
    tm = MOE_TILE
    v = pl.program_id(0)
    t = vt_ref[v]
    e = ve_ref[v]
    slot = vpar_ref[v]
    prev = jnp.maximum(v - 1, 0)

    def fetch(expert, s):
        return [pltpu.make_async_copy(w_hbm.at[expert], buf.at[s], sem.at[s])
                for w_hbm, buf in ((wg_hbm, wgf), (wu_hbm, wuf), (wd_hbm, wdf))]

    @pl.when(v == 0)
    def _():
        for copy in fetch(e, slot):
            copy.start()

    @pl.when((v == 0) | (ve_ref[prev] != e))
    def _():
        for copy in fetch(e, slot):
            copy.wait()
        nxt = vnext_ref[v]

        @pl.when(nxt != e)
        def _():
            for copy in fetch(nxt, 1 - slot):
                copy.start()

        wgb[...] = wgf[slot].astype(BF16)
        wub[...] = wuf[slot].astype(BF16)
        wdb[...] = wdf[slot].astype(BF16)

    xb = _unpack_rows(xs_ref, tm)
    g = _dot(xb, wgb[...])
    u = _dot(xb, wub[...])
    hmid = (g * jax.nn.sigmoid(g)) * u
    words = _pack_words(_dot(hmid.astype(BF16), wdb[...]))
    first = (v == 0) | (vt_ref[prev] != t)

    @pl.when(first)
    def _():
        for c in range(PACK_SUB):
            ys_ref[_packed_chunk(c, tm)] = words[c]

    @pl.when(jnp.logical_not(first))
    def _():
        rows = t * tm + lax.broadcasted_iota(jnp.int32, (tm, LANES), 0)
        mine = (rows >= vlo_ref[v]) & (rows < vhi_ref[v])
        for c in range(PACK_SUB):
            ys_ref[_packed_chunk(c, tm)] = jnp.where(mine, words[c], ys_ref[_packed_chunk(c, tm)])


def _moe(plan, xs, wg, wu, wd):
    tm = MOE_TILE
    n_rows = xs.shape[0] // PACK_SUB
    tile = lambda v, vt, *_: (vt[v], 0)
    hbm = pl.BlockSpec(memory_space=pl.ANY)
    grid_spec = pltpu.PrefetchScalarGridSpec(
        num_scalar_prefetch=len(plan),
        grid=(plan[0].shape[0],),
        in_specs=[pl.BlockSpec((tm * PACK_SUB, LANES), tile), hbm, hbm, hbm],
        out_specs=pl.BlockSpec((tm * PACK_SUB, LANES), tile),
        scratch_shapes=[pltpu.VMEM((D_MODEL, D_EXPERT), BF16), pltpu.VMEM((D_MODEL, D_EXPERT), BF16),
                        pltpu.VMEM((D_EXPERT, D_MODEL), BF16),
                        pltpu.VMEM((2, D_MODEL, D_EXPERT), F32), pltpu.VMEM((2, D_MODEL, D_EXPERT), F32),
                        pltpu.VMEM((2, D_EXPERT, D_MODEL), F32), pltpu.SemaphoreType.DMA((2,))],
    )
    return pl.pallas_call(
        _moe_kernel,
        grid_spec=grid_spec,
        out_shape=jax.ShapeDtypeStruct((n_rows * PACK_SUB, LANES), jnp.uint32),
        compiler_params=pltpu.CompilerParams(dimension_semantics=("arbitrary",), vmem_limit_bytes=VMEM_LIMIT),
        name="moe_experts",
    )(*plan, xs, wg, wu, wd)


def _combine_kernel(off_ref, offn_ref, x1_ref, w_ref, g_ref, ys_hbm, o_ref, ybuf, sem):
    tc = ROW_TILE
    i = pl.program_id(0)
    n = pl.num_programs(0)
    slot = lax.rem(i, 2)

    def gather(o_ref_, s):
        def start(tk, k, off):
            pltpu.make_async_copy(ys_hbm.at[pl.ds(off, PACK_SUB)],
                                  ybuf.at[s, pl.ds(pl.multiple_of((k * tc + tk) * PACK_SUB, PACK_SUB), PACK_SUB)],
                                  sem.at[s]).start(priority=k)
        _for_each_assignment(o_ref_, start)

    def drain(s):
        pltpu.make_async_copy(ys_hbm.at[pl.ds(0, TOP_K * tc * PACK_SUB)], ybuf.at[s], sem.at[s]).wait()

    @pl.when(i == 0)
    def _():
        gather(off_ref, 0)

    @pl.when(i + 1 < n)
    def _():
        gather(offn_ref, 1 - slot)

    drain(slot)

    n_chunks = D_MODEL // LANES
    z = [x1_ref[:, c * LANES:(c + 1) * LANES] for c in range(n_chunks)]
    for k in range(TOP_K):
        wk = w_ref[:, k:k + 1]
        yk = _unpack_chunks(ybuf.at[slot], tc, first_row=k * tc)
        z = [zc + wk * yc for zc, yc in zip(z, yk)]
    ss = sum(jnp.sum(zc * zc, axis=-1, keepdims=True) for zc in z)
    inv = lax.rsqrt(ss * (1.0 / D_MODEL) + NORM_EPS)
    for c in range(n_chunks):
        o_ref[:, c * LANES:(c + 1) * LANES] = z[c] * inv * g_ref[:, c * LANES:(c + 1) * LANES]


def _combine(off3, x1, weight, g, ys):
    N = x1.shape[0]
    tc = ROW_TILE
    nt = N // tc
    idx = lambda f: pl.BlockSpec((1, 1, TOP_K * tc), f, memory_space=pltpu.SMEM)
    return pl.pallas_call(
        _combine_kernel,
        grid=(nt,),
        in_specs=[idx(lambda i: (i, 0, 0)),
                  idx(lambda i: (jnp.minimum(i + 1, nt - 1), 0, 0)),
                  pl.BlockSpec((tc, D_MODEL), lambda i: (i, 0)),
                  pl.BlockSpec((tc, TOP_K), lambda i: (i, 0)),
                  pl.BlockSpec((1, D_MODEL), lambda i: (0, 0)),
                  pl.BlockSpec(memory_space=pl.ANY)],
        out_specs=pl.BlockSpec((tc, D_MODEL), lambda i: (i, 0)),
        out_shape=jax.ShapeDtypeStruct((N, D_MODEL), F32),
        scratch_shapes=[pltpu.VMEM((2, TOP_K * tc * PACK_SUB, LANES), jnp.uint32), pltpu.SemaphoreType.DMA((2,))],
        compiler_params=pltpu.CompilerParams(dimension_semantics=("arbitrary",), vmem_limit_bytes=VMEM_LIMIT),
        name="combine_norm",
    )(off3, off3, x1, weight, g, ys)


def _block_diag(w):
    nb, c, _ = w.shape
    eye = jnp.eye(nb, dtype=w.dtype)
    return (eye[:, None, :, None] * w[:, :, None, :]).reshape(nb * c, nb * c)


def _visit_plan(counts, n_rows):
    tm = MOE_TILE
    n_tiles = n_rows // tm
    n_visits = n_tiles + N_EXPERTS - 1
    cnt = counts.astype(jnp.int32)
    lanes = jnp.arange(ROUTE_ROWS, dtype=jnp.int32)
    upto = (lanes[None, :] <= lanes[:, None]).astype(jnp.int32)
    ends = upto @ cnt
    starts = ends - cnt
    first_tile = starts // tm
    n_vis = jnp.where(cnt > 0, (ends - 1) // tm - first_tile + 1, 0)
    v_end = upto @ n_vis
    v_start = v_end - n_vis
    total = v_end[-1]
    v = jnp.arange(n_visits, dtype=jnp.int32)
    vc = jnp.minimum(v, total - 1)
    own = ((vc[:, None] >= v_start[None, :]) & (vc[:, None] < v_end[None, :])).astype(jnp.int32)
    pick = lambda per_lane: own @ per_lane
    tile = jnp.maximum(pick(first_tile - v_start) + vc, 0)
    lo = jnp.maximum(pick(starts), tile * tm)
    hi = jnp.minimum(pick(ends), (tile + 1) * tm)
    valid = v < total
    to_expert = lambda lane: jnp.maximum(lane - EXPERT_LANE0, 0)
    used = cnt > 0
    later = jnp.where((lanes[None, :] > lanes[:, None]) & used[None, :], lanes[None, :], ROUTE_ROWS)
    nxt = jnp.min(later, axis=1)
    nxt = jnp.where(nxt == ROUTE_ROWS, lanes, nxt)
    parity = (upto @ used.astype(jnp.int32) - 1) % 2
    plan = (tile, to_expert(pick(lanes)), jnp.where(valid, lo, 0), jnp.where(valid, hi, 0),
            to_expert(pick(nxt)), pick(parity))
    return starts, plan


def kernel(x, norm_mix_g, w_in, lambda_qk, subln_g, conv_w, conv_b, w_r, b_r, w_i, b_i, lru_lambda, w_o_attn, w_o_lru, w_out, norm_ffn_g, w_group, w_expert_router, w_gate, w_up, w_down, final_norm_g):
    B, S, D = x.shape
    N = B * S
    nt = S // SEQ_TILE
    depth = norm_mix_g.shape[0]
    assert depth == 1 and D == D_MODEL and S % SEQ_TILE == 0 and S % LRU_TILE == 0
    assert N % ROW_TILE == 0 and (N * TOP_K) % MOE_TILE == 0
    l = 0
    row = lambda v: v.reshape(1, -1).astype(F32)

    x2 = x.reshape(N, D)
    w_in_l = w_in[l]
    qT, k, vT, xr, yr, kn = _inproj(x2, row(norm_mix_g[l]), w_in_l[:, :PROJ_COLS].astype(BF16), B, S)

    kn_max = jnp.max(kn[:, 0, :2 * N_HEADS].reshape(B, nt, 2 * N_HEADS), axis=1)
    attn = _attn(kn_max.reshape(-1), qT, k.reshape(B, nt, SEQ_TILE, N_HEADS * K_COLS), vT,
                 lambda_qk[l].reshape(4, HEAD_DIM).astype(F32), row(subln_g[l]), B, S)

    lru = _lru(xr.reshape(B, S, LRU_WIDTH), yr.reshape(B, S, LRU_WIDTH), conv_w[l].astype(F32), row(conv_b[l]),
               _block_diag(w_r[l]).astype(BF16), _block_diag(w_i[l]).astype(BF16), row(b_r[l]), row(b_i[l]),
               row(lru_lambda[l]), B, S)

    w_route = jnp.concatenate(
        [w_group[l], jnp.transpose(w_expert_router[l], (1, 0, 2)).reshape(D, N_EXPERTS),
         jnp.zeros((D, ROUTE_LANES - N_GROUPS - N_EXPERTS), F32)], axis=1).astype(F32)
    w_route_hi = w_route.astype(BF16)
    w_route = jnp.concatenate([w_route_hi, (w_route - w_route_hi.astype(F32)).astype(BF16)], axis=1).T
    x1, h2p, route, counts = _merge(x2, attn.reshape(N, ATTN_WIDTH), lru.reshape(N, LRU_WIDTH), row(norm_mix_g[l]),
                                    w_in_l[:, PROJ_COLS:].astype(BF16), w_o_attn[l].astype(BF16),
                                    w_o_lru[l].astype(BF16), w_out[l].astype(BF16), row(norm_ffn_g[l]), w_route)

    starts, plan = _visit_plan(counts[:, 0], N * TOP_K)
    per_token = lambda rows: jnp.transpose(rows)
    lane = per_token(route[0:TOP_K]).astype(jnp.int32)
    weight = per_token(route[TOP_K:2 * TOP_K])
    rank = per_token(route[2 * TOP_K:3 * TOP_K]).astype(jnp.int32)
    lane_ids = jnp.arange(ROUTE_ROWS, dtype=jnp.int32)
    start_of = jnp.sum(jnp.where(lane[..., None] == lane_ids, starts, 0), axis=-1)
    off3 = ((start_of + rank) * PACK_SUB).reshape(N // ROW_TILE, 1, TOP_K * ROW_TILE)

    xs = _dispatch(off3, h2p)
    ys = _moe(plan, xs, w_gate[l], w_up[l], w_down[l])
    out = _combine(off3, x1, weight, row(final_norm_g), ys)
    return out.reshape(B, S, D)
```

```python
import functools
import math

import jax
import jax.numpy as jnp
from jax import lax
from jax.experimental import pallas as pl
from jax.experimental.pallas import tpu as pltpu

F32 = jnp.float32
BF16 = jnp.bfloat16

D_MODEL = 1024
N_HEADS = 4
HEAD_DIM = 64
V_DIM = 2 * HEAD_DIM
ATTN_WIDTH = N_HEADS * V_DIM
V_ROWS = V_DIM + 16
K_COLS = 2 * V_DIM
POS_SPLIT = 3
POS_RADIX = 256
SKIP_MARGIN = 138.0
NORM_SLACK = 1.0201
LRU_WIDTH = D_MODEL // 2
LRU_BLOCKS = 8
CONV_W = 4
LRU_C = 8.0
N_GROUPS = 4
EXPERTS_PER_GROUP = 8
N_EXPERTS = N_GROUPS * EXPERTS_PER_GROUP
TOP_K = 2
D_EXPERT = D_MODEL // 2
NORM_EPS = 1e-6
LAM_INIT = 0.8 - 0.6 * math.exp(-0.3 * 0)

QK_COLS = N_HEADS * 2 * HEAD_DIM
PROJ_COLS = 2 * QK_COLS + ATTN_WIDTH + 2 * LRU_WIDTH
ROUTE_LANES = 128
ROUTE_ROWS = 64
ROUTE_FIELDS = 8
NEG_BIG = -1e30
LOG2E = math.log2(math.e)
ALIBI_SLOPES = tuple(2.0 ** (-8.0 * (h + 1) / N_HEADS) for h in range(N_HEADS))

SEQ_TILE = 512
LRU_TILE = 512
ROW_TILE = 256
MERGE_CHAIN = 512
MOE_TILE = 256
DMA_GROUP = 8
DISPATCH_SLOTS = 3
LANES = 128
SUBLANES = 8
PACK_SUB = D_MODEL // 2 // LANES
EXPERT_LANE0 = N_GROUPS
VMEM_LIMIT = 48 * 1024 * 1024


def _rms(x, g):
    return x * lax.rsqrt(jnp.mean(x * x, axis=-1, keepdims=True) + NORM_EPS) * g


def _dot(a, b):
    return jnp.dot(a, b, preferred_element_type=F32)


def _inproj_kernel(x_ref, g_ref, w_ref, qT_ref, k_ref, vT_ref, xr_ref, yr_ref, kn_ref):
    hb = _rms(x_ref[...], g_ref[...]).astype(BF16)

    def proj(lo, hi):
        return _dot(hb, w_ref[:, lo:hi])

    q = proj(0, QK_COLS) * (HEAD_DIM ** -0.5 * LOG2E)
    for h in range(N_HEADS):
        qT_ref[h] = q[:, h * V_DIM:(h + 1) * V_DIM].T.astype(BF16)
    k = proj(QK_COLS, 2 * QK_COLS).astype(BF16)
    r = lax.broadcasted_iota(jnp.int32, (k.shape[0], K_COLS - V_DIM), 0)
    lane = lax.broadcasted_iota(jnp.int32, r.shape, 1)
    a = r // POS_RADIX * POS_RADIX
    feat = jnp.where(lane < POS_SPLIT, a, jnp.where(lane < 2 * POS_SPLIT, r - a, 0)).astype(F32).astype(BF16)
    for h in range(N_HEADS):
        k_ref[:, h * K_COLS:h * K_COLS + V_DIM] = k[:, h * V_DIM:(h + 1) * V_DIM]
        k_ref[:, h * K_COLS + V_DIM:(h + 1) * K_COLS] = feat
    seg = lax.broadcasted_iota(jnp.int32, (QK_COLS, LANES), 0) // HEAD_DIM
    pick = jnp.where(seg == lax.broadcasted_iota(jnp.int32, (QK_COLS, LANES), 1), 1.0, 0.0).astype(BF16)
    row_n2 = _dot(jnp.square(k.astype(F32)).astype(BF16), pick)
    kn_ref[...] = jnp.max(row_n2, axis=0, keepdims=True)
    v = proj(2 * QK_COLS, 2 * QK_COLS + ATTN_WIDTH)
    for h in range(N_HEADS):
        vT_ref[h, :V_DIM, :] = v[:, h * V_DIM:(h + 1) * V_DIM].T.astype(BF16)
        pad_row = lax.broadcasted_iota(jnp.int32, (V_ROWS - V_DIM, v.shape[0]), 0)
        vT_ref[h, V_DIM:, :] = jnp.where(pad_row == 0, 1.0, 0.0).astype(BF16)
    c0 = 2 * QK_COLS + ATTN_WIDTH
    xr_ref[...] = proj(c0, c0 + LRU_WIDTH)
    yr_ref[...] = proj(c0 + LRU_WIDTH, c0 + 2 * LRU_WIDTH)


def _inproj(x2, g, w, B, S):
    N = B * S
    tm = SEQ_TILE
    nt = S // tm
    tile5 = pl.BlockSpec((None, N_HEADS, None, V_DIM, tm), lambda i: (i // nt, 0, i % nt, 0, 0))
    rows = lambda c: pl.BlockSpec((tm, c), lambda i: (i, 0))
    return pl.pallas_call(
        _inproj_kernel,
        grid=(N // tm,),
        in_specs=[rows(D_MODEL),
                  pl.BlockSpec((1, D_MODEL), lambda i: (0, 0)),
                  pl.BlockSpec((D_MODEL, PROJ_COLS), lambda i: (0, 0))],
        out_specs=[tile5, rows(N_HEADS * K_COLS),
                   pl.BlockSpec((None, N_HEADS, None, V_ROWS, tm), lambda i: (i // nt, 0, i % nt, 0, 0)),
                   rows(LRU_WIDTH), rows(LRU_WIDTH),
                   pl.BlockSpec((None, 1, LANES), lambda i: (i, 0, 0))],
        out_shape=[jax.ShapeDtypeStruct((B, N_HEADS, nt, V_DIM, tm), BF16),
                   jax.ShapeDtypeStruct((N, N_HEADS * K_COLS), BF16),
                   jax.ShapeDtypeStruct((B, N_HEADS, nt, V_ROWS, tm), BF16),
                   jax.ShapeDtypeStruct((N, LRU_WIDTH), F32),
                   jax.ShapeDtypeStruct((N, LRU_WIDTH), F32),
                   jax.ShapeDtypeStruct((N // tm, 1, LANES), F32)],
        compiler_params=pltpu.CompilerParams(dimension_semantics=("parallel",),
                                             vmem_limit_bytes=VMEM_LIMIT),
        name="inproj",
    )(x2, g, w)


def _attn_kernel(kn_ref, qT_ref, k_ref, vT_ref, lam_ref, g_ref, o_ref, mask_ref, acc_ref, sa_ref, sb_ref, pa_ref,
                 pb_ref):
    t = SEQ_TILE
    b = pl.program_id(0)
    h = pl.program_id(1)
    i = pl.program_id(2)
    slope = LOG2E * jnp.where(h == 0, ALIBI_SLOPES[0], jnp.where(h == 1, ALIBI_SLOPES[1],
                              jnp.where(h == 2, ALIBI_SLOPES[2], ALIBI_SLOPES[3]))).astype(F32)

    @pl.when(i == 0)
    def _():
        r = lax.broadcasted_iota(jnp.int32, (t, t), 0)
        c = lax.broadcasted_iota(jnp.int32, (t, t), 1)
        mask_ref[...] = jnp.where(r <= c, 0.0, NEG_BIG)

    qf = qT_ref[...].astype(F32)
    row = lax.broadcasted_iota(jnp.int32, qf.shape, 0)
    sl = jnp.full(qf.shape, slope, F32)
    hi = sl.astype(BF16).astype(F32)
    mid = (sl - hi).astype(BF16).astype(F32)
    lo = (sl - hi - mid).astype(BF16).astype(F32)
    piece = jnp.where(row % 3 == 0, hi, jnp.where(row % 3 == 1, mid, lo))
    srows = jnp.where(row < 2 * POS_SPLIT, piece, 0.0).astype(BF16)
    qs = tuple(jnp.concatenate([jnp.where(sel, qf, 0.0).astype(BF16), srows], axis=0)
               for sel in (row < HEAD_DIM, row >= HEAD_DIM))
    acc_ref[...] = jnp.zeros_like(acc_ref)
    pb_ref[...] = jnp.zeros_like(pb_ref)

    first_off = [0]

    def key_tile(tau):
        return jnp.where(tau <= 0, i, first_off[0] + tau - 1)

    def stage_q(tau, s_ref, diagonal=False):
        kt = k_ref[key_tile(tau)]
        tile_max = []
        for mi in range(2):
            s = _dot(kt, qs[mi])
            if diagonal:
                s = s + mask_ref[...]
            s_ref[mi] = s
            tile_max.append(jnp.max(s, axis=0, keepdims=True))
        return tuple(tile_max)

    def stage_s(tau, s_ref, p_ref, ms, tile_max):
        cj = slope * (key_tile(tau) * t).astype(F32)
        m_out, alphas = [], []
        for mi in range(2):
            m_new = jnp.maximum(ms[mi], tile_max[mi] + cj)
            alphas.append(jnp.exp2(ms[mi] - m_new))
            p_ref[mi] = jnp.exp2(s_ref[mi] - (m_new - cj)).astype(BF16)
            m_out.append(m_new)
        return tuple(m_out), tuple(alphas)

    def stage_v(tau, p_ref, alphas):
        vt = vT_ref[key_tile(tau)]
        for mi in range(2):
            acc_ref[mi] = alphas[mi] * acc_ref[mi] + _dot(vt, p_ref[mi])

    def body(jj, carry):
        ms, alphas, tmax = carry[:2], carry[2:4], carry[4:]
        tau = 2 * jj
        tmax_b = stage_q(tau + 1, sb_ref)
        ms, alphas_a = stage_s(tau, sa_ref, pa_ref, ms, tmax)
        stage_v(tau - 1, pb_ref, alphas)
        tmax_a = stage_q(tau + 2, sa_ref)
        ms, alphas_b = stage_s(tau + 1, sb_ref, pb_ref, ms, tmax_b)
        stage_v(tau, pa_ref, alphas_a)
        return ms + alphas_b + tmax_a

    def finalize():
        lp = lam_ref[...]
        s1 = jnp.sum(lp[0:1] * lp[1:2], axis=-1, keepdims=True)
        s2 = jnp.sum(lp[2:3] * lp[3:4], axis=-1, keepdims=True)
        lam = jnp.exp(s1) - jnp.exp(s2) + LAM_INIT
        norm = [acc_ref[mi, :V_DIM, :] * (1.0 / acc_ref[mi, V_DIM:V_DIM + 1, :]) for mi in range(2)]
        oT = norm[0] - lam * norm[1]
        o = _rms(oT.T, g_ref[...]) * (1.0 - LAM_INIT)
        o_ref[...] = o.astype(BF16)

    m_init = jnp.full((1, t), NEG_BIG, F32)
    one = jnp.ones((1, t), F32)
    tmax0 = stage_q(0, sa_ref, diagonal=True)

    cj0 = slope * (i * t).astype(F32)
    n_skip = None
    for mi in range(2):
        qsq = jnp.square(qf[mi * HEAD_DIM:(mi + 1) * HEAD_DIM, :])
        qn2 = jnp.max(jnp.sum(qsq, axis=0, keepdims=True), axis=1, keepdims=True)
        m_low = jnp.min(tmax0[mi], axis=1, keepdims=True) + cj0
        qk = jnp.sqrt(qn2 * (kn_ref[(b * N_HEADS + h) * 2 + mi] * NORM_SLACK))
        count = jnp.ceil((m_low - SKIP_MARGIN - slope * (t - 1) - qk) / (slope * t))
        n_skip = count if n_skip is None else jnp.minimum(n_skip, count)
    first_off[0] = jnp.clip(n_skip, 0.0, i.astype(F32)).astype(jnp.int32)[0, 0]
    n_off = i - first_off[0]

    n_main = jnp.right_shift(n_off, 1)
    fin = lax.fori_loop(0, n_main, body, (m_init, m_init, one, one) + tmax0)
    ms, alphas, tmax = fin[:2], fin[2:4], fin[4:]
    tau = 2 * n_main
    odd_tiles = tau == n_off

    @pl.when(odd_tiles)
    def _():
        _, alphas_a = stage_s(tau, sa_ref, pa_ref, ms, tmax)
        stage_v(tau - 1, pb_ref, alphas)
        stage_v(tau, pa_ref, alphas_a)
        finalize()

    @pl.when(jnp.logical_not(odd_tiles))
    def _():
        tmax_b = stage_q(tau + 1, sb_ref)
        ms_a, alphas_a = stage_s(tau, sa_ref, pa_ref, ms, tmax)
        stage_v(tau - 1, pb_ref, alphas)
        _, alphas_b = stage_s(tau + 1, sb_ref, pb_ref, ms_a, tmax_b)
        stage_v(tau, pa_ref, alphas_a)
        stage_v(tau + 1, pb_ref, alphas_b)
        finalize()


def _attn(kn, qT, k4, vT, lam, g, B, S):
    t = SEQ_TILE
    nt = S // t
    grid_spec = pltpu.PrefetchScalarGridSpec(
        num_scalar_prefetch=1,
        grid=(B, N_HEADS, nt),
        in_specs=[pl.BlockSpec((None, None, None, V_DIM, t), lambda b, h, i, kn: (b, h, i, 0, 0)),
                  pl.BlockSpec((None, nt, t, K_COLS), lambda b, h, i, kn: (b, 0, 0, h)),
                  pl.BlockSpec((None, None, nt, V_ROWS, t), lambda b, h, i, kn: (b, h, 0, 0, 0)),
                  pl.BlockSpec((4, HEAD_DIM), lambda b, h, i, kn: (0, 0)),
                  pl.BlockSpec((1, V_DIM), lambda b, h, i, kn: (0, 0))],
        out_specs=pl.BlockSpec((None, t, V_DIM), lambda b, h, i, kn: (b, i, h)),
        scratch_shapes=[pltpu.VMEM((t, t), F32), pltpu.VMEM((2, V_ROWS, t), F32),
                        pltpu.VMEM((2, t, t), F32), pltpu.VMEM((2, t, t), F32),
                        pltpu.VMEM((2, t, t), BF16), pltpu.VMEM((2, t, t), BF16)],
    )
    return pl.pallas_call(
        _attn_kernel,
        grid_spec=grid_spec,
        out_shape=jax.ShapeDtypeStruct((B, S, ATTN_WIDTH), BF16),
        compiler_params=pltpu.CompilerParams(dimension_semantics=("parallel", "arbitrary", "arbitrary"),
                                             vmem_limit_bytes=VMEM_LIMIT),
        name="diff_attn",
    )(kn, qT, k4, vT, lam, g)


def _lru_kernel(xr_ref, yr_ref, cw_ref, cb_ref, wr_ref, wi_ref, br_ref, bi_ref, lam_ref, o_ref, xbuf, hc):
    T = LRU_TILE
    ti = pl.program_id(1)

    @pl.when(ti == 0)
    def _():
        xbuf[0:8] = jnp.zeros((8, LRU_WIDTH), F32)
        hc[...] = jnp.zeros_like(hc)

    x = xr_ref[...]
    xbuf[8:8 + T] = x
    cw = cw_ref[...]
    xc = cb_ref[...] + cw[3:4] * x
    for j in range(CONV_W - 1):
        xc = xc + cw[j:j + 1] * xbuf[5 + j:5 + j + T]
    xbuf[0:8] = x[T - 8:T]

    xb = xc.astype(BF16)
    r = jax.nn.sigmoid(_dot(xb, wr_ref[...]) + br_ref[...])
    ig = jax.nn.sigmoid(_dot(xb, wi_ref[...]) + bi_ref[...])
    z = -lam_ref[...]
    softplus = jnp.maximum(z, 0.0) + jnp.log1p(jnp.exp(-jnp.abs(z)))
    la = -LRU_C * r * softplus
    a = jnp.exp(la)
    m2 = -jnp.tanh(la) * (a * a + 1.0)
    mult = jnp.where(m2 > 0.0, m2 * lax.rsqrt(m2), 0.0)
    row = lax.broadcasted_iota(jnp.int32, (T, LRU_WIDTH), 0)
    mult = jnp.where((row == 0) & (ti == 0), 1.0, mult)
    u = (xc * ig) * mult

    nb = T // SUBLANES
    a3 = a.reshape(nb, SUBLANES, LRU_WIDTH)
    u3 = u.reshape(nb, SUBLANES, LRU_WIDTH)
    sub = lax.broadcasted_iota(jnp.int32, a3.shape, 1)
    d = 1
    while d < SUBLANES:
        valid = sub >= d
        u3 = jnp.where(valid, a3 * pltpu.roll(u3, d, 1) + u3, u3)
        a3 = jnp.where(valid, a3 * pltpu.roll(a3, d, 1), a3)
        d *= 2
    h = hc[...]
    blocks = []
    for b in range(nb):
        hb = u3[b] + a3[b] * h
        blocks.append(hb)
        h = hb[SUBLANES - 1:SUBLANES]
    hfull = jnp.concatenate(blocks, axis=0)
    hc[...] = h
    y = yr_ref[...]
    gelu = 0.5 * y * (1.0 + jnp.tanh(0.7978845608028654 * (y + 0.044715 * (y * y * y))))
    o_ref[...] = (hfull * gelu).astype(BF16)


def _lru(xr, yr, cw, cb, wr, wi, br, bi, lam, B, S):
    T = LRU_TILE
    seq = pl.BlockSpec((None, T, LRU_WIDTH), lambda b, t: (b, t, 0))
    full = lambda r, c: pl.BlockSpec((r, c), lambda b, t: (0, 0))
    return pl.pallas_call(
        _lru_kernel,
        grid=(B, S // T),
        in_specs=[seq, seq, full(CONV_W, LRU_WIDTH), full(1, LRU_WIDTH), full(LRU_WIDTH, LRU_WIDTH),
                  full(LRU_WIDTH, LRU_WIDTH), full(1, LRU_WIDTH), full(1, LRU_WIDTH), full(1, LRU_WIDTH)],
        out_specs=seq,
        out_shape=jax.ShapeDtypeStruct((B, S, LRU_WIDTH), BF16),
        scratch_shapes=[pltpu.VMEM((T + 8, LRU_WIDTH), F32), pltpu.VMEM((1, LRU_WIDTH), F32)],
        compiler_params=pltpu.CompilerParams(dimension_semantics=("arbitrary", "arbitrary"),
                                             vmem_limit_bytes=VMEM_LIMIT),
        name="rg_lru",
    )(xr, yr, cw, cb, wr, wi, br, bi, lam)


def _pack_words(v):
    bits = pltpu.bitcast(v.astype(BF16).astype(F32), jnp.uint32)
    half = D_MODEL // 2
    packed = (bits[:, :half] >> 16) | (bits[:, half:] & jnp.uint32(0xFFFF0000))
    return [packed[:, c * LANES:(c + 1) * LANES] for c in range(PACK_SUB)]


def _packed_chunk(c, rows, first_row=0):
    return (pl.ds(first_row * PACK_SUB + c, rows, stride=PACK_SUB), slice(None))


def _pack_rows(v, out_ref, first_row=0):
    for c, words in enumerate(_pack_words(v)):
        out_ref[_packed_chunk(c, v.shape[0], first_row)] = words


def _unpack_chunks(in_ref, rows, first_row=0):
    lo, hi = [], []
    for c in range(PACK_SUB):
        w = in_ref[_packed_chunk(c, rows, first_row)]
        lo.append(pltpu.bitcast(w << 16, F32))
        hi.append(pltpu.bitcast(w & jnp.uint32(0xFFFF0000), F32))
    return lo + hi


def _unpack_rows(in_ref, rows):
    return jnp.concatenate(_unpack_chunks(in_ref, rows), axis=1).astype(BF16)


def _merge_kernel(x_ref, attn_ref, lru_ref, g1_ref, wg_ref, woa_ref, wol_ref, wout_ref, g2_ref, wrt_ref,
                  x1_ref, h2p_ref, route_ref, counts_ref, cnt, tri):
    tm = MERGE_CHAIN

    @pl.when(pl.program_id(0) == 0)
    def _():
        cnt[...] = jnp.zeros_like(cnt)
        r = lax.broadcasted_iota(jnp.int32, (tm, tm), 0)
        c = lax.broadcasted_iota(jnp.int32, (tm, tm), 1)
        tri[...] = jnp.where(r < c, 1.0, 0.0).astype(BF16)

    counts = cnt[...]
    for r0 in range(0, SEQ_TILE, tm):
        counts = _merge_chain(r0, tm, counts, x_ref, attn_ref, lru_ref, g1_ref, wg_ref, woa_ref, wol_ref, wout_ref,
                              g2_ref, wrt_ref, x1_ref, h2p_ref, route_ref, tri)
    cnt[...] = counts
    counts_ref[...] = counts


def _merge_chain(r0, tm, counts, x_ref, attn_ref, lru_ref, g1_ref, wg_ref, woa_ref, wol_ref, wout_ref, g2_ref, wrt_ref,
                 x1_ref, h2p_ref, route_ref, tri):
    rs = slice(r0, r0 + tm)
    x = x_ref[rs, :]
    hb = _rms(x, g1_ref[...]).astype(BF16)
    gates = 0.5 * jnp.tanh(0.5 * _dot(hb, wg_ref[:, PROJ_COLS:])) + 0.5
    merged = gates[:, :D_MODEL] * _dot(attn_ref[rs, :], woa_ref[...]) + gates[:, D_MODEL:] * _dot(lru_ref[rs, :], wol_ref[...])
    x1 = x + _dot(merged.astype(BF16), wout_ref[...])
    x1_ref[rs, :] = x1
    h2 = _rms(x1, g2_ref[...])
    _pack_rows(h2, h2p_ref, first_row=r0)

    h_hi = h2.astype(BF16)
    h_lo = (h2 - h_hi.astype(F32)).astype(BF16)
    wrt = wrt_ref[...]
    nt_dims = (((1,), (1,)), ((), ()))
    hh = lax.dot_general(wrt, h_hi, nt_dims, preferred_element_type=F32)
    lo_pass = lax.dot_general(wrt[:ROUTE_LANES], h_lo, nt_dims, preferred_element_type=F32)
    logits = (hh[:ROUTE_LANES] + hh[ROUTE_LANES:] + lo_pass)[:ROUTE_ROWS]
    row = lax.broadcasted_iota(jnp.int32, logits.shape, 0)
    big = jnp.int32(1 << 20)

    def first_argmax(v):
        m = jnp.max(v, axis=0, keepdims=True)
        return m, jnp.min(jnp.where(v == m, row, big), axis=0, keepdims=True)

    gmask = row < N_GROUPS
    gmax, gidx = first_argmax(jnp.where(gmask, logits, -jnp.inf))
    gsum = jnp.sum(jnp.where(gmask, jnp.exp(logits - gmax), 0.0), axis=0, keepdims=True)
    g_w = 1.0 / gsum
    lo = N_GROUPS + EXPERTS_PER_GROUP * gidx
    el = jnp.where((row >= lo) & (row < lo + EXPERTS_PER_GROUP), logits, -jnp.inf)
    m1, i1 = first_argmax(el)
    m2, i2 = first_argmax(jnp.where(row == i1, -jnp.inf, el))
    rr = jnp.exp(m2 - m1)
    w1 = g_w / (1.0 + rr)
    w2 = g_w * rr / (1.0 + rr)
    oh1 = row == i1
    oh2 = row == i2
    oh = jnp.where(oh1 | oh2, 1.0, 0.0)
    before = _dot(oh.astype(BF16), tri[...]) + counts
    r1 = jnp.sum(jnp.where(oh1, before, 0.0), axis=0, keepdims=True)
    r2 = jnp.sum(jnp.where(oh2, before, 0.0), axis=0, keepdims=True)
    vals = (i1.astype(F32), i2.astype(F32), w1, w2, r1, r2, jnp.zeros_like(w1), jnp.zeros_like(w1))
    for k, v in enumerate(vals):
        route_ref[k:k + 1, rs] = v
    return counts + jnp.sum(oh, axis=1, keepdims=True)


def _merge(x2, attn, lru, g1, wg, woa, wol, wout, g2, wrt):
    N = x2.shape[0]
    tm = SEQ_TILE
    rows = lambda c: pl.BlockSpec((tm, c), lambda i: (i, 0))
    full = lambda r, c: pl.BlockSpec((r, c), lambda i: (0, 0))
    return pl.pallas_call(
        _merge_kernel,
        grid=(N // tm,),
        in_specs=[rows(D_MODEL), rows(ATTN_WIDTH), rows(LRU_WIDTH), full(1, D_MODEL),
                  full(D_MODEL, PROJ_COLS + 2 * D_MODEL),
                  full(ATTN_WIDTH, D_MODEL), full(LRU_WIDTH, D_MODEL), full(D_MODEL, D_MODEL), full(1, D_MODEL),
                  full(2 * ROUTE_LANES, D_MODEL)],
        out_specs=[rows(D_MODEL), pl.BlockSpec((tm * PACK_SUB, LANES), lambda i: (i, 0)),
                   pl.BlockSpec((ROUTE_FIELDS, tm), lambda i: (0, i)), full(ROUTE_ROWS, 1)],
        out_shape=[jax.ShapeDtypeStruct((N, D_MODEL), F32), jax.ShapeDtypeStruct((N * PACK_SUB, LANES), jnp.uint32),
                   jax.ShapeDtypeStruct((ROUTE_FIELDS, N), F32), jax.ShapeDtypeStruct((ROUTE_ROWS, 1), F32)],
        scratch_shapes=[pltpu.VMEM((ROUTE_ROWS, 1), F32), pltpu.VMEM((MERGE_CHAIN, MERGE_CHAIN), BF16)],
        compiler_params=pltpu.CompilerParams(dimension_semantics=("arbitrary",), vmem_limit_bytes=VMEM_LIMIT),
        name="merge_route",
    )(x2, attn, lru, g1, wg, woa, wol, wout, g2, wrt)


def _for_each_assignment(off_ref, fn):
    def group(gi, _):
        toks = [gi * DMA_GROUP + j for j in range(DMA_GROUP)]
        offs = [[off_ref[0, 0, k * ROW_TILE + tk] for k in range(TOP_K)] for tk in toks]
        for tk, o in zip(toks, offs):
            for k in range(TOP_K):
                fn(tk, k, pl.multiple_of(o[k], PACK_SUB))
        return 0
    lax.fori_loop(0, ROW_TILE // DMA_GROUP, group, 0)


def _dispatch_kernel(off_ref, h2p_hbm, xs_hbm, xin, in_sem, out_sem):
    tile_rows = ROW_TILE * PACK_SUB
    i = pl.program_id(0)
    n = pl.num_programs(0)
    slot = lax.rem(i, DISPATCH_SLOTS)

    def fetch(tile, s):
        return pltpu.make_async_copy(h2p_hbm.at[pl.ds(pl.multiple_of(tile * tile_rows, tile_rows), tile_rows)],
                                     xin.at[s], in_sem.at[s])

    def drain(s):
        for _ in range(TOP_K):
            pltpu.make_async_copy(xin.at[s], xs_hbm.at[pl.ds(0, tile_rows)], out_sem.at[s]).wait()

    @pl.when(i == 0)
    def _():
        fetch(0, 0).start()

    @pl.when(i + 1 < n)
    def _():
        fetch(i + 1, lax.rem(i + 1, DISPATCH_SLOTS)).start()

    fetch(i, slot).wait()

    def start(tk, k, off):
        pltpu.make_async_copy(xin.at[slot, pl.ds(pl.multiple_of(tk * PACK_SUB, PACK_SUB), PACK_SUB)],
                              xs_hbm.at[pl.ds(off, PACK_SUB)], out_sem.at[slot]).start(priority=k)

    _for_each_assignment(off_ref, start)

    @pl.when(i > 0)
    def _():
        drain(lax.rem(i + DISPATCH_SLOTS - 1, DISPATCH_SLOTS))

    @pl.when(i == n - 1)
    def _():
        drain(slot)


def _dispatch(off3, h2p):
    return pl.pallas_call(
        _dispatch_kernel,
        grid=(off3.shape[0],),
        in_specs=[pl.BlockSpec((1, 1, TOP_K * ROW_TILE), lambda i: (i, 0, 0), memory_space=pltpu.SMEM),
                  pl.BlockSpec(memory_space=pl.ANY)],
        out_specs=pl.BlockSpec(memory_space=pl.ANY),
        out_shape=jax.ShapeDtypeStruct((TOP_K * h2p.shape[0], LANES), jnp.uint32),
        scratch_shapes=[pltpu.VMEM((DISPATCH_SLOTS, ROW_TILE * PACK_SUB, LANES), jnp.uint32),
                        pltpu.SemaphoreType.DMA((DISPATCH_SLOTS,)), pltpu.SemaphoreType.DMA((DISPATCH_SLOTS,))],
        compiler_params=pltpu.CompilerParams(dimension_semantics=("arbitrary",), has_side_effects=True),
        name="moe_dispatch",
    )(off3, h2p)


def _moe_kernel(vt_ref, ve_ref, vlo_ref, vhi_ref, vnext_ref, vpar_ref, xs_ref, wg_hbm, wu_hbm, wd_hbm, ys_ref,
                wgb, wub, wdb, wgf, wuf, wdf, sem):
    tm = MOE_TILE
    v = pl.program_id(0)
    t = vt_ref[v]
    e = ve_ref[v]
    slot = vpar_ref[v]
    prev = jnp.maximum(v - 1, 0)

    def fetch(expert, s):
        return [pltpu.make_async_copy(w_hbm.at[expert], buf.at[s], sem.at[s])
                for w_hbm, buf in ((wg_hbm, wgf), (wu_hbm, wuf), (wd_hbm, wdf))]

    @pl.when(v == 0)
    def _():
        for copy in fetch(e, slot):
            copy.start()

    @pl.when((v == 0) | (ve_ref[prev] != e))
    def _():
        for copy in fetch(e, slot):
            copy.wait()
        nxt = vnext_ref[v]

        @pl.when(nxt != e)
        def _():
            for copy in fetch(nxt, 1 - slot):
                copy.start()

        wgb[...] = wgf[slot].astype(BF16)
        wub[...] = wuf[slot].astype(BF16)
        wdb[...] = wdf[slot].astype(BF16)

    xb = _unpack_rows(xs_ref, tm)
    g = _dot(xb, wgb[...])
    u = _dot(xb, wub[...])
    hmid = (g * jax.nn.sigmoid(g)) * u
    words = _pack_words(_dot(hmid.astype(BF16), wdb[...]))
    first = (v == 0) | (vt_ref[prev] != t)

    @pl.when(first)
    def _():
        for c in range(PACK_SUB):
            ys_ref[_packed_chunk(c, tm)] = words[c]

    @pl.when(jnp.logical_not(first))
    def _():
        rows = t * tm + lax.broadcasted_iota(jnp.int32, (tm, LANES), 0)
        mine = (rows >= vlo_ref[v]) & (rows < vhi_ref[v])
        for c in range(PACK_SUB):
            ys_ref[_packed_chunk(c, tm)] = jnp.where(mine, words[c], ys_ref[_packed_chunk(c, tm)])


def _moe(plan, xs, wg, wu, wd):
    tm = MOE_TILE
    n_rows = xs.shape[0] // PACK_SUB
    tile = lambda v, vt, *_: (vt[v], 0)
    hbm = pl.BlockSpec(memory_space=pl.ANY)
    grid_spec = pltpu.PrefetchScalarGridSpec(
        num_scalar_prefetch=len(plan),
        grid=(plan[0].shape[0],),
        in_specs=[pl.BlockSpec((tm * PACK_SUB, LANES), tile), hbm, hbm, hbm],
        out_specs=pl.BlockSpec((tm * PACK_SUB, LANES), tile),
        scratch_shapes=[pltpu.VMEM((D_MODEL, D_EXPERT), BF16), pltpu.VMEM((D_MODEL, D_EXPERT), BF16),
                        pltpu.VMEM((D_EXPERT, D_MODEL), BF16),
                        pltpu.VMEM((2, D_MODEL, D_EXPERT), F32), pltpu.VMEM((2, D_MODEL, D_EXPERT), F32),
                        pltpu.VMEM((2, D_EXPERT, D_MODEL), F32), pltpu.SemaphoreType.DMA((2,))],
    )
    return pl.pallas_call(
        _moe_kernel,
        grid_spec=grid_spec,
        out_shape=jax.ShapeDtypeStruct((n_rows * PACK_SUB, LANES), jnp.uint32),
        compiler_params=pltpu.CompilerParams(dimension_semantics=("arbitrary",), vmem_limit_bytes=VMEM_LIMIT),
        name="moe_experts",
    )(*plan, xs, wg, wu, wd)


def _combine_kernel(off_ref, offn_ref, x1_ref, w_ref, g_ref, ys_hbm, o_ref, ybuf, sem):
    tc = ROW_TILE
    i = pl.program_id(0)
    n = pl.num_programs(0)
    slot = lax.rem(i, 2)

    def gather(o_ref_, s):
        def start(tk, k, off):
            pltpu.make_async_copy(ys_hbm.at[pl.ds(off, PACK_SUB)],
                                  ybuf.at[s, pl.ds(pl.multiple_of((k * tc + tk) * PACK_SUB, PACK_SUB), PACK_SUB)],
                                  sem.at[s]).start(priority=k)
        _for_each_assignment(o_ref_, start)

    def drain(s):
        pltpu.make_async_copy(ys_hbm.at[pl.ds(0, TOP_K * tc * PACK_SUB)], ybuf.at[s], sem.at[s]).wait()

    @pl.when(i == 0)
    def _():
        gather(off_ref, 0)

    @pl.when(i + 1 < n)
    def _():
        gather(offn_ref, 1 - slot)

    drain(slot)

    n_chunks = D_MODEL // LANES
    z = [x1_ref[:, c * LANES:(c + 1) * LANES] for c in range(n_chunks)]
    for k in range(TOP_K):
        wk = w_ref[:, k:k + 1]
        yk = _unpack_chunks(ybuf.at[slot], tc, first_row=k * tc)
        z = [zc + wk * yc for zc, yc in zip(z, yk)]
    ss = sum(jnp.sum(zc * zc, axis=-1, keepdims=True) for zc in z)
    inv = lax.rsqrt(ss * (1.0 / D_MODEL) + NORM_EPS)
    for c in range(n_chunks):
        o_ref[:, c * LANES:(c + 1) * LANES] = z[c] * inv * g_ref[:, c * LANES:(c + 1) * LANES]


def _combine(off3, x1, weight, g, ys):
    N = x1.shape[0]
    tc = ROW_TILE
    nt = N // tc
    idx = lambda f: pl.BlockSpec((1, 1, TOP_K * tc), f, memory_space=pltpu.SMEM)
    return pl.pallas_call(
        _combine_kernel,
        grid=(nt,),
        in_specs=[idx(lambda i: (i, 0, 0)),
                  idx(lambda i: (jnp.minimum(i + 1, nt - 1), 0, 0)),
                  pl.BlockSpec((tc, D_MODEL), lambda i: (i, 0)),
                  pl.BlockSpec((tc, TOP_K), lambda i: (i, 0)),
                  pl.BlockSpec((1, D_MODEL), lambda i: (0, 0)),
                  pl.BlockSpec(memory_space=pl.ANY)],
        out_specs=pl.BlockSpec((tc, D_MODEL), lambda i: (i, 0)),
        out_shape=jax.ShapeDtypeStruct((N, D_MODEL), F32),
        scratch_shapes=[pltpu.VMEM((2, TOP_K * tc * PACK_SUB, LANES), jnp.uint32), pltpu.SemaphoreType.DMA((2,))],
        compiler_params=pltpu.CompilerParams(dimension_semantics=("arbitrary",), vmem_limit_bytes=VMEM_LIMIT),
        name="combine_norm",
    )(off3, off3, x1, weight, g, ys)


def _block_diag(w):
    nb, c, _ = w.shape
    eye = jnp.eye(nb, dtype=w.dtype)
    return (eye[:, None, :, None] * w[:, :, None, :]).reshape(nb * c, nb * c)


def _visit_plan(counts, n_rows):
    tm = MOE_TILE
    n_tiles = n_rows // tm
    n_visits = n_tiles + N_EXPERTS - 1
    cnt = counts.astype(jnp.int32)
    lanes = jnp.arange(ROUTE_ROWS, dtype=jnp.int32)
    upto = (lanes[None, :] <= lanes[:, None]).astype(jnp.int32)
    ends = upto @ cnt
    starts = ends - cnt
    first_tile = starts // tm
    n_vis = jnp.where(cnt > 0, (ends - 1) // tm - first_tile + 1, 0)
    v_end = upto @ n_vis
    v_start = v_end - n_vis
    total = v_end[-1]
    v = jnp.arange(n_visits, dtype=jnp.int32)
    vc = jnp.minimum(v, total - 1)
    own = ((vc[:, None] >= v_start[None, :]) & (vc[:, None] < v_end[None, :])).astype(jnp.int32)
    pick = lambda per_lane: own @ per_lane
    tile = jnp.maximum(pick(first_tile - v_start) + vc, 0)
    lo = jnp.maximum(pick(starts), tile * tm)
    hi = jnp.minimum(pick(ends), (tile + 1) * tm)
    valid = v < total
    to_expert = lambda lane: jnp.maximum(lane - EXPERT_LANE0, 0)
    used = cnt > 0
    later = jnp.where((lanes[None, :] > lanes[:, None]) & used[None, :], lanes[None, :], ROUTE_ROWS)
    nxt = jnp.min(later, axis=1)
    nxt = jnp.where(nxt == ROUTE_ROWS, lanes, nxt)
    parity = (upto @ used.astype(jnp.int32) - 1) % 2
    plan = (tile, to_expert(pick(lanes)), jnp.where(valid, lo, 0), jnp.where(valid, hi, 0),
            to_expert(pick(nxt)), pick(parity))
    return starts, plan


def kernel(x, norm_mix_g, w_in, lambda_qk, subln_g, conv_w, conv_b, w_r, b_r, w_i, b_i, lru_lambda, w_o_attn, w_o_lru, w_out, norm_ffn_g, w_group, w_expert_router, w_gate, w_up, w_down, final_norm_g):
    B, S, D = x.shape
    N = B * S
    nt = S // SEQ_TILE
    depth = norm_mix_g.shape[0]
    assert depth == 1 and D == D_MODEL and S % SEQ_TILE == 0 and S % LRU_TILE == 0
    assert N % ROW_TILE == 0 and (N * TOP_K) % MOE_TILE == 0
    l = 0
    row = lambda v: v.reshape(1, -1).astype(F32)

    x2 = x.reshape(N, D)
    w_in_l = w_in[l].astype(BF16)
    qT, k, vT, xr, yr, kn = _inproj(x2, row(norm_mix_g[l]), w_in_l, B, S)

    kn_max = jnp.max(kn[:, 0, :2 * N_HEADS].reshape(B, nt, 2 * N_HEADS), axis=1)
    attn = _attn(kn_max.reshape(-1), qT, k.reshape(B, nt, SEQ_TILE, N_HEADS * K_COLS), vT,
                 lambda_qk[l].reshape(4, HEAD_DIM).astype(F32), row(subln_g[l]), B, S)

    lru = _lru(xr.reshape(B, S, LRU_WIDTH), yr.reshape(B, S, LRU_WIDTH), conv_w[l].astype(F32), row(conv_b[l]),
               _block_diag(w_r[l]).astype(BF16), _block_diag(w_i[l]).astype(BF16), row(b_r[l]), row(b_i[l]),
               row(lru_lambda[l]), B, S)

    w_route = jnp.concatenate(
        [w_group[l], jnp.transpose(w_expert_router[l], (1, 0, 2)).reshape(D, N_EXPERTS),
         jnp.zeros((D, ROUTE_LANES - N_GROUPS - N_EXPERTS), F32)], axis=1).astype(F32)
    w_route_hi = w_route.astype(BF16)
    w_route = jnp.concatenate([w_route_hi, (w_route - w_route_hi.astype(F32)).astype(BF16)], axis=1).T
    x1, h2p, route, counts = _merge(x2, attn.reshape(N, ATTN_WIDTH), lru.reshape(N, LRU_WIDTH), row(norm_mix_g[l]),
                                    w_in_l, w_o_attn[l].astype(BF16),
                                    w_o_lru[l].astype(BF16), w_out[l].astype(BF16), row(norm_ffn_g[l]), w_route)

    starts, plan = _visit_plan(counts[:, 0], N * TOP_K)
    lane = route[0:TOP_K].astype(jnp.int32)
    weight = jnp.transpose(route[TOP_K:2 * TOP_K])
    rank = route[2 * TOP_K:3 * TOP_K].astype(jnp.int32)
    lane_ids = jnp.arange(ROUTE_ROWS, dtype=jnp.int32)
    start_of = jnp.sum(jnp.where(lane[..., None] == lane_ids, starts, 0), axis=-1)
    off = ((start_of + rank) * PACK_SUB).reshape(TOP_K, N // ROW_TILE, ROW_TILE)
    off3 = jnp.transpose(off, (1, 0, 2)).reshape(N // ROW_TILE, 1, TOP_K * ROW_TILE)

    xs = _dispatch(off3, h2p)
    ys = _moe(plan, xs, w_gate[l], w_up[l], w_down[l])
    out = _combine(off3, x1, weight, row(final_norm_g), ys)
    return out.reshape(B, S, D)
```

```python
import math

import jax
import jax.numpy as jnp
from jax import lax
from jax.experimental import pallas as pl
from jax.experimental.pallas import tpu as pltpu

F32 = jnp.float32
BF16 = jnp.bfloat16

D_MODEL = 1024
N_HEADS = 4
HEAD_DIM = 64
V_DIM = 2 * HEAD_DIM
ATTN_WIDTH = N_HEADS * V_DIM
V_ROWS = V_DIM + 16
K_COLS = 2 * V_DIM
POS_SPLIT = 3
POS_RADIX = 256
SKIP_MARGIN = 138.0
NORM_SLACK = 1.0201
LRU_WIDTH = D_MODEL // 2
LRU_BLOCKS = 8
CONV_W = 4
LRU_C = 8.0
N_GROUPS = 4
EXPERTS_PER_GROUP = 8
N_EXPERTS = N_GROUPS * EXPERTS_PER_GROUP
TOP_K = 2
D_EXPERT = D_MODEL // 2
NORM_EPS = 1e-6
LAM_INIT = 0.8 - 0.6 * math.exp(-0.3 * 0)

QK_COLS = N_HEADS * 2 * HEAD_DIM
PROJ_COLS = 2 * QK_COLS + ATTN_WIDTH + 2 * LRU_WIDTH
ROUTE_LANES = 128
ROUTE_ROWS = 64
ROUTE_FIELDS = 8
NEG_BIG = -1e30
LOG2E = math.log2(math.e)
ALIBI_SLOPES = tuple(2.0 ** (-8.0 * (h + 1) / N_HEADS) for h in range(N_HEADS))

SEQ_TILE = 512
LRU_TILE = 512
ROW_TILE = 256
MERGE_CHAIN = 512
MOE_TILE = 256
DMA_GROUP = 8
DISPATCH_SLOTS = 3
LANES = 128
SUBLANES = 8
PACK_SUB = D_MODEL // 2 // LANES
EXPERT_LANE0 = N_GROUPS
V7X_VMEM_BYTES = 64 * 1024 * 1024
VMEM_LIMIT = V7X_VMEM_BYTES * 3 // 4


def _rms(x, g):
    return x * lax.rsqrt(jnp.mean(x * x, axis=-1, keepdims=True) + NORM_EPS) * g


def _dot(a, b):
    return jnp.dot(a, b, preferred_element_type=F32)


def _inproj_kernel(x_ref, g_ref, w_ref, qT_ref, k_ref, vT_ref, xr_ref, yr_ref, kn_ref):
    hb = _rms(x_ref[...], g_ref[...]).astype(BF16)

    def proj(lo, hi):
        return _dot(hb, w_ref[:, lo:hi])

    q = proj(0, QK_COLS) * (HEAD_DIM ** -0.5 * LOG2E)
    for h in range(N_HEADS):
        qT_ref[h] = q[:, h * V_DIM:(h + 1) * V_DIM].T.astype(BF16)
    k = proj(QK_COLS, 2 * QK_COLS).astype(BF16)
    r = lax.broadcasted_iota(jnp.int32, (k.shape[0], K_COLS - V_DIM), 0)
    lane = lax.broadcasted_iota(jnp.int32, r.shape, 1)
    a = r // POS_RADIX * POS_RADIX
    feat = jnp.where(lane < POS_SPLIT, a, jnp.where(lane < 2 * POS_SPLIT, r - a, 0)).astype(F32).astype(BF16)
    for h in range(N_HEADS):
        k_ref[:, h * K_COLS:h * K_COLS + V_DIM] = k[:, h * V_DIM:(h + 1) * V_DIM]
        k_ref[:, h * K_COLS + V_DIM:(h + 1) * K_COLS] = feat
    seg = lax.broadcasted_iota(jnp.int32, (QK_COLS, LANES), 0) // HEAD_DIM
    pick = jnp.where(seg == lax.broadcasted_iota(jnp.int32, (QK_COLS, LANES), 1), 1.0, 0.0).astype(BF16)
    row_n2 = _dot(jnp.square(k.astype(F32)).astype(BF16), pick)
    kn_ref[...] = jnp.max(row_n2, axis=0, keepdims=True)
    v = proj(2 * QK_COLS, 2 * QK_COLS + ATTN_WIDTH)
    for h in range(N_HEADS):
        vT_ref[h, :V_DIM, :] = v[:, h * V_DIM:(h + 1) * V_DIM].T.astype(BF16)
        pad_row = lax.broadcasted_iota(jnp.int32, (V_ROWS - V_DIM, v.shape[0]), 0)
        vT_ref[h, V_DIM:, :] = jnp.where(pad_row == 0, 1.0, 0.0).astype(BF16)
    c0 = 2 * QK_COLS + ATTN_WIDTH
    xr_ref[...] = proj(c0, c0 + LRU_WIDTH)
    yr_ref[...] = proj(c0 + LRU_WIDTH, c0 + 2 * LRU_WIDTH)


def _inproj(x2, g, w, B, S):
    N = B * S
    tm = SEQ_TILE
    nt = S // tm
    tile5 = pl.BlockSpec((None, N_HEADS, None, V_DIM, tm), lambda i: (i // nt, 0, i % nt, 0, 0))
    rows = lambda c: pl.BlockSpec((tm, c), lambda i: (i, 0))
    return pl.pallas_call(
        _inproj_kernel,
        grid=(N // tm,),
        in_specs=[rows(D_MODEL),
                  pl.BlockSpec((1, D_MODEL), lambda i: (0, 0)),
                  pl.BlockSpec((D_MODEL, PROJ_COLS), lambda i: (0, 0))],
        out_specs=[tile5, rows(N_HEADS * K_COLS),
                   pl.BlockSpec((None, N_HEADS, None, V_ROWS, tm), lambda i: (i // nt, 0, i % nt, 0, 0)),
                   rows(LRU_WIDTH), rows(LRU_WIDTH),
                   pl.BlockSpec((None, 1, LANES), lambda i: (i, 0, 0))],
        out_shape=[jax.ShapeDtypeStruct((B, N_HEADS, nt, V_DIM, tm), BF16),
                   jax.ShapeDtypeStruct((N, N_HEADS * K_COLS), BF16),
                   jax.ShapeDtypeStruct((B, N_HEADS, nt, V_ROWS, tm), BF16),
                   jax.ShapeDtypeStruct((N, LRU_WIDTH), F32),
                   jax.ShapeDtypeStruct((N, LRU_WIDTH), F32),
                   jax.ShapeDtypeStruct((N // tm, 1, LANES), F32)],
        compiler_params=pltpu.CompilerParams(dimension_semantics=("parallel",),
                                             vmem_limit_bytes=VMEM_LIMIT),
        name="inproj",
    )(x2, g, w)


def _attn_kernel(kn_ref, qT_ref, k_ref, vT_ref, lam_ref, g_ref, o_ref, mask_ref, acc_ref, sa_ref, sb_ref, pa_ref,
                 pb_ref):
    t = SEQ_TILE
    b = pl.program_id(0)
    h = pl.program_id(1)
    i = pl.program_id(2)
    slope = LOG2E * jnp.where(h == 0, ALIBI_SLOPES[0], jnp.where(h == 1, ALIBI_SLOPES[1],
                              jnp.where(h == 2, ALIBI_SLOPES[2], ALIBI_SLOPES[3]))).astype(F32)

    @pl.when(i == 0)
    def _():
        r = lax.broadcasted_iota(jnp.int32, (t, t), 0)
        c = lax.broadcasted_iota(jnp.int32, (t, t), 1)
        mask_ref[...] = jnp.where(r <= c, 0.0, NEG_BIG)

    qf = qT_ref[...].astype(F32)
    row = lax.broadcasted_iota(jnp.int32, qf.shape, 0)
    sl = jnp.full(qf.shape, slope, F32)
    hi = sl.astype(BF16).astype(F32)
    mid = (sl - hi).astype(BF16).astype(F32)
    lo = (sl - hi - mid).astype(BF16).astype(F32)
    piece = jnp.where(row % 3 == 0, hi, jnp.where(row % 3 == 1, mid, lo))
    srows = jnp.where(row < 2 * POS_SPLIT, piece, 0.0).astype(BF16)
    qs = tuple(jnp.concatenate([jnp.where(sel, qf, 0.0).astype(BF16), srows], axis=0)
               for sel in (row < HEAD_DIM, row >= HEAD_DIM))
    acc_ref[...] = jnp.zeros_like(acc_ref)
    pb_ref[...] = jnp.zeros_like(pb_ref)

    first_off = [0]

    def key_tile(tau):
        return jnp.where(tau <= 0, i, first_off[0] + tau - 1)

    def stage_q(tau, s_ref, diagonal=False):
        kt = k_ref[key_tile(tau)]
        tile_max = []
        for mi in range(2):
            s = _dot(kt, qs[mi])
            if diagonal:
                s = s + mask_ref[...]
            s_ref[mi] = s
            tile_max.append(jnp.max(s, axis=0, keepdims=True))
        return tuple(tile_max)

    def stage_s(tau, s_ref, p_ref, ms, tile_max):
        cj = slope * (key_tile(tau) * t).astype(F32)
        m_out, alphas = [], []
        for mi in range(2):
            m_new = jnp.maximum(ms[mi], tile_max[mi] + cj)
            alphas.append(jnp.exp2(ms[mi] - m_new))
            p_ref[mi] = jnp.exp2(s_ref[mi] - (m_new - cj)).astype(BF16)
            m_out.append(m_new)
        return tuple(m_out), tuple(alphas)

    def stage_v(tau, p_ref, alphas):
        vt = vT_ref[key_tile(tau)]
        for mi in range(2):
            acc_ref[mi] = alphas[mi] * acc_ref[mi] + _dot(vt, p_ref[mi])

    def body(jj, carry):
        ms, alphas, tmax = carry[:2], carry[2:4], carry[4:]
        tau = 2 * jj
        tmax_b = stage_q(tau + 1, sb_ref)
        ms, alphas_a = stage_s(tau, sa_ref, pa_ref, ms, tmax)
        stage_v(tau - 1, pb_ref, alphas)
        tmax_a = stage_q(tau + 2, sa_ref)
        ms, alphas_b = stage_s(tau + 1, sb_ref, pb_ref, ms, tmax_b)
        stage_v(tau, pa_ref, alphas_a)
        return ms + alphas_b + tmax_a

    def finalize():
        lp = lam_ref[...]
        s1 = jnp.sum(lp[0:1] * lp[1:2], axis=-1, keepdims=True)
        s2 = jnp.sum(lp[2:3] * lp[3:4], axis=-1, keepdims=True)
        lam = jnp.exp(s1) - jnp.exp(s2) + LAM_INIT
        norm = [acc_ref[mi, :V_DIM, :] * (1.0 / acc_ref[mi, V_DIM:V_DIM + 1, :]) for mi in range(2)]
        oT = norm[0] - lam * norm[1]
        o = _rms(oT.T, g_ref[...]) * (1.0 - LAM_INIT)
        o_ref[...] = o.astype(BF16)

    m_init = jnp.full((1, t), NEG_BIG, F32)
    one = jnp.ones((1, t), F32)
    tmax0 = stage_q(0, sa_ref, diagonal=True)

    cj0 = slope * (i * t).astype(F32)
    n_skip = None
    for mi in range(2):
        qsq = jnp.square(qf[mi * HEAD_DIM:(mi + 1) * HEAD_DIM, :])
        qn2 = jnp.max(jnp.sum(qsq, axis=0, keepdims=True), axis=1, keepdims=True)
        m_low = jnp.min(tmax0[mi], axis=1, keepdims=True) + cj0
        qk = jnp.sqrt(qn2 * (kn_ref[(b * N_HEADS + h) * 2 + mi] * NORM_SLACK))
        count = jnp.ceil((m_low - SKIP_MARGIN - slope * (t - 1) - qk) / (slope * t))
        n_skip = count if n_skip is None else jnp.minimum(n_skip, count)
    first_off[0] = jnp.clip(n_skip, 0.0, i.astype(F32)).astype(jnp.int32)[0, 0]
    n_off = i - first_off[0]

    n_main = jnp.right_shift(n_off, 1)
    fin = lax.fori_loop(0, n_main, body, (m_init, m_init, one, one) + tmax0)
    ms, alphas, tmax = fin[:2], fin[2:4], fin[4:]
    tau = 2 * n_main
    odd_tiles = tau == n_off

    @pl.when(odd_tiles)
    def _():
        _, alphas_a = stage_s(tau, sa_ref, pa_ref, ms, tmax)
        stage_v(tau - 1, pb_ref, alphas)
        stage_v(tau, pa_ref, alphas_a)
        finalize()

    @pl.when(jnp.logical_not(odd_tiles))
    def _():
        tmax_b = stage_q(tau + 1, sb_ref)
        ms_a, alphas_a = stage_s(tau, sa_ref, pa_ref, ms, tmax)
        stage_v(tau - 1, pb_ref, alphas)
        _, alphas_b = stage_s(tau + 1, sb_ref, pb_ref, ms_a, tmax_b)
        stage_v(tau, pa_ref, alphas_a)
        stage_v(tau + 1, pb_ref, alphas_b)
        finalize()


def _attn(kn, qT, k4, vT, lam, g, B, S):
    t = SEQ_TILE
    nt = S // t
    grid_spec = pltpu.PrefetchScalarGridSpec(
        num_scalar_prefetch=1,
        grid=(B, N_HEADS, nt),
        in_specs=[pl.BlockSpec((None, None, None, V_DIM, t), lambda b, h, i, kn: (b, h, i, 0, 0)),
                  pl.BlockSpec((None, nt, t, K_COLS), lambda b, h, i, kn: (b, 0, 0, h)),
                  pl.BlockSpec((None, None, nt, V_ROWS, t), lambda b, h, i, kn: (b, h, 0, 0, 0)),
                  pl.BlockSpec((4, HEAD_DIM), lambda b, h, i, kn: (0, 0)),
                  pl.BlockSpec((1, V_DIM), lambda b, h, i, kn: (0, 0))],
        out_specs=pl.BlockSpec((None, t, V_DIM), lambda b, h, i, kn: (b, i, h)),
        scratch_shapes=[pltpu.VMEM((t, t), F32), pltpu.VMEM((2, V_ROWS, t), F32),
                        pltpu.VMEM((2, t, t), F32), pltpu.VMEM((2, t, t), F32),
                        pltpu.VMEM((2, t, t), BF16), pltpu.VMEM((2, t, t), BF16)],
    )
    return pl.pallas_call(
        _attn_kernel,
        grid_spec=grid_spec,
        out_shape=jax.ShapeDtypeStruct((B, S, ATTN_WIDTH), BF16),
        compiler_params=pltpu.CompilerParams(dimension_semantics=("parallel", "arbitrary", "arbitrary"),
                                             vmem_limit_bytes=VMEM_LIMIT),
        name="diff_attn",
    )(kn, qT, k4, vT, lam, g)


def _lru_kernel(xr_ref, yr_ref, cw_ref, cb_ref, wr_ref, wi_ref, br_ref, bi_ref, lam_ref, o_ref, xbuf, hc):
    T = LRU_TILE
    ti = pl.program_id(1)

    @pl.when(ti == 0)
    def _():
        xbuf[0:8] = jnp.zeros((8, LRU_WIDTH), F32)
        hc[...] = jnp.zeros_like(hc)

    x = xr_ref[...]
    xbuf[8:8 + T] = x
    cw = cw_ref[...]
    xc = cb_ref[...] + cw[3:4] * x
    for j in range(CONV_W - 1):
        xc = xc + cw[j:j + 1] * xbuf[5 + j:5 + j + T]
    xbuf[0:8] = x[T - 8:T]

    xb = xc.astype(BF16)
    r = jax.nn.sigmoid(_dot(xb, wr_ref[...]) + br_ref[...])
    ig = jax.nn.sigmoid(_dot(xb, wi_ref[...]) + bi_ref[...])
    z = -lam_ref[...]
    softplus = jnp.maximum(z, 0.0) + jnp.log1p(jnp.exp(-jnp.abs(z)))
    la = -LRU_C * r * softplus
    a = jnp.exp(la)
    m2 = -jnp.tanh(la) * (a * a + 1.0)
    mult = jnp.where(m2 > 0.0, m2 * lax.rsqrt(m2), 0.0)
    row = lax.broadcasted_iota(jnp.int32, (T, LRU_WIDTH), 0)
    mult = jnp.where((row == 0) & (ti == 0), 1.0, mult)
    u = (xc * ig) * mult

    nb = T // SUBLANES
    a3 = a.reshape(nb, SUBLANES, LRU_WIDTH)
    u3 = u.reshape(nb, SUBLANES, LRU_WIDTH)
    sub = lax.broadcasted_iota(jnp.int32, a3.shape, 1)
    d = 1
    while d < SUBLANES:
        valid = sub >= d
        u3 = jnp.where(valid, a3 * pltpu.roll(u3, d, 1) + u3, u3)
        a3 = jnp.where(valid, a3 * pltpu.roll(a3, d, 1), a3)
        d *= 2
    h = hc[...]
    blocks = []
    for b in range(nb):
        hb = u3[b] + a3[b] * h
        blocks.append(hb)
        h = hb[SUBLANES - 1:SUBLANES]
    hfull = jnp.concatenate(blocks, axis=0)
    hc[...] = h
    y = yr_ref[...]
    gelu = 0.5 * y * (1.0 + jnp.tanh(0.7978845608028654 * (y + 0.044715 * (y * y * y))))
    o_ref[...] = (hfull * gelu).astype(BF16)


def _lru(xr, yr, cw, cb, wr, wi, br, bi, lam, B, S):
    T = LRU_TILE
    seq = pl.BlockSpec((None, T, LRU_WIDTH), lambda b, t: (b, t, 0))
    full = lambda r, c: pl.BlockSpec((r, c), lambda b, t: (0, 0))
    return pl.pallas_call(
        _lru_kernel,
        grid=(B, S // T),
        in_specs=[seq, seq, full(CONV_W, LRU_WIDTH), full(1, LRU_WIDTH), full(LRU_WIDTH, LRU_WIDTH),
                  full(LRU_WIDTH, LRU_WIDTH), full(1, LRU_WIDTH), full(1, LRU_WIDTH), full(1, LRU_WIDTH)],
        out_specs=seq,
        out_shape=jax.ShapeDtypeStruct((B, S, LRU_WIDTH), BF16),
        scratch_shapes=[pltpu.VMEM((T + 8, LRU_WIDTH), F32), pltpu.VMEM((1, LRU_WIDTH), F32)],
        compiler_params=pltpu.CompilerParams(dimension_semantics=("arbitrary", "arbitrary"),
                                             vmem_limit_bytes=VMEM_LIMIT),
        name="rg_lru",
    )(xr, yr, cw, cb, wr, wi, br, bi, lam)


def _pack_words(v):
    bits = pltpu.bitcast(v.astype(BF16).astype(F32), jnp.uint32)
    half = D_MODEL // 2
    packed = (bits[:, :half] >> 16) | (bits[:, half:] & jnp.uint32(0xFFFF0000))
    return [packed[:, c * LANES:(c + 1) * LANES] for c in range(PACK_SUB)]


def _packed_chunk(c, rows, first_row=0):
    return (pl.ds(first_row * PACK_SUB + c, rows, stride=PACK_SUB), slice(None))


def _pack_rows(v, out_ref, first_row=0):
    for c, words in enumerate(_pack_words(v)):
        out_ref[_packed_chunk(c, v.shape[0], first_row)] = words


def _unpack_chunks(in_ref, rows, first_row=0):
    lo, hi = [], []
    for c in range(PACK_SUB):
        w = in_ref[_packed_chunk(c, rows, first_row)]
        lo.append(pltpu.bitcast(w << 16, F32))
        hi.append(pltpu.bitcast(w & jnp.uint32(0xFFFF0000), F32))
    return lo + hi


def _unpack_rows(in_ref, rows):
    return jnp.concatenate(_unpack_chunks(in_ref, rows), axis=1).astype(BF16)


def _merge_kernel(x_ref, attn_ref, lru_ref, g1_ref, wg_ref, woa_ref, wol_ref, wout_ref, g2_ref, wrt_ref,
                  x1_ref, h2p_ref, route_ref, counts_ref, cnt, tri):
    tm = MERGE_CHAIN

    @pl.when(pl.program_id(0) == 0)
    def _():
        cnt[...] = jnp.zeros_like(cnt)
        r = lax.broadcasted_iota(jnp.int32, (tm, tm), 0)
        c = lax.broadcasted_iota(jnp.int32, (tm, tm), 1)
        tri[...] = jnp.where(r < c, 1.0, 0.0).astype(BF16)

    counts = cnt[...]
    for r0 in range(0, SEQ_TILE, tm):
        counts = _merge_chain(r0, tm, counts, x_ref, attn_ref, lru_ref, g1_ref, wg_ref, woa_ref, wol_ref, wout_ref,
                              g2_ref, wrt_ref, x1_ref, h2p_ref, route_ref, tri)
    cnt[...] = counts
    counts_ref[...] = counts


def _merge_chain(r0, tm, counts, x_ref, attn_ref, lru_ref, g1_ref, wg_ref, woa_ref, wol_ref, wout_ref, g2_ref, wrt_ref,
                 x1_ref, h2p_ref, route_ref, tri):
    rs = slice(r0, r0 + tm)
    x = x_ref[rs, :]
    hb = _rms(x, g1_ref[...]).astype(BF16)
    gates = 0.5 * jnp.tanh(0.5 * _dot(hb, wg_ref[:, PROJ_COLS:])) + 0.5
    merged = gates[:, :D_MODEL] * _dot(attn_ref[rs, :], woa_ref[...]) + gates[:, D_MODEL:] * _dot(lru_ref[rs, :], wol_ref[...])
    x1 = x + _dot(merged.astype(BF16), wout_ref[...])
    x1_ref[rs, :] = x1
    h2 = _rms(x1, g2_ref[...])
    _pack_rows(h2, h2p_ref, first_row=r0)

    h_hi = h2.astype(BF16)
    h_lo = (h2 - h_hi.astype(F32)).astype(BF16)
    wrt = wrt_ref[...]
    nt_dims = (((1,), (1,)), ((), ()))
    hh = lax.dot_general(wrt, h_hi, nt_dims, preferred_element_type=F32)
    lo_pass = lax.dot_general(wrt[:ROUTE_LANES], h_lo, nt_dims, preferred_element_type=F32)
    logits = (hh[:ROUTE_LANES] + hh[ROUTE_LANES:] + lo_pass)[:ROUTE_ROWS]
    row = lax.broadcasted_iota(jnp.int32, logits.shape, 0)
    big = jnp.int32(1 << 20)

    def first_argmax(v):
        m = jnp.max(v, axis=0, keepdims=True)
        return m, jnp.min(jnp.where(v == m, row, big), axis=0, keepdims=True)

    gmask = row < N_GROUPS
    gmax, gidx = first_argmax(jnp.where(gmask, logits, -jnp.inf))
    gsum = jnp.sum(jnp.where(gmask, jnp.exp(logits - gmax), 0.0), axis=0, keepdims=True)
    g_w = 1.0 / gsum
    lo = N_GROUPS + EXPERTS_PER_GROUP * gidx
    el = jnp.where((row >= lo) & (row < lo + EXPERTS_PER_GROUP), logits, -jnp.inf)
    m1, i1 = first_argmax(el)
    m2, i2 = first_argmax(jnp.where(row == i1, -jnp.inf, el))
    rr = jnp.exp(m2 - m1)
    w1 = g_w / (1.0 + rr)
    w2 = g_w * rr / (1.0 + rr)
    oh1 = row == i1
    oh2 = row == i2
    oh = jnp.where(oh1 | oh2, 1.0, 0.0)
    before = _dot(oh.astype(BF16), tri[...]) + counts
    r1 = jnp.sum(jnp.where(oh1, before, 0.0), axis=0, keepdims=True)
    r2 = jnp.sum(jnp.where(oh2, before, 0.0), axis=0, keepdims=True)
    vals = (i1.astype(F32), i2.astype(F32), w1, w2, r1, r2, jnp.zeros_like(w1), jnp.zeros_like(w1))
    for k, v in enumerate(vals):
        route_ref[k:k + 1, rs] = v
    return counts + jnp.sum(oh, axis=1, keepdims=True)


def _merge(x2, attn, lru, g1, wg, woa, wol, wout, g2, wrt):
    N = x2.shape[0]
    tm = SEQ_TILE
    rows = lambda c: pl.BlockSpec((tm, c), lambda i: (i, 0))
    full = lambda r, c: pl.BlockSpec((r, c), lambda i: (0, 0))
    return pl.pallas_call(
        _merge_kernel,
        grid=(N // tm,),
        in_specs=[rows(D_MODEL), rows(ATTN_WIDTH), rows(LRU_WIDTH), full(1, D_MODEL),
                  full(D_MODEL, PROJ_COLS + 2 * D_MODEL),
                  full(ATTN_WIDTH, D_MODEL), full(LRU_WIDTH, D_MODEL), full(D_MODEL, D_MODEL), full(1, D_MODEL),
                  full(2 * ROUTE_LANES, D_MODEL)],
        out_specs=[rows(D_MODEL), pl.BlockSpec((tm * PACK_SUB, LANES), lambda i: (i, 0)),
                   pl.BlockSpec((ROUTE_FIELDS, tm), lambda i: (0, i)), full(ROUTE_ROWS, 1)],
        out_shape=[jax.ShapeDtypeStruct((N, D_MODEL), F32), jax.ShapeDtypeStruct((N * PACK_SUB, LANES), jnp.uint32),
                   jax.ShapeDtypeStruct((ROUTE_FIELDS, N), F32), jax.ShapeDtypeStruct((ROUTE_ROWS, 1), F32)],
        scratch_shapes=[pltpu.VMEM((ROUTE_ROWS, 1), F32), pltpu.VMEM((MERGE_CHAIN, MERGE_CHAIN), BF16)],
        compiler_params=pltpu.CompilerParams(dimension_semantics=("arbitrary",), vmem_limit_bytes=VMEM_LIMIT),
        name="merge_route",
    )(x2, attn, lru, g1, wg, woa, wol, wout, g2, wrt)


def _for_each_assignment(off_ref, fn):
    def group(gi, _):
        toks = [gi * DMA_GROUP + j for j in range(DMA_GROUP)]
        offs = [[off_ref[0, 0, k * ROW_TILE + tk] for k in range(TOP_K)] for tk in toks]
        for tk, o in zip(toks, offs):
            for k in range(TOP_K):
                fn(tk, k, pl.multiple_of(o[k], PACK_SUB))
        return 0
    lax.fori_loop(0, ROW_TILE // DMA_GROUP, group, 0)


def _dispatch_kernel(off_ref, h2p_hbm, xs_hbm, xin, in_sem, out_sem):
    tile_rows = ROW_TILE * PACK_SUB
    i = pl.program_id(0)
    n = pl.num_programs(0)
    slot = lax.rem(i, DISPATCH_SLOTS)

    def fetch(tile, s):
        return pltpu.make_async_copy(h2p_hbm.at[pl.ds(pl.multiple_of(tile * tile_rows, tile_rows), tile_rows)],
                                     xin.at[s], in_sem.at[s])

    def drain(s):
        for _ in range(TOP_K):
            pltpu.make_async_copy(xin.at[s], xs_hbm.at[pl.ds(0, tile_rows)], out_sem.at[s]).wait()

    @pl.when(i == 0)
    def _():
        fetch(0, 0).start()

    @pl.when(i + 1 < n)
    def _():
        fetch(i + 1, lax.rem(i + 1, DISPATCH_SLOTS)).start()

    fetch(i, slot).wait()

    def start(tk, k, off):
        pltpu.make_async_copy(xin.at[slot, pl.ds(pl.multiple_of(tk * PACK_SUB, PACK_SUB), PACK_SUB)],
                              xs_hbm.at[pl.ds(off, PACK_SUB)], out_sem.at[slot]).start(priority=k)

    _for_each_assignment(off_ref, start)

    @pl.when(i > 0)
    def _():
        drain(lax.rem(i + DISPATCH_SLOTS - 1, DISPATCH_SLOTS))

    @pl.when(i == n - 1)
    def _():
        drain(slot)


def _dispatch(off3, h2p):
    return pl.pallas_call(
        _dispatch_kernel,
        grid=(off3.shape[0],),
        in_specs=[pl.BlockSpec((1, 1, TOP_K * ROW_TILE), lambda i: (i, 0, 0), memory_space=pltpu.SMEM),
                  pl.BlockSpec(memory_space=pl.ANY)],
        out_specs=pl.BlockSpec(memory_space=pl.ANY),
        out_shape=jax.ShapeDtypeStruct((TOP_K * h2p.shape[0], LANES), jnp.uint32),
        scratch_shapes=[pltpu.VMEM((DISPATCH_SLOTS, ROW_TILE * PACK_SUB, LANES), jnp.uint32),
                        pltpu.SemaphoreType.DMA((DISPATCH_SLOTS,)), pltpu.SemaphoreType.DMA((DISPATCH_SLOTS,))],
        compiler_params=pltpu.CompilerParams(dimension_semantics=("arbitrary",), has_side_effects=True),
        name="moe_dispatch",
    )(off3, h2p)


def _moe_kernel(vt_ref, ve_ref, vlo_ref, vhi_ref, vnext_ref, vpar_ref, xs_ref, wg_hbm, wu_hbm, wd_hbm, ys_ref,
                wgb, wub, wdb, wgf, wuf, wdf, sem):
    tm = MOE_TILE
    v = pl.program_id(0)
    t = vt_ref[v]
    e = ve_ref[v]
    slot = vpar_ref[v]
    prev = jnp.maximum(v - 1, 0)

    def fetch(expert, s):
        return [pltpu.make_async_copy(w_hbm.at[expert], buf.at[s], sem.at[s])
                for w_hbm, buf in ((wg_hbm, wgf), (wu_hbm, wuf), (wd_hbm, wdf))]

    @pl.when(v == 0)
    def _():
        for copy in fetch(e, slot):
            copy.start()

    @pl.when((v == 0) | (ve_ref[prev] != e))
    def _():
        for copy in fetch(e, slot):
            copy.wait()
        nxt = vnext_ref[v]

        @pl.when(nxt != e)
        def _():
            for copy in fetch(nxt, 1 - slot):
                copy.start()

        wgb[...] = wgf[slot].astype(BF16)
        wub[...] = wuf[slot].astype(BF16)
        wdb[...] = wdf[slot].astype(BF16)

    xb = _unpack_rows(xs_ref, tm)
    g = _dot(xb, wgb[...])
    u = _dot(xb, wub[...])
    hmid = (g * jax.nn.sigmoid(g)) * u
    words = _pack_words(_dot(hmid.astype(BF16), wdb[...]))
    first = (v == 0) | (vt_ref[prev] != t)

    @pl.when(first)
    def _():
        for c in range(PACK_SUB):
            ys_ref[_packed_chunk(c, tm)] = words[c]

    @pl.when(jnp.logical_not(first))
    def _():
        rows = t * tm + lax.broadcasted_iota(jnp.int32, (tm, LANES), 0)
        mine = (rows >= vlo_ref[v]) & (rows < vhi_ref[v])
        for c in range(PACK_SUB):
            ys_ref[_packed_chunk(c, tm)] = jnp.where(mine, words[c], ys_ref[_packed_chunk(c, tm)])


def _moe(plan, xs, wg, wu, wd):
    tm = MOE_TILE
    n_rows = xs.shape[0] // PACK_SUB
    tile = lambda v, vt, *_: (vt[v], 0)
    hbm = pl.BlockSpec(memory_space=pl.ANY)
    grid_spec = pltpu.PrefetchScalarGridSpec(
        num_scalar_prefetch=len(plan),
        grid=(plan[0].shape[0],),
        in_specs=[pl.BlockSpec((tm * PACK_SUB, LANES), tile), hbm, hbm, hbm],
        out_specs=pl.BlockSpec((tm * PACK_SUB, LANES), tile),
        scratch_shapes=[pltpu.VMEM((D_MODEL, D_EXPERT), BF16), pltpu.VMEM((D_MODEL, D_EXPERT), BF16),
                        pltpu.VMEM((D_EXPERT, D_MODEL), BF16),
                        pltpu.VMEM((2, D_MODEL, D_EXPERT), F32), pltpu.VMEM((2, D_MODEL, D_EXPERT), F32),
                        pltpu.VMEM((2, D_EXPERT, D_MODEL), F32), pltpu.SemaphoreType.DMA((2,))],
    )
    return pl.pallas_call(
        _moe_kernel,
        grid_spec=grid_spec,
        out_shape=jax.ShapeDtypeStruct((n_rows * PACK_SUB, LANES), jnp.uint32),
        compiler_params=pltpu.CompilerParams(dimension_semantics=("arbitrary",), vmem_limit_bytes=VMEM_LIMIT),
        name="moe_experts",
    )(*plan, xs, wg, wu, wd)


def _combine_kernel(off_ref, offn_ref, x1_ref, w_ref, g_ref, ys_hbm, o_ref, ybuf, sem):
    tc = ROW_TILE
    i = pl.program_id(0)
    n = pl.num_programs(0)
    slot = lax.rem(i, 2)

    def gather(o_ref_, s):
        def start(tk, k, off):
            pltpu.make_async_copy(ys_hbm.at[pl.ds(off, PACK_SUB)],
                                  ybuf.at[s, pl.ds(pl.multiple_of((k * tc + tk) * PACK_SUB, PACK_SUB), PACK_SUB)],
                                  sem.at[s]).start(priority=k)
        _for_each_assignment(o_ref_, start)

    def drain(s):
        pltpu.make_async_copy(ys_hbm.at[pl.ds(0, TOP_K * tc * PACK_SUB)], ybuf.at[s], sem.at[s]).wait()

    @pl.when(i == 0)
    def _():
        gather(off_ref, 0)

    @pl.when(i + 1 < n)
    def _():
        gather(offn_ref, 1 - slot)

    drain(slot)

    n_chunks = D_MODEL // LANES
    z = [x1_ref[:, c * LANES:(c + 1) * LANES] for c in range(n_chunks)]
    for k in range(TOP_K):
        wk = w_ref[:, k:k + 1]
        yk = _unpack_chunks(ybuf.at[slot], tc, first_row=k * tc)
        z = [zc + wk * yc for zc, yc in zip(z, yk)]
    ss = sum(jnp.sum(zc * zc, axis=-1, keepdims=True) for zc in z)
    inv = lax.rsqrt(ss * (1.0 / D_MODEL) + NORM_EPS)
    for c in range(n_chunks):
        o_ref[:, c * LANES:(c + 1) * LANES] = z[c] * inv * g_ref[:, c * LANES:(c + 1) * LANES]


def _combine(off3, x1, weight, g, ys):
    N = x1.shape[0]
    tc = ROW_TILE
    nt = N // tc
    idx = lambda f: pl.BlockSpec((1, 1, TOP_K * tc), f, memory_space=pltpu.SMEM)
    return pl.pallas_call(
        _combine_kernel,
        grid=(nt,),
        in_specs=[idx(lambda i: (i, 0, 0)),
                  idx(lambda i: (jnp.minimum(i + 1, nt - 1), 0, 0)),
                  pl.BlockSpec((tc, D_MODEL), lambda i: (i, 0)),
                  pl.BlockSpec((tc, TOP_K), lambda i: (i, 0)),
                  pl.BlockSpec((1, D_MODEL), lambda i: (0, 0)),
                  pl.BlockSpec(memory_space=pl.ANY)],
        out_specs=pl.BlockSpec((tc, D_MODEL), lambda i: (i, 0)),
        out_shape=jax.ShapeDtypeStruct((N, D_MODEL), F32),
        scratch_shapes=[pltpu.VMEM((2, TOP_K * tc * PACK_SUB, LANES), jnp.uint32), pltpu.SemaphoreType.DMA((2,))],
        compiler_params=pltpu.CompilerParams(dimension_semantics=("arbitrary",), vmem_limit_bytes=VMEM_LIMIT),
        name="combine_norm",
    )(off3, off3, x1, weight, g, ys)


def _block_diag(w):
    nb, c, _ = w.shape
    eye = jnp.eye(nb, dtype=w.dtype)
    return (eye[:, None, :, None] * w[:, :, None, :]).reshape(nb * c, nb * c)


def _visit_plan(counts, n_rows):
    tm = MOE_TILE
    n_tiles = n_rows // tm
    n_visits = n_tiles + N_EXPERTS - 1
    cnt = counts.astype(jnp.int32)
    lanes = jnp.arange(ROUTE_ROWS, dtype=jnp.int32)
    upto = (lanes[None, :] <= lanes[:, None]).astype(jnp.int32)
    ends = upto @ cnt
    starts = ends - cnt
    first_tile = starts // tm
    n_vis = jnp.where(cnt > 0, (ends - 1) // tm - first_tile + 1, 0)
    v_end = upto @ n_vis
    v_start = v_end - n_vis
    total = v_end[-1]
    v = jnp.arange(n_visits, dtype=jnp.int32)
    vc = jnp.minimum(v, total - 1)
    own = ((vc[:, None] >= v_start[None, :]) & (vc[:, None] < v_end[None, :])).astype(jnp.int32)
    pick = lambda per_lane: own @ per_lane
    tile = jnp.maximum(pick(first_tile - v_start) + vc, 0)
    lo = jnp.maximum(pick(starts), tile * tm)
    hi = jnp.minimum(pick(ends), (tile + 1) * tm)
    valid = v < total
    to_expert = lambda lane: jnp.maximum(lane - EXPERT_LANE0, 0)
    used = cnt > 0
    later = jnp.where((lanes[None, :] > lanes[:, None]) & used[None, :], lanes[None, :], ROUTE_ROWS)
    nxt = jnp.min(later, axis=1)
    nxt = jnp.where(nxt == ROUTE_ROWS, lanes, nxt)
    parity = (upto @ used.astype(jnp.int32) - 1) % 2
    plan = (tile, to_expert(pick(lanes)), jnp.where(valid, lo, 0), jnp.where(valid, hi, 0),
            to_expert(pick(nxt)), pick(parity))
    return starts, plan


def kernel(x, norm_mix_g, w_in, lambda_qk, subln_g, conv_w, conv_b, w_r, b_r, w_i, b_i, lru_lambda, w_o_attn, w_o_lru, w_out, norm_ffn_g, w_group, w_expert_router, w_gate, w_up, w_down, final_norm_g):
    B, S, D = x.shape
    N = B * S
    nt = S // SEQ_TILE
    depth = norm_mix_g.shape[0]
    assert depth == 1 and D == D_MODEL and S % SEQ_TILE == 0 and S % LRU_TILE == 0
    assert N % ROW_TILE == 0 and (N * TOP_K) % MOE_TILE == 0
    l = 0
    row = lambda v: v.reshape(1, -1).astype(F32)

    x2 = x.reshape(N, D)
    w_in_l = w_in[l].astype(BF16)
    qT, k, vT, xr, yr, kn = _inproj(x2, row(norm_mix_g[l]), w_in_l, B, S)

    kn_max = jnp.max(kn[:, 0, :2 * N_HEADS].reshape(B, nt, 2 * N_HEADS), axis=1)
    attn = _attn(kn_max.reshape(-1), qT, k.reshape(B, nt, SEQ_TILE, N_HEADS * K_COLS), vT,
                 lambda_qk[l].reshape(4, HEAD_DIM).astype(F32), row(subln_g[l]), B, S)

    lru = _lru(xr.reshape(B, S, LRU_WIDTH), yr.reshape(B, S, LRU_WIDTH), conv_w[l].astype(F32), row(conv_b[l]),
               _block_diag(w_r[l]).astype(BF16), _block_diag(w_i[l]).astype(BF16), row(b_r[l]), row(b_i[l]),
               row(lru_lambda[l]), B, S)

    w_route = jnp.concatenate(
        [w_group[l], jnp.transpose(w_expert_router[l], (1, 0, 2)).reshape(D, N_EXPERTS),
         jnp.zeros((D, ROUTE_LANES - N_GROUPS - N_EXPERTS), F32)], axis=1).astype(F32)
    w_route_hi = w_route.astype(BF16)
    w_route = jnp.concatenate([w_route_hi, (w_route - w_route_hi.astype(F32)).astype(BF16)], axis=1).T
    x1, h2p, route, counts = _merge(x2, attn.reshape(N, ATTN_WIDTH), lru.reshape(N, LRU_WIDTH), row(norm_mix_g[l]),
                                    w_in_l, w_o_attn[l].astype(BF16),
                                    w_o_lru[l].astype(BF16), w_out[l].astype(BF16), row(norm_ffn_g[l]), w_route)

    starts, plan = _visit_plan(counts[:, 0], N * TOP_K)
    lane = route[0:TOP_K].astype(jnp.int32)
    weight = jnp.transpose(route[TOP_K:2 * TOP_K])
    rank = route[2 * TOP_K:3 * TOP_K].astype(jnp.int32)
    lane_ids = jnp.arange(ROUTE_ROWS, dtype=jnp.int32)
    start_of = jnp.sum(jnp.where(lane[..., None] == lane_ids, starts, 0), axis=-1)
    off = ((start_of + rank) * PACK_SUB).reshape(TOP_K, N // ROW_TILE, ROW_TILE)
    off3 = jnp.transpose(off, (1, 0, 2)).reshape(N // ROW_TILE, 1, TOP_K * ROW_TILE)

    xs = _dispatch(off3, h2p)
    ys = _moe(plan, xs, w_gate[l], w_up[l], w_down[l])
    out = _combine(off3, x1, weight, row(final_norm_g), ys)
    return out.reshape(B, S, D)
```

```python
import functools
import math

import jax
import jax.numpy as jnp
from jax import lax
from jax.experimental import pallas as pl
from jax.experimental.pallas import tpu as pltpu

F32 = jnp.float32
BF16 = jnp.bfloat16

D_MODEL = 1024
N_HEADS = 4
HEAD_DIM = 64
V_DIM = 2 * HEAD_DIM
ATTN_WIDTH = N_HEADS * V_DIM
V_ROWS = V_DIM + 16
K_COLS = 2 * V_DIM
POS_SPLIT = 3
POS_RADIX = 256
SKIP_MARGIN = 138.0
NORM_SLACK = 1.0201
LRU_WIDTH = D_MODEL // 2
LRU_BLOCKS = 8
CONV_W = 4
LRU_C = 8.0
N_GROUPS = 4
EXPERTS_PER_GROUP = 8
N_EXPERTS = N_GROUPS * EXPERTS_PER_GROUP
TOP_K = 2
D_EXPERT = D_MODEL // 2
NORM_EPS = 1e-6
LAM_INIT = 0.8 - 0.6 * math.exp(-0.3 * 0)

QK_COLS = N_HEADS * 2 * HEAD_DIM
PROJ_COLS = 2 * QK_COLS + ATTN_WIDTH + 2 * LRU_WIDTH
ROUTE_LANES = 128
ROUTE_ROWS = 64
ROUTE_FIELDS = 8
NEG_BIG = -1e30
LOG2E = math.log2(math.e)
ALIBI_SLOPES = tuple(2.0 ** (-8.0 * (h + 1) / N_HEADS) for h in range(N_HEADS))

SEQ_TILE = 512
ROW_TILE = 256
MERGE_CHAIN = 512
MOE_TILE = 256
DMA_GROUP = 8
DISPATCH_SLOTS = 3
LANES = 128
SUBLANES = 8
PACK_SUB = D_MODEL // 2 // LANES
EXPERT_LANE0 = N_GROUPS
V7X_VMEM_BYTES = 64 * 1024 * 1024
VMEM_LIMIT = V7X_VMEM_BYTES * 3 // 4


def _rms(x, g):
    return x * lax.rsqrt(jnp.mean(x * x, axis=-1, keepdims=True) + NORM_EPS) * g


def _dot(a, b):
    return jnp.dot(a, b, preferred_element_type=F32)


def _inproj_kernel(tiles_per_seq, x_ref, g_ref, w_ref, cw_ref, cb_ref, wr_ref, wi_ref, br_ref, bi_ref, lam_ref,
                   qT_ref, k_ref, vT_ref, lru_ref, kn_ref, xbuf, hc):
    ti = lax.rem(pl.program_id(0), tiles_per_seq)

    @pl.when(ti == 0)
    def _():
        xbuf[0:8] = jnp.zeros((8, LRU_WIDTH), F32)
        hc[...] = jnp.zeros_like(hc)

    hb = _rms(x_ref[...], g_ref[...]).astype(BF16)

    def proj(lo, hi):
        return _dot(hb, w_ref[:, lo:hi])

    c0 = 2 * QK_COLS + ATTN_WIDTH
    lru_x = proj(c0, c0 + LRU_WIDTH)
    lru_y = proj(c0 + LRU_WIDTH, c0 + 2 * LRU_WIDTH)

    q = proj(0, QK_COLS) * (HEAD_DIM ** -0.5 * LOG2E)
    for h in range(N_HEADS):
        qT_ref[h] = q[:, h * V_DIM:(h + 1) * V_DIM].T.astype(BF16)
    k = proj(QK_COLS, 2 * QK_COLS).astype(BF16)
    r = lax.broadcasted_iota(jnp.int32, (k.shape[0], K_COLS - V_DIM), 0)
    lane = lax.broadcasted_iota(jnp.int32, r.shape, 1)
    a = r // POS_RADIX * POS_RADIX
    feat = jnp.where(lane < POS_SPLIT, a, jnp.where(lane < 2 * POS_SPLIT, r - a, 0)).astype(F32).astype(BF16)
    for h in range(N_HEADS):
        k_ref[:, h * K_COLS:h * K_COLS + V_DIM] = k[:, h * V_DIM:(h + 1) * V_DIM]
        k_ref[:, h * K_COLS + V_DIM:(h + 1) * K_COLS] = feat
    seg = lax.broadcasted_iota(jnp.int32, (QK_COLS, LANES), 0) // HEAD_DIM
    pick = jnp.where(seg == lax.broadcasted_iota(jnp.int32, (QK_COLS, LANES), 1), 1.0, 0.0).astype(BF16)
    row_n2 = _dot(jnp.square(k.astype(F32)).astype(BF16), pick)
    kn_ref[...] = jnp.max(row_n2, axis=0, keepdims=True)
    v = proj(2 * QK_COLS, 2 * QK_COLS + ATTN_WIDTH)
    for h in range(N_HEADS):
        vT_ref[h, :V_DIM, :] = v[:, h * V_DIM:(h + 1) * V_DIM].T.astype(BF16)
        pad_row = lax.broadcasted_iota(jnp.int32, (V_ROWS - V_DIM, v.shape[0]), 0)
        vT_ref[h, V_DIM:, :] = jnp.where(pad_row == 0, 1.0, 0.0).astype(BF16)

    lru_ref[...] = _lru_tile(lru_x, lru_y, ti, cw_ref, cb_ref, wr_ref, wi_ref, br_ref, bi_ref, lam_ref, xbuf, hc)


def _inproj(x2, g, w, lru_params, B, S):
    N = B * S
    tm = SEQ_TILE
    nt = S // tm
    tile5 = pl.BlockSpec((None, N_HEADS, None, V_DIM, tm), lambda i: (i // nt, 0, i % nt, 0, 0))
    rows = lambda c: pl.BlockSpec((tm, c), lambda i: (i, 0))
    full = lambda a: pl.BlockSpec(a.shape, lambda i: (0, 0))
    return pl.pallas_call(
        functools.partial(_inproj_kernel, nt),
        grid=(N // tm,),
        in_specs=[rows(D_MODEL),
                  pl.BlockSpec((1, D_MODEL), lambda i: (0, 0)),
                  pl.BlockSpec((D_MODEL, PROJ_COLS), lambda i: (0, 0))] + [full(a) for a in lru_params],
        out_specs=[tile5, rows(N_HEADS * K_COLS),
                   pl.BlockSpec((None, N_HEADS, None, V_ROWS, tm), lambda i: (i // nt, 0, i % nt, 0, 0)),
                   rows(LRU_WIDTH),
                   pl.BlockSpec((None, 1, LANES), lambda i: (i, 0, 0))],
        out_shape=[jax.ShapeDtypeStruct((B, N_HEADS, nt, V_DIM, tm), BF16),
                   jax.ShapeDtypeStruct((N, N_HEADS * K_COLS), BF16),
                   jax.ShapeDtypeStruct((B, N_HEADS, nt, V_ROWS, tm), BF16),
                   jax.ShapeDtypeStruct((N, LRU_WIDTH), BF16),
                   jax.ShapeDtypeStruct((N // tm, 1, LANES), F32)],
        scratch_shapes=[pltpu.VMEM((tm + 8, LRU_WIDTH), F32), pltpu.VMEM((1, LRU_WIDTH), F32)],
        compiler_params=pltpu.CompilerParams(dimension_semantics=("arbitrary",),
                                             vmem_limit_bytes=VMEM_LIMIT),
        name="inproj_lru",
    )(x2, g, w, *lru_params)


def _attn_kernel(kn_ref, qT_ref, k_ref, vT_ref, lam_ref, g_ref, o_ref, mask_ref, acc_ref, sa_ref, sb_ref, pa_ref,
                 pb_ref):
    t = SEQ_TILE
    b = pl.program_id(0)
    h = pl.program_id(1)
    i = pl.program_id(2)
    slope = LOG2E * jnp.where(h == 0, ALIBI_SLOPES[0], jnp.where(h == 1, ALIBI_SLOPES[1],
                              jnp.where(h == 2, ALIBI_SLOPES[2], ALIBI_SLOPES[3]))).astype(F32)

    @pl.when(i == 0)
    def _():
        r = lax.broadcasted_iota(jnp.int32, (t, t), 0)
        c = lax.broadcasted_iota(jnp.int32, (t, t), 1)
        mask_ref[...] = jnp.where(r <= c, 0.0, NEG_BIG)

    qf = qT_ref[...].astype(F32)
    row = lax.broadcasted_iota(jnp.int32, qf.shape, 0)
    sl = jnp.full(qf.shape, slope, F32)
    hi = sl.astype(BF16).astype(F32)
    mid = (sl - hi).astype(BF16).astype(F32)
    lo = (sl - hi - mid).astype(BF16).astype(F32)
    piece = jnp.where(row % 3 == 0, hi, jnp.where(row % 3 == 1, mid, lo))
    srows = jnp.where(row < 2 * POS_SPLIT, piece, 0.0).astype(BF16)
    qs = tuple(jnp.concatenate([jnp.where(sel, qf, 0.0).astype(BF16), srows], axis=0)
               for sel in (row < HEAD_DIM, row >= HEAD_DIM))
    acc_ref[...] = jnp.zeros_like(acc_ref)
    pb_ref[...] = jnp.zeros_like(pb_ref)

    first_off = [0]

    def key_tile(tau):
        return jnp.where(tau <= 0, i, first_off[0] + tau - 1)

    def stage_q(tau, s_ref, diagonal=False):
        kt = k_ref[key_tile(tau)]
        tile_max = []
        for mi in range(2):
            s = _dot(kt, qs[mi])
            if diagonal:
                s = s + mask_ref[...]
            s_ref[mi] = s
            tile_max.append(jnp.max(s, axis=0, keepdims=True))
        return tuple(tile_max)

    def stage_s(tau, s_ref, p_ref, ms, tile_max):
        cj = slope * (key_tile(tau) * t).astype(F32)
        m_out, alphas = [], []
        for mi in range(2):
            m_new = jnp.maximum(ms[mi], tile_max[mi] + cj)
            alphas.append(jnp.exp2(ms[mi] - m_new))
            p_ref[mi] = jnp.exp2(s_ref[mi] - (m_new - cj)).astype(BF16)
            m_out.append(m_new)
        return tuple(m_out), tuple(alphas)

    def stage_v(tau, p_ref, alphas):
        vt = vT_ref[key_tile(tau)]
        for mi in range(2):
            acc_ref[mi] = alphas[mi] * acc_ref[mi] + _dot(vt, p_ref[mi])

    def body(jj, carry):
        ms, alphas, tmax = carry[:2], carry[2:4], carry[4:]
        tau = 2 * jj
        tmax_b = stage_q(tau + 1, sb_ref)
        ms, alphas_a = stage_s(tau, sa_ref, pa_ref, ms, tmax)
        stage_v(tau - 1, pb_ref, alphas)
        tmax_a = stage_q(tau + 2, sa_ref)
        ms, alphas_b = stage_s(tau + 1, sb_ref, pb_ref, ms, tmax_b)
        stage_v(tau, pa_ref, alphas_a)
        return ms + alphas_b + tmax_a

    def finalize():
        lp = lam_ref[...]
        s1 = jnp.sum(lp[0:1] * lp[1:2], axis=-1, keepdims=True)
        s2 = jnp.sum(lp[2:3] * lp[3:4], axis=-1, keepdims=True)
        lam = jnp.exp(s1) - jnp.exp(s2) + LAM_INIT
        norm = [acc_ref[mi, :V_DIM, :] * (1.0 / acc_ref[mi, V_DIM:V_DIM + 1, :]) for mi in range(2)]
        oT = norm[0] - lam * norm[1]
        o = _rms(oT.T, g_ref[...]) * (1.0 - LAM_INIT)
        o_ref[...] = o.astype(BF16)

    m_init = jnp.full((1, t), NEG_BIG, F32)
    one = jnp.ones((1, t), F32)
    tmax0 = stage_q(0, sa_ref, diagonal=True)

    cj0 = slope * (i * t).astype(F32)
    n_skip = None
    for mi in range(2):
        qsq = jnp.square(qf[mi * HEAD_DIM:(mi + 1) * HEAD_DIM, :])
        qn2 = jnp.max(jnp.sum(qsq, axis=0, keepdims=True), axis=1, keepdims=True)
        m_low = jnp.min(tmax0[mi], axis=1, keepdims=True) + cj0
        qk = jnp.sqrt(qn2 * (kn_ref[(b * N_HEADS + h) * 2 + mi] * NORM_SLACK))
        count = jnp.ceil((m_low - SKIP_MARGIN - slope * (t - 1) - qk) / (slope * t))
        n_skip = count if n_skip is None else jnp.minimum(n_skip, count)
    first_off[0] = jnp.clip(n_skip, 0.0, i.astype(F32)).astype(jnp.int32)[0, 0]
    n_off = i - first_off[0]

    n_main = jnp.right_shift(n_off, 1)
    fin = lax.fori_loop(0, n_main, body, (m_init, m_init, one, one) + tmax0)
    ms, alphas, tmax = fin[:2], fin[2:4], fin[4:]
    tau = 2 * n_main
    odd_tiles = tau == n_off

    @pl.when(odd_tiles)
    def _():
        _, alphas_a = stage_s(tau, sa_ref, pa_ref, ms, tmax)
        stage_v(tau - 1, pb_ref, alphas)
        stage_v(tau, pa_ref, alphas_a)
        finalize()

    @pl.when(jnp.logical_not(odd_tiles))
    def _():
        tmax_b = stage_q(tau + 1, sb_ref)
        ms_a, alphas_a = stage_s(tau, sa_ref, pa_ref, ms, tmax)
        stage_v(tau - 1, pb_ref, alphas)
        _, alphas_b = stage_s(tau + 1, sb_ref, pb_ref, ms_a, tmax_b)
        stage_v(tau, pa_ref, alphas_a)
        stage_v(tau + 1, pb_ref, alphas_b)
        finalize()


def _attn(kn, qT, k4, vT, lam, g, B, S):
    t = SEQ_TILE
    nt = S // t
    grid_spec = pltpu.PrefetchScalarGridSpec(
        num_scalar_prefetch=1,
        grid=(B, N_HEADS, nt),
        in_specs=[pl.BlockSpec((None, None, None, V_DIM, t), lambda b, h, i, kn: (b, h, i, 0, 0)),
                  pl.BlockSpec((None, nt, t, K_COLS), lambda b, h, i, kn: (b, 0, 0, h)),
                  pl.BlockSpec((None, None, nt, V_ROWS, t), lambda b, h, i, kn: (b, h, 0, 0, 0)),
                  pl.BlockSpec((4, HEAD_DIM), lambda b, h, i, kn: (0, 0)),
                  pl.BlockSpec((1, V_DIM), lambda b, h, i, kn: (0, 0))],
        out_specs=pl.BlockSpec((None, t, V_DIM), lambda b, h, i, kn: (b, i, h)),
        scratch_shapes=[pltpu.VMEM((t, t), F32), pltpu.VMEM((2, V_ROWS, t), F32),
                        pltpu.VMEM((2, t, t), F32), pltpu.VMEM((2, t, t), F32),
                        pltpu.VMEM((2, t, t), BF16), pltpu.VMEM((2, t, t), BF16)],
    )
    return pl.pallas_call(
        _attn_kernel,
        grid_spec=grid_spec,
        out_shape=jax.ShapeDtypeStruct((B, S, ATTN_WIDTH), BF16),
        compiler_params=pltpu.CompilerParams(dimension_semantics=("parallel", "arbitrary", "arbitrary"),
                                             vmem_limit_bytes=VMEM_LIMIT),
        name="diff_attn",
    )(kn, qT, k4, vT, lam, g)


def _lru_tile(x, y, ti, cw_ref, cb_ref, wr_ref, wi_ref, br_ref, bi_ref, lam_ref, xbuf, hc):
    T = SEQ_TILE
    xbuf[8:8 + T] = x
    cw = cw_ref[...]
    xc = cb_ref[...] + cw[3:4] * x
    for j in range(CONV_W - 1):
        xc = xc + cw[j:j + 1] * xbuf[5 + j:5 + j + T]
    xbuf[0:8] = x[T - 8:T]

    xb = xc.astype(BF16)
    sigmoid = lambda z: 0.5 * jnp.tanh(0.5 * z) + 0.5
    r = sigmoid(_dot(xb, wr_ref[...]) + br_ref[...])
    ig = sigmoid(_dot(xb, wi_ref[...]) + bi_ref[...])
    z = -lam_ref[...]
    softplus = jnp.maximum(z, 0.0) + jnp.log1p(jnp.exp(-jnp.abs(z)))
    la = -LRU_C * r * softplus
    a = jnp.exp(la)
    m2 = -jnp.tanh(la) * (a * a + 1.0)
    mult = jnp.where(m2 > 0.0, m2 * lax.rsqrt(m2), 0.0)
    row = lax.broadcasted_iota(jnp.int32, (T, LRU_WIDTH), 0)
    mult = jnp.where((row == 0) & (ti == 0), 1.0, mult)
    u = (xc * ig) * mult

    nb = T // SUBLANES
    a3 = a.reshape(nb, SUBLANES, LRU_WIDTH)
    u3 = u.reshape(nb, SUBLANES, LRU_WIDTH)
    sub = lax.broadcasted_iota(jnp.int32, a3.shape, 1)
    d = 1
    while d < SUBLANES:
        valid = sub >= d
        u3 = jnp.where(valid, a3 * pltpu.roll(u3, d, 1) + u3, u3)
        a3 = jnp.where(valid, a3 * pltpu.roll(a3, d, 1), a3)
        d *= 2
    h = hc[...]
    blocks = []
    for b in range(nb):
        hb = u3[b] + a3[b] * h
        blocks.append(hb)
        h = hb[SUBLANES - 1:SUBLANES]
    hfull = jnp.concatenate(blocks, axis=0)
    hc[...] = h
    gelu = 0.5 * y * (1.0 + jnp.tanh(0.7978845608028654 * (y + 0.044715 * (y * y * y))))
    return (hfull * gelu).astype(BF16)


def _pack_words(v):
    bits = pltpu.bitcast(v.astype(BF16).astype(F32), jnp.uint32)
    half = D_MODEL // 2
    packed = (bits[:, :half] >> 16) | (bits[:, half:] & jnp.uint32(0xFFFF0000))
    return [packed[:, c * LANES:(c + 1) * LANES] for c in range(PACK_SUB)]


def _packed_chunk(c, rows, first_row=0):
    return (pl.ds(first_row * PACK_SUB + c, rows, stride=PACK_SUB), slice(None))


def _pack_rows(v, out_ref, first_row=0):
    for c, words in enumerate(_pack_words(v)):
        out_ref[_packed_chunk(c, v.shape[0], first_row)] = words


def _unpack_chunks(in_ref, rows, first_row=0):
    lo, hi = [], []
    for c in range(PACK_SUB):
        w = in_ref[_packed_chunk(c, rows, first_row)]
        lo.append(pltpu.bitcast(w << 16, F32))
        hi.append(pltpu.bitcast(w & jnp.uint32(0xFFFF0000), F32))
    return lo + hi


def _unpack_rows(in_ref, rows):
    return jnp.concatenate(_unpack_chunks(in_ref, rows), axis=1).astype(BF16)


def _merge_kernel(x_ref, attn_ref, lru_ref, g1_ref, wg_ref, woa_ref, wol_ref, wout_ref, g2_ref, wrt_ref,
                  x1_ref, h2p_ref, route_ref, counts_ref, cnt, tri):
    tm = MERGE_CHAIN

    @pl.when(pl.program_id(0) == 0)
    def _():
        cnt[...] = jnp.zeros_like(cnt)
        r = lax.broadcasted_iota(jnp.int32, (tm, tm), 0)
        c = lax.broadcasted_iota(jnp.int32, (tm, tm), 1)
        tri[...] = jnp.where(r < c, 1.0, 0.0).astype(BF16)

    counts = cnt[...]
    for r0 in range(0, SEQ_TILE, tm):
        counts = _merge_chain(r0, tm, counts, x_ref, attn_ref, lru_ref, g1_ref, wg_ref, woa_ref, wol_ref, wout_ref,
                              g2_ref, wrt_ref, x1_ref, h2p_ref, route_ref, tri)
    cnt[...] = counts
    counts_ref[...] = counts


def _merge_chain(r0, tm, counts, x_ref, attn_ref, lru_ref, g1_ref, wg_ref, woa_ref, wol_ref, wout_ref, g2_ref, wrt_ref,
                 x1_ref, h2p_ref, route_ref, tri):
    rs = slice(r0, r0 + tm)
    x = x_ref[rs, :]
    hb = _rms(x, g1_ref[...]).astype(BF16)
    gates = 0.5 * jnp.tanh(0.5 * _dot(hb, wg_ref[:, PROJ_COLS:])) + 0.5
    merged = gates[:, :D_MODEL] * _dot(attn_ref[rs, :], woa_ref[...]) + gates[:, D_MODEL:] * _dot(lru_ref[rs, :], wol_ref[...])
    x1 = x + _dot(merged.astype(BF16), wout_ref[...])
    x1_ref[rs, :] = x1
    h2 = _rms(x1, g2_ref[...])
    _pack_rows(h2, h2p_ref, first_row=r0)

    h_hi = h2.astype(BF16)
    h_lo = (h2 - h_hi.astype(F32)).astype(BF16)
    wrt = wrt_ref[...]
    nt_dims = (((1,), (1,)), ((), ()))
    hh = lax.dot_general(wrt, h_hi, nt_dims, preferred_element_type=F32)
    lo_pass = lax.dot_general(wrt[:ROUTE_LANES], h_lo, nt_dims, preferred_element_type=F32)
    logits = (hh[:ROUTE_LANES] + hh[ROUTE_LANES:] + lo_pass)[:ROUTE_ROWS]
    row = lax.broadcasted_iota(jnp.int32, logits.shape, 0)
    big = jnp.int32(1 << 20)

    def first_argmax(v):
        m = jnp.max(v, axis=0, keepdims=True)
        return m, jnp.min(jnp.where(v == m, row, big), axis=0, keepdims=True)

    gmask = row < N_GROUPS
    gmax, gidx = first_argmax(jnp.where(gmask, logits, -jnp.inf))
    gsum = jnp.sum(jnp.where(gmask, jnp.exp(logits - gmax), 0.0), axis=0, keepdims=True)
    g_w = 1.0 / gsum
    lo = N_GROUPS + EXPERTS_PER_GROUP * gidx
    el = jnp.where((row >= lo) & (row < lo + EXPERTS_PER_GROUP), logits, -jnp.inf)
    m1, i1 = first_argmax(el)
    m2, i2 = first_argmax(jnp.where(row == i1, -jnp.inf, el))
    rr = jnp.exp(m2 - m1)
    w1 = g_w / (1.0 + rr)
    w2 = g_w * rr / (1.0 + rr)
    oh1 = row == i1
    oh2 = row == i2
    oh = jnp.where(oh1 | oh2, 1.0, 0.0)
    before = _dot(oh.astype(BF16), tri[...]) + counts
    r1 = jnp.sum(jnp.where(oh1, before, 0.0), axis=0, keepdims=True)
    r2 = jnp.sum(jnp.where(oh2, before, 0.0), axis=0, keepdims=True)
    vals = (i1.astype(F32), i2.astype(F32), w1, w2, r1, r2, jnp.zeros_like(w1), jnp.zeros_like(w1))
    for k, v in enumerate(vals):
        route_ref[k:k + 1, rs] = v
    return counts + jnp.sum(oh, axis=1, keepdims=True)


def _merge(x2, attn, lru, g1, wg, woa, wol, wout, g2, wrt):
    N = x2.shape[0]
    tm = SEQ_TILE
    rows = lambda c: pl.BlockSpec((tm, c), lambda i: (i, 0))
    full = lambda r, c: pl.BlockSpec((r, c), lambda i: (0, 0))
    return pl.pallas_call(
        _merge_kernel,
        grid=(N // tm,),
        in_specs=[rows(D_MODEL), rows(ATTN_WIDTH), rows(LRU_WIDTH), full(1, D_MODEL),
                  full(D_MODEL, PROJ_COLS + 2 * D_MODEL),
                  full(ATTN_WIDTH, D_MODEL), full(LRU_WIDTH, D_MODEL), full(D_MODEL, D_MODEL), full(1, D_MODEL),
                  full(2 * ROUTE_LANES, D_MODEL)],
        out_specs=[rows(D_MODEL), pl.BlockSpec((tm * PACK_SUB, LANES), lambda i: (i, 0)),
                   pl.BlockSpec((ROUTE_FIELDS, tm), lambda i: (0, i)), full(ROUTE_ROWS, 1)],
        out_shape=[jax.ShapeDtypeStruct((N, D_MODEL), F32), jax.ShapeDtypeStruct((N * PACK_SUB, LANES), jnp.uint32),
                   jax.ShapeDtypeStruct((ROUTE_FIELDS, N), F32), jax.ShapeDtypeStruct((ROUTE_ROWS, 1), F32)],
        scratch_shapes=[pltpu.VMEM((ROUTE_ROWS, 1), F32), pltpu.VMEM((MERGE_CHAIN, MERGE_CHAIN), BF16)],
        compiler_params=pltpu.CompilerParams(dimension_semantics=("arbitrary",), vmem_limit_bytes=VMEM_LIMIT),
        name="merge_route",
    )(x2, attn, lru, g1, wg, woa, wol, wout, g2, wrt)


def _for_each_assignment(off_ref, fn):
    def group(gi, _):
        toks = [gi * DMA_GROUP + j for j in range(DMA_GROUP)]
        offs = [[off_ref[0, 0, k * ROW_TILE + tk] for k in range(TOP_K)] for tk in toks]
        for tk, o in zip(toks, offs):
            for k in range(TOP_K):
                fn(tk, k, pl.multiple_of(o[k], PACK_SUB))
        return 0
    lax.fori_loop(0, ROW_TILE // DMA_GROUP, group, 0)


def _dispatch_kernel(off_ref, h2p_hbm, xs_hbm, xin, in_sem, out_sem):
    tile_rows = ROW_TILE * PACK_SUB
    i = pl.program_id(0)
    n = pl.num_programs(0)
    slot = lax.rem(i, DISPATCH_SLOTS)

    def fetch(tile, s):
        return pltpu.make_async_copy(h2p_hbm.at[pl.ds(pl.multiple_of(tile * tile_rows, tile_rows), tile_rows)],
                                     xin.at[s], in_sem.at[s])

    def drain(s):
        for _ in range(TOP_K):
            pltpu.make_async_copy(xin.at[s], xs_hbm.at[pl.ds(0, tile_rows)], out_sem.at[s]).wait()

    @pl.when(i == 0)
    def _():
        fetch(0, 0).start()

    @pl.when(i + 1 < n)
    def _():
        fetch(i + 1, lax.rem(i + 1, DISPATCH_SLOTS)).start()

    fetch(i, slot).wait()

    def start(tk, k, off):
        pltpu.make_async_copy(xin.at[slot, pl.ds(pl.multiple_of(tk * PACK_SUB, PACK_SUB), PACK_SUB)],
                              xs_hbm.at[pl.ds(off, PACK_SUB)], out_sem.at[slot]).start(priority=k)

    _for_each_assignment(off_ref, start)

    @pl.when(i > 0)
    def _():
        drain(lax.rem(i + DISPATCH_SLOTS - 1, DISPATCH_SLOTS))

    @pl.when(i == n - 1)
    def _():
        drain(slot)


def _dispatch(off3, h2p):
    return pl.pallas_call(
        _dispatch_kernel,
        grid=(off3.shape[0],),
        in_specs=[pl.BlockSpec((1, 1, TOP_K * ROW_TILE), lambda i: (i, 0, 0), memory_space=pltpu.SMEM),
                  pl.BlockSpec(memory_space=pl.ANY)],
        out_specs=pl.BlockSpec(memory_space=pl.ANY),
        out_shape=jax.ShapeDtypeStruct((TOP_K * h2p.shape[0], LANES), jnp.uint32),
        scratch_shapes=[pltpu.VMEM((DISPATCH_SLOTS, ROW_TILE * PACK_SUB, LANES), jnp.uint32),
                        pltpu.SemaphoreType.DMA((DISPATCH_SLOTS,)), pltpu.SemaphoreType.DMA((DISPATCH_SLOTS,))],
        compiler_params=pltpu.CompilerParams(dimension_semantics=("arbitrary",), has_side_effects=True),
        name="moe_dispatch",
    )(off3, h2p)


def _moe_kernel(vt_ref, ve_ref, vlo_ref, vhi_ref, vnext_ref, vpar_ref, xs_ref, wg_hbm, wu_hbm, wd_hbm, ys_ref,
                wgb, wub, wdb, wgf, wuf, wdf, sem):
    tm = MOE_TILE
    v = pl.program_id(0)
    t = vt_ref[v]
    e = ve_ref[v]
    slot = vpar_ref[v]
    prev = jnp.maximum(v - 1, 0)

    def fetch(expert, s):
        return [pltpu.make_async_copy(w_hbm.at[expert], buf.at[s], sem.at[s])
                for w_hbm, buf in ((wg_hbm, wgf), (wu_hbm, wuf), (wd_hbm, wdf))]

    @pl.when(v == 0)
    def _():
        for copy in fetch(e, slot):
            copy.start()

    @pl.when((v == 0) | (ve_ref[prev] != e))
    def _():
        for copy in fetch(e, slot):
            copy.wait()
        nxt = vnext_ref[v]

        @pl.when(nxt != e)
        def _():
            for copy in fetch(nxt, 1 - slot):
                copy.start()

        wgb[...] = wgf[slot].astype(BF16)
        wub[...] = wuf[slot].astype(BF16)
        wdb[...] = wdf[slot].astype(BF16)

    xb = _unpack_rows(xs_ref, tm)
    g = _dot(xb, wgb[...])
    u = _dot(xb, wub[...])
    hmid = (g * jax.nn.sigmoid(g)) * u
    words = _pack_words(_dot(hmid.astype(BF16), wdb[...]))
    first = (v == 0) | (vt_ref[prev] != t)

    @pl.when(first)
    def _():
        for c in range(PACK_SUB):
            ys_ref[_packed_chunk(c, tm)] = words[c]

    @pl.when(jnp.logical_not(first))
    def _():
        rows = t * tm + lax.broadcasted_iota(jnp.int32, (tm, LANES), 0)
        mine = (rows >= vlo_ref[v]) & (rows < vhi_ref[v])
        for c in range(PACK_SUB):
            ys_ref[_packed_chunk(c, tm)] = jnp.where(mine, words[c], ys_ref[_packed_chunk(c, tm)])


def _moe(plan, xs, wg, wu, wd):
    tm = MOE_TILE
    n_rows = xs.shape[0] // PACK_SUB
    tile = lambda v, vt, *_: (vt[v], 0)
    hbm = pl.BlockSpec(memory_space=pl.ANY)
    grid_spec = pltpu.PrefetchScalarGridSpec(
        num_scalar_prefetch=len(plan),
        grid=(plan[0].shape[0],),
        in_specs=[pl.BlockSpec((tm * PACK_SUB, LANES), tile), hbm, hbm, hbm],
        out_specs=pl.BlockSpec((tm * PACK_SUB, LANES), tile),
        scratch_shapes=[pltpu.VMEM((D_MODEL, D_EXPERT), BF16), pltpu.VMEM((D_MODEL, D_EXPERT), BF16),
                        pltpu.VMEM((D_EXPERT, D_MODEL), BF16),
                        pltpu.VMEM((2, D_MODEL, D_EXPERT), F32), pltpu.VMEM((2, D_MODEL, D_EXPERT), F32),
                        pltpu.VMEM((2, D_EXPERT, D_MODEL), F32), pltpu.SemaphoreType.DMA((2,))],
    )
    return pl.pallas_call(
        _moe_kernel,
        grid_spec=grid_spec,
        out_shape=jax.ShapeDtypeStruct((n_rows * PACK_SUB, LANES), jnp.uint32),
        compiler_params=pltpu.CompilerParams(dimension_semantics=("arbitrary",), vmem_limit_bytes=VMEM_LIMIT),
        name="moe_experts",
    )(*plan, xs, wg, wu, wd)


def _combine_kernel(off_ref, offn_ref, x1_ref, w_ref, g_ref, ys_hbm, o_ref, ybuf, sem):
    tc = ROW_TILE
    i = pl.program_id(0)
    n = pl.num_programs(0)
    slot = lax.rem(i, 2)

    def gather(o_ref_, s):
        def start(tk, k, off):
            pltpu.make_async_copy(ys_hbm.at[pl.ds(off, PACK_SUB)],
                                  ybuf.at[s, pl.ds(pl.multiple_of((k * tc + tk) * PACK_SUB, PACK_SUB), PACK_SUB)],
                                  sem.at[s]).start(priority=k)
        _for_each_assignment(o_ref_, start)

    def drain(s):
        pltpu.make_async_copy(ys_hbm.at[pl.ds(0, TOP_K * tc * PACK_SUB)], ybuf.at[s], sem.at[s]).wait()

    @pl.when(i == 0)
    def _():
        gather(off_ref, 0)

    @pl.when(i + 1 < n)
    def _():
        gather(offn_ref, 1 - slot)

    drain(slot)

    n_chunks = D_MODEL // LANES
    z = [x1_ref[:, c * LANES:(c + 1) * LANES] for c in range(n_chunks)]
    for k in range(TOP_K):
        wk = w_ref[:, k:k + 1]
        yk = _unpack_chunks(ybuf.at[slot], tc, first_row=k * tc)
        z = [zc + wk * yc for zc, yc in zip(z, yk)]
    ss = sum(jnp.sum(zc * zc, axis=-1, keepdims=True) for zc in z)
    inv = lax.rsqrt(ss * (1.0 / D_MODEL) + NORM_EPS)
    for c in range(n_chunks):
        o_ref[:, c * LANES:(c + 1) * LANES] = z[c] * inv * g_ref[:, c * LANES:(c + 1) * LANES]


def _combine(off3, x1, weight, g, ys):
    N = x1.shape[0]
    tc = ROW_TILE
    nt = N // tc
    idx = lambda f: pl.BlockSpec((1, 1, TOP_K * tc), f, memory_space=pltpu.SMEM)
    return pl.pallas_call(
        _combine_kernel,
        grid=(nt,),
        in_specs=[idx(lambda i: (i, 0, 0)),
                  idx(lambda i: (jnp.minimum(i + 1, nt - 1), 0, 0)),
                  pl.BlockSpec((tc, D_MODEL), lambda i: (i, 0)),
                  pl.BlockSpec((tc, TOP_K), lambda i: (i, 0)),
                  pl.BlockSpec((1, D_MODEL), lambda i: (0, 0)),
                  pl.BlockSpec(memory_space=pl.ANY)],
        out_specs=pl.BlockSpec((tc, D_MODEL), lambda i: (i, 0)),
        out_shape=jax.ShapeDtypeStruct((N, D_MODEL), F32),
        scratch_shapes=[pltpu.VMEM((2, TOP_K * tc * PACK_SUB, LANES), jnp.uint32), pltpu.SemaphoreType.DMA((2,))],
        compiler_params=pltpu.CompilerParams(dimension_semantics=("arbitrary",), vmem_limit_bytes=VMEM_LIMIT),
        name="combine_norm",
    )(off3, off3, x1, weight, g, ys)


def _block_diag(w):
    nb, c, _ = w.shape
    eye = jnp.eye(nb, dtype=w.dtype)
    return (eye[:, None, :, None] * w[:, :, None, :]).reshape(nb * c, nb * c)


def _visit_plan(counts, n_rows):
    tm = MOE_TILE
    n_tiles = n_rows // tm
    n_visits = n_tiles + N_EXPERTS - 1
    cnt = counts.astype(jnp.int32)
    lanes = jnp.arange(ROUTE_ROWS, dtype=jnp.int32)
    upto = (lanes[None, :] <= lanes[:, None]).astype(jnp.int32)
    ends = upto @ cnt
    starts = ends - cnt
    first_tile = starts // tm
    n_vis = jnp.where(cnt > 0, (ends - 1) // tm - first_tile + 1, 0)
    v_end = upto @ n_vis
    v_start = v_end - n_vis
    total = v_end[-1]
    v = jnp.arange(n_visits, dtype=jnp.int32)
    vc = jnp.minimum(v, total - 1)
    own = ((vc[:, None] >= v_start[None, :]) & (vc[:, None] < v_end[None, :])).astype(jnp.int32)
    pick = lambda per_lane: own @ per_lane
    tile = jnp.maximum(pick(first_tile - v_start) + vc, 0)
    lo = jnp.maximum(pick(starts), tile * tm)
    hi = jnp.minimum(pick(ends), (tile + 1) * tm)
    valid = v < total
    to_expert = lambda lane: jnp.maximum(lane - EXPERT_LANE0, 0)
    used = cnt > 0
    later = jnp.where((lanes[None, :] > lanes[:, None]) & used[None, :], lanes[None, :], ROUTE_ROWS)
    nxt = jnp.min(later, axis=1)
    nxt = jnp.where(nxt == ROUTE_ROWS, lanes, nxt)
    parity = (upto @ used.astype(jnp.int32) - 1) % 2
    plan = (tile, to_expert(pick(lanes)), jnp.where(valid, lo, 0), jnp.where(valid, hi, 0),
            to_expert(pick(nxt)), pick(parity))
    return starts, plan


def kernel(x, norm_mix_g, w_in, lambda_qk, subln_g, conv_w, conv_b, w_r, b_r, w_i, b_i, lru_lambda, w_o_attn, w_o_lru, w_out, norm_ffn_g, w_group, w_expert_router, w_gate, w_up, w_down, final_norm_g):
    B, S, D = x.shape
    N = B * S
    nt = S // SEQ_TILE
    depth = norm_mix_g.shape[0]
    assert depth == 1 and D == D_MODEL and S % SEQ_TILE == 0
    assert N % ROW_TILE == 0 and (N * TOP_K) % MOE_TILE == 0
    l = 0
    row = lambda v: v.reshape(1, -1).astype(F32)

    x2 = x.reshape(N, D)
    w_in_l = w_in[l].astype(BF16)
    lru_params = (conv_w[l].astype(F32), row(conv_b[l]), _block_diag(w_r[l]).astype(BF16),
                  _block_diag(w_i[l]).astype(BF16), row(b_r[l]), row(b_i[l]), row(lru_lambda[l]))
    qT, k, vT, lru, kn = _inproj(x2, row(norm_mix_g[l]), w_in_l, lru_params, B, S)

    kn_max = jnp.max(kn[:, 0, :2 * N_HEADS].reshape(B, nt, 2 * N_HEADS), axis=1)
    attn = _attn(kn_max.reshape(-1), qT, k.reshape(B, nt, SEQ_TILE, N_HEADS * K_COLS), vT,
                 lambda_qk[l].reshape(4, HEAD_DIM).astype(F32), row(subln_g[l]), B, S)

    w_route = jnp.concatenate(
        [w_group[l], jnp.transpose(w_expert_router[l], (1, 0, 2)).reshape(D, N_EXPERTS),
         jnp.zeros((D, ROUTE_LANES - N_GROUPS - N_EXPERTS), F32)], axis=1).astype(F32)
    w_route_hi = w_route.astype(BF16)
    w_route = jnp.concatenate([w_route_hi, (w_route - w_route_hi.astype(F32)).astype(BF16)], axis=1).T
    x1, h2p, route, counts = _merge(x2, attn.reshape(N, ATTN_WIDTH), lru.reshape(N, LRU_WIDTH), row(norm_mix_g[l]),
                                    w_in_l, w_o_attn[l].astype(BF16),
                                    w_o_lru[l].astype(BF16), w_out[l].astype(BF16), row(norm_ffn_g[l]), w_route)

    starts, plan = _visit_plan(counts[:, 0], N * TOP_K)
    lane = route[0:TOP_K].astype(jnp.int32)
    weight = jnp.transpose(route[TOP_K:2 * TOP_K])
    rank = route[2 * TOP_K:3 * TOP_K].astype(jnp.int32)
    lane_ids = jnp.arange(ROUTE_ROWS, dtype=jnp.int32)
    start_of = jnp.sum(jnp.where(lane[..., None] == lane_ids, starts, 0), axis=-1)
    off = ((start_of + rank) * PACK_SUB).reshape(TOP_K, N // ROW_TILE, ROW_TILE)
    off3 = jnp.transpose(off, (1, 0, 2)).reshape(N // ROW_TILE, 1, TOP_K * ROW_TILE)

    xs = _dispatch(off3, h2p)
    ys = _moe(plan, xs, w_gate[l], w_up[l], w_down[l])
    out = _combine(off3, x1, weight, row(final_norm_g), ys)
    return out.reshape(B, S, D)
```

```python
import functools
import math

import jax
import jax.numpy as jnp
from jax import lax
from jax.experimental import pallas as pl
from jax.experimental.pallas import tpu as pltpu

F32 = jnp.float32
BF16 = jnp.bfloat16

D_MODEL = 1024
N_HEADS = 4
HEAD_DIM = 64
V_DIM = 2 * HEAD_DIM
ATTN_WIDTH = N_HEADS * V_DIM
V_ROWS = V_DIM + 16
K_COLS = 2 * V_DIM
POS_SPLIT = 3
POS_RADIX = 256
SKIP_MARGIN = 138.0
NORM_SLACK = 1.0201
LRU_WIDTH = D_MODEL // 2
LRU_BLOCKS = 8
CONV_W = 4
LRU_C = 8.0
N_GROUPS = 4
EXPERTS_PER_GROUP = 8
N_EXPERTS = N_GROUPS * EXPERTS_PER_GROUP
TOP_K = 2
D_EXPERT = D_MODEL // 2
NORM_EPS = 1e-6
LAM_INIT = 0.8 - 0.6 * math.exp(-0.3 * 0)

QK_COLS = N_HEADS * 2 * HEAD_DIM
PROJ_COLS = 2 * QK_COLS + ATTN_WIDTH + 2 * LRU_WIDTH
ROUTE_LANES = 128
ROUTE_ROWS = 64
ROUTE_FIELDS = 8
NEG_BIG = -1e30
LOG2E = math.log2(math.e)
ALIBI_SLOPES = tuple(2.0 ** (-8.0 * (h + 1) / N_HEADS) for h in range(N_HEADS))

SEQ_TILE = 512
ROW_TILE = 512
MERGE_CHAIN = 512
MOE_TILE = 256
DMA_GROUP = 8
DISPATCH_SLOTS = 3
LANES = 128
SUBLANES = 8
PACK_SUB = D_MODEL // 2 // LANES
EXPERT_LANE0 = N_GROUPS
V7X_VMEM_BYTES = 64 * 1024 * 1024
VMEM_LIMIT = V7X_VMEM_BYTES * 3 // 4


def _rms(x, g):
    return x * lax.rsqrt(jnp.mean(x * x, axis=-1, keepdims=True) + NORM_EPS) * g


def _dot(a, b):
    return jnp.dot(a, b, preferred_element_type=F32)


def _inproj_kernel(tiles_per_seq, x_ref, g_ref, w_ref, cw_ref, cb_ref, wr_ref, wi_ref, br_ref, bi_ref, lam_ref,
                   qT_ref, k_ref, vT_ref, lru_ref, kn_ref, xbuf, hc):
    ti = lax.rem(pl.program_id(0), tiles_per_seq)

    @pl.when(ti == 0)
    def _():
        xbuf[0:8] = jnp.zeros((8, LRU_WIDTH), F32)
        hc[...] = jnp.zeros_like(hc)

    hb = _rms(x_ref[...], g_ref[...]).astype(BF16)

    def proj(lo, hi):
        return _dot(hb, w_ref[:, lo:hi])

    c0 = 2 * QK_COLS + ATTN_WIDTH
    lru_x = proj(c0, c0 + LRU_WIDTH)
    lru_y = proj(c0 + LRU_WIDTH, c0 + 2 * LRU_WIDTH)

    q = proj(0, QK_COLS) * (HEAD_DIM ** -0.5 * LOG2E)
    for h in range(N_HEADS):
        qT_ref[h] = q[:, h * V_DIM:(h + 1) * V_DIM].T.astype(BF16)
    k = proj(QK_COLS, 2 * QK_COLS).astype(BF16)
    r = lax.broadcasted_iota(jnp.int32, (k.shape[0], K_COLS - V_DIM), 0)
    lane = lax.broadcasted_iota(jnp.int32, r.shape, 1)
    a = r // POS_RADIX * POS_RADIX
    feat = jnp.where(lane < POS_SPLIT, a, jnp.where(lane < 2 * POS_SPLIT, r - a, 0)).astype(F32).astype(BF16)
    for h in range(N_HEADS):
        k_ref[:, h * K_COLS:h * K_COLS + V_DIM] = k[:, h * V_DIM:(h + 1) * V_DIM]
        k_ref[:, h * K_COLS + V_DIM:(h + 1) * K_COLS] = feat
    seg = lax.broadcasted_iota(jnp.int32, (QK_COLS, LANES), 0) // HEAD_DIM
    pick = jnp.where(seg == lax.broadcasted_iota(jnp.int32, (QK_COLS, LANES), 1), 1.0, 0.0).astype(BF16)
    row_n2 = _dot(jnp.square(k.astype(F32)).astype(BF16), pick)
    kn_ref[...] = jnp.max(row_n2, axis=0, keepdims=True)
    v = proj(2 * QK_COLS, 2 * QK_COLS + ATTN_WIDTH)
    for h in range(N_HEADS):
        vT_ref[h, :V_DIM, :] = v[:, h * V_DIM:(h + 1) * V_DIM].T.astype(BF16)
        pad_row = lax.broadcasted_iota(jnp.int32, (V_ROWS - V_DIM, v.shape[0]), 0)
        vT_ref[h, V_DIM:, :] = jnp.where(pad_row == 0, 1.0, 0.0).astype(BF16)

    lru_ref[...] = _lru_tile(lru_x, lru_y, ti, cw_ref, cb_ref, wr_ref, wi_ref, br_ref, bi_ref, lam_ref, xbuf, hc)


def _inproj(x2, g, w, lru_params, B, S):
    N = B * S
    tm = SEQ_TILE
    nt = S // tm
    tile5 = pl.BlockSpec((None, N_HEADS, None, V_DIM, tm), lambda i: (i // nt, 0, i % nt, 0, 0))
    rows = lambda c: pl.BlockSpec((tm, c), lambda i: (i, 0))
    full = lambda a: pl.BlockSpec(a.shape, lambda i: (0, 0))
    return pl.pallas_call(
        functools.partial(_inproj_kernel, nt),
        grid=(N // tm,),
        in_specs=[rows(D_MODEL),
                  pl.BlockSpec((1, D_MODEL), lambda i: (0, 0)),
                  pl.BlockSpec((D_MODEL, PROJ_COLS), lambda i: (0, 0))] + [full(a) for a in lru_params],
        out_specs=[tile5, rows(N_HEADS * K_COLS),
                   pl.BlockSpec((None, N_HEADS, None, V_ROWS, tm), lambda i: (i // nt, 0, i % nt, 0, 0)),
                   rows(LRU_WIDTH),
                   pl.BlockSpec((None, 1, LANES), lambda i: (i, 0, 0))],
        out_shape=[jax.ShapeDtypeStruct((B, N_HEADS, nt, V_DIM, tm), BF16),
                   jax.ShapeDtypeStruct((N, N_HEADS * K_COLS), BF16),
                   jax.ShapeDtypeStruct((B, N_HEADS, nt, V_ROWS, tm), BF16),
                   jax.ShapeDtypeStruct((N, LRU_WIDTH), BF16),
                   jax.ShapeDtypeStruct((N // tm, 1, LANES), F32)],
        scratch_shapes=[pltpu.VMEM((tm + 8, LRU_WIDTH), F32), pltpu.VMEM((1, LRU_WIDTH), F32)],
        compiler_params=pltpu.CompilerParams(dimension_semantics=("arbitrary",),
                                             vmem_limit_bytes=VMEM_LIMIT),
        name="inproj_lru",
    )(x2, g, w, *lru_params)


def _attn_kernel(kn_ref, qT_ref, k_ref, vT_ref, lam_ref, g_ref, o_ref, mask_ref, acc_ref, sa_ref, sb_ref, pa_ref,
                 pb_ref):
    t = SEQ_TILE
    b = pl.program_id(0)
    h = pl.program_id(1)
    i = pl.program_id(2)
    slope = LOG2E * jnp.where(h == 0, ALIBI_SLOPES[0], jnp.where(h == 1, ALIBI_SLOPES[1],
                              jnp.where(h == 2, ALIBI_SLOPES[2], ALIBI_SLOPES[3]))).astype(F32)

    @pl.when(i == 0)
    def _():
        r = lax.broadcasted_iota(jnp.int32, (t, t), 0)
        c = lax.broadcasted_iota(jnp.int32, (t, t), 1)
        mask_ref[...] = jnp.where(r <= c, 0.0, NEG_BIG)

    qf = qT_ref[...].astype(F32)
    row = lax.broadcasted_iota(jnp.int32, qf.shape, 0)
    sl = jnp.full(qf.shape, slope, F32)
    hi = sl.astype(BF16).astype(F32)
    mid = (sl - hi).astype(BF16).astype(F32)
    lo = (sl - hi - mid).astype(BF16).astype(F32)
    piece = jnp.where(row % 3 == 0, hi, jnp.where(row % 3 == 1, mid, lo))
    srows = jnp.where(row < 2 * POS_SPLIT, piece, 0.0).astype(BF16)
    qs = tuple(jnp.concatenate([jnp.where(sel, qf, 0.0).astype(BF16), srows], axis=0)
               for sel in (row < HEAD_DIM, row >= HEAD_DIM))
    acc_ref[...] = jnp.zeros_like(acc_ref)
    pb_ref[...] = jnp.zeros_like(pb_ref)

    first_off = [0]

    def key_tile(tau):
        return jnp.where(tau <= 0, i, first_off[0] + tau - 1)

    def stage_q(tau, s_ref, diagonal=False):
        kt = k_ref[key_tile(tau)]
        tile_max = []
        for mi in range(2):
            s = _dot(kt, qs[mi])
            if diagonal:
                s = s + mask_ref[...]
            s_ref[mi] = s
            tile_max.append(jnp.max(s, axis=0, keepdims=True))
        return tuple(tile_max)

    def stage_s(tau, s_ref, p_ref, ms, tile_max):
        cj = slope * (key_tile(tau) * t).astype(F32)
        m_out, alphas = [], []
        for mi in range(2):
            m_new = jnp.maximum(ms[mi], tile_max[mi] + cj)
            alphas.append(jnp.exp2(ms[mi] - m_new))
            p_ref[mi] = jnp.exp2(s_ref[mi] - (m_new - cj)).astype(BF16)
            m_out.append(m_new)
        return tuple(m_out), tuple(alphas)

    def stage_v(tau, p_ref, alphas):
        vt = vT_ref[key_tile(tau)]
        for mi in range(2):
            acc_ref[mi] = alphas[mi] * acc_ref[mi] + _dot(vt, p_ref[mi])

    def body(jj, carry):
        ms, alphas, tmax = carry[:2], carry[2:4], carry[4:]
        tau = 2 * jj
        tmax_b = stage_q(tau + 1, sb_ref)
        ms, alphas_a = stage_s(tau, sa_ref, pa_ref, ms, tmax)
        stage_v(tau - 1, pb_ref, alphas)
        tmax_a = stage_q(tau + 2, sa_ref)
        ms, alphas_b = stage_s(tau + 1, sb_ref, pb_ref, ms, tmax_b)
        stage_v(tau, pa_ref, alphas_a)
        return ms + alphas_b + tmax_a

    def finalize():
        lp = lam_ref[...]
        s1 = jnp.sum(lp[0:1] * lp[1:2], axis=-1, keepdims=True)
        s2 = jnp.sum(lp[2:3] * lp[3:4], axis=-1, keepdims=True)
        lam = jnp.exp(s1) - jnp.exp(s2) + LAM_INIT
        norm = [acc_ref[mi, :V_DIM, :] * (1.0 / acc_ref[mi, V_DIM:V_DIM + 1, :]) for mi in range(2)]
        oT = norm[0] - lam * norm[1]
        o = _rms(oT.T, g_ref[...]) * (1.0 - LAM_INIT)
        o_ref[...] = o.astype(BF16)

    m_init = jnp.full((1, t), NEG_BIG, F32)
    one = jnp.ones((1, t), F32)
    tmax0 = stage_q(0, sa_ref, diagonal=True)

    cj0 = slope * (i * t).astype(F32)
    n_skip = None
    for mi in range(2):
        qsq = jnp.square(qf[mi * HEAD_DIM:(mi + 1) * HEAD_DIM, :])
        qn2 = jnp.max(jnp.sum(qsq, axis=0, keepdims=True), axis=1, keepdims=True)
        m_low = jnp.min(tmax0[mi], axis=1, keepdims=True) + cj0
        qk = jnp.sqrt(qn2 * (kn_ref[(b * N_HEADS + h) * 2 + mi] * NORM_SLACK))
        count = jnp.ceil((m_low - SKIP_MARGIN - slope * (t - 1) - qk) / (slope * t))
        n_skip = count if n_skip is None else jnp.minimum(n_skip, count)
    first_off[0] = jnp.clip(n_skip, 0.0, i.astype(F32)).astype(jnp.int32)[0, 0]
    n_off = i - first_off[0]

    n_main = jnp.right_shift(n_off, 1)
    fin = lax.fori_loop(0, n_main, body, (m_init, m_init, one, one) + tmax0)
    ms, alphas, tmax = fin[:2], fin[2:4], fin[4:]
    tau = 2 * n_main
    odd_tiles = tau == n_off

    @pl.when(odd_tiles)
    def _():
        _, alphas_a = stage_s(tau, sa_ref, pa_ref, ms, tmax)
        stage_v(tau - 1, pb_ref, alphas)
        stage_v(tau, pa_ref, alphas_a)
        finalize()

    @pl.when(jnp.logical_not(odd_tiles))
    def _():
        tmax_b = stage_q(tau + 1, sb_ref)
        ms_a, alphas_a = stage_s(tau, sa_ref, pa_ref, ms, tmax)
        stage_v(tau - 1, pb_ref, alphas)
        _, alphas_b = stage_s(tau + 1, sb_ref, pb_ref, ms_a, tmax_b)
        stage_v(tau, pa_ref, alphas_a)
        stage_v(tau + 1, pb_ref, alphas_b)
        finalize()


def _attn(kn, qT, k4, vT, lam, g, B, S):
    t = SEQ_TILE
    nt = S // t
    grid_spec = pltpu.PrefetchScalarGridSpec(
        num_scalar_prefetch=1,
        grid=(B, N_HEADS, nt),
        in_specs=[pl.BlockSpec((None, None, None, V_DIM, t), lambda b, h, i, kn: (b, h, i, 0, 0)),
                  pl.BlockSpec((None, nt, t, K_COLS), lambda b, h, i, kn: (b, 0, 0, h)),
                  pl.BlockSpec((None, None, nt, V_ROWS, t), lambda b, h, i, kn: (b, h, 0, 0, 0)),
                  pl.BlockSpec((4, HEAD_DIM), lambda b, h, i, kn: (0, 0)),
                  pl.BlockSpec((1, V_DIM), lambda b, h, i, kn: (0, 0))],
        out_specs=pl.BlockSpec((None, t, V_DIM), lambda b, h, i, kn: (b, i, h)),
        scratch_shapes=[pltpu.VMEM((t, t), F32), pltpu.VMEM((2, V_ROWS, t), F32),
                        pltpu.VMEM((2, t, t), F32), pltpu.VMEM((2, t, t), F32),
                        pltpu.VMEM((2, t, t), BF16), pltpu.VMEM((2, t, t), BF16)],
    )
    return pl.pallas_call(
        _attn_kernel,
        grid_spec=grid_spec,
        out_shape=jax.ShapeDtypeStruct((B, S, ATTN_WIDTH), BF16),
        compiler_params=pltpu.CompilerParams(dimension_semantics=("parallel", "arbitrary", "arbitrary"),
                                             vmem_limit_bytes=VMEM_LIMIT),
        name="diff_attn",
    )(kn, qT, k4, vT, lam, g)


def _lru_tile(x, y, ti, cw_ref, cb_ref, wr_ref, wi_ref, br_ref, bi_ref, lam_ref, xbuf, hc):
    T = SEQ_TILE
    xbuf[8:8 + T] = x
    cw = cw_ref[...]
    xc = cb_ref[...] + cw[3:4] * x
    for j in range(CONV_W - 1):
        xc = xc + cw[j:j + 1] * xbuf[5 + j:5 + j + T]
    xbuf[0:8] = x[T - 8:T]

    xb = xc.astype(BF16)
    sigmoid = lambda z: 0.5 * jnp.tanh(0.5 * z) + 0.5
    r = sigmoid(_dot(xb, wr_ref[...]) + br_ref[...])
    ig = sigmoid(_dot(xb, wi_ref[...]) + bi_ref[...])
    z = -lam_ref[...]
    softplus = jnp.maximum(z, 0.0) + jnp.log1p(jnp.exp(-jnp.abs(z)))
    la = -LRU_C * r * softplus
    a = jnp.exp(la)
    m2 = -jnp.tanh(la) * (a * a + 1.0)
    mult = jnp.where(m2 > 0.0, m2 * lax.rsqrt(m2), 0.0)
    row = lax.broadcasted_iota(jnp.int32, (T, LRU_WIDTH), 0)
    mult = jnp.where((row == 0) & (ti == 0), 1.0, mult)
    u = (xc * ig) * mult

    nb = T // SUBLANES
    a3 = a.reshape(nb, SUBLANES, LRU_WIDTH)
    u3 = u.reshape(nb, SUBLANES, LRU_WIDTH)
    sub = lax.broadcasted_iota(jnp.int32, a3.shape, 1)
    d = 1
    while d < SUBLANES:
        valid = sub >= d
        u3 = jnp.where(valid, a3 * pltpu.roll(u3, d, 1) + u3, u3)
        a3 = jnp.where(valid, a3 * pltpu.roll(a3, d, 1), a3)
        d *= 2
    h = hc[...]
    blocks = []
    for b in range(nb):
        hb = u3[b] + a3[b] * h
        blocks.append(hb)
        h = hb[SUBLANES - 1:SUBLANES]
    hfull = jnp.concatenate(blocks, axis=0)
    hc[...] = h
    gelu = 0.5 * y * (1.0 + jnp.tanh(0.7978845608028654 * (y + 0.044715 * (y * y * y))))
    return (hfull * gelu).astype(BF16)


def _pack_words(v):
    bits = pltpu.bitcast(v.astype(BF16).astype(F32), jnp.uint32)
    half = D_MODEL // 2
    packed = (bits[:, :half] >> 16) | (bits[:, half:] & jnp.uint32(0xFFFF0000))
    return [packed[:, c * LANES:(c + 1) * LANES] for c in range(PACK_SUB)]


def _packed_chunk(c, rows, first_row=0):
    return (pl.ds(first_row * PACK_SUB + c, rows, stride=PACK_SUB), slice(None))


def _pack_rows(v, out_ref, first_row=0):
    for c, words in enumerate(_pack_words(v)):
        out_ref[_packed_chunk(c, v.shape[0], first_row)] = words


def _unpack_chunks(in_ref, rows, first_row=0):
    lo, hi = [], []
    for c in range(PACK_SUB):
        w = in_ref[_packed_chunk(c, rows, first_row)]
        lo.append(pltpu.bitcast(w << 16, F32))
        hi.append(pltpu.bitcast(w & jnp.uint32(0xFFFF0000), F32))
    return lo + hi


def _unpack_rows(in_ref, rows):
    return jnp.concatenate(_unpack_chunks(in_ref, rows), axis=1).astype(BF16)


def _merge_kernel(x_ref, attn_ref, lru_ref, g1_ref, wg_ref, woa_ref, wol_ref, wout_ref, g2_ref, wrt_ref,
                  x1_ref, h2p_ref, route_ref, counts_ref, cnt, tri):
    tm = MERGE_CHAIN

    @pl.when(pl.program_id(0) == 0)
    def _():
        cnt[...] = jnp.zeros_like(cnt)
        r = lax.broadcasted_iota(jnp.int32, (tm, tm), 0)
        c = lax.broadcasted_iota(jnp.int32, (tm, tm), 1)
        tri[...] = jnp.where(r < c, 1.0, 0.0).astype(BF16)

    counts = cnt[...]
    for r0 in range(0, SEQ_TILE, tm):
        counts = _merge_chain(r0, tm, counts, x_ref, attn_ref, lru_ref, g1_ref, wg_ref, woa_ref, wol_ref, wout_ref,
                              g2_ref, wrt_ref, x1_ref, h2p_ref, route_ref, tri)
    cnt[...] = counts
    counts_ref[...] = counts


def _merge_chain(r0, tm, counts, x_ref, attn_ref, lru_ref, g1_ref, wg_ref, woa_ref, wol_ref, wout_ref, g2_ref, wrt_ref,
                 x1_ref, h2p_ref, route_ref, tri):
    rs = slice(r0, r0 + tm)
    x = x_ref[rs, :]
    hb = _rms(x, g1_ref[...]).astype(BF16)
    gates = 0.5 * jnp.tanh(0.5 * _dot(hb, wg_ref[:, PROJ_COLS:])) + 0.5
    merged = gates[:, :D_MODEL] * _dot(attn_ref[rs, :], woa_ref[...]) + gates[:, D_MODEL:] * _dot(lru_ref[rs, :], wol_ref[...])
    x1 = x + _dot(merged.astype(BF16), wout_ref[...])
    x1_ref[rs, :] = x1
    h2 = _rms(x1, g2_ref[...])
    _pack_rows(h2, h2p_ref, first_row=r0)

    h_hi = h2.astype(BF16)
    h_lo = (h2 - h_hi.astype(F32)).astype(BF16)
    wrt = wrt_ref[...]
    nt_dims = (((1,), (1,)), ((), ()))
    hh = lax.dot_general(wrt, h_hi, nt_dims, preferred_element_type=F32)
    lo_pass = lax.dot_general(wrt[:ROUTE_LANES], h_lo, nt_dims, preferred_element_type=F32)
    logits = (hh[:ROUTE_LANES] + hh[ROUTE_LANES:] + lo_pass)[:ROUTE_ROWS]
    row = lax.broadcasted_iota(jnp.int32, logits.shape, 0)
    big = jnp.int32(1 << 20)

    def first_argmax(v):
        m = jnp.max(v, axis=0, keepdims=True)
        return m, jnp.min(jnp.where(v == m, row, big), axis=0, keepdims=True)

    gmask = row < N_GROUPS
    gmax, gidx = first_argmax(jnp.where(gmask, logits, -jnp.inf))
    gsum = jnp.sum(jnp.where(gmask, jnp.exp(logits - gmax), 0.0), axis=0, keepdims=True)
    g_w = 1.0 / gsum
    lo = N_GROUPS + EXPERTS_PER_GROUP * gidx
    el = jnp.where((row >= lo) & (row < lo + EXPERTS_PER_GROUP), logits, -jnp.inf)
    m1, i1 = first_argmax(el)
    m2, i2 = first_argmax(jnp.where(row == i1, -jnp.inf, el))
    rr = jnp.exp(m2 - m1)
    w1 = g_w / (1.0 + rr)
    w2 = g_w * rr / (1.0 + rr)
    oh1 = row == i1
    oh2 = row == i2
    oh = jnp.where(oh1 | oh2, 1.0, 0.0)
    before = _dot(oh.astype(BF16), tri[...]) + counts
    r1 = jnp.sum(jnp.where(oh1, before, 0.0), axis=0, keepdims=True)
    r2 = jnp.sum(jnp.where(oh2, before, 0.0), axis=0, keepdims=True)
    vals = (i1.astype(F32), i2.astype(F32), w1, w2, r1, r2, jnp.zeros_like(w1), jnp.zeros_like(w1))
    for k, v in enumerate(vals):
        route_ref[k:k + 1, rs] = v
    return counts + jnp.sum(oh, axis=1, keepdims=True)


def _merge(x2, attn, lru, g1, wg, woa, wol, wout, g2, wrt):
    N = x2.shape[0]
    tm = SEQ_TILE
    rows = lambda c: pl.BlockSpec((tm, c), lambda i: (i, 0))
    full = lambda r, c: pl.BlockSpec((r, c), lambda i: (0, 0))
    return pl.pallas_call(
        _merge_kernel,
        grid=(N // tm,),
        in_specs=[rows(D_MODEL), rows(ATTN_WIDTH), rows(LRU_WIDTH), full(1, D_MODEL),
                  full(D_MODEL, PROJ_COLS + 2 * D_MODEL),
                  full(ATTN_WIDTH, D_MODEL), full(LRU_WIDTH, D_MODEL), full(D_MODEL, D_MODEL), full(1, D_MODEL),
                  full(2 * ROUTE_LANES, D_MODEL)],
        out_specs=[rows(D_MODEL), pl.BlockSpec((tm * PACK_SUB, LANES), lambda i: (i, 0)),
                   pl.BlockSpec((ROUTE_FIELDS, tm), lambda i: (0, i)), full(ROUTE_ROWS, 1)],
        out_shape=[jax.ShapeDtypeStruct((N, D_MODEL), F32), jax.ShapeDtypeStruct((N * PACK_SUB, LANES), jnp.uint32),
                   jax.ShapeDtypeStruct((ROUTE_FIELDS, N), F32), jax.ShapeDtypeStruct((ROUTE_ROWS, 1), F32)],
        scratch_shapes=[pltpu.VMEM((ROUTE_ROWS, 1), F32), pltpu.VMEM((MERGE_CHAIN, MERGE_CHAIN), BF16)],
        compiler_params=pltpu.CompilerParams(dimension_semantics=("arbitrary",), vmem_limit_bytes=VMEM_LIMIT),
        name="merge_route",
    )(x2, attn, lru, g1, wg, woa, wol, wout, g2, wrt)


def _for_each_assignment(off_ref, fn):
    def group(gi, _):
        toks = [gi * DMA_GROUP + j for j in range(DMA_GROUP)]
        offs = [[off_ref[0, 0, k * ROW_TILE + tk] for k in range(TOP_K)] for tk in toks]
        for tk, o in zip(toks, offs):
            for k in range(TOP_K):
                fn(tk, k, pl.multiple_of(o[k], PACK_SUB))
        return 0
    lax.fori_loop(0, ROW_TILE // DMA_GROUP, group, 0)


def _dispatch_kernel(off_ref, h2p_hbm, xs_hbm, xin, in_sem, out_sem):
    tile_rows = ROW_TILE * PACK_SUB
    i = pl.program_id(0)
    n = pl.num_programs(0)
    slot = lax.rem(i, DISPATCH_SLOTS)

    def fetch(tile, s):
        return pltpu.make_async_copy(h2p_hbm.at[pl.ds(pl.multiple_of(tile * tile_rows, tile_rows), tile_rows)],
                                     xin.at[s], in_sem.at[s])

    def drain(s):
        for _ in range(TOP_K):
            pltpu.make_async_copy(xin.at[s], xs_hbm.at[pl.ds(0, tile_rows)], out_sem.at[s]).wait()

    @pl.when(i == 0)
    def _():
        fetch(0, 0).start()

    @pl.when(i + 1 < n)
    def _():
        fetch(i + 1, lax.rem(i + 1, DISPATCH_SLOTS)).start()

    fetch(i, slot).wait()

    def start(tk, k, off):
        pltpu.make_async_copy(xin.at[slot, pl.ds(pl.multiple_of(tk * PACK_SUB, PACK_SUB), PACK_SUB)],
                              xs_hbm.at[pl.ds(off, PACK_SUB)], out_sem.at[slot]).start(priority=k)

    _for_each_assignment(off_ref, start)

    @pl.when(i > 0)
    def _():
        drain(lax.rem(i + DISPATCH_SLOTS - 1, DISPATCH_SLOTS))

    @pl.when(i == n - 1)
    def _():
        drain(slot)


def _dispatch(off3, h2p):
    return pl.pallas_call(
        _dispatch_kernel,
        grid=(off3.shape[0],),
        in_specs=[pl.BlockSpec((1, 1, TOP_K * ROW_TILE), lambda i: (i, 0, 0), memory_space=pltpu.SMEM),
                  pl.BlockSpec(memory_space=pl.ANY)],
        out_specs=pl.BlockSpec(memory_space=pl.ANY),
        out_shape=jax.ShapeDtypeStruct((TOP_K * h2p.shape[0], LANES), jnp.uint32),
        scratch_shapes=[pltpu.VMEM((DISPATCH_SLOTS, ROW_TILE * PACK_SUB, LANES), jnp.uint32),
                        pltpu.SemaphoreType.DMA((DISPATCH_SLOTS,)), pltpu.SemaphoreType.DMA((DISPATCH_SLOTS,))],
        compiler_params=pltpu.CompilerParams(dimension_semantics=("arbitrary",), has_side_effects=True),
        name="moe_dispatch",
    )(off3, h2p)


def _moe_kernel(vt_ref, ve_ref, vlo_ref, vhi_ref, vnext_ref, vpar_ref, xs_ref, wg_hbm, wu_hbm, wd_hbm, ys_ref,
                wgb, wub, wdb, wgf, wuf, wdf, sem):
    tm = MOE_TILE
    v = pl.program_id(0)
    t = vt_ref[v]
    e = ve_ref[v]
    slot = vpar_ref[v]
    prev = jnp.maximum(v - 1, 0)

    def fetch(expert, s):
        return [pltpu.make_async_copy(w_hbm.at[expert], buf.at[s], sem.at[s])
                for w_hbm, buf in ((wg_hbm, wgf), (wu_hbm, wuf), (wd_hbm, wdf))]

    @pl.when(v == 0)
    def _():
        for copy in fetch(e, slot):
            copy.start()

    @pl.when((v == 0) | (ve_ref[prev] != e))
    def _():
        for copy in fetch(e, slot):
            copy.wait()
        nxt = vnext_ref[v]

        @pl.when(nxt != e)
        def _():
            for copy in fetch(nxt, 1 - slot):
                copy.start()

        wgb[...] = wgf[slot].astype(BF16)
        wub[...] = wuf[slot].astype(BF16)
        wdb[...] = wdf[slot].astype(BF16)

    xb = _unpack_rows(xs_ref, tm)
    g = _dot(xb, wgb[...])
    u = _dot(xb, wub[...])
    hmid = (g * jax.nn.sigmoid(g)) * u
    words = _pack_words(_dot(hmid.astype(BF16), wdb[...]))
    first = (v == 0) | (vt_ref[prev] != t)

    @pl.when(first)
    def _():
        for c in range(PACK_SUB):
            ys_ref[_packed_chunk(c, tm)] = words[c]

    @pl.when(jnp.logical_not(first))
    def _():
        rows = t * tm + lax.broadcasted_iota(jnp.int32, (tm, LANES), 0)
        mine = (rows >= vlo_ref[v]) & (rows < vhi_ref[v])
        for c in range(PACK_SUB):
            ys_ref[_packed_chunk(c, tm)] = jnp.where(mine, words[c], ys_ref[_packed_chunk(c, tm)])


def _moe(plan, xs, wg, wu, wd):
    tm = MOE_TILE
    n_rows = xs.shape[0] // PACK_SUB
    tile = lambda v, vt, *_: (vt[v], 0)
    hbm = pl.BlockSpec(memory_space=pl.ANY)
    grid_spec = pltpu.PrefetchScalarGridSpec(
        num_scalar_prefetch=len(plan),
        grid=(plan[0].shape[0],),
        in_specs=[pl.BlockSpec((tm * PACK_SUB, LANES), tile), hbm, hbm, hbm],
        out_specs=pl.BlockSpec((tm * PACK_SUB, LANES), tile),
        scratch_shapes=[pltpu.VMEM((D_MODEL, D_EXPERT), BF16), pltpu.VMEM((D_MODEL, D_EXPERT), BF16),
                        pltpu.VMEM((D_EXPERT, D_MODEL), BF16),
                        pltpu.VMEM((2, D_MODEL, D_EXPERT), F32), pltpu.VMEM((2, D_MODEL, D_EXPERT), F32),
                        pltpu.VMEM((2, D_EXPERT, D_MODEL), F32), pltpu.SemaphoreType.DMA((2,))],
    )
    return pl.pallas_call(
        _moe_kernel,
        grid_spec=grid_spec,
        out_shape=jax.ShapeDtypeStruct((n_rows * PACK_SUB, LANES), jnp.uint32),
        compiler_params=pltpu.CompilerParams(dimension_semantics=("arbitrary",), vmem_limit_bytes=VMEM_LIMIT),
        name="moe_experts",
    )(*plan, xs, wg, wu, wd)


def _combine_kernel(off_ref, offn_ref, x1_ref, w_ref, g_ref, ys_hbm, o_ref, ybuf, sem):
    tc = ROW_TILE
    i = pl.program_id(0)
    n = pl.num_programs(0)
    slot = lax.rem(i, 2)

    def gather(o_ref_, s):
        def start(tk, k, off):
            pltpu.make_async_copy(ys_hbm.at[pl.ds(off, PACK_SUB)],
                                  ybuf.at[s, pl.ds(pl.multiple_of((k * tc + tk) * PACK_SUB, PACK_SUB), PACK_SUB)],
                                  sem.at[s]).start(priority=k)
        _for_each_assignment(o_ref_, start)

    def drain(s):
        pltpu.make_async_copy(ys_hbm.at[pl.ds(0, TOP_K * tc * PACK_SUB)], ybuf.at[s], sem.at[s]).wait()

    @pl.when(i == 0)
    def _():
        gather(off_ref, 0)

    @pl.when(i + 1 < n)
    def _():
        gather(offn_ref, 1 - slot)

    drain(slot)

    n_chunks = D_MODEL // LANES
    z = [x1_ref[:, c * LANES:(c + 1) * LANES] for c in range(n_chunks)]
    for k in range(TOP_K):
        wk = w_ref[:, k:k + 1]
        yk = _unpack_chunks(ybuf.at[slot], tc, first_row=k * tc)
        z = [zc + wk * yc for zc, yc in zip(z, yk)]
    ss = sum(jnp.sum(zc * zc, axis=-1, keepdims=True) for zc in z)
    inv = lax.rsqrt(ss * (1.0 / D_MODEL) + NORM_EPS)
    for c in range(n_chunks):
        o_ref[:, c * LANES:(c + 1) * LANES] = z[c] * inv * g_ref[:, c * LANES:(c + 1) * LANES]


def _combine(off3, x1, weight, g, ys):
    N = x1.shape[0]
    tc = ROW_TILE
    nt = N // tc
    idx = lambda f: pl.BlockSpec((1, 1, TOP_K * tc), f, memory_space=pltpu.SMEM)
    return pl.pallas_call(
        _combine_kernel,
        grid=(nt,),
        in_specs=[idx(lambda i: (i, 0, 0)),
                  idx(lambda i: (jnp.minimum(i + 1, nt - 1), 0, 0)),
                  pl.BlockSpec((tc, D_MODEL), lambda i: (i, 0)),
                  pl.BlockSpec((tc, TOP_K), lambda i: (i, 0)),
                  pl.BlockSpec((1, D_MODEL), lambda i: (0, 0)),
                  pl.BlockSpec(memory_space=pl.ANY)],
        out_specs=pl.BlockSpec((tc, D_MODEL), lambda i: (i, 0)),
        out_shape=jax.ShapeDtypeStruct((N, D_MODEL), F32),
        scratch_shapes=[pltpu.VMEM((2, TOP_K * tc * PACK_SUB, LANES), jnp.uint32), pltpu.SemaphoreType.DMA((2,))],
        compiler_params=pltpu.CompilerParams(dimension_semantics=("arbitrary",), vmem_limit_bytes=VMEM_LIMIT),
        name="combine_norm",
    )(off3, off3, x1, weight, g, ys)


def _block_diag(w):
    nb, c, _ = w.shape
    eye = jnp.eye(nb, dtype=w.dtype)
    return (eye[:, None, :, None] * w[:, :, None, :]).reshape(nb * c, nb * c)


def _visit_plan(counts, n_rows):
    tm = MOE_TILE
    n_tiles = n_rows // tm
    n_visits = n_tiles + N_EXPERTS - 1
    cnt = counts.astype(jnp.int32)
    lanes = jnp.arange(ROUTE_ROWS, dtype=jnp.int32)
    upto = (lanes[None, :] <= lanes[:, None]).astype(jnp.int32)
    ends = upto @ cnt
    starts = ends - cnt
    first_tile = starts // tm
    n_vis = jnp.where(cnt > 0, (ends - 1) // tm - first_tile + 1, 0)
    v_end = upto @ n_vis
    v_start = v_end - n_vis
    total = v_end[-1]
    v = jnp.arange(n_visits, dtype=jnp.int32)
    vc = jnp.minimum(v, total - 1)
    own = ((vc[:, None] >= v_start[None, :]) & (vc[:, None] < v_end[None, :])).astype(jnp.int32)
    pick = lambda per_lane: own @ per_lane
    tile = jnp.maximum(pick(first_tile - v_start) + vc, 0)
    lo = jnp.maximum(pick(starts), tile * tm)
    hi = jnp.minimum(pick(ends), (tile + 1) * tm)
    valid = v < total
    to_expert = lambda lane: jnp.maximum(lane - EXPERT_LANE0, 0)
    used = cnt > 0
    later = jnp.where((lanes[None, :] > lanes[:, None]) & used[None, :], lanes[None, :], ROUTE_ROWS)
    nxt = jnp.min(later, axis=1)
    nxt = jnp.where(nxt == ROUTE_ROWS, lanes, nxt)
    parity = (upto @ used.astype(jnp.int32) - 1) % 2
    plan = (tile, to_expert(pick(lanes)), jnp.where(valid, lo, 0), jnp.where(valid, hi, 0),
            to_expert(pick(nxt)), pick(parity))
    return starts, plan


def kernel(x, norm_mix_g, w_in, lambda_qk, subln_g, conv_w, conv_b, w_r, b_r, w_i, b_i, lru_lambda, w_o_attn, w_o_lru, w_out, norm_ffn_g, w_group, w_expert_router, w_gate, w_up, w_down, final_norm_g):
    B, S, D = x.shape
    N = B * S
    nt = S // SEQ_TILE
    depth = norm_mix_g.shape[0]
    assert depth == 1 and D == D_MODEL and S % SEQ_TILE == 0
    assert N % ROW_TILE == 0 and (N * TOP_K) % MOE_TILE == 0
    l = 0
    row = lambda v: v.reshape(1, -1).astype(F32)

    x2 = x.reshape(N, D)
    w_in_l = w_in[l].astype(BF16)
    lru_params = (conv_w[l].astype(F32), row(conv_b[l]), _block_diag(w_r[l]).astype(BF16),
                  _block_diag(w_i[l]).astype(BF16), row(b_r[l]), row(b_i[l]), row(lru_lambda[l]))
    qT, k, vT, lru, kn = _inproj(x2, row(norm_mix_g[l]), w_in_l, lru_params, B, S)

    kn_max = jnp.max(kn[:, 0, :2 * N_HEADS].reshape(B, nt, 2 * N_HEADS), axis=1)
    attn = _attn(kn_max.reshape(-1), qT, k.reshape(B, nt, SEQ_TILE, N_HEADS * K_COLS), vT,
                 lambda_qk[l].reshape(4, HEAD_DIM).astype(F32), row(subln_g[l]), B, S)

    w_route = jnp.concatenate(
        [w_group[l], jnp.transpose(w_expert_router[l], (1, 0, 2)).reshape(D, N_EXPERTS),
         jnp.zeros((D, ROUTE_LANES - N_GROUPS - N_EXPERTS), F32)], axis=1).astype(F32)
    w_route_hi = w_route.astype(BF16)
    w_route = jnp.concatenate([w_route_hi, (w_route - w_route_hi.astype(F32)).astype(BF16)], axis=1).T
    x1, h2p, route, counts = _merge(x2, attn.reshape(N, ATTN_WIDTH), lru.reshape(N, LRU_WIDTH), row(norm_mix_g[l]),
                                    w_in_l, w_o_attn[l].astype(BF16),
                                    w_o_lru[l].astype(BF16), w_out[l].astype(BF16), row(norm_ffn_g[l]), w_route)

    starts, plan = _visit_plan(counts[:, 0], N * TOP_K)
    lane = route[0:TOP_K].astype(jnp.int32)
    weight = jnp.transpose(route[TOP_K:2 * TOP_K])
    rank = route[2 * TOP_K:3 * TOP_K].astype(jnp.int32)
    lane_ids = jnp.arange(ROUTE_ROWS, dtype=jnp.int32)
    start_of = jnp.sum(jnp.where(lane[..., None] == lane_ids, starts, 0), axis=-1)
    off = ((start_of + rank) * PACK_SUB).reshape(TOP_K, N // ROW_TILE, ROW_TILE)
    off3 = jnp.transpose(off, (1, 0, 2)).reshape(N // ROW_TILE, 1, TOP_K * ROW_TILE)

    xs = _dispatch(off3, h2p)
    ys = _moe(plan, xs, w_gate[l], w_up[l], w_down[l])
    out = _combine(off3, x1, weight, row(final_norm_g), ys)
    return out.reshape(B, S, D)
```

```python
import functools
import math

import jax
import jax.numpy as jnp
from jax import lax
from jax.experimental import pallas as pl
from jax.experimental.pallas import tpu as pltpu

F32 = jnp.float32
BF16 = jnp.bfloat16

D_MODEL = 1024
N_HEADS = 4
HEAD_DIM = 64
V_DIM = 2 * HEAD_DIM
ATTN_WIDTH = N_HEADS * V_DIM
V_ROWS = V_DIM + 16
K_COLS = 2 * V_DIM
POS_SPLIT = 3
POS_RADIX = 256
SKIP_MARGIN = 138.0
NORM_SLACK = 1.0201
LRU_WIDTH = D_MODEL // 2
LRU_BLOCKS = 8
CONV_W = 4
LRU_C = 8.0
N_GROUPS = 4
EXPERTS_PER_GROUP = 8
N_EXPERTS = N_GROUPS * EXPERTS_PER_GROUP
TOP_K = 2
D_EXPERT = D_MODEL // 2
NORM_EPS = 1e-6
LAM_INIT = 0.8 - 0.6 * math.exp(-0.3 * 0)

QK_COLS = N_HEADS * 2 * HEAD_DIM
PROJ_COLS = 2 * QK_COLS + ATTN_WIDTH + 2 * LRU_WIDTH
ROUTE_LANES = 128
ROUTE_ROWS = 64
ROUTE_FIELDS = 8
NEG_BIG = -1e30
LOG2E = math.log2(math.e)
ALIBI_SLOPES = tuple(2.0 ** (-8.0 * (h + 1) / N_HEADS) for h in range(N_HEADS))

SEQ_TILE = 512
ROW_TILE = 256
MERGE_CHAIN = 512
MOE_TILE = 256
DMA_GROUP = 8
DISPATCH_SLOTS = 3
LANES = 128
SUBLANES = 8
PACK_SUB = D_MODEL // 2 // LANES
EXPERT_LANE0 = N_GROUPS
V7X_VMEM_BYTES = 64 * 1024 * 1024
VMEM_LIMIT = V7X_VMEM_BYTES * 3 // 4


def _rms(x, g):
    return x * lax.rsqrt(jnp.mean(x * x, axis=-1, keepdims=True) + NORM_EPS) * g


def _dot(a, b):
    return jnp.dot(a, b, preferred_element_type=F32)


def _inproj_kernel(tiles_per_seq, x_ref, g_ref, w_ref, cw_ref, cb_ref, wr_ref, wi_ref, br_ref, bi_ref, lam_ref,
                   qT_ref, k_ref, vT_ref, lru_ref, kn_ref, xbuf, hc):
    ti = lax.rem(pl.program_id(0), tiles_per_seq)

    @pl.when(ti == 0)
    def _():
        xbuf[0:8] = jnp.zeros((8, LRU_WIDTH), F32)
        hc[...] = jnp.zeros_like(hc)

    hb = _rms(x_ref[...], g_ref[...]).astype(BF16)

    def proj(lo, hi):
        return _dot(hb, w_ref[:, lo:hi])

    c0 = 2 * QK_COLS + ATTN_WIDTH
    lru_x = proj(c0, c0 + LRU_WIDTH)
    lru_y = proj(c0 + LRU_WIDTH, c0 + 2 * LRU_WIDTH)

    q = proj(0, QK_COLS) * (HEAD_DIM ** -0.5 * LOG2E)
    for h in range(N_HEADS):
        qT_ref[h] = q[:, h * V_DIM:(h + 1) * V_DIM].T.astype(BF16)
    k = proj(QK_COLS, 2 * QK_COLS).astype(BF16)
    r = lax.broadcasted_iota(jnp.int32, (k.shape[0], K_COLS - V_DIM), 0)
    lane = lax.broadcasted_iota(jnp.int32, r.shape, 1)
    a = r // POS_RADIX * POS_RADIX
    feat = jnp.where(lane < POS_SPLIT, a, jnp.where(lane < 2 * POS_SPLIT, r - a, 0)).astype(F32).astype(BF16)
    for h in range(N_HEADS):
        k_ref[:, h * K_COLS:h * K_COLS + V_DIM] = k[:, h * V_DIM:(h + 1) * V_DIM]
        k_ref[:, h * K_COLS + V_DIM:(h + 1) * K_COLS] = feat
    seg = lax.broadcasted_iota(jnp.int32, (QK_COLS, LANES), 0) // HEAD_DIM
    pick = jnp.where(seg == lax.broadcasted_iota(jnp.int32, (QK_COLS, LANES), 1), 1.0, 0.0).astype(BF16)
    row_n2 = _dot(jnp.square(k.astype(F32)).astype(BF16), pick)
    kn_ref[...] = jnp.max(row_n2, axis=0, keepdims=True)
    v = proj(2 * QK_COLS, 2 * QK_COLS + ATTN_WIDTH)
    for h in range(N_HEADS):
        vT_ref[h, :V_DIM, :] = v[:, h * V_DIM:(h + 1) * V_DIM].T.astype(BF16)
        pad_row = lax.broadcasted_iota(jnp.int32, (V_ROWS - V_DIM, v.shape[0]), 0)
        vT_ref[h, V_DIM:, :] = jnp.where(pad_row == 0, 1.0, 0.0).astype(BF16)

    lru_ref[...] = _lru_tile(lru_x, lru_y, ti, cw_ref, cb_ref, wr_ref, wi_ref, br_ref, bi_ref, lam_ref, xbuf, hc)


def _inproj(x2, g, w, lru_params, B, S):
    N = B * S
    tm = SEQ_TILE
    nt = S // tm
    tile5 = pl.BlockSpec((None, N_HEADS, None, V_DIM, tm), lambda i: (i // nt, 0, i % nt, 0, 0))
    rows = lambda c: pl.BlockSpec((tm, c), lambda i: (i, 0))
    full = lambda a: pl.BlockSpec(a.shape, lambda i: (0, 0))
    return pl.pallas_call(
        functools.partial(_inproj_kernel, nt),
        grid=(N // tm,),
        in_specs=[rows(D_MODEL),
                  pl.BlockSpec((1, D_MODEL), lambda i: (0, 0)),
                  pl.BlockSpec((D_MODEL, PROJ_COLS), lambda i: (0, 0))] + [full(a) for a in lru_params],
        out_specs=[tile5, rows(N_HEADS * K_COLS),
                   pl.BlockSpec((None, N_HEADS, None, V_ROWS, tm), lambda i: (i // nt, 0, i % nt, 0, 0)),
                   rows(LRU_WIDTH),
                   pl.BlockSpec((None, 1, LANES), lambda i: (i, 0, 0))],
        out_shape=[jax.ShapeDtypeStruct((B, N_HEADS, nt, V_DIM, tm), BF16),
                   jax.ShapeDtypeStruct((N, N_HEADS * K_COLS), BF16),
                   jax.ShapeDtypeStruct((B, N_HEADS, nt, V_ROWS, tm), BF16),
                   jax.ShapeDtypeStruct((N, LRU_WIDTH), BF16),
                   jax.ShapeDtypeStruct((N // tm, 1, LANES), F32)],
        scratch_shapes=[pltpu.VMEM((tm + 8, LRU_WIDTH), F32), pltpu.VMEM((1, LRU_WIDTH), F32)],
        compiler_params=pltpu.CompilerParams(dimension_semantics=("arbitrary",),
                                             vmem_limit_bytes=VMEM_LIMIT),
        name="inproj_lru",
    )(x2, g, w, *lru_params)


def _attn_kernel(kn_ref, qT_ref, k_ref, vT_ref, lam_ref, g_ref, o_ref, mask_ref, acc_ref, sa_ref, sb_ref, pa_ref,
                 pb_ref):
    t = SEQ_TILE
    b = pl.program_id(0)
    h = pl.program_id(1)
    i = pl.program_id(2)
    slope = LOG2E * jnp.where(h == 0, ALIBI_SLOPES[0], jnp.where(h == 1, ALIBI_SLOPES[1],
                              jnp.where(h == 2, ALIBI_SLOPES[2], ALIBI_SLOPES[3]))).astype(F32)

    @pl.when(i == 0)
    def _():
        r = lax.broadcasted_iota(jnp.int32, (t, t), 0)
        c = lax.broadcasted_iota(jnp.int32, (t, t), 1)
        mask_ref[...] = jnp.where(r <= c, 0.0, NEG_BIG)

    qf = qT_ref[...].astype(F32)
    row = lax.broadcasted_iota(jnp.int32, qf.shape, 0)
    sl = jnp.full(qf.shape, slope, F32)
    hi = sl.astype(BF16).astype(F32)
    mid = (sl - hi).astype(BF16).astype(F32)
    lo = (sl - hi - mid).astype(BF16).astype(F32)
    piece = jnp.where(row % 3 == 0, hi, jnp.where(row % 3 == 1, mid, lo))
    srows = jnp.where(row < 2 * POS_SPLIT, piece, 0.0).astype(BF16)
    qs = tuple(jnp.concatenate([jnp.where(sel, qf, 0.0).astype(BF16), srows], axis=0)
               for sel in (row < HEAD_DIM, row >= HEAD_DIM))
    acc_ref[...] = jnp.zeros_like(acc_ref)
    pb_ref[...] = jnp.zeros_like(pb_ref)

    first_off = [0]

    def key_tile(tau):
        return jnp.where(tau <= 0, i, first_off[0] + tau - 1)

    def stage_q(tau, s_ref, diagonal=False):
        kt = k_ref[key_tile(tau)]
        tile_max = []
        for mi in range(2):
            s = _dot(kt, qs[mi])
            if diagonal:
                s = s + mask_ref[...]
            s_ref[mi, :, :t] = s
            tile_max.append(jnp.max(s, axis=0, keepdims=True))
        return tuple(tile_max)

    def stage_s(tau, s_ref, p_ref, ms, tile_max):
        cj = slope * (key_tile(tau) * t).astype(F32)
        m_out, alphas = [], []
        for mi in range(2):
            m_new = jnp.maximum(ms[mi], tile_max[mi] + cj)
            alphas.append(jnp.exp2(ms[mi] - m_new))
            p_ref[mi, :, :t] = jnp.exp2(s_ref[mi, :, :t] - (m_new - cj)).astype(BF16)
            m_out.append(m_new)
        return tuple(m_out), tuple(alphas)

    def stage_v(tau, p_ref, alphas):
        vt = vT_ref[key_tile(tau)]
        for mi in range(2):
            acc_ref[mi] = alphas[mi] * acc_ref[mi] + _dot(vt, p_ref[mi, :, :t])

    def body(jj, carry):
        ms, alphas, tmax = carry[:2], carry[2:4], carry[4:]
        tau = 2 * jj
        tmax_b = stage_q(tau + 1, sb_ref)
        ms, alphas_a = stage_s(tau, sa_ref, pa_ref, ms, tmax)
        stage_v(tau - 1, pb_ref, alphas)
        tmax_a = stage_q(tau + 2, sa_ref)
        ms, alphas_b = stage_s(tau + 1, sb_ref, pb_ref, ms, tmax_b)
        stage_v(tau, pa_ref, alphas_a)
        return ms + alphas_b + tmax_a

    def finalize():
        lp = lam_ref[...]
        s1 = jnp.sum(lp[0:1] * lp[1:2], axis=-1, keepdims=True)
        s2 = jnp.sum(lp[2:3] * lp[3:4], axis=-1, keepdims=True)
        lam = jnp.exp(s1) - jnp.exp(s2) + LAM_INIT
        norm = [acc_ref[mi, :V_DIM, :] * (1.0 / acc_ref[mi, V_DIM:V_DIM + 1, :]) for mi in range(2)]
        oT = norm[0] - lam * norm[1]
        o = _rms(oT.T, g_ref[...]) * (1.0 - LAM_INIT)
        o_ref[...] = o.astype(BF16)

    m_init = jnp.full((1, t), NEG_BIG, F32)
    one = jnp.ones((1, t), F32)
    tmax0 = stage_q(0, sa_ref, diagonal=True)

    cj0 = slope * (i * t).astype(F32)
    n_skip = None
    for mi in range(2):
        qsq = jnp.square(qf[mi * HEAD_DIM:(mi + 1) * HEAD_DIM, :])
        qn2 = jnp.max(jnp.sum(qsq, axis=0, keepdims=True), axis=1, keepdims=True)
        m_low = jnp.min(tmax0[mi], axis=1, keepdims=True) + cj0
        qk = jnp.sqrt(qn2 * (kn_ref[(b * N_HEADS + h) * 2 + mi] * NORM_SLACK))
        count = jnp.ceil((m_low - SKIP_MARGIN - slope * (t - 1) - qk) / (slope * t))
        n_skip = count if n_skip is None else jnp.minimum(n_skip, count)
    first_off[0] = jnp.clip(n_skip, 0.0, i.astype(F32)).astype(jnp.int32)[0, 0]
    n_off = i - first_off[0]

    n_main = jnp.right_shift(n_off, 1)
    fin = lax.fori_loop(0, n_main, body, (m_init, m_init, one, one) + tmax0)
    ms, alphas, tmax = fin[:2], fin[2:4], fin[4:]
    tau = 2 * n_main
    odd_tiles = tau == n_off

    @pl.when(odd_tiles)
    def _():
        _, alphas_a = stage_s(tau, sa_ref, pa_ref, ms, tmax)
        stage_v(tau - 1, pb_ref, alphas)
        stage_v(tau, pa_ref, alphas_a)
        finalize()

    @pl.when(jnp.logical_not(odd_tiles))
    def _():
        tmax_b = stage_q(tau + 1, sb_ref)
        ms_a, alphas_a = stage_s(tau, sa_ref, pa_ref, ms, tmax)
        stage_v(tau - 1, pb_ref, alphas)
        _, alphas_b = stage_s(tau + 1, sb_ref, pb_ref, ms_a, tmax_b)
        stage_v(tau, pa_ref, alphas_a)
        stage_v(tau + 1, pb_ref, alphas_b)
        finalize()


def _attn(kn, qT, k4, vT, lam, g, B, S):
    t = SEQ_TILE
    nt = S // t
    grid_spec = pltpu.PrefetchScalarGridSpec(
        num_scalar_prefetch=1,
        grid=(B, N_HEADS, nt),
        in_specs=[pl.BlockSpec((None, None, None, V_DIM, t), lambda b, h, i, kn: (b, h, i, 0, 0)),
                  pl.BlockSpec((None, nt, t, K_COLS), lambda b, h, i, kn: (b, 0, 0, h)),
                  pl.BlockSpec((None, None, nt, V_ROWS, t), lambda b, h, i, kn: (b, h, 0, 0, 0)),
                  pl.BlockSpec((4, HEAD_DIM), lambda b, h, i, kn: (0, 0)),
                  pl.BlockSpec((1, V_DIM), lambda b, h, i, kn: (0, 0))],
        out_specs=pl.BlockSpec((None, t, V_DIM), lambda b, h, i, kn: (b, i, h)),
        scratch_shapes=[pltpu.VMEM((t, t), F32), pltpu.VMEM((2, V_ROWS, t), F32),
                        pltpu.VMEM((2, t, t + LANES), F32), pltpu.VMEM((2, t, t + LANES), F32),
                        pltpu.VMEM((2, t, t + LANES), BF16), pltpu.VMEM((2, t, t + LANES), BF16)],
    )
    return pl.pallas_call(
        _attn_kernel,
        grid_spec=grid_spec,
        out_shape=jax.ShapeDtypeStruct((B, S, ATTN_WIDTH), BF16),
        compiler_params=pltpu.CompilerParams(dimension_semantics=("parallel", "arbitrary", "arbitrary"),
                                             vmem_limit_bytes=VMEM_LIMIT),
        name="diff_attn",
    )(kn, qT, k4, vT, lam, g)


def _lru_tile(x, y, ti, cw_ref, cb_ref, wr_ref, wi_ref, br_ref, bi_ref, lam_ref, xbuf, hc):
    T = SEQ_TILE
    xbuf[8:8 + T] = x
    cw = cw_ref[...]
    xc = cb_ref[...] + cw[3:4] * x
    for j in range(CONV_W - 1):
        xc = xc + cw[j:j + 1] * xbuf[5 + j:5 + j + T]
    xbuf[0:8] = x[T - 8:T]

    xb = xc.astype(BF16)
    sigmoid = lambda z: 0.5 * jnp.tanh(0.5 * z) + 0.5
    r = sigmoid(_dot(xb, wr_ref[...]) + br_ref[...])
    ig = sigmoid(_dot(xb, wi_ref[...]) + bi_ref[...])
    z = -lam_ref[...]
    softplus = jnp.maximum(z, 0.0) + jnp.log1p(jnp.exp(-jnp.abs(z)))
    la = -LRU_C * r * softplus
    a = jnp.exp(la)
    m2 = -jnp.tanh(la) * (a * a + 1.0)
    mult = jnp.where(m2 > 0.0, m2 * lax.rsqrt(m2), 0.0)
    row = lax.broadcasted_iota(jnp.int32, (T, LRU_WIDTH), 0)
    mult = jnp.where((row == 0) & (ti == 0), 1.0, mult)
    u = (xc * ig) * mult

    nb = T // SUBLANES
    a3 = a.reshape(nb, SUBLANES, LRU_WIDTH)
    u3 = u.reshape(nb, SUBLANES, LRU_WIDTH)
    sub = lax.broadcasted_iota(jnp.int32, a3.shape, 1)
    d = 1
    while d < SUBLANES:
        valid = sub >= d
        u3 = jnp.where(valid, a3 * pltpu.roll(u3, d, 1) + u3, u3)
        a3 = jnp.where(valid, a3 * pltpu.roll(a3, d, 1), a3)
        d *= 2
    h = hc[...]
    blocks = []
    for b in range(nb):
        hb = u3[b] + a3[b] * h
        blocks.append(hb)
        h = hb[SUBLANES - 1:SUBLANES]
    hfull = jnp.concatenate(blocks, axis=0)
    hc[...] = h
    gelu = 0.5 * y * (1.0 + jnp.tanh(0.7978845608028654 * (y + 0.044715 * (y * y * y))))
    return (hfull * gelu).astype(BF16)


def _pack_words(v):
    bits = pltpu.bitcast(v.astype(BF16).astype(F32), jnp.uint32)
    half = D_MODEL // 2
    packed = (bits[:, :half] >> 16) | (bits[:, half:] & jnp.uint32(0xFFFF0000))
    return [packed[:, c * LANES:(c + 1) * LANES] for c in range(PACK_SUB)]


def _packed_chunk(c, rows, first_row=0):
    return (pl.ds(first_row * PACK_SUB + c, rows, stride=PACK_SUB), slice(None))


def _pack_rows(v, out_ref, first_row=0):
    for c, words in enumerate(_pack_words(v)):
        out_ref[_packed_chunk(c, v.shape[0], first_row)] = words


def _unpack_chunks(in_ref, rows, first_row=0):
    lo, hi = [], []
    for c in range(PACK_SUB):
        w = in_ref[_packed_chunk(c, rows, first_row)]
        lo.append(pltpu.bitcast(w << 16, F32))
        hi.append(pltpu.bitcast(w & jnp.uint32(0xFFFF0000), F32))
    return lo + hi


def _unpack_rows(in_ref, rows):
    return jnp.concatenate(_unpack_chunks(in_ref, rows), axis=1).astype(BF16)


def _merge_kernel(x_ref, attn_ref, lru_ref, g1_ref, wg_ref, woa_ref, wol_ref, wout_ref, g2_ref, wrt_ref,
                  x1_ref, h2p_ref, route_ref, counts_ref, cnt, tri):
    tm = MERGE_CHAIN

    @pl.when(pl.program_id(0) == 0)
    def _():
        cnt[...] = jnp.zeros_like(cnt)
        r = lax.broadcasted_iota(jnp.int32, (tm, tm), 0)
        c = lax.broadcasted_iota(jnp.int32, (tm, tm), 1)
        tri[...] = jnp.where(r < c, 1.0, 0.0).astype(BF16)

    counts = cnt[...]
    for r0 in range(0, SEQ_TILE, tm):
        counts = _merge_chain(r0, tm, counts, x_ref, attn_ref, lru_ref, g1_ref, wg_ref, woa_ref, wol_ref, wout_ref,
                              g2_ref, wrt_ref, x1_ref, h2p_ref, route_ref, tri)
    cnt[...] = counts
    counts_ref[...] = counts


def _merge_chain(r0, tm, counts, x_ref, attn_ref, lru_ref, g1_ref, wg_ref, woa_ref, wol_ref, wout_ref, g2_ref, wrt_ref,
                 x1_ref, h2p_ref, route_ref, tri):
    rs = slice(r0, r0 + tm)
    x = x_ref[rs, :]
    hb = _rms(x, g1_ref[...]).astype(BF16)
    gates = 0.5 * jnp.tanh(0.5 * _dot(hb, wg_ref[:, PROJ_COLS:])) + 0.5
    merged = gates[:, :D_MODEL] * _dot(attn_ref[rs, :], woa_ref[...]) + gates[:, D_MODEL:] * _dot(lru_ref[rs, :], wol_ref[...])
    x1 = x + _dot(merged.astype(BF16), wout_ref[...])
    x1_ref[rs, :] = x1
    h2 = _rms(x1, g2_ref[...])
    _pack_rows(h2, h2p_ref, first_row=r0)

    h_hi = h2.astype(BF16)
    h_lo = (h2 - h_hi.astype(F32)).astype(BF16)
    wrt = wrt_ref[...]
    nt_dims = (((1,), (1,)), ((), ()))
    hh = lax.dot_general(wrt, h_hi, nt_dims, preferred_element_type=F32)
    lo_pass = lax.dot_general(wrt[:ROUTE_LANES], h_lo, nt_dims, preferred_element_type=F32)
    logits = (hh[:ROUTE_LANES] + hh[ROUTE_LANES:] + lo_pass)[:ROUTE_ROWS]
    row = lax.broadcasted_iota(jnp.int32, logits.shape, 0)
    big = jnp.int32(1 << 20)

    def first_argmax(v):
        m = jnp.max(v, axis=0, keepdims=True)
        return m, jnp.min(jnp.where(v == m, row, big), axis=0, keepdims=True)

    gmask = row < N_GROUPS
    gmax, gidx = first_argmax(jnp.where(gmask, logits, -jnp.inf))
    gsum = jnp.sum(jnp.where(gmask, jnp.exp(logits - gmax), 0.0), axis=0, keepdims=True)
    g_w = 1.0 / gsum
    lo = N_GROUPS + EXPERTS_PER_GROUP * gidx
    el = jnp.where((row >= lo) & (row < lo + EXPERTS_PER_GROUP), logits, -jnp.inf)
    m1, i1 = first_argmax(el)
    m2, i2 = first_argmax(jnp.where(row == i1, -jnp.inf, el))
    rr = jnp.exp(m2 - m1)
    w1 = g_w / (1.0 + rr)
    w2 = g_w * rr / (1.0 + rr)
    oh1 = row == i1
    oh2 = row == i2
    oh = jnp.where(oh1 | oh2, 1.0, 0.0)
    before = _dot(oh.astype(BF16), tri[...]) + counts
    r1 = jnp.sum(jnp.where(oh1, before, 0.0), axis=0, keepdims=True)
    r2 = jnp.sum(jnp.where(oh2, before, 0.0), axis=0, keepdims=True)
    vals = (i1.astype(F32), i2.astype(F32), w1, w2, r1, r2, jnp.zeros_like(w1), jnp.zeros_like(w1))
    for k, v in enumerate(vals):
        route_ref[k:k + 1, rs] = v
    return counts + jnp.sum(oh, axis=1, keepdims=True)


def _merge(x2, attn, lru, g1, wg, woa, wol, wout, g2, wrt):
    N = x2.shape[0]
    tm = SEQ_TILE
    rows = lambda c: pl.BlockSpec((tm, c), lambda i: (i, 0))
    full = lambda r, c: pl.BlockSpec((r, c), lambda i: (0, 0))
    return pl.pallas_call(
        _merge_kernel,
        grid=(N // tm,),
        in_specs=[rows(D_MODEL), rows(ATTN_WIDTH), rows(LRU_WIDTH), full(1, D_MODEL),
                  full(D_MODEL, PROJ_COLS + 2 * D_MODEL),
                  full(ATTN_WIDTH, D_MODEL), full(LRU_WIDTH, D_MODEL), full(D_MODEL, D_MODEL), full(1, D_MODEL),
                  full(2 * ROUTE_LANES, D_MODEL)],
        out_specs=[rows(D_MODEL), pl.BlockSpec((tm * PACK_SUB, LANES), lambda i: (i, 0)),
                   pl.BlockSpec((ROUTE_FIELDS, tm), lambda i: (0, i)), full(ROUTE_ROWS, 1)],
        out_shape=[jax.ShapeDtypeStruct((N, D_MODEL), F32), jax.ShapeDtypeStruct((N * PACK_SUB, LANES), jnp.uint32),
                   jax.ShapeDtypeStruct((ROUTE_FIELDS, N), F32), jax.ShapeDtypeStruct((ROUTE_ROWS, 1), F32)],
        scratch_shapes=[pltpu.VMEM((ROUTE_ROWS, 1), F32), pltpu.VMEM((MERGE_CHAIN, MERGE_CHAIN), BF16)],
        compiler_params=pltpu.CompilerParams(dimension_semantics=("arbitrary",), vmem_limit_bytes=VMEM_LIMIT),
        name="merge_route",
    )(x2, attn, lru, g1, wg, woa, wol, wout, g2, wrt)


def _for_each_assignment(off_ref, fn):
    def group(gi, _):
        toks = [gi * DMA_GROUP + j for j in range(DMA_GROUP)]
        offs = [[off_ref[0, 0, k * ROW_TILE + tk] for k in range(TOP_K)] for tk in toks]
        for tk, o in zip(toks, offs):
            for k in range(TOP_K):
                fn(tk, k, pl.multiple_of(o[k], PACK_SUB))
        return 0
    lax.fori_loop(0, ROW_TILE // DMA_GROUP, group, 0)


def _dispatch_kernel(off_ref, h2p_hbm, xs_hbm, xin, in_sem, out_sem):
    tile_rows = ROW_TILE * PACK_SUB
    i = pl.program_id(0)
    n = pl.num_programs(0)
    slot = lax.rem(i, DISPATCH_SLOTS)

    def fetch(tile, s):
        return pltpu.make_async_copy(h2p_hbm.at[pl.ds(pl.multiple_of(tile * tile_rows, tile_rows), tile_rows)],
                                     xin.at[s], in_sem.at[s])

    def drain(s):
        for _ in range(TOP_K):
            pltpu.make_async_copy(xin.at[s], xs_hbm.at[pl.ds(0, tile_rows)], out_sem.at[s]).wait()

    @pl.when(i == 0)
    def _():
        fetch(0, 0).start()

    @pl.when(i + 1 < n)
    def _():
        fetch(i + 1, lax.rem(i + 1, DISPATCH_SLOTS)).start()

    fetch(i, slot).wait()

    def start(tk, k, off):
        pltpu.make_async_copy(xin.at[slot, pl.ds(pl.multiple_of(tk * PACK_SUB, PACK_SUB), PACK_SUB)],
                              xs_hbm.at[pl.ds(off, PACK_SUB)], out_sem.at[slot]).start(priority=k)

    _for_each_assignment(off_ref, start)

    @pl.when(i > 0)
    def _():
        drain(lax.rem(i + DISPATCH_SLOTS - 1, DISPATCH_SLOTS))

    @pl.when(i == n - 1)
    def _():
        drain(slot)


def _dispatch(off3, h2p):
    return pl.pallas_call(
        _dispatch_kernel,
        grid=(off3.shape[0],),
        in_specs=[pl.BlockSpec((1, 1, TOP_K * ROW_TILE), lambda i: (i, 0, 0), memory_space=pltpu.SMEM),
                  pl.BlockSpec(memory_space=pl.ANY)],
        out_specs=pl.BlockSpec(memory_space=pl.ANY),
        out_shape=jax.ShapeDtypeStruct((TOP_K * h2p.shape[0], LANES), jnp.uint32),
        scratch_shapes=[pltpu.VMEM((DISPATCH_SLOTS, ROW_TILE * PACK_SUB, LANES), jnp.uint32),
                        pltpu.SemaphoreType.DMA((DISPATCH_SLOTS,)), pltpu.SemaphoreType.DMA((DISPATCH_SLOTS,))],
        compiler_params=pltpu.CompilerParams(dimension_semantics=("arbitrary",), has_side_effects=True),
        name="moe_dispatch",
    )(off3, h2p)


def _moe_kernel(vt_ref, ve_ref, vlo_ref, vhi_ref, vnext_ref, vpar_ref, xs_ref, wg_hbm, wu_hbm, wd_hbm, ys_ref,
                wgb, wub, wdb, wgf, wuf, wdf, sem):
    tm = MOE_TILE
    v = pl.program_id(0)
    t = vt_ref[v]
    e = ve_ref[v]
    slot = vpar_ref[v]
    prev = jnp.maximum(v - 1, 0)

    def fetch(expert, s):
        return [pltpu.make_async_copy(w_hbm.at[expert], buf.at[s], sem.at[s])
                for w_hbm, buf in ((wg_hbm, wgf), (wu_hbm, wuf), (wd_hbm, wdf))]

    @pl.when(v == 0)
    def _():
        for copy in fetch(e, slot):
            copy.start()

    @pl.when((v == 0) | (ve_ref[prev] != e))
    def _():
        for copy in fetch(e, slot):
            copy.wait()
        nxt = vnext_ref[v]

        @pl.when(nxt != e)
        def _():
            for copy in fetch(nxt, 1 - slot):
                copy.start()

        wgb[...] = wgf[slot].astype(BF16)
        wub[...] = wuf[slot].astype(BF16)
        wdb[...] = wdf[slot].astype(BF16)

    xb = _unpack_rows(xs_ref, tm)
    g = _dot(xb, wgb[...])
    u = _dot(xb, wub[...])
    hmid = (g * jax.nn.sigmoid(g)) * u
    words = _pack_words(_dot(hmid.astype(BF16), wdb[...]))
    first = (v == 0) | (vt_ref[prev] != t)

    @pl.when(first)
    def _():
        for c in range(PACK_SUB):
            ys_ref[_packed_chunk(c, tm)] = words[c]

    @pl.when(jnp.logical_not(first))
    def _():
        rows = t * tm + lax.broadcasted_iota(jnp.int32, (tm, LANES), 0)
        mine = (rows >= vlo_ref[v]) & (rows < vhi_ref[v])
        for c in range(PACK_SUB):
            ys_ref[_packed_chunk(c, tm)] = jnp.where(mine, words[c], ys_ref[_packed_chunk(c, tm)])


def _moe(plan, xs, wg, wu, wd):
    tm = MOE_TILE
    n_rows = xs.shape[0] // PACK_SUB
    tile = lambda v, vt, *_: (vt[v], 0)
    hbm = pl.BlockSpec(memory_space=pl.ANY)
    grid_spec = pltpu.PrefetchScalarGridSpec(
        num_scalar_prefetch=len(plan),
        grid=(plan[0].shape[0],),
        in_specs=[pl.BlockSpec((tm * PACK_SUB, LANES), tile), hbm, hbm, hbm],
        out_specs=pl.BlockSpec((tm * PACK_SUB, LANES), tile),
        scratch_shapes=[pltpu.VMEM((D_MODEL, D_EXPERT), BF16), pltpu.VMEM((D_MODEL, D_EXPERT), BF16),
                        pltpu.VMEM((D_EXPERT, D_MODEL), BF16),
                        pltpu.VMEM((2, D_MODEL, D_EXPERT), F32), pltpu.VMEM((2, D_MODEL, D_EXPERT), F32),
                        pltpu.VMEM((2, D_EXPERT, D_MODEL), F32), pltpu.SemaphoreType.DMA((2,))],
    )
    return pl.pallas_call(
        _moe_kernel,
        grid_spec=grid_spec,
        out_shape=jax.ShapeDtypeStruct((n_rows * PACK_SUB, LANES), jnp.uint32),
        compiler_params=pltpu.CompilerParams(dimension_semantics=("arbitrary",), vmem_limit_bytes=VMEM_LIMIT),
        name="moe_experts",
    )(*plan, xs, wg, wu, wd)


def _combine_kernel(off_ref, offn_ref, x1_ref, w_ref, g_ref, ys_hbm, o_ref, ybuf, sem):
    tc = ROW_TILE
    i = pl.program_id(0)
    n = pl.num_programs(0)
    slot = lax.rem(i, 2)

    def gather(o_ref_, s):
        def start(tk, k, off):
            pltpu.make_async_copy(ys_hbm.at[pl.ds(off, PACK_SUB)],
                                  ybuf.at[s, pl.ds(pl.multiple_of((k * tc + tk) * PACK_SUB, PACK_SUB), PACK_SUB)],
                                  sem.at[s]).start(priority=k)
        _for_each_assignment(o_ref_, start)

    def drain(s):
        pltpu.make_async_copy(ys_hbm.at[pl.ds(0, TOP_K * tc * PACK_SUB)], ybuf.at[s], sem.at[s]).wait()

    @pl.when(i == 0)
    def _():
        gather(off_ref, 0)

    @pl.when(i + 1 < n)
    def _():
        gather(offn_ref, 1 - slot)

    drain(slot)

    n_chunks = D_MODEL // LANES
    z = [x1_ref[:, c * LANES:(c + 1) * LANES] for c in range(n_chunks)]
    for k in range(TOP_K):
        wk = w_ref[:, k:k + 1]
        yk = _unpack_chunks(ybuf.at[slot], tc, first_row=k * tc)
        z = [zc + wk * yc for zc, yc in zip(z, yk)]
    ss = sum(jnp.sum(zc * zc, axis=-1, keepdims=True) for zc in z)
    inv = lax.rsqrt(ss * (1.0 / D_MODEL) + NORM_EPS)
    for c in range(n_chunks):
        o_ref[:, c * LANES:(c + 1) * LANES] = z[c] * inv * g_ref[:, c * LANES:(c + 1) * LANES]


def _combine(off3, x1, weight, g, ys):
    N = x1.shape[0]
    tc = ROW_TILE
    nt = N // tc
    idx = lambda f: pl.BlockSpec((1, 1, TOP_K * tc), f, memory_space=pltpu.SMEM)
    return pl.pallas_call(
        _combine_kernel,
        grid=(nt,),
        in_specs=[idx(lambda i: (i, 0, 0)),
                  idx(lambda i: (jnp.minimum(i + 1, nt - 1), 0, 0)),
                  pl.BlockSpec((tc, D_MODEL), lambda i: (i, 0)),
                  pl.BlockSpec((tc, TOP_K), lambda i: (i, 0)),
                  pl.BlockSpec((1, D_MODEL), lambda i: (0, 0)),
                  pl.BlockSpec(memory_space=pl.ANY)],
        out_specs=pl.BlockSpec((tc, D_MODEL), lambda i: (i, 0)),
        out_shape=jax.ShapeDtypeStruct((N, D_MODEL), F32),
        scratch_shapes=[pltpu.VMEM((2, TOP_K * tc * PACK_SUB, LANES), jnp.uint32), pltpu.SemaphoreType.DMA((2,))],
        compiler_params=pltpu.CompilerParams(dimension_semantics=("arbitrary",), vmem_limit_bytes=VMEM_LIMIT),
        name="combine_norm",
    )(off3, off3, x1, weight, g, ys)


def _block_diag(w):
    nb, c, _ = w.shape
    eye = jnp.eye(nb, dtype=w.dtype)
    return (eye[:, None, :, None] * w[:, :, None, :]).reshape(nb * c, nb * c)


def _visit_plan(counts, n_rows):
    tm = MOE_TILE
    n_tiles = n_rows // tm
    n_visits = n_tiles + N_EXPERTS - 1
    cnt = counts.astype(jnp.int32)
    lanes = jnp.arange(ROUTE_ROWS, dtype=jnp.int32)
    upto = (lanes[None, :] <= lanes[:, None]).astype(jnp.int32)
    ends = upto @ cnt
    starts = ends - cnt
    first_tile = starts // tm
    n_vis = jnp.where(cnt > 0, (ends - 1) // tm - first_tile + 1, 0)
    v_end = upto @ n_vis
    v_start = v_end - n_vis
    total = v_end[-1]
    v = jnp.arange(n_visits, dtype=jnp.int32)
    vc = jnp.minimum(v, total - 1)
    own = ((vc[:, None] >= v_start[None, :]) & (vc[:, None] < v_end[None, :])).astype(jnp.int32)
    pick = lambda per_lane: own @ per_lane
    tile = jnp.maximum(pick(first_tile - v_start) + vc, 0)
    lo = jnp.maximum(pick(starts), tile * tm)
    hi = jnp.minimum(pick(ends), (tile + 1) * tm)
    valid = v < total
    to_expert = lambda lane: jnp.maximum(lane - EXPERT_LANE0, 0)
    used = cnt > 0
    later = jnp.where((lanes[None, :] > lanes[:, None]) & used[None, :], lanes[None, :], ROUTE_ROWS)
    nxt = jnp.min(later, axis=1)
    nxt = jnp.where(nxt == ROUTE_ROWS, lanes, nxt)
    parity = (upto @ used.astype(jnp.int32) - 1) % 2
    plan = (tile, to_expert(pick(lanes)), jnp.where(valid, lo, 0), jnp.where(valid, hi, 0),
            to_expert(pick(nxt)), pick(parity))
    return starts, plan


def kernel(x, norm_mix_g, w_in, lambda_qk, subln_g, conv_w, conv_b, w_r, b_r, w_i, b_i, lru_lambda, w_o_attn, w_o_lru, w_out, norm_ffn_g, w_group, w_expert_router, w_gate, w_up, w_down, final_norm_g):
    B, S, D = x.shape
    N = B * S
    nt = S // SEQ_TILE
    depth = norm_mix_g.shape[0]
    assert depth == 1 and D == D_MODEL and S % SEQ_TILE == 0
    assert N % ROW_TILE == 0 and (N * TOP_K) % MOE_TILE == 0
    l = 0
    row = lambda v: v.reshape(1, -1).astype(F32)

    x2 = x.reshape(N, D)
    w_in_l = w_in[l].astype(BF16)
    lru_params = (conv_w[l].astype(F32), row(conv_b[l]), _block_diag(w_r[l]).astype(BF16),
                  _block_diag(w_i[l]).astype(BF16), row(b_r[l]), row(b_i[l]), row(lru_lambda[l]))
    qT, k, vT, lru, kn = _inproj(x2, row(norm_mix_g[l]), w_in_l, lru_params, B, S)

    kn_max = jnp.max(kn[:, 0, :2 * N_HEADS].reshape(B, nt, 2 * N_HEADS), axis=1)
    attn = _attn(kn_max.reshape(-1), qT, k.reshape(B, nt, SEQ_TILE, N_HEADS * K_COLS), vT,
                 lambda_qk[l].reshape(4, HEAD_DIM).astype(F32), row(subln_g[l]), B, S)

    w_route = jnp.concatenate(
        [w_group[l], jnp.transpose(w_expert_router[l], (1, 0, 2)).reshape(D, N_EXPERTS),
         jnp.zeros((D, ROUTE_LANES - N_GROUPS - N_EXPERTS), F32)], axis=1).astype(F32)
    w_route_hi = w_route.astype(BF16)
    w_route = jnp.concatenate([w_route_hi, (w_route - w_route_hi.astype(F32)).astype(BF16)], axis=1).T
    x1, h2p, route, counts = _merge(x2, attn.reshape(N, ATTN_WIDTH), lru.reshape(N, LRU_WIDTH), row(norm_mix_g[l]),
                                    w_in_l, w_o_attn[l].astype(BF16),
                                    w_o_lru[l].astype(BF16), w_out[l].astype(BF16), row(norm_ffn_g[l]), w_route)

    starts, plan = _visit_plan(counts[:, 0], N * TOP_K)
    lane = route[0:TOP_K].astype(jnp.int32)
    weight = jnp.transpose(route[TOP_K:2 * TOP_K])
    rank = route[2 * TOP_K:3 * TOP_K].astype(jnp.int32)
    lane_ids = jnp.arange(ROUTE_ROWS, dtype=jnp.int32)
    start_of = jnp.sum(jnp.where(lane[..., None] == lane_ids, starts, 0), axis=-1)
    off = ((start_of + rank) * PACK_SUB).reshape(TOP_K, N // ROW_TILE, ROW_TILE)
    off3 = jnp.transpose(off, (1, 0, 2)).reshape(N // ROW_TILE, 1, TOP_K * ROW_TILE)

    xs = _dispatch(off3, h2p)
    ys = _moe(plan, xs, w_gate[l], w_up[l], w_down[l])
    out = _combine(off3, x1, weight, row(final_norm_g), ys)
    return out.reshape(B, S, D)
```

```python
import functools
import math

import jax
import jax.numpy as jnp
from jax import lax
from jax.experimental import pallas as pl
from jax.experimental.pallas import tpu as pltpu

F32 = jnp.float32
BF16 = jnp.bfloat16

D_MODEL = 1024
N_HEADS = 4
HEAD_DIM = 64
V_DIM = 2 * HEAD_DIM
ATTN_WIDTH = N_HEADS * V_DIM
V_ROWS = V_DIM + 16
K_COLS = 2 * V_DIM
POS_SPLIT = 3
POS_RADIX = 256
SKIP_MARGIN = 138.0
NORM_SLACK = 1.0201
LRU_WIDTH = D_MODEL // 2
LRU_BLOCKS = 8
CONV_W = 4
LRU_C = 8.0
N_GROUPS = 4
EXPERTS_PER_GROUP = 8
N_EXPERTS = N_GROUPS * EXPERTS_PER_GROUP
TOP_K = 2
D_EXPERT = D_MODEL // 2
NORM_EPS = 1e-6
LAM_INIT = 0.8 - 0.6 * math.exp(-0.3 * 0)

QK_COLS = N_HEADS * 2 * HEAD_DIM
PROJ_COLS = 2 * QK_COLS + ATTN_WIDTH + 2 * LRU_WIDTH
ROUTE_LANES = 128
ROUTE_ROWS = 64
ROUTE_FIELDS = 8
NEG_BIG = -1e30
LOG2E = math.log2(math.e)
ALIBI_SLOPES = tuple(2.0 ** (-8.0 * (h + 1) / N_HEADS) for h in range(N_HEADS))

SEQ_TILE = 512
ROW_TILE = 256
MERGE_CHAIN = 512
MOE_TILE = 256
DMA_GROUP = 8
DISPATCH_SLOTS = 3
LANES = 128
SUBLANES = 8
PACK_SUB = D_MODEL // 2 // LANES
EXPERT_LANE0 = N_GROUPS
V7X_VMEM_BYTES = 64 * 1024 * 1024
VMEM_LIMIT = V7X_VMEM_BYTES * 3 // 4


def _rms(x, g):
    return x * lax.rsqrt(jnp.mean(x * x, axis=-1, keepdims=True) + NORM_EPS) * g


def _dot(a, b):
    return jnp.dot(a, b, preferred_element_type=F32)


def _inproj_kernel(tiles_per_seq, x_ref, g_ref, w_ref, cw_ref, cb_ref, wr_ref, wi_ref, br_ref, bi_ref, lam_ref,
                   qT_ref, k_ref, vT_ref, lru_ref, kn_ref, xbuf, hc):
    ti = lax.rem(pl.program_id(0), tiles_per_seq)

    @pl.when(ti == 0)
    def _():
        xbuf[0:8] = jnp.zeros((8, LRU_WIDTH), F32)
        hc[...] = jnp.zeros_like(hc)

    hb = _rms(x_ref[...], g_ref[...]).astype(BF16)

    def proj(lo, hi):
        return _dot(hb, w_ref[:, lo:hi])

    c0 = 2 * QK_COLS + ATTN_WIDTH
    lru_x = proj(c0, c0 + LRU_WIDTH)
    lru_y = proj(c0 + LRU_WIDTH, c0 + 2 * LRU_WIDTH)

    q = proj(0, QK_COLS) * (HEAD_DIM ** -0.5 * LOG2E)
    for h in range(N_HEADS):
        qT_ref[h] = q[:, h * V_DIM:(h + 1) * V_DIM].T.astype(BF16)
    k = proj(QK_COLS, 2 * QK_COLS).astype(BF16)
    r = lax.broadcasted_iota(jnp.int32, (k.shape[0], K_COLS - V_DIM), 0)
    lane = lax.broadcasted_iota(jnp.int32, r.shape, 1)
    a = r // POS_RADIX * POS_RADIX
    feat = jnp.where(lane < POS_SPLIT, a, jnp.where(lane < 2 * POS_SPLIT, r - a, 0)).astype(F32).astype(BF16)
    for h in range(N_HEADS):
        k_ref[:, h * K_COLS:h * K_COLS + V_DIM] = k[:, h * V_DIM:(h + 1) * V_DIM]
        k_ref[:, h * K_COLS + V_DIM:(h + 1) * K_COLS] = feat
    seg = lax.broadcasted_iota(jnp.int32, (QK_COLS, LANES), 0) // HEAD_DIM
    pick = jnp.where(seg == lax.broadcasted_iota(jnp.int32, (QK_COLS, LANES), 1), 1.0, 0.0).astype(BF16)
    row_n2 = _dot(jnp.square(k.astype(F32)).astype(BF16), pick)
    kn_ref[...] = jnp.max(row_n2, axis=0, keepdims=True)
    v = proj(2 * QK_COLS, 2 * QK_COLS + ATTN_WIDTH)
    for h in range(N_HEADS):
        vT_ref[h, :V_DIM, :] = v[:, h * V_DIM:(h + 1) * V_DIM].T.astype(BF16)
        pad_row = lax.broadcasted_iota(jnp.int32, (V_ROWS - V_DIM, v.shape[0]), 0)
        vT_ref[h, V_DIM:, :] = jnp.where(pad_row == 0, 1.0, 0.0).astype(BF16)

    lru_ref[...] = _lru_tile(lru_x, lru_y, ti, cw_ref, cb_ref, wr_ref, wi_ref, br_ref, bi_ref, lam_ref, xbuf, hc)


def _inproj(x2, g, w, lru_params, B, S):
    N = B * S
    tm = SEQ_TILE
    nt = S // tm
    tile5 = pl.BlockSpec((None, N_HEADS, None, V_DIM, tm), lambda i: (i // nt, 0, i % nt, 0, 0))
    rows = lambda c: pl.BlockSpec((tm, c), lambda i: (i, 0))
    full = lambda a: pl.BlockSpec(a.shape, lambda i: (0, 0))
    return pl.pallas_call(
        functools.partial(_inproj_kernel, nt),
        grid=(N // tm,),
        in_specs=[rows(D_MODEL),
                  pl.BlockSpec((1, D_MODEL), lambda i: (0, 0)),
                  pl.BlockSpec((D_MODEL, PROJ_COLS), lambda i: (0, 0))] + [full(a) for a in lru_params],
        out_specs=[tile5, rows(N_HEADS * K_COLS),
                   pl.BlockSpec((None, N_HEADS, None, V_ROWS, tm), lambda i: (i // nt, 0, i % nt, 0, 0)),
                   rows(LRU_WIDTH),
                   pl.BlockSpec((None, 1, LANES), lambda i: (i, 0, 0))],
        out_shape=[jax.ShapeDtypeStruct((B, N_HEADS, nt, V_DIM, tm), BF16),
                   jax.ShapeDtypeStruct((N, N_HEADS * K_COLS), BF16),
                   jax.ShapeDtypeStruct((B, N_HEADS, nt, V_ROWS, tm), BF16),
                   jax.ShapeDtypeStruct((N, LRU_WIDTH), BF16),
                   jax.ShapeDtypeStruct((N // tm, 1, LANES), F32)],
        scratch_shapes=[pltpu.VMEM((tm + 8, LRU_WIDTH), F32), pltpu.VMEM((1, LRU_WIDTH), F32)],
        compiler_params=pltpu.CompilerParams(dimension_semantics=("arbitrary",),
                                             vmem_limit_bytes=VMEM_LIMIT),
        name="inproj_lru",
    )(x2, g, w, *lru_params)


def _attn_kernel(kn_ref, qT_ref, k_ref, vT_ref, lam_ref, g_ref, o_ref, mask_ref, acc_ref, sa_ref, sb_ref, pa_ref,
                 pb_ref):
    t = SEQ_TILE
    b = pl.program_id(0)
    h = pl.program_id(1)
    i = pl.program_id(2)
    slope = LOG2E * jnp.where(h == 0, ALIBI_SLOPES[0], jnp.where(h == 1, ALIBI_SLOPES[1],
                              jnp.where(h == 2, ALIBI_SLOPES[2], ALIBI_SLOPES[3]))).astype(F32)

    @pl.when(i == 0)
    def _():
        r = lax.broadcasted_iota(jnp.int32, (t, t), 0)
        c = lax.broadcasted_iota(jnp.int32, (t, t), 1)
        mask_ref[...] = jnp.where(r <= c, 0.0, NEG_BIG)

    qf = qT_ref[...].astype(F32)
    row = lax.broadcasted_iota(jnp.int32, qf.shape, 0)
    sl = jnp.full(qf.shape, slope, F32)
    hi = sl.astype(BF16).astype(F32)
    mid = (sl - hi).astype(BF16).astype(F32)
    lo = (sl - hi - mid).astype(BF16).astype(F32)
    piece = jnp.where(row % 3 == 0, hi, jnp.where(row % 3 == 1, mid, lo))
    srows = jnp.where(row < 2 * POS_SPLIT, piece, 0.0).astype(BF16)
    qs = tuple(jnp.concatenate([jnp.where(sel, qf, 0.0).astype(BF16), srows], axis=0)
               for sel in (row < HEAD_DIM, row >= HEAD_DIM))
    acc_ref[...] = jnp.zeros_like(acc_ref)
    pb_ref[...] = jnp.zeros_like(pb_ref)

    first_off = [0]

    def key_tile(tau):
        return jnp.where(tau <= 0, i, first_off[0] + tau - 1)

    def stage_q(tau, s_ref, diagonal=False):
        kt = k_ref[key_tile(tau)]
        tile_max = []
        for mi in range(2):
            s = _dot(kt, qs[mi])
            if diagonal:
                s = s + mask_ref[...]
            s_ref[mi, :, :t] = s
            tile_max.append(jnp.max(s, axis=0, keepdims=True))
        return tuple(tile_max)

    def stage_s(tau, s_ref, p_ref, ms, tile_max):
        cj = slope * (key_tile(tau) * t).astype(F32)
        m_out, alphas = [], []
        for mi in range(2):
            m_new = jnp.maximum(ms[mi], tile_max[mi] + cj)
            alphas.append(jnp.exp2(ms[mi] - m_new))
            p_ref[mi, :, :t] = jnp.exp2(s_ref[mi, :, :t] - (m_new - cj)).astype(BF16)
            m_out.append(m_new)
        return tuple(m_out), tuple(alphas)

    def stage_v(tau, p_ref, alphas):
        vt = vT_ref[key_tile(tau)]
        for mi in range(2):
            acc_ref[mi] = alphas[mi] * acc_ref[mi] + _dot(vt, p_ref[mi, :, :t])

    def body(jj, carry):
        ms, alphas, tmax = carry[:2], carry[2:4], carry[4:]
        tau = 2 * jj
        tmax_b = stage_q(tau + 1, sb_ref)
        ms, alphas_a = stage_s(tau, sa_ref, pa_ref, ms, tmax)
        stage_v(tau - 1, pb_ref, alphas)
        tmax_a = stage_q(tau + 2, sa_ref)
        ms, alphas_b = stage_s(tau + 1, sb_ref, pb_ref, ms, tmax_b)
        stage_v(tau, pa_ref, alphas_a)
        return ms + alphas_b + tmax_a

    def finalize():
        lp = lam_ref[...]
        s1 = jnp.sum(lp[0:1] * lp[1:2], axis=-1, keepdims=True)
        s2 = jnp.sum(lp[2:3] * lp[3:4], axis=-1, keepdims=True)
        lam = jnp.exp(s1) - jnp.exp(s2) + LAM_INIT
        norm = [acc_ref[mi, :V_DIM, :] * (1.0 / acc_ref[mi, V_DIM:V_DIM + 1, :]) for mi in range(2)]
        oT = norm[0] - lam * norm[1]
        o = _rms(oT.T, g_ref[...]) * (1.0 - LAM_INIT)
        o_ref[...] = o.astype(BF16)

    m_init = jnp.full((1, t), NEG_BIG, F32)
    one = jnp.ones((1, t), F32)
    tmax0 = stage_q(0, sa_ref, diagonal=True)

    cj0 = slope * (i * t).astype(F32)
    n_skip = None
    for mi in range(2):
        qsq = jnp.square(qf[mi * HEAD_DIM:(mi + 1) * HEAD_DIM, :])
        qn2 = jnp.max(jnp.sum(qsq, axis=0, keepdims=True), axis=1, keepdims=True)
        m_low = jnp.min(tmax0[mi], axis=1, keepdims=True) + cj0
        qk = jnp.sqrt(qn2 * (kn_ref[(b * N_HEADS + h) * 2 + mi] * NORM_SLACK))
        count = jnp.ceil((m_low - SKIP_MARGIN - slope * (t - 1) - qk) / (slope * t))
        n_skip = count if n_skip is None else jnp.minimum(n_skip, count)
    first_off[0] = jnp.clip(n_skip, 0.0, i.astype(F32)).astype(jnp.int32)[0, 0]
    n_off = i - first_off[0]

    n_main = jnp.right_shift(n_off, 1)
    fin = lax.fori_loop(0, n_main, body, (m_init, m_init, one, one) + tmax0)
    ms, alphas, tmax = fin[:2], fin[2:4], fin[4:]
    tau = 2 * n_main
    odd_tiles = tau == n_off

    @pl.when(odd_tiles)
    def _():
        _, alphas_a = stage_s(tau, sa_ref, pa_ref, ms, tmax)
        stage_v(tau - 1, pb_ref, alphas)
        stage_v(tau, pa_ref, alphas_a)
        finalize()

    @pl.when(jnp.logical_not(odd_tiles))
    def _():
        tmax_b = stage_q(tau + 1, sb_ref)
        ms_a, alphas_a = stage_s(tau, sa_ref, pa_ref, ms, tmax)
        stage_v(tau - 1, pb_ref, alphas)
        _, alphas_b = stage_s(tau + 1, sb_ref, pb_ref, ms_a, tmax_b)
        stage_v(tau, pa_ref, alphas_a)
        stage_v(tau + 1, pb_ref, alphas_b)
        finalize()


def _attn(kn, qT, k4, vT, lam, g, B, S):
    t = SEQ_TILE
    nt = S // t
    grid_spec = pltpu.PrefetchScalarGridSpec(
        num_scalar_prefetch=1,
        grid=(B, N_HEADS, nt),
        in_specs=[pl.BlockSpec((None, None, None, V_DIM, t), lambda b, h, i, kn: (b, h, i, 0, 0)),
                  pl.BlockSpec((None, nt, t, K_COLS), lambda b, h, i, kn: (b, 0, 0, h)),
                  pl.BlockSpec((None, None, nt, V_ROWS, t), lambda b, h, i, kn: (b, h, 0, 0, 0)),
                  pl.BlockSpec((4, HEAD_DIM), lambda b, h, i, kn: (0, 0)),
                  pl.BlockSpec((1, V_DIM), lambda b, h, i, kn: (0, 0))],
        out_specs=pl.BlockSpec((None, t, V_DIM), lambda b, h, i, kn: (b, i, h)),
        scratch_shapes=[pltpu.VMEM((t, t), F32), pltpu.VMEM((2, V_ROWS, t), F32),
                        pltpu.VMEM((2, t, t + LANES), F32), pltpu.VMEM((2, t, t + LANES), F32),
                        pltpu.VMEM((2, t, t + LANES), BF16), pltpu.VMEM((2, t, t + LANES), BF16)],
    )
    return pl.pallas_call(
        _attn_kernel,
        grid_spec=grid_spec,
        out_shape=jax.ShapeDtypeStruct((B, S, ATTN_WIDTH), BF16),
        compiler_params=pltpu.CompilerParams(dimension_semantics=("parallel", "arbitrary", "arbitrary"),
                                             vmem_limit_bytes=VMEM_LIMIT),
        name="diff_attn",
    )(kn, qT, k4, vT, lam, g)


def _lru_tile(x, y, ti, cw_ref, cb_ref, wr_ref, wi_ref, br_ref, bi_ref, lam_ref, xbuf, hc):
    T = SEQ_TILE
    xbuf[8:8 + T] = x
    cw = cw_ref[...]
    xc = cb_ref[...] + cw[3:4] * x
    for j in range(CONV_W - 1):
        xc = xc + cw[j:j + 1] * xbuf[5 + j:5 + j + T]
    xbuf[0:8] = x[T - 8:T]

    xb = xc.astype(BF16)
    sigmoid = lambda z: 0.5 * jnp.tanh(0.5 * z) + 0.5
    r = sigmoid(_dot(xb, wr_ref[...]) + br_ref[...])
    ig = sigmoid(_dot(xb, wi_ref[...]) + bi_ref[...])
    z = -lam_ref[...]
    softplus = jnp.maximum(z, 0.0) + jnp.log1p(jnp.exp(-jnp.abs(z)))
    la = -LRU_C * r * softplus
    a = jnp.exp(la)
    m2 = -jnp.tanh(la) * (a * a + 1.0)
    mult = jnp.where(m2 > 0.0, m2 * lax.rsqrt(m2), 0.0)
    row = lax.broadcasted_iota(jnp.int32, (T, LRU_WIDTH), 0)
    mult = jnp.where((row == 0) & (ti == 0), 1.0, mult)
    u = (xc * ig) * mult

    nb = T // SUBLANES
    a3 = a.reshape(nb, SUBLANES, LRU_WIDTH)
    u3 = u.reshape(nb, SUBLANES, LRU_WIDTH)
    sub = lax.broadcasted_iota(jnp.int32, a3.shape, 1)
    d = 1
    while d < SUBLANES:
        valid = sub >= d
        u3 = jnp.where(valid, a3 * pltpu.roll(u3, d, 1) + u3, u3)
        a3 = jnp.where(valid, a3 * pltpu.roll(a3, d, 1), a3)
        d *= 2
    h = hc[...]
    blocks = []
    for b in range(nb):
        hb = u3[b] + a3[b] * h
        blocks.append(hb)
        h = hb[SUBLANES - 1:SUBLANES]
    hfull = jnp.concatenate(blocks, axis=0)
    hc[...] = h
    gelu = 0.5 * y * (1.0 + jnp.tanh(0.7978845608028654 * (y + 0.044715 * (y * y * y))))
    return (hfull * gelu).astype(BF16)


def _pack_words(v):
    bits = pltpu.bitcast(v.astype(BF16).astype(F32), jnp.uint32)
    half = D_MODEL // 2
    packed = (bits[:, :half] >> 16) | (bits[:, half:] & jnp.uint32(0xFFFF0000))
    return [packed[:, c * LANES:(c + 1) * LANES] for c in range(PACK_SUB)]


def _packed_chunk(c, rows, first_row=0):
    return (pl.ds(first_row * PACK_SUB + c, rows, stride=PACK_SUB), slice(None))


def _pack_rows(v, out_ref, first_row=0):
    for c, words in enumerate(_pack_words(v)):
        out_ref[_packed_chunk(c, v.shape[0], first_row)] = words


def _unpack_chunks(in_ref, rows, first_row=0):
    lo, hi = [], []
    for c in range(PACK_SUB):
        w = in_ref[_packed_chunk(c, rows, first_row)]
        lo.append(pltpu.bitcast(w << 16, F32))
        hi.append(pltpu.bitcast(w & jnp.uint32(0xFFFF0000), F32))
    return lo + hi


def _unpack_rows(in_ref, rows):
    return jnp.concatenate(_unpack_chunks(in_ref, rows), axis=1).astype(BF16)


def _merge_kernel(x_ref, attn_ref, lru_ref, g1_ref, wg_ref, woa_ref, wol_ref, wout_ref, g2_ref, wrt_ref,
                  x1_ref, h2p_ref, route_ref, counts_ref, cnt, tri):
    tm = MERGE_CHAIN

    @pl.when(pl.program_id(0) == 0)
    def _():
        cnt[...] = jnp.zeros_like(cnt)
        r = lax.broadcasted_iota(jnp.int32, (tm, tm), 0)
        c = lax.broadcasted_iota(jnp.int32, (tm, tm), 1)
        tri[...] = jnp.where(r < c, 1.0, 0.0).astype(BF16)

    counts = cnt[...]
    for r0 in range(0, SEQ_TILE, tm):
        counts = _merge_chain(r0, tm, counts, x_ref, attn_ref, lru_ref, g1_ref, wg_ref, woa_ref, wol_ref, wout_ref,
                              g2_ref, wrt_ref, x1_ref, h2p_ref, route_ref, tri)
    cnt[...] = counts
    counts_ref[...] = counts


def _merge_chain(r0, tm, counts, x_ref, attn_ref, lru_ref, g1_ref, wg_ref, woa_ref, wol_ref, wout_ref, g2_ref, wrt_ref,
                 x1_ref, h2p_ref, route_ref, tri):
    rs = slice(r0, r0 + tm)
    x = x_ref[rs, :]
    hb = _rms(x, g1_ref[...]).astype(BF16)
    gates = 0.5 * jnp.tanh(0.5 * _dot(hb, wg_ref[:, PROJ_COLS:])) + 0.5
    merged = gates[:, :D_MODEL] * _dot(attn_ref[rs, :], woa_ref[...]) + gates[:, D_MODEL:] * _dot(lru_ref[rs, :], wol_ref[...])
    x1 = x + _dot(merged.astype(BF16), wout_ref[...])
    x1_ref[rs, :] = x1
    h2 = _rms(x1, g2_ref[...])
    _pack_rows(h2, h2p_ref, first_row=r0)

    h_hi = h2.astype(BF16)
    h_lo = (h2 - h_hi.astype(F32)).astype(BF16)
    wrt = wrt_ref[...]
    nt_dims = (((1,), (1,)), ((), ()))
    hh = lax.dot_general(wrt, h_hi, nt_dims, preferred_element_type=F32)
    lo_pass = lax.dot_general(wrt[:ROUTE_LANES], h_lo, nt_dims, preferred_element_type=F32)
    logits = (hh[:ROUTE_LANES] + hh[ROUTE_LANES:] + lo_pass)[:ROUTE_ROWS]
    row = lax.broadcasted_iota(jnp.int32, logits.shape, 0)
    big = jnp.int32(1 << 20)

    def first_argmax(v):
        m = jnp.max(v, axis=0, keepdims=True)
        return m, jnp.min(jnp.where(v == m, row, big), axis=0, keepdims=True)

    gmask = row < N_GROUPS
    gmax, gidx = first_argmax(jnp.where(gmask, logits, -jnp.inf))
    gsum = jnp.sum(jnp.where(gmask, jnp.exp(logits - gmax), 0.0), axis=0, keepdims=True)
    g_w = 1.0 / gsum
    lo = N_GROUPS + EXPERTS_PER_GROUP * gidx
    el = jnp.where((row >= lo) & (row < lo + EXPERTS_PER_GROUP), logits, -jnp.inf)
    m1, i1 = first_argmax(el)
    m2, i2 = first_argmax(jnp.where(row == i1, -jnp.inf, el))
    rr = jnp.exp(m2 - m1)
    w1 = g_w / (1.0 + rr)
    w2 = g_w * rr / (1.0 + rr)
    oh1 = row == i1
    oh2 = row == i2
    oh = jnp.where(oh1 | oh2, 1.0, 0.0)
    before = _dot(oh.astype(BF16), tri[...]) + counts
    r1 = jnp.sum(jnp.where(oh1, before, 0.0), axis=0, keepdims=True)
    r2 = jnp.sum(jnp.where(oh2, before, 0.0), axis=0, keepdims=True)
    vals = (i1.astype(F32), i2.astype(F32), w1, w2, r1, r2, jnp.zeros_like(w1), jnp.zeros_like(w1))
    for k, v in enumerate(vals):
        route_ref[k:k + 1, rs] = v
    return counts + jnp.sum(oh, axis=1, keepdims=True)


def _merge(x2, attn, lru, g1, wg, woa, wol, wout, g2, wrt):
    N = x2.shape[0]
    tm = SEQ_TILE
    rows = lambda c: pl.BlockSpec((tm, c), lambda i: (i, 0))
    full = lambda r, c: pl.BlockSpec((r, c), lambda i: (0, 0))
    return pl.pallas_call(
        _merge_kernel,
        grid=(N // tm,),
        in_specs=[rows(D_MODEL), rows(ATTN_WIDTH), rows(LRU_WIDTH), full(1, D_MODEL),
                  full(D_MODEL, PROJ_COLS + 2 * D_MODEL),
                  full(ATTN_WIDTH, D_MODEL), full(LRU_WIDTH, D_MODEL), full(D_MODEL, D_MODEL), full(1, D_MODEL),
                  full(2 * ROUTE_LANES, D_MODEL)],
        out_specs=[rows(D_MODEL), pl.BlockSpec((tm * PACK_SUB, LANES), lambda i: (i, 0)),
                   pl.BlockSpec((ROUTE_FIELDS, tm), lambda i: (0, i)), full(ROUTE_ROWS, 1)],
        out_shape=[jax.ShapeDtypeStruct((N, D_MODEL), F32), jax.ShapeDtypeStruct((N * PACK_SUB, LANES), jnp.uint32),
                   jax.ShapeDtypeStruct((ROUTE_FIELDS, N), F32), jax.ShapeDtypeStruct((ROUTE_ROWS, 1), F32)],
        scratch_shapes=[pltpu.VMEM((ROUTE_ROWS, 1), F32), pltpu.VMEM((MERGE_CHAIN, MERGE_CHAIN), BF16)],
        compiler_params=pltpu.CompilerParams(dimension_semantics=("arbitrary",), vmem_limit_bytes=VMEM_LIMIT),
        name="merge_route",
    )(x2, attn, lru, g1, wg, woa, wol, wout, g2, wrt)


def _for_each_assignment(off_ref, fn):
    def group(gi, _):
        toks = [gi * DMA_GROUP + j for j in range(DMA_GROUP)]
        offs = [[off_ref[0, 0, k * ROW_TILE + tk] for k in range(TOP_K)] for tk in toks]
        for tk, o in zip(toks, offs):
            for k in range(TOP_K):
                fn(tk, k, pl.multiple_of(o[k], PACK_SUB))
        return 0
    lax.fori_loop(0, ROW_TILE // DMA_GROUP, group, 0)


def _dispatch_kernel(off_ref, h2p_hbm, xs_hbm, xin, in_sem, out_sem):
    tile_rows = ROW_TILE * PACK_SUB
    i = pl.program_id(0)
    n = pl.num_programs(0)
    slot = lax.rem(i, DISPATCH_SLOTS)

    def fetch(tile, s):
        return pltpu.make_async_copy(h2p_hbm.at[pl.ds(pl.multiple_of(tile * tile_rows, tile_rows), tile_rows)],
                                     xin.at[s], in_sem.at[s])

    def drain(s):
        for _ in range(TOP_K):
            pltpu.make_async_copy(xin.at[s], xs_hbm.at[pl.ds(0, tile_rows)], out_sem.at[s]).wait()

    @pl.when(i == 0)
    def _():
        fetch(0, 0).start()

    @pl.when(i + 1 < n)
    def _():
        fetch(i + 1, lax.rem(i + 1, DISPATCH_SLOTS)).start()

    fetch(i, slot).wait()

    def start(tk, k, off):
        pltpu.make_async_copy(xin.at[slot, pl.ds(pl.multiple_of(tk * PACK_SUB, PACK_SUB), PACK_SUB)],
                              xs_hbm.at[pl.ds(off, PACK_SUB)], out_sem.at[slot]).start(priority=k)

    _for_each_assignment(off_ref, start)

    @pl.when(i > 0)
    def _():
        drain(lax.rem(i + DISPATCH_SLOTS - 1, DISPATCH_SLOTS))

    @pl.when(i == n - 1)
    def _():
        drain(slot)


def _dispatch(off3, h2p):
    return pl.pallas_call(
        _dispatch_kernel,
        grid=(off3.shape[0],),
        in_specs=[pl.BlockSpec((1, 1, TOP_K * ROW_TILE), lambda i: (i, 0, 0), memory_space=pltpu.SMEM),
                  pl.BlockSpec(memory_space=pl.ANY)],
        out_specs=pl.BlockSpec(memory_space=pl.ANY),
        out_shape=jax.ShapeDtypeStruct((TOP_K * h2p.shape[0], LANES), jnp.uint32),
        scratch_shapes=[pltpu.VMEM((DISPATCH_SLOTS, ROW_TILE * PACK_SUB, LANES), jnp.uint32),
                        pltpu.SemaphoreType.DMA((DISPATCH_SLOTS,)), pltpu.SemaphoreType.DMA((DISPATCH_SLOTS,))],
        compiler_params=pltpu.CompilerParams(dimension_semantics=("arbitrary",), has_side_effects=True),
        name="moe_dispatch",
    )(off3, h2p)


def _moe_kernel(vt_ref, ve_ref, vlo_ref, vhi_ref, vnext_ref, vpar_ref, xs_ref, wg_hbm, wu_hbm, wd_hbm, ys_ref,
                wgb, wub, wdb, wgf, wuf, wdf, sem):
    tm = MOE_TILE
    v = pl.program_id(0)
    t = vt_ref[v]
    e = ve_ref[v]
    slot = vpar_ref[v]
    prev = jnp.maximum(v - 1, 0)

    def fetch(expert, s):
        return [pltpu.make_async_copy(w_hbm.at[expert], buf.at[s], sem.at[s])
                for w_hbm, buf in ((wg_hbm, wgf), (wu_hbm, wuf), (wd_hbm, wdf))]

    @pl.when(v == 0)
    def _():
        for copy in fetch(e, slot):
            copy.start()

    @pl.when((v == 0) | (ve_ref[prev] != e))
    def _():
        for copy in fetch(e, slot):
            copy.wait()
        nxt = vnext_ref[v]

        @pl.when(nxt != e)
        def _():
            for copy in fetch(nxt, 1 - slot):
                copy.start()

        wgb[:, :D_EXPERT] = wgf[slot].astype(BF16)
        wub[:, :D_EXPERT] = wuf[slot].astype(BF16)
        wdb[:, :D_MODEL] = wdf[slot].astype(BF16)

    xb = _unpack_rows(xs_ref, tm)
    g = _dot(xb, wgb[:, :D_EXPERT])
    u = _dot(xb, wub[:, :D_EXPERT])
    hmid = (g * jax.nn.sigmoid(g)) * u
    words = _pack_words(_dot(hmid.astype(BF16), wdb[:, :D_MODEL]))
    first = (v == 0) | (vt_ref[prev] != t)

    @pl.when(first)
    def _():
        for c in range(PACK_SUB):
            ys_ref[_packed_chunk(c, tm)] = words[c]

    @pl.when(jnp.logical_not(first))
    def _():
        rows = t * tm + lax.broadcasted_iota(jnp.int32, (tm, LANES), 0)
        mine = (rows >= vlo_ref[v]) & (rows < vhi_ref[v])
        for c in range(PACK_SUB):
            ys_ref[_packed_chunk(c, tm)] = jnp.where(mine, words[c], ys_ref[_packed_chunk(c, tm)])


def _moe(plan, xs, wg, wu, wd):
    tm = MOE_TILE
    n_rows = xs.shape[0] // PACK_SUB
    tile = lambda v, vt, *_: (vt[v], 0)
    hbm = pl.BlockSpec(memory_space=pl.ANY)
    grid_spec = pltpu.PrefetchScalarGridSpec(
        num_scalar_prefetch=len(plan),
        grid=(plan[0].shape[0],),
        in_specs=[pl.BlockSpec((tm * PACK_SUB, LANES), tile), hbm, hbm, hbm],
        out_specs=pl.BlockSpec((tm * PACK_SUB, LANES), tile),
        scratch_shapes=[pltpu.VMEM((D_MODEL, D_EXPERT + LANES), BF16), pltpu.VMEM((D_MODEL, D_EXPERT + LANES), BF16),
                        pltpu.VMEM((D_EXPERT, D_MODEL + LANES), BF16),
                        pltpu.VMEM((2, D_MODEL, D_EXPERT), F32), pltpu.VMEM((2, D_MODEL, D_EXPERT), F32),
                        pltpu.VMEM((2, D_EXPERT, D_MODEL), F32), pltpu.SemaphoreType.DMA((2,))],
    )
    return pl.pallas_call(
        _moe_kernel,
        grid_spec=grid_spec,
        out_shape=jax.ShapeDtypeStruct((n_rows * PACK_SUB, LANES), jnp.uint32),
        compiler_params=pltpu.CompilerParams(dimension_semantics=("arbitrary",), vmem_limit_bytes=VMEM_LIMIT),
        name="moe_experts",
    )(*plan, xs, wg, wu, wd)


def _combine_kernel(off_ref, offn_ref, x1_ref, w_ref, g_ref, ys_hbm, o_ref, ybuf, sem):
    tc = ROW_TILE
    i = pl.program_id(0)
    n = pl.num_programs(0)
    slot = lax.rem(i, 2)

    def gather(o_ref_, s):
        def start(tk, k, off):
            pltpu.make_async_copy(ys_hbm.at[pl.ds(off, PACK_SUB)],
                                  ybuf.at[s, pl.ds(pl.multiple_of((k * tc + tk) * PACK_SUB, PACK_SUB), PACK_SUB)],
                                  sem.at[s]).start(priority=k)
        _for_each_assignment(o_ref_, start)

    def drain(s):
        pltpu.make_async_copy(ys_hbm.at[pl.ds(0, TOP_K * tc * PACK_SUB)], ybuf.at[s], sem.at[s]).wait()

    @pl.when(i == 0)
    def _():
        gather(off_ref, 0)

    @pl.when(i + 1 < n)
    def _():
        gather(offn_ref, 1 - slot)

    drain(slot)

    n_chunks = D_MODEL // LANES
    z = [x1_ref[:, c * LANES:(c + 1) * LANES] for c in range(n_chunks)]
    for k in range(TOP_K):
        wk = w_ref[:, k:k + 1]
        yk = _unpack_chunks(ybuf.at[slot], tc, first_row=k * tc)
        z = [zc + wk * yc for zc, yc in zip(z, yk)]
    ss = sum(jnp.sum(zc * zc, axis=-1, keepdims=True) for zc in z)
    inv = lax.rsqrt(ss * (1.0 / D_MODEL) + NORM_EPS)
    for c in range(n_chunks):
        o_ref[:, c * LANES:(c + 1) * LANES] = z[c] * inv * g_ref[:, c * LANES:(c + 1) * LANES]


def _combine(off3, x1, weight, g, ys):
    N = x1.shape[0]
    tc = ROW_TILE
    nt = N // tc
    idx = lambda f: pl.BlockSpec((1, 1, TOP_K * tc), f, memory_space=pltpu.SMEM)
    return pl.pallas_call(
        _combine_kernel,
        grid=(nt,),
        in_specs=[idx(lambda i: (i, 0, 0)),
                  idx(lambda i: (jnp.minimum(i + 1, nt - 1), 0, 0)),
                  pl.BlockSpec((tc, D_MODEL), lambda i: (i, 0)),
                  pl.BlockSpec((tc, TOP_K), lambda i: (i, 0)),
                  pl.BlockSpec((1, D_MODEL), lambda i: (0, 0)),
                  pl.BlockSpec(memory_space=pl.ANY)],
        out_specs=pl.BlockSpec((tc, D_MODEL), lambda i: (i, 0)),
        out_shape=jax.ShapeDtypeStruct((N, D_MODEL), F32),
        scratch_shapes=[pltpu.VMEM((2, TOP_K * tc * PACK_SUB, LANES), jnp.uint32), pltpu.SemaphoreType.DMA((2,))],
        compiler_params=pltpu.CompilerParams(dimension_semantics=("arbitrary",), vmem_limit_bytes=VMEM_LIMIT),
        name="combine_norm",
    )(off3, off3, x1, weight, g, ys)


def _block_diag(w):
    nb, c, _ = w.shape
    eye = jnp.eye(nb, dtype=w.dtype)
    return (eye[:, None, :, None] * w[:, :, None, :]).reshape(nb * c, nb * c)


def _visit_plan(counts, n_rows):
    tm = MOE_TILE
    n_tiles = n_rows // tm
    n_visits = n_tiles + N_EXPERTS - 1
    cnt = counts.astype(jnp.int32)
    lanes = jnp.arange(ROUTE_ROWS, dtype=jnp.int32)
    upto = (lanes[None, :] <= lanes[:, None]).astype(jnp.int32)
    ends = upto @ cnt
    starts = ends - cnt
    first_tile = starts // tm
    n_vis = jnp.where(cnt > 0, (ends - 1) // tm - first_tile + 1, 0)
    v_end = upto @ n_vis
    v_start = v_end - n_vis
    total = v_end[-1]
    v = jnp.arange(n_visits, dtype=jnp.int32)
    vc = jnp.minimum(v, total - 1)
    own = ((vc[:, None] >= v_start[None, :]) & (vc[:, None] < v_end[None, :])).astype(jnp.int32)
    pick = lambda per_lane: own @ per_lane
    tile = jnp.maximum(pick(first_tile - v_start) + vc, 0)
    lo = jnp.maximum(pick(starts), tile * tm)
    hi = jnp.minimum(pick(ends), (tile + 1) * tm)
    valid = v < total
    to_expert = lambda lane: jnp.maximum(lane - EXPERT_LANE0, 0)
    used = cnt > 0
    later = jnp.where((lanes[None, :] > lanes[:, None]) & used[None, :], lanes[None, :], ROUTE_ROWS)
    nxt = jnp.min(later, axis=1)
    nxt = jnp.where(nxt == ROUTE_ROWS, lanes, nxt)
    parity = (upto @ used.astype(jnp.int32) - 1) % 2
    plan = (tile, to_expert(pick(lanes)), jnp.where(valid, lo, 0), jnp.where(valid, hi, 0),
            to_expert(pick(nxt)), pick(parity))
    return starts, plan


def kernel(x, norm_mix_g, w_in, lambda_qk, subln_g, conv_w, conv_b, w_r, b_r, w_i, b_i, lru_lambda, w_o_attn, w_o_lru, w_out, norm_ffn_g, w_group, w_expert_router, w_gate, w_up, w_down, final_norm_g):
    B, S, D = x.shape
    N = B * S
    nt = S // SEQ_TILE
    depth = norm_mix_g.shape[0]
    assert depth == 1 and D == D_MODEL and S % SEQ_TILE == 0
    assert N % ROW_TILE == 0 and (N * TOP_K) % MOE_TILE == 0
    l = 0
    row = lambda v: v.reshape(1, -1).astype(F32)

    x2 = x.reshape(N, D)
    w_in_l = w_in[l].astype(BF16)
    lru_params = (conv_w[l].astype(F32), row(conv_b[l]), _block_diag(w_r[l]).astype(BF16),
                  _block_diag(w_i[l]).astype(BF16), row(b_r[l]), row(b_i[l]), row(lru_lambda[l]))
    qT, k, vT, lru, kn = _inproj(x2, row(norm_mix_g[l]), w_in_l, lru_params, B, S)

    kn_max = jnp.max(kn[:, 0, :2 * N_HEADS].reshape(B, nt, 2 * N_HEADS), axis=1)
    attn = _attn(kn_max.reshape(-1), qT, k.reshape(B, nt, SEQ_TILE, N_HEADS * K_COLS), vT,
                 lambda_qk[l].reshape(4, HEAD_DIM).astype(F32), row(subln_g[l]), B, S)

    w_route = jnp.concatenate(
        [w_group[l], jnp.transpose(w_expert_router[l], (1, 0, 2)).reshape(D, N_EXPERTS),
         jnp.zeros((D, ROUTE_LANES - N_GROUPS - N_EXPERTS), F32)], axis=1).astype(F32)
    w_route_hi = w_route.astype(BF16)
    w_route = jnp.concatenate([w_route_hi, (w_route - w_route_hi.astype(F32)).astype(BF16)], axis=1).T
    x1, h2p, route, counts = _merge(x2, attn.reshape(N, ATTN_WIDTH), lru.reshape(N, LRU_WIDTH), row(norm_mix_g[l]),
                                    w_in_l, w_o_attn[l].astype(BF16),
                                    w_o_lru[l].astype(BF16), w_out[l].astype(BF16), row(norm_ffn_g[l]), w_route)

    starts, plan = _visit_plan(counts[:, 0], N * TOP_K)
    lane = route[0:TOP_K].astype(jnp.int32)
    weight = jnp.transpose(route[TOP_K:2 * TOP_K])
    rank = route[2 * TOP_K:3 * TOP_K].astype(jnp.int32)
    lane_ids = jnp.arange(ROUTE_ROWS, dtype=jnp.int32)
    start_of = jnp.sum(jnp.where(lane[..., None] == lane_ids, starts, 0), axis=-1)
    off = ((start_of + rank) * PACK_SUB).reshape(TOP_K, N // ROW_TILE, ROW_TILE)
    off3 = jnp.transpose(off, (1, 0, 2)).reshape(N // ROW_TILE, 1, TOP_K * ROW_TILE)

    xs = _dispatch(off3, h2p)
    ys = _moe(plan, xs, w_gate[l], w_up[l], w_down[l])
    out = _combine(off3, x1, weight, row(final_norm_g), ys)
    return out.reshape(B, S, D)
```

```python
import functools
import math

import jax
import jax.numpy as jnp
from jax import lax
from jax.experimental import pallas as pl
from jax.experimental.pallas import tpu as pltpu

F32 = jnp.float32
BF16 = jnp.bfloat16

D_MODEL = 1024
N_HEADS = 4
HEAD_DIM = 64
V_DIM = 2 * HEAD_DIM
ATTN_WIDTH = N_HEADS * V_DIM
V_ROWS = V_DIM + 16
K_COLS = 2 * V_DIM
POS_SPLIT = 3
POS_RADIX = 256
SKIP_MARGIN = 138.0
NORM_SLACK = 1.0201
LRU_WIDTH = D_MODEL // 2
LRU_BLOCKS = 8
CONV_W = 4
LRU_C = 8.0
N_GROUPS = 4
EXPERTS_PER_GROUP = 8
N_EXPERTS = N_GROUPS * EXPERTS_PER_GROUP
TOP_K = 2
D_EXPERT = D_MODEL // 2
NORM_EPS = 1e-6
LAM_INIT = 0.8 - 0.6 * math.exp(-0.3 * 0)

QK_COLS = N_HEADS * 2 * HEAD_DIM
PROJ_COLS = 2 * QK_COLS + ATTN_WIDTH + 2 * LRU_WIDTH
ROUTE_LANES = 128
ROUTE_ROWS = 64
ROUTE_FIELDS = 8
NEG_BIG = -1e30
LOG2E = math.log2(math.e)
ALIBI_SLOPES = tuple(2.0 ** (-8.0 * (h + 1) / N_HEADS) for h in range(N_HEADS))

SEQ_TILE = 512
ROW_TILE = 256
MERGE_CHAIN = 512
MOE_TILE = 256
DMA_GROUP = 8
DISPATCH_SLOTS = 3
LANES = 128
SUBLANES = 8
PACK_SUB = D_MODEL // 2 // LANES
EXPERT_LANE0 = N_GROUPS
V7X_VMEM_BYTES = 64 * 1024 * 1024
VMEM_LIMIT = V7X_VMEM_BYTES * 3 // 4


def _rms(x, g):
    return x * lax.rsqrt(jnp.mean(x * x, axis=-1, keepdims=True) + NORM_EPS) * g


def _dot(a, b):
    return jnp.dot(a, b, preferred_element_type=F32)


def _inproj_kernel(tiles_per_seq, x_ref, g_ref, w_ref, cw_ref, cb_ref, wr_ref, wi_ref, br_ref, bi_ref, lam_ref,
                   qT_ref, k_ref, vT_ref, lru_ref, kn_ref, xbuf, hc):
    ti = lax.rem(pl.program_id(0), tiles_per_seq)

    @pl.when(ti == 0)
    def _():
        xbuf[0:8] = jnp.zeros((8, LRU_WIDTH), F32)
        hc[...] = jnp.zeros_like(hc)

    hb = _rms(x_ref[...], g_ref[...]).astype(BF16)

    def proj(lo, hi):
        return _dot(hb, w_ref[:, lo:hi])

    c0 = 2 * QK_COLS + ATTN_WIDTH
    lru_x = proj(c0, c0 + LRU_WIDTH)
    lru_y = proj(c0 + LRU_WIDTH, c0 + 2 * LRU_WIDTH)

    q = proj(0, QK_COLS) * (HEAD_DIM ** -0.5 * LOG2E)
    for h in range(N_HEADS):
        qT_ref[h] = q[:, h * V_DIM:(h + 1) * V_DIM].T.astype(BF16)
    k = proj(QK_COLS, 2 * QK_COLS).astype(BF16)
    r = lax.broadcasted_iota(jnp.int32, (k.shape[0], K_COLS - V_DIM), 0)
    lane = lax.broadcasted_iota(jnp.int32, r.shape, 1)
    a = r // POS_RADIX * POS_RADIX
    feat = jnp.where(lane < POS_SPLIT, a, jnp.where(lane < 2 * POS_SPLIT, r - a, 0)).astype(F32).astype(BF16)
    for h in range(N_HEADS):
        k_ref[:, h * K_COLS:h * K_COLS + V_DIM] = k[:, h * V_DIM:(h + 1) * V_DIM]
        k_ref[:, h * K_COLS + V_DIM:(h + 1) * K_COLS] = feat
    seg = lax.broadcasted_iota(jnp.int32, (QK_COLS, LANES), 0) // HEAD_DIM
    pick = jnp.where(seg == lax.broadcasted_iota(jnp.int32, (QK_COLS, LANES), 1), 1.0, 0.0).astype(BF16)
    row_n2 = _dot(jnp.square(k.astype(F32)).astype(BF16), pick)
    kn_ref[...] = jnp.max(row_n2, axis=0, keepdims=True)
    v = proj(2 * QK_COLS, 2 * QK_COLS + ATTN_WIDTH)
    for h in range(N_HEADS):
        vT_ref[h, :V_DIM, :] = v[:, h * V_DIM:(h + 1) * V_DIM].T.astype(BF16)
        pad_row = lax.broadcasted_iota(jnp.int32, (V_ROWS - V_DIM, v.shape[0]), 0)
        vT_ref[h, V_DIM:, :] = jnp.where(pad_row == 0, 1.0, 0.0).astype(BF16)

    lru_ref[...] = _lru_tile(lru_x, lru_y, ti, cw_ref, cb_ref, wr_ref, wi_ref, br_ref, bi_ref, lam_ref, xbuf, hc)


def _inproj(x2, g, w, lru_params, B, S):
    N = B * S
    tm = SEQ_TILE
    nt = S // tm
    tile5 = pl.BlockSpec((None, N_HEADS, None, V_DIM, tm), lambda i: (i // nt, 0, i % nt, 0, 0))
    rows = lambda c: pl.BlockSpec((tm, c), lambda i: (i, 0))
    full = lambda a: pl.BlockSpec(a.shape, lambda i: (0, 0))
    return pl.pallas_call(
        functools.partial(_inproj_kernel, nt),
        grid=(N // tm,),
        in_specs=[rows(D_MODEL),
                  pl.BlockSpec((1, D_MODEL), lambda i: (0, 0)),
                  pl.BlockSpec((D_MODEL, PROJ_COLS), lambda i: (0, 0))] + [full(a) for a in lru_params],
        out_specs=[tile5, rows(N_HEADS * K_COLS),
                   pl.BlockSpec((None, N_HEADS, None, V_ROWS, tm), lambda i: (i // nt, 0, i % nt, 0, 0)),
                   rows(LRU_WIDTH),
                   pl.BlockSpec((None, 1, LANES), lambda i: (i, 0, 0))],
        out_shape=[jax.ShapeDtypeStruct((B, N_HEADS, nt, V_DIM, tm), BF16),
                   jax.ShapeDtypeStruct((N, N_HEADS * K_COLS), BF16),
                   jax.ShapeDtypeStruct((B, N_HEADS, nt, V_ROWS, tm), BF16),
                   jax.ShapeDtypeStruct((N, LRU_WIDTH), BF16),
                   jax.ShapeDtypeStruct((N // tm, 1, LANES), F32)],
        scratch_shapes=[pltpu.VMEM((tm + 8, LRU_WIDTH), F32), pltpu.VMEM((1, LRU_WIDTH), F32)],
        compiler_params=pltpu.CompilerParams(dimension_semantics=("arbitrary",),
                                             vmem_limit_bytes=VMEM_LIMIT),
        name="inproj_lru",
    )(x2, g, w, *lru_params)


def _attn_kernel(kn_ref, qT_ref, k_ref, vT_ref, lam_ref, g_ref, o_ref, mask_ref, acc_ref, sa_ref, sb_ref, pa_ref,
                 pb_ref):
    t = SEQ_TILE
    b = pl.program_id(0)
    h = pl.program_id(1)
    i = pl.program_id(2)
    slope = LOG2E * jnp.where(h == 0, ALIBI_SLOPES[0], jnp.where(h == 1, ALIBI_SLOPES[1],
                              jnp.where(h == 2, ALIBI_SLOPES[2], ALIBI_SLOPES[3]))).astype(F32)

    @pl.when(i == 0)
    def _():
        r = lax.broadcasted_iota(jnp.int32, (t, t), 0)
        c = lax.broadcasted_iota(jnp.int32, (t, t), 1)
        mask_ref[...] = jnp.where(r <= c, 0.0, NEG_BIG)

    qf = qT_ref[...].astype(F32)
    row = lax.broadcasted_iota(jnp.int32, qf.shape, 0)
    sl = jnp.full(qf.shape, slope, F32)
    hi = sl.astype(BF16).astype(F32)
    mid = (sl - hi).astype(BF16).astype(F32)
    lo = (sl - hi - mid).astype(BF16).astype(F32)
    piece = jnp.where(row % 3 == 0, hi, jnp.where(row % 3 == 1, mid, lo))
    srows = jnp.where(row < 2 * POS_SPLIT, piece, 0.0).astype(BF16)
    qs = tuple(jnp.concatenate([jnp.where(sel, qf, 0.0).astype(BF16), srows], axis=0)
               for sel in (row < HEAD_DIM, row >= HEAD_DIM))
    acc_ref[...] = jnp.zeros_like(acc_ref)
    pb_ref[...] = jnp.zeros_like(pb_ref)

    first_off = [0]

    def key_tile(tau):
        return jnp.where(tau <= 0, i, first_off[0] + tau - 1)

    def stage_q(tau, s_ref, diagonal=False):
        kt = k_ref[key_tile(tau)]
        tile_max = []
        for mi in range(2):
            s = _dot(kt, qs[mi])
            if diagonal:
                s = s + mask_ref[...]
            s_ref[mi, :, :t] = s
            tile_max.append(jnp.max(s, axis=0, keepdims=True))
        return tuple(tile_max)

    def stage_s(tau, s_ref, p_ref, ms, tile_max):
        cj = slope * (key_tile(tau) * t).astype(F32)
        m_out, alphas = [], []
        for mi in range(2):
            m_new = jnp.maximum(ms[mi], tile_max[mi] + cj)
            alphas.append(jnp.exp2(ms[mi] - m_new))
            p_ref[mi, :, :t] = jnp.exp2(s_ref[mi, :, :t] - (m_new - cj)).astype(BF16)
            m_out.append(m_new)
        return tuple(m_out), tuple(alphas)

    def stage_v(tau, p_ref, alphas):
        vt = vT_ref[key_tile(tau)]
        for mi in range(2):
            acc_ref[mi, :, :t] = alphas[mi] * acc_ref[mi, :, :t] + _dot(vt, p_ref[mi, :, :t])

    def body(jj, carry):
        ms, alphas, tmax = carry[:2], carry[2:4], carry[4:]
        tau = 2 * jj
        tmax_b = stage_q(tau + 1, sb_ref)
        ms, alphas_a = stage_s(tau, sa_ref, pa_ref, ms, tmax)
        stage_v(tau - 1, pb_ref, alphas)
        tmax_a = stage_q(tau + 2, sa_ref)
        ms, alphas_b = stage_s(tau + 1, sb_ref, pb_ref, ms, tmax_b)
        stage_v(tau, pa_ref, alphas_a)
        return ms + alphas_b + tmax_a

    def finalize():
        lp = lam_ref[...]
        s1 = jnp.sum(lp[0:1] * lp[1:2], axis=-1, keepdims=True)
        s2 = jnp.sum(lp[2:3] * lp[3:4], axis=-1, keepdims=True)
        lam = jnp.exp(s1) - jnp.exp(s2) + LAM_INIT
        norm = [acc_ref[mi, :V_DIM, :t] * (1.0 / acc_ref[mi, V_DIM:V_DIM + 1, :t]) for mi in range(2)]
        oT = norm[0] - lam * norm[1]
        o = _rms(oT.T, g_ref[...]) * (1.0 - LAM_INIT)
        o_ref[...] = o.astype(BF16)

    m_init = jnp.full((1, t), NEG_BIG, F32)
    one = jnp.ones((1, t), F32)
    tmax0 = stage_q(0, sa_ref, diagonal=True)

    cj0 = slope * (i * t).astype(F32)
    n_skip = None
    for mi in range(2):
        qsq = jnp.square(qf[mi * HEAD_DIM:(mi + 1) * HEAD_DIM, :])
        qn2 = jnp.max(jnp.sum(qsq, axis=0, keepdims=True), axis=1, keepdims=True)
        m_low = jnp.min(tmax0[mi], axis=1, keepdims=True) + cj0
        qk = jnp.sqrt(qn2 * (kn_ref[(b * N_HEADS + h) * 2 + mi] * NORM_SLACK))
        count = jnp.ceil((m_low - SKIP_MARGIN - slope * (t - 1) - qk) / (slope * t))
        n_skip = count if n_skip is None else jnp.minimum(n_skip, count)
    first_off[0] = jnp.clip(n_skip, 0.0, i.astype(F32)).astype(jnp.int32)[0, 0]
    n_off = i - first_off[0]

    n_main = jnp.right_shift(n_off, 1)
    fin = lax.fori_loop(0, n_main, body, (m_init, m_init, one, one) + tmax0)
    ms, alphas, tmax = fin[:2], fin[2:4], fin[4:]
    tau = 2 * n_main
    odd_tiles = tau == n_off

    @pl.when(odd_tiles)
    def _():
        _, alphas_a = stage_s(tau, sa_ref, pa_ref, ms, tmax)
        stage_v(tau - 1, pb_ref, alphas)
        stage_v(tau, pa_ref, alphas_a)
        finalize()

    @pl.when(jnp.logical_not(odd_tiles))
    def _():
        tmax_b = stage_q(tau + 1, sb_ref)
        ms_a, alphas_a = stage_s(tau, sa_ref, pa_ref, ms, tmax)
        stage_v(tau - 1, pb_ref, alphas)
        _, alphas_b = stage_s(tau + 1, sb_ref, pb_ref, ms_a, tmax_b)
        stage_v(tau, pa_ref, alphas_a)
        stage_v(tau + 1, pb_ref, alphas_b)
        finalize()


def _attn(kn, qT, k4, vT, lam, g, B, S):
    t = SEQ_TILE
    nt = S // t
    grid_spec = pltpu.PrefetchScalarGridSpec(
        num_scalar_prefetch=1,
        grid=(B, N_HEADS, nt),
        in_specs=[pl.BlockSpec((None, None, None, V_DIM, t), lambda b, h, i, kn: (b, h, i, 0, 0)),
                  pl.BlockSpec((None, nt, t, K_COLS), lambda b, h, i, kn: (b, 0, 0, h)),
                  pl.BlockSpec((None, None, nt, V_ROWS, t), lambda b, h, i, kn: (b, h, 0, 0, 0)),
                  pl.BlockSpec((4, HEAD_DIM), lambda b, h, i, kn: (0, 0)),
                  pl.BlockSpec((1, V_DIM), lambda b, h, i, kn: (0, 0))],
        out_specs=pl.BlockSpec((None, t, V_DIM), lambda b, h, i, kn: (b, i, h)),
        scratch_shapes=[pltpu.VMEM((t, t), F32), pltpu.VMEM((2, V_ROWS, t + LANES), F32),
                        pltpu.VMEM((2, t, t + LANES), F32), pltpu.VMEM((2, t, t + LANES), F32),
                        pltpu.VMEM((2, t, t + LANES), BF16), pltpu.VMEM((2, t, t + LANES), BF16)],
    )
    return pl.pallas_call(
        _attn_kernel,
        grid_spec=grid_spec,
        out_shape=jax.ShapeDtypeStruct((B, S, ATTN_WIDTH), BF16),
        compiler_params=pltpu.CompilerParams(dimension_semantics=("parallel", "arbitrary", "arbitrary"),
                                             vmem_limit_bytes=VMEM_LIMIT),
        name="diff_attn",
    )(kn, qT, k4, vT, lam, g)


def _lru_tile(x, y, ti, cw_ref, cb_ref, wr_ref, wi_ref, br_ref, bi_ref, lam_ref, xbuf, hc):
    T = SEQ_TILE
    xbuf[8:8 + T] = x
    cw = cw_ref[...]
    xc = cb_ref[...] + cw[3:4] * x
    for j in range(CONV_W - 1):
        xc = xc + cw[j:j + 1] * xbuf[5 + j:5 + j + T]
    xbuf[0:8] = x[T - 8:T]

    xb = xc.astype(BF16)
    sigmoid = lambda z: 0.5 * jnp.tanh(0.5 * z) + 0.5
    r = sigmoid(_dot(xb, wr_ref[...]) + br_ref[...])
    ig = sigmoid(_dot(xb, wi_ref[...]) + bi_ref[...])
    z = -lam_ref[...]
    softplus = jnp.maximum(z, 0.0) + jnp.log1p(jnp.exp(-jnp.abs(z)))
    la = -LRU_C * r * softplus
    a = jnp.exp(la)
    m2 = -jnp.tanh(la) * (a * a + 1.0)
    mult = jnp.where(m2 > 0.0, m2 * lax.rsqrt(m2), 0.0)
    row = lax.broadcasted_iota(jnp.int32, (T, LRU_WIDTH), 0)
    mult = jnp.where((row == 0) & (ti == 0), 1.0, mult)
    u = (xc * ig) * mult

    nb = T // SUBLANES
    a3 = a.reshape(nb, SUBLANES, LRU_WIDTH)
    u3 = u.reshape(nb, SUBLANES, LRU_WIDTH)
    sub = lax.broadcasted_iota(jnp.int32, a3.shape, 1)
    d = 1
    while d < SUBLANES:
        valid = sub >= d
        u3 = jnp.where(valid, a3 * pltpu.roll(u3, d, 1) + u3, u3)
        a3 = jnp.where(valid, a3 * pltpu.roll(a3, d, 1), a3)
        d *= 2
    h = hc[...]
    blocks = []
    for b in range(nb):
        hb = u3[b] + a3[b] * h
        blocks.append(hb)
        h = hb[SUBLANES - 1:SUBLANES]
    hfull = jnp.concatenate(blocks, axis=0)
    hc[...] = h
    gelu = 0.5 * y * (1.0 + jnp.tanh(0.7978845608028654 * (y + 0.044715 * (y * y * y))))
    return (hfull * gelu).astype(BF16)


def _pack_words(v):
    bits = pltpu.bitcast(v.astype(BF16).astype(F32), jnp.uint32)
    half = D_MODEL // 2
    packed = (bits[:, :half] >> 16) | (bits[:, half:] & jnp.uint32(0xFFFF0000))
    return [packed[:, c * LANES:(c + 1) * LANES] for c in range(PACK_SUB)]


def _packed_chunk(c, rows, first_row=0):
    return (pl.ds(first_row * PACK_SUB + c, rows, stride=PACK_SUB), slice(None))


def _pack_rows(v, out_ref, first_row=0):
    for c, words in enumerate(_pack_words(v)):
        out_ref[_packed_chunk(c, v.shape[0], first_row)] = words


def _unpack_chunks(in_ref, rows, first_row=0):
    lo, hi = [], []
    for c in range(PACK_SUB):
        w = in_ref[_packed_chunk(c, rows, first_row)]
        lo.append(pltpu.bitcast(w << 16, F32))
        hi.append(pltpu.bitcast(w & jnp.uint32(0xFFFF0000), F32))
    return lo + hi


def _unpack_rows(in_ref, rows):
    return jnp.concatenate(_unpack_chunks(in_ref, rows), axis=1).astype(BF16)


def _merge_kernel(x_ref, attn_ref, lru_ref, g1_ref, wg_ref, woa_ref, wol_ref, wout_ref, g2_ref, wrt_ref,
                  x1_ref, h2p_ref, route_ref, counts_ref, cnt, tri):
    tm = MERGE_CHAIN

    @pl.when(pl.program_id(0) == 0)
    def _():
        cnt[...] = jnp.zeros_like(cnt)
        r = lax.broadcasted_iota(jnp.int32, (tm, tm), 0)
        c = lax.broadcasted_iota(jnp.int32, (tm, tm), 1)
        tri[...] = jnp.where(r < c, 1.0, 0.0).astype(BF16)

    counts = cnt[...]
    for r0 in range(0, SEQ_TILE, tm):
        counts = _merge_chain(r0, tm, counts, x_ref, attn_ref, lru_ref, g1_ref, wg_ref, woa_ref, wol_ref, wout_ref,
                              g2_ref, wrt_ref, x1_ref, h2p_ref, route_ref, tri)
    cnt[...] = counts
    counts_ref[...] = counts


def _merge_chain(r0, tm, counts, x_ref, attn_ref, lru_ref, g1_ref, wg_ref, woa_ref, wol_ref, wout_ref, g2_ref, wrt_ref,
                 x1_ref, h2p_ref, route_ref, tri):
    rs = slice(r0, r0 + tm)
    x = x_ref[rs, :]
    hb = _rms(x, g1_ref[...]).astype(BF16)
    gates = 0.5 * jnp.tanh(0.5 * _dot(hb, wg_ref[:, PROJ_COLS:])) + 0.5
    merged = gates[:, :D_MODEL] * _dot(attn_ref[rs, :], woa_ref[...]) + gates[:, D_MODEL:] * _dot(lru_ref[rs, :], wol_ref[...])
    x1 = x + _dot(merged.astype(BF16), wout_ref[...])
    x1_ref[rs, :] = x1
    h2 = _rms(x1, g2_ref[...])
    _pack_rows(h2, h2p_ref, first_row=r0)

    h_hi = h2.astype(BF16)
    h_lo = (h2 - h_hi.astype(F32)).astype(BF16)
    wrt = wrt_ref[...]
    nt_dims = (((1,), (1,)), ((), ()))
    hh = lax.dot_general(wrt, h_hi, nt_dims, preferred_element_type=F32)
    lo_pass = lax.dot_general(wrt[:ROUTE_LANES], h_lo, nt_dims, preferred_element_type=F32)
    logits = (hh[:ROUTE_LANES] + hh[ROUTE_LANES:] + lo_pass)[:ROUTE_ROWS]
    row = lax.broadcasted_iota(jnp.int32, logits.shape, 0)
    big = jnp.int32(1 << 20)

    def first_argmax(v):
        m = jnp.max(v, axis=0, keepdims=True)
        return m, jnp.min(jnp.where(v == m, row, big), axis=0, keepdims=True)

    gmask = row < N_GROUPS
    gmax, gidx = first_argmax(jnp.where(gmask, logits, -jnp.inf))
    gsum = jnp.sum(jnp.where(gmask, jnp.exp(logits - gmax), 0.0), axis=0, keepdims=True)
    g_w = 1.0 / gsum
    lo = N_GROUPS + EXPERTS_PER_GROUP * gidx
    el = jnp.where((row >= lo) & (row < lo + EXPERTS_PER_GROUP), logits, -jnp.inf)
    m1, i1 = first_argmax(el)
    m2, i2 = first_argmax(jnp.where(row == i1, -jnp.inf, el))
    rr = jnp.exp(m2 - m1)
    w1 = g_w / (1.0 + rr)
    w2 = g_w * rr / (1.0 + rr)
    oh1 = row == i1
    oh2 = row == i2
    oh = jnp.where(oh1 | oh2, 1.0, 0.0)
    before = _dot(oh.astype(BF16), tri[...]) + counts
    r1 = jnp.sum(jnp.where(oh1, before, 0.0), axis=0, keepdims=True)
    r2 = jnp.sum(jnp.where(oh2, before, 0.0), axis=0, keepdims=True)
    vals = (i1.astype(F32), i2.astype(F32), w1, w2, r1, r2, jnp.zeros_like(w1), jnp.zeros_like(w1))
    for k, v in enumerate(vals):
        route_ref[k:k + 1, rs] = v
    return counts + jnp.sum(oh, axis=1, keepdims=True)


def _merge(x2, attn, lru, g1, wg, woa, wol, wout, g2, wrt):
    N = x2.shape[0]
    tm = SEQ_TILE
    rows = lambda c: pl.BlockSpec((tm, c), lambda i: (i, 0))
    full = lambda r, c: pl.BlockSpec((r, c), lambda i: (0, 0))
    return pl.pallas_call(
        _merge_kernel,
        grid=(N // tm,),
        in_specs=[rows(D_MODEL), rows(ATTN_WIDTH), rows(LRU_WIDTH), full(1, D_MODEL),
                  full(D_MODEL, PROJ_COLS + 2 * D_MODEL),
                  full(ATTN_WIDTH, D_MODEL), full(LRU_WIDTH, D_MODEL), full(D_MODEL, D_MODEL), full(1, D_MODEL),
                  full(2 * ROUTE_LANES, D_MODEL)],
        out_specs=[rows(D_MODEL), pl.BlockSpec((tm * PACK_SUB, LANES), lambda i: (i, 0)),
                   pl.BlockSpec((ROUTE_FIELDS, tm), lambda i: (0, i)), full(ROUTE_ROWS, 1)],
        out_shape=[jax.ShapeDtypeStruct((N, D_MODEL), F32), jax.ShapeDtypeStruct((N * PACK_SUB, LANES), jnp.uint32),
                   jax.ShapeDtypeStruct((ROUTE_FIELDS, N), F32), jax.ShapeDtypeStruct((ROUTE_ROWS, 1), F32)],
        scratch_shapes=[pltpu.VMEM((ROUTE_ROWS, 1), F32), pltpu.VMEM((MERGE_CHAIN, MERGE_CHAIN), BF16)],
        compiler_params=pltpu.CompilerParams(dimension_semantics=("arbitrary",), vmem_limit_bytes=VMEM_LIMIT),
        name="merge_route",
    )(x2, attn, lru, g1, wg, woa, wol, wout, g2, wrt)


def _for_each_assignment(off_ref, fn):
    def group(gi, _):
        toks = [gi * DMA_GROUP + j for j in range(DMA_GROUP)]
        offs = [[off_ref[0, 0, k * ROW_TILE + tk] for k in range(TOP_K)] for tk in toks]
        for tk, o in zip(toks, offs):
            for k in range(TOP_K):
                fn(tk, k, pl.multiple_of(o[k], PACK_SUB))
        return 0
    lax.fori_loop(0, ROW_TILE // DMA_GROUP, group, 0)


def _dispatch_kernel(off_ref, h2p_hbm, xs_hbm, xin, in_sem, out_sem):
    tile_rows = ROW_TILE * PACK_SUB
    i = pl.program_id(0)
    n = pl.num_programs(0)
    slot = lax.rem(i, DISPATCH_SLOTS)

    def fetch(tile, s):
        return pltpu.make_async_copy(h2p_hbm.at[pl.ds(pl.multiple_of(tile * tile_rows, tile_rows), tile_rows)],
                                     xin.at[s], in_sem.at[s])

    def drain(s):
        for _ in range(TOP_K):
            pltpu.make_async_copy(xin.at[s], xs_hbm.at[pl.ds(0, tile_rows)], out_sem.at[s]).wait()

    @pl.when(i == 0)
    def _():
        fetch(0, 0).start()

    @pl.when(i + 1 < n)
    def _():
        fetch(i + 1, lax.rem(i + 1, DISPATCH_SLOTS)).start()

    fetch(i, slot).wait()

    def start(tk, k, off):
        pltpu.make_async_copy(xin.at[slot, pl.ds(pl.multiple_of(tk * PACK_SUB, PACK_SUB), PACK_SUB)],
                              xs_hbm.at[pl.ds(off, PACK_SUB)], out_sem.at[slot]).start(priority=k)

    _for_each_assignment(off_ref, start)

    @pl.when(i > 0)
    def _():
        drain(lax.rem(i + DISPATCH_SLOTS - 1, DISPATCH_SLOTS))

    @pl.when(i == n - 1)
    def _():
        drain(slot)


def _dispatch(off3, h2p):
    return pl.pallas_call(
        _dispatch_kernel,
        grid=(off3.shape[0],),
        in_specs=[pl.BlockSpec((1, 1, TOP_K * ROW_TILE), lambda i: (i, 0, 0), memory_space=pltpu.SMEM),
                  pl.BlockSpec(memory_space=pl.ANY)],
        out_specs=pl.BlockSpec(memory_space=pl.ANY),
        out_shape=jax.ShapeDtypeStruct((TOP_K * h2p.shape[0], LANES), jnp.uint32),
        scratch_shapes=[pltpu.VMEM((DISPATCH_SLOTS, ROW_TILE * PACK_SUB, LANES), jnp.uint32),
                        pltpu.SemaphoreType.DMA((DISPATCH_SLOTS,)), pltpu.SemaphoreType.DMA((DISPATCH_SLOTS,))],
        compiler_params=pltpu.CompilerParams(dimension_semantics=("arbitrary",), has_side_effects=True),
        name="moe_dispatch",
    )(off3, h2p)


def _moe_kernel(vt_ref, ve_ref, vlo_ref, vhi_ref, vnext_ref, vpar_ref, xs_ref, wg_hbm, wu_hbm, wd_hbm, ys_ref,
                wgb, wub, wdb, wgf, wuf, wdf, sem):
    tm = MOE_TILE
    v = pl.program_id(0)
    t = vt_ref[v]
    e = ve_ref[v]
    slot = vpar_ref[v]
    prev = jnp.maximum(v - 1, 0)

    def fetch(expert, s):
        return [pltpu.make_async_copy(w_hbm.at[expert], buf.at[s], sem.at[s])
                for w_hbm, buf in ((wg_hbm, wgf), (wu_hbm, wuf), (wd_hbm, wdf))]

    @pl.when(v == 0)
    def _():
        for copy in fetch(e, slot):
            copy.start()

    @pl.when((v == 0) | (ve_ref[prev] != e))
    def _():
        for copy in fetch(e, slot):
            copy.wait()
        nxt = vnext_ref[v]

        @pl.when(nxt != e)
        def _():
            for copy in fetch(nxt, 1 - slot):
                copy.start()

        wgb[...] = wgf[slot].astype(BF16)
        wub[...] = wuf[slot].astype(BF16)
        wdb[...] = wdf[slot].astype(BF16)

    xb = _unpack_rows(xs_ref, tm)
    g = _dot(xb, wgb[...])
    u = _dot(xb, wub[...])
    hmid = (g * jax.nn.sigmoid(g)) * u
    words = _pack_words(_dot(hmid.astype(BF16), wdb[...]))
    first = (v == 0) | (vt_ref[prev] != t)

    @pl.when(first)
    def _():
        for c in range(PACK_SUB):
            ys_ref[_packed_chunk(c, tm)] = words[c]

    @pl.when(jnp.logical_not(first))
    def _():
        rows = t * tm + lax.broadcasted_iota(jnp.int32, (tm, LANES), 0)
        mine = (rows >= vlo_ref[v]) & (rows < vhi_ref[v])
        for c in range(PACK_SUB):
            ys_ref[_packed_chunk(c, tm)] = jnp.where(mine, words[c], ys_ref[_packed_chunk(c, tm)])


def _moe(plan, xs, wg, wu, wd):
    tm = MOE_TILE
    n_rows = xs.shape[0] // PACK_SUB
    tile = lambda v, vt, *_: (vt[v], 0)
    hbm = pl.BlockSpec(memory_space=pl.ANY)
    grid_spec = pltpu.PrefetchScalarGridSpec(
        num_scalar_prefetch=len(plan),
        grid=(plan[0].shape[0],),
        in_specs=[pl.BlockSpec((tm * PACK_SUB, LANES), tile), hbm, hbm, hbm],
        out_specs=pl.BlockSpec((tm * PACK_SUB, LANES), tile),
        scratch_shapes=[pltpu.VMEM((D_MODEL, D_EXPERT), BF16), pltpu.VMEM((D_MODEL, D_EXPERT), BF16),
                        pltpu.VMEM((D_EXPERT, D_MODEL), BF16),
                        pltpu.VMEM((2, D_MODEL, D_EXPERT), F32), pltpu.VMEM((2, D_MODEL, D_EXPERT), F32),
                        pltpu.VMEM((2, D_EXPERT, D_MODEL), F32), pltpu.SemaphoreType.DMA((2,))],
    )
    return pl.pallas_call(
        _moe_kernel,
        grid_spec=grid_spec,
        out_shape=jax.ShapeDtypeStruct((n_rows * PACK_SUB, LANES), jnp.uint32),
        compiler_params=pltpu.CompilerParams(dimension_semantics=("arbitrary",), vmem_limit_bytes=VMEM_LIMIT),
        name="moe_experts",
    )(*plan, xs, wg, wu, wd)


def _combine_kernel(off_ref, offn_ref, x1_ref, w_ref, g_ref, ys_hbm, o_ref, ybuf, sem):
    tc = ROW_TILE
    i = pl.program_id(0)
    n = pl.num_programs(0)
    slot = lax.rem(i, 2)

    def gather(o_ref_, s):
        def start(tk, k, off):
            pltpu.make_async_copy(ys_hbm.at[pl.ds(off, PACK_SUB)],
                                  ybuf.at[s, pl.ds(pl.multiple_of((k * tc + tk) * PACK_SUB, PACK_SUB), PACK_SUB)],
                                  sem.at[s]).start(priority=k)
        _for_each_assignment(o_ref_, start)

    def drain(s):
        pltpu.make_async_copy(ys_hbm.at[pl.ds(0, TOP_K * tc * PACK_SUB)], ybuf.at[s], sem.at[s]).wait()

    @pl.when(i == 0)
    def _():
        gather(off_ref, 0)

    @pl.when(i + 1 < n)
    def _():
        gather(offn_ref, 1 - slot)

    drain(slot)

    n_chunks = D_MODEL // LANES
    z = [x1_ref[:, c * LANES:(c + 1) * LANES] for c in range(n_chunks)]
    for k in range(TOP_K):
        wk = w_ref[:, k:k + 1]
        yk = _unpack_chunks(ybuf.at[slot], tc, first_row=k * tc)
        z = [zc + wk * yc for zc, yc in zip(z, yk)]
    ss = sum(jnp.sum(zc * zc, axis=-1, keepdims=True) for zc in z)
    inv = lax.rsqrt(ss * (1.0 / D_MODEL) + NORM_EPS)
    for c in range(n_chunks):
        o_ref[:, c * LANES:(c + 1) * LANES] = z[c] * inv * g_ref[:, c * LANES:(c + 1) * LANES]


def _combine(off3, x1, weight, g, ys):
    N = x1.shape[0]
    tc = ROW_TILE
    nt = N // tc
    idx = lambda f: pl.BlockSpec((1, 1, TOP_K * tc), f, memory_space=pltpu.SMEM)
    return pl.pallas_call(
        _combine_kernel,
        grid=(nt,),
        in_specs=[idx(lambda i: (i, 0, 0)),
                  idx(lambda i: (jnp.minimum(i + 1, nt - 1), 0, 0)),
                  pl.BlockSpec((tc, D_MODEL), lambda i: (i, 0)),
                  pl.BlockSpec((tc, TOP_K), lambda i: (i, 0)),
                  pl.BlockSpec((1, D_MODEL), lambda i: (0, 0)),
                  pl.BlockSpec(memory_space=pl.ANY)],
        out_specs=pl.BlockSpec((tc, D_MODEL), lambda i: (i, 0)),
        out_shape=jax.ShapeDtypeStruct((N, D_MODEL), F32),
        scratch_shapes=[pltpu.VMEM((2, TOP_K * tc * PACK_SUB, LANES), jnp.uint32), pltpu.SemaphoreType.DMA((2,))],
        compiler_params=pltpu.CompilerParams(dimension_semantics=("arbitrary",), vmem_limit_bytes=VMEM_LIMIT),
        name="combine_norm",
    )(off3, off3, x1, weight, g, ys)


def _block_diag(w):
    nb, c, _ = w.shape
    eye = jnp.eye(nb, dtype=w.dtype)
    return (eye[:, None, :, None] * w[:, :, None, :]).reshape(nb * c, nb * c)


def _visit_plan(counts, n_rows):
    tm = MOE_TILE
    n_tiles = n_rows // tm
    n_visits = n_tiles + N_EXPERTS - 1
    cnt = counts.astype(jnp.int32)
    lanes = jnp.arange(ROUTE_ROWS, dtype=jnp.int32)
    upto = (lanes[None, :] <= lanes[:, None]).astype(jnp.int32)
    ends = upto @ cnt
    starts = ends - cnt
    first_tile = starts // tm
    n_vis = jnp.where(cnt > 0, (ends - 1) // tm - first_tile + 1, 0)
    v_end = upto @ n_vis
    v_start = v_end - n_vis
    total = v_end[-1]
    v = jnp.arange(n_visits, dtype=jnp.int32)
    vc = jnp.minimum(v, total - 1)
    own = ((vc[:, None] >= v_start[None, :]) & (vc[:, None] < v_end[None, :])).astype(jnp.int32)
    pick = lambda per_lane: own @ per_lane
    tile = jnp.maximum(pick(first_tile - v_start) + vc, 0)
    lo = jnp.maximum(pick(starts), tile * tm)
    hi = jnp.minimum(pick(ends), (tile + 1) * tm)
    valid = v < total
    to_expert = lambda lane: jnp.maximum(lane - EXPERT_LANE0, 0)
    used = cnt > 0
    later = jnp.where((lanes[None, :] > lanes[:, None]) & used[None, :], lanes[None, :], ROUTE_ROWS)
    nxt = jnp.min(later, axis=1)
    nxt = jnp.where(nxt == ROUTE_ROWS, lanes, nxt)
    parity = (upto @ used.astype(jnp.int32) - 1) % 2
    plan = (tile, to_expert(pick(lanes)), jnp.where(valid, lo, 0), jnp.where(valid, hi, 0),
            to_expert(pick(nxt)), pick(parity))
    return starts, plan


def kernel(x, norm_mix_g, w_in, lambda_qk, subln_g, conv_w, conv_b, w_r, b_r, w_i, b_i, lru_lambda, w_o_attn, w_o_lru, w_out, norm_ffn_g, w_group, w_expert_router, w_gate, w_up, w_down, final_norm_g):
    B, S, D = x.shape
    N = B * S
    nt = S // SEQ_TILE
    depth = norm_mix_g.shape[0]
    assert depth == 1 and D == D_MODEL and S % SEQ_TILE == 0
    assert N % ROW_TILE == 0 and (N * TOP_K) % MOE_TILE == 0
    l = 0
    row = lambda v: v.reshape(1, -1).astype(F32)

    x2 = x.reshape(N, D)
    w_in_l = w_in[l].astype(BF16)
    lru_params = (conv_w[l].astype(F32), row(conv_b[l]), _block_diag(w_r[l]).astype(BF16),
                  _block_diag(w_i[l]).astype(BF16), row(b_r[l]), row(b_i[l]), row(lru_lambda[l]))
    qT, k, vT, lru, kn = _inproj(x2, row(norm_mix_g[l]), w_in_l, lru_params, B, S)

    kn_max = jnp.max(kn[:, 0, :2 * N_HEADS].reshape(B, nt, 2 * N_HEADS), axis=1)
    attn = _attn(kn_max.reshape(-1), qT, k.reshape(B, nt, SEQ_TILE, N_HEADS * K_COLS), vT,
                 lambda_qk[l].reshape(4, HEAD_DIM).astype(F32), row(subln_g[l]), B, S)

    w_route = jnp.concatenate(
        [w_group[l], jnp.transpose(w_expert_router[l], (1, 0, 2)).reshape(D, N_EXPERTS),
         jnp.zeros((D, ROUTE_LANES - N_GROUPS - N_EXPERTS), F32)], axis=1).astype(F32)
    w_route_hi = w_route.astype(BF16)
    w_route = jnp.concatenate([w_route_hi, (w_route - w_route_hi.astype(F32)).astype(BF16)], axis=1).T
    x1, h2p, route, counts = _merge(x2, attn.reshape(N, ATTN_WIDTH), lru.reshape(N, LRU_WIDTH), row(norm_mix_g[l]),
                                    w_in_l, w_o_attn[l].astype(BF16),
                                    w_o_lru[l].astype(BF16), w_out[l].astype(BF16), row(norm_ffn_g[l]), w_route)

    starts, plan = _visit_plan(counts[:, 0], N * TOP_K)
    lane = route[0:TOP_K].astype(jnp.int32)
    weight = jnp.transpose(route[TOP_K:2 * TOP_K])
    rank = route[2 * TOP_K:3 * TOP_K].astype(jnp.int32)
    lane_ids = jnp.arange(ROUTE_ROWS, dtype=jnp.int32)
    start_of = jnp.sum(jnp.where(lane[..., None] == lane_ids, starts, 0), axis=-1)
    off = ((start_of + rank) * PACK_SUB).reshape(TOP_K, N // ROW_TILE, ROW_TILE)
    off3 = jnp.transpose(off, (1, 0, 2)).reshape(N // ROW_TILE, 1, TOP_K * ROW_TILE)

    xs = _dispatch(off3, h2p)
    ys = _moe(plan, xs, w_gate[l], w_up[l], w_down[l])
    out = _combine(off3, x1, weight, row(final_norm_g), ys)
    return out.reshape(B, S, D)
```

```python
import functools
import math

import jax
import jax.numpy as jnp
from jax import lax
from jax.experimental import pallas as pl
from jax.experimental.pallas import tpu as pltpu

F32 = jnp.float32
BF16 = jnp.bfloat16

D_MODEL = 1024
N_HEADS = 4
HEAD_DIM = 64
V_DIM = 2 * HEAD_DIM
ATTN_WIDTH = N_HEADS * V_DIM
V_ROWS = V_DIM + 16
K_COLS = 2 * V_DIM
POS_SPLIT = 3
POS_RADIX = 256
SKIP_MARGIN = 138.0
NORM_SLACK = 1.0201
LRU_WIDTH = D_MODEL // 2
LRU_BLOCKS = 8
CONV_W = 4
LRU_C = 8.0
N_GROUPS = 4
EXPERTS_PER_GROUP = 8
N_EXPERTS = N_GROUPS * EXPERTS_PER_GROUP
TOP_K = 2
D_EXPERT = D_MODEL // 2
NORM_EPS = 1e-6
LAM_INIT = 0.8 - 0.6 * math.exp(-0.3 * 0)

QK_COLS = N_HEADS * 2 * HEAD_DIM
PROJ_COLS = 2 * QK_COLS + ATTN_WIDTH + 2 * LRU_WIDTH
ROUTE_LANES = 128
ROUTE_ROWS = 64
ROUTE_FIELDS = 8
NEG_BIG = -1e30
LOG2E = math.log2(math.e)
ALIBI_SLOPES = tuple(2.0 ** (-8.0 * (h + 1) / N_HEADS) for h in range(N_HEADS))

SEQ_TILE = 512
ROW_TILE = 256
MERGE_CHAIN = 512
MOE_TILE = 256
DMA_GROUP = 8
DISPATCH_SLOTS = 3
LANES = 128
SUBLANES = 8
PACK_SUB = D_MODEL // 2 // LANES
EXPERT_LANE0 = N_GROUPS
V7X_VMEM_BYTES = 64 * 1024 * 1024
VMEM_LIMIT = V7X_VMEM_BYTES * 3 // 4


def _rms(x, g):
    return x * lax.rsqrt(jnp.mean(x * x, axis=-1, keepdims=True) + NORM_EPS) * g


def _dot(a, b):
    return jnp.dot(a, b, preferred_element_type=F32)


def _inproj_kernel(tiles_per_seq, x_ref, g_ref, w_ref, cw_ref, cb_ref, wr_ref, wi_ref, br_ref, bi_ref, lam_ref,
                   qT_ref, k_ref, vT_ref, lru_ref, kn_ref, xbuf, hc):
    ti = lax.rem(pl.program_id(0), tiles_per_seq)

    @pl.when(ti == 0)
    def _():
        xbuf[0:8] = jnp.zeros((8, LRU_WIDTH), F32)
        hc[...] = jnp.zeros_like(hc)

    hb = _rms(x_ref[...], g_ref[...]).astype(BF16)

    def proj(lo, hi):
        return _dot(hb, w_ref[:, lo:hi])

    c0 = 2 * QK_COLS + ATTN_WIDTH
    lru_x = proj(c0, c0 + LRU_WIDTH)
    lru_y = proj(c0 + LRU_WIDTH, c0 + 2 * LRU_WIDTH)

    q = proj(0, QK_COLS) * (HEAD_DIM ** -0.5 * LOG2E)
    for h in range(N_HEADS):
        qT_ref[h] = q[:, h * V_DIM:(h + 1) * V_DIM].T.astype(BF16)
    k = proj(QK_COLS, 2 * QK_COLS).astype(BF16)
    r = lax.broadcasted_iota(jnp.int32, (k.shape[0], K_COLS - V_DIM), 0)
    lane = lax.broadcasted_iota(jnp.int32, r.shape, 1)
    a = r // POS_RADIX * POS_RADIX
    feat = jnp.where(lane < POS_SPLIT, a, jnp.where(lane < 2 * POS_SPLIT, r - a, 0)).astype(F32).astype(BF16)
    for h in range(N_HEADS):
        k_ref[:, h * K_COLS:h * K_COLS + V_DIM] = k[:, h * V_DIM:(h + 1) * V_DIM]
        k_ref[:, h * K_COLS + V_DIM:(h + 1) * K_COLS] = feat
    seg = lax.broadcasted_iota(jnp.int32, (QK_COLS, LANES), 0) // HEAD_DIM
    pick = jnp.where(seg == lax.broadcasted_iota(jnp.int32, (QK_COLS, LANES), 1), 1.0, 0.0).astype(BF16)
    row_n2 = _dot(jnp.square(k.astype(F32)).astype(BF16), pick)
    kn_ref[...] = jnp.max(row_n2, axis=0, keepdims=True)
    v = proj(2 * QK_COLS, 2 * QK_COLS + ATTN_WIDTH)
    for h in range(N_HEADS):
        vT_ref[h, :V_DIM, :] = v[:, h * V_DIM:(h + 1) * V_DIM].T.astype(BF16)
        pad_row = lax.broadcasted_iota(jnp.int32, (V_ROWS - V_DIM, v.shape[0]), 0)
        vT_ref[h, V_DIM:, :] = jnp.where(pad_row == 0, 1.0, 0.0).astype(BF16)

    lru_ref[...] = _lru_tile(lru_x, lru_y, ti, cw_ref, cb_ref, wr_ref, wi_ref, br_ref, bi_ref, lam_ref, xbuf, hc)


def _inproj(x2, g, w, lru_params, B, S):
    N = B * S
    tm = SEQ_TILE
    nt = S // tm
    tile5 = pl.BlockSpec((None, N_HEADS, None, V_DIM, tm), lambda i: (i // nt, 0, i % nt, 0, 0))
    rows = lambda c: pl.BlockSpec((tm, c), lambda i: (i, 0))
    full = lambda a: pl.BlockSpec(a.shape, lambda i: (0, 0))
    return pl.pallas_call(
        functools.partial(_inproj_kernel, nt),
        grid=(N // tm,),
        in_specs=[rows(D_MODEL),
                  pl.BlockSpec((1, D_MODEL), lambda i: (0, 0)),
                  pl.BlockSpec((D_MODEL, PROJ_COLS), lambda i: (0, 0))] + [full(a) for a in lru_params],
        out_specs=[tile5, rows(N_HEADS * K_COLS),
                   pl.BlockSpec((None, N_HEADS, None, V_ROWS, tm), lambda i: (i // nt, 0, i % nt, 0, 0)),
                   rows(LRU_WIDTH),
                   pl.BlockSpec((None, 1, LANES), lambda i: (i, 0, 0))],
        out_shape=[jax.ShapeDtypeStruct((B, N_HEADS, nt, V_DIM, tm), BF16),
                   jax.ShapeDtypeStruct((N, N_HEADS * K_COLS), BF16),
                   jax.ShapeDtypeStruct((B, N_HEADS, nt, V_ROWS, tm), BF16),
                   jax.ShapeDtypeStruct((N, LRU_WIDTH), BF16),
                   jax.ShapeDtypeStruct((N // tm, 1, LANES), F32)],
        scratch_shapes=[pltpu.VMEM((tm + 8, LRU_WIDTH), F32), pltpu.VMEM((1, LRU_WIDTH), F32)],
        compiler_params=pltpu.CompilerParams(dimension_semantics=("arbitrary",),
                                             vmem_limit_bytes=VMEM_LIMIT),
        name="inproj_lru",
    )(x2, g, w, *lru_params)


def _attn_kernel(kn_ref, qT_ref, k_ref, vT_ref, lam_ref, g_ref, o_ref, mask_ref, acc_ref, sa_ref, sb_ref, pa_ref,
                 pb_ref):
    t = SEQ_TILE
    b = pl.program_id(0)
    h = pl.program_id(1)
    i = pl.program_id(2)
    slope = LOG2E * jnp.where(h == 0, ALIBI_SLOPES[0], jnp.where(h == 1, ALIBI_SLOPES[1],
                              jnp.where(h == 2, ALIBI_SLOPES[2], ALIBI_SLOPES[3]))).astype(F32)

    @pl.when(i == 0)
    def _():
        r = lax.broadcasted_iota(jnp.int32, (t, t), 0)
        c = lax.broadcasted_iota(jnp.int32, (t, t), 1)
        mask_ref[...] = jnp.where(r <= c, 0.0, NEG_BIG)

    qf = qT_ref[...].astype(F32)
    row = lax.broadcasted_iota(jnp.int32, qf.shape, 0)
    sl = jnp.full(qf.shape, slope, F32)
    hi = sl.astype(BF16).astype(F32)
    mid = (sl - hi).astype(BF16).astype(F32)
    lo = (sl - hi - mid).astype(BF16).astype(F32)
    piece = jnp.where(row % 3 == 0, hi, jnp.where(row % 3 == 1, mid, lo))
    srows = jnp.where(row < 2 * POS_SPLIT, piece, 0.0).astype(BF16)
    qs = tuple(jnp.concatenate([jnp.where(sel, qf, 0.0).astype(BF16), srows], axis=0)
               for sel in (row < HEAD_DIM, row >= HEAD_DIM))
    acc_ref[...] = jnp.zeros_like(acc_ref)
    pb_ref[...] = jnp.zeros_like(pb_ref)

    first_off = [0]

    def key_tile(tau):
        return jnp.where(tau <= 0, i, first_off[0] + tau - 1)

    def stage_q(tau, s_ref, diagonal=False):
        kt = k_ref[key_tile(tau)]
        tile_max = []
        for mi in range(2):
            s = _dot(kt, qs[mi])
            if diagonal:
                s = s + mask_ref[...]
            s_ref[mi, :, :t] = s
            tile_max.append(jnp.max(s, axis=0, keepdims=True))
        return tuple(tile_max)

    def stage_s(tau, s_ref, p_ref, ms, tile_max):
        cj = slope * (key_tile(tau) * t).astype(F32)
        m_out, alphas = [], []
        for mi in range(2):
            m_new = jnp.maximum(ms[mi], tile_max[mi] + cj)
            alphas.append(jnp.exp2(ms[mi] - m_new))
            p_ref[mi, :, :t] = jnp.exp2(s_ref[mi, :, :t] - (m_new - cj)).astype(BF16)
            m_out.append(m_new)
        return tuple(m_out), tuple(alphas)

    def stage_v(tau, p_ref, alphas):
        vt = vT_ref[key_tile(tau)]
        for mi in range(2):
            acc_ref[mi] = alphas[mi] * acc_ref[mi] + _dot(vt, p_ref[mi, :, :t])

    def body(jj, carry):
        ms, alphas, tmax = carry[:2], carry[2:4], carry[4:]
        tau = 2 * jj
        tmax_b = stage_q(tau + 1, sb_ref)
        ms, alphas_a = stage_s(tau, sa_ref, pa_ref, ms, tmax)
        stage_v(tau - 1, pb_ref, alphas)
        tmax_a = stage_q(tau + 2, sa_ref)
        ms, alphas_b = stage_s(tau + 1, sb_ref, pb_ref, ms, tmax_b)
        stage_v(tau, pa_ref, alphas_a)
        return ms + alphas_b + tmax_a

    def finalize():
        lp = lam_ref[...]
        s1 = jnp.sum(lp[0:1] * lp[1:2], axis=-1, keepdims=True)
        s2 = jnp.sum(lp[2:3] * lp[3:4], axis=-1, keepdims=True)
        lam = jnp.exp(s1) - jnp.exp(s2) + LAM_INIT
        norm = [acc_ref[mi, :V_DIM, :] * (1.0 / acc_ref[mi, V_DIM:V_DIM + 1, :]) for mi in range(2)]
        oT = norm[0] - lam * norm[1]
        o = _rms(oT.T, g_ref[...]) * (1.0 - LAM_INIT)
        o_ref[...] = o.astype(BF16)

    m_init = jnp.full((1, t), NEG_BIG, F32)
    one = jnp.ones((1, t), F32)
    tmax0 = stage_q(0, sa_ref, diagonal=True)

    cj0 = slope * (i * t).astype(F32)
    n_skip = None
    for mi in range(2):
        qsq = jnp.square(qf[mi * HEAD_DIM:(mi + 1) * HEAD_DIM, :])
        qn2 = jnp.max(jnp.sum(qsq, axis=0, keepdims=True), axis=1, keepdims=True)
        m_low = jnp.min(tmax0[mi], axis=1, keepdims=True) + cj0
        qk = jnp.sqrt(qn2 * (kn_ref[(b * N_HEADS + h) * 2 + mi] * NORM_SLACK))
        count = jnp.ceil((m_low - SKIP_MARGIN - slope * (t - 1) - qk) / (slope * t))
        n_skip = count if n_skip is None else jnp.minimum(n_skip, count)
    first_off[0] = jnp.clip(n_skip, 0.0, i.astype(F32)).astype(jnp.int32)[0, 0]
    n_off = i - first_off[0]

    n_main = jnp.right_shift(n_off, 1)
    fin = lax.fori_loop(0, n_main, body, (m_init, m_init, one, one) + tmax0)
    ms, alphas, tmax = fin[:2], fin[2:4], fin[4:]
    tau = 2 * n_main
    odd_tiles = tau == n_off

    @pl.when(odd_tiles)
    def _():
        _, alphas_a = stage_s(tau, sa_ref, pa_ref, ms, tmax)
        stage_v(tau - 1, pb_ref, alphas)
        stage_v(tau, pa_ref, alphas_a)
        finalize()

    @pl.when(jnp.logical_not(odd_tiles))
    def _():
        tmax_b = stage_q(tau + 1, sb_ref)
        ms_a, alphas_a = stage_s(tau, sa_ref, pa_ref, ms, tmax)
        stage_v(tau - 1, pb_ref, alphas)
        _, alphas_b = stage_s(tau + 1, sb_ref, pb_ref, ms_a, tmax_b)
        stage_v(tau, pa_ref, alphas_a)
        stage_v(tau + 1, pb_ref, alphas_b)
        finalize()


def _attn(kn, qT, k4, vT, lam, g, B, S):
    t = SEQ_TILE
    nt = S // t
    grid_spec = pltpu.PrefetchScalarGridSpec(
        num_scalar_prefetch=1,
        grid=(B, N_HEADS, nt),
        in_specs=[pl.BlockSpec((None, None, None, V_DIM, t), lambda b, h, i, kn: (b, h, i, 0, 0)),
                  pl.BlockSpec((None, nt, t, K_COLS), lambda b, h, i, kn: (b, 0, 0, h)),
                  pl.BlockSpec((None, None, nt, V_ROWS, t), lambda b, h, i, kn: (b, h, 0, 0, 0)),
                  pl.BlockSpec((4, HEAD_DIM), lambda b, h, i, kn: (0, 0)),
                  pl.BlockSpec((1, V_DIM), lambda b, h, i, kn: (0, 0))],
        out_specs=pl.BlockSpec((None, t, V_DIM), lambda b, h, i, kn: (b, i, h)),
        scratch_shapes=[pltpu.VMEM((t, t), F32), pltpu.VMEM((2, V_ROWS, t), F32),
                        pltpu.VMEM((2, t, t + LANES), F32), pltpu.VMEM((2, t, t + LANES), F32),
                        pltpu.VMEM((2, t, t + LANES), BF16), pltpu.VMEM((2, t, t + LANES), BF16)],
    )
    return pl.pallas_call(
        _attn_kernel,
        grid_spec=grid_spec,
        out_shape=jax.ShapeDtypeStruct((B, S, ATTN_WIDTH), BF16),
        compiler_params=pltpu.CompilerParams(dimension_semantics=("parallel", "arbitrary", "arbitrary"),
                                             vmem_limit_bytes=VMEM_LIMIT),
        name="diff_attn",
    )(kn, qT, k4, vT, lam, g)


def _lru_tile(x, y, ti, cw_ref, cb_ref, wr_ref, wi_ref, br_ref, bi_ref, lam_ref, xbuf, hc):
    T = SEQ_TILE
    xbuf[8:8 + T] = x
    cw = cw_ref[...]
    xc = cb_ref[...] + cw[3:4] * x
    for j in range(CONV_W - 1):
        xc = xc + cw[j:j + 1] * xbuf[5 + j:5 + j + T]
    xbuf[0:8] = x[T - 8:T]

    xb = xc.astype(BF16)
    sigmoid = lambda z: 0.5 * jnp.tanh(0.5 * z) + 0.5
    r = sigmoid(_dot(xb, wr_ref[...]) + br_ref[...])
    ig = sigmoid(_dot(xb, wi_ref[...]) + bi_ref[...])
    z = -lam_ref[...]
    softplus = jnp.maximum(z, 0.0) + jnp.log1p(jnp.exp(-jnp.abs(z)))
    la = -LRU_C * r * softplus
    a = jnp.exp(la)
    m2 = -jnp.tanh(la) * (a * a + 1.0)
    mult = jnp.where(m2 > 0.0, m2 * lax.rsqrt(m2), 0.0)
    row = lax.broadcasted_iota(jnp.int32, (T, LRU_WIDTH), 0)
    mult = jnp.where((row == 0) & (ti == 0), 1.0, mult)
    u = (xc * ig) * mult

    nb = T // SUBLANES
    a3 = a.reshape(nb, SUBLANES, LRU_WIDTH)
    u3 = u.reshape(nb, SUBLANES, LRU_WIDTH)
    sub = lax.broadcasted_iota(jnp.int32, a3.shape, 1)
    d = 1
    while d < SUBLANES:
        valid = sub >= d
        u3 = jnp.where(valid, a3 * pltpu.roll(u3, d, 1) + u3, u3)
        a3 = jnp.where(valid, a3 * pltpu.roll(a3, d, 1), a3)
        d *= 2
    h = hc[...]
    blocks = []
    for b in range(nb):
        hb = u3[b] + a3[b] * h
        blocks.append(hb)
        h = hb[SUBLANES - 1:SUBLANES]
    hfull = jnp.concatenate(blocks, axis=0)
    hc[...] = h
    gelu = 0.5 * y * (1.0 + jnp.tanh(0.7978845608028654 * (y + 0.044715 * (y * y * y))))
    return (hfull * gelu).astype(BF16)


def _pack_words(v):
    bits = pltpu.bitcast(v.astype(BF16).astype(F32), jnp.uint32)
    half = D_MODEL // 2
    packed = (bits[:, :half] >> 16) | (bits[:, half:] & jnp.uint32(0xFFFF0000))
    return [packed[:, c * LANES:(c + 1) * LANES] for c in range(PACK_SUB)]


def _packed_chunk(c, rows, first_row=0):
    return (pl.ds(first_row * PACK_SUB + c, rows, stride=PACK_SUB), slice(None))


def _pack_rows(v, out_ref, first_row=0):
    for c, words in enumerate(_pack_words(v)):
        out_ref[_packed_chunk(c, v.shape[0], first_row)] = words


def _unpack_chunks(in_ref, rows, first_row=0):
    lo, hi = [], []
    for c in range(PACK_SUB):
        w = in_ref[_packed_chunk(c, rows, first_row)]
        lo.append(pltpu.bitcast(w << 16, F32))
        hi.append(pltpu.bitcast(w & jnp.uint32(0xFFFF0000), F32))
    return lo + hi


def _unpack_rows(in_ref, rows):
    return jnp.concatenate(_unpack_chunks(in_ref, rows), axis=1).astype(BF16)


def _merge_kernel(x_ref, attn_ref, lru_ref, g1_ref, wg_ref, woa_ref, wol_ref, wout_ref, g2_ref, wrt_ref,
                  x1_ref, h2p_ref, route_ref, counts_ref, cnt, tri):
    tm = MERGE_CHAIN

    @pl.when(pl.program_id(0) == 0)
    def _():
        cnt[...] = jnp.zeros_like(cnt)
        r = lax.broadcasted_iota(jnp.int32, (tm, tm), 0)
        c = lax.broadcasted_iota(jnp.int32, (tm, tm), 1)
        tri[...] = jnp.where(r < c, 1.0, 0.0).astype(BF16)

    counts = cnt[...]
    for r0 in range(0, SEQ_TILE, tm):
        counts = _merge_chain(r0, tm, counts, x_ref, attn_ref, lru_ref, g1_ref, wg_ref, woa_ref, wol_ref, wout_ref,
                              g2_ref, wrt_ref, x1_ref, h2p_ref, route_ref, tri)
    cnt[...] = counts
    counts_ref[...] = counts


def _merge_chain(r0, tm, counts, x_ref, attn_ref, lru_ref, g1_ref, wg_ref, woa_ref, wol_ref, wout_ref, g2_ref, wrt_ref,
                 x1_ref, h2p_ref, route_ref, tri):
    rs = slice(r0, r0 + tm)
    x = x_ref[rs, :]
    hb = _rms(x, g1_ref[...]).astype(BF16)
    gates = 0.5 * jnp.tanh(0.5 * _dot(hb, wg_ref[:, PROJ_COLS:])) + 0.5
    merged = gates[:, :D_MODEL] * _dot(attn_ref[rs, :], woa_ref[...]) + gates[:, D_MODEL:] * _dot(lru_ref[rs, :], wol_ref[...])
    x1 = x + _dot(merged.astype(BF16), wout_ref[...])
    x1_ref[rs, :] = x1
    h2 = _rms(x1, g2_ref[...])
    _pack_rows(h2, h2p_ref, first_row=r0)

    h_hi = h2.astype(BF16)
    h_lo = (h2 - h_hi.astype(F32)).astype(BF16)
    wrt = wrt_ref[...]
    nt_dims = (((1,), (1,)), ((), ()))
    hh = lax.dot_general(wrt, h_hi, nt_dims, preferred_element_type=F32)
    lo_pass = lax.dot_general(wrt[:ROUTE_LANES], h_lo, nt_dims, preferred_element_type=F32)
    logits = (hh[:ROUTE_LANES] + hh[ROUTE_LANES:] + lo_pass)[:ROUTE_ROWS]
    row = lax.broadcasted_iota(jnp.int32, logits.shape, 0)
    big = jnp.int32(1 << 20)

    def first_argmax(v):
        m = jnp.max(v, axis=0, keepdims=True)
        return m, jnp.min(jnp.where(v == m, row, big), axis=0, keepdims=True)

    gmask = row < N_GROUPS
    gmax, gidx = first_argmax(jnp.where(gmask, logits, -jnp.inf))
    gsum = jnp.sum(jnp.where(gmask, jnp.exp(logits - gmax), 0.0), axis=0, keepdims=True)
    g_w = 1.0 / gsum
    lo = N_GROUPS + EXPERTS_PER_GROUP * gidx
    el = jnp.where((row >= lo) & (row < lo + EXPERTS_PER_GROUP), logits, -jnp.inf)
    m1, i1 = first_argmax(el)
    m2, i2 = first_argmax(jnp.where(row == i1, -jnp.inf, el))
    rr = jnp.exp(m2 - m1)
    w1 = g_w / (1.0 + rr)
    w2 = g_w * rr / (1.0 + rr)
    oh1 = row == i1
    oh2 = row == i2
    oh = jnp.where(oh1 | oh2, 1.0, 0.0)
    before = _dot(oh.astype(BF16), tri[...]) + counts
    r1 = jnp.sum(jnp.where(oh1, before, 0.0), axis=0, keepdims=True)
    r2 = jnp.sum(jnp.where(oh2, before, 0.0), axis=0, keepdims=True)
    vals = (i1.astype(F32), i2.astype(F32), w1, w2, r1, r2, jnp.zeros_like(w1), jnp.zeros_like(w1))
    for k, v in enumerate(vals):
        route_ref[k:k + 1, rs] = v
    return counts + jnp.sum(oh, axis=1, keepdims=True)


def _merge(x2, attn, lru, g1, wg, woa, wol, wout, g2, wrt):
    N = x2.shape[0]
    tm = SEQ_TILE
    rows = lambda c: pl.BlockSpec((tm, c), lambda i: (i, 0))
    full = lambda r, c: pl.BlockSpec((r, c), lambda i: (0, 0))
    return pl.pallas_call(
        _merge_kernel,
        grid=(N // tm,),
        in_specs=[rows(D_MODEL), rows(ATTN_WIDTH), rows(LRU_WIDTH), full(1, D_MODEL),
                  full(D_MODEL, PROJ_COLS + 2 * D_MODEL),
                  full(ATTN_WIDTH, D_MODEL), full(LRU_WIDTH, D_MODEL), full(D_MODEL, D_MODEL), full(1, D_MODEL),
                  full(2 * ROUTE_LANES, D_MODEL)],
        out_specs=[rows(D_MODEL), pl.BlockSpec((tm * PACK_SUB, LANES), lambda i: (i, 0)),
                   pl.BlockSpec((ROUTE_FIELDS, tm), lambda i: (0, i)), full(ROUTE_ROWS, 1)],
        out_shape=[jax.ShapeDtypeStruct((N, D_MODEL), F32), jax.ShapeDtypeStruct((N * PACK_SUB, LANES), jnp.uint32),
                   jax.ShapeDtypeStruct((ROUTE_FIELDS, N), F32), jax.ShapeDtypeStruct((ROUTE_ROWS, 1), F32)],
        scratch_shapes=[pltpu.VMEM((ROUTE_ROWS, 1), F32), pltpu.VMEM((MERGE_CHAIN, MERGE_CHAIN), BF16)],
        compiler_params=pltpu.CompilerParams(dimension_semantics=("arbitrary",), vmem_limit_bytes=VMEM_LIMIT),
        name="merge_route",
    )(x2, attn, lru, g1, wg, woa, wol, wout, g2, wrt)


def _for_each_assignment(off_ref, fn):
    def group(gi, _):
        toks = [gi * DMA_GROUP + j for j in range(DMA_GROUP)]
        offs = [[off_ref[0, 0, k * ROW_TILE + tk] for k in range(TOP_K)] for tk in toks]
        for tk, o in zip(toks, offs):
            for k in range(TOP_K):
                fn(tk, k, pl.multiple_of(o[k], PACK_SUB))
        return 0
    lax.fori_loop(0, ROW_TILE // DMA_GROUP, group, 0)


def _dispatch_kernel(off_ref, h2p_hbm, xs_hbm, xin, in_sem, out_sem):
    tile_rows = ROW_TILE * PACK_SUB
    i = pl.program_id(0)
    n = pl.num_programs(0)
    slot = lax.rem(i, DISPATCH_SLOTS)

    def fetch(tile, s):
        return pltpu.make_async_copy(h2p_hbm.at[pl.ds(pl.multiple_of(tile * tile_rows, tile_rows), tile_rows)],
                                     xin.at[s], in_sem.at[s])

    def drain(s):
        for _ in range(TOP_K):
            pltpu.make_async_copy(xin.at[s], xs_hbm.at[pl.ds(0, tile_rows)], out_sem.at[s]).wait()

    @pl.when(i == 0)
    def _():
        fetch(0, 0).start()

    @pl.when(i + 1 < n)
    def _():
        fetch(i + 1, lax.rem(i + 1, DISPATCH_SLOTS)).start()

    fetch(i, slot).wait()

    def start(tk, k, off):
        pltpu.make_async_copy(xin.at[slot, pl.ds(pl.multiple_of(tk * PACK_SUB, PACK_SUB), PACK_SUB)],
                              xs_hbm.at[pl.ds(off, PACK_SUB)], out_sem.at[slot]).start(priority=k)

    _for_each_assignment(off_ref, start)

    @pl.when(i > 0)
    def _():
        drain(lax.rem(i + DISPATCH_SLOTS - 1, DISPATCH_SLOTS))

    @pl.when(i == n - 1)
    def _():
        drain(slot)


def _dispatch(off3, h2p):
    return pl.pallas_call(
        _dispatch_kernel,
        grid=(off3.shape[0],),
        in_specs=[pl.BlockSpec((1, 1, TOP_K * ROW_TILE), lambda i: (i, 0, 0), memory_space=pltpu.SMEM),
                  pl.BlockSpec(memory_space=pl.ANY)],
        out_specs=pl.BlockSpec(memory_space=pl.ANY),
        out_shape=jax.ShapeDtypeStruct((TOP_K * h2p.shape[0], LANES), jnp.uint32),
        scratch_shapes=[pltpu.VMEM((DISPATCH_SLOTS, ROW_TILE * PACK_SUB, LANES), jnp.uint32),
                        pltpu.SemaphoreType.DMA((DISPATCH_SLOTS,)), pltpu.SemaphoreType.DMA((DISPATCH_SLOTS,))],
        compiler_params=pltpu.CompilerParams(dimension_semantics=("arbitrary",), has_side_effects=True),
        name="moe_dispatch",
    )(off3, h2p)


def _moe_kernel(vt_ref, ve_ref, vlo_ref, vhi_ref, vnext_ref, vpar_ref, xs_ref, wg_hbm, wu_hbm, wd_hbm, ys_ref,
                wgb, wub, wdb, wgf, wuf, wdf, sem):
    tm = MOE_TILE
    v = pl.program_id(0)
    t = vt_ref[v]
    e = ve_ref[v]
    slot = vpar_ref[v]
    prev = jnp.maximum(v - 1, 0)

    def fetch(expert, s):
        return [pltpu.make_async_copy(w_hbm.at[expert], buf.at[s], sem.at[s])
                for w_hbm, buf in ((wg_hbm, wgf), (wu_hbm, wuf), (wd_hbm, wdf))]

    @pl.when(v == 0)
    def _():
        for copy in fetch(e, slot):
            copy.start()

    @pl.when((v == 0) | (ve_ref[prev] != e))
    def _():
        for copy in fetch(e, slot):
            copy.wait()
        nxt = vnext_ref[v]

        @pl.when(nxt != e)
        def _():
            for copy in fetch(nxt, 1 - slot):
                copy.start()

        wgb[...] = wgf[slot].astype(BF16)
        wub[...] = wuf[slot].astype(BF16)
        wdb[...] = wdf[slot].astype(BF16)

    xb = _unpack_rows(xs_ref, tm)
    g = _dot(xb, wgb[...])
    u = _dot(xb, wub[...])
    hmid = (g * jax.nn.sigmoid(g)) * u
    words = _pack_words(_dot(hmid.astype(BF16), wdb[...]))
    first = (v == 0) | (vt_ref[prev] != t)

    @pl.when(first)
    def _():
        for c in range(PACK_SUB):
            ys_ref[_packed_chunk(c, tm)] = words[c]

    @pl.when(jnp.logical_not(first))
    def _():
        rows = t * tm + lax.broadcasted_iota(jnp.int32, (tm, LANES), 0)
        mine = (rows >= vlo_ref[v]) & (rows < vhi_ref[v])
        for c in range(PACK_SUB):
            ys_ref[_packed_chunk(c, tm)] = jnp.where(mine, words[c], ys_ref[_packed_chunk(c, tm)])


def _moe(plan, xs, wg, wu, wd):
    tm = MOE_TILE
    n_rows = xs.shape[0] // PACK_SUB
    tile = lambda v, vt, *_: (vt[v], 0)
    hbm = pl.BlockSpec(memory_space=pl.ANY)
    grid_spec = pltpu.PrefetchScalarGridSpec(
        num_scalar_prefetch=len(plan),
        grid=(plan[0].shape[0],),
        in_specs=[pl.BlockSpec((tm * PACK_SUB, LANES), tile), hbm, hbm, hbm],
        out_specs=pl.BlockSpec((tm * PACK_SUB, LANES), tile),
        scratch_shapes=[pltpu.VMEM((D_MODEL, D_EXPERT), BF16), pltpu.VMEM((D_MODEL, D_EXPERT), BF16),
                        pltpu.VMEM((D_EXPERT, D_MODEL), BF16),
                        pltpu.VMEM((2, D_MODEL, D_EXPERT), F32), pltpu.VMEM((2, D_MODEL, D_EXPERT), F32),
                        pltpu.VMEM((2, D_EXPERT, D_MODEL), F32), pltpu.SemaphoreType.DMA((2,))],
    )
    return pl.pallas_call(
        _moe_kernel,
        grid_spec=grid_spec,
        out_shape=jax.ShapeDtypeStruct((n_rows * PACK_SUB, LANES), jnp.uint32),
        compiler_params=pltpu.CompilerParams(dimension_semantics=("arbitrary",), vmem_limit_bytes=VMEM_LIMIT),
        name="moe_experts",
    )(*plan, xs, wg, wu, wd)


def _combine_kernel(off_ref, offn_ref, x1_ref, w_ref, g_ref, ys_hbm, o_ref, ybuf, sem):
    tc = ROW_TILE
    i = pl.program_id(0)
    n = pl.num_programs(0)
    slot = lax.rem(i, 2)

    def gather(o_ref_, s):
        def start(tk, k, off):
            pltpu.make_async_copy(ys_hbm.at[pl.ds(off, PACK_SUB)],
                                  ybuf.at[s, pl.ds(pl.multiple_of((k * tc + tk) * PACK_SUB, PACK_SUB), PACK_SUB)],
                                  sem.at[s]).start(priority=k)
        _for_each_assignment(o_ref_, start)

    def drain(s):
        pltpu.make_async_copy(ys_hbm.at[pl.ds(0, TOP_K * tc * PACK_SUB)], ybuf.at[s], sem.at[s]).wait()

    @pl.when(i == 0)
    def _():
        gather(off_ref, 0)

    @pl.when(i + 1 < n)
    def _():
        gather(offn_ref, 1 - slot)

    drain(slot)

    n_chunks = D_MODEL // LANES
    z = [x1_ref[:, c * LANES:(c + 1) * LANES] for c in range(n_chunks)]
    for k in range(TOP_K):
        wk = w_ref[:, k:k + 1]
        yk = _unpack_chunks(ybuf.at[slot], tc, first_row=k * tc)
        z = [zc + wk * yc for zc, yc in zip(z, yk)]
    ss = sum(jnp.sum(zc * zc, axis=-1, keepdims=True) for zc in z)
    inv = lax.rsqrt(ss * (1.0 / D_MODEL) + NORM_EPS)
    for c in range(n_chunks):
        o_ref[:, c * LANES:(c + 1) * LANES] = z[c] * inv * g_ref[:, c * LANES:(c + 1) * LANES]


def _combine(off3, x1, weight, g, ys):
    N = x1.shape[0]
    tc = ROW_TILE
    nt = N // tc
    idx = lambda f: pl.BlockSpec((1, 1, TOP_K * tc), f, memory_space=pltpu.SMEM)
    return pl.pallas_call(
        _combine_kernel,
        grid=(nt,),
        in_specs=[idx(lambda i: (i, 0, 0)),
                  idx(lambda i: (jnp.minimum(i + 1, nt - 1), 0, 0)),
                  pl.BlockSpec((tc, D_MODEL), lambda i: (i, 0)),
                  pl.BlockSpec((tc, TOP_K), lambda i: (i, 0)),
                  pl.BlockSpec((1, D_MODEL), lambda i: (0, 0)),
                  pl.BlockSpec(memory_space=pl.ANY)],
        out_specs=pl.BlockSpec((tc, D_MODEL), lambda i: (i, 0)),
        out_shape=jax.ShapeDtypeStruct((N, D_MODEL), F32),
        scratch_shapes=[pltpu.VMEM((2, TOP_K * tc * PACK_SUB, LANES), jnp.uint32), pltpu.SemaphoreType.DMA((2,))],
        compiler_params=pltpu.CompilerParams(dimension_semantics=("arbitrary",), vmem_limit_bytes=VMEM_LIMIT),
        name="combine_norm",
    )(off3, off3, x1, weight, g, ys)


def _block_diag(w):
    nb, c, _ = w.shape
    eye = jnp.eye(nb, dtype=w.dtype)
    return (eye[:, None, :, None] * w[:, :, None, :]).reshape(nb * c, nb * c)


def _visit_plan(counts, n_rows):
    tm = MOE_TILE
    n_tiles = n_rows // tm
    n_visits = n_tiles + N_EXPERTS - 1
    cnt = counts.astype(jnp.int32)
    lanes = jnp.arange(ROUTE_ROWS, dtype=jnp.int32)
    upto = (lanes[None, :] <= lanes[:, None]).astype(jnp.int32)
    ends = upto @ cnt
    starts = ends - cnt
    first_tile = starts // tm
    n_vis = jnp.where(cnt > 0, (ends - 1) // tm - first_tile + 1, 0)
    v_end = upto @ n_vis
    v_start = v_end - n_vis
    total = v_end[-1]
    v = jnp.arange(n_visits, dtype=jnp.int32)
    vc = jnp.minimum(v, total - 1)
    own = ((vc[:, None] >= v_start[None, :]) & (vc[:, None] < v_end[None, :])).astype(jnp.int32)
    pick = lambda per_lane: own @ per_lane
    tile = jnp.maximum(pick(first_tile - v_start) + vc, 0)
    lo = jnp.maximum(pick(starts), tile * tm)
    hi = jnp.minimum(pick(ends), (tile + 1) * tm)
    valid = v < total
    to_expert = lambda lane: jnp.maximum(lane - EXPERT_LANE0, 0)
    used = cnt > 0
    later = jnp.where((lanes[None, :] > lanes[:, None]) & used[None, :], lanes[None, :], ROUTE_ROWS)
    nxt = jnp.min(later, axis=1)
    nxt = jnp.where(nxt == ROUTE_ROWS, lanes, nxt)
    parity = (upto @ used.astype(jnp.int32) - 1) % 2
    plan = (tile, to_expert(pick(lanes)), jnp.where(valid, lo, 0), jnp.where(valid, hi, 0),
            to_expert(pick(nxt)), pick(parity))
    return starts, plan


def kernel(x, norm_mix_g, w_in, lambda_qk, subln_g, conv_w, conv_b, w_r, b_r, w_i, b_i, lru_lambda, w_o_attn, w_o_lru, w_out, norm_ffn_g, w_group, w_expert_router, w_gate, w_up, w_down, final_norm_g):
    B, S, D = x.shape
    N = B * S
    nt = S // SEQ_TILE
    depth = norm_mix_g.shape[0]
    assert depth == 1 and D == D_MODEL and S % SEQ_TILE == 0
    assert N % ROW_TILE == 0 and (N * TOP_K) % MOE_TILE == 0
    l = 0
    row = lambda v: v.reshape(1, -1).astype(F32)

    x2 = x.reshape(N, D)
    w_in_l = w_in[l].astype(BF16)
    lru_params = (conv_w[l].astype(F32), row(conv_b[l]), _block_diag(w_r[l]).astype(BF16),
                  _block_diag(w_i[l]).astype(BF16), row(b_r[l]), row(b_i[l]), row(lru_lambda[l]))
    qT, k, vT, lru, kn = _inproj(x2, row(norm_mix_g[l]), w_in_l, lru_params, B, S)

    kn_max = jnp.max(kn[:, 0, :2 * N_HEADS].reshape(B, nt, 2 * N_HEADS), axis=1)
    attn = _attn(kn_max.reshape(-1), qT, k.reshape(B, nt, SEQ_TILE, N_HEADS * K_COLS), vT,
                 lambda_qk[l].reshape(4, HEAD_DIM).astype(F32), row(subln_g[l]), B, S)

    w_route = jnp.concatenate(
        [w_group[l], jnp.transpose(w_expert_router[l], (1, 0, 2)).reshape(D, N_EXPERTS),
         jnp.zeros((D, ROUTE_LANES - N_GROUPS - N_EXPERTS), F32)], axis=1).astype(F32)
    w_route_hi = w_route.astype(BF16)
    w_route = jnp.concatenate([w_route_hi, (w_route - w_route_hi.astype(F32)).astype(BF16)], axis=1).T
    x1, h2p, route, counts = _merge(x2, attn.reshape(N, ATTN_WIDTH), lru.reshape(N, LRU_WIDTH), row(norm_mix_g[l]),
                                    w_in_l, w_o_attn[l].astype(BF16),
                                    w_o_lru[l].astype(BF16), w_out[l].astype(BF16), row(norm_ffn_g[l]), w_route)

    starts, plan = _visit_plan(counts[:, 0], N * TOP_K)
    lane = route[0:TOP_K].astype(jnp.int32)
    weight = jnp.transpose(route[TOP_K:2 * TOP_K])
    rank = route[2 * TOP_K:3 * TOP_K].astype(jnp.int32)
    lane_ids = jnp.arange(ROUTE_ROWS, dtype=jnp.int32)
    start_of = jnp.sum(jnp.where(lane[..., None] == lane_ids, starts, 0), axis=-1)
    off = ((start_of + rank) * PACK_SUB).reshape(TOP_K, N // ROW_TILE, ROW_TILE)
    off3 = jnp.transpose(off, (1, 0, 2)).reshape(N // ROW_TILE, 1, TOP_K * ROW_TILE)

    xs = _dispatch(off3, h2p)
    ys = _moe(plan, xs, w_gate[l], w_up[l], w_down[l])
    out = _combine(off3, x1, weight, row(final_norm_g), ys)
    return out.reshape(B, S, D)
```

```python
import functools
import math

import jax
import jax.numpy as jnp
from jax import lax
from jax.experimental import pallas as pl
from jax.experimental.pallas import tpu as pltpu

F32 = jnp.float32
BF16 = jnp.bfloat16

D_MODEL = 1024
N_HEADS = 4
HEAD_DIM = 64
V_DIM = 2 * HEAD_DIM
ATTN_WIDTH = N_HEADS * V_DIM
V_ROWS = V_DIM + 16
K_COLS = 2 * V_DIM
POS_SPLIT = 3
POS_RADIX = 256
SKIP_MARGIN = 138.0
NORM_SLACK = 1.0201
LRU_WIDTH = D_MODEL // 2
LRU_BLOCKS = 8
CONV_W = 4
LRU_C = 8.0
N_GROUPS = 4
EXPERTS_PER_GROUP = 8
N_EXPERTS = N_GROUPS * EXPERTS_PER_GROUP
TOP_K = 2
D_EXPERT = D_MODEL // 2
NORM_EPS = 1e-6
LAM_INIT = 0.8 - 0.6 * math.exp(-0.3 * 0)

QK_COLS = N_HEADS * 2 * HEAD_DIM
PROJ_COLS = 2 * QK_COLS + ATTN_WIDTH + 2 * LRU_WIDTH
ROUTE_LANES = 128
ROUTE_ROWS = 64
ROUTE_FIELDS = 8
NEG_BIG = -1e30
LOG2E = math.log2(math.e)
ALIBI_SLOPES = tuple(2.0 ** (-8.0 * (h + 1) / N_HEADS) for h in range(N_HEADS))

SEQ_TILE = 512
ROW_TILE = 256
MERGE_CHAIN = 512
MOE_TILE = 256
DMA_GROUP = 8
DISPATCH_SLOTS = 3
LANES = 128
SUBLANES = 8
PACK_SUB = D_MODEL // 2 // LANES
EXPERT_LANE0 = N_GROUPS
V7X_VMEM_BYTES = 64 * 1024 * 1024
VMEM_LIMIT = V7X_VMEM_BYTES * 3 // 4


def _rms(x, g):
    return x * lax.rsqrt(jnp.mean(x * x, axis=-1, keepdims=True) + NORM_EPS) * g


def _dot(a, b):
    return jnp.dot(a, b, preferred_element_type=F32)


def _inproj_kernel(tiles_per_seq, x_ref, g_ref, w_ref, cw_ref, cb_ref, wr_ref, wi_ref, br_ref, bi_ref, lam_ref,
                   qT_ref, k_ref, vT_ref, lru_ref, kn_ref, xbuf, hc):
    ti = lax.rem(pl.program_id(0), tiles_per_seq)

    @pl.when(ti == 0)
    def _():
        xbuf[0:8] = jnp.zeros((8, LRU_WIDTH), F32)
        hc[...] = jnp.zeros_like(hc)

    hb = _rms(x_ref[...], g_ref[...]).astype(BF16)

    def proj(lo, hi):
        return _dot(hb, w_ref[:, lo:hi])

    c0 = 2 * QK_COLS + ATTN_WIDTH
    lru_x = proj(c0, c0 + LRU_WIDTH)
    lru_y = proj(c0 + LRU_WIDTH, c0 + 2 * LRU_WIDTH)

    q = proj(0, QK_COLS) * (HEAD_DIM ** -0.5 * LOG2E)
    for h in range(N_HEADS):
        qT_ref[h] = q[:, h * V_DIM:(h + 1) * V_DIM].T.astype(BF16)
    k = proj(QK_COLS, 2 * QK_COLS).astype(BF16)
    r = lax.broadcasted_iota(jnp.int32, (k.shape[0], K_COLS - V_DIM), 0)
    lane = lax.broadcasted_iota(jnp.int32, r.shape, 1)
    a = r // POS_RADIX * POS_RADIX
    feat = jnp.where(lane < POS_SPLIT, a, jnp.where(lane < 2 * POS_SPLIT, r - a, 0)).astype(F32).astype(BF16)
    for h in range(N_HEADS):
        k_ref[:, h * K_COLS:h * K_COLS + V_DIM] = k[:, h * V_DIM:(h + 1) * V_DIM]
        k_ref[:, h * K_COLS + V_DIM:(h + 1) * K_COLS] = feat
    seg = lax.broadcasted_iota(jnp.int32, (QK_COLS, LANES), 0) // HEAD_DIM
    pick = jnp.where(seg == lax.broadcasted_iota(jnp.int32, (QK_COLS, LANES), 1), 1.0, 0.0).astype(BF16)
    row_n2 = _dot(jnp.square(k.astype(F32)).astype(BF16), pick)
    kn_ref[...] = jnp.max(row_n2, axis=0, keepdims=True)
    v = proj(2 * QK_COLS, 2 * QK_COLS + ATTN_WIDTH)
    for h in range(N_HEADS):
        vT_ref[h, :V_DIM, :] = v[:, h * V_DIM:(h + 1) * V_DIM].T.astype(BF16)
        pad_row = lax.broadcasted_iota(jnp.int32, (V_ROWS - V_DIM, v.shape[0]), 0)
        vT_ref[h, V_DIM:, :] = jnp.where(pad_row == 0, 1.0, 0.0).astype(BF16)

    lru_ref[...] = _lru_tile(lru_x, lru_y, ti, cw_ref, cb_ref, wr_ref, wi_ref, br_ref, bi_ref, lam_ref, xbuf, hc)


def _inproj(x2, g, w, lru_params, B, S):
    N = B * S
    tm = SEQ_TILE
    nt = S // tm
    tile5 = pl.BlockSpec((None, N_HEADS, None, V_DIM, tm), lambda i: (i // nt, 0, i % nt, 0, 0))
    rows = lambda c: pl.BlockSpec((tm, c), lambda i: (i, 0))
    full = lambda a: pl.BlockSpec(a.shape, lambda i: (0, 0))
    return pl.pallas_call(
        functools.partial(_inproj_kernel, nt),
        grid=(N // tm,),
        in_specs=[rows(D_MODEL),
                  pl.BlockSpec((1, D_MODEL), lambda i: (0, 0)),
                  pl.BlockSpec((D_MODEL, PROJ_COLS), lambda i: (0, 0))] + [full(a) for a in lru_params],
        out_specs=[tile5, rows(N_HEADS * K_COLS),
                   pl.BlockSpec((None, N_HEADS, None, V_ROWS, tm), lambda i: (i // nt, 0, i % nt, 0, 0)),
                   rows(LRU_WIDTH),
                   pl.BlockSpec((None, 1, LANES), lambda i: (i, 0, 0))],
        out_shape=[jax.ShapeDtypeStruct((B, N_HEADS, nt, V_DIM, tm), BF16),
                   jax.ShapeDtypeStruct((N, N_HEADS * K_COLS), BF16),
                   jax.ShapeDtypeStruct((B, N_HEADS, nt, V_ROWS, tm), BF16),
                   jax.ShapeDtypeStruct((N, LRU_WIDTH), BF16),
                   jax.ShapeDtypeStruct((N // tm, 1, LANES), F32)],
        scratch_shapes=[pltpu.VMEM((tm + 8, LRU_WIDTH), F32), pltpu.VMEM((1, LRU_WIDTH), F32)],
        compiler_params=pltpu.CompilerParams(dimension_semantics=("arbitrary",),
                                             vmem_limit_bytes=VMEM_LIMIT),
        name="inproj_lru",
    )(x2, g, w, *lru_params)


def _attn_kernel(kn_ref, qT_ref, k_ref, vT_ref, lam_ref, g_ref, o_ref, mask_ref, acc_ref, sa_ref, sb_ref, pa_ref,
                 pb_ref):
    t = SEQ_TILE
    b = pl.program_id(0)
    h = pl.program_id(1)
    i = pl.program_id(2)
    slope = LOG2E * jnp.where(h == 0, ALIBI_SLOPES[0], jnp.where(h == 1, ALIBI_SLOPES[1],
                              jnp.where(h == 2, ALIBI_SLOPES[2], ALIBI_SLOPES[3]))).astype(F32)

    @pl.when(i == 0)
    def _():
        r = lax.broadcasted_iota(jnp.int32, (t, t), 0)
        c = lax.broadcasted_iota(jnp.int32, (t, t), 1)
        mask_ref[...] = jnp.where(r <= c, 0.0, NEG_BIG)
        acc_ref[...] = jnp.zeros_like(acc_ref)
        pb_ref[...] = jnp.zeros_like(pb_ref)

    qf = qT_ref[...].astype(F32)
    row = lax.broadcasted_iota(jnp.int32, qf.shape, 0)
    sl = jnp.full(qf.shape, slope, F32)
    hi = sl.astype(BF16).astype(F32)
    mid = (sl - hi).astype(BF16).astype(F32)
    lo = (sl - hi - mid).astype(BF16).astype(F32)
    piece = jnp.where(row % 3 == 0, hi, jnp.where(row % 3 == 1, mid, lo))
    srows = jnp.where(row < 2 * POS_SPLIT, piece, 0.0).astype(BF16)
    qs = tuple(jnp.concatenate([jnp.where(sel, qf, 0.0).astype(BF16), srows], axis=0)
               for sel in (row < HEAD_DIM, row >= HEAD_DIM))

    first_off = [0]

    def key_tile(tau):
        return jnp.where(tau <= 0, i, first_off[0] + tau - 1)

    def stage_q(tau, s_ref, diagonal=False):
        kt = k_ref[key_tile(tau)]
        tile_max = []
        for mi in range(2):
            s = _dot(kt, qs[mi])
            if diagonal:
                s = s + mask_ref[...]
            s_ref[mi, :, :t] = s
            tile_max.append(jnp.max(s, axis=0, keepdims=True))
        return tuple(tile_max)

    def stage_s(tau, s_ref, p_ref, ms, tile_max):
        cj = slope * (key_tile(tau) * t).astype(F32)
        m_out, alphas = [], []
        for mi in range(2):
            m_new = jnp.maximum(ms[mi], tile_max[mi] + cj)
            alphas.append(jnp.exp2(ms[mi] - m_new))
            p_ref[mi, :, :t] = jnp.exp2(s_ref[mi, :, :t] - (m_new - cj)).astype(BF16)
            m_out.append(m_new)
        return tuple(m_out), tuple(alphas)

    def stage_v(tau, p_ref, alphas):
        vt = vT_ref[key_tile(tau)]
        for mi in range(2):
            acc_ref[mi] = alphas[mi] * acc_ref[mi] + _dot(vt, p_ref[mi, :, :t])

    def body(jj, carry):
        ms, alphas, tmax = carry[:2], carry[2:4], carry[4:]
        tau = 2 * jj
        tmax_b = stage_q(tau + 1, sb_ref)
        ms, alphas_a = stage_s(tau, sa_ref, pa_ref, ms, tmax)
        stage_v(tau - 1, pb_ref, alphas)
        tmax_a = stage_q(tau + 2, sa_ref)
        ms, alphas_b = stage_s(tau + 1, sb_ref, pb_ref, ms, tmax_b)
        stage_v(tau, pa_ref, alphas_a)
        return ms + alphas_b + tmax_a

    def finalize():
        lp = lam_ref[...]
        s1 = jnp.sum(lp[0:1] * lp[1:2], axis=-1, keepdims=True)
        s2 = jnp.sum(lp[2:3] * lp[3:4], axis=-1, keepdims=True)
        lam = jnp.exp(s1) - jnp.exp(s2) + LAM_INIT
        norm = [acc_ref[mi, :V_DIM, :] * (1.0 / acc_ref[mi, V_DIM:V_DIM + 1, :]) for mi in range(2)]
        oT = norm[0] - lam * norm[1]
        o = _rms(oT.T, g_ref[...]) * (1.0 - LAM_INIT)
        o_ref[...] = o.astype(BF16)

    m_init = jnp.full((1, t), NEG_BIG, F32)
    alpha_init = jnp.zeros((1, t), F32)
    tmax0 = stage_q(0, sa_ref, diagonal=True)

    cj0 = slope * (i * t).astype(F32)
    n_skip = None
    for mi in range(2):
        qsq = jnp.square(qf[mi * HEAD_DIM:(mi + 1) * HEAD_DIM, :])
        qn2 = jnp.max(jnp.sum(qsq, axis=0, keepdims=True), axis=1, keepdims=True)
        m_low = jnp.min(tmax0[mi], axis=1, keepdims=True) + cj0
        qk = jnp.sqrt(qn2 * (kn_ref[(b * N_HEADS + h) * 2 + mi] * NORM_SLACK))
        count = jnp.ceil((m_low - SKIP_MARGIN - slope * (t - 1) - qk) / (slope * t))
        n_skip = count if n_skip is None else jnp.minimum(n_skip, count)
    first_off[0] = jnp.clip(n_skip, 0.0, i.astype(F32)).astype(jnp.int32)[0, 0]
    n_off = i - first_off[0]

    n_main = jnp.right_shift(n_off, 1)
    fin = lax.fori_loop(0, n_main, body, (m_init, m_init, alpha_init, alpha_init) + tmax0)
    ms, alphas, tmax = fin[:2], fin[2:4], fin[4:]
    tau = 2 * n_main
    odd_tiles = tau == n_off

    @pl.when(odd_tiles)
    def _():
        _, alphas_a = stage_s(tau, sa_ref, pa_ref, ms, tmax)
        stage_v(tau - 1, pb_ref, alphas)
        stage_v(tau, pa_ref, alphas_a)
        finalize()

    @pl.when(jnp.logical_not(odd_tiles))
    def _():
        tmax_b = stage_q(tau + 1, sb_ref)
        ms_a, alphas_a = stage_s(tau, sa_ref, pa_ref, ms, tmax)
        stage_v(tau - 1, pb_ref, alphas)
        _, alphas_b = stage_s(tau + 1, sb_ref, pb_ref, ms_a, tmax_b)
        stage_v(tau, pa_ref, alphas_a)
        stage_v(tau + 1, pb_ref, alphas_b)
        finalize()


def _attn(kn, qT, k4, vT, lam, g, B, S):
    t = SEQ_TILE
    nt = S // t
    grid_spec = pltpu.PrefetchScalarGridSpec(
        num_scalar_prefetch=1,
        grid=(B, N_HEADS, nt),
        in_specs=[pl.BlockSpec((None, None, None, V_DIM, t), lambda b, h, i, kn: (b, h, i, 0, 0)),
                  pl.BlockSpec((None, nt, t, K_COLS), lambda b, h, i, kn: (b, 0, 0, h)),
                  pl.BlockSpec((None, None, nt, V_ROWS, t), lambda b, h, i, kn: (b, h, 0, 0, 0)),
                  pl.BlockSpec((4, HEAD_DIM), lambda b, h, i, kn: (0, 0)),
                  pl.BlockSpec((1, V_DIM), lambda b, h, i, kn: (0, 0))],
        out_specs=pl.BlockSpec((None, t, V_DIM), lambda b, h, i, kn: (b, i, h)),
        scratch_shapes=[pltpu.VMEM((t, t), F32), pltpu.VMEM((2, V_ROWS, t), F32),
                        pltpu.VMEM((2, t, t + LANES), F32), pltpu.VMEM((2, t, t + LANES), F32),
                        pltpu.VMEM((2, t, t + LANES), BF16), pltpu.VMEM((2, t, t + LANES), BF16)],
    )
    return pl.pallas_call(
        _attn_kernel,
        grid_spec=grid_spec,
        out_shape=jax.ShapeDtypeStruct((B, S, ATTN_WIDTH), BF16),
        compiler_params=pltpu.CompilerParams(dimension_semantics=("parallel", "arbitrary", "arbitrary"),
                                             vmem_limit_bytes=VMEM_LIMIT),
        name="diff_attn",
    )(kn, qT, k4, vT, lam, g)


def _lru_tile(x, y, ti, cw_ref, cb_ref, wr_ref, wi_ref, br_ref, bi_ref, lam_ref, xbuf, hc):
    T = SEQ_TILE
    xbuf[8:8 + T] = x
    cw = cw_ref[...]
    xc = cb_ref[...] + cw[3:4] * x
    for j in range(CONV_W - 1):
        xc = xc + cw[j:j + 1] * xbuf[5 + j:5 + j + T]
    xbuf[0:8] = x[T - 8:T]

    xb = xc.astype(BF16)
    sigmoid = lambda z: 0.5 * jnp.tanh(0.5 * z) + 0.5
    r = sigmoid(_dot(xb, wr_ref[...]) + br_ref[...])
    ig = sigmoid(_dot(xb, wi_ref[...]) + bi_ref[...])
    z = -lam_ref[...]
    softplus = jnp.maximum(z, 0.0) + jnp.log1p(jnp.exp(-jnp.abs(z)))
    la = -LRU_C * r * softplus
    a = jnp.exp(la)
    m2 = -jnp.tanh(la) * (a * a + 1.0)
    mult = jnp.where(m2 > 0.0, m2 * lax.rsqrt(m2), 0.0)
    row = lax.broadcasted_iota(jnp.int32, (T, LRU_WIDTH), 0)
    mult = jnp.where((row == 0) & (ti == 0), 1.0, mult)
    u = (xc * ig) * mult

    nb = T // SUBLANES
    a3 = a.reshape(nb, SUBLANES, LRU_WIDTH)
    u3 = u.reshape(nb, SUBLANES, LRU_WIDTH)
    sub = lax.broadcasted_iota(jnp.int32, a3.shape, 1)
    d = 1
    while d < SUBLANES:
        valid = sub >= d
        u3 = jnp.where(valid, a3 * pltpu.roll(u3, d, 1) + u3, u3)
        a3 = jnp.where(valid, a3 * pltpu.roll(a3, d, 1), a3)
        d *= 2
    h = hc[...]
    blocks = []
    for b in range(nb):
        hb = u3[b] + a3[b] * h
        blocks.append(hb)
        h = hb[SUBLANES - 1:SUBLANES]
    hfull = jnp.concatenate(blocks, axis=0)
    hc[...] = h
    gelu = 0.5 * y * (1.0 + jnp.tanh(0.7978845608028654 * (y + 0.044715 * (y * y * y))))
    return (hfull * gelu).astype(BF16)


def _pack_words(v):
    bits = pltpu.bitcast(v.astype(BF16).astype(F32), jnp.uint32)
    half = D_MODEL // 2
    packed = (bits[:, :half] >> 16) | (bits[:, half:] & jnp.uint32(0xFFFF0000))
    return [packed[:, c * LANES:(c + 1) * LANES] for c in range(PACK_SUB)]


def _packed_chunk(c, rows, first_row=0):
    return (pl.ds(first_row * PACK_SUB + c, rows, stride=PACK_SUB), slice(None))


def _pack_rows(v, out_ref, first_row=0):
    for c, words in enumerate(_pack_words(v)):
        out_ref[_packed_chunk(c, v.shape[0], first_row)] = words


def _unpack_chunks(in_ref, rows, first_row=0):
    lo, hi = [], []
    for c in range(PACK_SUB):
        w = in_ref[_packed_chunk(c, rows, first_row)]
        lo.append(pltpu.bitcast(w << 16, F32))
        hi.append(pltpu.bitcast(w & jnp.uint32(0xFFFF0000), F32))
    return lo + hi


def _unpack_rows(in_ref, rows):
    return jnp.concatenate(_unpack_chunks(in_ref, rows), axis=1).astype(BF16)


def _merge_kernel(x_ref, attn_ref, lru_ref, g1_ref, wg_ref, woa_ref, wol_ref, wout_ref, g2_ref, wrt_ref,
                  x1_ref, h2p_ref, route_ref, counts_ref, cnt, tri):
    tm = MERGE_CHAIN

    @pl.when(pl.program_id(0) == 0)
    def _():
        cnt[...] = jnp.zeros_like(cnt)
        r = lax.broadcasted_iota(jnp.int32, (tm, tm), 0)
        c = lax.broadcasted_iota(jnp.int32, (tm, tm), 1)
        tri[...] = jnp.where(r < c, 1.0, 0.0).astype(BF16)

    counts = cnt[...]
    for r0 in range(0, SEQ_TILE, tm):
        counts = _merge_chain(r0, tm, counts, x_ref, attn_ref, lru_ref, g1_ref, wg_ref, woa_ref, wol_ref, wout_ref,
                              g2_ref, wrt_ref, x1_ref, h2p_ref, route_ref, tri)
    cnt[...] = counts
    counts_ref[...] = counts


def _merge_chain(r0, tm, counts, x_ref, attn_ref, lru_ref, g1_ref, wg_ref, woa_ref, wol_ref, wout_ref, g2_ref, wrt_ref,
                 x1_ref, h2p_ref, route_ref, tri):
    rs = slice(r0, r0 + tm)
    x = x_ref[rs, :]
    hb = _rms(x, g1_ref[...]).astype(BF16)
    gates = 0.5 * jnp.tanh(0.5 * _dot(hb, wg_ref[:, PROJ_COLS:])) + 0.5
    merged = gates[:, :D_MODEL] * _dot(attn_ref[rs, :], woa_ref[...]) + gates[:, D_MODEL:] * _dot(lru_ref[rs, :], wol_ref[...])
    x1 = x + _dot(merged.astype(BF16), wout_ref[...])
    x1_ref[rs, :] = x1
    h2 = _rms(x1, g2_ref[...])
    _pack_rows(h2, h2p_ref, first_row=r0)

    h_hi = h2.astype(BF16)
    h_lo = (h2 - h_hi.astype(F32)).astype(BF16)
    wrt = wrt_ref[...]
    nt_dims = (((1,), (1,)), ((), ()))
    hh = lax.dot_general(wrt, h_hi, nt_dims, preferred_element_type=F32)
    lo_pass = lax.dot_general(wrt[:ROUTE_LANES], h_lo, nt_dims, preferred_element_type=F32)
    logits = (hh[:ROUTE_LANES] + hh[ROUTE_LANES:] + lo_pass)[:ROUTE_ROWS]
    row = lax.broadcasted_iota(jnp.int32, logits.shape, 0)
    big = jnp.int32(1 << 20)

    def first_argmax(v):
        m = jnp.max(v, axis=0, keepdims=True)
        return m, jnp.min(jnp.where(v == m, row, big), axis=0, keepdims=True)

    gmask = row < N_GROUPS
    gmax, gidx = first_argmax(jnp.where(gmask, logits, -jnp.inf))
    gsum = jnp.sum(jnp.where(gmask, jnp.exp(logits - gmax), 0.0), axis=0, keepdims=True)
    g_w = 1.0 / gsum
    lo = N_GROUPS + EXPERTS_PER_GROUP * gidx
    el = jnp.where((row >= lo) & (row < lo + EXPERTS_PER_GROUP), logits, -jnp.inf)
    m1, i1 = first_argmax(el)
    m2, i2 = first_argmax(jnp.where(row == i1, -jnp.inf, el))
    rr = jnp.exp(m2 - m1)
    w1 = g_w / (1.0 + rr)
    w2 = g_w * rr / (1.0 + rr)
    oh1 = row == i1
    oh2 = row == i2
    oh = jnp.where(oh1 | oh2, 1.0, 0.0)
    before = _dot(oh.astype(BF16), tri[...]) + counts
    r1 = jnp.sum(jnp.where(oh1, before, 0.0), axis=0, keepdims=True)
    r2 = jnp.sum(jnp.where(oh2, before, 0.0), axis=0, keepdims=True)
    vals = (i1.astype(F32), i2.astype(F32), w1, w2, r1, r2, jnp.zeros_like(w1), jnp.zeros_like(w1))
    for k, v in enumerate(vals):
        route_ref[k:k + 1, rs] = v
    return counts + jnp.sum(oh, axis=1, keepdims=True)


def _merge(x2, attn, lru, g1, wg, woa, wol, wout, g2, wrt):
    N = x2.shape[0]
    tm = SEQ_TILE
    rows = lambda c: pl.BlockSpec((tm, c), lambda i: (i, 0))
    full = lambda r, c: pl.BlockSpec((r, c), lambda i: (0, 0))
    return pl.pallas_call(
        _merge_kernel,
        grid=(N // tm,),
        in_specs=[rows(D_MODEL), rows(ATTN_WIDTH), rows(LRU_WIDTH), full(1, D_MODEL),
                  full(D_MODEL, PROJ_COLS + 2 * D_MODEL),
                  full(ATTN_WIDTH, D_MODEL), full(LRU_WIDTH, D_MODEL), full(D_MODEL, D_MODEL), full(1, D_MODEL),
                  full(2 * ROUTE_LANES, D_MODEL)],
        out_specs=[rows(D_MODEL), pl.BlockSpec((tm * PACK_SUB, LANES), lambda i: (i, 0)),
                   pl.BlockSpec((ROUTE_FIELDS, tm), lambda i: (0, i)), full(ROUTE_ROWS, 1)],
        out_shape=[jax.ShapeDtypeStruct((N, D_MODEL), F32), jax.ShapeDtypeStruct((N * PACK_SUB, LANES), jnp.uint32),
                   jax.ShapeDtypeStruct((ROUTE_FIELDS, N), F32), jax.ShapeDtypeStruct((ROUTE_ROWS, 1), F32)],
        scratch_shapes=[pltpu.VMEM((ROUTE_ROWS, 1), F32), pltpu.VMEM((MERGE_CHAIN, MERGE_CHAIN), BF16)],
        compiler_params=pltpu.CompilerParams(dimension_semantics=("arbitrary",), vmem_limit_bytes=VMEM_LIMIT),
        name="merge_route",
    )(x2, attn, lru, g1, wg, woa, wol, wout, g2, wrt)


def _for_each_assignment(off_ref, fn):
    def group(gi, _):
        toks = [gi * DMA_GROUP + j for j in range(DMA_GROUP)]
        offs = [[off_ref[0, 0, k * ROW_TILE + tk] for k in range(TOP_K)] for tk in toks]
        for tk, o in zip(toks, offs):
            for k in range(TOP_K):
                fn(tk, k, pl.multiple_of(o[k], PACK_SUB))
        return 0
    lax.fori_loop(0, ROW_TILE // DMA_GROUP, group, 0)


def _dispatch_kernel(off_ref, h2p_hbm, xs_hbm, xin, in_sem, out_sem):
    tile_rows = ROW_TILE * PACK_SUB
    i = pl.program_id(0)
    n = pl.num_programs(0)
    slot = lax.rem(i, DISPATCH_SLOTS)

    def fetch(tile, s):
        return pltpu.make_async_copy(h2p_hbm.at[pl.ds(pl.multiple_of(tile * tile_rows, tile_rows), tile_rows)],
                                     xin.at[s], in_sem.at[s])

    def drain(s):
        for _ in range(TOP_K):
            pltpu.make_async_copy(xin.at[s], xs_hbm.at[pl.ds(0, tile_rows)], out_sem.at[s]).wait()

    @pl.when(i == 0)
    def _():
        fetch(0, 0).start()

    @pl.when(i + 1 < n)
    def _():
        fetch(i + 1, lax.rem(i + 1, DISPATCH_SLOTS)).start()

    fetch(i, slot).wait()

    def start(tk, k, off):
        pltpu.make_async_copy(xin.at[slot, pl.ds(pl.multiple_of(tk * PACK_SUB, PACK_SUB), PACK_SUB)],
                              xs_hbm.at[pl.ds(off, PACK_SUB)], out_sem.at[slot]).start(priority=k)

    _for_each_assignment(off_ref, start)

    @pl.when(i > 0)
    def _():
        drain(lax.rem(i + DISPATCH_SLOTS - 1, DISPATCH_SLOTS))

    @pl.when(i == n - 1)
    def _():
        drain(slot)


def _dispatch(off3, h2p):
    return pl.pallas_call(
        _dispatch_kernel,
        grid=(off3.shape[0],),
        in_specs=[pl.BlockSpec((1, 1, TOP_K * ROW_TILE), lambda i: (i, 0, 0), memory_space=pltpu.SMEM),
                  pl.BlockSpec(memory_space=pl.ANY)],
        out_specs=pl.BlockSpec(memory_space=pl.ANY),
        out_shape=jax.ShapeDtypeStruct((TOP_K * h2p.shape[0], LANES), jnp.uint32),
        scratch_shapes=[pltpu.VMEM((DISPATCH_SLOTS, ROW_TILE * PACK_SUB, LANES), jnp.uint32),
                        pltpu.SemaphoreType.DMA((DISPATCH_SLOTS,)), pltpu.SemaphoreType.DMA((DISPATCH_SLOTS,))],
        compiler_params=pltpu.CompilerParams(dimension_semantics=("arbitrary",), has_side_effects=True),
        name="moe_dispatch",
    )(off3, h2p)


def _moe_kernel(vt_ref, ve_ref, vlo_ref, vhi_ref, vnext_ref, vpar_ref, xs_ref, wg_hbm, wu_hbm, wd_hbm, ys_ref,
                wgb, wub, wdb, wgf, wuf, wdf, sem):
    tm = MOE_TILE
    v = pl.program_id(0)
    t = vt_ref[v]
    e = ve_ref[v]
    slot = vpar_ref[v]
    prev = jnp.maximum(v - 1, 0)

    def fetch(expert, s):
        return [pltpu.make_async_copy(w_hbm.at[expert], buf.at[s], sem.at[s])
                for w_hbm, buf in ((wg_hbm, wgf), (wu_hbm, wuf), (wd_hbm, wdf))]

    @pl.when(v == 0)
    def _():
        for copy in fetch(e, slot):
            copy.start()

    @pl.when((v == 0) | (ve_ref[prev] != e))
    def _():
        for copy in fetch(e, slot):
            copy.wait()
        nxt = vnext_ref[v]

        @pl.when(nxt != e)
        def _():
            for copy in fetch(nxt, 1 - slot):
                copy.start()

        wgb[...] = wgf[slot].astype(BF16)
        wub[...] = wuf[slot].astype(BF16)
        wdb[...] = wdf[slot].astype(BF16)

    xb = _unpack_rows(xs_ref, tm)
    g = _dot(xb, wgb[...])
    u = _dot(xb, wub[...])
    hmid = (g * jax.nn.sigmoid(g)) * u
    words = _pack_words(_dot(hmid.astype(BF16), wdb[...]))
    first = (v == 0) | (vt_ref[prev] != t)

    @pl.when(first)
    def _():
        for c in range(PACK_SUB):
            ys_ref[_packed_chunk(c, tm)] = words[c]

    @pl.when(jnp.logical_not(first))
    def _():
        rows = t * tm + lax.broadcasted_iota(jnp.int32, (tm, LANES), 0)
        mine = (rows >= vlo_ref[v]) & (rows < vhi_ref[v])
        for c in range(PACK_SUB):
            ys_ref[_packed_chunk(c, tm)] = jnp.where(mine, words[c], ys_ref[_packed_chunk(c, tm)])


def _moe(plan, xs, wg, wu, wd):
    tm = MOE_TILE
    n_rows = xs.shape[0] // PACK_SUB
    tile = lambda v, vt, *_: (vt[v], 0)
    hbm = pl.BlockSpec(memory_space=pl.ANY)
    grid_spec = pltpu.PrefetchScalarGridSpec(
        num_scalar_prefetch=len(plan),
        grid=(plan[0].shape[0],),
        in_specs=[pl.BlockSpec((tm * PACK_SUB, LANES), tile), hbm, hbm, hbm],
        out_specs=pl.BlockSpec((tm * PACK_SUB, LANES), tile),
        scratch_shapes=[pltpu.VMEM((D_MODEL, D_EXPERT), BF16), pltpu.VMEM((D_MODEL, D_EXPERT), BF16),
                        pltpu.VMEM((D_EXPERT, D_MODEL), BF16),
                        pltpu.VMEM((2, D_MODEL, D_EXPERT), F32), pltpu.VMEM((2, D_MODEL, D_EXPERT), F32),
                        pltpu.VMEM((2, D_EXPERT, D_MODEL), F32), pltpu.SemaphoreType.DMA((2,))],
    )
    return pl.pallas_call(
        _moe_kernel,
        grid_spec=grid_spec,
        out_shape=jax.ShapeDtypeStruct((n_rows * PACK_SUB, LANES), jnp.uint32),
        compiler_params=pltpu.CompilerParams(dimension_semantics=("arbitrary",), vmem_limit_bytes=VMEM_LIMIT),
        name="moe_experts",
    )(*plan, xs, wg, wu, wd)


def _combine_kernel(off_ref, offn_ref, x1_ref, w_ref, g_ref, ys_hbm, o_ref, ybuf, sem):
    tc = ROW_TILE
    i = pl.program_id(0)
    n = pl.num_programs(0)
    slot = lax.rem(i, 2)

    def gather(o_ref_, s):
        def start(tk, k, off):
            pltpu.make_async_copy(ys_hbm.at[pl.ds(off, PACK_SUB)],
                                  ybuf.at[s, pl.ds(pl.multiple_of((k * tc + tk) * PACK_SUB, PACK_SUB), PACK_SUB)],
                                  sem.at[s]).start(priority=k)
        _for_each_assignment(o_ref_, start)

    def drain(s):
        pltpu.make_async_copy(ys_hbm.at[pl.ds(0, TOP_K * tc * PACK_SUB)], ybuf.at[s], sem.at[s]).wait()

    @pl.when(i == 0)
    def _():
        gather(off_ref, 0)

    @pl.when(i + 1 < n)
    def _():
        gather(offn_ref, 1 - slot)

    drain(slot)

    n_chunks = D_MODEL // LANES
    z = [x1_ref[:, c * LANES:(c + 1) * LANES] for c in range(n_chunks)]
    for k in range(TOP_K):
        wk = w_ref[:, k:k + 1]
        yk = _unpack_chunks(ybuf.at[slot], tc, first_row=k * tc)
        z = [zc + wk * yc for zc, yc in zip(z, yk)]
    ss = sum(jnp.sum(zc * zc, axis=-1, keepdims=True) for zc in z)
    inv = lax.rsqrt(ss * (1.0 / D_MODEL) + NORM_EPS)
    for c in range(n_chunks):
        o_ref[:, c * LANES:(c + 1) * LANES] = z[c] * inv * g_ref[:, c * LANES:(c + 1) * LANES]


def _combine(off3, x1, weight, g, ys):
    N = x1.shape[0]
    tc = ROW_TILE
    nt = N // tc
    idx = lambda f: pl.BlockSpec((1, 1, TOP_K * tc), f, memory_space=pltpu.SMEM)
    return pl.pallas_call(
        _combine_kernel,
        grid=(nt,),
        in_specs=[idx(lambda i: (i, 0, 0)),
                  idx(lambda i: (jnp.minimum(i + 1, nt - 1), 0, 0)),
                  pl.BlockSpec((tc, D_MODEL), lambda i: (i, 0)),
                  pl.BlockSpec((tc, TOP_K), lambda i: (i, 0)),
                  pl.BlockSpec((1, D_MODEL), lambda i: (0, 0)),
                  pl.BlockSpec(memory_space=pl.ANY)],
        out_specs=pl.BlockSpec((tc, D_MODEL), lambda i: (i, 0)),
        out_shape=jax.ShapeDtypeStruct((N, D_MODEL), F32),
        scratch_shapes=[pltpu.VMEM((2, TOP_K * tc * PACK_SUB, LANES), jnp.uint32), pltpu.SemaphoreType.DMA((2,))],
        compiler_params=pltpu.CompilerParams(dimension_semantics=("arbitrary",), vmem_limit_bytes=VMEM_LIMIT),
        name="combine_norm",
    )(off3, off3, x1, weight, g, ys)


def _block_diag(w):
    nb, c, _ = w.shape
    eye = jnp.eye(nb, dtype=w.dtype)
    return (eye[:, None, :, None] * w[:, :, None, :]).reshape(nb * c, nb * c)


def _visit_plan(counts, n_rows):
    tm = MOE_TILE
    n_tiles = n_rows // tm
    n_visits = n_tiles + N_EXPERTS - 1
    cnt = counts.astype(jnp.int32)
    lanes = jnp.arange(ROUTE_ROWS, dtype=jnp.int32)
    upto = (lanes[None, :] <= lanes[:, None]).astype(jnp.int32)
    ends = upto @ cnt
    starts = ends - cnt
    first_tile = starts // tm
    n_vis = jnp.where(cnt > 0, (ends - 1) // tm - first_tile + 1, 0)
    v_end = upto @ n_vis
    v_start = v_end - n_vis
    total = v_end[-1]
    v = jnp.arange(n_visits, dtype=jnp.int32)
    vc = jnp.minimum(v, total - 1)
    own = ((vc[:, None] >= v_start[None, :]) & (vc[:, None] < v_end[None, :])).astype(jnp.int32)
    pick = lambda per_lane: own @ per_lane
    tile = jnp.maximum(pick(first_tile - v_start) + vc, 0)
    lo = jnp.maximum(pick(starts), tile * tm)
    hi = jnp.minimum(pick(ends), (tile + 1) * tm)
    valid = v < total
    to_expert = lambda lane: jnp.maximum(lane - EXPERT_LANE0, 0)
    used = cnt > 0
    later = jnp.where((lanes[None, :] > lanes[:, None]) & used[None, :], lanes[None, :], ROUTE_ROWS)
    nxt = jnp.min(later, axis=1)
    nxt = jnp.where(nxt == ROUTE_ROWS, lanes, nxt)
    parity = (upto @ used.astype(jnp.int32) - 1) % 2
    plan = (tile, to_expert(pick(lanes)), jnp.where(valid, lo, 0), jnp.where(valid, hi, 0),
            to_expert(pick(nxt)), pick(parity))
    return starts, plan


def kernel(x, norm_mix_g, w_in, lambda_qk, subln_g, conv_w, conv_b, w_r, b_r, w_i, b_i, lru_lambda, w_o_attn, w_o_lru, w_out, norm_ffn_g, w_group, w_expert_router, w_gate, w_up, w_down, final_norm_g):
    B, S, D = x.shape
    N = B * S
    nt = S // SEQ_TILE
    depth = norm_mix_g.shape[0]
    assert depth == 1 and D == D_MODEL and S % SEQ_TILE == 0
    assert N % ROW_TILE == 0 and (N * TOP_K) % MOE_TILE == 0
    l = 0
    row = lambda v: v.reshape(1, -1).astype(F32)

    x2 = x.reshape(N, D)
    w_in_l = w_in[l].astype(BF16)
    lru_params = (conv_w[l].astype(F32), row(conv_b[l]), _block_diag(w_r[l]).astype(BF16),
                  _block_diag(w_i[l]).astype(BF16), row(b_r[l]), row(b_i[l]), row(lru_lambda[l]))
    qT, k, vT, lru, kn = _inproj(x2, row(norm_mix_g[l]), w_in_l, lru_params, B, S)

    kn_max = jnp.max(kn[:, 0, :2 * N_HEADS].reshape(B, nt, 2 * N_HEADS), axis=1)
    attn = _attn(kn_max.reshape(-1), qT, k.reshape(B, nt, SEQ_TILE, N_HEADS * K_COLS), vT,
                 lambda_qk[l].reshape(4, HEAD_DIM).astype(F32), row(subln_g[l]), B, S)

    w_route = jnp.concatenate(
        [w_group[l], jnp.transpose(w_expert_router[l], (1, 0, 2)).reshape(D, N_EXPERTS),
         jnp.zeros((D, ROUTE_LANES - N_GROUPS - N_EXPERTS), F32)], axis=1).astype(F32)
    w_route_hi = w_route.astype(BF16)
    w_route = jnp.concatenate([w_route_hi, (w_route - w_route_hi.astype(F32)).astype(BF16)], axis=1).T
    x1, h2p, route, counts = _merge(x2, attn.reshape(N, ATTN_WIDTH), lru.reshape(N, LRU_WIDTH), row(norm_mix_g[l]),
                                    w_in_l, w_o_attn[l].astype(BF16),
                                    w_o_lru[l].astype(BF16), w_out[l].astype(BF16), row(norm_ffn_g[l]), w_route)

    starts, plan = _visit_plan(counts[:, 0], N * TOP_K)
    lane = route[0:TOP_K].astype(jnp.int32)
    weight = jnp.transpose(route[TOP_K:2 * TOP_K])
    rank = route[2 * TOP_K:3 * TOP_K].astype(jnp.int32)
    lane_ids = jnp.arange(ROUTE_ROWS, dtype=jnp.int32)
    start_of = jnp.sum(jnp.where(lane[..., None] == lane_ids, starts, 0), axis=-1)
    off = ((start_of + rank) * PACK_SUB).reshape(TOP_K, N // ROW_TILE, ROW_TILE)
    off3 = jnp.transpose(off, (1, 0, 2)).reshape(N // ROW_TILE, 1, TOP_K * ROW_TILE)

    xs = _dispatch(off3, h2p)
    ys = _moe(plan, xs, w_gate[l], w_up[l], w_down[l])
    out = _combine(off3, x1, weight, row(final_norm_g), ys)
    return out.reshape(B, S, D)
```
